```python
import jax, jax.numpy as jnp
from jax import lax
import numpy as np

D_MODEL = 1024
BATCH = 4
SEQ = 4096
DEPTH = 1

MIX_WIDTH = D_MODEL
CONV_WIDTH = MIX_WIDTH // 2
CONV_GROUPS = 8
CONV_K = 3
MLSTM_HEADS = 4
MLSTM_V_DIM = (MIX_WIDTH - CONV_WIDTH) // MLSTM_HEADS
MLSTM_QK_DIM = MLSTM_V_DIM // 2
CHUNK = 128
D_FF = 2816
FFN_RESIDUAL_SCALE = 0.5
EPS = 1e-6
NEG_INF = -1e30
IN_SIZES = (CONV_WIDTH, CONV_WIDTH, CONV_WIDTH,
            MLSTM_HEADS * MLSTM_QK_DIM, MLSTM_HEADS * MLSTM_QK_DIM,
            MLSTM_HEADS * MLSTM_V_DIM, MLSTM_HEADS * MLSTM_V_DIM,
            2 * MLSTM_HEADS, 2 * MLSTM_HEADS)
W_IN_COLS = sum(IN_SIZES)

kernel_name = "hybrid_conv_mlstm_macaron_sandwich_encoder"


def rmsnorm(x, g):
    x32 = x.astype(jnp.float32)
    y = x32 * lax.rsqrt(jnp.mean(x32 * x32, axis=-1, keepdims=True) + EPS)
    return (y * g.astype(jnp.float32)).astype(x.dtype)


def swiglu(x, w_in, w_out):
    gate, up = jnp.split(x @ w_in, 2, axis=-1)
    return (jax.nn.silu(gate) * up) @ w_out


def mlstm_chunkwise(q, k, v, li, lf):
    B, H, S, DK = q.shape
    DV = v.shape[-1]
    NC = S // CHUNK
    q = q.reshape(B, H, NC, CHUNK, DK)
    k = k.reshape(B, H, NC, CHUNK, DK)
    v = v.reshape(B, H, NC, CHUNK, DV)
    li = li.reshape(B, H, NC, CHUNK)
    lf = lf.reshape(B, H, NC, CHUNK)
    b = jnp.cumsum(lf, axis=-1)
    g = b[..., -1]
    causal = jnp.tril(jnp.ones((CHUNK, CHUNK), dtype=bool))
    D = jnp.where(causal, b[..., :, None] - b[..., None, :] + li[..., None, :], NEG_INF)
    m_intra = jnp.max(D, axis=-1)
    w = g[..., None] - b + li
    m_chunk = jnp.max(w, axis=-1)
    e = jnp.exp(w - m_chunk[..., None])
    C_chunk = jnp.einsum('bhcl,bhclk,bhclv->bhckv', e, k, v)
    n_chunk = jnp.einsum('bhcl,bhclk->bhck', e, k)

    def step(carry, inp):
        C, n, m = carry
        g_c, Cc, nc, mc = inp
        m_new = jnp.maximum(g_c + m, mc)
        a = jnp.exp(g_c + m - m_new)
        s = jnp.exp(mc - m_new)
        C_new = a[..., None, None] * C + s[..., None, None] * Cc
        n_new = a[..., None] * n + s[..., None] * nc
        return (C_new, n_new, m_new), (C, n, m)

    init = (jnp.zeros((B, H, DK, DV), jnp.float32),
            jnp.zeros((B, H, DK), jnp.float32),
            jnp.full((B, H), NEG_INF, jnp.float32))
    xs = (jnp.moveaxis(g, 2, 0), jnp.moveaxis(C_chunk, 2, 0),
          jnp.moveaxis(n_chunk, 2, 0), jnp.moveaxis(m_chunk, 2, 0))
    _, (C_prev, n_prev, m_prev) = lax.scan(step, init, xs)
    C_prev = jnp.moveaxis(C_prev, 0, 2)
    n_prev = jnp.moveaxis(n_prev, 0, 2)
    m_prev = jnp.moveaxis(m_prev, 0, 2)

    m_inter = b + m_prev[..., None]
    m_t = jnp.maximum(m_inter, m_intra)
    P = jnp.exp(D - m_t[..., None]) * jnp.einsum('bhctk,bhcsk->bhcts', q, k)
    inter = jnp.exp(m_inter - m_t)
    numer = inter[..., None] * jnp.einsum('bhctk,bhckv->bhctv', q, C_prev) \
        + jnp.einsum('bhcts,bhcsv->bhctv', P, v)
    denom = inter * jnp.einsum('bhctk,bhck->bhct', q, n_prev) + jnp.sum(P, axis=-1)
    h = numer / jnp.maximum(jnp.abs(denom), jnp.exp(-m_t))[..., None]
    return h.reshape(B, H, S, DV)


def mixer(xn, w_in, conv_w, conv_b, gate_i_bias, gate_f_bias, head_gain, w_out):
    Bsz, S, _ = xn.shape
    H, DK, DV = MLSTM_HEADS, MLSTM_QK_DIM, MLSTM_V_DIM
    split_points = [int(p) for p in np.cumsum(IN_SIZES)[:-1]]
    bg, cg, hc, q, k, v, o, ig, fg = jnp.split(xn @ w_in, split_points, axis=-1)

    u = cg * hc
    u = lax.conv_general_dilated(u, conv_w[:, None, :].astype(u.dtype), window_strides=(1,),
                                 padding=[(1, 1)], dimension_numbers=('NWC', 'WIO', 'NWC'),
                                 feature_group_count=CONV_WIDTH)
    y_conv = bg * (u + conv_b.astype(u.dtype))

    def heads(t, d):
        return t.reshape(Bsz, S, H, d).transpose(0, 2, 1, 3).astype(jnp.float32)
    qh = heads(q, DK) * (DK ** -0.5)
    kh = heads(k, DK)
    vh = heads(v, DV)
    li = (ig.astype(jnp.float32) + gate_i_bias.astype(jnp.float32)).reshape(Bsz, S, 2, H).transpose(2, 0, 3, 1)
    lf = jax.nn.log_sigmoid(fg.astype(jnp.float32) + gate_f_bias.astype(jnp.float32)).reshape(Bsz, S, 2, H).transpose(2, 0, 3, 1)
    h_fwd = mlstm_chunkwise(qh, kh, vh, li[0], lf[0])
    flip = lambda t: jnp.flip(t, axis=2)
    h_bwd = flip(mlstm_chunkwise(flip(qh), flip(kh), flip(vh), flip(li[1]), flip(lf[1])))
    h = h_fwd + h_bwd
    h = h * lax.rsqrt(jnp.mean(h * h, axis=-1, keepdims=True) + EPS) \
        * head_gain.astype(jnp.float32).reshape(H, 1, DV)
    h = h.transpose(0, 2, 1, 3).reshape(Bsz, S, H * DV).astype(xn.dtype)
    y_mlstm = jax.nn.sigmoid(o) * h

    return jnp.concatenate([y_conv, y_mlstm], axis=-1) @ w_out


def setup_inputs(seed: int = 0) -> dict:
    key = jax.random.key(seed)
    ks = jax.random.split(key, 20)
    H = MLSTM_HEADS

    def nrm(k, shape, scale):
        return jax.random.normal(k, shape, jnp.float32) * scale

    def gain(k, n):
        return 1.0 + 0.05 * jax.random.normal(k, (DEPTH, n), jnp.float32)

    f_bias = jnp.tile(jnp.linspace(3.0, 6.0, H, dtype=jnp.float32), 2)[None, :] + nrm(ks[12], (DEPTH, 2 * H), 0.1)
    return {
        'x': jax.random.normal(ks[0], (BATCH, SEQ, D_MODEL), jnp.float32),
        'norm_ffn1_pre': gain(ks[1], D_MODEL),
        'norm_ffn1_post': gain(ks[2], D_MODEL),
        'w_ffn1_in': nrm(ks[3], (DEPTH, D_MODEL, 2 * D_FF), D_MODEL ** -0.5),
        'w_ffn1_out': nrm(ks[4], (DEPTH, D_FF, D_MODEL), D_FF ** -0.5),
        'norm_mix_pre': gain(ks[5], D_MODEL),
        'norm_mix_post': gain(ks[6], D_MODEL),
        'w_mix_in': nrm(ks[7], (DEPTH, D_MODEL, W_IN_COLS), D_MODEL ** -0.5),
        'conv_w': nrm(ks[8], (DEPTH, CONV_K, CONV_WIDTH), CONV_K ** -0.5),
        'conv_b': nrm(ks[9], (DEPTH, CONV_WIDTH), 0.02),
        'gate_i_bias': nrm(ks[10], (DEPTH, 2 * H), 0.1),
        'gate_f_bias': f_bias,
        'mlstm_norm': gain(ks[11], H * MLSTM_V_DIM),
        'w_mix_out': nrm(ks[13], (DEPTH, MIX_WIDTH, D_MODEL), MIX_WIDTH ** -0.5),
        'norm_ffn2_pre': gain(ks[14], D_MODEL),
        'norm_ffn2_post': gain(ks[15], D_MODEL),
        'w_ffn2_in': nrm(ks[16], (DEPTH, D_MODEL, 2 * D_FF), D_MODEL ** -0.5),
        'w_ffn2_out': nrm(ks[17], (DEPTH, D_FF, D_MODEL), D_FF ** -0.5),
    }


def reference(x, norm_ffn1_pre, norm_ffn1_post, w_ffn1_in, w_ffn1_out,
              norm_mix_pre, norm_mix_post, w_mix_in, conv_w, conv_b,
              gate_i_bias, gate_f_bias, mlstm_norm, w_mix_out,
              norm_ffn2_pre, norm_ffn2_post, w_ffn2_in, w_ffn2_out):
    for l in range(DEPTH):
        h = swiglu(rmsnorm(x, norm_ffn1_pre[l]), w_ffn1_in[l], w_ffn1_out[l])
        x = x + FFN_RESIDUAL_SCALE * rmsnorm(h, norm_ffn1_post[l])
        h = mixer(rmsnorm(x, norm_mix_pre[l]), w_mix_in[l], conv_w[l], conv_b[l],
                  gate_i_bias[l], gate_f_bias[l], mlstm_norm[l], w_mix_out[l])
        x = x + rmsnorm(h, norm_mix_post[l])
        h = swiglu(rmsnorm(x, norm_ffn2_pre[l]), w_ffn2_in[l], w_ffn2_out[l])
        x = x + FFN_RESIDUAL_SCALE * rmsnorm(h, norm_ffn2_post[l])
    return x
```

```python
import functools

import jax
import jax.numpy as jnp
from jax import lax
from jax.experimental import pallas as pl
from jax.experimental.pallas import tpu as pltpu

D_MODEL = 1024
D_FF = 2816
CONV_WIDTH = 512
HEADS = 4
DK = 64
DV = 128
CHUNK = 128
EPS = 1e-6
NEG_INF = -1e30

FF_TILE = 256
VMEM_LIMIT = 56 * 1024 * 1024

F32 = jnp.float32
BF16 = jnp.bfloat16


def _rms(x, g):
    return x * lax.rsqrt(jnp.mean(x * x, axis=-1, keepdims=True) + EPS) * g


def _resident(shape):
    zeros = (0,) * len(shape)
    return pl.BlockSpec(shape, lambda *_: zeros, pipeline_mode=pl.Buffered(1))


def _ffn_kernel(x_ref, gpre_ref, gpost_ref, win_ref, wout_ref, o_ref, h_ref):
    x = x_ref[...]
    xn = _rms(x, gpre_ref[...]).astype(BF16)
    for j in range(D_FF // FF_TILE):
        lo = j * FF_TILE
        gate = jnp.dot(xn, win_ref[:, lo:lo + FF_TILE], preferred_element_type=F32)
        up = jnp.dot(xn, win_ref[:, D_FF + lo:D_FF + lo + FF_TILE], preferred_element_type=F32)
        h_ref[:, lo:lo + FF_TILE] = (gate * jax.nn.sigmoid(gate) * up).astype(BF16)
    y = jnp.dot(h_ref[...], wout_ref[...], preferred_element_type=F32)
    o_ref[...] = x + 0.5 * _rms(y, gpost_ref[...])


def _ffn(x, gpre, gpost, w_in, w_out, tm):
    T = x.shape[0]
    return pl.pallas_call(
        _ffn_kernel,
        grid=(T // tm,),
        in_specs=[
            pl.BlockSpec((tm, D_MODEL), lambda i: (i, 0)),
            _resident((1, D_MODEL)),
            _resident((1, D_MODEL)),
            _resident((D_MODEL, 2 * D_FF)),
            _resident((D_FF, D_MODEL)),
        ],
        out_specs=pl.BlockSpec((tm, D_MODEL), lambda i: (i, 0)),
        out_shape=jax.ShapeDtypeStruct((T, D_MODEL), F32),
        scratch_shapes=[pltpu.VMEM((tm, D_FF), BF16)],
        compiler_params=pltpu.CompilerParams(
            dimension_semantics=("arbitrary",), vmem_limit_bytes=VMEM_LIMIT),
        name="ffn",
    )(x, gpre, gpost, w_in, w_out)


def _mix_in_kernel(x_ref, gpre_ref, wa_ref, wkt_ref, wb_ref, wg_ref, gbias_ref,
                   bg_ref, u_ref, q_ref, kt_ref, v_ref, o_ref, gates_ref):
    xn = _rms(x_ref[...], gpre_ref[...]).astype(BF16)
    W = CONV_WIDTH
    bg_ref[...] = jnp.dot(xn, wa_ref[:, 0:W], preferred_element_type=F32)
    cg = jnp.dot(xn, wa_ref[:, W:2 * W], preferred_element_type=F32)
    hc = jnp.dot(xn, wa_ref[:, 2 * W:3 * W], preferred_element_type=F32)
    u_ref[...] = cg * hc
    q = jnp.dot(xn, wa_ref[:, 3 * W:3 * W + HEADS * DK], preferred_element_type=F32)
    q_ref[...] = (q * (DK ** -0.5)).astype(BF16)
    kt = lax.dot_general(wkt_ref[...], xn, (((1,), (1,)), ((), ())), preferred_element_type=F32)
    kt_ref[...] = kt.astype(BF16)
    v_ref[...] = jnp.dot(xn, wb_ref[:, 0:HEADS * DV], preferred_element_type=F32).astype(BF16)
    o_ref[...] = jnp.dot(xn, wb_ref[:, HEADS * DV:2 * HEADS * DV], preferred_element_type=F32)
    z = jnp.dot(xn, wg_ref[...], preferred_element_type=F32) + gbias_ref[...]
    lane = lax.broadcasted_iota(jnp.int32, z.shape, 1)
    logsig = jnp.minimum(z, 0.0) - jnp.log1p(jnp.exp(-jnp.abs(z)))
    gates_ref[...] = jnp.where(lane < 2 * HEADS, z, jnp.where(lane < 4 * HEADS, logsig, 0.0))


def _mix_in(x, gpre, wa, wkt, wb, wg, gbias, tm):
    T = x.shape[0]
    row = lambda w: pl.BlockSpec((tm, w), lambda i: (i, 0))
    return pl.pallas_call(
        _mix_in_kernel,
        grid=(T // tm,),
        in_specs=[
            row(D_MODEL),
            _resident((1, D_MODEL)),
            _resident(wa.shape),
            _resident(wkt.shape),
            _resident(wb.shape),
            _resident(wg.shape),
            _resident((1, 128)),
        ],
        out_specs=[
            row(CONV_WIDTH), row(CONV_WIDTH), row(HEADS * DK),
            pl.BlockSpec((HEADS * DK, tm), lambda i: (0, i)),
            row(HEADS * DV), row(HEADS * DV), row(128),
        ],
        out_shape=[
            jax.ShapeDtypeStruct((T, CONV_WIDTH), F32),
            jax.ShapeDtypeStruct((T, CONV_WIDTH), F32),
            jax.ShapeDtypeStruct((T, HEADS * DK), BF16),
            jax.ShapeDtypeStruct((HEADS * DK, T), BF16),
            jax.ShapeDtypeStruct((T, HEADS * DV), BF16),
            jax.ShapeDtypeStruct((T, HEADS * DV), F32),
            jax.ShapeDtypeStruct((T, 128), F32),
        ],
        compiler_params=pltpu.CompilerParams(
            dimension_semantics=("arbitrary",), vmem_limit_bytes=VMEM_LIMIT),
        name="mix_in",
    )(x, gpre, wa, wkt, wb, wg, gbias)


def _split3(x):
    hi = x.astype(BF16)
    r1 = x - hi.astype(F32)
    mid = r1.astype(BF16)
    lo = (r1 - mid.astype(F32)).astype(BF16)
    return hi, mid, lo


def _mlstm_chunk(direction, q_ref, kt_ref, v_ref, gates_ref, c_ref, m_ref):
    L = CHUNK
    rev = direction == 1
    ri = lax.broadcasted_iota(jnp.int32, (L, L), 0)
    ci = lax.broadcasted_iota(jnp.int32, (L, L), 1)
    keep = (ci >= ri) if rev else (ci <= ri)
    tri = keep.astype(BF16)

    gates = gates_ref[...]
    hi, mid, lo = _split3(gates)
    cum3 = jnp.dot(tri, jnp.concatenate([hi, mid, lo], axis=1), preferred_element_type=F32)
    cum = cum3[:, 0:128] + cum3[:, 128:256] + cum3[:, 256:384]
    lane = lax.broadcasted_iota(jnp.int32, (L, 128), 1)
    comb = jnp.where(lane < 2 * HEADS, gates, cum)
    comb_t = comb.T

    ones_col = (lax.broadcasted_iota(jnp.int32, (L, DV), 1) == 0).astype(BF16)
    outs = []
    for h in range(HEADS):
        gi = direction * HEADS + h
        li_row = comb_t[gi:gi + 1, :]
        b_row = comb_t[2 * HEADS + gi:2 * HEADS + gi + 1, :]
        b_col = comb[:, 2 * HEADS + gi:2 * HEADS + gi + 1]
        g = b_row[:, 0:1] if rev else b_row[:, L - 1:L]
        m_prev = m_ref[h]
        c_prev = c_ref[h]

        q = q_ref[:, h * DK:(h + 1) * DK]
        kt = kt_ref[h * DK:(h + 1) * DK, :]
        v_aug = jnp.concatenate([v_ref[:, h * DV:(h + 1) * DV], ones_col], axis=1)

        d = jnp.where(keep, b_col + (li_row - b_row), NEG_INF)
        m_intra = jnp.max(d, axis=-1, keepdims=True)
        m_inter = b_col + m_prev
        m_t = jnp.maximum(m_inter, m_intra)
        s = jnp.dot(q, kt, preferred_element_type=F32)
        p = (jnp.exp(d - m_t) * s).astype(BF16)
        inter = jnp.exp(m_inter - m_t)
        acc = inter * jnp.dot(q, c_prev.astype(BF16), preferred_element_type=F32) \
            + jnp.dot(p, v_aug, preferred_element_type=F32)
        numer = acc[:, 0:DV]
        denom = acc[:, DV:DV + 1]
        outs.append(numer / jnp.maximum(jnp.abs(denom), jnp.exp(-m_t)))

        w = g - b_row + li_row
        m_chunk = jnp.max(w, axis=-1, keepdims=True)
        e = jnp.exp(w - m_chunk)
        c_chunk = jnp.dot((kt.astype(F32) * e).astype(BF16), v_aug, preferred_element_type=F32)
        m_new = jnp.maximum(g + m_prev, m_chunk)
        c_ref[h] = jnp.exp(g + m_prev - m_new) * c_prev + jnp.exp(m_chunk - m_new) * c_chunk
        m_ref[h] = m_new
    return outs


def _mlstm_init(c_ref, m_ref):
    c_ref[...] = jnp.zeros(c_ref.shape, F32)
    m_ref[...] = jnp.full(m_ref.shape, NEG_INF, F32)


def _mlstm_fwd_kernel(q_ref, kt_ref, v_ref, gates_ref, hf_ref, c_ref, m_ref):
    @pl.when(pl.program_id(1) == 0)
    def _():
        _mlstm_init(c_ref, m_ref)

    outs = _mlstm_chunk(0, q_ref, kt_ref, v_ref, gates_ref, c_ref, m_ref)
    for h in range(HEADS):
        hf_ref[:, h * DV:(h + 1) * DV] = outs[h]


def _mlstm_bwd_kernel(q_ref, kt_ref, v_ref, gates_ref, hf_ref, o_ref, gain_ref, y_ref, c_ref, m_ref):
    @pl.when(pl.program_id(1) == 0)
    def _():
        _mlstm_init(c_ref, m_ref)

    outs = _mlstm_chunk(1, q_ref, kt_ref, v_ref, gates_ref, c_ref, m_ref)
    for h in range(HEADS):
        sl = slice(h * DV, (h + 1) * DV)
        hsum = hf_ref[:, sl] + outs[h]
        hn = _rms(hsum, gain_ref[:, sl])
        y_ref[:, sl] = (jax.nn.sigmoid(o_ref[:, sl]) * hn).astype(BF16)


def _mlstm(q, kt, v, gates, o, gain, batch, seq):
    T = batch * seq
    nc = seq // CHUNK
    scratch = [pltpu.VMEM((HEADS, DK, 2 * DV), F32), pltpu.VMEM((HEADS, 1, 1), F32)]
    params = pltpu.CompilerParams(dimension_semantics=("arbitrary", "arbitrary"),
                                  vmem_limit_bytes=VMEM_LIMIT)

    def specs(chunk_of):
        row = lambda w: pl.BlockSpec((CHUNK, w), lambda b, c: (b * nc + chunk_of(c), 0))
        col = pl.BlockSpec((HEADS * DK, CHUNK), lambda b, c: (0, b * nc + chunk_of(c)))
        return row, col

    row, col = specs(lambda c: c)
    hf = pl.pallas_call(
        _mlstm_fwd_kernel,
        grid=(batch, nc),
        in_specs=[row(HEADS * DK), col, row(HEADS * DV), row(128)],
        out_specs=row(HEADS * DV),
        out_shape=jax.ShapeDtypeStruct((T, HEADS * DV), F32),
        scratch_shapes=scratch,
        compiler_params=params,
        name="mlstm_fwd",
    )(q, kt, v, gates)

    row, col = specs(lambda c: nc - 1 - c)
    return pl.pallas_call(
        _mlstm_bwd_kernel,
        grid=(batch, nc),
        in_specs=[row(HEADS * DK), col, row(HEADS * DV), row(128),
                  row(HEADS * DV), row(HEADS * DV), _resident((1, HEADS * DV))],
        out_specs=row(HEADS * DV),
        out_shape=jax.ShapeDtypeStruct((T, HEADS * DV), BF16),
        scratch_shapes=scratch,
        compiler_params=params,
        name="mlstm_bwd",
    )(q, kt, v, gates, hf, o, gain)


def _mix_out_kernel(tiles_per_seq, x_ref, bg_ref, u_ref, uprev_ref, unext_ref, cw_ref, cb_ref,
                    y_ref, wout_ref, gpost_ref, o_ref):
    i = pl.program_id(0)
    tm = u_ref.shape[0]
    u = u_ref[...]
    has_prev = (i % tiles_per_seq != 0).astype(F32)
    has_next = (i % tiles_per_seq != tiles_per_seq - 1).astype(F32)
    prev_row = uprev_ref[7:8, :] * has_prev
    next_row = unext_ref[0:1, :] * has_next
    ri = lax.broadcasted_iota(jnp.int32, u.shape, 0)
    u_m1 = jnp.where(ri == 0, prev_row, pltpu.roll(u, 1, 0))
    u_p1 = jnp.where(ri == tm - 1, next_row, pltpu.roll(u, tm - 1, 0))
    conv = cw_ref[0:1, :] * u_m1 + cw_ref[1:2, :] * u + cw_ref[2:3, :] * u_p1
    y_conv = (bg_ref[...] * (conv + cb_ref[...])).astype(BF16)
    h = jnp.dot(y_conv, wout_ref[0:CONV_WIDTH, :], preferred_element_type=F32) \
        + jnp.dot(y_ref[...], wout_ref[CONV_WIDTH:, :], preferred_element_type=F32)
    o_ref[...] = x_ref[...] + _rms(h, gpost_ref[...])


def _mix_out(x, bg, u, conv_w, conv_b, y_mlstm, w_out, gpost, seq, tm):
    T = x.shape[0]
    tiles_per_seq = seq // tm
    sub = tm // 8
    last = T // 8 - 1
    row = lambda w: pl.BlockSpec((tm, w), lambda i: (i, 0))
    return pl.pallas_call(
        functools.partial(_mix_out_kernel, tiles_per_seq),
        grid=(T // tm,),
        in_specs=[
            row(D_MODEL), row(CONV_WIDTH), row(CONV_WIDTH),
            pl.BlockSpec((8, CONV_WIDTH), lambda i: (jnp.maximum(i * sub - 1, 0), 0)),
            pl.BlockSpec((8, CONV_WIDTH), lambda i: (jnp.minimum((i + 1) * sub, last), 0)),
            _resident((3, CONV_WIDTH)),
            _resident((1, CONV_WIDTH)),
            row(HEADS * DV),
            _resident((D_MODEL, D_MODEL)),
            _resident((1, D_MODEL)),
        ],
        out_specs=row(D_MODEL),
        out_shape=jax.ShapeDtypeStruct((T, D_MODEL), F32),
        compiler_params=pltpu.CompilerParams(
            dimension_semantics=("arbitrary",), vmem_limit_bytes=VMEM_LIMIT),
        name="mix_out",
    )(x, bg, u, u, u, conv_w, conv_b, y_mlstm, w_out, gpost)


def kernel(x, norm_ffn1_pre, norm_ffn1_post, w_ffn1_in, w_ffn1_out, norm_mix_pre, norm_mix_post,
           w_mix_in, conv_w, conv_b, gate_i_bias, gate_f_bias, mlstm_norm, w_mix_out,
           norm_ffn2_pre, norm_ffn2_post, w_ffn2_in, w_ffn2_out):
    batch, seq, _ = x.shape
    T = batch * seq
    depth = norm_ffn1_pre.shape[0]
    tm = 512
    xt = x.reshape(T, D_MODEL)
    qk0 = 3 * CONV_WIDTH
    k0 = qk0 + HEADS * DK
    v0 = k0 + HEADS * DK
    g0 = v0 + 2 * HEADS * DV
    for l in range(depth):
        xt = _ffn(xt, norm_ffn1_pre[l][None], norm_ffn1_post[l][None],
                  w_ffn1_in[l].astype(BF16), w_ffn1_out[l].astype(BF16), tm)

        w_in = w_mix_in[l]
        wa = w_in[:, :k0].astype(BF16)
        wkt = w_in[:, k0:v0].T.astype(BF16)
        wb = w_in[:, v0:g0].astype(BF16)
        wg = jnp.pad(w_in[:, g0:], ((0, 0), (0, 128 - 4 * HEADS))).astype(BF16)
        gbias = jnp.pad(jnp.concatenate([gate_i_bias[l], gate_f_bias[l]]), (0, 128 - 4 * HEADS))[None]
        bg, u, q, kt, v, o, gates = _mix_in(xt, norm_mix_pre[l][None], wa, wkt, wb, wg, gbias, tm)
        y_mlstm = _mlstm(q, kt, v, gates, o, mlstm_norm[l][None], batch, seq)
        xt = _mix_out(xt, bg, u, conv_w[l], conv_b[l][None], y_mlstm,
                      w_mix_out[l].astype(BF16), norm_mix_post[l][None], seq, tm)

        xt = _ffn(xt, norm_ffn2_pre[l][None], norm_ffn2_post[l][None],
                  w_ffn2_in[l].astype(BF16), w_ffn2_out[l].astype(BF16), tm)
    return xt.reshape(batch, seq, D_MODEL)
```

```python
import functools

import jax
import jax.numpy as jnp
from jax import lax
from jax.experimental import pallas as pl
from jax.experimental.pallas import tpu as pltpu

D_MODEL = 1024
D_FF = 2816
CONV_WIDTH = 512
HEADS = 4
DK = 64
DV = 128
CHUNK = 128
EPS = 1e-6
NEG_INF = -1e30

FF_TILE = 256
MLSTM_ROWS = 512
DVA = DV + 16
VMEM_LIMIT = 56 * 1024 * 1024

F32 = jnp.float32
BF16 = jnp.bfloat16


def _rms(x, g):
    return x * lax.rsqrt(jnp.mean(x * x, axis=-1, keepdims=True) + EPS) * g


def _log_sigmoid(z):
    return jnp.minimum(z, 0.0) - jnp.log1p(jnp.exp(-jnp.abs(z)))


def _resident(shape):
    zeros = (0,) * len(shape)
    return pl.BlockSpec(shape, lambda *_: zeros, pipeline_mode=pl.Buffered(1))


def _ffn_kernel(x_ref, gpre_ref, gpost_ref, win_ref, wout_ref, o_ref, h_ref):
    x = x_ref[...]
    xn = _rms(x, gpre_ref[...]).astype(BF16)
    for j in range(D_FF // FF_TILE):
        lo = j * FF_TILE
        gate = jnp.dot(xn, win_ref[:, lo:lo + FF_TILE], preferred_element_type=F32)
        up = jnp.dot(xn, win_ref[:, D_FF + lo:D_FF + lo + FF_TILE], preferred_element_type=F32)
        h_ref[:, lo:lo + FF_TILE] = (gate * jax.nn.sigmoid(gate) * up).astype(BF16)
    y = jnp.dot(h_ref[...], wout_ref[...], preferred_element_type=F32)
    o_ref[...] = x + 0.5 * _rms(y, gpost_ref[...])


def _ffn(x, gpre, gpost, w_in, w_out, tm):
    T = x.shape[0]
    return pl.pallas_call(
        _ffn_kernel,
        grid=(T // tm,),
        in_specs=[
            pl.BlockSpec((tm, D_MODEL), lambda i: (i, 0)),
            _resident((1, D_MODEL)),
            _resident((1, D_MODEL)),
            _resident((D_MODEL, 2 * D_FF)),
            _resident((D_FF, D_MODEL)),
        ],
        out_specs=pl.BlockSpec((tm, D_MODEL), lambda i: (i, 0)),
        out_shape=jax.ShapeDtypeStruct((T, D_MODEL), F32),
        scratch_shapes=[pltpu.VMEM((tm, D_FF), BF16)],
        compiler_params=pltpu.CompilerParams(
            dimension_semantics=("arbitrary",), vmem_limit_bytes=VMEM_LIMIT),
        name="ffn",
    )(x, gpre, gpost, w_in, w_out)


def _mix_in_kernel(x_ref, gpre_ref, wa_ref, wqt_ref, wk_ref, wvt_ref, wo_ref, wgt_ref, wgc_ref,
                   brow_ref, bcol_ref,
                   bg_ref, u_ref, qt_ref, k_ref, ksw_ref, vt_ref, o_ref, grow_ref, gcol_ref):
    xn = _rms(x_ref[...], gpre_ref[...]).astype(BF16)
    W = CONV_WIDTH
    HK = HEADS * DK

    def nt(w):
        return lax.dot_general(w, xn, (((1,), (1,)), ((), ())), preferred_element_type=F32)

    bg_ref[...] = jnp.dot(xn, wa_ref[:, 0:W], preferred_element_type=F32)
    cg = jnp.dot(xn, wa_ref[:, W:2 * W], preferred_element_type=F32)
    hc = jnp.dot(xn, wa_ref[:, 2 * W:3 * W], preferred_element_type=F32)
    u_ref[...] = cg * hc
    qt_ref[...] = (nt(wqt_ref[...]) * (DK ** -0.5)).astype(BF16)
    kk = jnp.dot(xn, wk_ref[...], preferred_element_type=F32)
    k_ref[...] = kk[:, 0:HK].astype(BF16)
    ksw_ref[...] = kk[:, HK:2 * HK].astype(BF16)
    vt_ref[...] = nt(wvt_ref[...]).astype(BF16)
    o_ref[...] = jnp.dot(xn, wo_ref[...], preferred_element_type=F32)
    zr = nt(wgt_ref[...]) + brow_ref[...]
    sub = lax.broadcasted_iota(jnp.int32, zr.shape, 0)
    grow_ref[...] = jnp.where(sub < 2 * HEADS, zr, _log_sigmoid(zr))
    zc = jnp.dot(xn, wgc_ref[...], preferred_element_type=F32) + bcol_ref[...]
    lane = lax.broadcasted_iota(jnp.int32, zc.shape, 1)
    gcol_ref[...] = jnp.where(lane < 128, zc, jnp.where(lane < 128 + 2 * HEADS, _log_sigmoid(zc), 0.0))


def _mix_in(x, gpre, wa, wqt, wk, wvt, wo, wgt, wgc, brow, bcol, tm):
    T = x.shape[0]
    row = lambda w: pl.BlockSpec((tm, w), lambda i: (i, 0))
    col = lambda h: pl.BlockSpec((h, tm), lambda i: (0, i))
    HK, HV = HEADS * DK, HEADS * DV
    return pl.pallas_call(
        _mix_in_kernel,
        grid=(T // tm,),
        in_specs=[row(D_MODEL), _resident((1, D_MODEL))]
        + [_resident(w.shape) for w in (wa, wqt, wk, wvt, wo, wgt, wgc, brow, bcol)],
        out_specs=[row(CONV_WIDTH), row(CONV_WIDTH), col(HK), row(HK), row(HK), col(HV), row(HV),
                   col(4 * HEADS), row(256)],
        out_shape=[
            jax.ShapeDtypeStruct((T, CONV_WIDTH), F32),
            jax.ShapeDtypeStruct((T, CONV_WIDTH), F32),
            jax.ShapeDtypeStruct((HK, T), BF16),
            jax.ShapeDtypeStruct((T, HK), BF16),
            jax.ShapeDtypeStruct((T, HK), BF16),
            jax.ShapeDtypeStruct((HV, T), BF16),
            jax.ShapeDtypeStruct((T, HV), F32),
            jax.ShapeDtypeStruct((4 * HEADS, T), F32),
            jax.ShapeDtypeStruct((T, 256), F32),
        ],
        compiler_params=pltpu.CompilerParams(
            dimension_semantics=("arbitrary",), vmem_limit_bytes=VMEM_LIMIT),
        name="mix_in",
    )(x, gpre, wa, wqt, wk, wvt, wo, wgt, wgc, brow, bcol)


def _split3(x):
    hi = x.astype(BF16)
    r1 = x - hi.astype(F32)
    mid = r1.astype(BF16)
    lo = (r1 - mid.astype(F32)).astype(BF16)
    return hi, mid, lo


def _visible(rev):
    s = lax.broadcasted_iota(jnp.int32, (CHUNK, CHUNK), 0)
    t = lax.broadcasted_iota(jnp.int32, (CHUNK, CHUNK), 1)
    return (s >= t) if rev else (s <= t)


def _running_max(x, rev):
    L = x.shape[1]
    lane = lax.broadcasted_iota(jnp.int32, x.shape, 1)
    k = 1
    while k < L:
        if rev:
            shifted, ok = pltpu.roll(x, L - k, 1), lane < L - k
        else:
            shifted, ok = pltpu.roll(x, k, 1), lane >= k
        x = jnp.maximum(x, jnp.where(ok, shifted, NEG_INF))
        k *= 2
    return x


def _scan_direction(d, vt_ref, k_ref, ksw_ref, grow_ref, gcol_ref, ct_ref, rows_ref, ccol_ref,
                    ct_state, m_state):
    L = CHUNK
    n_chunks = k_ref.shape[0] // L
    rev = d == 1
    cum_rows = _visible(rev).astype(BF16)
    cum_cols = _visible(not rev).astype(BF16)
    lane_half = lax.broadcasted_iota(jnp.int32, (L, 128), 1) // DK
    ones_rows = jnp.ones((DVA - DV, L), BF16)

    cts = [ct_state[d, h] for h in range(HEADS)]
    m_prev = m_state[d]
    for chunk in (range(n_chunks - 1, -1, -1) if rev else range(n_chunks)):
        sl = slice(chunk * L, (chunk + 1) * L)
        gr = grow_ref[:, sl]
        cum3 = jnp.dot(jnp.concatenate(_split3(gr), axis=0), cum_rows, preferred_element_type=F32)
        cum = cum3[0:16] + cum3[16:32] + cum3[32:48]
        li, b = gr[0:8], cum[8:16]
        g = b[:, 0:1] if rev else b[:, L - 1:L]
        w = g - b + li
        m_chunk = jnp.max(w, axis=1, keepdims=True)
        e = jnp.exp(w - m_chunk)
        n_t = jnp.maximum(m_prev, _running_max(li - b, rev))
        rows_ref[0:8, sl] = n_t
        rows_ref[8:16, sl] = jnp.exp(m_prev - n_t)
        rows_ref[16:24, sl] = jnp.exp(-(b + n_t))
        m_new = jnp.maximum(g + m_prev, m_chunk)
        a_old = jnp.exp(g + m_prev - m_new)
        a_new = jnp.exp(m_chunk - m_new)
        gc = gcol_ref[sl, :]
        cumc3 = jnp.dot(cum_cols, jnp.concatenate(_split3(gc[:, 128:256]), axis=1),
                        preferred_element_type=F32)
        ccol_ref[sl, :] = gc[:, 0:128] - (cumc3[:, 0:128] + cumc3[:, 128:256] + cumc3[:, 256:384])
        for h in range(HEADS):
            r = HEADS * d + h
            vt_aug = jnp.concatenate([vt_ref[h * DV:(h + 1) * DV, sl], ones_rows], axis=0)
            vte = (vt_aug.astype(F32) * e[r:r + 1, :]).astype(BF16)
            src = k_ref if h % 2 == d else ksw_ref
            pair = slice((h // 2) * 2 * DK, (h // 2 + 1) * 2 * DK)
            k_half = jnp.where(lane_half == d, src[sl, pair], jnp.zeros((L, 128), BF16))
            ct_chunk = jnp.dot(vte, k_half, preferred_element_type=F32)
            ct_ref[chunk, h * DVA:(h + 1) * DVA, :] = cts[h].astype(BF16)
            cts[h] = a_old[r:r + 1, 0:1] * cts[h] + a_new[r:r + 1, 0:1] * ct_chunk
        m_prev = m_new
    for h in range(HEADS):
        ct_state[d, h] = cts[h]
    m_state[d] = m_prev


def _mlstm_scan_kernel(vt_f, k_f, ksw_f, grow_f, gcol_f, vt_b, k_b, ksw_b, grow_b, gcol_b,
                       ct_f, rows_f, ccol_f, ct_b, rows_b, ccol_b, ct_state, m_state):
    @pl.when(pl.program_id(1) == 0)
    def _():
        ct_state[...] = jnp.zeros(ct_state.shape, F32)
        m_state[...] = jnp.full(m_state.shape, NEG_INF, F32)

    _scan_direction(0, vt_f, k_f, ksw_f, grow_f, gcol_f, ct_f, rows_f, ccol_f, ct_state, m_state)
    _scan_direction(1, vt_b, k_b, ksw_b, grow_b, gcol_b, ct_b, rows_b, ccol_b, ct_state, m_state)


def _mlstm_out_kernel(qt_ref, k_ref, vt_ref, rows_f, rows_b, ccol_f, ccol_b, ct_f, ct_b,
                      o_ref, gain_ref, y_ref):
    L = CHUNK
    n_chunks = k_ref.shape[0] // L
    visible = (_visible(False), _visible(True))
    rows, ccol = (rows_f, rows_b), (ccol_f, ccol_b)
    lane_half = lax.broadcasted_iota(jnp.int32, (DVA, 128), 1) // DK
    ones_rows = jnp.ones((DVA - DV, L), BF16)
    zq = jnp.zeros((DK, L), BF16)
    for chunk in range(n_chunks):
        sl = slice(chunk * L, (chunk + 1) * L)
        for h in range(HEADS):
            hs = slice(h * DV, (h + 1) * DV)
            qt = qt_ref[h * DK:(h + 1) * DK, sl]
            k_pair = k_ref[sl, (h // 2) * 2 * DK:(h // 2 + 1) * 2 * DK]
            qt_pair = jnp.concatenate([qt, zq] if h % 2 == 0 else [zq, qt], axis=0)
            st = jnp.dot(k_pair, qt_pair, preferred_element_type=F32)
            pts, qfs = [], []
            for d in range(2):
                r = HEADS * d + h
                n_row = rows[d][r:r + 1, sl]
                f_row = rows[d][8 + r:9 + r, sl]
                c_col = ccol[d][sl, r:r + 1]
                pts.append((jnp.exp(jnp.where(visible[d], c_col - n_row, NEG_INF)) * st).astype(BF16))
                qfs.append((qt.astype(F32) * f_row).astype(BF16))
            rhs = jnp.concatenate([
                jnp.concatenate(pts, axis=1),
                jnp.concatenate([qfs[0], zq], axis=1),
                jnp.concatenate([zq, qfs[1]], axis=1)], axis=0)
            vt_aug = jnp.concatenate([vt_ref[hs, sl], ones_rows], axis=0)
            ct = jnp.where(lane_half == 0, ct_f[chunk, h * DVA:(h + 1) * DVA, :],
                           ct_b[chunk, h * DVA:(h + 1) * DVA, :])
            both = jnp.dot(jnp.concatenate([vt_aug, ct], axis=1), rhs, preferred_element_type=F32)
            ht = None
            for d in range(2):
                r = HEADS * d + h
                numer = both[0:DV, d * L:(d + 1) * L]
                denom = both[DV:DV + 1, d * L:(d + 1) * L]
                part = numer / jnp.maximum(jnp.abs(denom), rows[d][16 + r:17 + r, sl])
                ht = part if ht is None else ht + part
            ms = jnp.mean(ht * ht, axis=0, keepdims=True)
            hn = ht * lax.rsqrt(ms + EPS) * gain_ref[hs, :]
            y_ref[sl, hs] = (jax.nn.sigmoid(o_ref[sl, hs]) * hn.T).astype(BF16)


def _mlstm(qt, k, ksw, vt, grow, gcol, o, gain_b, batch, seq, rows):
    T = batch * seq
    ng = seq // rows
    gc = rows // CHUNK
    HK, HV = HEADS * DK, HEADS * DV

    def scan_specs(group_of):
        blk = lambda b, j: b * ng + group_of(j)
        ins = [pl.BlockSpec((HV, rows), lambda b, j: (0, blk(b, j))),
               pl.BlockSpec((rows, HK), lambda b, j: (blk(b, j), 0)),
               pl.BlockSpec((rows, HK), lambda b, j: (blk(b, j), 0)),
               pl.BlockSpec((4 * HEADS, rows), lambda b, j: (0, blk(b, j))),
               pl.BlockSpec((rows, 256), lambda b, j: (blk(b, j), 0))]
        outs = [pl.BlockSpec((gc, HEADS * DVA, 128), lambda b, j: (blk(b, j), 0, 0)),
                pl.BlockSpec((24, rows), lambda b, j: (0, blk(b, j))),
                pl.BlockSpec((rows, 128), lambda b, j: (blk(b, j), 0))]
        return ins, outs

    ins_f, outs_f = scan_specs(lambda j: j)
    ins_b, outs_b = scan_specs(lambda j: ng - 1 - j)
    state_shapes = [jax.ShapeDtypeStruct((T // CHUNK, HEADS * DVA, 128), BF16),
                    jax.ShapeDtypeStruct((24, T), F32),
                    jax.ShapeDtypeStruct((T, 128), F32)]
    ct_f, rows_f, ccol_f, ct_b, rows_b, ccol_b = pl.pallas_call(
        _mlstm_scan_kernel,
        grid=(batch, ng),
        in_specs=ins_f + ins_b,
        out_specs=outs_f + outs_b,
        out_shape=state_shapes + state_shapes,
        scratch_shapes=[pltpu.VMEM((2, HEADS, DVA, 128), F32), pltpu.VMEM((2, 8, 128), F32)],
        compiler_params=pltpu.CompilerParams(dimension_semantics=("arbitrary", "arbitrary"),
                                             vmem_limit_bytes=VMEM_LIMIT),
        name="mlstm_scan",
    )(vt, k, ksw, grow, gcol, vt, k, ksw, grow, gcol)

    row = lambda w: pl.BlockSpec((rows, w), lambda i: (i, 0))
    col = lambda h: pl.BlockSpec((h, rows), lambda i: (0, i))
    ctb = pl.BlockSpec((gc, HEADS * DVA, 128), lambda i: (i, 0, 0))
    return pl.pallas_call(
        _mlstm_out_kernel,
        grid=(T // rows,),
        in_specs=[col(HK), row(HK), col(HV), col(24), col(24), row(128), row(128), ctb, ctb,
                  row(HV), _resident((HV, 128))],
        out_specs=row(HV),
        out_shape=jax.ShapeDtypeStruct((T, HV), BF16),
        compiler_params=pltpu.CompilerParams(dimension_semantics=("arbitrary",),
                                             vmem_limit_bytes=VMEM_LIMIT),
        name="mlstm_out",
    )(qt, k, vt, rows_f, rows_b, ccol_f, ccol_b, ct_f, ct_b, o, gain_b)


def _mix_out_kernel(tiles_per_seq, x_ref, bg_ref, u_ref, uprev_ref, unext_ref, cw_ref, cb_ref,
                    y_ref, wout_ref, gpost_ref, o_ref):
    i = pl.program_id(0)
    tm = u_ref.shape[0]
    u = u_ref[...]
    has_prev = (i % tiles_per_seq != 0).astype(F32)
    has_next = (i % tiles_per_seq != tiles_per_seq - 1).astype(F32)
    prev_row = uprev_ref[7:8, :] * has_prev
    next_row = unext_ref[0:1, :] * has_next
    ri = lax.broadcasted_iota(jnp.int32, u.shape, 0)
    u_m1 = jnp.where(ri == 0, prev_row, pltpu.roll(u, 1, 0))
    u_p1 = jnp.where(ri == tm - 1, next_row, pltpu.roll(u, tm - 1, 0))
    conv = cw_ref[0:1, :] * u_m1 + cw_ref[1:2, :] * u + cw_ref[2:3, :] * u_p1
    y_conv = (bg_ref[...] * (conv + cb_ref[...])).astype(BF16)
    h = jnp.dot(y_conv, wout_ref[0:CONV_WIDTH, :], preferred_element_type=F32) \
        + jnp.dot(y_ref[...], wout_ref[CONV_WIDTH:, :], preferred_element_type=F32)
    o_ref[...] = x_ref[...] + _rms(h, gpost_ref[...])


def _mix_out(x, bg, u, conv_w, conv_b, y_mlstm, w_out, gpost, seq, tm):
    T = x.shape[0]
    tiles_per_seq = seq // tm
    sub = tm // 8
    last = T // 8 - 1
    row = lambda w: pl.BlockSpec((tm, w), lambda i: (i, 0))
    return pl.pallas_call(
        functools.partial(_mix_out_kernel, tiles_per_seq),
        grid=(T // tm,),
        in_specs=[
            row(D_MODEL), row(CONV_WIDTH), row(CONV_WIDTH),
            pl.BlockSpec((8, CONV_WIDTH), lambda i: (jnp.maximum(i * sub - 1, 0), 0)),
            pl.BlockSpec((8, CONV_WIDTH), lambda i: (jnp.minimum((i + 1) * sub, last), 0)),
            _resident((3, CONV_WIDTH)),
            _resident((1, CONV_WIDTH)),
            row(HEADS * DV),
            _resident((D_MODEL, D_MODEL)),
            _resident((1, D_MODEL)),
        ],
        out_specs=row(D_MODEL),
        out_shape=jax.ShapeDtypeStruct((T, D_MODEL), F32),
        compiler_params=pltpu.CompilerParams(
            dimension_semantics=("arbitrary",), vmem_limit_bytes=VMEM_LIMIT),
        name="mix_out",
    )(x, bg, u, u, u, conv_w, conv_b, y_mlstm, w_out, gpost)


def _swap_head_pairs(w):
    d = w.shape[0]
    return w.reshape(d, HEADS // 2, 2, DK)[:, :, ::-1, :].reshape(d, HEADS * DK)


def kernel(x, norm_ffn1_pre, norm_ffn1_post, w_ffn1_in, w_ffn1_out, norm_mix_pre, norm_mix_post,
           w_mix_in, conv_w, conv_b, gate_i_bias, gate_f_bias, mlstm_norm, w_mix_out,
           norm_ffn2_pre, norm_ffn2_post, w_ffn2_in, w_ffn2_out):
    batch, seq, _ = x.shape
    T = batch * seq
    depth = norm_ffn1_pre.shape[0]
    tm = 512
    xt = x.reshape(T, D_MODEL)
    HK, HV, NG = HEADS * DK, HEADS * DV, 2 * HEADS
    q0 = 3 * CONV_WIDTH
    k0 = q0 + HK
    v0 = k0 + HK
    o0 = v0 + HV
    g0 = o0 + HV
    for l in range(depth):
        xt = _ffn(xt, norm_ffn1_pre[l][None], norm_ffn1_post[l][None],
                  w_ffn1_in[l].astype(BF16), w_ffn1_out[l].astype(BF16), tm)

        w_in = w_mix_in[l]
        wa = w_in[:, :q0].astype(BF16)
        wqt = w_in[:, q0:k0].T.astype(BF16)
        w_k = w_in[:, k0:v0]
        wk = jnp.concatenate([w_k, _swap_head_pairs(w_k)], axis=1).astype(BF16)
        wvt = w_in[:, v0:o0].T.astype(BF16)
        wo = w_in[:, o0:g0].astype(BF16)
        w_gi, w_gf = w_in[:, g0:g0 + NG], w_in[:, g0 + NG:]
        wgt = w_in[:, g0:].T.astype(BF16)
        wgc = jnp.concatenate([jnp.pad(w_gi, ((0, 0), (0, 128 - NG))),
                               jnp.pad(w_gf, ((0, 0), (0, 128 - NG)))], axis=1).astype(BF16)
        brow = jnp.concatenate([gate_i_bias[l], gate_f_bias[l]])[:, None]
        bcol = jnp.concatenate([jnp.pad(gate_i_bias[l], (0, 128 - NG)),
                                jnp.pad(gate_f_bias[l], (0, 128 - NG))])[None]
        bg, u, qt, k, ksw, vt, o, grow, gcol = _mix_in(
            xt, norm_mix_pre[l][None], wa, wqt, wk, wvt, wo, wgt, wgc, brow, bcol, tm)
        gain_b = jnp.broadcast_to(mlstm_norm[l][:, None], (HV, 128))
        y_mlstm = _mlstm(qt, k, ksw, vt, grow, gcol, o, gain_b, batch, seq, MLSTM_ROWS)
        xt = _mix_out(xt, bg, u, conv_w[l], conv_b[l][None], y_mlstm,
                      w_mix_out[l].astype(BF16), norm_mix_post[l][None], seq, tm)

        xt = _ffn(xt, norm_ffn2_pre[l][None], norm_ffn2_post[l][None],
                  w_ffn2_in[l].astype(BF16), w_ffn2_out[l].astype(BF16), tm)
    return xt.reshape(batch, seq, D_MODEL)
```

```python
import functools

import jax
import jax.numpy as jnp
from jax import lax
from jax.experimental import pallas as pl
from jax.experimental.pallas import tpu as pltpu

D_MODEL = 1024
D_FF = 2816
CONV_WIDTH = 512
HEADS = 4
DK = 64
DV = 128
CHUNK = 128
EPS = 1e-6
NEG_INF = -1e30

FF_TILE = 256
MLSTM_ROWS = 512
DVA = DV + 16
NGR = 2 * HEADS
VMEM_LIMIT = 56 * 1024 * 1024

P_E, P_M, P_B, P_G, P_MC = (slice(i * NGR, (i + 1) * NGR) for i in range(5))

F32 = jnp.float32
BF16 = jnp.bfloat16


def _rms(x, g):
    return x * lax.rsqrt(jnp.mean(x * x, axis=-1, keepdims=True) + EPS) * g


def _log_sigmoid(z):
    return jnp.minimum(z, 0.0) - jnp.log1p(jnp.exp(-jnp.abs(z)))


def _resident(shape):
    zeros = (0,) * len(shape)
    return pl.BlockSpec(shape, lambda *_: zeros, pipeline_mode=pl.Buffered(1))


def _split3(x):
    hi = x.astype(BF16)
    r1 = x - hi.astype(F32)
    mid = r1.astype(BF16)
    lo = (r1 - mid.astype(F32)).astype(BF16)
    return hi, mid, lo


def _visible(rev):
    s = lax.broadcasted_iota(jnp.int32, (CHUNK, CHUNK), 0)
    t = lax.broadcasted_iota(jnp.int32, (CHUNK, CHUNK), 1)
    return (s >= t) if rev else (s <= t)


def _running_max(x, rev):
    n = x.shape[1]
    pos = lax.broadcasted_iota(jnp.int32, x.shape, 1) & (CHUNK - 1)
    k = 1
    while k < CHUNK:
        if rev:
            shifted, ok = pltpu.roll(x, n - k, 1), pos < CHUNK - k
        else:
            shifted, ok = pltpu.roll(x, k, 1), pos >= k
        x = jnp.maximum(x, jnp.where(ok, shifted, NEG_INF))
        k *= 2
    return x


def _ffn_kernel(x_ref, gpre_ref, gpost_ref, win_ref, wout_ref, o_ref, h_ref):
    x = x_ref[...]
    xn = _rms(x, gpre_ref[...]).astype(BF16)
    for j in range(D_FF // FF_TILE):
        lo = j * FF_TILE
        gate = jnp.dot(xn, win_ref[:, lo:lo + FF_TILE], preferred_element_type=F32)
        up = jnp.dot(xn, win_ref[:, D_FF + lo:D_FF + lo + FF_TILE], preferred_element_type=F32)
        h_ref[:, lo:lo + FF_TILE] = (gate * jax.nn.sigmoid(gate) * up).astype(BF16)
    y = jnp.dot(h_ref[...], wout_ref[...], preferred_element_type=F32)
    o_ref[...] = x + 0.5 * _rms(y, gpost_ref[...])


def _ffn(x, gpre, gpost, w_in, w_out, tm):
    T = x.shape[0]
    return pl.pallas_call(
        _ffn_kernel,
        grid=(T // tm,),
        in_specs=[
            pl.BlockSpec((tm, D_MODEL), lambda i: (i, 0)),
            _resident((1, D_MODEL)),
            _resident((1, D_MODEL)),
            _resident((D_MODEL, 2 * D_FF)),
            _resident((D_FF, D_MODEL)),
        ],
        out_specs=pl.BlockSpec((tm, D_MODEL), lambda i: (i, 0)),
        out_shape=jax.ShapeDtypeStruct((T, D_MODEL), F32),
        scratch_shapes=[pltpu.VMEM((tm, D_FF), BF16)],
        compiler_params=pltpu.CompilerParams(
            dimension_semantics=("arbitrary",), vmem_limit_bytes=VMEM_LIMIT),
        name="ffn",
    )(x, gpre, gpost, w_in, w_out)


def _gate_rows(zr, p_ref, ccol_ref):
    L = CHUNK
    n_chunks = zr.shape[1] // L
    fwd_row = lax.broadcasted_iota(jnp.int32, (NGR, L), 0) < HEADS
    fwd_col = lax.broadcasted_iota(jnp.int32, (NGR, 1), 0) < HEADS
    li = zr[0:NGR]
    gates = jnp.concatenate([li, _log_sigmoid(zr[NGR:2 * NGR])], axis=0)
    x3 = jnp.concatenate(_split3(gates), axis=0)
    stacked = jnp.concatenate([x3[:, c * L:(c + 1) * L] for c in range(n_chunks)], axis=0)
    cum_ops = jnp.concatenate([_visible(False), _visible(True)], axis=1).astype(BF16)
    cum = jnp.dot(stacked, cum_ops, preferred_element_type=F32)
    b_chunks = []
    for c in range(n_chunks):
        blk = cum[c * 48:(c + 1) * 48]
        s16 = blk[0:16] + blk[16:32] + blk[32:48]
        b_chunks.append(jnp.where(fwd_row, s16[NGR:2 * NGR, 0:L], s16[NGR:2 * NGR, L:2 * L]))
    b = jnp.concatenate(b_chunks, axis=1)
    cc = li - b
    fwd_all = lax.broadcasted_iota(jnp.int32, cc.shape, 0) < HEADS
    p_ref[P_M, :] = jnp.where(fwd_all, _running_max(cc, False), _running_max(cc, True))
    p_ref[P_B, :] = b
    pad = jnp.zeros((L - NGR, L), F32)
    for c in range(n_chunks):
        sl = slice(c * L, (c + 1) * L)
        b_c, cc_c = b_chunks[c], cc[:, sl]
        g = jnp.where(fwd_col, b_c[:, L - 1:L], b_c[:, 0:1])
        m_chunk = g + jnp.max(cc_c, axis=1, keepdims=True)
        p_ref[P_E, sl] = jnp.exp(g + cc_c - m_chunk)
        p_ref[P_G, sl] = jnp.broadcast_to(g, (NGR, L))
        p_ref[P_MC, sl] = jnp.broadcast_to(m_chunk, (NGR, L))
        ccol_ref[sl, :] = jnp.concatenate([cc_c, pad], axis=0).T


def _mix_in_kernel(x_ref, gpre_ref, wa_ref, wqt_ref, wk_ref, wvt_ref, wo_ref, wgt_ref, brow_ref,
                   bg_ref, u_ref, qt_ref, k_ref, ksw_ref, vt_ref, o_ref, p_ref, ccol_ref):
    xn = _rms(x_ref[...], gpre_ref[...]).astype(BF16)
    W = CONV_WIDTH
    HK = HEADS * DK

    def nt(w):
        return lax.dot_general(w, xn, (((1,), (1,)), ((), ())), preferred_element_type=F32)

    _gate_rows(nt(wgt_ref[...]) + brow_ref[...], p_ref, ccol_ref)
    bg_ref[...] = jnp.dot(xn, wa_ref[:, 0:W], preferred_element_type=F32)
    cg = jnp.dot(xn, wa_ref[:, W:2 * W], preferred_element_type=F32)
    hc = jnp.dot(xn, wa_ref[:, 2 * W:3 * W], preferred_element_type=F32)
    u_ref[...] = cg * hc
    qt_ref[...] = (nt(wqt_ref[...]) * (DK ** -0.5)).astype(BF16)
    kk = jnp.dot(xn, wk_ref[...], preferred_element_type=F32)
    k_ref[...] = kk[:, 0:HK].astype(BF16)
    ksw_ref[...] = kk[:, HK:2 * HK].astype(BF16)
    vt_ref[...] = nt(wvt_ref[...]).astype(BF16)
    o_ref[...] = jnp.dot(xn, wo_ref[...], preferred_element_type=F32)


def _mix_in(x, gpre, wa, wqt, wk, wvt, wo, wgt, brow, tm):
    T = x.shape[0]
    row = lambda w: pl.BlockSpec((tm, w), lambda i: (i, 0))
    col = lambda h: pl.BlockSpec((h, tm), lambda i: (0, i))
    HK, HV = HEADS * DK, HEADS * DV
    return pl.pallas_call(
        _mix_in_kernel,
        grid=(T // tm,),
        in_specs=[row(D_MODEL), _resident((1, D_MODEL))]
        + [_resident(w.shape) for w in (wa, wqt, wk, wvt, wo, wgt, brow)],
        out_specs=[row(CONV_WIDTH), row(CONV_WIDTH), col(HK), row(HK), row(HK), col(HV), row(HV),
                   col(5 * NGR), row(128)],
        out_shape=[
            jax.ShapeDtypeStruct((T, CONV_WIDTH), F32),
            jax.ShapeDtypeStruct((T, CONV_WIDTH), F32),
            jax.ShapeDtypeStruct((HK, T), BF16),
            jax.ShapeDtypeStruct((T, HK), BF16),
            jax.ShapeDtypeStruct((T, HK), BF16),
            jax.ShapeDtypeStruct((HV, T), BF16),
            jax.ShapeDtypeStruct((T, HV), F32),
            jax.ShapeDtypeStruct((5 * NGR, T), F32),
            jax.ShapeDtypeStruct((T, 128), F32),
        ],
        compiler_params=pltpu.CompilerParams(
            dimension_semantics=("arbitrary",), vmem_limit_bytes=VMEM_LIMIT),
        name="mix_in",
    )(x, gpre, wa, wqt, wk, wvt, wo, wgt, brow)


def _scan_direction(d, vt_ref, k_ref, ksw_ref, p_ref, ct_ref, mp_ref, ct_state, m_state):
    L = CHUNK
    n_chunks = k_ref.shape[0] // L
    lane_half = lax.broadcasted_iota(jnp.int32, (L, 128), 1) // DK
    ones_rows = jnp.ones((DVA - DV, L), BF16)

    cts = [ct_state[d, h] for h in range(HEADS)]
    m_prev = m_state[d]
    for chunk in (range(n_chunks - 1, -1, -1) if d == 1 else range(n_chunks)):
        sl = slice(chunk * L, (chunk + 1) * L)
        e, g, m_chunk = p_ref[P_E, sl], p_ref[P_G, sl], p_ref[P_MC, sl]
        mp_ref[:, sl] = m_prev
        m_new = jnp.maximum(g + m_prev, m_chunk)
        a_old = jnp.exp(g + m_prev - m_new)
        a_new = jnp.exp(m_chunk - m_new)
        for h in range(HEADS):
            r = HEADS * d + h
            vt_aug = jnp.concatenate([vt_ref[h * DV:(h + 1) * DV, sl], ones_rows], axis=0)
            vte = (vt_aug.astype(F32) * e[r:r + 1, :]).astype(BF16)
            src = k_ref if h % 2 == d else ksw_ref
            pair = slice((h // 2) * 2 * DK, (h // 2 + 1) * 2 * DK)
            k_half = jnp.where(lane_half == d, src[sl, pair], jnp.zeros((L, 128), BF16))
            ct_chunk = jnp.dot(vte, k_half, preferred_element_type=F32)
            ct_ref[chunk, h * DVA:(h + 1) * DVA, :] = cts[h].astype(BF16)
            cts[h] = a_old[r:r + 1, 0:1] * cts[h] + a_new[r:r + 1, 0:1] * ct_chunk
        m_prev = m_new
    for h in range(HEADS):
        ct_state[d, h] = cts[h]
    m_state[d] = m_prev


def _mlstm_scan_kernel(vt_f, k_f, ksw_f, p_f, vt_b, k_b, ksw_b, p_b,
                       ct_f, mp_f, ct_b, mp_b, ct_state, m_state):
    @pl.when(pl.program_id(1) == 0)
    def _():
        ct_state[...] = jnp.zeros(ct_state.shape, F32)
        m_state[...] = jnp.full(m_state.shape, NEG_INF, F32)

    _scan_direction(0, vt_f, k_f, ksw_f, p_f, ct_f, mp_f, ct_state, m_state)
    _scan_direction(1, vt_b, k_b, ksw_b, p_b, ct_b, mp_b, ct_state, m_state)


def _mlstm_out_kernel(qt_ref, k_ref, vt_ref, p_ref, mp_f, mp_b, ccol_ref, ct_f, ct_b,
                      o_ref, gain_ref, y_ref):
    L = CHUNK
    n_chunks = k_ref.shape[0] // L
    visible = (_visible(False), _visible(True))
    fwd_row = lax.broadcasted_iota(jnp.int32, (NGR, L), 0) < HEADS
    lane_half = lax.broadcasted_iota(jnp.int32, (DVA, 128), 1) // DK
    ones_rows = jnp.ones((DVA - DV, L), BF16)
    zq = jnp.zeros((DK, L), BF16)
    for chunk in range(n_chunks):
        sl = slice(chunk * L, (chunk + 1) * L)
        m_prev = jnp.where(fwd_row, mp_f[:, sl], mp_b[:, sl])
        n_t = jnp.maximum(m_prev, p_ref[P_M, sl])
        f_inter = jnp.exp(m_prev - n_t)
        e_min = jnp.exp(-(p_ref[P_B, sl] + n_t))
        ccol = ccol_ref[sl, :]
        for h in range(HEADS):
            hs = slice(h * DV, (h + 1) * DV)
            qt = qt_ref[h * DK:(h + 1) * DK, sl]
            k_pair = k_ref[sl, (h // 2) * 2 * DK:(h // 2 + 1) * 2 * DK]
            qt_pair = jnp.concatenate([qt, zq] if h % 2 == 0 else [zq, qt], axis=0)
            st = jnp.dot(k_pair, qt_pair, preferred_element_type=F32)
            pts, qfs = [], []
            for d in range(2):
                r = HEADS * d + h
                arg = jnp.where(visible[d], ccol[:, r:r + 1] - n_t[r:r + 1, :], NEG_INF)
                pts.append((jnp.exp(arg) * st).astype(BF16))
                qfs.append((qt.astype(F32) * f_inter[r:r + 1, :]).astype(BF16))
            rhs = jnp.concatenate([
                jnp.concatenate(pts, axis=1),
                jnp.concatenate([qfs[0], zq], axis=1),
                jnp.concatenate([zq, qfs[1]], axis=1)], axis=0)
            vt_aug = jnp.concatenate([vt_ref[hs, sl], ones_rows], axis=0)
            ct = jnp.where(lane_half == 0, ct_f[chunk, h * DVA:(h + 1) * DVA, :],
                           ct_b[chunk, h * DVA:(h + 1) * DVA, :])
            both = jnp.dot(jnp.concatenate([vt_aug, ct], axis=1), rhs, preferred_element_type=F32)
            ht = None
            for d in range(2):
                r = HEADS * d + h
                numer = both[0:DV, d * L:(d + 1) * L]
                denom = both[DV:DV + 1, d * L:(d + 1) * L]
                part = numer / jnp.maximum(jnp.abs(denom), e_min[r:r + 1, :])
                ht = part if ht is None else ht + part
            ms = jnp.mean(ht * ht, axis=0, keepdims=True)
            hn = ht * lax.rsqrt(ms + EPS) * gain_ref[hs, :]
            y_ref[sl, hs] = (jax.nn.sigmoid(o_ref[sl, hs]) * hn.T).astype(BF16)


def _mlstm(qt, k, ksw, vt, p, ccol, o, gain_b, batch, seq, rows):
    T = batch * seq
    ng = seq // rows
    gc = rows // CHUNK
    HK, HV = HEADS * DK, HEADS * DV

    def scan_specs(group_of):
        blk = lambda b, j: b * ng + group_of(j)
        ins = [pl.BlockSpec((HV, rows), lambda b, j: (0, blk(b, j))),
               pl.BlockSpec((rows, HK), lambda b, j: (blk(b, j), 0)),
               pl.BlockSpec((rows, HK), lambda b, j: (blk(b, j), 0)),
               pl.BlockSpec((5 * NGR, rows), lambda b, j: (0, blk(b, j)))]
        outs = [pl.BlockSpec((gc, HEADS * DVA, 128), lambda b, j: (blk(b, j), 0, 0)),
                pl.BlockSpec((NGR, rows), lambda b, j: (0, blk(b, j)))]
        return ins, outs

    ins_f, outs_f = scan_specs(lambda j: j)
    ins_b, outs_b = scan_specs(lambda j: ng - 1 - j)
    state_shapes = [jax.ShapeDtypeStruct((T // CHUNK, HEADS * DVA, 128), BF16),
                    jax.ShapeDtypeStruct((NGR, T), F32)]
    ct_f, mp_f, ct_b, mp_b = pl.pallas_call(
        _mlstm_scan_kernel,
        grid=(batch, ng),
        in_specs=ins_f + ins_b,
        out_specs=outs_f + outs_b,
        out_shape=state_shapes + state_shapes,
        scratch_shapes=[pltpu.VMEM((2, HEADS, DVA, 128), F32), pltpu.VMEM((2, NGR, 128), F32)],
        compiler_params=pltpu.CompilerParams(dimension_semantics=("arbitrary", "arbitrary"),
                                             vmem_limit_bytes=VMEM_LIMIT),
        name="mlstm_scan",
    )(vt, k, ksw, p, vt, k, ksw, p)

    row = lambda w: pl.BlockSpec((rows, w), lambda i: (i, 0))
    col = lambda h: pl.BlockSpec((h, rows), lambda i: (0, i))
    ctb = pl.BlockSpec((gc, HEADS * DVA, 128), lambda i: (i, 0, 0))
    return pl.pallas_call(
        _mlstm_out_kernel,
        grid=(T // rows,),
        in_specs=[col(HK), row(HK), col(HV), col(5 * NGR), col(NGR), col(NGR), row(128), ctb, ctb,
                  row(HV), _resident((HV, 128))],
        out_specs=row(HV),
        out_shape=jax.ShapeDtypeStruct((T, HV), BF16),
        compiler_params=pltpu.CompilerParams(dimension_semantics=("arbitrary",),
                                             vmem_limit_bytes=VMEM_LIMIT),
        name="mlstm_out",
    )(qt, k, vt, p, mp_f, mp_b, ccol, ct_f, ct_b, o, gain_b)


def _mix_out_kernel(tiles_per_seq, x_ref, bg_ref, u_ref, uprev_ref, unext_ref, cw_ref, cb_ref,
                    y_ref, wout_ref, gpost_ref, o_ref):
    i = pl.program_id(0)
    tm = u_ref.shape[0]
    u = u_ref[...]
    has_prev = (i % tiles_per_seq != 0).astype(F32)
    has_next = (i % tiles_per_seq != tiles_per_seq - 1).astype(F32)
    prev_row = uprev_ref[7:8, :] * has_prev
    next_row = unext_ref[0:1, :] * has_next
    ri = lax.broadcasted_iota(jnp.int32, u.shape, 0)
    u_m1 = jnp.where(ri == 0, prev_row, pltpu.roll(u, 1, 0))
    u_p1 = jnp.where(ri == tm - 1, next_row, pltpu.roll(u, tm - 1, 0))
    conv = cw_ref[0:1, :] * u_m1 + cw_ref[1:2, :] * u + cw_ref[2:3, :] * u_p1
    y_conv = (bg_ref[...] * (conv + cb_ref[...])).astype(BF16)
    h = jnp.dot(y_conv, wout_ref[0:CONV_WIDTH, :], preferred_element_type=F32) \
        + jnp.dot(y_ref[...], wout_ref[CONV_WIDTH:, :], preferred_element_type=F32)
    o_ref[...] = x_ref[...] + _rms(h, gpost_ref[...])


def _mix_out(x, bg, u, conv_w, conv_b, y_mlstm, w_out, gpost, seq, tm):
    T = x.shape[0]
    tiles_per_seq = seq // tm
    sub = tm // 8
    last = T // 8 - 1
    row = lambda w: pl.BlockSpec((tm, w), lambda i: (i, 0))
    return pl.pallas_call(
        functools.partial(_mix_out_kernel, tiles_per_seq),
        grid=(T // tm,),
        in_specs=[
            row(D_MODEL), row(CONV_WIDTH), row(CONV_WIDTH),
            pl.BlockSpec((8, CONV_WIDTH), lambda i: (jnp.maximum(i * sub - 1, 0), 0)),
            pl.BlockSpec((8, CONV_WIDTH), lambda i: (jnp.minimum((i + 1) * sub, last), 0)),
            _resident((3, CONV_WIDTH)),
            _resident((1, CONV_WIDTH)),
            row(HEADS * DV),
            _resident((D_MODEL, D_MODEL)),
            _resident((1, D_MODEL)),
        ],
        out_specs=row(D_MODEL),
        out_shape=jax.ShapeDtypeStruct((T, D_MODEL), F32),
        compiler_params=pltpu.CompilerParams(
            dimension_semantics=("arbitrary",), vmem_limit_bytes=VMEM_LIMIT),
        name="mix_out",
    )(x, bg, u, u, u, conv_w, conv_b, y_mlstm, w_out, gpost)


def _swap_head_pairs(w):
    d = w.shape[0]
    return w.reshape(d, HEADS // 2, 2, DK)[:, :, ::-1, :].reshape(d, HEADS * DK)


def kernel(x, norm_ffn1_pre, norm_ffn1_post, w_ffn1_in, w_ffn1_out, norm_mix_pre, norm_mix_post,
           w_mix_in, conv_w, conv_b, gate_i_bias, gate_f_bias, mlstm_norm, w_mix_out,
           norm_ffn2_pre, norm_ffn2_post, w_ffn2_in, w_ffn2_out):
    batch, seq, _ = x.shape
    T = batch * seq
    depth = norm_ffn1_pre.shape[0]
    tm = 512
    xt = x.reshape(T, D_MODEL)
    HK, HV = HEADS * DK, HEADS * DV
    q0 = 3 * CONV_WIDTH
    k0 = q0 + HK
    v0 = k0 + HK
    o0 = v0 + HV
    g0 = o0 + HV
    for l in range(depth):
        xt = _ffn(xt, norm_ffn1_pre[l][None], norm_ffn1_post[l][None],
                  w_ffn1_in[l].astype(BF16), w_ffn1_out[l].astype(BF16), tm)

        w_in = w_mix_in[l]
        wa = w_in[:, :q0].astype(BF16)
        wqt = w_in[:, q0:k0].T.astype(BF16)
        w_k = w_in[:, k0:v0]
        wk = jnp.concatenate([w_k, _swap_head_pairs(w_k)], axis=1).astype(BF16)
        wvt = w_in[:, v0:o0].T.astype(BF16)
        wo = w_in[:, o0:g0].astype(BF16)
        wgt = w_in[:, g0:].T.astype(BF16)
        brow = jnp.concatenate([gate_i_bias[l], gate_f_bias[l]])[:, None]
        bg, u, qt, k, ksw, vt, o, p, ccol = _mix_in(
            xt, norm_mix_pre[l][None], wa, wqt, wk, wvt, wo, wgt, brow, tm)
        gain_b = jnp.broadcast_to(mlstm_norm[l][:, None], (HV, 128))
        y_mlstm = _mlstm(qt, k, ksw, vt, p, ccol, o, gain_b, batch, seq, MLSTM_ROWS)
        xt = _mix_out(xt, bg, u, conv_w[l], conv_b[l][None], y_mlstm,
                      w_mix_out[l].astype(BF16), norm_mix_post[l][None], seq, tm)

        xt = _ffn(xt, norm_ffn2_pre[l][None], norm_ffn2_post[l][None],
                  w_ffn2_in[l].astype(BF16), w_ffn2_out[l].astype(BF16), tm)
    return xt.reshape(batch, seq, D_MODEL)
```

```python
import functools

import jax
import jax.numpy as jnp
from jax import lax
from jax.experimental import pallas as pl
from jax.experimental.pallas import tpu as pltpu

D_MODEL = 1024
D_FF = 2816
CONV_WIDTH = 512
HEADS = 4
DK = 64
DV = 128
CHUNK = 128
EPS = 1e-6
NEG_INF = -1e30

FF_TILE = 256
MLSTM_ROWS = 512
DVA = DV + 16
NGR = 2 * HEADS
VMEM_LIMIT = 56 * 1024 * 1024

P_E, P_M, P_B, P_G, P_MC = (slice(i * NGR, (i + 1) * NGR) for i in range(5))

F32 = jnp.float32
BF16 = jnp.bfloat16


def _rms(x, g):
    return x * lax.rsqrt(jnp.mean(x * x, axis=-1, keepdims=True) + EPS) * g


def _log_sigmoid(z):
    return jnp.minimum(z, 0.0) - jnp.log1p(jnp.exp(-jnp.abs(z)))


def _resident(shape):
    zeros = (0,) * len(shape)
    return pl.BlockSpec(shape, lambda *_: zeros, pipeline_mode=pl.Buffered(1))


def _split3(x):
    hi = x.astype(BF16)
    r1 = x - hi.astype(F32)
    mid = r1.astype(BF16)
    lo = (r1 - mid.astype(F32)).astype(BF16)
    return hi, mid, lo


def _visible(rev):
    s = lax.broadcasted_iota(jnp.int32, (CHUNK, CHUNK), 0)
    t = lax.broadcasted_iota(jnp.int32, (CHUNK, CHUNK), 1)
    return (s >= t) if rev else (s <= t)


def _running_max(x, rev):
    n = x.shape[1]
    pos = lax.broadcasted_iota(jnp.int32, x.shape, 1) & (CHUNK - 1)
    k = 1
    while k < CHUNK:
        if rev:
            shifted, ok = pltpu.roll(x, n - k, 1), pos < CHUNK - k
        else:
            shifted, ok = pltpu.roll(x, k, 1), pos >= k
        x = jnp.maximum(x, jnp.where(ok, shifted, NEG_INF))
        k *= 2
    return x


def _ffn_kernel(x_ref, gpre_ref, gpost_ref, win_ref, wout_ref, o_ref, h_ref):
    x = x_ref[...]
    xn = _rms(x, gpre_ref[...]).astype(BF16)
    for j in range(D_FF // FF_TILE):
        lo = j * FF_TILE
        gate = jnp.dot(xn, win_ref[:, lo:lo + FF_TILE], preferred_element_type=F32)
        up = jnp.dot(xn, win_ref[:, D_FF + lo:D_FF + lo + FF_TILE], preferred_element_type=F32)
        h_ref[:, lo:lo + FF_TILE] = (gate * jax.nn.sigmoid(gate) * up).astype(BF16)
    y = jnp.dot(h_ref[...], wout_ref[...], preferred_element_type=F32)
    o_ref[...] = x + 0.5 * _rms(y, gpost_ref[...])


def _ffn(x, gpre, gpost, w_in, w_out, tm):
    T = x.shape[0]
    return pl.pallas_call(
        _ffn_kernel,
        grid=(T // tm,),
        in_specs=[
            pl.BlockSpec((tm, D_MODEL), lambda i: (i, 0)),
            _resident((1, D_MODEL)),
            _resident((1, D_MODEL)),
            _resident((D_MODEL, 2 * D_FF)),
            _resident((D_FF, D_MODEL)),
        ],
        out_specs=pl.BlockSpec((tm, D_MODEL), lambda i: (i, 0)),
        out_shape=jax.ShapeDtypeStruct((T, D_MODEL), F32),
        scratch_shapes=[pltpu.VMEM((tm, D_FF), BF16)],
        compiler_params=pltpu.CompilerParams(
            dimension_semantics=("arbitrary",), vmem_limit_bytes=VMEM_LIMIT),
        name="ffn",
    )(x, gpre, gpost, w_in, w_out)


def _gate_rows(zr, p_ref, ccol_ref):
    L = CHUNK
    n_chunks = zr.shape[1] // L
    fwd_row = lax.broadcasted_iota(jnp.int32, (NGR, L), 0) < HEADS
    fwd_col = lax.broadcasted_iota(jnp.int32, (NGR, 1), 0) < HEADS
    li = zr[0:NGR]
    gates = jnp.concatenate([li, _log_sigmoid(zr[NGR:2 * NGR])], axis=0)
    x3 = jnp.concatenate(_split3(gates), axis=0)
    stacked = jnp.concatenate([x3[:, c * L:(c + 1) * L] for c in range(n_chunks)], axis=0)
    cum_ops = jnp.concatenate([_visible(False), _visible(True)], axis=1).astype(BF16)
    cum = jnp.dot(stacked, cum_ops, preferred_element_type=F32)
    b_chunks = []
    for c in range(n_chunks):
        blk = cum[c * 48:(c + 1) * 48]
        s16 = blk[0:16] + blk[16:32] + blk[32:48]
        b_chunks.append(jnp.where(fwd_row, s16[NGR:2 * NGR, 0:L], s16[NGR:2 * NGR, L:2 * L]))
    b = jnp.concatenate(b_chunks, axis=1)
    cc = li - b
    fwd_all = lax.broadcasted_iota(jnp.int32, cc.shape, 0) < HEADS
    p_ref[P_M, :] = jnp.where(fwd_all, _running_max(cc, False), _running_max(cc, True))
    p_ref[P_B, :] = b
    pad = jnp.zeros((L - NGR, L), F32)
    for c in range(n_chunks):
        sl = slice(c * L, (c + 1) * L)
        b_c, cc_c = b_chunks[c], cc[:, sl]
        g = jnp.where(fwd_col, b_c[:, L - 1:L], b_c[:, 0:1])
        m_chunk = g + jnp.max(cc_c, axis=1, keepdims=True)
        p_ref[P_E, sl] = jnp.exp(g + cc_c - m_chunk)
        p_ref[P_G, sl] = jnp.broadcast_to(g, (NGR, L))
        p_ref[P_MC, sl] = jnp.broadcast_to(m_chunk, (NGR, L))
        ccol_ref[sl, :] = jnp.concatenate([cc_c, pad], axis=0).T


def _mix_in_kernel(x_ref, gpre_ref, w_ref, gbias_ref,
                   bg_ref, u_ref, qt_ref, k_ref, ksw_ref, vt_ref, o_ref, p_ref, ccol_ref, wb_ref):
    W, HK, HV = CONV_WIDTH, HEADS * DK, HEADS * DV
    q0 = 3 * W
    k0, v0 = q0 + HK, q0 + 2 * HK
    o0 = v0 + HV
    g0 = o0 + HV

    @pl.when(pl.program_id(0) == 0)
    def _():
        wb_ref[:, 0:g0] = w_ref[:, 0:g0].astype(BF16)
        wb_ref[:, g0:g0 + 128] = jnp.zeros((D_MODEL, 128), BF16)
        wb_ref[:, g0:g0 + 2 * NGR] = w_ref[:, g0:g0 + 2 * NGR].astype(BF16)

    xn = _rms(x_ref[...], gpre_ref[...]).astype(BF16)
    proj = lambda a, b: jnp.dot(xn, wb_ref[:, a:b], preferred_element_type=F32)

    zg = proj(g0, g0 + 128) + gbias_ref[...]
    bg_ref[...] = proj(0, W)
    u_ref[...] = proj(W, 2 * W) * proj(2 * W, 3 * W)
    _gate_rows(zg.T[0:2 * NGR], p_ref, ccol_ref)
    qt_ref[...] = (proj(q0, k0) * (DK ** -0.5)).T.astype(BF16)
    kk = proj(k0, v0)
    k_ref[...] = kk.astype(BF16)
    for pair in range(HEADS // 2):
        ps = slice(pair * 2 * DK, (pair + 1) * 2 * DK)
        ksw_ref[:, ps] = pltpu.roll(kk[:, ps], DK, 1).astype(BF16)
    vt_ref[...] = proj(v0, o0).T.astype(BF16)
    o_ref[...] = proj(o0, g0)


def _mix_in(x, gpre, w, gbias, tm):
    T = x.shape[0]
    row = lambda w: pl.BlockSpec((tm, w), lambda i: (i, 0))
    col = lambda h: pl.BlockSpec((h, tm), lambda i: (0, i))
    HK, HV = HEADS * DK, HEADS * DV
    g0 = 3 * CONV_WIDTH + 2 * HK + 2 * HV
    return pl.pallas_call(
        _mix_in_kernel,
        grid=(T // tm,),
        in_specs=[row(D_MODEL), _resident((1, D_MODEL)), _resident(w.shape), _resident((1, 128))],
        scratch_shapes=[pltpu.VMEM((D_MODEL, g0 + 128), BF16)],
        out_specs=[row(CONV_WIDTH), row(CONV_WIDTH), col(HK), row(HK), row(HK), col(HV), row(HV),
                   col(5 * NGR), row(128)],
        out_shape=[
            jax.ShapeDtypeStruct((T, CONV_WIDTH), F32),
            jax.ShapeDtypeStruct((T, CONV_WIDTH), F32),
            jax.ShapeDtypeStruct((HK, T), BF16),
            jax.ShapeDtypeStruct((T, HK), BF16),
            jax.ShapeDtypeStruct((T, HK), BF16),
            jax.ShapeDtypeStruct((HV, T), BF16),
            jax.ShapeDtypeStruct((T, HV), F32),
            jax.ShapeDtypeStruct((5 * NGR, T), F32),
            jax.ShapeDtypeStruct((T, 128), F32),
        ],
        compiler_params=pltpu.CompilerParams(
            dimension_semantics=("arbitrary",), vmem_limit_bytes=VMEM_LIMIT),
        name="mix_in",
    )(x, gpre, w, gbias)


def _scan_direction(d, vt_ref, k_ref, ksw_ref, p_ref, ct_ref, mp_ref, ct_state, m_state):
    L = CHUNK
    n_chunks = k_ref.shape[0] // L
    lane_half = lax.broadcasted_iota(jnp.int32, (L, 128), 1) // DK
    ones_rows = jnp.ones((DVA - DV, L), BF16)

    cts = [ct_state[d, h] for h in range(HEADS)]
    m_prev = m_state[d]
    for chunk in (range(n_chunks - 1, -1, -1) if d == 1 else range(n_chunks)):
        sl = slice(chunk * L, (chunk + 1) * L)
        e, g, m_chunk = p_ref[P_E, sl], p_ref[P_G, sl], p_ref[P_MC, sl]
        mp_ref[:, sl] = m_prev
        m_new = jnp.maximum(g + m_prev, m_chunk)
        a_old = jnp.exp(g + m_prev - m_new)
        a_new = jnp.exp(m_chunk - m_new)
        for h in range(HEADS):
            r = HEADS * d + h
            vt_aug = jnp.concatenate([vt_ref[h * DV:(h + 1) * DV, sl], ones_rows], axis=0)
            vte = (vt_aug.astype(F32) * e[r:r + 1, :]).astype(BF16)
            src = k_ref if h % 2 == d else ksw_ref
            pair = slice((h // 2) * 2 * DK, (h // 2 + 1) * 2 * DK)
            k_half = jnp.where(lane_half == d, src[sl, pair], jnp.zeros((L, 128), BF16))
            ct_chunk = jnp.dot(vte, k_half, preferred_element_type=F32)
            ct_ref[chunk, h * DVA:(h + 1) * DVA, :] = cts[h].astype(BF16)
            cts[h] = a_old[r:r + 1, 0:1] * cts[h] + a_new[r:r + 1, 0:1] * ct_chunk
        m_prev = m_new
    for h in range(HEADS):
        ct_state[d, h] = cts[h]
    m_state[d] = m_prev


def _mlstm_scan_kernel(vt_f, k_f, ksw_f, p_f, vt_b, k_b, ksw_b, p_b,
                       ct_f, mp_f, ct_b, mp_b, ct_state, m_state):
    @pl.when(pl.program_id(1) == 0)
    def _():
        ct_state[...] = jnp.zeros(ct_state.shape, F32)
        m_state[...] = jnp.full(m_state.shape, NEG_INF, F32)

    _scan_direction(0, vt_f, k_f, ksw_f, p_f, ct_f, mp_f, ct_state, m_state)
    _scan_direction(1, vt_b, k_b, ksw_b, p_b, ct_b, mp_b, ct_state, m_state)


def _mlstm_out_kernel(qt_ref, k_ref, vt_ref, p_ref, mp_f, mp_b, ccol_ref, ct_f, ct_b,
                      o_ref, gain_ref, y_ref):
    L = CHUNK
    n_chunks = k_ref.shape[0] // L
    visible = (_visible(False), _visible(True))
    fwd_row = lax.broadcasted_iota(jnp.int32, (NGR, L), 0) < HEADS
    lane_half = lax.broadcasted_iota(jnp.int32, (DVA, 128), 1) // DK
    ones_rows = jnp.ones((DVA - DV, L), BF16)
    zq = jnp.zeros((DK, L), BF16)
    for chunk in range(n_chunks):
        sl = slice(chunk * L, (chunk + 1) * L)
        m_prev = jnp.where(fwd_row, mp_f[:, sl], mp_b[:, sl])
        n_t = jnp.maximum(m_prev, p_ref[P_M, sl])
        f_inter = jnp.exp(m_prev - n_t)
        e_min = jnp.exp(-(p_ref[P_B, sl] + n_t))
        ccol = ccol_ref[sl, :]
        for h in range(HEADS):
            hs = slice(h * DV, (h + 1) * DV)
            qt = qt_ref[h * DK:(h + 1) * DK, sl]
            k_pair = k_ref[sl, (h // 2) * 2 * DK:(h // 2 + 1) * 2 * DK]
            qt_pair = jnp.concatenate([qt, zq] if h % 2 == 0 else [zq, qt], axis=0)
            st = jnp.dot(k_pair, qt_pair, preferred_element_type=F32)
            pts, qfs = [], []
            for d in range(2):
                r = HEADS * d + h
                arg = jnp.where(visible[d], ccol[:, r:r + 1] - n_t[r:r + 1, :], NEG_INF)
                pts.append((jnp.exp(arg) * st).astype(BF16))
                qfs.append((qt.astype(F32) * f_inter[r:r + 1, :]).astype(BF16))
            rhs = jnp.concatenate([
                jnp.concatenate(pts, axis=1),
                jnp.concatenate([qfs[0], zq], axis=1),
                jnp.concatenate([zq, qfs[1]], axis=1)], axis=0)
            vt_aug = jnp.concatenate([vt_ref[hs, sl], ones_rows], axis=0)
            ct = jnp.where(lane_half == 0, ct_f[chunk, h * DVA:(h + 1) * DVA, :],
                           ct_b[chunk, h * DVA:(h + 1) * DVA, :])
            both = jnp.dot(jnp.concatenate([vt_aug, ct], axis=1), rhs, preferred_element_type=F32)
            ht = None
            for d in range(2):
                r = HEADS * d + h
                numer = both[0:DV, d * L:(d + 1) * L]
                denom = both[DV:DV + 1, d * L:(d + 1) * L]
                part = numer / jnp.maximum(jnp.abs(denom), e_min[r:r + 1, :])
                ht = part if ht is None else ht + part
            ms = jnp.mean(ht * ht, axis=0, keepdims=True)
            hn = ht * lax.rsqrt(ms + EPS) * gain_ref[hs, :]
            y_ref[sl, hs] = (jax.nn.sigmoid(o_ref[sl, hs]) * hn.T).astype(BF16)


def _mlstm(qt, k, ksw, vt, p, ccol, o, gain_b, batch, seq, rows):
    T = batch * seq
    ng = seq // rows
    gc = rows // CHUNK
    HK, HV = HEADS * DK, HEADS * DV

    def scan_specs(group_of):
        blk = lambda b, j: b * ng + group_of(j)
        ins = [pl.BlockSpec((HV, rows), lambda b, j: (0, blk(b, j))),
               pl.BlockSpec((rows, HK), lambda b, j: (blk(b, j), 0)),
               pl.BlockSpec((rows, HK), lambda b, j: (blk(b, j), 0)),
               pl.BlockSpec((5 * NGR, rows), lambda b, j: (0, blk(b, j)))]
        outs = [pl.BlockSpec((gc, HEADS * DVA, 128), lambda b, j: (blk(b, j), 0, 0)),
                pl.BlockSpec((NGR, rows), lambda b, j: (0, blk(b, j)))]
        return ins, outs

    ins_f, outs_f = scan_specs(lambda j: j)
    ins_b, outs_b = scan_specs(lambda j: ng - 1 - j)
    state_shapes = [jax.ShapeDtypeStruct((T // CHUNK, HEADS * DVA, 128), BF16),
                    jax.ShapeDtypeStruct((NGR, T), F32)]
    ct_f, mp_f, ct_b, mp_b = pl.pallas_call(
        _mlstm_scan_kernel,
        grid=(batch, ng),
        in_specs=ins_f + ins_b,
        out_specs=outs_f + outs_b,
        out_shape=state_shapes + state_shapes,
        scratch_shapes=[pltpu.VMEM((2, HEADS, DVA, 128), F32), pltpu.VMEM((2, NGR, 128), F32)],
        compiler_params=pltpu.CompilerParams(dimension_semantics=("arbitrary", "arbitrary"),
                                             vmem_limit_bytes=VMEM_LIMIT),
        name="mlstm_scan",
    )(vt, k, ksw, p, vt, k, ksw, p)

    row = lambda w: pl.BlockSpec((rows, w), lambda i: (i, 0))
    col = lambda h: pl.BlockSpec((h, rows), lambda i: (0, i))
    ctb = pl.BlockSpec((gc, HEADS * DVA, 128), lambda i: (i, 0, 0))
    return pl.pallas_call(
        _mlstm_out_kernel,
        grid=(T // rows,),
        in_specs=[col(HK), row(HK), col(HV), col(5 * NGR), col(NGR), col(NGR), row(128), ctb, ctb,
                  row(HV), _resident((HV, 128))],
        out_specs=row(HV),
        out_shape=jax.ShapeDtypeStruct((T, HV), BF16),
        compiler_params=pltpu.CompilerParams(dimension_semantics=("arbitrary",),
                                             vmem_limit_bytes=VMEM_LIMIT),
        name="mlstm_out",
    )(qt, k, vt, p, mp_f, mp_b, ccol, ct_f, ct_b, o, gain_b)


def _mix_out_kernel(tiles_per_seq, x_ref, bg_ref, u_ref, uprev_ref, unext_ref, cw_ref, cb_ref,
                    y_ref, wout_ref, gpost_ref, o_ref):
    i = pl.program_id(0)
    tm = u_ref.shape[0]
    u = u_ref[...]
    has_prev = (i % tiles_per_seq != 0).astype(F32)
    has_next = (i % tiles_per_seq != tiles_per_seq - 1).astype(F32)
    prev_row = uprev_ref[7:8, :] * has_prev
    next_row = unext_ref[0:1, :] * has_next
    ri = lax.broadcasted_iota(jnp.int32, u.shape, 0)
    u_m1 = jnp.where(ri == 0, prev_row, pltpu.roll(u, 1, 0))
    u_p1 = jnp.where(ri == tm - 1, next_row, pltpu.roll(u, tm - 1, 0))
    conv = cw_ref[0:1, :] * u_m1 + cw_ref[1:2, :] * u + cw_ref[2:3, :] * u_p1
    y_conv = (bg_ref[...] * (conv + cb_ref[...])).astype(BF16)
    h = jnp.dot(y_conv, wout_ref[0:CONV_WIDTH, :], preferred_element_type=F32) \
        + jnp.dot(y_ref[...], wout_ref[CONV_WIDTH:, :], preferred_element_type=F32)
    o_ref[...] = x_ref[...] + _rms(h, gpost_ref[...])


def _mix_out(x, bg, u, conv_w, conv_b, y_mlstm, w_out, gpost, seq, tm):
    T = x.shape[0]
    tiles_per_seq = seq // tm
    sub = tm // 8
    last = T // 8 - 1
    row = lambda w: pl.BlockSpec((tm, w), lambda i: (i, 0))
    return pl.pallas_call(
        functools.partial(_mix_out_kernel, tiles_per_seq),
        grid=(T // tm,),
        in_specs=[
            row(D_MODEL), row(CONV_WIDTH), row(CONV_WIDTH),
            pl.BlockSpec((8, CONV_WIDTH), lambda i: (jnp.maximum(i * sub - 1, 0), 0)),
            pl.BlockSpec((8, CONV_WIDTH), lambda i: (jnp.minimum((i + 1) * sub, last), 0)),
            _resident((3, CONV_WIDTH)),
            _resident((1, CONV_WIDTH)),
            row(HEADS * DV),
            _resident((D_MODEL, D_MODEL)),
            _resident((1, D_MODEL)),
        ],
        out_specs=row(D_MODEL),
        out_shape=jax.ShapeDtypeStruct((T, D_MODEL), F32),
        compiler_params=pltpu.CompilerParams(
            dimension_semantics=("arbitrary",), vmem_limit_bytes=VMEM_LIMIT),
        name="mix_out",
    )(x, bg, u, u, u, conv_w, conv_b, y_mlstm, w_out, gpost)


def kernel(x, norm_ffn1_pre, norm_ffn1_post, w_ffn1_in, w_ffn1_out, norm_mix_pre, norm_mix_post,
           w_mix_in, conv_w, conv_b, gate_i_bias, gate_f_bias, mlstm_norm, w_mix_out,
           norm_ffn2_pre, norm_ffn2_post, w_ffn2_in, w_ffn2_out):
    batch, seq, _ = x.shape
    T = batch * seq
    depth = norm_ffn1_pre.shape[0]
    tm = 512
    xt = x.reshape(T, D_MODEL)
    HV = HEADS * DV
    for l in range(depth):
        xt = _ffn(xt, norm_ffn1_pre[l][None], norm_ffn1_post[l][None],
                  w_ffn1_in[l].astype(BF16), w_ffn1_out[l].astype(BF16), tm)

        gbias = jnp.pad(jnp.concatenate([gate_i_bias[l], gate_f_bias[l]]), (0, 128 - 2 * NGR))[None]
        bg, u, qt, k, ksw, vt, o, p, ccol = _mix_in(xt, norm_mix_pre[l][None], w_mix_in[l], gbias, tm)
        gain_b = jnp.broadcast_to(mlstm_norm[l][:, None], (HV, 128))
        y_mlstm = _mlstm(qt, k, ksw, vt, p, ccol, o, gain_b, batch, seq, MLSTM_ROWS)
        xt = _mix_out(xt, bg, u, conv_w[l], conv_b[l][None], y_mlstm,
                      w_mix_out[l].astype(BF16), norm_mix_post[l][None], seq, tm)

        xt = _ffn(xt, norm_ffn2_pre[l][None], norm_ffn2_post[l][None],
                  w_ffn2_in[l].astype(BF16), w_ffn2_out[l].astype(BF16), tm)
    return xt.reshape(batch, seq, D_MODEL)
```

```python
import functools

import jax
import jax.numpy as jnp
from jax import lax
from jax.experimental import pallas as pl
from jax.experimental.pallas import tpu as pltpu

D_MODEL = 1024
D_FF = 2816
CONV_WIDTH = 512
HEADS = 4
DK = 64
DV = 128
CHUNK = 128
EPS = 1e-6
NEG_INF = -1e30

FF_TILE = 256
FF_STAGE_CHUNKS = 8
MLSTM_ROWS = 512
DVA = DV + 16
NGR = 2 * HEADS
VMEM_LIMIT = 56 * 1024 * 1024

P_E, P_M, P_B, P_G, P_MC = (slice(i * NGR, (i + 1) * NGR) for i in range(5))

F32 = jnp.float32
BF16 = jnp.bfloat16


def _rms(x, g):
    return x * lax.rsqrt(jnp.mean(x * x, axis=-1, keepdims=True) + EPS) * g


def _log_sigmoid(z):
    return jnp.minimum(z, 0.0) - jnp.log1p(jnp.exp(-jnp.abs(z)))


def _resident(shape):
    zeros = (0,) * len(shape)
    return pl.BlockSpec(shape, lambda *_: zeros, pipeline_mode=pl.Buffered(1))


def _split3(x):
    hi = x.astype(BF16)
    r1 = x - hi.astype(F32)
    mid = r1.astype(BF16)
    lo = (r1 - mid.astype(F32)).astype(BF16)
    return hi, mid, lo


def _visible(rev):
    s = lax.broadcasted_iota(jnp.int32, (CHUNK, CHUNK), 0)
    t = lax.broadcasted_iota(jnp.int32, (CHUNK, CHUNK), 1)
    return (s >= t) if rev else (s <= t)


def _running_max(x, rev):
    n = x.shape[1]
    pos = lax.broadcasted_iota(jnp.int32, x.shape, 1) & (CHUNK - 1)
    k = 1
    while k < CHUNK:
        if rev:
            shifted, ok = pltpu.roll(x, n - k, 1), pos < CHUNK - k
        else:
            shifted, ok = pltpu.roll(x, k, 1), pos >= k
        x = jnp.maximum(x, jnp.where(ok, shifted, NEG_INF))
        k *= 2
    return x


def _stage_bf16(src_hbm, dst_ref, stage_ref, sem):
    rows = stage_ref.shape[1]
    n_chunks = src_hbm.shape[0] // rows

    def copy(c):
        return pltpu.make_async_copy(src_hbm.at[pl.ds(c * rows, rows), :], stage_ref.at[c % 2], sem.at[c % 2])

    copy(0).start()
    for c in range(n_chunks):
        if c + 1 < n_chunks:
            copy(c + 1).start()
        copy(c).wait()
        dst_ref[c * rows:(c + 1) * rows, :] = stage_ref[c % 2].astype(BF16)


def _ffn_kernel(layer, x_ref, gpre_ref, gpost_ref, win_hbm, wout_hbm, o_ref,
                h_ref, win_ref, wout_ref, stage_in, stage_out, sem):
    @pl.when(pl.program_id(0) == 0)
    def _():
        _stage_bf16(win_hbm.at[layer], win_ref, stage_in, sem)
        _stage_bf16(wout_hbm.at[layer], wout_ref, stage_out, sem)

    x = x_ref[...]
    xn = _rms(x, gpre_ref[...]).astype(BF16)
    for j in range(D_FF // FF_TILE):
        lo = j * FF_TILE
        gate = jnp.dot(xn, win_ref[:, lo:lo + FF_TILE], preferred_element_type=F32)
        up = jnp.dot(xn, win_ref[:, D_FF + lo:D_FF + lo + FF_TILE], preferred_element_type=F32)
        h_ref[:, lo:lo + FF_TILE] = (gate * jax.nn.sigmoid(gate) * up).astype(BF16)
    y = jnp.dot(h_ref[...], wout_ref[...], preferred_element_type=F32)
    o_ref[...] = x + 0.5 * _rms(y, gpost_ref[...])


def _ffn(x, gpre, gpost, w_in, w_out, layer, tm):
    T = x.shape[0]
    return pl.pallas_call(
        functools.partial(_ffn_kernel, layer),
        grid=(T // tm,),
        in_specs=[
            pl.BlockSpec((tm, D_MODEL), lambda i: (i, 0)),
            _resident((1, D_MODEL)),
            _resident((1, D_MODEL)),
            pl.BlockSpec(memory_space=pl.ANY),
            pl.BlockSpec(memory_space=pl.ANY),
        ],
        out_specs=pl.BlockSpec((tm, D_MODEL), lambda i: (i, 0)),
        out_shape=jax.ShapeDtypeStruct((T, D_MODEL), F32),
        scratch_shapes=[
            pltpu.VMEM((tm, D_FF), BF16),
            pltpu.VMEM((D_MODEL, 2 * D_FF), BF16),
            pltpu.VMEM((D_FF, D_MODEL), BF16),
            pltpu.VMEM((2, D_MODEL // FF_STAGE_CHUNKS, 2 * D_FF), F32),
            pltpu.VMEM((2, D_FF // FF_STAGE_CHUNKS, D_MODEL), F32),
            pltpu.SemaphoreType.DMA((2,)),
        ],
        compiler_params=pltpu.CompilerParams(
            dimension_semantics=("arbitrary",), vmem_limit_bytes=VMEM_LIMIT),
        name="ffn",
    )(x, gpre, gpost, w_in, w_out)


def _gate_rows(zr, p_ref, ccol_ref):
    L = CHUNK
    n_chunks = zr.shape[1] // L
    fwd_row = lax.broadcasted_iota(jnp.int32, (NGR, L), 0) < HEADS
    fwd_col = lax.broadcasted_iota(jnp.int32, (NGR, 1), 0) < HEADS
    li = zr[0:NGR]
    gates = jnp.concatenate([li, _log_sigmoid(zr[NGR:2 * NGR])], axis=0)
    x3 = jnp.concatenate(_split3(gates), axis=0)
    stacked = jnp.concatenate([x3[:, c * L:(c + 1) * L] for c in range(n_chunks)], axis=0)
    cum_ops = jnp.concatenate([_visible(False), _visible(True)], axis=1).astype(BF16)
    cum = jnp.dot(stacked, cum_ops, preferred_element_type=F32)
    b_chunks = []
    for c in range(n_chunks):
        blk = cum[c * 48:(c + 1) * 48]
        s16 = blk[0:16] + blk[16:32] + blk[32:48]
        b_chunks.append(jnp.where(fwd_row, s16[NGR:2 * NGR, 0:L], s16[NGR:2 * NGR, L:2 * L]))
    b = jnp.concatenate(b_chunks, axis=1)
    cc = li - b
    fwd_all = lax.broadcasted_iota(jnp.int32, cc.shape, 0) < HEADS
    p_ref[P_M, :] = jnp.where(fwd_all, _running_max(cc, False), _running_max(cc, True))
    p_ref[P_B, :] = b
    pad = jnp.zeros((L - NGR, L), F32)
    for c in range(n_chunks):
        sl = slice(c * L, (c + 1) * L)
        b_c, cc_c = b_chunks[c], cc[:, sl]
        g = jnp.where(fwd_col, b_c[:, L - 1:L], b_c[:, 0:1])
        m_chunk = g + jnp.max(cc_c, axis=1, keepdims=True)
        p_ref[P_E, sl] = jnp.exp(g + cc_c - m_chunk)
        p_ref[P_G, sl] = jnp.broadcast_to(g, (NGR, L))
        p_ref[P_MC, sl] = jnp.broadcast_to(m_chunk, (NGR, L))
        ccol_ref[sl, :] = jnp.concatenate([cc_c, pad], axis=0).T


def _mix_in_kernel(x_ref, gpre_ref, w_ref, gbias_ref,
                   bg_ref, u_ref, qt_ref, k_ref, ksw_ref, vt_ref, o_ref, p_ref, ccol_ref, wb_ref):
    W, HK, HV = CONV_WIDTH, HEADS * DK, HEADS * DV
    q0 = 3 * W
    k0, v0 = q0 + HK, q0 + 2 * HK
    o0 = v0 + HV
    g0 = o0 + HV

    @pl.when(pl.program_id(0) == 0)
    def _():
        wb_ref[:, 0:g0] = w_ref[:, 0:g0].astype(BF16)
        wb_ref[:, g0:g0 + 128] = jnp.zeros((D_MODEL, 128), BF16)
        wb_ref[:, g0:g0 + 2 * NGR] = w_ref[:, g0:g0 + 2 * NGR].astype(BF16)

    xn = _rms(x_ref[...], gpre_ref[...]).astype(BF16)
    proj = lambda a, b: jnp.dot(xn, wb_ref[:, a:b], preferred_element_type=F32)

    zg = proj(g0, g0 + 128) + gbias_ref[...]
    bg_ref[...] = proj(0, W)
    u_ref[...] = proj(W, 2 * W) * proj(2 * W, 3 * W)
    _gate_rows(zg.T[0:2 * NGR], p_ref, ccol_ref)
    qt_ref[...] = (proj(q0, k0) * (DK ** -0.5)).T.astype(BF16)
    kk = proj(k0, v0)
    k_ref[...] = kk.astype(BF16)
    for pair in range(HEADS // 2):
        ps = slice(pair * 2 * DK, (pair + 1) * 2 * DK)
        ksw_ref[:, ps] = pltpu.roll(kk[:, ps], DK, 1).astype(BF16)
    vt_ref[...] = proj(v0, o0).T.astype(BF16)
    o_ref[...] = proj(o0, g0)


def _mix_in(x, gpre, w, gbias, layer, tm):
    T = x.shape[0]
    row = lambda w: pl.BlockSpec((tm, w), lambda i: (i, 0))
    col = lambda h: pl.BlockSpec((h, tm), lambda i: (0, i))
    HK, HV = HEADS * DK, HEADS * DV
    g0 = 3 * CONV_WIDTH + 2 * HK + 2 * HV
    return pl.pallas_call(
        _mix_in_kernel,
        grid=(T // tm,),
        in_specs=[row(D_MODEL), _resident((1, D_MODEL)),
                  pl.BlockSpec((None,) + w.shape[1:], lambda i: (layer, 0, 0), pipeline_mode=pl.Buffered(1)),
                  _resident((1, 128))],
        scratch_shapes=[pltpu.VMEM((D_MODEL, g0 + 128), BF16)],
        out_specs=[row(CONV_WIDTH), row(CONV_WIDTH), col(HK), row(HK), row(HK), col(HV), row(HV),
                   col(5 * NGR), row(128)],
        out_shape=[
            jax.ShapeDtypeStruct((T, CONV_WIDTH), F32),
            jax.ShapeDtypeStruct((T, CONV_WIDTH), F32),
            jax.ShapeDtypeStruct((HK, T), BF16),
            jax.ShapeDtypeStruct((T, HK), BF16),
            jax.ShapeDtypeStruct((T, HK), BF16),
            jax.ShapeDtypeStruct((HV, T), BF16),
            jax.ShapeDtypeStruct((T, HV), F32),
            jax.ShapeDtypeStruct((5 * NGR, T), F32),
            jax.ShapeDtypeStruct((T, 128), F32),
        ],
        compiler_params=pltpu.CompilerParams(
            dimension_semantics=("arbitrary",), vmem_limit_bytes=VMEM_LIMIT),
        name="mix_in",
    )(x, gpre, w, gbias)


def _scan_direction(d, vt_ref, k_ref, ksw_ref, p_ref, ct_ref, mp_ref, ct_state, m_state):
    L = CHUNK
    n_chunks = k_ref.shape[0] // L
    lane_half = lax.broadcasted_iota(jnp.int32, (L, 128), 1) // DK
    ones_rows = jnp.ones((DVA - DV, L), BF16)

    cts = [ct_state[d, h] for h in range(HEADS)]
    m_prev = m_state[d]
    for chunk in (range(n_chunks - 1, -1, -1) if d == 1 else range(n_chunks)):
        sl = slice(chunk * L, (chunk + 1) * L)
        e, g, m_chunk = p_ref[P_E, sl], p_ref[P_G, sl], p_ref[P_MC, sl]
        mp_ref[:, sl] = m_prev
        m_new = jnp.maximum(g + m_prev, m_chunk)
        a_old = jnp.exp(g + m_prev - m_new)
        a_new = jnp.exp(m_chunk - m_new)
        for h in range(HEADS):
            r = HEADS * d + h
            vt_aug = jnp.concatenate([vt_ref[h * DV:(h + 1) * DV, sl], ones_rows], axis=0)
            vte = (vt_aug.astype(F32) * e[r:r + 1, :]).astype(BF16)
            src = k_ref if h % 2 == d else ksw_ref
            pair = slice((h // 2) * 2 * DK, (h // 2 + 1) * 2 * DK)
            k_half = jnp.where(lane_half == d, src[sl, pair], jnp.zeros((L, 128), BF16))
            ct_chunk = jnp.dot(vte, k_half, preferred_element_type=F32)
            ct_ref[chunk, h * DVA:(h + 1) * DVA, :] = cts[h].astype(BF16)
            cts[h] = a_old[r:r + 1, 0:1] * cts[h] + a_new[r:r + 1, 0:1] * ct_chunk
        m_prev = m_new
    for h in range(HEADS):
        ct_state[d, h] = cts[h]
    m_state[d] = m_prev


def _mlstm_scan_kernel(vt_f, k_f, ksw_f, p_f, vt_b, k_b, ksw_b, p_b,
                       ct_f, mp_f, ct_b, mp_b, ct_state, m_state):
    @pl.when(pl.program_id(1) == 0)
    def _():
        ct_state[...] = jnp.zeros(ct_state.shape, F32)
        m_state[...] = jnp.full(m_state.shape, NEG_INF, F32)

    _scan_direction(0, vt_f, k_f, ksw_f, p_f, ct_f, mp_f, ct_state, m_state)
    _scan_direction(1, vt_b, k_b, ksw_b, p_b, ct_b, mp_b, ct_state, m_state)


def _mlstm_out_kernel(qt_ref, k_ref, vt_ref, p_ref, mp_f, mp_b, ccol_ref, ct_f, ct_b,
                      o_ref, gain_ref, y_ref):
    L = CHUNK
    n_chunks = k_ref.shape[0] // L
    visible = (_visible(False), _visible(True))
    fwd_row = lax.broadcasted_iota(jnp.int32, (NGR, L), 0) < HEADS
    lane_half = lax.broadcasted_iota(jnp.int32, (DVA, 128), 1) // DK
    ones_rows = jnp.ones((DVA - DV, L), BF16)
    zq = jnp.zeros((DK, L), BF16)
    for chunk in range(n_chunks):
        sl = slice(chunk * L, (chunk + 1) * L)
        m_prev = jnp.where(fwd_row, mp_f[:, sl], mp_b[:, sl])
        n_t = jnp.maximum(m_prev, p_ref[P_M, sl])
        f_inter = jnp.exp(m_prev - n_t)
        e_min = jnp.exp(-(p_ref[P_B, sl] + n_t))
        ccol = ccol_ref[sl, :]
        for h in range(HEADS):
            hs = slice(h * DV, (h + 1) * DV)
            qt = qt_ref[h * DK:(h + 1) * DK, sl]
            k_pair = k_ref[sl, (h // 2) * 2 * DK:(h // 2 + 1) * 2 * DK]
            qt_pair = jnp.concatenate([qt, zq] if h % 2 == 0 else [zq, qt], axis=0)
            st = jnp.dot(k_pair, qt_pair, preferred_element_type=F32)
            pts, qfs = [], []
            for d in range(2):
                r = HEADS * d + h
                arg = jnp.where(visible[d], ccol[:, r:r + 1] - n_t[r:r + 1, :], NEG_INF)
                pts.append((jnp.exp(arg) * st).astype(BF16))
                qfs.append((qt.astype(F32) * f_inter[r:r + 1, :]).astype(BF16))
            rhs = jnp.concatenate([
                jnp.concatenate(pts, axis=1),
                jnp.concatenate([qfs[0], zq], axis=1),
                jnp.concatenate([zq, qfs[1]], axis=1)], axis=0)
            vt_aug = jnp.concatenate([vt_ref[hs, sl], ones_rows], axis=0)
            ct = jnp.where(lane_half == 0, ct_f[chunk, h * DVA:(h + 1) * DVA, :],
                           ct_b[chunk, h * DVA:(h + 1) * DVA, :])
            both = jnp.dot(jnp.concatenate([vt_aug, ct], axis=1), rhs, preferred_element_type=F32)
            ht = None
            for d in range(2):
                r = HEADS * d + h
                numer = both[0:DV, d * L:(d + 1) * L]
                denom = both[DV:DV + 1, d * L:(d + 1) * L]
                part = numer / jnp.maximum(jnp.abs(denom), e_min[r:r + 1, :])
                ht = part if ht is None else ht + part
            ms = jnp.mean(ht * ht, axis=0, keepdims=True)
            hn = ht * lax.rsqrt(ms + EPS) * gain_ref[hs, :]
            y_ref[sl, hs] = (jax.nn.sigmoid(o_ref[sl, hs]) * hn.T).astype(BF16)


def _mlstm(qt, k, ksw, vt, p, ccol, o, gain_b, batch, seq, rows):
    T = batch * seq
    ng = seq // rows
    gc = rows // CHUNK
    HK, HV = HEADS * DK, HEADS * DV

    def scan_specs(group_of):
        blk = lambda b, j: b * ng + group_of(j)
        ins = [pl.BlockSpec((HV, rows), lambda b, j: (0, blk(b, j))),
               pl.BlockSpec((rows, HK), lambda b, j: (blk(b, j), 0)),
               pl.BlockSpec((rows, HK), lambda b, j: (blk(b, j), 0)),
               pl.BlockSpec((5 * NGR, rows), lambda b, j: (0, blk(b, j)))]
        outs = [pl.BlockSpec((gc, HEADS * DVA, 128), lambda b, j: (blk(b, j), 0, 0)),
                pl.BlockSpec((NGR, rows), lambda b, j: (0, blk(b, j)))]
        return ins, outs

    ins_f, outs_f = scan_specs(lambda j: j)
    ins_b, outs_b = scan_specs(lambda j: ng - 1 - j)
    state_shapes = [jax.ShapeDtypeStruct((T // CHUNK, HEADS * DVA, 128), BF16),
                    jax.ShapeDtypeStruct((NGR, T), F32)]
    ct_f, mp_f, ct_b, mp_b = pl.pallas_call(
        _mlstm_scan_kernel,
        grid=(batch, ng),
        in_specs=ins_f + ins_b,
        out_specs=outs_f + outs_b,
        out_shape=state_shapes + state_shapes,
        scratch_shapes=[pltpu.VMEM((2, HEADS, DVA, 128), F32), pltpu.VMEM((2, NGR, 128), F32)],
        compiler_params=pltpu.CompilerParams(dimension_semantics=("arbitrary", "arbitrary"),
                                             vmem_limit_bytes=VMEM_LIMIT),
        name="mlstm_scan",
    )(vt, k, ksw, p, vt, k, ksw, p)

    row = lambda w: pl.BlockSpec((rows, w), lambda i: (i, 0))
    col = lambda h: pl.BlockSpec((h, rows), lambda i: (0, i))
    ctb = pl.BlockSpec((gc, HEADS * DVA, 128), lambda i: (i, 0, 0))
    return pl.pallas_call(
        _mlstm_out_kernel,
        grid=(T // rows,),
        in_specs=[col(HK), row(HK), col(HV), col(5 * NGR), col(NGR), col(NGR), row(128), ctb, ctb,
                  row(HV), _resident((HV, 128))],
        out_specs=row(HV),
        out_shape=jax.ShapeDtypeStruct((T, HV), BF16),
        compiler_params=pltpu.CompilerParams(dimension_semantics=("arbitrary",),
                                             vmem_limit_bytes=VMEM_LIMIT),
        name="mlstm_out",
    )(qt, k, vt, p, mp_f, mp_b, ccol, ct_f, ct_b, o, gain_b)


def _mix_out_kernel(tiles_per_seq, x_ref, bg_ref, u_ref, uprev_ref, unext_ref, cw_ref, cb_ref,
                    y_ref, wout_ref, gpost_ref, o_ref):
    i = pl.program_id(0)
    tm = u_ref.shape[0]
    u = u_ref[...]
    has_prev = (i % tiles_per_seq != 0).astype(F32)
    has_next = (i % tiles_per_seq != tiles_per_seq - 1).astype(F32)
    prev_row = uprev_ref[7:8, :] * has_prev
    next_row = unext_ref[0:1, :] * has_next
    ri = lax.broadcasted_iota(jnp.int32, u.shape, 0)
    u_m1 = jnp.where(ri == 0, prev_row, pltpu.roll(u, 1, 0))
    u_p1 = jnp.where(ri == tm - 1, next_row, pltpu.roll(u, tm - 1, 0))
    conv = cw_ref[0:1, :] * u_m1 + cw_ref[1:2, :] * u + cw_ref[2:3, :] * u_p1
    y_conv = (bg_ref[...] * (conv + cb_ref[...])).astype(BF16)
    h = jnp.dot(y_conv, wout_ref[0:CONV_WIDTH, :], preferred_element_type=F32) \
        + jnp.dot(y_ref[...], wout_ref[CONV_WIDTH:, :], preferred_element_type=F32)
    o_ref[...] = x_ref[...] + _rms(h, gpost_ref[...])


def _mix_out(x, bg, u, conv_w, conv_b, y_mlstm, w_out, gpost, seq, tm):
    T = x.shape[0]
    tiles_per_seq = seq // tm
    sub = tm // 8
    last = T // 8 - 1
    row = lambda w: pl.BlockSpec((tm, w), lambda i: (i, 0))
    return pl.pallas_call(
        functools.partial(_mix_out_kernel, tiles_per_seq),
        grid=(T // tm,),
        in_specs=[
            row(D_MODEL), row(CONV_WIDTH), row(CONV_WIDTH),
            pl.BlockSpec((8, CONV_WIDTH), lambda i: (jnp.maximum(i * sub - 1, 0), 0)),
            pl.BlockSpec((8, CONV_WIDTH), lambda i: (jnp.minimum((i + 1) * sub, last), 0)),
            _resident((3, CONV_WIDTH)),
            _resident((1, CONV_WIDTH)),
            row(HEADS * DV),
            _resident((D_MODEL, D_MODEL)),
            _resident((1, D_MODEL)),
        ],
        out_specs=row(D_MODEL),
        out_shape=jax.ShapeDtypeStruct((T, D_MODEL), F32),
        compiler_params=pltpu.CompilerParams(
            dimension_semantics=("arbitrary",), vmem_limit_bytes=VMEM_LIMIT),
        name="mix_out",
    )(x, bg, u, u, u, conv_w, conv_b, y_mlstm, w_out, gpost)


def kernel(x, norm_ffn1_pre, norm_ffn1_post, w_ffn1_in, w_ffn1_out, norm_mix_pre, norm_mix_post,
           w_mix_in, conv_w, conv_b, gate_i_bias, gate_f_bias, mlstm_norm, w_mix_out,
           norm_ffn2_pre, norm_ffn2_post, w_ffn2_in, w_ffn2_out):
    batch, seq, _ = x.shape
    T = batch * seq
    depth = norm_ffn1_pre.shape[0]
    tm = 512
    xt = x.reshape(T, D_MODEL)
    HV = HEADS * DV
    for l in range(depth):
        xt = _ffn(xt, norm_ffn1_pre[l][None], norm_ffn1_post[l][None], w_ffn1_in, w_ffn1_out, l, tm)

        gbias = jnp.pad(jnp.concatenate([gate_i_bias[l], gate_f_bias[l]]), (0, 128 - 2 * NGR))[None]
        bg, u, qt, k, ksw, vt, o, p, ccol = _mix_in(xt, norm_mix_pre[l][None], w_mix_in, gbias, l, tm)
        gain_b = jnp.broadcast_to(mlstm_norm[l][:, None], (HV, 128))
        y_mlstm = _mlstm(qt, k, ksw, vt, p, ccol, o, gain_b, batch, seq, MLSTM_ROWS)
        xt = _mix_out(xt, bg, u, conv_w[l], conv_b[l][None], y_mlstm,
                      w_mix_out[l].astype(BF16), norm_mix_post[l][None], seq, tm)

        xt = _ffn(xt, norm_ffn2_pre[l][None], norm_ffn2_post[l][None], w_ffn2_in, w_ffn2_out, l, tm)
    return xt.reshape(batch, seq, D_MODEL)
```

```python
import functools

import jax
import jax.numpy as jnp
from jax import lax
from jax.experimental import pallas as pl
from jax.experimental.pallas import tpu as pltpu

D_MODEL = 1024
D_FF = 2816
CONV_WIDTH = 512
HEADS = 4
DK = 64
DV = 128
CHUNK = 128
EPS = 1e-6
NEG_INF = -1e30

FF_TILE = 256
FF_STAGE_CHUNKS = 8
MIX_STAGE_CHUNKS = 8
MLSTM_ROWS = 512
DVA = DV + 16
NGR = 2 * HEADS
VMEM_LIMIT = 56 * 1024 * 1024

P_E, P_M, P_B, P_G, P_MC = (slice(i * NGR, (i + 1) * NGR) for i in range(5))

F32 = jnp.float32
BF16 = jnp.bfloat16


def _rms(x, g):
    return x * lax.rsqrt(jnp.mean(x * x, axis=-1, keepdims=True) + EPS) * g


def _log_sigmoid(z):
    return jnp.minimum(z, 0.0) - jnp.log1p(jnp.exp(-jnp.abs(z)))


def _resident(shape):
    zeros = (0,) * len(shape)
    return pl.BlockSpec(shape, lambda *_: zeros, pipeline_mode=pl.Buffered(1))


def _split3(x):
    hi = x.astype(BF16)
    r1 = x - hi.astype(F32)
    mid = r1.astype(BF16)
    lo = (r1 - mid.astype(F32)).astype(BF16)
    return hi, mid, lo


def _visible(rev):
    s = lax.broadcasted_iota(jnp.int32, (CHUNK, CHUNK), 0)
    t = lax.broadcasted_iota(jnp.int32, (CHUNK, CHUNK), 1)
    return (s >= t) if rev else (s <= t)


def _running_max(x, rev):
    n = x.shape[1]
    pos = lax.broadcasted_iota(jnp.int32, x.shape, 1) & (CHUNK - 1)
    k = 1
    while k < CHUNK:
        if rev:
            shifted, ok = pltpu.roll(x, n - k, 1), pos < CHUNK - k
        else:
            shifted, ok = pltpu.roll(x, k, 1), pos >= k
        x = jnp.maximum(x, jnp.where(ok, shifted, NEG_INF))
        k *= 2
    return x


def _stage_bf16(src_hbm, dst_ref, stage_ref, sem):
    rows = stage_ref.shape[1]
    n_chunks = src_hbm.shape[0] // rows

    def copy(c):
        return pltpu.make_async_copy(src_hbm.at[pl.ds(c * rows, rows), :], stage_ref.at[c % 2], sem.at[c % 2])

    copy(0).start()
    for c in range(n_chunks):
        if c + 1 < n_chunks:
            copy(c + 1).start()
        copy(c).wait()
        dst_ref[c * rows:(c + 1) * rows, :] = stage_ref[c % 2].astype(BF16)


def _ffn_kernel(layer, x_ref, gpre_ref, gpost_ref, win_hbm, wout_hbm, o_ref,
                h_ref, win_ref, wout_ref, stage_in, stage_out, sem):
    @pl.when(pl.program_id(0) == 0)
    def _():
        _stage_bf16(win_hbm.at[layer], win_ref, stage_in, sem)
        _stage_bf16(wout_hbm.at[layer], wout_ref, stage_out, sem)

    x = x_ref[...]
    xn = _rms(x, gpre_ref[...]).astype(BF16)
    for j in range(D_FF // FF_TILE):
        lo = j * FF_TILE
        gate = jnp.dot(xn, win_ref[:, lo:lo + FF_TILE], preferred_element_type=F32)
        up = jnp.dot(xn, win_ref[:, D_FF + lo:D_FF + lo + FF_TILE], preferred_element_type=F32)
        h_ref[:, lo:lo + FF_TILE] = (gate * jax.nn.sigmoid(gate) * up).astype(BF16)
    y = jnp.dot(h_ref[...], wout_ref[...], preferred_element_type=F32)
    o_ref[...] = x + 0.5 * _rms(y, gpost_ref[...])


def _ffn(x, gpre, gpost, w_in, w_out, layer, tm):
    T = x.shape[0]
    return pl.pallas_call(
        functools.partial(_ffn_kernel, layer),
        grid=(T // tm,),
        in_specs=[
            pl.BlockSpec((tm, D_MODEL), lambda i: (i, 0)),
            _resident((1, D_MODEL)),
            _resident((1, D_MODEL)),
            pl.BlockSpec(memory_space=pl.ANY),
            pl.BlockSpec(memory_space=pl.ANY),
        ],
        out_specs=pl.BlockSpec((tm, D_MODEL), lambda i: (i, 0)),
        out_shape=jax.ShapeDtypeStruct((T, D_MODEL), F32),
        scratch_shapes=[
            pltpu.VMEM((tm, D_FF), BF16),
            pltpu.VMEM((D_MODEL, 2 * D_FF), BF16),
            pltpu.VMEM((D_FF, D_MODEL), BF16),
            pltpu.VMEM((2, D_MODEL // FF_STAGE_CHUNKS, 2 * D_FF), F32),
            pltpu.VMEM((2, D_FF // FF_STAGE_CHUNKS, D_MODEL), F32),
            pltpu.SemaphoreType.DMA((2,)),
        ],
        compiler_params=pltpu.CompilerParams(
            dimension_semantics=("arbitrary",), vmem_limit_bytes=VMEM_LIMIT),
        name="ffn",
    )(x, gpre, gpost, w_in, w_out)


def _gate_rows(zr, p_ref, ccol_ref):
    L = CHUNK
    n_chunks = zr.shape[1] // L
    fwd_row = lax.broadcasted_iota(jnp.int32, (NGR, L), 0) < HEADS
    fwd_col = lax.broadcasted_iota(jnp.int32, (NGR, 1), 0) < HEADS
    li = zr[0:NGR]
    gates = jnp.concatenate([li, _log_sigmoid(zr[NGR:2 * NGR])], axis=0)
    x3 = jnp.concatenate(_split3(gates), axis=0)
    stacked = jnp.concatenate([x3[:, c * L:(c + 1) * L] for c in range(n_chunks)], axis=0)
    cum_ops = jnp.concatenate([_visible(False), _visible(True)], axis=1).astype(BF16)
    cum = jnp.dot(stacked, cum_ops, preferred_element_type=F32)
    b_chunks = []
    for c in range(n_chunks):
        blk = cum[c * 48:(c + 1) * 48]
        s16 = blk[0:16] + blk[16:32] + blk[32:48]
        b_chunks.append(jnp.where(fwd_row, s16[NGR:2 * NGR, 0:L], s16[NGR:2 * NGR, L:2 * L]))
    b = jnp.concatenate(b_chunks, axis=1)
    cc = li - b
    fwd_all = lax.broadcasted_iota(jnp.int32, cc.shape, 0) < HEADS
    p_ref[P_M, :] = jnp.where(fwd_all, _running_max(cc, False), _running_max(cc, True))
    p_ref[P_B, :] = b
    pad = jnp.zeros((L - NGR, L), F32)
    for c in range(n_chunks):
        sl = slice(c * L, (c + 1) * L)
        b_c, cc_c = b_chunks[c], cc[:, sl]
        g = jnp.where(fwd_col, b_c[:, L - 1:L], b_c[:, 0:1])
        m_chunk = g + jnp.max(cc_c, axis=1, keepdims=True)
        p_ref[P_E, sl] = jnp.exp(g + cc_c - m_chunk)
        p_ref[P_G, sl] = jnp.broadcast_to(g, (NGR, L))
        p_ref[P_MC, sl] = jnp.broadcast_to(m_chunk, (NGR, L))
        ccol_ref[sl, :] = jnp.concatenate([cc_c, pad], axis=0).T


def _mix_in_kernel(layer, x_ref, gpre_ref, w_hbm, gbias_ref,
                   bg_ref, u_ref, qt_ref, k_ref, ksw_ref, vt_ref, o_ref, p_ref, ccol_ref,
                   wb_ref, stage_ref, sem):
    W, HK, HV = CONV_WIDTH, HEADS * DK, HEADS * DV
    q0 = 3 * W
    k0, v0 = q0 + HK, q0 + 2 * HK
    o0 = v0 + HV
    g0 = o0 + HV

    @pl.when(pl.program_id(0) == 0)
    def _():
        src = w_hbm.at[layer]
        rows = stage_ref.shape[1]
        n_chunks = D_MODEL // rows

        def copy(c):
            return pltpu.make_async_copy(src.at[pl.ds(c * rows, rows), :], stage_ref.at[c % 2], sem.at[c % 2])

        wb_ref[:, g0:g0 + 128] = jnp.zeros((D_MODEL, 128), BF16)
        copy(0).start()
        for c in range(n_chunks):
            if c + 1 < n_chunks:
                copy(c + 1).start()
            copy(c).wait()
            rs = slice(c * rows, (c + 1) * rows)
            wb_ref[rs, 0:g0] = stage_ref[c % 2, :, 0:g0].astype(BF16)
            wb_ref[rs, g0:g0 + 2 * NGR] = stage_ref[c % 2, :, g0:g0 + 2 * NGR].astype(BF16)

    xn = _rms(x_ref[...], gpre_ref[...]).astype(BF16)
    proj = lambda a, b: jnp.dot(xn, wb_ref[:, a:b], preferred_element_type=F32)

    zg = proj(g0, g0 + 128) + gbias_ref[...]
    bg_ref[...] = proj(0, W)
    u_ref[...] = proj(W, 2 * W) * proj(2 * W, 3 * W)
    _gate_rows(zg.T[0:2 * NGR], p_ref, ccol_ref)
    qt_ref[...] = (proj(q0, k0) * (DK ** -0.5)).T.astype(BF16)
    kk = proj(k0, v0)
    k_ref[...] = kk.astype(BF16)
    for pair in range(HEADS // 2):
        ps = slice(pair * 2 * DK, (pair + 1) * 2 * DK)
        ksw_ref[:, ps] = pltpu.roll(kk[:, ps], DK, 1).astype(BF16)
    vt_ref[...] = proj(v0, o0).T.astype(BF16)
    o_ref[...] = proj(o0, g0)


def _mix_in(x, gpre, w, gbias, layer, tm):
    T = x.shape[0]
    row = lambda w: pl.BlockSpec((tm, w), lambda i: (i, 0))
    col = lambda h: pl.BlockSpec((h, tm), lambda i: (0, i))
    HK, HV = HEADS * DK, HEADS * DV
    g0 = 3 * CONV_WIDTH + 2 * HK + 2 * HV
    return pl.pallas_call(
        functools.partial(_mix_in_kernel, layer),
        grid=(T // tm,),
        in_specs=[row(D_MODEL), _resident((1, D_MODEL)), pl.BlockSpec(memory_space=pl.ANY),
                  _resident((1, 128))],
        scratch_shapes=[pltpu.VMEM((D_MODEL, g0 + 128), BF16),
                        pltpu.VMEM((2, D_MODEL // MIX_STAGE_CHUNKS, w.shape[2]), F32),
                        pltpu.SemaphoreType.DMA((2,))],
        out_specs=[row(CONV_WIDTH), row(CONV_WIDTH), col(HK), row(HK), row(HK), col(HV), row(HV),
                   col(5 * NGR), row(128)],
        out_shape=[
            jax.ShapeDtypeStruct((T, CONV_WIDTH), F32),
            jax.ShapeDtypeStruct((T, CONV_WIDTH), F32),
            jax.ShapeDtypeStruct((HK, T), BF16),
            jax.ShapeDtypeStruct((T, HK), BF16),
            jax.ShapeDtypeStruct((T, HK), BF16),
            jax.ShapeDtypeStruct((HV, T), BF16),
            jax.ShapeDtypeStruct((T, HV), F32),
            jax.ShapeDtypeStruct((5 * NGR, T), F32),
            jax.ShapeDtypeStruct((T, 128), F32),
        ],
        compiler_params=pltpu.CompilerParams(
            dimension_semantics=("arbitrary",), vmem_limit_bytes=VMEM_LIMIT),
        name="mix_in",
    )(x, gpre, w, gbias)


def _scan_direction(d, vt_ref, k_ref, ksw_ref, p_ref, ct_ref, mp_ref, ct_state, m_state):
    L = CHUNK
    n_chunks = k_ref.shape[0] // L
    lane_half = lax.broadcasted_iota(jnp.int32, (L, 128), 1) // DK
    ones_rows = jnp.ones((DVA - DV, L), BF16)

    cts = [ct_state[d, h] for h in range(HEADS)]
    m_prev = m_state[d]
    for chunk in (range(n_chunks - 1, -1, -1) if d == 1 else range(n_chunks)):
        sl = slice(chunk * L, (chunk + 1) * L)
        e, g, m_chunk = p_ref[P_E, sl], p_ref[P_G, sl], p_ref[P_MC, sl]
        mp_ref[:, sl] = m_prev
        m_new = jnp.maximum(g + m_prev, m_chunk)
        a_old = jnp.exp(g + m_prev - m_new)
        a_new = jnp.exp(m_chunk - m_new)
        for h in range(HEADS):
            r = HEADS * d + h
            vt_aug = jnp.concatenate([vt_ref[h * DV:(h + 1) * DV, sl], ones_rows], axis=0)
            vte = (vt_aug.astype(F32) * e[r:r + 1, :]).astype(BF16)
            src = k_ref if h % 2 == d else ksw_ref
            pair = slice((h // 2) * 2 * DK, (h // 2 + 1) * 2 * DK)
            k_half = jnp.where(lane_half == d, src[sl, pair], jnp.zeros((L, 128), BF16))
            ct_chunk = jnp.dot(vte, k_half, preferred_element_type=F32)
            ct_ref[chunk, h * DVA:(h + 1) * DVA, :] = cts[h].astype(BF16)
            cts[h] = a_old[r:r + 1, 0:1] * cts[h] + a_new[r:r + 1, 0:1] * ct_chunk
        m_prev = m_new
    for h in range(HEADS):
        ct_state[d, h] = cts[h]
    m_state[d] = m_prev


def _mlstm_scan_kernel(vt_f, k_f, ksw_f, p_f, vt_b, k_b, ksw_b, p_b,
                       ct_f, mp_f, ct_b, mp_b, ct_state, m_state):
    @pl.when(pl.program_id(1) == 0)
    def _():
        ct_state[...] = jnp.zeros(ct_state.shape, F32)
        m_state[...] = jnp.full(m_state.shape, NEG_INF, F32)

    _scan_direction(0, vt_f, k_f, ksw_f, p_f, ct_f, mp_f, ct_state, m_state)
    _scan_direction(1, vt_b, k_b, ksw_b, p_b, ct_b, mp_b, ct_state, m_state)


def _mlstm_out_kernel(qt_ref, k_ref, vt_ref, p_ref, mp_f, mp_b, ccol_ref, ct_f, ct_b,
                      o_ref, gain_ref, y_ref):
    L = CHUNK
    n_chunks = k_ref.shape[0] // L
    visible = (_visible(False), _visible(True))
    fwd_row = lax.broadcasted_iota(jnp.int32, (NGR, L), 0) < HEADS
    lane_half = lax.broadcasted_iota(jnp.int32, (DVA, 128), 1) // DK
    ones_rows = jnp.ones((DVA - DV, L), BF16)
    zq = jnp.zeros((DK, L), BF16)
    for chunk in range(n_chunks):
        sl = slice(chunk * L, (chunk + 1) * L)
        m_prev = jnp.where(fwd_row, mp_f[:, sl], mp_b[:, sl])
        n_t = jnp.maximum(m_prev, p_ref[P_M, sl])
        f_inter = jnp.exp(m_prev - n_t)
        e_min = jnp.exp(-(p_ref[P_B, sl] + n_t))
        ccol = ccol_ref[sl, :]
        for h in range(HEADS):
            hs = slice(h * DV, (h + 1) * DV)
            qt = qt_ref[h * DK:(h + 1) * DK, sl]
            k_pair = k_ref[sl, (h // 2) * 2 * DK:(h // 2 + 1) * 2 * DK]
            qt_pair = jnp.concatenate([qt, zq] if h % 2 == 0 else [zq, qt], axis=0)
            st = jnp.dot(k_pair, qt_pair, preferred_element_type=F32)
            pts, qfs = [], []
            for d in range(2):
                r = HEADS * d + h
                arg = jnp.where(visible[d], ccol[:, r:r + 1] - n_t[r:r + 1, :], NEG_INF)
                pts.append((jnp.exp(arg) * st).astype(BF16))
                qfs.append((qt.astype(F32) * f_inter[r:r + 1, :]).astype(BF16))
            rhs = jnp.concatenate([
                jnp.concatenate(pts, axis=1),
                jnp.concatenate([qfs[0], zq], axis=1),
                jnp.concatenate([zq, qfs[1]], axis=1)], axis=0)
            vt_aug = jnp.concatenate([vt_ref[hs, sl], ones_rows], axis=0)
            ct = jnp.where(lane_half == 0, ct_f[chunk, h * DVA:(h + 1) * DVA, :],
                           ct_b[chunk, h * DVA:(h + 1) * DVA, :])
            both = jnp.dot(jnp.concatenate([vt_aug, ct], axis=1), rhs, preferred_element_type=F32)
            ht = None
            for d in range(2):
                r = HEADS * d + h
                numer = both[0:DV, d * L:(d + 1) * L]
                denom = both[DV:DV + 1, d * L:(d + 1) * L]
                part = numer / jnp.maximum(jnp.abs(denom), e_min[r:r + 1, :])
                ht = part if ht is None else ht + part
            ms = jnp.mean(ht * ht, axis=0, keepdims=True)
            hn = ht * lax.rsqrt(ms + EPS) * gain_ref[hs, :]
            y_ref[sl, hs] = (jax.nn.sigmoid(o_ref[sl, hs]) * hn.T).astype(BF16)


def _mlstm(qt, k, ksw, vt, p, ccol, o, gain_b, batch, seq, rows):
    T = batch * seq
    ng = seq // rows
    gc = rows // CHUNK
    HK, HV = HEADS * DK, HEADS * DV

    def scan_specs(group_of):
        blk = lambda b, j: b * ng + group_of(j)
        ins = [pl.BlockSpec((HV, rows), lambda b, j: (0, blk(b, j))),
               pl.BlockSpec((rows, HK), lambda b, j: (blk(b, j), 0)),
               pl.BlockSpec((rows, HK), lambda b, j: (blk(b, j), 0)),
               pl.BlockSpec((5 * NGR, rows), lambda b, j: (0, blk(b, j)))]
        outs = [pl.BlockSpec((gc, HEADS * DVA, 128), lambda b, j: (blk(b, j), 0, 0)),
                pl.BlockSpec((NGR, rows), lambda b, j: (0, blk(b, j)))]
        return ins, outs

    ins_f, outs_f = scan_specs(lambda j: j)
    ins_b, outs_b = scan_specs(lambda j: ng - 1 - j)
    state_shapes = [jax.ShapeDtypeStruct((T // CHUNK, HEADS * DVA, 128), BF16),
                    jax.ShapeDtypeStruct((NGR, T), F32)]
    ct_f, mp_f, ct_b, mp_b = pl.pallas_call(
        _mlstm_scan_kernel,
        grid=(batch, ng),
        in_specs=ins_f + ins_b,
        out_specs=outs_f + outs_b,
        out_shape=state_shapes + state_shapes,
        scratch_shapes=[pltpu.VMEM((2, HEADS, DVA, 128), F32), pltpu.VMEM((2, NGR, 128), F32)],
        compiler_params=pltpu.CompilerParams(dimension_semantics=("arbitrary", "arbitrary"),
                                             vmem_limit_bytes=VMEM_LIMIT),
        name="mlstm_scan",
    )(vt, k, ksw, p, vt, k, ksw, p)

    row = lambda w: pl.BlockSpec((rows, w), lambda i: (i, 0))
    col = lambda h: pl.BlockSpec((h, rows), lambda i: (0, i))
    ctb = pl.BlockSpec((gc, HEADS * DVA, 128), lambda i: (i, 0, 0))
    return pl.pallas_call(
        _mlstm_out_kernel,
        grid=(T // rows,),
        in_specs=[col(HK), row(HK), col(HV), col(5 * NGR), col(NGR), col(NGR), row(128), ctb, ctb,
                  row(HV), _resident((HV, 128))],
        out_specs=row(HV),
        out_shape=jax.ShapeDtypeStruct((T, HV), BF16),
        compiler_params=pltpu.CompilerParams(dimension_semantics=("arbitrary",),
                                             vmem_limit_bytes=VMEM_LIMIT),
        name="mlstm_out",
    )(qt, k, vt, p, mp_f, mp_b, ccol, ct_f, ct_b, o, gain_b)


def _mix_out_kernel(tiles_per_seq, x_ref, bg_ref, u_ref, uprev_ref, unext_ref, cw_ref, cb_ref,
                    y_ref, wout_ref, gpost_ref, o_ref):
    i = pl.program_id(0)
    tm = u_ref.shape[0]
    u = u_ref[...]
    has_prev = (i % tiles_per_seq != 0).astype(F32)
    has_next = (i % tiles_per_seq != tiles_per_seq - 1).astype(F32)
    prev_row = uprev_ref[7:8, :] * has_prev
    next_row = unext_ref[0:1, :] * has_next
    ri = lax.broadcasted_iota(jnp.int32, u.shape, 0)
    u_m1 = jnp.where(ri == 0, prev_row, pltpu.roll(u, 1, 0))
    u_p1 = jnp.where(ri == tm - 1, next_row, pltpu.roll(u, tm - 1, 0))
    conv = cw_ref[0:1, :] * u_m1 + cw_ref[1:2, :] * u + cw_ref[2:3, :] * u_p1
    y_conv = (bg_ref[...] * (conv + cb_ref[...])).astype(BF16)
    h = jnp.dot(y_conv, wout_ref[0:CONV_WIDTH, :], preferred_element_type=F32) \
        + jnp.dot(y_ref[...], wout_ref[CONV_WIDTH:, :], preferred_element_type=F32)
    o_ref[...] = x_ref[...] + _rms(h, gpost_ref[...])


def _mix_out(x, bg, u, conv_w, conv_b, y_mlstm, w_out, gpost, seq, tm):
    T = x.shape[0]
    tiles_per_seq = seq // tm
    sub = tm // 8
    last = T // 8 - 1
    row = lambda w: pl.BlockSpec((tm, w), lambda i: (i, 0))
    return pl.pallas_call(
        functools.partial(_mix_out_kernel, tiles_per_seq),
        grid=(T // tm,),
        in_specs=[
            row(D_MODEL), row(CONV_WIDTH), row(CONV_WIDTH),
            pl.BlockSpec((8, CONV_WIDTH), lambda i: (jnp.maximum(i * sub - 1, 0), 0)),
            pl.BlockSpec((8, CONV_WIDTH), lambda i: (jnp.minimum((i + 1) * sub, last), 0)),
            _resident((3, CONV_WIDTH)),
            _resident((1, CONV_WIDTH)),
            row(HEADS * DV),
            _resident((D_MODEL, D_MODEL)),
            _resident((1, D_MODEL)),
        ],
        out_specs=row(D_MODEL),
        out_shape=jax.ShapeDtypeStruct((T, D_MODEL), F32),
        compiler_params=pltpu.CompilerParams(
            dimension_semantics=("arbitrary",), vmem_limit_bytes=VMEM_LIMIT),
        name="mix_out",
    )(x, bg, u, u, u, conv_w, conv_b, y_mlstm, w_out, gpost)


def kernel(x, norm_ffn1_pre, norm_ffn1_post, w_ffn1_in, w_ffn1_out, norm_mix_pre, norm_mix_post,
           w_mix_in, conv_w, conv_b, gate_i_bias, gate_f_bias, mlstm_norm, w_mix_out,
           norm_ffn2_pre, norm_ffn2_post, w_ffn2_in, w_ffn2_out):
    batch, seq, _ = x.shape
    T = batch * seq
    depth = norm_ffn1_pre.shape[0]
    tm = 512
    xt = x.reshape(T, D_MODEL)
    HV = HEADS * DV
    for l in range(depth):
        xt = _ffn(xt, norm_ffn1_pre[l][None], norm_ffn1_post[l][None], w_ffn1_in, w_ffn1_out, l, tm)

        gbias = jnp.pad(jnp.concatenate([gate_i_bias[l], gate_f_bias[l]]), (0, 128 - 2 * NGR))[None]
        bg, u, qt, k, ksw, vt, o, p, ccol = _mix_in(xt, norm_mix_pre[l][None], w_mix_in, gbias, l, tm)
        gain_b = jnp.broadcast_to(mlstm_norm[l][:, None], (HV, 128))
        y_mlstm = _mlstm(qt, k, ksw, vt, p, ccol, o, gain_b, batch, seq, MLSTM_ROWS)
        xt = _mix_out(xt, bg, u, conv_w[l], conv_b[l][None], y_mlstm,
                      w_mix_out[l].astype(BF16), norm_mix_post[l][None], seq, tm)

        xt = _ffn(xt, norm_ffn2_pre[l][None], norm_ffn2_post[l][None], w_ffn2_in, w_ffn2_out, l, tm)
    return xt.reshape(batch, seq, D_MODEL)
```

```python
import functools

import jax
import jax.numpy as jnp
from jax import lax
from jax.experimental import pallas as pl
from jax.experimental.pallas import tpu as pltpu

D_MODEL = 1024
D_FF = 2816
CONV_WIDTH = 512
HEADS = 4
DK = 64
DV = 128
CHUNK = 128
EPS = 1e-6
NEG_INF = -1e30

FF_TILE = 256
FF_STAGE_CHUNKS = 8
MIX_STAGE_CHUNKS = 8
MLSTM_ROWS = 512
DVA = DV + 16
NGR = 2 * HEADS
VMEM_LIMIT = 56 * 1024 * 1024

P_E, P_M, P_B, P_G, P_MC = (slice(i * NGR, (i + 1) * NGR) for i in range(5))

F32 = jnp.float32
BF16 = jnp.bfloat16


def _rms(x, g):
    return x * lax.rsqrt(jnp.mean(x * x, axis=-1, keepdims=True) + EPS) * g


def _log_sigmoid(z):
    return jnp.minimum(z, 0.0) - jnp.log1p(jnp.exp(-jnp.abs(z)))


def _resident(shape):
    zeros = (0,) * len(shape)
    return pl.BlockSpec(shape, lambda *_: zeros, pipeline_mode=pl.Buffered(1))


def _split3(x):
    hi = x.astype(BF16)
    r1 = x - hi.astype(F32)
    mid = r1.astype(BF16)
    lo = (r1 - mid.astype(F32)).astype(BF16)
    return hi, mid, lo


def _visible(rev):
    s = lax.broadcasted_iota(jnp.int32, (CHUNK, CHUNK), 0)
    t = lax.broadcasted_iota(jnp.int32, (CHUNK, CHUNK), 1)
    return (s >= t) if rev else (s <= t)


def _running_max(x, rev):
    n = x.shape[1]
    pos = lax.broadcasted_iota(jnp.int32, x.shape, 1) & (CHUNK - 1)
    k = 1
    while k < CHUNK:
        if rev:
            shifted, ok = pltpu.roll(x, n - k, 1), pos < CHUNK - k
        else:
            shifted, ok = pltpu.roll(x, k, 1), pos >= k
        x = jnp.maximum(x, jnp.where(ok, shifted, NEG_INF))
        k *= 2
    return x


def _stage_bf16(src_hbm, dst_ref, stage_ref, sem):
    rows = stage_ref.shape[1]
    n_chunks = src_hbm.shape[0] // rows

    def copy(c):
        return pltpu.make_async_copy(src_hbm.at[pl.ds(c * rows, rows), :], stage_ref.at[c % 2], sem.at[c % 2])

    copy(0).start()
    for c in range(n_chunks):
        if c + 1 < n_chunks:
            copy(c + 1).start()
        copy(c).wait()
        dst_ref[c * rows:(c + 1) * rows, :] = stage_ref[c % 2].astype(BF16)


def _ffn_kernel(layer, x_ref, gpre_ref, gpost_ref, win_hbm, wout_hbm, o_ref,
                h_ref, win_ref, wout_ref, stage_in, stage_out, sem):
    @pl.when(pl.program_id(0) == 0)
    def _():
        _stage_bf16(win_hbm.at[layer], win_ref, stage_in, sem)
        _stage_bf16(wout_hbm.at[layer], wout_ref, stage_out, sem)

    x = x_ref[...]
    xn = _rms(x, gpre_ref[...]).astype(BF16)
    for j in range(D_FF // FF_TILE):
        lo = j * FF_TILE
        gate = jnp.dot(xn, win_ref[:, lo:lo + FF_TILE], preferred_element_type=F32)
        up = jnp.dot(xn, win_ref[:, D_FF + lo:D_FF + lo + FF_TILE], preferred_element_type=F32)
        h_ref[:, lo:lo + FF_TILE] = (gate * jax.nn.sigmoid(gate) * up).astype(BF16)
    y = jnp.dot(h_ref[...], wout_ref[...], preferred_element_type=F32)
    o_ref[...] = x + 0.5 * _rms(y, gpost_ref[...])


def _ffn(x, gpre, gpost, w_in, w_out, layer, tm):
    T = x.shape[0]
    return pl.pallas_call(
        functools.partial(_ffn_kernel, layer),
        grid=(T // tm,),
        in_specs=[
            pl.BlockSpec((tm, D_MODEL), lambda i: (i, 0)),
            _resident((1, D_MODEL)),
            _resident((1, D_MODEL)),
            pl.BlockSpec(memory_space=pl.ANY),
            pl.BlockSpec(memory_space=pl.ANY),
        ],
        out_specs=pl.BlockSpec((tm, D_MODEL), lambda i: (i, 0)),
        out_shape=jax.ShapeDtypeStruct((T, D_MODEL), F32),
        scratch_shapes=[
            pltpu.VMEM((tm, D_FF), BF16),
            pltpu.VMEM((D_MODEL, 2 * D_FF), BF16),
            pltpu.VMEM((D_FF, D_MODEL), BF16),
            pltpu.VMEM((2, D_MODEL // FF_STAGE_CHUNKS, 2 * D_FF), F32),
            pltpu.VMEM((2, D_FF // FF_STAGE_CHUNKS, D_MODEL), F32),
            pltpu.SemaphoreType.DMA((2,)),
        ],
        compiler_params=pltpu.CompilerParams(
            dimension_semantics=("arbitrary",), vmem_limit_bytes=VMEM_LIMIT),
        name="ffn",
    )(x, gpre, gpost, w_in, w_out)


def _gate_rows(zr, p_ref, ccol_ref):
    L = CHUNK
    n_chunks = zr.shape[1] // L
    fwd_row = lax.broadcasted_iota(jnp.int32, (NGR, L), 0) < HEADS
    fwd_col = lax.broadcasted_iota(jnp.int32, (NGR, 1), 0) < HEADS
    li = zr[0:NGR]
    gates = jnp.concatenate([li, _log_sigmoid(zr[NGR:2 * NGR])], axis=0)
    x3 = jnp.concatenate(_split3(gates), axis=0)
    stacked = jnp.concatenate([x3[:, c * L:(c + 1) * L] for c in range(n_chunks)], axis=0)
    cum_ops = jnp.concatenate([_visible(False), _visible(True)], axis=1).astype(BF16)
    cum = jnp.dot(stacked, cum_ops, preferred_element_type=F32)
    b_chunks = []
    for c in range(n_chunks):
        blk = cum[c * 48:(c + 1) * 48]
        s16 = blk[0:16] + blk[16:32] + blk[32:48]
        b_chunks.append(jnp.where(fwd_row, s16[NGR:2 * NGR, 0:L], s16[NGR:2 * NGR, L:2 * L]))
    b = jnp.concatenate(b_chunks, axis=1)
    cc = li - b
    fwd_all = lax.broadcasted_iota(jnp.int32, cc.shape, 0) < HEADS
    p_ref[P_M, :] = jnp.where(fwd_all, _running_max(cc, False), _running_max(cc, True))
    p_ref[P_B, :] = b
    pad = jnp.zeros((L - NGR, L), F32)
    for c in range(n_chunks):
        sl = slice(c * L, (c + 1) * L)
        b_c, cc_c = b_chunks[c], cc[:, sl]
        g = jnp.where(fwd_col, b_c[:, L - 1:L], b_c[:, 0:1])
        m_chunk = g + jnp.max(cc_c, axis=1, keepdims=True)
        p_ref[P_E, sl] = jnp.exp(g + cc_c - m_chunk)
        p_ref[P_G, sl] = jnp.broadcast_to(g, (NGR, L))
        p_ref[P_MC, sl] = jnp.broadcast_to(m_chunk, (NGR, L))
        ccol_ref[sl, :] = jnp.concatenate([cc_c, pad], axis=0).T


def _mix_in_kernel(layer, x_ref, gpre_ref, w_hbm, gbias_ref,
                   bg_ref, u_ref, qt_ref, k_ref, ksw_ref, vt_ref, o_ref, p_ref, ccol_ref,
                   wb_ref, stage_ref, sem):
    W, HK, HV = CONV_WIDTH, HEADS * DK, HEADS * DV
    q0 = 3 * W
    k0, v0 = q0 + HK, q0 + 2 * HK
    o0 = v0 + HV
    g0 = o0 + HV

    @pl.when(pl.program_id(0) == 0)
    def _():
        src = w_hbm.at[layer]
        rows = stage_ref.shape[1]
        n_chunks = g0 // rows

        def copy(c):
            n = rows if c < n_chunks else 2 * NGR
            return pltpu.make_async_copy(src.at[pl.ds(c * rows, n), :], stage_ref.at[c % 2, pl.ds(0, n), :],
                                         sem.at[c % 2])

        copy(0).start()
        for c in range(n_chunks + 1):
            if c < n_chunks:
                copy(c + 1).start()
            copy(c).wait()
            if c < n_chunks:
                wb_ref[:, c * rows:(c + 1) * rows] = stage_ref[c % 2].T.astype(BF16)
            else:
                gt = stage_ref[c % 2, 0:128, :].T
                lane = lax.broadcasted_iota(jnp.int32, gt.shape, 1)
                wb_ref[:, g0:g0 + 128] = jnp.where(lane < 2 * NGR, gt, 0.0).astype(BF16)

    xn = _rms(x_ref[...], gpre_ref[...]).astype(BF16)
    proj = lambda a, b: jnp.dot(xn, wb_ref[:, a:b], preferred_element_type=F32)

    zg = proj(g0, g0 + 128) + gbias_ref[...]
    bg_ref[...] = proj(0, W)
    u_ref[...] = proj(W, 2 * W) * proj(2 * W, 3 * W)
    _gate_rows(zg.T[0:2 * NGR], p_ref, ccol_ref)
    qt_ref[...] = (proj(q0, k0) * (DK ** -0.5)).T.astype(BF16)
    kk = proj(k0, v0)
    k_ref[...] = kk.astype(BF16)
    for pair in range(HEADS // 2):
        ps = slice(pair * 2 * DK, (pair + 1) * 2 * DK)
        ksw_ref[:, ps] = pltpu.roll(kk[:, ps], DK, 1).astype(BF16)
    vt_ref[...] = proj(v0, o0).T.astype(BF16)
    o_ref[...] = proj(o0, g0)


def _mix_in(x, gpre, w, gbias, layer, tm):
    T = x.shape[0]
    row = lambda w: pl.BlockSpec((tm, w), lambda i: (i, 0))
    col = lambda h: pl.BlockSpec((h, tm), lambda i: (0, i))
    HK, HV = HEADS * DK, HEADS * DV
    g0 = 3 * CONV_WIDTH + 2 * HK + 2 * HV
    return pl.pallas_call(
        functools.partial(_mix_in_kernel, layer),
        grid=(T // tm,),
        in_specs=[row(D_MODEL), _resident((1, D_MODEL)), pl.BlockSpec(memory_space=pl.ANY),
                  _resident((1, 128))],
        scratch_shapes=[pltpu.VMEM((D_MODEL, g0 + 128), BF16),
                        pltpu.VMEM((2, g0 // MIX_STAGE_CHUNKS, D_MODEL), F32),
                        pltpu.SemaphoreType.DMA((2,))],
        out_specs=[row(CONV_WIDTH), row(CONV_WIDTH), col(HK), row(HK), row(HK), col(HV), row(HV),
                   col(5 * NGR), row(128)],
        out_shape=[
            jax.ShapeDtypeStruct((T, CONV_WIDTH), F32),
            jax.ShapeDtypeStruct((T, CONV_WIDTH), F32),
            jax.ShapeDtypeStruct((HK, T), BF16),
            jax.ShapeDtypeStruct((T, HK), BF16),
            jax.ShapeDtypeStruct((T, HK), BF16),
            jax.ShapeDtypeStruct((HV, T), BF16),
            jax.ShapeDtypeStruct((T, HV), F32),
            jax.ShapeDtypeStruct((5 * NGR, T), F32),
            jax.ShapeDtypeStruct((T, 128), F32),
        ],
        compiler_params=pltpu.CompilerParams(
            dimension_semantics=("arbitrary",), vmem_limit_bytes=VMEM_LIMIT),
        name="mix_in",
    )(x, gpre, w, gbias)


def _scan_direction(d, vt_ref, k_ref, ksw_ref, p_ref, ct_ref, mp_ref, ct_state, m_state):
    L = CHUNK
    n_chunks = k_ref.shape[0] // L
    lane_half = lax.broadcasted_iota(jnp.int32, (L, 128), 1) // DK
    ones_rows = jnp.ones((DVA - DV, L), BF16)

    cts = [ct_state[d, h] for h in range(HEADS)]
    m_prev = m_state[d]
    for chunk in (range(n_chunks - 1, -1, -1) if d == 1 else range(n_chunks)):
        sl = slice(chunk * L, (chunk + 1) * L)
        e, g, m_chunk = p_ref[P_E, sl], p_ref[P_G, sl], p_ref[P_MC, sl]
        mp_ref[:, sl] = m_prev
        m_new = jnp.maximum(g + m_prev, m_chunk)
        a_old = jnp.exp(g + m_prev - m_new)
        a_new = jnp.exp(m_chunk - m_new)
        for h in range(HEADS):
            r = HEADS * d + h
            vt_aug = jnp.concatenate([vt_ref[h * DV:(h + 1) * DV, sl], ones_rows], axis=0)
            vte = (vt_aug.astype(F32) * e[r:r + 1, :]).astype(BF16)
            src = k_ref if h % 2 == d else ksw_ref
            pair = slice((h // 2) * 2 * DK, (h // 2 + 1) * 2 * DK)
            k_half = jnp.where(lane_half == d, src[sl, pair], jnp.zeros((L, 128), BF16))
            ct_chunk = jnp.dot(vte, k_half, preferred_element_type=F32)
            ct_ref[chunk, h * DVA:(h + 1) * DVA, :] = cts[h].astype(BF16)
            cts[h] = a_old[r:r + 1, 0:1] * cts[h] + a_new[r:r + 1, 0:1] * ct_chunk
        m_prev = m_new
    for h in range(HEADS):
        ct_state[d, h] = cts[h]
    m_state[d] = m_prev


def _mlstm_scan_kernel(vt_f, k_f, ksw_f, p_f, vt_b, k_b, ksw_b, p_b,
                       ct_f, mp_f, ct_b, mp_b, ct_state, m_state):
    @pl.when(pl.program_id(1) == 0)
    def _():
        ct_state[...] = jnp.zeros(ct_state.shape, F32)
        m_state[...] = jnp.full(m_state.shape, NEG_INF, F32)

    _scan_direction(0, vt_f, k_f, ksw_f, p_f, ct_f, mp_f, ct_state, m_state)
    _scan_direction(1, vt_b, k_b, ksw_b, p_b, ct_b, mp_b, ct_state, m_state)


def _mlstm_out_kernel(qt_ref, k_ref, vt_ref, p_ref, mp_f, mp_b, ccol_ref, ct_f, ct_b,
                      o_ref, gain_ref, y_ref):
    L = CHUNK
    n_chunks = k_ref.shape[0] // L
    visible = (_visible(False), _visible(True))
    fwd_row = lax.broadcasted_iota(jnp.int32, (NGR, L), 0) < HEADS
    lane_half = lax.broadcasted_iota(jnp.int32, (DVA, 128), 1) // DK
    ones_rows = jnp.ones((DVA - DV, L), BF16)
    zq = jnp.zeros((DK, L), BF16)
    for chunk in range(n_chunks):
        sl = slice(chunk * L, (chunk + 1) * L)
        m_prev = jnp.where(fwd_row, mp_f[:, sl], mp_b[:, sl])
        n_t = jnp.maximum(m_prev, p_ref[P_M, sl])
        f_inter = jnp.exp(m_prev - n_t)
        e_min = jnp.exp(-(p_ref[P_B, sl] + n_t))
        ccol = ccol_ref[sl, :]
        for h in range(HEADS):
            hs = slice(h * DV, (h + 1) * DV)
            qt = qt_ref[h * DK:(h + 1) * DK, sl]
            k_pair = k_ref[sl, (h // 2) * 2 * DK:(h // 2 + 1) * 2 * DK]
            qt_pair = jnp.concatenate([qt, zq] if h % 2 == 0 else [zq, qt], axis=0)
            st = jnp.dot(k_pair, qt_pair, preferred_element_type=F32)
            pts, qfs = [], []
            for d in range(2):
                r = HEADS * d + h
                arg = jnp.where(visible[d], ccol[:, r:r + 1] - n_t[r:r + 1, :], NEG_INF)
                pts.append((jnp.exp(arg) * st).astype(BF16))
                qfs.append((qt.astype(F32) * f_inter[r:r + 1, :]).astype(BF16))
            rhs = jnp.concatenate([
                jnp.concatenate(pts, axis=1),
                jnp.concatenate([qfs[0], zq], axis=1),
                jnp.concatenate([zq, qfs[1]], axis=1)], axis=0)
            vt_aug = jnp.concatenate([vt_ref[hs, sl], ones_rows], axis=0)
            ct = jnp.where(lane_half == 0, ct_f[chunk, h * DVA:(h + 1) * DVA, :],
                           ct_b[chunk, h * DVA:(h + 1) * DVA, :])
            both = jnp.dot(jnp.concatenate([vt_aug, ct], axis=1), rhs, preferred_element_type=F32)
            ht = None
            for d in range(2):
                r = HEADS * d + h
                numer = both[0:DV, d * L:(d + 1) * L]
                denom = both[DV:DV + 1, d * L:(d + 1) * L]
                part = numer / jnp.maximum(jnp.abs(denom), e_min[r:r + 1, :])
                ht = part if ht is None else ht + part
            ms = jnp.mean(ht * ht, axis=0, keepdims=True)
            hn = ht * lax.rsqrt(ms + EPS) * gain_ref[hs, :]
            y_ref[sl, hs] = (jax.nn.sigmoid(o_ref[sl, hs]) * hn.T).astype(BF16)


def _mlstm(qt, k, ksw, vt, p, ccol, o, gain_b, batch, seq, rows):
    T = batch * seq
    ng = seq // rows
    gc = rows // CHUNK
    HK, HV = HEADS * DK, HEADS * DV

    def scan_specs(group_of):
        blk = lambda b, j: b * ng + group_of(j)
        ins = [pl.BlockSpec((HV, rows), lambda b, j: (0, blk(b, j))),
               pl.BlockSpec((rows, HK), lambda b, j: (blk(b, j), 0)),
               pl.BlockSpec((rows, HK), lambda b, j: (blk(b, j), 0)),
               pl.BlockSpec((5 * NGR, rows), lambda b, j: (0, blk(b, j)))]
        outs = [pl.BlockSpec((gc, HEADS * DVA, 128), lambda b, j: (blk(b, j), 0, 0)),
                pl.BlockSpec((NGR, rows), lambda b, j: (0, blk(b, j)))]
        return ins, outs

    ins_f, outs_f = scan_specs(lambda j: j)
    ins_b, outs_b = scan_specs(lambda j: ng - 1 - j)
    state_shapes = [jax.ShapeDtypeStruct((T // CHUNK, HEADS * DVA, 128), BF16),
                    jax.ShapeDtypeStruct((NGR, T), F32)]
    ct_f, mp_f, ct_b, mp_b = pl.pallas_call(
        _mlstm_scan_kernel,
        grid=(batch, ng),
        in_specs=ins_f + ins_b,
        out_specs=outs_f + outs_b,
        out_shape=state_shapes + state_shapes,
        scratch_shapes=[pltpu.VMEM((2, HEADS, DVA, 128), F32), pltpu.VMEM((2, NGR, 128), F32)],
        compiler_params=pltpu.CompilerParams(dimension_semantics=("arbitrary", "arbitrary"),
                                             vmem_limit_bytes=VMEM_LIMIT),
        name="mlstm_scan",
    )(vt, k, ksw, p, vt, k, ksw, p)

    row = lambda w: pl.BlockSpec((rows, w), lambda i: (i, 0))
    col = lambda h: pl.BlockSpec((h, rows), lambda i: (0, i))
    ctb = pl.BlockSpec((gc, HEADS * DVA, 128), lambda i: (i, 0, 0))
    return pl.pallas_call(
        _mlstm_out_kernel,
        grid=(T // rows,),
        in_specs=[col(HK), row(HK), col(HV), col(5 * NGR), col(NGR), col(NGR), row(128), ctb, ctb,
                  row(HV), _resident((HV, 128))],
        out_specs=row(HV),
        out_shape=jax.ShapeDtypeStruct((T, HV), BF16),
        compiler_params=pltpu.CompilerParams(dimension_semantics=("arbitrary",),
                                             vmem_limit_bytes=VMEM_LIMIT),
        name="mlstm_out",
    )(qt, k, vt, p, mp_f, mp_b, ccol, ct_f, ct_b, o, gain_b)


def _mix_out_kernel(tiles_per_seq, x_ref, bg_ref, u_ref, uprev_ref, unext_ref, cw_ref, cb_ref,
                    y_ref, wout_ref, gpost_ref, o_ref):
    i = pl.program_id(0)
    tm = u_ref.shape[0]
    u = u_ref[...]
    has_prev = (i % tiles_per_seq != 0).astype(F32)
    has_next = (i % tiles_per_seq != tiles_per_seq - 1).astype(F32)
    prev_row = uprev_ref[7:8, :] * has_prev
    next_row = unext_ref[0:1, :] * has_next
    ri = lax.broadcasted_iota(jnp.int32, u.shape, 0)
    u_m1 = jnp.where(ri == 0, prev_row, pltpu.roll(u, 1, 0))
    u_p1 = jnp.where(ri == tm - 1, next_row, pltpu.roll(u, tm - 1, 0))
    conv = cw_ref[0:1, :] * u_m1 + cw_ref[1:2, :] * u + cw_ref[2:3, :] * u_p1
    y_conv = (bg_ref[...] * (conv + cb_ref[...])).astype(BF16)
    h = jnp.dot(y_conv, wout_ref[0:CONV_WIDTH, :], preferred_element_type=F32) \
        + jnp.dot(y_ref[...], wout_ref[CONV_WIDTH:, :], preferred_element_type=F32)
    o_ref[...] = x_ref[...] + _rms(h, gpost_ref[...])


def _mix_out(x, bg, u, conv_w, conv_b, y_mlstm, w_out, gpost, seq, tm):
    T = x.shape[0]
    tiles_per_seq = seq // tm
    sub = tm // 8
    last = T // 8 - 1
    row = lambda w: pl.BlockSpec((tm, w), lambda i: (i, 0))
    return pl.pallas_call(
        functools.partial(_mix_out_kernel, tiles_per_seq),
        grid=(T // tm,),
        in_specs=[
            row(D_MODEL), row(CONV_WIDTH), row(CONV_WIDTH),
            pl.BlockSpec((8, CONV_WIDTH), lambda i: (jnp.maximum(i * sub - 1, 0), 0)),
            pl.BlockSpec((8, CONV_WIDTH), lambda i: (jnp.minimum((i + 1) * sub, last), 0)),
            _resident((3, CONV_WIDTH)),
            _resident((1, CONV_WIDTH)),
            row(HEADS * DV),
            _resident((D_MODEL, D_MODEL)),
            _resident((1, D_MODEL)),
        ],
        out_specs=row(D_MODEL),
        out_shape=jax.ShapeDtypeStruct((T, D_MODEL), F32),
        compiler_params=pltpu.CompilerParams(
            dimension_semantics=("arbitrary",), vmem_limit_bytes=VMEM_LIMIT),
        name="mix_out",
    )(x, bg, u, u, u, conv_w, conv_b, y_mlstm, w_out, gpost)


def kernel(x, norm_ffn1_pre, norm_ffn1_post, w_ffn1_in, w_ffn1_out, norm_mix_pre, norm_mix_post,
           w_mix_in, conv_w, conv_b, gate_i_bias, gate_f_bias, mlstm_norm, w_mix_out,
           norm_ffn2_pre, norm_ffn2_post, w_ffn2_in, w_ffn2_out):
    batch, seq, _ = x.shape
    T = batch * seq
    depth = norm_ffn1_pre.shape[0]
    tm = 512
    xt = x.reshape(T, D_MODEL)
    HV = HEADS * DV
    for l in range(depth):
        xt = _ffn(xt, norm_ffn1_pre[l][None], norm_ffn1_post[l][None], w_ffn1_in, w_ffn1_out, l, tm)

        gbias = jnp.pad(jnp.concatenate([gate_i_bias[l], gate_f_bias[l]]), (0, 128 - 2 * NGR))[None]
        bg, u, qt, k, ksw, vt, o, p, ccol = _mix_in(
            xt, norm_mix_pre[l][None], jnp.swapaxes(w_mix_in, 1, 2), gbias, l, tm)
        gain_b = jnp.broadcast_to(mlstm_norm[l][:, None], (HV, 128))
        y_mlstm = _mlstm(qt, k, ksw, vt, p, ccol, o, gain_b, batch, seq, MLSTM_ROWS)
        xt = _mix_out(xt, bg, u, conv_w[l], conv_b[l][None], y_mlstm,
                      w_mix_out[l].astype(BF16), norm_mix_post[l][None], seq, tm)

        xt = _ffn(xt, norm_ffn2_pre[l][None], norm_ffn2_post[l][None], w_ffn2_in, w_ffn2_out, l, tm)
    return xt.reshape(batch, seq, D_MODEL)
```

```python
import functools

import jax
import jax.numpy as jnp
from jax import lax
from jax.experimental import pallas as pl
from jax.experimental.pallas import tpu as pltpu

D_MODEL = 1024
D_FF = 2816
CONV_WIDTH = 512
HEADS = 4
DK = 64
DV = 128
CHUNK = 128
EPS = 1e-6
NEG_INF = -1e30

FF_TILE = 256
FF_STAGE_CHUNKS = 8
MIX_STAGE_CHUNKS = 8
MLSTM_ROWS = 512
SCAN_ROWS = 2048
DVA = DV + 16
NGR = 2 * HEADS
VMEM_LIMIT = 56 * 1024 * 1024

P_E, P_M, P_B, P_G, P_MC = (slice(i * NGR, (i + 1) * NGR) for i in range(5))

F32 = jnp.float32
BF16 = jnp.bfloat16


def _rms(x, g):
    return x * lax.rsqrt(jnp.mean(x * x, axis=-1, keepdims=True) + EPS) * g


def _log_sigmoid(z):
    return jnp.minimum(z, 0.0) - jnp.log1p(jnp.exp(-jnp.abs(z)))


def _resident(shape):
    zeros = (0,) * len(shape)
    return pl.BlockSpec(shape, lambda *_: zeros, pipeline_mode=pl.Buffered(1))


def _split3(x):
    hi = x.astype(BF16)
    r1 = x - hi.astype(F32)
    mid = r1.astype(BF16)
    lo = (r1 - mid.astype(F32)).astype(BF16)
    return hi, mid, lo


def _visible(rev):
    s = lax.broadcasted_iota(jnp.int32, (CHUNK, CHUNK), 0)
    t = lax.broadcasted_iota(jnp.int32, (CHUNK, CHUNK), 1)
    return (s >= t) if rev else (s <= t)


def _running_max(x, rev):
    n = x.shape[1]
    pos = lax.broadcasted_iota(jnp.int32, x.shape, 1) & (CHUNK - 1)
    k = 1
    while k < CHUNK:
        if rev:
            shifted, ok = pltpu.roll(x, n - k, 1), pos < CHUNK - k
        else:
            shifted, ok = pltpu.roll(x, k, 1), pos >= k
        x = jnp.maximum(x, jnp.where(ok, shifted, NEG_INF))
        k *= 2
    return x


def _stage_bf16(src_hbm, dst_ref, stage_ref, sem):
    rows = stage_ref.shape[1]
    n_chunks = src_hbm.shape[0] // rows

    def copy(c):
        return pltpu.make_async_copy(src_hbm.at[pl.ds(c * rows, rows), :], stage_ref.at[c % 2], sem.at[c % 2])

    copy(0).start()
    for c in range(n_chunks):
        if c + 1 < n_chunks:
            copy(c + 1).start()
        copy(c).wait()
        dst_ref[c * rows:(c + 1) * rows, :] = stage_ref[c % 2].astype(BF16)


def _ffn_kernel(layer, x_ref, gpre_ref, gpost_ref, win_hbm, wout_hbm, o_ref,
                h_ref, win_ref, wout_ref, stage_in, stage_out, sem):
    @pl.when(pl.program_id(0) == 0)
    def _():
        _stage_bf16(win_hbm.at[layer], win_ref, stage_in, sem)
        _stage_bf16(wout_hbm.at[layer], wout_ref, stage_out, sem)

    x = x_ref[...]
    xn = _rms(x, gpre_ref[...]).astype(BF16)
    for j in range(D_FF // FF_TILE):
        lo = j * FF_TILE
        gate = jnp.dot(xn, win_ref[:, lo:lo + FF_TILE], preferred_element_type=F32)
        up = jnp.dot(xn, win_ref[:, D_FF + lo:D_FF + lo + FF_TILE], preferred_element_type=F32)
        h_ref[:, lo:lo + FF_TILE] = (gate * jax.nn.sigmoid(gate) * up).astype(BF16)
    y = jnp.dot(h_ref[...], wout_ref[...], preferred_element_type=F32)
    o_ref[...] = x + 0.5 * _rms(y, gpost_ref[...])


def _ffn(x, gpre, gpost, w_in, w_out, layer, tm):
    T = x.shape[0]
    return pl.pallas_call(
        functools.partial(_ffn_kernel, layer),
        grid=(T // tm,),
        in_specs=[
            pl.BlockSpec((tm, D_MODEL), lambda i: (i, 0)),
            _resident((1, D_MODEL)),
            _resident((1, D_MODEL)),
            pl.BlockSpec(memory_space=pl.ANY),
            pl.BlockSpec(memory_space=pl.ANY),
        ],
        out_specs=pl.BlockSpec((tm, D_MODEL), lambda i: (i, 0)),
        out_shape=jax.ShapeDtypeStruct((T, D_MODEL), F32),
        scratch_shapes=[
            pltpu.VMEM((tm, D_FF), BF16),
            pltpu.VMEM((D_MODEL, 2 * D_FF), BF16),
            pltpu.VMEM((D_FF, D_MODEL), BF16),
            pltpu.VMEM((2, D_MODEL // FF_STAGE_CHUNKS, 2 * D_FF), F32),
            pltpu.VMEM((2, D_FF // FF_STAGE_CHUNKS, D_MODEL), F32),
            pltpu.SemaphoreType.DMA((2,)),
        ],
        compiler_params=pltpu.CompilerParams(
            dimension_semantics=("arbitrary",), vmem_limit_bytes=VMEM_LIMIT),
        name="ffn",
    )(x, gpre, gpost, w_in, w_out)


def _gate_rows(zr, p_ref, ccol_ref):
    L = CHUNK
    n_chunks = zr.shape[1] // L
    fwd_row = lax.broadcasted_iota(jnp.int32, (NGR, L), 0) < HEADS
    fwd_col = lax.broadcasted_iota(jnp.int32, (NGR, 1), 0) < HEADS
    li = zr[0:NGR]
    gates = jnp.concatenate([li, _log_sigmoid(zr[NGR:2 * NGR])], axis=0)
    x3 = jnp.concatenate(_split3(gates), axis=0)
    stacked = jnp.concatenate([x3[:, c * L:(c + 1) * L] for c in range(n_chunks)], axis=0)
    cum_ops = jnp.concatenate([_visible(False), _visible(True)], axis=1).astype(BF16)
    cum = jnp.dot(stacked, cum_ops, preferred_element_type=F32)
    b_chunks = []
    for c in range(n_chunks):
        blk = cum[c * 48:(c + 1) * 48]
        s16 = blk[0:16] + blk[16:32] + blk[32:48]
        b_chunks.append(jnp.where(fwd_row, s16[NGR:2 * NGR, 0:L], s16[NGR:2 * NGR, L:2 * L]))
    b = jnp.concatenate(b_chunks, axis=1)
    cc = li - b
    fwd_all = lax.broadcasted_iota(jnp.int32, cc.shape, 0) < HEADS
    p_ref[P_M, :] = jnp.where(fwd_all, _running_max(cc, False), _running_max(cc, True))
    p_ref[P_B, :] = b
    pad = jnp.zeros((L - NGR, L), F32)
    for c in range(n_chunks):
        sl = slice(c * L, (c + 1) * L)
        b_c, cc_c = b_chunks[c], cc[:, sl]
        g = jnp.where(fwd_col, b_c[:, L - 1:L], b_c[:, 0:1])
        m_chunk = g + jnp.max(cc_c, axis=1, keepdims=True)
        p_ref[P_E, sl] = jnp.exp(g + cc_c - m_chunk)
        p_ref[P_G, sl] = jnp.broadcast_to(g, (NGR, L))
        p_ref[P_MC, sl] = jnp.broadcast_to(m_chunk, (NGR, L))
        ccol_ref[sl, :] = jnp.concatenate([cc_c, pad], axis=0).T


def _mix_in_kernel(layer, x_ref, gpre_ref, w_hbm, gbias_ref,
                   bg_ref, u_ref, qt_ref, k_ref, ksw_ref, vt_ref, o_ref, p_ref, ccol_ref,
                   wb_ref, stage_ref, sem):
    W, HK, HV = CONV_WIDTH, HEADS * DK, HEADS * DV
    q0 = 3 * W
    k0, v0 = q0 + HK, q0 + 2 * HK
    o0 = v0 + HV
    g0 = o0 + HV

    @pl.when(pl.program_id(0) == 0)
    def _():
        src = w_hbm.at[layer]
        rows = stage_ref.shape[1]
        n_chunks = g0 // rows

        def copy(c):
            n = rows if c < n_chunks else 2 * NGR
            return pltpu.make_async_copy(src.at[pl.ds(c * rows, n), :], stage_ref.at[c % 2, pl.ds(0, n), :],
                                         sem.at[c % 2])

        copy(0).start()
        for c in range(n_chunks + 1):
            if c < n_chunks:
                copy(c + 1).start()
            copy(c).wait()
            if c < n_chunks:
                wb_ref[:, c * rows:(c + 1) * rows] = stage_ref[c % 2].T.astype(BF16)
            else:
                gt = stage_ref[c % 2, 0:128, :].T
                lane = lax.broadcasted_iota(jnp.int32, gt.shape, 1)
                wb_ref[:, g0:g0 + 128] = jnp.where(lane < 2 * NGR, gt, 0.0).astype(BF16)

    xn = _rms(x_ref[...], gpre_ref[...]).astype(BF16)
    proj = lambda a, b: jnp.dot(xn, wb_ref[:, a:b], preferred_element_type=F32)

    zg = proj(g0, g0 + 128) + gbias_ref[...]
    bg_ref[...] = proj(0, W)
    u_ref[...] = proj(W, 2 * W) * proj(2 * W, 3 * W)
    _gate_rows(zg.T[0:2 * NGR], p_ref, ccol_ref)
    qt_ref[...] = (proj(q0, k0) * (DK ** -0.5)).T.astype(BF16)
    kk = proj(k0, v0)
    k_ref[...] = kk.astype(BF16)
    for pair in range(HEADS // 2):
        ps = slice(pair * 2 * DK, (pair + 1) * 2 * DK)
        ksw_ref[:, ps] = pltpu.roll(kk[:, ps], DK, 1).astype(BF16)
    vt_ref[...] = proj(v0, o0).T.astype(BF16)
    o_ref[...] = proj(o0, g0)


def _mix_in(x, gpre, w, gbias, layer, tm):
    T = x.shape[0]
    row = lambda w: pl.BlockSpec((tm, w), lambda i: (i, 0))
    col = lambda h: pl.BlockSpec((h, tm), lambda i: (0, i))
    HK, HV = HEADS * DK, HEADS * DV
    g0 = 3 * CONV_WIDTH + 2 * HK + 2 * HV
    return pl.pallas_call(
        functools.partial(_mix_in_kernel, layer),
        grid=(T // tm,),
        in_specs=[row(D_MODEL), _resident((1, D_MODEL)), pl.BlockSpec(memory_space=pl.ANY),
                  _resident((1, 128))],
        scratch_shapes=[pltpu.VMEM((D_MODEL, g0 + 128), BF16),
                        pltpu.VMEM((2, g0 // MIX_STAGE_CHUNKS, D_MODEL), F32),
                        pltpu.SemaphoreType.DMA((2,))],
        out_specs=[row(CONV_WIDTH), row(CONV_WIDTH), col(HK), row(HK), row(HK), col(HV), row(HV),
                   col(5 * NGR), row(128)],
        out_shape=[
            jax.ShapeDtypeStruct((T, CONV_WIDTH), F32),
            jax.ShapeDtypeStruct((T, CONV_WIDTH), F32),
            jax.ShapeDtypeStruct((HK, T), BF16),
            jax.ShapeDtypeStruct((T, HK), BF16),
            jax.ShapeDtypeStruct((T, HK), BF16),
            jax.ShapeDtypeStruct((HV, T), BF16),
            jax.ShapeDtypeStruct((T, HV), F32),
            jax.ShapeDtypeStruct((5 * NGR, T), F32),
            jax.ShapeDtypeStruct((T, 128), F32),
        ],
        compiler_params=pltpu.CompilerParams(
            dimension_semantics=("arbitrary",), vmem_limit_bytes=VMEM_LIMIT),
        name="mix_in",
    )(x, gpre, w, gbias)


def _scan_direction(d, vt_ref, k_ref, ksw_ref, p_ref, ct_ref, mp_ref, ct_state, m_state):
    L = CHUNK
    n_chunks = k_ref.shape[0] // L
    lane_half = lax.broadcasted_iota(jnp.int32, (L, 128), 1) // DK
    ones_rows = jnp.ones((DVA - DV, L), BF16)

    cts = [ct_state[d, h] for h in range(HEADS)]
    m_prev = m_state[d]
    for chunk in (range(n_chunks - 1, -1, -1) if d == 1 else range(n_chunks)):
        sl = slice(chunk * L, (chunk + 1) * L)
        e, g, m_chunk = p_ref[P_E, sl], p_ref[P_G, sl], p_ref[P_MC, sl]
        mp_ref[:, sl] = m_prev
        m_new = jnp.maximum(g + m_prev, m_chunk)
        a_old = jnp.exp(g + m_prev - m_new)
        a_new = jnp.exp(m_chunk - m_new)
        for h in range(HEADS):
            r = HEADS * d + h
            vt_aug = jnp.concatenate([vt_ref[h * DV:(h + 1) * DV, sl], ones_rows], axis=0)
            vte = (vt_aug.astype(F32) * e[r:r + 1, :]).astype(BF16)
            src = k_ref if h % 2 == d else ksw_ref
            pair = slice((h // 2) * 2 * DK, (h // 2 + 1) * 2 * DK)
            k_half = jnp.where(lane_half == d, src[sl, pair], jnp.zeros((L, 128), BF16))
            ct_chunk = jnp.dot(vte, k_half, preferred_element_type=F32)
            ct_ref[chunk, h * DVA:(h + 1) * DVA, :] = cts[h].astype(BF16)
            cts[h] = a_old[r:r + 1, 0:1] * cts[h] + a_new[r:r + 1, 0:1] * ct_chunk
        m_prev = m_new
    for h in range(HEADS):
        ct_state[d, h] = cts[h]
    m_state[d] = m_prev


def _mlstm_scan_kernel(vt_f, k_f, ksw_f, p_f, vt_b, k_b, ksw_b, p_b,
                       ct_f, mp_f, ct_b, mp_b, ct_state, m_state):
    @pl.when(pl.program_id(1) == 0)
    def _():
        ct_state[...] = jnp.zeros(ct_state.shape, F32)
        m_state[...] = jnp.full(m_state.shape, NEG_INF, F32)

    _scan_direction(0, vt_f, k_f, ksw_f, p_f, ct_f, mp_f, ct_state, m_state)
    _scan_direction(1, vt_b, k_b, ksw_b, p_b, ct_b, mp_b, ct_state, m_state)


def _mlstm_out_kernel(qt_ref, k_ref, vt_ref, p_ref, mp_f, mp_b, ccol_ref, ct_f, ct_b,
                      o_ref, gain_ref, y_ref):
    L = CHUNK
    n_chunks = k_ref.shape[0] // L
    visible = (_visible(False), _visible(True))
    fwd_row = lax.broadcasted_iota(jnp.int32, (NGR, L), 0) < HEADS
    lane_half = lax.broadcasted_iota(jnp.int32, (DVA, 128), 1) // DK
    ones_rows = jnp.ones((DVA - DV, L), BF16)
    zq = jnp.zeros((DK, L), BF16)
    for chunk in range(n_chunks):
        sl = slice(chunk * L, (chunk + 1) * L)
        m_prev = jnp.where(fwd_row, mp_f[:, sl], mp_b[:, sl])
        n_t = jnp.maximum(m_prev, p_ref[P_M, sl])
        f_inter = jnp.exp(m_prev - n_t)
        e_min = jnp.exp(-(p_ref[P_B, sl] + n_t))
        ccol = ccol_ref[sl, :]
        for h in range(HEADS):
            hs = slice(h * DV, (h + 1) * DV)
            qt = qt_ref[h * DK:(h + 1) * DK, sl]
            k_pair = k_ref[sl, (h // 2) * 2 * DK:(h // 2 + 1) * 2 * DK]
            qt_pair = jnp.concatenate([qt, zq] if h % 2 == 0 else [zq, qt], axis=0)
            st = jnp.dot(k_pair, qt_pair, preferred_element_type=F32)
            pts, qfs = [], []
            for d in range(2):
                r = HEADS * d + h
                arg = jnp.where(visible[d], ccol[:, r:r + 1] - n_t[r:r + 1, :], NEG_INF)
                pts.append((jnp.exp(arg) * st).astype(BF16))
                qfs.append((qt.astype(F32) * f_inter[r:r + 1, :]).astype(BF16))
            rhs = jnp.concatenate([
                jnp.concatenate(pts, axis=1),
                jnp.concatenate([qfs[0], zq], axis=1),
                jnp.concatenate([zq, qfs[1]], axis=1)], axis=0)
            vt_aug = jnp.concatenate([vt_ref[hs, sl], ones_rows], axis=0)
            ct = jnp.where(lane_half == 0, ct_f[chunk, h * DVA:(h + 1) * DVA, :],
                           ct_b[chunk, h * DVA:(h + 1) * DVA, :])
            both = jnp.dot(jnp.concatenate([vt_aug, ct], axis=1), rhs, preferred_element_type=F32)
            ht = None
            for d in range(2):
                r = HEADS * d + h
                numer = both[0:DV, d * L:(d + 1) * L]
                denom = both[DV:DV + 1, d * L:(d + 1) * L]
                part = numer / jnp.maximum(jnp.abs(denom), e_min[r:r + 1, :])
                ht = part if ht is None else ht + part
            ms = jnp.mean(ht * ht, axis=0, keepdims=True)
            hn = ht * lax.rsqrt(ms + EPS) * gain_ref[hs, :]
            y_ref[sl, hs] = (jax.nn.sigmoid(o_ref[sl, hs]) * hn.T).astype(BF16)


def _mlstm(qt, k, ksw, vt, p, ccol, o, gain_b, batch, seq, scan_rows, rows):
    T = batch * seq
    ng = seq // scan_rows
    HK, HV = HEADS * DK, HEADS * DV

    def scan_specs(group_of):
        blk = lambda b, j: b * ng + group_of(j)
        ins = [pl.BlockSpec((HV, scan_rows), lambda b, j: (0, blk(b, j))),
               pl.BlockSpec((scan_rows, HK), lambda b, j: (blk(b, j), 0)),
               pl.BlockSpec((scan_rows, HK), lambda b, j: (blk(b, j), 0)),
               pl.BlockSpec((5 * NGR, scan_rows), lambda b, j: (0, blk(b, j)))]
        outs = [pl.BlockSpec((scan_rows // CHUNK, HEADS * DVA, 128), lambda b, j: (blk(b, j), 0, 0)),
                pl.BlockSpec((NGR, scan_rows), lambda b, j: (0, blk(b, j)))]
        return ins, outs

    ins_f, outs_f = scan_specs(lambda j: j)
    ins_b, outs_b = scan_specs(lambda j: ng - 1 - j)
    state_shapes = [jax.ShapeDtypeStruct((T // CHUNK, HEADS * DVA, 128), BF16),
                    jax.ShapeDtypeStruct((NGR, T), F32)]
    ct_f, mp_f, ct_b, mp_b = pl.pallas_call(
        _mlstm_scan_kernel,
        grid=(batch, ng),
        in_specs=ins_f + ins_b,
        out_specs=outs_f + outs_b,
        out_shape=state_shapes + state_shapes,
        scratch_shapes=[pltpu.VMEM((2, HEADS, DVA, 128), F32), pltpu.VMEM((2, NGR, 128), F32)],
        compiler_params=pltpu.CompilerParams(dimension_semantics=("arbitrary", "arbitrary"),
                                             vmem_limit_bytes=VMEM_LIMIT),
        name="mlstm_scan",
    )(vt, k, ksw, p, vt, k, ksw, p)

    row = lambda w: pl.BlockSpec((rows, w), lambda i: (i, 0))
    col = lambda h: pl.BlockSpec((h, rows), lambda i: (0, i))
    ctb = pl.BlockSpec((rows // CHUNK, HEADS * DVA, 128), lambda i: (i, 0, 0))
    return pl.pallas_call(
        _mlstm_out_kernel,
        grid=(T // rows,),
        in_specs=[col(HK), row(HK), col(HV), col(5 * NGR), col(NGR), col(NGR), row(128), ctb, ctb,
                  row(HV), _resident((HV, 128))],
        out_specs=row(HV),
        out_shape=jax.ShapeDtypeStruct((T, HV), BF16),
        compiler_params=pltpu.CompilerParams(dimension_semantics=("arbitrary",),
                                             vmem_limit_bytes=VMEM_LIMIT),
        name="mlstm_out",
    )(qt, k, vt, p, mp_f, mp_b, ccol, ct_f, ct_b, o, gain_b)


def _mix_out_kernel(tiles_per_seq, x_ref, bg_ref, u_ref, uprev_ref, unext_ref, cw_ref, cb_ref,
                    y_ref, wout_ref, gpost_ref, o_ref):
    i = pl.program_id(0)
    tm = u_ref.shape[0]
    u = u_ref[...]
    has_prev = (i % tiles_per_seq != 0).astype(F32)
    has_next = (i % tiles_per_seq != tiles_per_seq - 1).astype(F32)
    prev_row = uprev_ref[7:8, :] * has_prev
    next_row = unext_ref[0:1, :] * has_next
    ri = lax.broadcasted_iota(jnp.int32, u.shape, 0)
    u_m1 = jnp.where(ri == 0, prev_row, pltpu.roll(u, 1, 0))
    u_p1 = jnp.where(ri == tm - 1, next_row, pltpu.roll(u, tm - 1, 0))
    conv = cw_ref[0:1, :] * u_m1 + cw_ref[1:2, :] * u + cw_ref[2:3, :] * u_p1
    y_conv = (bg_ref[...] * (conv + cb_ref[...])).astype(BF16)
    h = jnp.dot(y_conv, wout_ref[0:CONV_WIDTH, :], preferred_element_type=F32) \
        + jnp.dot(y_ref[...], wout_ref[CONV_WIDTH:, :], preferred_element_type=F32)
    o_ref[...] = x_ref[...] + _rms(h, gpost_ref[...])


def _mix_out(x, bg, u, conv_w, conv_b, y_mlstm, w_out, gpost, seq, tm):
    T = x.shape[0]
    tiles_per_seq = seq // tm
    sub = tm // 8
    last = T // 8 - 1
    row = lambda w: pl.BlockSpec((tm, w), lambda i: (i, 0))
    return pl.pallas_call(
        functools.partial(_mix_out_kernel, tiles_per_seq),
        grid=(T // tm,),
        in_specs=[
            row(D_MODEL), row(CONV_WIDTH), row(CONV_WIDTH),
            pl.BlockSpec((8, CONV_WIDTH), lambda i: (jnp.maximum(i * sub - 1, 0), 0)),
            pl.BlockSpec((8, CONV_WIDTH), lambda i: (jnp.minimum((i + 1) * sub, last), 0)),
            _resident((3, CONV_WIDTH)),
            _resident((1, CONV_WIDTH)),
            row(HEADS * DV),
            _resident((D_MODEL, D_MODEL)),
            _resident((1, D_MODEL)),
        ],
        out_specs=row(D_MODEL),
        out_shape=jax.ShapeDtypeStruct((T, D_MODEL), F32),
        compiler_params=pltpu.CompilerParams(
            dimension_semantics=("arbitrary",), vmem_limit_bytes=VMEM_LIMIT),
        name="mix_out",
    )(x, bg, u, u, u, conv_w, conv_b, y_mlstm, w_out, gpost)


def kernel(x, norm_ffn1_pre, norm_ffn1_post, w_ffn1_in, w_ffn1_out, norm_mix_pre, norm_mix_post,
           w_mix_in, conv_w, conv_b, gate_i_bias, gate_f_bias, mlstm_norm, w_mix_out,
           norm_ffn2_pre, norm_ffn2_post, w_ffn2_in, w_ffn2_out):
    batch, seq, _ = x.shape
    T = batch * seq
    depth = norm_ffn1_pre.shape[0]
    tm = 512
    xt = x.reshape(T, D_MODEL)
    HV = HEADS * DV
    for l in range(depth):
        xt = _ffn(xt, norm_ffn1_pre[l][None], norm_ffn1_post[l][None], w_ffn1_in, w_ffn1_out, l, tm)

        gbias = jnp.pad(jnp.concatenate([gate_i_bias[l], gate_f_bias[l]]), (0, 128 - 2 * NGR))[None]
        bg, u, qt, k, ksw, vt, o, p, ccol = _mix_in(
            xt, norm_mix_pre[l][None], jnp.swapaxes(w_mix_in, 1, 2), gbias, l, tm)
        gain_b = jnp.broadcast_to(mlstm_norm[l][:, None], (HV, 128))
        y_mlstm = _mlstm(qt, k, ksw, vt, p, ccol, o, gain_b, batch, seq, SCAN_ROWS, MLSTM_ROWS)
        xt = _mix_out(xt, bg, u, conv_w[l], conv_b[l][None], y_mlstm,
                      w_mix_out[l].astype(BF16), norm_mix_post[l][None], seq, tm)

        xt = _ffn(xt, norm_ffn2_pre[l][None], norm_ffn2_post[l][None], w_ffn2_in, w_ffn2_out, l, tm)
    return xt.reshape(batch, seq, D_MODEL)
```

```python
import functools

import jax
import jax.numpy as jnp
from jax import lax
from jax.experimental import pallas as pl
from jax.experimental.pallas import tpu as pltpu

D_MODEL = 1024
D_FF = 2816
CONV_WIDTH = 512
HEADS = 4
DK = 64
DV = 128
CHUNK = 128
EPS = 1e-6
NEG_INF = -1e30

FF_TILE = 256
FF_STAGE_CHUNKS = 8
MIX_STAGE_CHUNKS = 8
MLSTM_ROWS = 512
SCAN_ROWS = 2048
DVA = DV + 16
NGR = 2 * HEADS
CT_ROWS = (HEADS // 2) * DVA
VMEM_LIMIT = 56 * 1024 * 1024

P_E, P_M, P_B, P_G, P_MC = (slice(i * NGR, (i + 1) * NGR) for i in range(5))

F32 = jnp.float32
BF16 = jnp.bfloat16


def _rms(x, g):
    return x * lax.rsqrt(jnp.mean(x * x, axis=-1, keepdims=True) + EPS) * g


def _log_sigmoid(z):
    return jnp.minimum(z, 0.0) - jnp.log1p(jnp.exp(-jnp.abs(z)))


def _resident(shape):
    zeros = (0,) * len(shape)
    return pl.BlockSpec(shape, lambda *_: zeros, pipeline_mode=pl.Buffered(1))


def _split3(x):
    hi = x.astype(BF16)
    r1 = x - hi.astype(F32)
    mid = r1.astype(BF16)
    lo = (r1 - mid.astype(F32)).astype(BF16)
    return hi, mid, lo


def _visible(rev):
    s = lax.broadcasted_iota(jnp.int32, (CHUNK, CHUNK), 0)
    t = lax.broadcasted_iota(jnp.int32, (CHUNK, CHUNK), 1)
    return (s >= t) if rev else (s <= t)


def _running_max(x, rev):
    n = x.shape[1]
    pos = lax.broadcasted_iota(jnp.int32, x.shape, 1) & (CHUNK - 1)
    k = 1
    while k < CHUNK:
        if rev:
            shifted, ok = pltpu.roll(x, n - k, 1), pos < CHUNK - k
        else:
            shifted, ok = pltpu.roll(x, k, 1), pos >= k
        x = jnp.maximum(x, jnp.where(ok, shifted, NEG_INF))
        k *= 2
    return x


def _stage_bf16(src_hbm, dst_ref, stage_ref, sem):
    rows = stage_ref.shape[1]
    n_chunks = src_hbm.shape[0] // rows

    def copy(c):
        return pltpu.make_async_copy(src_hbm.at[pl.ds(c * rows, rows), :], stage_ref.at[c % 2], sem.at[c % 2])

    copy(0).start()
    for c in range(n_chunks):
        if c + 1 < n_chunks:
            copy(c + 1).start()
        copy(c).wait()
        dst_ref[c * rows:(c + 1) * rows, :] = stage_ref[c % 2].astype(BF16)


def _ffn_kernel(layer, x_ref, gpre_ref, gpost_ref, win_hbm, wout_hbm, o_ref,
                h_ref, win_ref, wout_ref, stage_in, stage_out, sem):
    @pl.when(pl.program_id(0) == 0)
    def _():
        _stage_bf16(win_hbm.at[layer], win_ref, stage_in, sem)
        _stage_bf16(wout_hbm.at[layer], wout_ref, stage_out, sem)

    x = x_ref[...]
    xn = _rms(x, gpre_ref[...]).astype(BF16)
    for j in range(D_FF // FF_TILE):
        lo = j * FF_TILE
        gate = jnp.dot(xn, win_ref[:, lo:lo + FF_TILE], preferred_element_type=F32)
        up = jnp.dot(xn, win_ref[:, D_FF + lo:D_FF + lo + FF_TILE], preferred_element_type=F32)
        h_ref[:, lo:lo + FF_TILE] = (gate * jax.nn.sigmoid(gate) * up).astype(BF16)
    y = jnp.dot(h_ref[...], wout_ref[...], preferred_element_type=F32)
    o_ref[...] = x + 0.5 * _rms(y, gpost_ref[...])


def _ffn(x, gpre, gpost, w_in, w_out, layer, tm):
    T = x.shape[0]
    return pl.pallas_call(
        functools.partial(_ffn_kernel, layer),
        grid=(T // tm,),
        in_specs=[
            pl.BlockSpec((tm, D_MODEL), lambda i: (i, 0)),
            _resident((1, D_MODEL)),
            _resident((1, D_MODEL)),
            pl.BlockSpec(memory_space=pl.ANY),
            pl.BlockSpec(memory_space=pl.ANY),
        ],
        out_specs=pl.BlockSpec((tm, D_MODEL), lambda i: (i, 0)),
        out_shape=jax.ShapeDtypeStruct((T, D_MODEL), F32),
        scratch_shapes=[
            pltpu.VMEM((tm, D_FF), BF16),
            pltpu.VMEM((D_MODEL, 2 * D_FF), BF16),
            pltpu.VMEM((D_FF, D_MODEL), BF16),
            pltpu.VMEM((2, D_MODEL // FF_STAGE_CHUNKS, 2 * D_FF), F32),
            pltpu.VMEM((2, D_FF // FF_STAGE_CHUNKS, D_MODEL), F32),
            pltpu.SemaphoreType.DMA((2,)),
        ],
        compiler_params=pltpu.CompilerParams(
            dimension_semantics=("arbitrary",), vmem_limit_bytes=VMEM_LIMIT),
        name="ffn",
    )(x, gpre, gpost, w_in, w_out)


def _gate_rows(zr, p_ref, ccol_ref):
    L = CHUNK
    n_chunks = zr.shape[1] // L
    fwd_row = lax.broadcasted_iota(jnp.int32, (NGR, L), 0) < HEADS
    fwd_col = lax.broadcasted_iota(jnp.int32, (NGR, 1), 0) < HEADS
    li = zr[0:NGR]
    gates = jnp.concatenate([li, _log_sigmoid(zr[NGR:2 * NGR])], axis=0)
    x3 = jnp.concatenate(_split3(gates), axis=0)
    stacked = jnp.concatenate([x3[:, c * L:(c + 1) * L] for c in range(n_chunks)], axis=0)
    cum_ops = jnp.concatenate([_visible(False), _visible(True)], axis=1).astype(BF16)
    cum = jnp.dot(stacked, cum_ops, preferred_element_type=F32)
    b_chunks = []
    for c in range(n_chunks):
        blk = cum[c * 48:(c + 1) * 48]
        s16 = blk[0:16] + blk[16:32] + blk[32:48]
        b_chunks.append(jnp.where(fwd_row, s16[NGR:2 * NGR, 0:L], s16[NGR:2 * NGR, L:2 * L]))
    b = jnp.concatenate(b_chunks, axis=1)
    cc = li - b
    fwd_all = lax.broadcasted_iota(jnp.int32, cc.shape, 0) < HEADS
    p_ref[P_M, :] = jnp.where(fwd_all, _running_max(cc, False), _running_max(cc, True))
    p_ref[P_B, :] = b
    pad = jnp.zeros((L - NGR, L), F32)
    for c in range(n_chunks):
        sl = slice(c * L, (c + 1) * L)
        b_c, cc_c = b_chunks[c], cc[:, sl]
        g = jnp.where(fwd_col, b_c[:, L - 1:L], b_c[:, 0:1])
        m_chunk = g + jnp.max(cc_c, axis=1, keepdims=True)
        p_ref[P_E, sl] = jnp.exp(g + cc_c - m_chunk)
        p_ref[P_G, sl] = jnp.broadcast_to(g, (NGR, L))
        p_ref[P_MC, sl] = jnp.broadcast_to(m_chunk, (NGR, L))
        ccol_ref[sl, :] = jnp.concatenate([cc_c, pad], axis=0).T


def _mix_in_kernel(layer, x_ref, gpre_ref, w_hbm, gbias_ref,
                   bg_ref, u_ref, qt_ref, k_ref, ksw_ref, vt_ref, o_ref, p_ref, ccol_ref,
                   wb_ref, stage_ref, sem):
    W, HK, HV = CONV_WIDTH, HEADS * DK, HEADS * DV
    q0 = 3 * W
    k0, v0 = q0 + HK, q0 + 2 * HK
    o0 = v0 + HV
    g0 = o0 + HV

    @pl.when(pl.program_id(0) == 0)
    def _():
        src = w_hbm.at[layer]
        rows = stage_ref.shape[1]
        n_chunks = g0 // rows

        def copy(c):
            n = rows if c < n_chunks else 2 * NGR
            return pltpu.make_async_copy(src.at[pl.ds(c * rows, n), :], stage_ref.at[c % 2, pl.ds(0, n), :],
                                         sem.at[c % 2])

        copy(0).start()
        for c in range(n_chunks + 1):
            if c < n_chunks:
                copy(c + 1).start()
            copy(c).wait()
            if c < n_chunks:
                wb_ref[:, c * rows:(c + 1) * rows] = stage_ref[c % 2].T.astype(BF16)
            else:
                gt = stage_ref[c % 2, 0:128, :].T
                lane = lax.broadcasted_iota(jnp.int32, gt.shape, 1)
                wb_ref[:, g0:g0 + 128] = jnp.where(lane < 2 * NGR, gt, 0.0).astype(BF16)

    xn = _rms(x_ref[...], gpre_ref[...]).astype(BF16)
    proj = lambda a, b: jnp.dot(xn, wb_ref[:, a:b], preferred_element_type=F32)

    zg = proj(g0, g0 + 128) + gbias_ref[...]
    bg_ref[...] = proj(0, W)
    u_ref[...] = proj(W, 2 * W) * proj(2 * W, 3 * W)
    _gate_rows(zg.T[0:2 * NGR], p_ref, ccol_ref)
    qt_ref[...] = (proj(q0, k0) * (DK ** -0.5)).T.astype(BF16)
    kk = proj(k0, v0)
    k_ref[...] = kk.astype(BF16)
    for pair in range(HEADS // 2):
        ps = slice(pair * 2 * DK, (pair + 1) * 2 * DK)
        ksw_ref[:, ps] = pltpu.roll(kk[:, ps], DK, 1).astype(BF16)
    vt_ref[...] = proj(v0, o0).T.astype(BF16)
    o_ref[...] = proj(o0, g0)


def _mix_in(x, gpre, w, gbias, layer, tm):
    T = x.shape[0]
    row = lambda w: pl.BlockSpec((tm, w), lambda i: (i, 0))
    col = lambda h: pl.BlockSpec((h, tm), lambda i: (0, i))
    HK, HV = HEADS * DK, HEADS * DV
    g0 = 3 * CONV_WIDTH + 2 * HK + 2 * HV
    return pl.pallas_call(
        functools.partial(_mix_in_kernel, layer),
        grid=(T // tm,),
        in_specs=[row(D_MODEL), _resident((1, D_MODEL)), pl.BlockSpec(memory_space=pl.ANY),
                  _resident((1, 128))],
        scratch_shapes=[pltpu.VMEM((D_MODEL, g0 + 128), BF16),
                        pltpu.VMEM((2, g0 // MIX_STAGE_CHUNKS, D_MODEL), F32),
                        pltpu.SemaphoreType.DMA((2,))],
        out_specs=[row(CONV_WIDTH), row(CONV_WIDTH), col(HK), row(HK), row(HK), col(HV), row(HV),
                   col(5 * NGR), row(128)],
        out_shape=[
            jax.ShapeDtypeStruct((T, CONV_WIDTH), F32),
            jax.ShapeDtypeStruct((T, CONV_WIDTH), F32),
            jax.ShapeDtypeStruct((HK, T), BF16),
            jax.ShapeDtypeStruct((T, HK), BF16),
            jax.ShapeDtypeStruct((T, HK), BF16),
            jax.ShapeDtypeStruct((HV, T), BF16),
            jax.ShapeDtypeStruct((T, HV), F32),
            jax.ShapeDtypeStruct((5 * NGR, T), F32),
            jax.ShapeDtypeStruct((T, 128), F32),
        ],
        compiler_params=pltpu.CompilerParams(
            dimension_semantics=("arbitrary",), vmem_limit_bytes=VMEM_LIMIT),
        name="mix_in",
    )(x, gpre, w, gbias)


def _state_half(d, h):
    return (h % 2) ^ d


def _scan_direction(d, vt_ref, k_ref, p_ref, ct_ref, mp_ref, ct_state, m_state):
    L = CHUNK
    n_chunks = k_ref.shape[0] // L
    lane_half = lax.broadcasted_iota(jnp.int32, (L, 128), 1) // DK
    lane_half_s = lax.broadcasted_iota(jnp.int32, (DVA, 128), 1) // DK
    ones_rows = jnp.ones((DVA - DV, L), BF16)

    cts = [ct_state[d, h] for h in range(HEADS)]
    m_prev = m_state[d]
    for chunk in (range(n_chunks - 1, -1, -1) if d == 1 else range(n_chunks)):
        sl = slice(chunk * L, (chunk + 1) * L)
        e, g, m_chunk = p_ref[P_E, sl], p_ref[P_G, sl], p_ref[P_MC, sl]
        mp_ref[:, sl] = m_prev
        m_new = jnp.maximum(g + m_prev, m_chunk)
        a_old = jnp.exp(g + m_prev - m_new)
        a_new = jnp.exp(m_chunk - m_new)
        new_cts = []
        for h in range(HEADS):
            r = HEADS * d + h
            vt_aug = jnp.concatenate([vt_ref[h * DV:(h + 1) * DV, sl], ones_rows], axis=0)
            vte = (vt_aug.astype(F32) * e[r:r + 1, :]).astype(BF16)
            pair = slice((h // 2) * 2 * DK, (h // 2 + 1) * 2 * DK)
            k_half = jnp.where(lane_half == _state_half(d, h), k_ref[sl, pair], jnp.zeros((L, 128), BF16))
            ct_chunk = jnp.dot(vte, k_half, preferred_element_type=F32)
            new_cts.append(a_old[r:r + 1, 0:1] * cts[h] + a_new[r:r + 1, 0:1] * ct_chunk)
        for pr in range(HEADS // 2):
            both = jnp.where(lane_half_s == _state_half(d, 2 * pr), cts[2 * pr], cts[2 * pr + 1])
            ct_ref[chunk, pr * DVA:(pr + 1) * DVA, :] = both.astype(BF16)
        cts = new_cts
        m_prev = m_new
    for h in range(HEADS):
        ct_state[d, h] = cts[h]
    m_state[d] = m_prev


def _mlstm_scan_kernel(vt_f, k_f, p_f, vt_b, ksw_b, p_b, ct_f, mp_f, ct_b, mp_b, ct_state, m_state):
    @pl.when(pl.program_id(1) == 0)
    def _():
        ct_state[...] = jnp.zeros(ct_state.shape, F32)
        m_state[...] = jnp.full(m_state.shape, NEG_INF, F32)

    _scan_direction(0, vt_f, k_f, p_f, ct_f, mp_f, ct_state, m_state)
    _scan_direction(1, vt_b, ksw_b, p_b, ct_b, mp_b, ct_state, m_state)


def _mlstm_out_kernel(qt_ref, k_ref, vt_ref, p_ref, mp_f, mp_b, ccol_ref, ct_f, ct_b,
                      o_ref, gain_ref, y_ref):
    L = CHUNK
    n_chunks = k_ref.shape[0] // L
    visible = (_visible(False), _visible(True))
    fwd_row = lax.broadcasted_iota(jnp.int32, (NGR, L), 0) < HEADS
    lane_half = lax.broadcasted_iota(jnp.int32, (DVA, 128), 1) // DK
    ones_rows = jnp.ones((DVA - DV, L), BF16)
    zq = jnp.zeros((DK, L), BF16)
    for chunk in range(n_chunks):
        sl = slice(chunk * L, (chunk + 1) * L)
        m_prev = jnp.where(fwd_row, mp_f[:, sl], mp_b[:, sl])
        n_t = jnp.maximum(m_prev, p_ref[P_M, sl])
        f_inter = jnp.exp(m_prev - n_t)
        e_min = jnp.exp(-(p_ref[P_B, sl] + n_t))
        ccol = ccol_ref[sl, :]
        for h in range(HEADS):
            hs = slice(h * DV, (h + 1) * DV)
            qt = qt_ref[h * DK:(h + 1) * DK, sl]
            k_pair = k_ref[sl, (h // 2) * 2 * DK:(h // 2 + 1) * 2 * DK]
            qt_pair = jnp.concatenate([qt, zq] if h % 2 == 0 else [zq, qt], axis=0)
            st = jnp.dot(k_pair, qt_pair, preferred_element_type=F32)
            pts, qfs = [], []
            for d in range(2):
                r = HEADS * d + h
                arg = jnp.where(visible[d], ccol[:, r:r + 1] - n_t[r:r + 1, :], NEG_INF)
                pts.append((jnp.exp(arg) * st).astype(BF16))
                qfs.append((qt.astype(F32) * f_inter[r:r + 1, :]).astype(BF16))
            inter = [jnp.concatenate([qfs[0], zq], axis=1), jnp.concatenate([zq, qfs[1]], axis=1)]
            first = 0 if _state_half(0, h) == 0 else 1
            rhs = jnp.concatenate([jnp.concatenate(pts, axis=1), inter[first], inter[1 - first]], axis=0)
            vt_aug = jnp.concatenate([vt_ref[hs, sl], ones_rows], axis=0)
            ps = slice((h // 2) * DVA, (h // 2 + 1) * DVA)
            blocks = (ct_f[chunk, ps, :], ct_b[chunk, ps, :])
            ct = jnp.where(lane_half == 0, blocks[first], blocks[1 - first])
            both = jnp.dot(jnp.concatenate([vt_aug, ct], axis=1), rhs, preferred_element_type=F32)
            ht = None
            for d in range(2):
                r = HEADS * d + h
                numer = both[0:DV, d * L:(d + 1) * L]
                denom = both[DV:DV + 1, d * L:(d + 1) * L]
                part = numer / jnp.maximum(jnp.abs(denom), e_min[r:r + 1, :])
                ht = part if ht is None else ht + part
            ms = jnp.mean(ht * ht, axis=0, keepdims=True)
            hn = ht * lax.rsqrt(ms + EPS) * gain_ref[hs, :]
            y_ref[sl, hs] = (jax.nn.sigmoid(o_ref[sl, hs]) * hn.T).astype(BF16)


def _mlstm(qt, k, ksw, vt, p, ccol, o, gain_b, batch, seq, scan_rows, rows):
    T = batch * seq
    ng = seq // scan_rows
    HK, HV = HEADS * DK, HEADS * DV

    def scan_specs(group_of):
        blk = lambda b, j: b * ng + group_of(j)
        ins = [pl.BlockSpec((HV, scan_rows), lambda b, j: (0, blk(b, j))),
               pl.BlockSpec((scan_rows, HK), lambda b, j: (blk(b, j), 0)),
               pl.BlockSpec((5 * NGR, scan_rows), lambda b, j: (0, blk(b, j)))]
        outs = [pl.BlockSpec((scan_rows // CHUNK, CT_ROWS, 128), lambda b, j: (blk(b, j), 0, 0)),
                pl.BlockSpec((NGR, scan_rows), lambda b, j: (0, blk(b, j)))]
        return ins, outs

    ins_f, outs_f = scan_specs(lambda j: j)
    ins_b, outs_b = scan_specs(lambda j: ng - 1 - j)
    state_shapes = [jax.ShapeDtypeStruct((T // CHUNK, CT_ROWS, 128), BF16),
                    jax.ShapeDtypeStruct((NGR, T), F32)]
    ct_f, mp_f, ct_b, mp_b = pl.pallas_call(
        _mlstm_scan_kernel,
        grid=(batch, ng),
        in_specs=ins_f + ins_b,
        out_specs=outs_f + outs_b,
        out_shape=state_shapes + state_shapes,
        scratch_shapes=[pltpu.VMEM((2, HEADS, DVA, 128), F32), pltpu.VMEM((2, NGR, 128), F32)],
        compiler_params=pltpu.CompilerParams(dimension_semantics=("arbitrary", "arbitrary"),
                                             vmem_limit_bytes=VMEM_LIMIT),
        name="mlstm_scan",
    )(vt, k, p, vt, ksw, p)

    row = lambda w: pl.BlockSpec((rows, w), lambda i: (i, 0))
    col = lambda h: pl.BlockSpec((h, rows), lambda i: (0, i))
    ctb = pl.BlockSpec((rows // CHUNK, CT_ROWS, 128), lambda i: (i, 0, 0))
    return pl.pallas_call(
        _mlstm_out_kernel,
        grid=(T // rows,),
        in_specs=[col(HK), row(HK), col(HV), col(5 * NGR), col(NGR), col(NGR), row(128), ctb, ctb,
                  row(HV), _resident((HV, 128))],
        out_specs=row(HV),
        out_shape=jax.ShapeDtypeStruct((T, HV), BF16),
        compiler_params=pltpu.CompilerParams(dimension_semantics=("arbitrary",),
                                             vmem_limit_bytes=VMEM_LIMIT),
        name="mlstm_out",
    )(qt, k, vt, p, mp_f, mp_b, ccol, ct_f, ct_b, o, gain_b)


def _mix_out_kernel(tiles_per_seq, x_ref, bg_ref, u_ref, uprev_ref, unext_ref, cw_ref, cb_ref,
                    y_ref, wout_ref, gpost_ref, o_ref):
    i = pl.program_id(0)
    tm = u_ref.shape[0]
    u = u_ref[...]
    has_prev = (i % tiles_per_seq != 0).astype(F32)
    has_next = (i % tiles_per_seq != tiles_per_seq - 1).astype(F32)
    prev_row = uprev_ref[7:8, :] * has_prev
    next_row = unext_ref[0:1, :] * has_next
    ri = lax.broadcasted_iota(jnp.int32, u.shape, 0)
    u_m1 = jnp.where(ri == 0, prev_row, pltpu.roll(u, 1, 0))
    u_p1 = jnp.where(ri == tm - 1, next_row, pltpu.roll(u, tm - 1, 0))
    conv = cw_ref[0:1, :] * u_m1 + cw_ref[1:2, :] * u + cw_ref[2:3, :] * u_p1
    y_conv = (bg_ref[...] * (conv + cb_ref[...])).astype(BF16)
    h = jnp.dot(y_conv, wout_ref[0:CONV_WIDTH, :], preferred_element_type=F32) \
        + jnp.dot(y_ref[...], wout_ref[CONV_WIDTH:, :], preferred_element_type=F32)
    o_ref[...] = x_ref[...] + _rms(h, gpost_ref[...])


def _mix_out(x, bg, u, conv_w, conv_b, y_mlstm, w_out, gpost, seq, tm):
    T = x.shape[0]
    tiles_per_seq = seq // tm
    sub = tm // 8
    last = T // 8 - 1
    row = lambda w: pl.BlockSpec((tm, w), lambda i: (i, 0))
    return pl.pallas_call(
        functools.partial(_mix_out_kernel, tiles_per_seq),
        grid=(T // tm,),
        in_specs=[
            row(D_MODEL), row(CONV_WIDTH), row(CONV_WIDTH),
            pl.BlockSpec((8, CONV_WIDTH), lambda i: (jnp.maximum(i * sub - 1, 0), 0)),
            pl.BlockSpec((8, CONV_WIDTH), lambda i: (jnp.minimum((i + 1) * sub, last), 0)),
            _resident((3, CONV_WIDTH)),
            _resident((1, CONV_WIDTH)),
            row(HEADS * DV),
            _resident((D_MODEL, D_MODEL)),
            _resident((1, D_MODEL)),
        ],
        out_specs=row(D_MODEL),
        out_shape=jax.ShapeDtypeStruct((T, D_MODEL), F32),
        compiler_params=pltpu.CompilerParams(
            dimension_semantics=("arbitrary",), vmem_limit_bytes=VMEM_LIMIT),
        name="mix_out",
    )(x, bg, u, u, u, conv_w, conv_b, y_mlstm, w_out, gpost)


def kernel(x, norm_ffn1_pre, norm_ffn1_post, w_ffn1_in, w_ffn1_out, norm_mix_pre, norm_mix_post,
           w_mix_in, conv_w, conv_b, gate_i_bias, gate_f_bias, mlstm_norm, w_mix_out,
           norm_ffn2_pre, norm_ffn2_post, w_ffn2_in, w_ffn2_out):
    batch, seq, _ = x.shape
    T = batch * seq
    depth = norm_ffn1_pre.shape[0]
    tm = 512
    xt = x.reshape(T, D_MODEL)
    HV = HEADS * DV
    for l in range(depth):
        xt = _ffn(xt, norm_ffn1_pre[l][None], norm_ffn1_post[l][None], w_ffn1_in, w_ffn1_out, l, tm)

        gbias = jnp.pad(jnp.concatenate([gate_i_bias[l], gate_f_bias[l]]), (0, 128 - 2 * NGR))[None]
        bg, u, qt, k, ksw, vt, o, p, ccol = _mix_in(
            xt, norm_mix_pre[l][None], jnp.swapaxes(w_mix_in, 1, 2), gbias, l, tm)
        gain_b = jnp.broadcast_to(mlstm_norm[l][:, None], (HV, 128))
        y_mlstm = _mlstm(qt, k, ksw, vt, p, ccol, o, gain_b, batch, seq, SCAN_ROWS, MLSTM_ROWS)
        xt = _mix_out(xt, bg, u, conv_w[l], conv_b[l][None], y_mlstm,
                      w_mix_out[l].astype(BF16), norm_mix_post[l][None], seq, tm)

        xt = _ffn(xt, norm_ffn2_pre[l][None], norm_ffn2_post[l][None], w_ffn2_in, w_ffn2_out, l, tm)
    return xt.reshape(batch, seq, D_MODEL)
```

```python
import functools

import jax
import jax.numpy as jnp
from jax import lax
from jax.experimental import pallas as pl
from jax.experimental.pallas import tpu as pltpu

D_MODEL = 1024
D_FF = 2816
CONV_WIDTH = 512
HEADS = 4
DK = 64
DV = 128
CHUNK = 128
EPS = 1e-6
NEG_INF = -1e30

FF_TILE = 256
FF_STAGE_CHUNKS = 8
MIX_STAGE_CHUNKS = 8
MLSTM_ROWS = 512
SCAN_ROWS = 2048
DVA = DV + 16
NGR = 2 * HEADS
CT_ROWS = (HEADS // 2) * DVA
VMEM_LIMIT = 56 * 1024 * 1024

P_E, P_M, P_B, P_G, P_MC = (slice(i * NGR, (i + 1) * NGR) for i in range(5))

F32 = jnp.float32
BF16 = jnp.bfloat16


def _rms(x, g):
    return x * lax.rsqrt(jnp.mean(x * x, axis=-1, keepdims=True) + EPS) * g


def _log_sigmoid(z):
    return jnp.minimum(z, 0.0) - jnp.log1p(jnp.exp(-jnp.abs(z)))


def _resident(shape):
    zeros = (0,) * len(shape)
    return pl.BlockSpec(shape, lambda *_: zeros, pipeline_mode=pl.Buffered(1))


def _split3(x):
    hi = x.astype(BF16)
    r1 = x - hi.astype(F32)
    mid = r1.astype(BF16)
    lo = (r1 - mid.astype(F32)).astype(BF16)
    return hi, mid, lo


def _visible(rev):
    s = lax.broadcasted_iota(jnp.int32, (CHUNK, CHUNK), 0)
    t = lax.broadcasted_iota(jnp.int32, (CHUNK, CHUNK), 1)
    return (s >= t) if rev else (s <= t)


def _running_max(x, rev):
    n = x.shape[1]
    pos = lax.broadcasted_iota(jnp.int32, x.shape, 1) & (CHUNK - 1)
    k = 1
    while k < CHUNK:
        if rev:
            shifted, ok = pltpu.roll(x, n - k, 1), pos < CHUNK - k
        else:
            shifted, ok = pltpu.roll(x, k, 1), pos >= k
        x = jnp.maximum(x, jnp.where(ok, shifted, NEG_INF))
        k *= 2
    return x


def _stage_bf16(src_hbm, dst_ref, stage_ref, sem, rows=None):
    rows = stage_ref.shape[1] if rows is None else rows
    n_chunks = src_hbm.shape[0] // rows

    def copy(c):
        return pltpu.make_async_copy(src_hbm.at[pl.ds(c * rows, rows), :],
                                     stage_ref.at[c % 2, pl.ds(0, rows), :], sem.at[c % 2])

    copy(0).start()
    for c in range(n_chunks):
        if c + 1 < n_chunks:
            copy(c + 1).start()
        copy(c).wait()
        dst_ref[c * rows:(c + 1) * rows, :] = stage_ref[c % 2, 0:rows, :].astype(BF16)


def _mix_out_tile(tiles_per_seq, x_ref, bg_ref, u_ref, uprev_ref, unext_ref, cw_ref, cb_ref,
                  y_ref, wmo_ref, gmix_ref):
    i = pl.program_id(0)
    tm = u_ref.shape[0]
    u = u_ref[...]
    has_prev = (i % tiles_per_seq != 0).astype(F32)
    has_next = (i % tiles_per_seq != tiles_per_seq - 1).astype(F32)
    prev_row = uprev_ref[7:8, :] * has_prev
    next_row = unext_ref[0:1, :] * has_next
    ri = lax.broadcasted_iota(jnp.int32, u.shape, 0)
    u_m1 = jnp.where(ri == 0, prev_row, pltpu.roll(u, 1, 0))
    u_p1 = jnp.where(ri == tm - 1, next_row, pltpu.roll(u, tm - 1, 0))
    conv = cw_ref[0:1, :] * u_m1 + cw_ref[1:2, :] * u + cw_ref[2:3, :] * u_p1
    y_conv = (bg_ref[...] * (conv + cb_ref[...])).astype(BF16)
    h = jnp.dot(y_conv, wmo_ref[0:CONV_WIDTH, :], preferred_element_type=F32) \
        + jnp.dot(y_ref[...], wmo_ref[CONV_WIDTH:, :], preferred_element_type=F32)
    return x_ref[...] + _rms(h, gmix_ref[...])


def _ffn_tile(x, gpre_ref, gpost_ref, win_ref, wout_ref, h_ref):
    xn = _rms(x, gpre_ref[...]).astype(BF16)
    for j in range(D_FF // FF_TILE):
        lo = j * FF_TILE
        gate = jnp.dot(xn, win_ref[:, lo:lo + FF_TILE], preferred_element_type=F32)
        up = jnp.dot(xn, win_ref[:, D_FF + lo:D_FF + lo + FF_TILE], preferred_element_type=F32)
        h_ref[:, lo:lo + FF_TILE] = (gate * jax.nn.sigmoid(gate) * up).astype(BF16)
    y = jnp.dot(h_ref[...], wout_ref[...], preferred_element_type=F32)
    return x + 0.5 * _rms(y, gpost_ref[...])


def _ffn_kernel(layer, x_ref, gpre_ref, gpost_ref, win_hbm, wout_hbm, o_ref,
                h_ref, win_ref, wout_ref, stage_in, stage_out, sem):
    @pl.when(pl.program_id(0) == 0)
    def _():
        _stage_bf16(win_hbm.at[layer], win_ref, stage_in, sem)
        _stage_bf16(wout_hbm.at[layer], wout_ref, stage_out, sem)

    o_ref[...] = _ffn_tile(x_ref[...], gpre_ref, gpost_ref, win_ref, wout_ref, h_ref)


def _mix_ffn_kernel(layer, tiles_per_seq, x_ref, bg_ref, u_ref, uprev_ref, unext_ref, cw_ref, cb_ref,
                    y_ref, gmix_ref, wmo_hbm, gpre_ref, gpost_ref, win_hbm, wout_hbm, o_ref,
                    h_ref, win_ref, wout_ref, wmo_ref, stage_in, stage_out, sem):
    @pl.when(pl.program_id(0) == 0)
    def _():
        _stage_bf16(wmo_hbm.at[layer], wmo_ref, stage_out, sem, rows=D_MODEL // 4)
        _stage_bf16(win_hbm.at[layer], win_ref, stage_in, sem)
        _stage_bf16(wout_hbm.at[layer], wout_ref, stage_out, sem)

    x = _mix_out_tile(tiles_per_seq, x_ref, bg_ref, u_ref, uprev_ref, unext_ref, cw_ref, cb_ref,
                      y_ref, wmo_ref, gmix_ref)
    o_ref[...] = _ffn_tile(x, gpre_ref, gpost_ref, win_ref, wout_ref, h_ref)


def _ffn_scratch(tm):
    return [
        pltpu.VMEM((tm, D_FF), BF16),
        pltpu.VMEM((D_MODEL, 2 * D_FF), BF16),
        pltpu.VMEM((D_FF, D_MODEL), BF16),
    ], [
        pltpu.VMEM((2, D_MODEL // FF_STAGE_CHUNKS, 2 * D_FF), F32),
        pltpu.VMEM((2, D_FF // FF_STAGE_CHUNKS, D_MODEL), F32),
        pltpu.SemaphoreType.DMA((2,)),
    ]


def _ffn(x, gpre, gpost, w_in, w_out, layer, tm):
    T = x.shape[0]
    resident, staging = _ffn_scratch(tm)
    return pl.pallas_call(
        functools.partial(_ffn_kernel, layer),
        grid=(T // tm,),
        in_specs=[
            pl.BlockSpec((tm, D_MODEL), lambda i: (i, 0)),
            _resident((1, D_MODEL)),
            _resident((1, D_MODEL)),
            pl.BlockSpec(memory_space=pl.ANY),
            pl.BlockSpec(memory_space=pl.ANY),
        ],
        out_specs=pl.BlockSpec((tm, D_MODEL), lambda i: (i, 0)),
        out_shape=jax.ShapeDtypeStruct((T, D_MODEL), F32),
        scratch_shapes=resident + staging,
        compiler_params=pltpu.CompilerParams(
            dimension_semantics=("arbitrary",), vmem_limit_bytes=VMEM_LIMIT),
        name="ffn",
    )(x, gpre, gpost, w_in, w_out)


def _mix_ffn(x, bg, u, conv_w, conv_b, y_mlstm, gmix, w_mix_out, gpre, gpost, w_in, w_out, layer, seq, tm):
    T = x.shape[0]
    tiles_per_seq = seq // tm
    sub = tm // 8
    last = T // 8 - 1
    row = lambda w: pl.BlockSpec((tm, w), lambda i: (i, 0))
    hbm = pl.BlockSpec(memory_space=pl.ANY)
    resident, staging = _ffn_scratch(tm)
    return pl.pallas_call(
        functools.partial(_mix_ffn_kernel, layer, tiles_per_seq),
        grid=(T // tm,),
        in_specs=[
            row(D_MODEL), row(CONV_WIDTH), row(CONV_WIDTH),
            pl.BlockSpec((8, CONV_WIDTH), lambda i: (jnp.maximum(i * sub - 1, 0), 0)),
            pl.BlockSpec((8, CONV_WIDTH), lambda i: (jnp.minimum((i + 1) * sub, last), 0)),
            _resident((3, CONV_WIDTH)), _resident((1, CONV_WIDTH)),
            row(HEADS * DV), _resident((1, D_MODEL)), hbm,
            _resident((1, D_MODEL)), _resident((1, D_MODEL)), hbm, hbm,
        ],
        out_specs=row(D_MODEL),
        out_shape=jax.ShapeDtypeStruct((T, D_MODEL), F32),
        scratch_shapes=resident + [pltpu.VMEM((D_MODEL, D_MODEL), BF16)] + staging,
        compiler_params=pltpu.CompilerParams(
            dimension_semantics=("arbitrary",), vmem_limit_bytes=VMEM_LIMIT),
        name="mix_ffn",
    )(x, bg, u, u, u, conv_w, conv_b, y_mlstm, gmix, w_mix_out, gpre, gpost, w_in, w_out)


def _gate_rows(zr, p_ref, ccol_ref):
    L = CHUNK
    n_chunks = zr.shape[1] // L
    fwd_row = lax.broadcasted_iota(jnp.int32, (NGR, L), 0) < HEADS
    fwd_col = lax.broadcasted_iota(jnp.int32, (NGR, 1), 0) < HEADS
    li = zr[0:NGR]
    gates = jnp.concatenate([li, _log_sigmoid(zr[NGR:2 * NGR])], axis=0)
    x3 = jnp.concatenate(_split3(gates), axis=0)
    stacked = jnp.concatenate([x3[:, c * L:(c + 1) * L] for c in range(n_chunks)], axis=0)
    cum_ops = jnp.concatenate([_visible(False), _visible(True)], axis=1).astype(BF16)
    cum = jnp.dot(stacked, cum_ops, preferred_element_type=F32)
    b_chunks = []
    for c in range(n_chunks):
        blk = cum[c * 48:(c + 1) * 48]
        s16 = blk[0:16] + blk[16:32] + blk[32:48]
        b_chunks.append(jnp.where(fwd_row, s16[NGR:2 * NGR, 0:L], s16[NGR:2 * NGR, L:2 * L]))
    b = jnp.concatenate(b_chunks, axis=1)
    cc = li - b
    fwd_all = lax.broadcasted_iota(jnp.int32, cc.shape, 0) < HEADS
    p_ref[P_M, :] = jnp.where(fwd_all, _running_max(cc, False), _running_max(cc, True))
    p_ref[P_B, :] = b
    pad = jnp.zeros((L - NGR, L), F32)
    for c in range(n_chunks):
        sl = slice(c * L, (c + 1) * L)
        b_c, cc_c = b_chunks[c], cc[:, sl]
        g = jnp.where(fwd_col, b_c[:, L - 1:L], b_c[:, 0:1])
        m_chunk = g + jnp.max(cc_c, axis=1, keepdims=True)
        p_ref[P_E, sl] = jnp.exp(g + cc_c - m_chunk)
        p_ref[P_G, sl] = jnp.broadcast_to(g, (NGR, L))
        p_ref[P_MC, sl] = jnp.broadcast_to(m_chunk, (NGR, L))
        ccol_ref[sl, :] = jnp.concatenate([cc_c, pad], axis=0).T


def _mix_in_kernel(layer, x_ref, gpre_ref, w_hbm, gbias_ref,
                   bg_ref, u_ref, qt_ref, k_ref, ksw_ref, vt_ref, o_ref, p_ref, ccol_ref,
                   wb_ref, stage_ref, sem):
    W, HK, HV = CONV_WIDTH, HEADS * DK, HEADS * DV
    q0 = 3 * W
    k0, v0 = q0 + HK, q0 + 2 * HK
    o0 = v0 + HV
    g0 = o0 + HV

    @pl.when(pl.program_id(0) == 0)
    def _():
        src = w_hbm.at[layer]
        rows = stage_ref.shape[1]
        n_chunks = g0 // rows

        def copy(c):
            n = rows if c < n_chunks else 2 * NGR
            return pltpu.make_async_copy(src.at[pl.ds(c * rows, n), :], stage_ref.at[c % 2, pl.ds(0, n), :],
                                         sem.at[c % 2])

        copy(0).start()
        for c in range(n_chunks + 1):
            if c < n_chunks:
                copy(c + 1).start()
            copy(c).wait()
            if c < n_chunks:
                wb_ref[:, c * rows:(c + 1) * rows] = stage_ref[c % 2].T.astype(BF16)
            else:
                gt = stage_ref[c % 2, 0:128, :].T
                lane = lax.broadcasted_iota(jnp.int32, gt.shape, 1)
                wb_ref[:, g0:g0 + 128] = jnp.where(lane < 2 * NGR, gt, 0.0).astype(BF16)

    xn = _rms(x_ref[...], gpre_ref[...]).astype(BF16)
    proj = lambda a, b: jnp.dot(xn, wb_ref[:, a:b], preferred_element_type=F32)

    zg = proj(g0, g0 + 128) + gbias_ref[...]
    bg_ref[...] = proj(0, W)
    u_ref[...] = proj(W, 2 * W) * proj(2 * W, 3 * W)
    _gate_rows(zg.T[0:2 * NGR], p_ref, ccol_ref)
    qt_ref[...] = (proj(q0, k0) * (DK ** -0.5)).T.astype(BF16)
    kk = proj(k0, v0)
    k_ref[...] = kk.astype(BF16)
    for pair in range(HEADS // 2):
        ps = slice(pair * 2 * DK, (pair + 1) * 2 * DK)
        ksw_ref[:, ps] = pltpu.roll(kk[:, ps], DK, 1).astype(BF16)
    vt_ref[...] = proj(v0, o0).T.astype(BF16)
    o_ref[...] = proj(o0, g0)


def _mix_in(x, gpre, w, gbias, layer, tm):
    T = x.shape[0]
    row = lambda w: pl.BlockSpec((tm, w), lambda i: (i, 0))
    col = lambda h: pl.BlockSpec((h, tm), lambda i: (0, i))
    HK, HV = HEADS * DK, HEADS * DV
    g0 = 3 * CONV_WIDTH + 2 * HK + 2 * HV
    return pl.pallas_call(
        functools.partial(_mix_in_kernel, layer),
        grid=(T // tm,),
        in_specs=[row(D_MODEL), _resident((1, D_MODEL)), pl.BlockSpec(memory_space=pl.ANY),
                  _resident((1, 128))],
        scratch_shapes=[pltpu.VMEM((D_MODEL, g0 + 128), BF16),
                        pltpu.VMEM((2, g0 // MIX_STAGE_CHUNKS, D_MODEL), F32),
                        pltpu.SemaphoreType.DMA((2,))],
        out_specs=[row(CONV_WIDTH), row(CONV_WIDTH), col(HK), row(HK), row(HK), col(HV), row(HV),
                   col(5 * NGR), row(128)],
        out_shape=[
            jax.ShapeDtypeStruct((T, CONV_WIDTH), F32),
            jax.ShapeDtypeStruct((T, CONV_WIDTH), F32),
            jax.ShapeDtypeStruct((HK, T), BF16),
            jax.ShapeDtypeStruct((T, HK), BF16),
            jax.ShapeDtypeStruct((T, HK), BF16),
            jax.ShapeDtypeStruct((HV, T), BF16),
            jax.ShapeDtypeStruct((T, HV), F32),
            jax.ShapeDtypeStruct((5 * NGR, T), F32),
            jax.ShapeDtypeStruct((T, 128), F32),
        ],
        compiler_params=pltpu.CompilerParams(
            dimension_semantics=("arbitrary",), vmem_limit_bytes=VMEM_LIMIT),
        name="mix_in",
    )(x, gpre, w, gbias)


def _state_half(d, h):
    return (h % 2) ^ d


def _scan_direction(d, vt_ref, k_ref, p_ref, ct_ref, mp_ref, ct_state, m_state):
    L = CHUNK
    n_chunks = k_ref.shape[0] // L
    lane_half = lax.broadcasted_iota(jnp.int32, (L, 128), 1) // DK
    lane_half_s = lax.broadcasted_iota(jnp.int32, (DVA, 128), 1) // DK
    ones_rows = jnp.ones((DVA - DV, L), BF16)

    cts = [ct_state[d, h] for h in range(HEADS)]
    m_prev = m_state[d]
    for chunk in (range(n_chunks - 1, -1, -1) if d == 1 else range(n_chunks)):
        sl = slice(chunk * L, (chunk + 1) * L)
        e, g, m_chunk = p_ref[P_E, sl], p_ref[P_G, sl], p_ref[P_MC, sl]
        mp_ref[:, sl] = m_prev
        m_new = jnp.maximum(g + m_prev, m_chunk)
        a_old = jnp.exp(g + m_prev - m_new)
        a_new = jnp.exp(m_chunk - m_new)
        new_cts = []
        for h in range(HEADS):
            r = HEADS * d + h
            vt_aug = jnp.concatenate([vt_ref[h * DV:(h + 1) * DV, sl], ones_rows], axis=0)
            vte = (vt_aug.astype(F32) * e[r:r + 1, :]).astype(BF16)
            pair = slice((h // 2) * 2 * DK, (h // 2 + 1) * 2 * DK)
            k_half = jnp.where(lane_half == _state_half(d, h), k_ref[sl, pair], jnp.zeros((L, 128), BF16))
            ct_chunk = jnp.dot(vte, k_half, preferred_element_type=F32)
            new_cts.append(a_old[r:r + 1, 0:1] * cts[h] + a_new[r:r + 1, 0:1] * ct_chunk)
        for pr in range(HEADS // 2):
            both = jnp.where(lane_half_s == _state_half(d, 2 * pr), cts[2 * pr], cts[2 * pr + 1])
            ct_ref[chunk, pr * DVA:(pr + 1) * DVA, :] = both.astype(BF16)
        cts = new_cts
        m_prev = m_new
    for h in range(HEADS):
        ct_state[d, h] = cts[h]
    m_state[d] = m_prev


def _mlstm_scan_kernel(vt_f, k_f, p_f, vt_b, ksw_b, p_b, ct_f, mp_f, ct_b, mp_b, ct_state, m_state):
    @pl.when(pl.program_id(1) == 0)
    def _():
        ct_state[...] = jnp.zeros(ct_state.shape, F32)
        m_state[...] = jnp.full(m_state.shape, NEG_INF, F32)

    _scan_direction(0, vt_f, k_f, p_f, ct_f, mp_f, ct_state, m_state)
    _scan_direction(1, vt_b, ksw_b, p_b, ct_b, mp_b, ct_state, m_state)


def _mlstm_out_kernel(qt_ref, k_ref, vt_ref, p_ref, mp_f, mp_b, ccol_ref, ct_f, ct_b,
                      o_ref, gain_ref, y_ref):
    L = CHUNK
    n_chunks = k_ref.shape[0] // L
    visible = (_visible(False), _visible(True))
    fwd_row = lax.broadcasted_iota(jnp.int32, (NGR, L), 0) < HEADS
    lane_half = lax.broadcasted_iota(jnp.int32, (DVA, 128), 1) // DK
    ones_rows = jnp.ones((DVA - DV, L), BF16)
    zq = jnp.zeros((DK, L), BF16)
    for chunk in range(n_chunks):
        sl = slice(chunk * L, (chunk + 1) * L)
        m_prev = jnp.where(fwd_row, mp_f[:, sl], mp_b[:, sl])
        n_t = jnp.maximum(m_prev, p_ref[P_M, sl])
        f_inter = jnp.exp(m_prev - n_t)
        e_min = jnp.exp(-(p_ref[P_B, sl] + n_t))
        ccol = ccol_ref[sl, :]
        for h in range(HEADS):
            hs = slice(h * DV, (h + 1) * DV)
            qt = qt_ref[h * DK:(h + 1) * DK, sl]
            k_pair = k_ref[sl, (h // 2) * 2 * DK:(h // 2 + 1) * 2 * DK]
            qt_pair = jnp.concatenate([qt, zq] if h % 2 == 0 else [zq, qt], axis=0)
            st = jnp.dot(k_pair, qt_pair, preferred_element_type=F32)
            pts, qfs = [], []
            for d in range(2):
                r = HEADS * d + h
                arg = jnp.where(visible[d], ccol[:, r:r + 1] - n_t[r:r + 1, :], NEG_INF)
                pts.append((jnp.exp(arg) * st).astype(BF16))
                qfs.append((qt.astype(F32) * f_inter[r:r + 1, :]).astype(BF16))
            inter = [jnp.concatenate([qfs[0], zq], axis=1), jnp.concatenate([zq, qfs[1]], axis=1)]
            first = 0 if _state_half(0, h) == 0 else 1
            rhs = jnp.concatenate([jnp.concatenate(pts, axis=1), inter[first], inter[1 - first]], axis=0)
            vt_aug = jnp.concatenate([vt_ref[hs, sl], ones_rows], axis=0)
            ps = slice((h // 2) * DVA, (h // 2 + 1) * DVA)
            blocks = (ct_f[chunk, ps, :], ct_b[chunk, ps, :])
            ct = jnp.where(lane_half == 0, blocks[first], blocks[1 - first])
            both = jnp.dot(jnp.concatenate([vt_aug, ct], axis=1), rhs, preferred_element_type=F32)
            ht = None
            for d in range(2):
                r = HEADS * d + h
                numer = both[0:DV, d * L:(d + 1) * L]
                denom = both[DV:DV + 1, d * L:(d + 1) * L]
                part = numer / jnp.maximum(jnp.abs(denom), e_min[r:r + 1, :])
                ht = part if ht is None else ht + part
            ms = jnp.mean(ht * ht, axis=0, keepdims=True)
            hn = ht * lax.rsqrt(ms + EPS) * gain_ref[hs, :]
            y_ref[sl, hs] = (jax.nn.sigmoid(o_ref[sl, hs]) * hn.T).astype(BF16)


def _mlstm(qt, k, ksw, vt, p, ccol, o, gain_b, batch, seq, scan_rows, rows):
    T = batch * seq
    ng = seq // scan_rows
    HK, HV = HEADS * DK, HEADS * DV

    def scan_specs(group_of):
        blk = lambda b, j: b * ng + group_of(j)
        ins = [pl.BlockSpec((HV, scan_rows), lambda b, j: (0, blk(b, j))),
               pl.BlockSpec((scan_rows, HK), lambda b, j: (blk(b, j), 0)),
               pl.BlockSpec((5 * NGR, scan_rows), lambda b, j: (0, blk(b, j)))]
        outs = [pl.BlockSpec((scan_rows // CHUNK, CT_ROWS, 128), lambda b, j: (blk(b, j), 0, 0)),
                pl.BlockSpec((NGR, scan_rows), lambda b, j: (0, blk(b, j)))]
        return ins, outs

    ins_f, outs_f = scan_specs(lambda j: j)
    ins_b, outs_b = scan_specs(lambda j: ng - 1 - j)
    state_shapes = [jax.ShapeDtypeStruct((T // CHUNK, CT_ROWS, 128), BF16),
                    jax.ShapeDtypeStruct((NGR, T), F32)]
    ct_f, mp_f, ct_b, mp_b = pl.pallas_call(
        _mlstm_scan_kernel,
        grid=(batch, ng),
        in_specs=ins_f + ins_b,
        out_specs=outs_f + outs_b,
        out_shape=state_shapes + state_shapes,
        scratch_shapes=[pltpu.VMEM((2, HEADS, DVA, 128), F32), pltpu.VMEM((2, NGR, 128), F32)],
        compiler_params=pltpu.CompilerParams(dimension_semantics=("arbitrary", "arbitrary"),
                                             vmem_limit_bytes=VMEM_LIMIT),
        name="mlstm_scan",
    )(vt, k, p, vt, ksw, p)

    row = lambda w: pl.BlockSpec((rows, w), lambda i: (i, 0))
    col = lambda h: pl.BlockSpec((h, rows), lambda i: (0, i))
    ctb = pl.BlockSpec((rows // CHUNK, CT_ROWS, 128), lambda i: (i, 0, 0))
    return pl.pallas_call(
        _mlstm_out_kernel,
        grid=(T // rows,),
        in_specs=[col(HK), row(HK), col(HV), col(5 * NGR), col(NGR), col(NGR), row(128), ctb, ctb,
                  row(HV), _resident((HV, 128))],
        out_specs=row(HV),
        out_shape=jax.ShapeDtypeStruct((T, HV), BF16),
        compiler_params=pltpu.CompilerParams(dimension_semantics=("arbitrary",),
                                             vmem_limit_bytes=VMEM_LIMIT),
        name="mlstm_out",
    )(qt, k, vt, p, mp_f, mp_b, ccol, ct_f, ct_b, o, gain_b)


def kernel(x, norm_ffn1_pre, norm_ffn1_post, w_ffn1_in, w_ffn1_out, norm_mix_pre, norm_mix_post,
           w_mix_in, conv_w, conv_b, gate_i_bias, gate_f_bias, mlstm_norm, w_mix_out,
           norm_ffn2_pre, norm_ffn2_post, w_ffn2_in, w_ffn2_out):
    batch, seq, _ = x.shape
    T = batch * seq
    depth = norm_ffn1_pre.shape[0]
    tm = 512
    xt = x.reshape(T, D_MODEL)
    HV = HEADS * DV
    for l in range(depth):
        xt = _ffn(xt, norm_ffn1_pre[l][None], norm_ffn1_post[l][None], w_ffn1_in, w_ffn1_out, l, tm)

        gbias = jnp.pad(jnp.concatenate([gate_i_bias[l], gate_f_bias[l]]), (0, 128 - 2 * NGR))[None]
        bg, u, qt, k, ksw, vt, o, p, ccol = _mix_in(
            xt, norm_mix_pre[l][None], jnp.swapaxes(w_mix_in, 1, 2), gbias, l, tm)
        gain_b = jnp.broadcast_to(mlstm_norm[l][:, None], (HV, 128))
        y_mlstm = _mlstm(qt, k, ksw, vt, p, ccol, o, gain_b, batch, seq, SCAN_ROWS, MLSTM_ROWS)
        xt = _mix_ffn(xt, bg, u, conv_w[l], conv_b[l][None], y_mlstm, norm_mix_post[l][None], w_mix_out,
                      norm_ffn2_pre[l][None], norm_ffn2_post[l][None], w_ffn2_in, w_ffn2_out, l, seq, tm)
    return xt.reshape(batch, seq, D_MODEL)
```

```python
import functools

import jax
import jax.numpy as jnp
from jax import lax
from jax.experimental import pallas as pl
from jax.experimental.pallas import tpu as pltpu

D_MODEL = 1024
D_FF = 2816
CONV_WIDTH = 512
HEADS = 4
DK = 64
DV = 128
CHUNK = 128
EPS = 1e-6
NEG_INF = -1e30

FF_TILE = 256
FF_STAGE_CHUNKS = 8
MIX_STAGE_CHUNKS = 8
MLSTM_ROWS = 2048
SCAN_ROWS = 2048
DVA = DV + 16
NGR = 2 * HEADS
CT_ROWS = (HEADS // 2) * DVA
VMEM_LIMIT = 56 * 1024 * 1024

P_E, P_M, P_B, P_G, P_MC = (slice(i * NGR, (i + 1) * NGR) for i in range(5))

F32 = jnp.float32
BF16 = jnp.bfloat16


def _rms(x, g):
    return x * lax.rsqrt(jnp.mean(x * x, axis=-1, keepdims=True) + EPS) * g


def _log_sigmoid(z):
    return jnp.minimum(z, 0.0) - jnp.log1p(jnp.exp(-jnp.abs(z)))


def _resident(shape):
    zeros = (0,) * len(shape)
    return pl.BlockSpec(shape, lambda *_: zeros, pipeline_mode=pl.Buffered(1))


def _split3(x):
    hi = x.astype(BF16)
    r1 = x - hi.astype(F32)
    mid = r1.astype(BF16)
    lo = (r1 - mid.astype(F32)).astype(BF16)
    return hi, mid, lo


def _visible(rev):
    s = lax.broadcasted_iota(jnp.int32, (CHUNK, CHUNK), 0)
    t = lax.broadcasted_iota(jnp.int32, (CHUNK, CHUNK), 1)
    return (s >= t) if rev else (s <= t)


def _running_max(x, rev):
    n = x.shape[1]
    pos = lax.broadcasted_iota(jnp.int32, x.shape, 1) & (CHUNK - 1)
    k = 1
    while k < CHUNK:
        if rev:
            shifted, ok = pltpu.roll(x, n - k, 1), pos < CHUNK - k
        else:
            shifted, ok = pltpu.roll(x, k, 1), pos >= k
        x = jnp.maximum(x, jnp.where(ok, shifted, NEG_INF))
        k *= 2
    return x


def _stage_bf16(src_hbm, dst_ref, stage_ref, sem, rows=None):
    rows = stage_ref.shape[1] if rows is None else rows
    n_chunks = src_hbm.shape[0] // rows

    def copy(c):
        return pltpu.make_async_copy(src_hbm.at[pl.ds(c * rows, rows), :],
                                     stage_ref.at[c % 2, pl.ds(0, rows), :], sem.at[c % 2])

    copy(0).start()
    for c in range(n_chunks):
        if c + 1 < n_chunks:
            copy(c + 1).start()
        copy(c).wait()
        dst_ref[c * rows:(c + 1) * rows, :] = stage_ref[c % 2, 0:rows, :].astype(BF16)


def _mix_out_tile(tiles_per_seq, x_ref, bg_ref, u_ref, uprev_ref, unext_ref, cw_ref, cb_ref,
                  y_ref, wmo_ref, gmix_ref):
    i = pl.program_id(0)
    tm = u_ref.shape[0]
    u = u_ref[...]
    has_prev = (i % tiles_per_seq != 0).astype(F32)
    has_next = (i % tiles_per_seq != tiles_per_seq - 1).astype(F32)
    prev_row = uprev_ref[7:8, :] * has_prev
    next_row = unext_ref[0:1, :] * has_next
    ri = lax.broadcasted_iota(jnp.int32, u.shape, 0)
    u_m1 = jnp.where(ri == 0, prev_row, pltpu.roll(u, 1, 0))
    u_p1 = jnp.where(ri == tm - 1, next_row, pltpu.roll(u, tm - 1, 0))
    conv = cw_ref[0:1, :] * u_m1 + cw_ref[1:2, :] * u + cw_ref[2:3, :] * u_p1
    y_conv = (bg_ref[...] * (conv + cb_ref[...])).astype(BF16)
    h = jnp.dot(y_conv, wmo_ref[0:CONV_WIDTH, :], preferred_element_type=F32) \
        + jnp.dot(y_ref[...], wmo_ref[CONV_WIDTH:, :], preferred_element_type=F32)
    return x_ref[...] + _rms(h, gmix_ref[...])


def _ffn_tile(x, gpre_ref, gpost_ref, win_ref, wout_ref, h_ref):
    xn = _rms(x, gpre_ref[...]).astype(BF16)
    for j in range(D_FF // FF_TILE):
        lo = j * FF_TILE
        gate = jnp.dot(xn, win_ref[:, lo:lo + FF_TILE], preferred_element_type=F32)
        up = jnp.dot(xn, win_ref[:, D_FF + lo:D_FF + lo + FF_TILE], preferred_element_type=F32)
        h_ref[:, lo:lo + FF_TILE] = (gate * jax.nn.sigmoid(gate) * up).astype(BF16)
    y = jnp.dot(h_ref[...], wout_ref[...], preferred_element_type=F32)
    return x + 0.5 * _rms(y, gpost_ref[...])


def _ffn_kernel(layer, x_ref, gpre_ref, gpost_ref, win_hbm, wout_hbm, o_ref,
                h_ref, win_ref, wout_ref, stage_in, stage_out, sem):
    @pl.when(pl.program_id(0) == 0)
    def _():
        _stage_bf16(win_hbm.at[layer], win_ref, stage_in, sem)
        _stage_bf16(wout_hbm.at[layer], wout_ref, stage_out, sem)

    o_ref[...] = _ffn_tile(x_ref[...], gpre_ref, gpost_ref, win_ref, wout_ref, h_ref)


def _mix_ffn_kernel(layer, tiles_per_seq, x_ref, bg_ref, u_ref, uprev_ref, unext_ref, cw_ref, cb_ref,
                    y_ref, gmix_ref, wmo_hbm, gpre_ref, gpost_ref, win_hbm, wout_hbm, o_ref,
                    h_ref, win_ref, wout_ref, wmo_ref, stage_in, stage_out, sem):
    @pl.when(pl.program_id(0) == 0)
    def _():
        _stage_bf16(wmo_hbm.at[layer], wmo_ref, stage_out, sem, rows=D_MODEL // 4)
        _stage_bf16(win_hbm.at[layer], win_ref, stage_in, sem)
        _stage_bf16(wout_hbm.at[layer], wout_ref, stage_out, sem)

    x = _mix_out_tile(tiles_per_seq, x_ref, bg_ref, u_ref, uprev_ref, unext_ref, cw_ref, cb_ref,
                      y_ref, wmo_ref, gmix_ref)
    o_ref[...] = _ffn_tile(x, gpre_ref, gpost_ref, win_ref, wout_ref, h_ref)


def _ffn_scratch(tm):
    return [
        pltpu.VMEM((tm, D_FF), BF16),
        pltpu.VMEM((D_MODEL, 2 * D_FF), BF16),
        pltpu.VMEM((D_FF, D_MODEL), BF16),
    ], [
        pltpu.VMEM((2, D_MODEL // FF_STAGE_CHUNKS, 2 * D_FF), F32),
        pltpu.VMEM((2, D_FF // FF_STAGE_CHUNKS, D_MODEL), F32),
        pltpu.SemaphoreType.DMA((2,)),
    ]


def _ffn(x, gpre, gpost, w_in, w_out, layer, tm):
    T = x.shape[0]
    resident, staging = _ffn_scratch(tm)
    return pl.pallas_call(
        functools.partial(_ffn_kernel, layer),
        grid=(T // tm,),
        in_specs=[
            pl.BlockSpec((tm, D_MODEL), lambda i: (i, 0)),
            _resident((1, D_MODEL)),
            _resident((1, D_MODEL)),
            pl.BlockSpec(memory_space=pl.ANY),
            pl.BlockSpec(memory_space=pl.ANY),
        ],
        out_specs=pl.BlockSpec((tm, D_MODEL), lambda i: (i, 0)),
        out_shape=jax.ShapeDtypeStruct((T, D_MODEL), F32),
        scratch_shapes=resident + staging,
        compiler_params=pltpu.CompilerParams(
            dimension_semantics=("arbitrary",), vmem_limit_bytes=VMEM_LIMIT),
        name="ffn",
    )(x, gpre, gpost, w_in, w_out)


def _mix_ffn(x, bg, u, conv_w, conv_b, y_mlstm, gmix, w_mix_out, gpre, gpost, w_in, w_out, layer, seq, tm):
    T = x.shape[0]
    tiles_per_seq = seq // tm
    sub = tm // 8
    last = T // 8 - 1
    row = lambda w: pl.BlockSpec((tm, w), lambda i: (i, 0))
    hbm = pl.BlockSpec(memory_space=pl.ANY)
    resident, staging = _ffn_scratch(tm)
    return pl.pallas_call(
        functools.partial(_mix_ffn_kernel, layer, tiles_per_seq),
        grid=(T // tm,),
        in_specs=[
            row(D_MODEL), row(CONV_WIDTH), row(CONV_WIDTH),
            pl.BlockSpec((8, CONV_WIDTH), lambda i: (jnp.maximum(i * sub - 1, 0), 0)),
            pl.BlockSpec((8, CONV_WIDTH), lambda i: (jnp.minimum((i + 1) * sub, last), 0)),
            _resident((3, CONV_WIDTH)), _resident((1, CONV_WIDTH)),
            row(HEADS * DV), _resident((1, D_MODEL)), hbm,
            _resident((1, D_MODEL)), _resident((1, D_MODEL)), hbm, hbm,
        ],
        out_specs=row(D_MODEL),
        out_shape=jax.ShapeDtypeStruct((T, D_MODEL), F32),
        scratch_shapes=resident + [pltpu.VMEM((D_MODEL, D_MODEL), BF16)] + staging,
        compiler_params=pltpu.CompilerParams(
            dimension_semantics=("arbitrary",), vmem_limit_bytes=VMEM_LIMIT),
        name="mix_ffn",
    )(x, bg, u, u, u, conv_w, conv_b, y_mlstm, gmix, w_mix_out, gpre, gpost, w_in, w_out)


def _gate_rows(zr, p_ref, ccol_ref):
    L = CHUNK
    n_chunks = zr.shape[1] // L
    fwd_row = lax.broadcasted_iota(jnp.int32, (NGR, L), 0) < HEADS
    fwd_col = lax.broadcasted_iota(jnp.int32, (NGR, 1), 0) < HEADS
    li = zr[0:NGR]
    gates = jnp.concatenate([li, _log_sigmoid(zr[NGR:2 * NGR])], axis=0)
    x3 = jnp.concatenate(_split3(gates), axis=0)
    stacked = jnp.concatenate([x3[:, c * L:(c + 1) * L] for c in range(n_chunks)], axis=0)
    cum_ops = jnp.concatenate([_visible(False), _visible(True)], axis=1).astype(BF16)
    cum = jnp.dot(stacked, cum_ops, preferred_element_type=F32)
    b_chunks = []
    for c in range(n_chunks):
        blk = cum[c * 48:(c + 1) * 48]
        s16 = blk[0:16] + blk[16:32] + blk[32:48]
        b_chunks.append(jnp.where(fwd_row, s16[NGR:2 * NGR, 0:L], s16[NGR:2 * NGR, L:2 * L]))
    b = jnp.concatenate(b_chunks, axis=1)
    cc = li - b
    fwd_all = lax.broadcasted_iota(jnp.int32, cc.shape, 0) < HEADS
    p_ref[P_M, :] = jnp.where(fwd_all, _running_max(cc, False), _running_max(cc, True))
    p_ref[P_B, :] = b
    pad = jnp.zeros((L - NGR, L), F32)
    for c in range(n_chunks):
        sl = slice(c * L, (c + 1) * L)
        b_c, cc_c = b_chunks[c], cc[:, sl]
        g = jnp.where(fwd_col, b_c[:, L - 1:L], b_c[:, 0:1])
        m_chunk = g + jnp.max(cc_c, axis=1, keepdims=True)
        p_ref[P_E, sl] = jnp.exp(g + cc_c - m_chunk)
        p_ref[P_G, sl] = jnp.broadcast_to(g, (NGR, L))
        p_ref[P_MC, sl] = jnp.broadcast_to(m_chunk, (NGR, L))
        ccol_ref[sl, :] = jnp.concatenate([cc_c, pad], axis=0).T


def _mix_in_kernel(layer, x_ref, gpre_ref, w_hbm, gbias_ref,
                   bg_ref, u_ref, qt_ref, k_ref, ksw_ref, vt_ref, o_ref, p_ref, ccol_ref,
                   wb_ref, stage_ref, sem):
    W, HK, HV = CONV_WIDTH, HEADS * DK, HEADS * DV
    q0 = 3 * W
    k0, v0 = q0 + HK, q0 + 2 * HK
    o0 = v0 + HV
    g0 = o0 + HV

    @pl.when(pl.program_id(0) == 0)
    def _():
        src = w_hbm.at[layer]
        rows = stage_ref.shape[1]
        n_chunks = g0 // rows

        def copy(c):
            n = rows if c < n_chunks else 2 * NGR
            return pltpu.make_async_copy(src.at[pl.ds(c * rows, n), :], stage_ref.at[c % 2, pl.ds(0, n), :],
                                         sem.at[c % 2])

        copy(0).start()
        for c in range(n_chunks + 1):
            if c < n_chunks:
                copy(c + 1).start()
            copy(c).wait()
            if c < n_chunks:
                wb_ref[:, c * rows:(c + 1) * rows] = stage_ref[c % 2].T.astype(BF16)
            else:
                gt = stage_ref[c % 2, 0:128, :].T
                lane = lax.broadcasted_iota(jnp.int32, gt.shape, 1)
                wb_ref[:, g0:g0 + 128] = jnp.where(lane < 2 * NGR, gt, 0.0).astype(BF16)

    xn = _rms(x_ref[...], gpre_ref[...]).astype(BF16)
    proj = lambda a, b: jnp.dot(xn, wb_ref[:, a:b], preferred_element_type=F32)

    zg = proj(g0, g0 + 128) + gbias_ref[...]
    bg_ref[...] = proj(0, W)
    u_ref[...] = proj(W, 2 * W) * proj(2 * W, 3 * W)
    _gate_rows(zg.T[0:2 * NGR], p_ref, ccol_ref)
    qt_ref[...] = (proj(q0, k0) * (DK ** -0.5)).T.astype(BF16)
    kk = proj(k0, v0)
    k_ref[...] = kk.astype(BF16)
    for pair in range(HEADS // 2):
        ps = slice(pair * 2 * DK, (pair + 1) * 2 * DK)
        ksw_ref[:, ps] = pltpu.roll(kk[:, ps], DK, 1).astype(BF16)
    vt_ref[...] = proj(v0, o0).T.astype(BF16)
    o_ref[...] = proj(o0, g0)


def _mix_in(x, gpre, w, gbias, layer, tm):
    T = x.shape[0]
    row = lambda w: pl.BlockSpec((tm, w), lambda i: (i, 0))
    col = lambda h: pl.BlockSpec((h, tm), lambda i: (0, i))
    HK, HV = HEADS * DK, HEADS * DV
    g0 = 3 * CONV_WIDTH + 2 * HK + 2 * HV
    return pl.pallas_call(
        functools.partial(_mix_in_kernel, layer),
        grid=(T // tm,),
        in_specs=[row(D_MODEL), _resident((1, D_MODEL)), pl.BlockSpec(memory_space=pl.ANY),
                  _resident((1, 128))],
        scratch_shapes=[pltpu.VMEM((D_MODEL, g0 + 128), BF16),
                        pltpu.VMEM((2, g0 // MIX_STAGE_CHUNKS, D_MODEL), F32),
                        pltpu.SemaphoreType.DMA((2,))],
        out_specs=[row(CONV_WIDTH), row(CONV_WIDTH), col(HK), row(HK), row(HK), col(HV), row(HV),
                   col(5 * NGR), row(128)],
        out_shape=[
            jax.ShapeDtypeStruct((T, CONV_WIDTH), F32),
            jax.ShapeDtypeStruct((T, CONV_WIDTH), F32),
            jax.ShapeDtypeStruct((HK, T), BF16),
            jax.ShapeDtypeStruct((T, HK), BF16),
            jax.ShapeDtypeStruct((T, HK), BF16),
            jax.ShapeDtypeStruct((HV, T), BF16),
            jax.ShapeDtypeStruct((T, HV), F32),
            jax.ShapeDtypeStruct((5 * NGR, T), F32),
            jax.ShapeDtypeStruct((T, 128), F32),
        ],
        compiler_params=pltpu.CompilerParams(
            dimension_semantics=("arbitrary",), vmem_limit_bytes=VMEM_LIMIT),
        name="mix_in",
    )(x, gpre, w, gbias)


def _state_half(d, h):
    return (h % 2) ^ d


def _scan_direction(d, vt_ref, k_ref, p_ref, ct_ref, mp_ref, ct_state, m_state):
    L = CHUNK
    n_chunks = k_ref.shape[0] // L
    lane_half = lax.broadcasted_iota(jnp.int32, (L, 128), 1) // DK
    lane_half_s = lax.broadcasted_iota(jnp.int32, (DVA, 128), 1) // DK
    ones_rows = jnp.ones((DVA - DV, L), BF16)

    cts = [ct_state[d, h] for h in range(HEADS)]
    m_prev = m_state[d]
    for chunk in (range(n_chunks - 1, -1, -1) if d == 1 else range(n_chunks)):
        sl = slice(chunk * L, (chunk + 1) * L)
        e, g, m_chunk = p_ref[P_E, sl], p_ref[P_G, sl], p_ref[P_MC, sl]
        mp_ref[:, sl] = m_prev
        m_new = jnp.maximum(g + m_prev, m_chunk)
        a_old = jnp.exp(g + m_prev - m_new)
        a_new = jnp.exp(m_chunk - m_new)
        new_cts = []
        for h in range(HEADS):
            r = HEADS * d + h
            vt_aug = jnp.concatenate([vt_ref[h * DV:(h + 1) * DV, sl], ones_rows], axis=0)
            vte = (vt_aug.astype(F32) * e[r:r + 1, :]).astype(BF16)
            pair = slice((h // 2) * 2 * DK, (h // 2 + 1) * 2 * DK)
            k_half = jnp.where(lane_half == _state_half(d, h), k_ref[sl, pair], jnp.zeros((L, 128), BF16))
            ct_chunk = jnp.dot(vte, k_half, preferred_element_type=F32)
            new_cts.append(a_old[r:r + 1, 0:1] * cts[h] + a_new[r:r + 1, 0:1] * ct_chunk)
        for pr in range(HEADS // 2):
            both = jnp.where(lane_half_s == _state_half(d, 2 * pr), cts[2 * pr], cts[2 * pr + 1])
            ct_ref[chunk, pr * DVA:(pr + 1) * DVA, :] = both.astype(BF16)
        cts = new_cts
        m_prev = m_new
    for h in range(HEADS):
        ct_state[d, h] = cts[h]
    m_state[d] = m_prev


def _mlstm_scan_kernel(vt_f, k_f, p_f, vt_b, ksw_b, p_b, ct_f, mp_f, ct_b, mp_b, ct_state, m_state):
    @pl.when(pl.program_id(1) == 0)
    def _():
        ct_state[...] = jnp.zeros(ct_state.shape, F32)
        m_state[...] = jnp.full(m_state.shape, NEG_INF, F32)

    _scan_direction(0, vt_f, k_f, p_f, ct_f, mp_f, ct_state, m_state)
    _scan_direction(1, vt_b, ksw_b, p_b, ct_b, mp_b, ct_state, m_state)


def _mlstm_out_kernel(qt_ref, k_ref, vt_ref, p_ref, mp_f, mp_b, ccol_ref, ct_f, ct_b,
                      o_ref, gain_ref, y_ref):
    L = CHUNK
    n_chunks = k_ref.shape[0] // L
    visible = (_visible(False), _visible(True))
    fwd_row = lax.broadcasted_iota(jnp.int32, (NGR, L), 0) < HEADS
    lane_half = lax.broadcasted_iota(jnp.int32, (DVA, 128), 1) // DK
    ones_rows = jnp.ones((DVA - DV, L), BF16)
    zq = jnp.zeros((DK, L), BF16)
    for chunk in range(n_chunks):
        sl = slice(chunk * L, (chunk + 1) * L)
        m_prev = jnp.where(fwd_row, mp_f[:, sl], mp_b[:, sl])
        n_t = jnp.maximum(m_prev, p_ref[P_M, sl])
        f_inter = jnp.exp(m_prev - n_t)
        e_min = jnp.exp(-(p_ref[P_B, sl] + n_t))
        ccol = ccol_ref[sl, :]
        for h in range(HEADS):
            hs = slice(h * DV, (h + 1) * DV)
            qt = qt_ref[h * DK:(h + 1) * DK, sl]
            k_pair = k_ref[sl, (h // 2) * 2 * DK:(h // 2 + 1) * 2 * DK]
            qt_pair = jnp.concatenate([qt, zq] if h % 2 == 0 else [zq, qt], axis=0)
            st = jnp.dot(k_pair, qt_pair, preferred_element_type=F32)
            pts, qfs = [], []
            for d in range(2):
                r = HEADS * d + h
                arg = jnp.where(visible[d], ccol[:, r:r + 1] - n_t[r:r + 1, :], NEG_INF)
                pts.append((jnp.exp(arg) * st).astype(BF16))
                qfs.append((qt.astype(F32) * f_inter[r:r + 1, :]).astype(BF16))
            inter = [jnp.concatenate([qfs[0], zq], axis=1), jnp.concatenate([zq, qfs[1]], axis=1)]
            first = 0 if _state_half(0, h) == 0 else 1
            rhs = jnp.concatenate([jnp.concatenate(pts, axis=1), inter[first], inter[1 - first]], axis=0)
            vt_aug = jnp.concatenate([vt_ref[hs, sl], ones_rows], axis=0)
            ps = slice((h // 2) * DVA, (h // 2 + 1) * DVA)
            blocks = (ct_f[chunk, ps, :], ct_b[chunk, ps, :])
            ct = jnp.where(lane_half == 0, blocks[first], blocks[1 - first])
            both = jnp.dot(jnp.concatenate([vt_aug, ct], axis=1), rhs, preferred_element_type=F32)
            ht = None
            for d in range(2):
                r = HEADS * d + h
                numer = both[0:DV, d * L:(d + 1) * L]
                denom = both[DV:DV + 1, d * L:(d + 1) * L]
                part = numer / jnp.maximum(jnp.abs(denom), e_min[r:r + 1, :])
                ht = part if ht is None else ht + part
            ms = jnp.mean(ht * ht, axis=0, keepdims=True)
            hn = ht * lax.rsqrt(ms + EPS) * gain_ref[hs, :]
            y_ref[sl, hs] = (jax.nn.sigmoid(o_ref[sl, hs]) * hn.T).astype(BF16)


def _mlstm(qt, k, ksw, vt, p, ccol, o, gain_b, batch, seq, scan_rows, rows):
    T = batch * seq
    ng = seq // scan_rows
    HK, HV = HEADS * DK, HEADS * DV

    def scan_specs(group_of):
        blk = lambda b, j: b * ng + group_of(j)
        ins = [pl.BlockSpec((HV, scan_rows), lambda b, j: (0, blk(b, j))),
               pl.BlockSpec((scan_rows, HK), lambda b, j: (blk(b, j), 0)),
               pl.BlockSpec((5 * NGR, scan_rows), lambda b, j: (0, blk(b, j)))]
        outs = [pl.BlockSpec((scan_rows // CHUNK, CT_ROWS, 128), lambda b, j: (blk(b, j), 0, 0)),
                pl.BlockSpec((NGR, scan_rows), lambda b, j: (0, blk(b, j)))]
        return ins, outs

    ins_f, outs_f = scan_specs(lambda j: j)
    ins_b, outs_b = scan_specs(lambda j: ng - 1 - j)
    state_shapes = [jax.ShapeDtypeStruct((T // CHUNK, CT_ROWS, 128), BF16),
                    jax.ShapeDtypeStruct((NGR, T), F32)]
    ct_f, mp_f, ct_b, mp_b = pl.pallas_call(
        _mlstm_scan_kernel,
        grid=(batch, ng),
        in_specs=ins_f + ins_b,
        out_specs=outs_f + outs_b,
        out_shape=state_shapes + state_shapes,
        scratch_shapes=[pltpu.VMEM((2, HEADS, DVA, 128), F32), pltpu.VMEM((2, NGR, 128), F32)],
        compiler_params=pltpu.CompilerParams(dimension_semantics=("arbitrary", "arbitrary"),
                                             vmem_limit_bytes=VMEM_LIMIT),
        name="mlstm_scan",
    )(vt, k, p, vt, ksw, p)

    row = lambda w: pl.BlockSpec((rows, w), lambda i: (i, 0))
    col = lambda h: pl.BlockSpec((h, rows), lambda i: (0, i))
    ctb = pl.BlockSpec((rows // CHUNK, CT_ROWS, 128), lambda i: (i, 0, 0))
    return pl.pallas_call(
        _mlstm_out_kernel,
        grid=(T // rows,),
        in_specs=[col(HK), row(HK), col(HV), col(5 * NGR), col(NGR), col(NGR), row(128), ctb, ctb,
                  row(HV), _resident((HV, 128))],
        out_specs=row(HV),
        out_shape=jax.ShapeDtypeStruct((T, HV), BF16),
        compiler_params=pltpu.CompilerParams(dimension_semantics=("arbitrary",),
                                             vmem_limit_bytes=VMEM_LIMIT),
        name="mlstm_out",
    )(qt, k, vt, p, mp_f, mp_b, ccol, ct_f, ct_b, o, gain_b)


def kernel(x, norm_ffn1_pre, norm_ffn1_post, w_ffn1_in, w_ffn1_out, norm_mix_pre, norm_mix_post,
           w_mix_in, conv_w, conv_b, gate_i_bias, gate_f_bias, mlstm_norm, w_mix_out,
           norm_ffn2_pre, norm_ffn2_post, w_ffn2_in, w_ffn2_out):
    batch, seq, _ = x.shape
    T = batch * seq
    depth = norm_ffn1_pre.shape[0]
    tm = 512
    xt = x.reshape(T, D_MODEL)
    HV = HEADS * DV
    for l in range(depth):
        xt = _ffn(xt, norm_ffn1_pre[l][None], norm_ffn1_post[l][None], w_ffn1_in, w_ffn1_out, l, tm)

        gbias = jnp.pad(jnp.concatenate([gate_i_bias[l], gate_f_bias[l]]), (0, 128 - 2 * NGR))[None]
        bg, u, qt, k, ksw, vt, o, p, ccol = _mix_in(
            xt, norm_mix_pre[l][None], jnp.swapaxes(w_mix_in, 1, 2), gbias, l, tm)
        gain_b = jnp.broadcast_to(mlstm_norm[l][:, None], (HV, 128))
        y_mlstm = _mlstm(qt, k, ksw, vt, p, ccol, o, gain_b, batch, seq, SCAN_ROWS, MLSTM_ROWS)
        xt = _mix_ffn(xt, bg, u, conv_w[l], conv_b[l][None], y_mlstm, norm_mix_post[l][None], w_mix_out,
                      norm_ffn2_pre[l][None], norm_ffn2_post[l][None], w_ffn2_in, w_ffn2_out, l, seq, tm)
    return xt.reshape(batch, seq, D_MODEL)
```

```python
import functools

import jax
import jax.numpy as jnp
from jax import lax
from jax.experimental import pallas as pl
from jax.experimental.pallas import tpu as pltpu

D_MODEL = 1024
D_FF = 2816
CONV_WIDTH = 512
HEADS = 4
DK = 64
DV = 128
CHUNK = 128
EPS = 1e-6
NEG_INF = -1e30

FF_TILE = 256
FF_STAGE_CHUNKS = 8
MIX_STAGE_CHUNKS = 8
MLSTM_ROWS = 2048
SCAN_ROWS = 2048
DVA = DV + 16
NGR = 2 * HEADS
CT_ROWS = (HEADS // 2) * DVA
VMEM_LIMIT = 56 * 1024 * 1024

P_E, P_M, P_B, P_G, P_MC = (slice(i * NGR, (i + 1) * NGR) for i in range(5))

F32 = jnp.float32
BF16 = jnp.bfloat16


def _rms(x, g):
    return x * lax.rsqrt(jnp.mean(x * x, axis=-1, keepdims=True) + EPS) * g


def _log_sigmoid(z):
    return jnp.minimum(z, 0.0) - jnp.log1p(jnp.exp(-jnp.abs(z)))


def _resident(shape):
    zeros = (0,) * len(shape)
    return pl.BlockSpec(shape, lambda *_: zeros, pipeline_mode=pl.Buffered(1))


def _split3(x):
    hi = x.astype(BF16)
    r1 = x - hi.astype(F32)
    mid = r1.astype(BF16)
    lo = (r1 - mid.astype(F32)).astype(BF16)
    return hi, mid, lo


def _visible(rev):
    s = lax.broadcasted_iota(jnp.int32, (CHUNK, CHUNK), 0)
    t = lax.broadcasted_iota(jnp.int32, (CHUNK, CHUNK), 1)
    return (s >= t) if rev else (s <= t)


def _running_max(x, rev):
    n = x.shape[1]
    pos = lax.broadcasted_iota(jnp.int32, x.shape, 1) & (CHUNK - 1)
    k = 1
    while k < CHUNK:
        if rev:
            shifted, ok = pltpu.roll(x, n - k, 1), pos < CHUNK - k
        else:
            shifted, ok = pltpu.roll(x, k, 1), pos >= k
        x = jnp.maximum(x, jnp.where(ok, shifted, NEG_INF))
        k *= 2
    return x


def _stage_bf16(src_hbm, dst_ref, stage_ref, sem, rows=None):
    rows = stage_ref.shape[1] if rows is None else rows
    n_chunks = src_hbm.shape[0] // rows

    def copy(c):
        return pltpu.make_async_copy(src_hbm.at[pl.ds(c * rows, rows), :],
                                     stage_ref.at[c % 2, pl.ds(0, rows), :], sem.at[c % 2])

    copy(0).start()
    for c in range(n_chunks):
        if c + 1 < n_chunks:
            copy(c + 1).start()
        copy(c).wait()
        dst_ref[c * rows:(c + 1) * rows, :] = stage_ref[c % 2, 0:rows, :].astype(BF16)


def _mix_out_tile(tiles_per_seq, x_ref, bg_ref, u_ref, uprev_ref, unext_ref, cw_ref, cb_ref,
                  y_ref, wmo_ref, gmix_ref):
    i = pl.program_id(0)
    tm = u_ref.shape[0]
    u = u_ref[...]
    has_prev = (i % tiles_per_seq != 0).astype(F32)
    has_next = (i % tiles_per_seq != tiles_per_seq - 1).astype(F32)
    prev_row = uprev_ref[7:8, :] * has_prev
    next_row = unext_ref[0:1, :] * has_next
    ri = lax.broadcasted_iota(jnp.int32, u.shape, 0)
    u_m1 = jnp.where(ri == 0, prev_row, pltpu.roll(u, 1, 0))
    u_p1 = jnp.where(ri == tm - 1, next_row, pltpu.roll(u, tm - 1, 0))
    conv = cw_ref[0:1, :] * u_m1 + cw_ref[1:2, :] * u + cw_ref[2:3, :] * u_p1
    y_conv = (bg_ref[...] * (conv + cb_ref[...])).astype(BF16)
    h = jnp.dot(y_conv, wmo_ref[0:CONV_WIDTH, :], preferred_element_type=F32) \
        + jnp.dot(y_ref[...], wmo_ref[CONV_WIDTH:, :], preferred_element_type=F32)
    return x_ref[...] + _rms(h, gmix_ref[...])


def _ffn_tile(x, gpre_ref, gpost_ref, win_ref, wout_ref, h_ref):
    xn = _rms(x, gpre_ref[...]).astype(BF16)
    for j in range(D_FF // FF_TILE):
        lo = j * FF_TILE
        gate = jnp.dot(xn, win_ref[:, lo:lo + FF_TILE], preferred_element_type=F32)
        up = jnp.dot(xn, win_ref[:, D_FF + lo:D_FF + lo + FF_TILE], preferred_element_type=F32)
        h_ref[:, lo:lo + FF_TILE] = (gate * jax.nn.sigmoid(gate) * up).astype(BF16)
    y = jnp.dot(h_ref[...], wout_ref[...], preferred_element_type=F32)
    return x + _rms(y, 0.5 * gpost_ref[...])


def _ffn_kernel(layer, x_ref, gpre_ref, gpost_ref, win_hbm, wout_hbm, o_ref,
                h_ref, win_ref, wout_ref, stage_in, stage_out, sem):
    @pl.when(pl.program_id(0) == 0)
    def _():
        _stage_bf16(win_hbm.at[layer], win_ref, stage_in, sem)
        _stage_bf16(wout_hbm.at[layer], wout_ref, stage_out, sem)

    o_ref[...] = _ffn_tile(x_ref[...], gpre_ref, gpost_ref, win_ref, wout_ref, h_ref)


def _mix_ffn_kernel(layer, tiles_per_seq, x_ref, bg_ref, u_ref, uprev_ref, unext_ref, cw_ref, cb_ref,
                    y_ref, gmix_ref, wmo_hbm, gpre_ref, gpost_ref, win_hbm, wout_hbm, o_ref,
                    h_ref, win_ref, wout_ref, wmo_ref, stage_in, stage_out, sem):
    @pl.when(pl.program_id(0) == 0)
    def _():
        _stage_bf16(wmo_hbm.at[layer], wmo_ref, stage_out, sem, rows=D_MODEL // 4)
        _stage_bf16(win_hbm.at[layer], win_ref, stage_in, sem)
        _stage_bf16(wout_hbm.at[layer], wout_ref, stage_out, sem)

    x = _mix_out_tile(tiles_per_seq, x_ref, bg_ref, u_ref, uprev_ref, unext_ref, cw_ref, cb_ref,
                      y_ref, wmo_ref, gmix_ref)
    o_ref[...] = _ffn_tile(x, gpre_ref, gpost_ref, win_ref, wout_ref, h_ref)


def _ffn_scratch(tm):
    return [
        pltpu.VMEM((tm, D_FF), BF16),
        pltpu.VMEM((D_MODEL, 2 * D_FF), BF16),
        pltpu.VMEM((D_FF, D_MODEL), BF16),
    ], [
        pltpu.VMEM((2, D_MODEL // FF_STAGE_CHUNKS, 2 * D_FF), F32),
        pltpu.VMEM((2, D_FF // FF_STAGE_CHUNKS, D_MODEL), F32),
        pltpu.SemaphoreType.DMA((2,)),
    ]


def _ffn(x, gpre, gpost, w_in, w_out, layer, tm):
    T = x.shape[0]
    resident, staging = _ffn_scratch(tm)
    return pl.pallas_call(
        functools.partial(_ffn_kernel, layer),
        grid=(T // tm,),
        in_specs=[
            pl.BlockSpec((tm, D_MODEL), lambda i: (i, 0)),
            _resident((1, D_MODEL)),
            _resident((1, D_MODEL)),
            pl.BlockSpec(memory_space=pl.ANY),
            pl.BlockSpec(memory_space=pl.ANY),
        ],
        out_specs=pl.BlockSpec((tm, D_MODEL), lambda i: (i, 0)),
        out_shape=jax.ShapeDtypeStruct((T, D_MODEL), F32),
        scratch_shapes=resident + staging,
        compiler_params=pltpu.CompilerParams(
            dimension_semantics=("arbitrary",), vmem_limit_bytes=VMEM_LIMIT),
        name="ffn",
    )(x, gpre, gpost, w_in, w_out)


def _mix_ffn(x, bg, u, conv_w, conv_b, y_mlstm, gmix, w_mix_out, gpre, gpost, w_in, w_out, layer, seq, tm):
    T = x.shape[0]
    tiles_per_seq = seq // tm
    sub = tm // 8
    last = T // 8 - 1
    row = lambda w: pl.BlockSpec((tm, w), lambda i: (i, 0))
    hbm = pl.BlockSpec(memory_space=pl.ANY)
    resident, staging = _ffn_scratch(tm)
    return pl.pallas_call(
        functools.partial(_mix_ffn_kernel, layer, tiles_per_seq),
        grid=(T // tm,),
        in_specs=[
            row(D_MODEL), row(CONV_WIDTH), row(CONV_WIDTH),
            pl.BlockSpec((8, CONV_WIDTH), lambda i: (jnp.maximum(i * sub - 1, 0), 0)),
            pl.BlockSpec((8, CONV_WIDTH), lambda i: (jnp.minimum((i + 1) * sub, last), 0)),
            _resident((3, CONV_WIDTH)), _resident((1, CONV_WIDTH)),
            row(HEADS * DV), _resident((1, D_MODEL)), hbm,
            _resident((1, D_MODEL)), _resident((1, D_MODEL)), hbm, hbm,
        ],
        out_specs=row(D_MODEL),
        out_shape=jax.ShapeDtypeStruct((T, D_MODEL), F32),
        scratch_shapes=resident + [pltpu.VMEM((D_MODEL, D_MODEL), BF16)] + staging,
        compiler_params=pltpu.CompilerParams(
            dimension_semantics=("arbitrary",), vmem_limit_bytes=VMEM_LIMIT),
        name="mix_ffn",
    )(x, bg, u, u, u, conv_w, conv_b, y_mlstm, gmix, w_mix_out, gpre, gpost, w_in, w_out)


def _gate_rows(zr, p_ref, ccol_ref):
    L = CHUNK
    n_chunks = zr.shape[1] // L
    fwd_row = lax.broadcasted_iota(jnp.int32, (NGR, L), 0) < HEADS
    fwd_col = lax.broadcasted_iota(jnp.int32, (NGR, 1), 0) < HEADS
    li = zr[0:NGR]
    gates = jnp.concatenate([li, _log_sigmoid(zr[NGR:2 * NGR])], axis=0)
    x3 = jnp.concatenate(_split3(gates), axis=0)
    stacked = jnp.concatenate([x3[:, c * L:(c + 1) * L] for c in range(n_chunks)], axis=0)
    cum_ops = jnp.concatenate([_visible(False), _visible(True)], axis=1).astype(BF16)
    cum = jnp.dot(stacked, cum_ops, preferred_element_type=F32)
    b_chunks = []
    for c in range(n_chunks):
        blk = cum[c * 48:(c + 1) * 48]
        s16 = blk[0:16] + blk[16:32] + blk[32:48]
        b_chunks.append(jnp.where(fwd_row, s16[NGR:2 * NGR, 0:L], s16[NGR:2 * NGR, L:2 * L]))
    b = jnp.concatenate(b_chunks, axis=1)
    cc = li - b
    fwd_all = lax.broadcasted_iota(jnp.int32, cc.shape, 0) < HEADS
    p_ref[P_M, :] = jnp.where(fwd_all, _running_max(cc, False), _running_max(cc, True))
    p_ref[P_B, :] = b
    pad = jnp.zeros((L - NGR, L), F32)
    for c in range(n_chunks):
        sl = slice(c * L, (c + 1) * L)
        b_c, cc_c = b_chunks[c], cc[:, sl]
        g = jnp.where(fwd_col, b_c[:, L - 1:L], b_c[:, 0:1])
        m_chunk = g + jnp.max(cc_c, axis=1, keepdims=True)
        p_ref[P_E, sl] = jnp.exp(g + cc_c - m_chunk)
        p_ref[P_G, sl] = jnp.broadcast_to(g, (NGR, L))
        p_ref[P_MC, sl] = jnp.broadcast_to(m_chunk, (NGR, L))
        ccol_ref[sl, :] = jnp.concatenate([cc_c, pad], axis=0).T


def _mix_in_kernel(layer, x_ref, gpre_ref, w_hbm, gbias_ref,
                   bg_ref, u_ref, qt_ref, k_ref, ksw_ref, vt_ref, o_ref, p_ref, ccol_ref,
                   wb_ref, stage_ref, sem):
    W, HK, HV = CONV_WIDTH, HEADS * DK, HEADS * DV
    q0 = 3 * W
    k0, v0 = q0 + HK, q0 + 2 * HK
    o0 = v0 + HV
    g0 = o0 + HV

    @pl.when(pl.program_id(0) == 0)
    def _():
        src = w_hbm.at[layer]
        rows = stage_ref.shape[1]
        n_chunks = g0 // rows

        def copy(c):
            n = rows if c < n_chunks else 2 * NGR
            return pltpu.make_async_copy(src.at[pl.ds(c * rows, n), :], stage_ref.at[c % 2, pl.ds(0, n), :],
                                         sem.at[c % 2])

        copy(0).start()
        for c in range(n_chunks + 1):
            if c < n_chunks:
                copy(c + 1).start()
            copy(c).wait()
            if c < n_chunks:
                wb_ref[:, c * rows:(c + 1) * rows] = stage_ref[c % 2].T.astype(BF16)
            else:
                gt = stage_ref[c % 2, 0:128, :].T
                lane = lax.broadcasted_iota(jnp.int32, gt.shape, 1)
                wb_ref[:, g0:g0 + 128] = jnp.where(lane < 2 * NGR, gt, 0.0).astype(BF16)

    xn = _rms(x_ref[...], gpre_ref[...]).astype(BF16)
    proj = lambda a, b: jnp.dot(xn, wb_ref[:, a:b], preferred_element_type=F32)

    zg = proj(g0, g0 + 128) + gbias_ref[...]
    bg_ref[...] = proj(0, W)
    u_ref[...] = proj(W, 2 * W) * proj(2 * W, 3 * W)
    _gate_rows(zg.T[0:2 * NGR], p_ref, ccol_ref)
    qt_ref[...] = (proj(q0, k0) * (DK ** -0.5)).T.astype(BF16)
    kk = proj(k0, v0)
    k_ref[...] = kk.astype(BF16)
    for pair in range(HEADS // 2):
        ps = slice(pair * 2 * DK, (pair + 1) * 2 * DK)
        ksw_ref[:, ps] = pltpu.roll(kk[:, ps], DK, 1).astype(BF16)
    vt_ref[...] = proj(v0, o0).T.astype(BF16)
    o_ref[...] = proj(o0, g0)


def _mix_in(x, gpre, w, gbias, layer, tm):
    T = x.shape[0]
    row = lambda w: pl.BlockSpec((tm, w), lambda i: (i, 0))
    col = lambda h: pl.BlockSpec((h, tm), lambda i: (0, i))
    HK, HV = HEADS * DK, HEADS * DV
    g0 = 3 * CONV_WIDTH + 2 * HK + 2 * HV
    return pl.pallas_call(
        functools.partial(_mix_in_kernel, layer),
        grid=(T // tm,),
        in_specs=[row(D_MODEL), _resident((1, D_MODEL)), pl.BlockSpec(memory_space=pl.ANY),
                  _resident((1, 128))],
        scratch_shapes=[pltpu.VMEM((D_MODEL, g0 + 128), BF16),
                        pltpu.VMEM((2, g0 // MIX_STAGE_CHUNKS, D_MODEL), F32),
                        pltpu.SemaphoreType.DMA((2,))],
        out_specs=[row(CONV_WIDTH), row(CONV_WIDTH), col(HK), row(HK), row(HK), col(HV), row(HV),
                   col(5 * NGR), row(128)],
        out_shape=[
            jax.ShapeDtypeStruct((T, CONV_WIDTH), F32),
            jax.ShapeDtypeStruct((T, CONV_WIDTH), F32),
            jax.ShapeDtypeStruct((HK, T), BF16),
            jax.ShapeDtypeStruct((T, HK), BF16),
            jax.ShapeDtypeStruct((T, HK), BF16),
            jax.ShapeDtypeStruct((HV, T), BF16),
            jax.ShapeDtypeStruct((T, HV), F32),
            jax.ShapeDtypeStruct((5 * NGR, T), F32),
            jax.ShapeDtypeStruct((T, 128), F32),
        ],
        compiler_params=pltpu.CompilerParams(
            dimension_semantics=("arbitrary",), vmem_limit_bytes=VMEM_LIMIT),
        name="mix_in",
    )(x, gpre, w, gbias)


def _state_half(d, h):
    return (h % 2) ^ d


def _scan_direction(d, vt_ref, k_ref, p_ref, ct_ref, mp_ref, ct_state, m_state):
    L = CHUNK
    n_chunks = k_ref.shape[0] // L
    lane_half = lax.broadcasted_iota(jnp.int32, (L, 128), 1) // DK
    lane_half_s = lax.broadcasted_iota(jnp.int32, (DVA, 128), 1) // DK
    ones_rows = jnp.ones((DVA - DV, L), BF16)

    cts = [ct_state[d, h] for h in range(HEADS)]
    m_prev = m_state[d]
    for chunk in (range(n_chunks - 1, -1, -1) if d == 1 else range(n_chunks)):
        sl = slice(chunk * L, (chunk + 1) * L)
        e, g, m_chunk = p_ref[P_E, sl], p_ref[P_G, sl], p_ref[P_MC, sl]
        mp_ref[:, sl] = m_prev
        m_new = jnp.maximum(g + m_prev, m_chunk)
        a_old = jnp.exp(g + m_prev - m_new)
        a_new = jnp.exp(m_chunk - m_new)
        new_cts = []
        for h in range(HEADS):
            r = HEADS * d + h
            vt_aug = jnp.concatenate([vt_ref[h * DV:(h + 1) * DV, sl], ones_rows], axis=0)
            vte = (vt_aug.astype(F32) * e[r:r + 1, :]).astype(BF16)
            pair = slice((h // 2) * 2 * DK, (h // 2 + 1) * 2 * DK)
            k_half = jnp.where(lane_half == _state_half(d, h), k_ref[sl, pair], jnp.zeros((L, 128), BF16))
            ct_chunk = jnp.dot(vte, k_half, preferred_element_type=F32)
            new_cts.append(a_old[r:r + 1, 0:1] * cts[h] + a_new[r:r + 1, 0:1] * ct_chunk)
        for pr in range(HEADS // 2):
            both = jnp.where(lane_half_s == _state_half(d, 2 * pr), cts[2 * pr], cts[2 * pr + 1])
            ct_ref[chunk, pr * DVA:(pr + 1) * DVA, :] = both.astype(BF16)
        cts = new_cts
        m_prev = m_new
    for h in range(HEADS):
        ct_state[d, h] = cts[h]
    m_state[d] = m_prev


def _mlstm_scan_kernel(vt_f, k_f, p_f, vt_b, ksw_b, p_b, ct_f, mp_f, ct_b, mp_b, ct_state, m_state):
    @pl.when(pl.program_id(1) == 0)
    def _():
        ct_state[...] = jnp.zeros(ct_state.shape, F32)
        m_state[...] = jnp.full(m_state.shape, NEG_INF, F32)

    _scan_direction(0, vt_f, k_f, p_f, ct_f, mp_f, ct_state, m_state)
    _scan_direction(1, vt_b, ksw_b, p_b, ct_b, mp_b, ct_state, m_state)


def _mlstm_out_kernel(qt_ref, k_ref, vt_ref, p_ref, mp_f, mp_b, ccol_ref, ct_f, ct_b,
                      o_ref, gain_ref, y_ref):
    L = CHUNK
    n_chunks = k_ref.shape[0] // L
    visible = (_visible(False), _visible(True))
    fwd_row = lax.broadcasted_iota(jnp.int32, (NGR, L), 0) < HEADS
    lane_half = lax.broadcasted_iota(jnp.int32, (DVA, 128), 1) // DK
    ones_rows = jnp.ones((DVA - DV, L), BF16)
    zq = jnp.zeros((DK, L), BF16)
    for chunk in range(n_chunks):
        sl = slice(chunk * L, (chunk + 1) * L)
        m_prev = jnp.where(fwd_row, mp_f[:, sl], mp_b[:, sl])
        n_t = jnp.maximum(m_prev, p_ref[P_M, sl])
        f_inter = jnp.exp(m_prev - n_t)
        e_min = jnp.exp(-(p_ref[P_B, sl] + n_t))
        ccol = ccol_ref[sl, :]
        for h in range(HEADS):
            hs = slice(h * DV, (h + 1) * DV)
            qt = qt_ref[h * DK:(h + 1) * DK, sl]
            k_pair = k_ref[sl, (h // 2) * 2 * DK:(h // 2 + 1) * 2 * DK]
            qt_pair = jnp.concatenate([qt, zq] if h % 2 == 0 else [zq, qt], axis=0)
            st = jnp.dot(k_pair, qt_pair, preferred_element_type=F32)
            pts, qfs = [], []
            for d in range(2):
                r = HEADS * d + h
                arg = jnp.where(visible[d], ccol[:, r:r + 1] - n_t[r:r + 1, :], NEG_INF)
                pts.append((jnp.exp(arg) * st).astype(BF16))
                qfs.append((qt.astype(F32) * f_inter[r:r + 1, :]).astype(BF16))
            inter = [jnp.concatenate([qfs[0], zq], axis=1), jnp.concatenate([zq, qfs[1]], axis=1)]
            first = 0 if _state_half(0, h) == 0 else 1
            rhs = jnp.concatenate([jnp.concatenate(pts, axis=1), inter[first], inter[1 - first]], axis=0)
            vt_aug = jnp.concatenate([vt_ref[hs, sl], ones_rows], axis=0)
            ps = slice((h // 2) * DVA, (h // 2 + 1) * DVA)
            blocks = (ct_f[chunk, ps, :], ct_b[chunk, ps, :])
            ct = jnp.where(lane_half == 0, blocks[first], blocks[1 - first])
            both = jnp.dot(jnp.concatenate([vt_aug, ct], axis=1), rhs, preferred_element_type=F32)
            ht = None
            for d in range(2):
                r = HEADS * d + h
                numer = both[0:DV, d * L:(d + 1) * L]
                denom = both[DV:DV + 1, d * L:(d + 1) * L]
                part = numer / jnp.maximum(jnp.abs(denom), e_min[r:r + 1, :])
                ht = part if ht is None else ht + part
            ms = jnp.mean(ht * ht, axis=0, keepdims=True)
            hn = ht * lax.rsqrt(ms + EPS) * gain_ref[hs, :]
            y_ref[sl, hs] = (jax.nn.sigmoid(o_ref[sl, hs]) * hn.T).astype(BF16)


def _mlstm(qt, k, ksw, vt, p, ccol, o, gain_b, batch, seq, scan_rows, rows):
    T = batch * seq
    ng = seq // scan_rows
    HK, HV = HEADS * DK, HEADS * DV

    def scan_specs(group_of):
        blk = lambda b, j: b * ng + group_of(j)
        ins = [pl.BlockSpec((HV, scan_rows), lambda b, j: (0, blk(b, j))),
               pl.BlockSpec((scan_rows, HK), lambda b, j: (blk(b, j), 0)),
               pl.BlockSpec((5 * NGR, scan_rows), lambda b, j: (0, blk(b, j)))]
        outs = [pl.BlockSpec((scan_rows // CHUNK, CT_ROWS, 128), lambda b, j: (blk(b, j), 0, 0)),
                pl.BlockSpec((NGR, scan_rows), lambda b, j: (0, blk(b, j)))]
        return ins, outs

    ins_f, outs_f = scan_specs(lambda j: j)
    ins_b, outs_b = scan_specs(lambda j: ng - 1 - j)
    state_shapes = [jax.ShapeDtypeStruct((T // CHUNK, CT_ROWS, 128), BF16),
                    jax.ShapeDtypeStruct((NGR, T), F32)]
    ct_f, mp_f, ct_b, mp_b = pl.pallas_call(
        _mlstm_scan_kernel,
        grid=(batch, ng),
        in_specs=ins_f + ins_b,
        out_specs=outs_f + outs_b,
        out_shape=state_shapes + state_shapes,
        scratch_shapes=[pltpu.VMEM((2, HEADS, DVA, 128), F32), pltpu.VMEM((2, NGR, 128), F32)],
        compiler_params=pltpu.CompilerParams(dimension_semantics=("arbitrary", "arbitrary"),
                                             vmem_limit_bytes=VMEM_LIMIT),
        name="mlstm_scan",
    )(vt, k, p, vt, ksw, p)

    row = lambda w: pl.BlockSpec((rows, w), lambda i: (i, 0))
    col = lambda h: pl.BlockSpec((h, rows), lambda i: (0, i))
    ctb = pl.BlockSpec((rows // CHUNK, CT_ROWS, 128), lambda i: (i, 0, 0))
    return pl.pallas_call(
        _mlstm_out_kernel,
        grid=(T // rows,),
        in_specs=[col(HK), row(HK), col(HV), col(5 * NGR), col(NGR), col(NGR), row(128), ctb, ctb,
                  row(HV), _resident((HV, 128))],
        out_specs=row(HV),
        out_shape=jax.ShapeDtypeStruct((T, HV), BF16),
        compiler_params=pltpu.CompilerParams(dimension_semantics=("arbitrary",),
                                             vmem_limit_bytes=VMEM_LIMIT),
        name="mlstm_out",
    )(qt, k, vt, p, mp_f, mp_b, ccol, ct_f, ct_b, o, gain_b)


def kernel(x, norm_ffn1_pre, norm_ffn1_post, w_ffn1_in, w_ffn1_out, norm_mix_pre, norm_mix_post,
           w_mix_in, conv_w, conv_b, gate_i_bias, gate_f_bias, mlstm_norm, w_mix_out,
           norm_ffn2_pre, norm_ffn2_post, w_ffn2_in, w_ffn2_out):
    batch, seq, _ = x.shape
    T = batch * seq
    depth = norm_ffn1_pre.shape[0]
    tm = 512
    xt = x.reshape(T, D_MODEL)
    HV = HEADS * DV
    for l in range(depth):
        xt = _ffn(xt, norm_ffn1_pre[l][None], norm_ffn1_post[l][None], w_ffn1_in, w_ffn1_out, l, 2 * tm)

        gbias = jnp.pad(jnp.concatenate([gate_i_bias[l], gate_f_bias[l]]), (0, 128 - 2 * NGR))[None]
        bg, u, qt, k, ksw, vt, o, p, ccol = _mix_in(
            xt, norm_mix_pre[l][None], jnp.swapaxes(w_mix_in, 1, 2), gbias, l, tm)
        gain_b = jnp.broadcast_to(mlstm_norm[l][:, None], (HV, 128))
        y_mlstm = _mlstm(qt, k, ksw, vt, p, ccol, o, gain_b, batch, seq, SCAN_ROWS, MLSTM_ROWS)
        xt = _mix_ffn(xt, bg, u, conv_w[l], conv_b[l][None], y_mlstm, norm_mix_post[l][None], w_mix_out,
                      norm_ffn2_pre[l][None], norm_ffn2_post[l][None], w_ffn2_in, w_ffn2_out, l, seq, tm)
    return xt.reshape(batch, seq, D_MODEL)
```

```python
import functools

import jax
import jax.numpy as jnp
from jax import lax
from jax.experimental import pallas as pl
from jax.experimental.pallas import tpu as pltpu

D_MODEL = 1024
D_FF = 2816
CONV_WIDTH = 512
HEADS = 4
DK = 64
DV = 128
CHUNK = 128
EPS = 1e-6
NEG_INF = -1e30

FF_TILE = 256
FF_STAGE_CHUNKS = 8
MIX_STAGE_CHUNKS = 8
MLSTM_ROWS = 2048
SCAN_ROWS = 2048
DVA = DV + 16
NGR = 2 * HEADS
CT_ROWS = (HEADS // 2) * DVA
VMEM_LIMIT = 56 * 1024 * 1024

P_E, P_M, P_B, P_G, P_MC = (slice(i * NGR, (i + 1) * NGR) for i in range(5))

F32 = jnp.float32
BF16 = jnp.bfloat16


def _rms(x, g):
    return x * lax.rsqrt(jnp.mean(x * x, axis=-1, keepdims=True) + EPS) * g


def _log_sigmoid(z):
    return jnp.minimum(z, 0.0) - jnp.log1p(jnp.exp(-jnp.abs(z)))


def _resident(shape):
    zeros = (0,) * len(shape)
    return pl.BlockSpec(shape, lambda *_: zeros, pipeline_mode=pl.Buffered(1))


def _split3(x):
    hi = x.astype(BF16)
    r1 = x - hi.astype(F32)
    mid = r1.astype(BF16)
    lo = (r1 - mid.astype(F32)).astype(BF16)
    return hi, mid, lo


def _visible(rev):
    s = lax.broadcasted_iota(jnp.int32, (CHUNK, CHUNK), 0)
    t = lax.broadcasted_iota(jnp.int32, (CHUNK, CHUNK), 1)
    return (s >= t) if rev else (s <= t)


def _running_max(x, rev):
    n = x.shape[1]
    pos = lax.broadcasted_iota(jnp.int32, x.shape, 1) & (CHUNK - 1)
    k = 1
    while k < CHUNK:
        if rev:
            shifted, ok = pltpu.roll(x, n - k, 1), pos < CHUNK - k
        else:
            shifted, ok = pltpu.roll(x, k, 1), pos >= k
        x = jnp.maximum(x, jnp.where(ok, shifted, NEG_INF))
        k *= 2
    return x


def _stage_bf16(src_hbm, dst_ref, stage_ref, sem, rows=None):
    rows = stage_ref.shape[1] if rows is None else rows
    n_chunks = src_hbm.shape[0] // rows

    def copy(c):
        return pltpu.make_async_copy(src_hbm.at[pl.ds(c * rows, rows), :],
                                     stage_ref.at[c % 2, pl.ds(0, rows), :], sem.at[c % 2])

    copy(0).start()
    for c in range(n_chunks):
        if c + 1 < n_chunks:
            copy(c + 1).start()
        copy(c).wait()
        dst_ref[c * rows:(c + 1) * rows, :] = stage_ref[c % 2, 0:rows, :].astype(BF16)


def _mix_out_tile(tiles_per_seq, x_ref, bg_ref, u_ref, uprev_ref, unext_ref, cw_ref, cb_ref,
                  y_ref, wmo_ref, gmix_ref):
    i = pl.program_id(0)
    tm = u_ref.shape[0]
    u = u_ref[...]
    has_prev = (i % tiles_per_seq != 0).astype(F32)
    has_next = (i % tiles_per_seq != tiles_per_seq - 1).astype(F32)
    prev_row = uprev_ref[7:8, :] * has_prev
    next_row = unext_ref[0:1, :] * has_next
    ri = lax.broadcasted_iota(jnp.int32, u.shape, 0)
    u_m1 = jnp.where(ri == 0, prev_row, pltpu.roll(u, 1, 0))
    u_p1 = jnp.where(ri == tm - 1, next_row, pltpu.roll(u, tm - 1, 0))
    conv = cw_ref[0:1, :] * u_m1 + cw_ref[1:2, :] * u + cw_ref[2:3, :] * u_p1
    y_conv = (bg_ref[...] * (conv + cb_ref[...])).astype(BF16)
    h = jnp.dot(y_conv, wmo_ref[0:CONV_WIDTH, :], preferred_element_type=F32) \
        + jnp.dot(y_ref[...], wmo_ref[CONV_WIDTH:, :], preferred_element_type=F32)
    return x_ref[...] + _rms(h, gmix_ref[...])


def _ffn_tile(x, gpre_ref, gpost_ref, win_ref, wout_ref, h_ref):
    xn = _rms(x, gpre_ref[...]).astype(BF16)
    for j in range(D_FF // FF_TILE):
        lo = j * FF_TILE
        gate = jnp.dot(xn, win_ref[:, lo:lo + FF_TILE], preferred_element_type=F32)
        up = jnp.dot(xn, win_ref[:, D_FF + lo:D_FF + lo + FF_TILE], preferred_element_type=F32)
        h_ref[:, lo:lo + FF_TILE] = (gate * jax.nn.sigmoid(gate) * up).astype(BF16)
    y = jnp.dot(h_ref[...], wout_ref[...], preferred_element_type=F32)
    return x + _rms(y, 0.5 * gpost_ref[...])


def _ffn_kernel(layer, x_ref, gpre_ref, gpost_ref, win_hbm, wout_hbm, o_ref,
                h_ref, win_ref, wout_ref, stage_in, stage_out, sem):
    @pl.when(pl.program_id(0) == 0)
    def _():
        _stage_bf16(win_hbm.at[layer], win_ref, stage_in, sem)
        _stage_bf16(wout_hbm.at[layer], wout_ref, stage_out, sem)

    o_ref[...] = _ffn_tile(x_ref[...], gpre_ref, gpost_ref, win_ref, wout_ref, h_ref)


def _mix_ffn_kernel(layer, tiles_per_seq, x_ref, bg_ref, u_ref, uprev_ref, unext_ref, cw_ref, cb_ref,
                    y_ref, gmix_ref, wmo_hbm, gpre_ref, gpost_ref, win_hbm, wout_hbm, o_ref,
                    h_ref, win_ref, wout_ref, wmo_ref, stage_in, stage_out, sem):
    @pl.when(pl.program_id(0) == 0)
    def _():
        _stage_bf16(wmo_hbm.at[layer], wmo_ref, stage_out, sem, rows=D_MODEL // 4)
        _stage_bf16(win_hbm.at[layer], win_ref, stage_in, sem)
        _stage_bf16(wout_hbm.at[layer], wout_ref, stage_out, sem)

    x = _mix_out_tile(tiles_per_seq, x_ref, bg_ref, u_ref, uprev_ref, unext_ref, cw_ref, cb_ref,
                      y_ref, wmo_ref, gmix_ref)
    o_ref[...] = _ffn_tile(x, gpre_ref, gpost_ref, win_ref, wout_ref, h_ref)


def _ffn_scratch(tm):
    return [
        pltpu.VMEM((tm, D_FF), BF16),
        pltpu.VMEM((D_MODEL, 2 * D_FF), BF16),
        pltpu.VMEM((D_FF, D_MODEL), BF16),
    ], [
        pltpu.VMEM((2, D_MODEL // FF_STAGE_CHUNKS, 2 * D_FF), F32),
        pltpu.VMEM((2, D_FF // FF_STAGE_CHUNKS, D_MODEL), F32),
        pltpu.SemaphoreType.DMA((2,)),
    ]


def _ffn(x, gpre, gpost, w_in, w_out, layer, tm):
    T = x.shape[0]
    resident, staging = _ffn_scratch(tm)
    return pl.pallas_call(
        functools.partial(_ffn_kernel, layer),
        grid=(T // tm,),
        in_specs=[
            pl.BlockSpec((tm, D_MODEL), lambda i: (i, 0)),
            _resident((1, D_MODEL)),
            _resident((1, D_MODEL)),
            pl.BlockSpec(memory_space=pl.ANY),
            pl.BlockSpec(memory_space=pl.ANY),
        ],
        out_specs=pl.BlockSpec((tm, D_MODEL), lambda i: (i, 0)),
        out_shape=jax.ShapeDtypeStruct((T, D_MODEL), F32),
        scratch_shapes=resident + staging,
        compiler_params=pltpu.CompilerParams(
            dimension_semantics=("arbitrary",), vmem_limit_bytes=VMEM_LIMIT),
        name="ffn",
    )(x, gpre, gpost, w_in, w_out)


def _mix_ffn(x, bg, u, conv_w, conv_b, y_mlstm, gmix, w_mix_out, gpre, gpost, w_in, w_out, layer, seq, tm):
    T = x.shape[0]
    tiles_per_seq = seq // tm
    sub = tm // 8
    last = T // 8 - 1
    row = lambda w: pl.BlockSpec((tm, w), lambda i: (i, 0))
    hbm = pl.BlockSpec(memory_space=pl.ANY)
    resident, staging = _ffn_scratch(tm)
    return pl.pallas_call(
        functools.partial(_mix_ffn_kernel, layer, tiles_per_seq),
        grid=(T // tm,),
        in_specs=[
            row(D_MODEL), row(CONV_WIDTH), row(CONV_WIDTH),
            pl.BlockSpec((8, CONV_WIDTH), lambda i: (jnp.maximum(i * sub - 1, 0), 0)),
            pl.BlockSpec((8, CONV_WIDTH), lambda i: (jnp.minimum((i + 1) * sub, last), 0)),
            _resident((3, CONV_WIDTH)), _resident((1, CONV_WIDTH)),
            row(HEADS * DV), _resident((1, D_MODEL)), hbm,
            _resident((1, D_MODEL)), _resident((1, D_MODEL)), hbm, hbm,
        ],
        out_specs=row(D_MODEL),
        out_shape=jax.ShapeDtypeStruct((T, D_MODEL), F32),
        scratch_shapes=resident + [pltpu.VMEM((D_MODEL, D_MODEL), BF16)] + staging,
        compiler_params=pltpu.CompilerParams(
            dimension_semantics=("arbitrary",), vmem_limit_bytes=VMEM_LIMIT),
        name="mix_ffn",
    )(x, bg, u, u, u, conv_w, conv_b, y_mlstm, gmix, w_mix_out, gpre, gpost, w_in, w_out)


def _gate_rows(zr, p_ref, ccol_ref):
    L = CHUNK
    n_chunks = zr.shape[1] // L
    fwd_row = lax.broadcasted_iota(jnp.int32, (NGR, L), 0) < HEADS
    fwd_col = lax.broadcasted_iota(jnp.int32, (NGR, 1), 0) < HEADS
    li = zr[0:NGR]
    gates = jnp.concatenate([li, _log_sigmoid(zr[NGR:2 * NGR])], axis=0)
    x3 = jnp.concatenate(_split3(gates), axis=0)
    stacked = jnp.concatenate([x3[:, c * L:(c + 1) * L] for c in range(n_chunks)], axis=0)
    cum_ops = jnp.concatenate([_visible(False), _visible(True)], axis=1).astype(BF16)
    cum = jnp.dot(stacked, cum_ops, preferred_element_type=F32)
    b_chunks = []
    for c in range(n_chunks):
        blk = cum[c * 48:(c + 1) * 48]
        s16 = blk[0:16] + blk[16:32] + blk[32:48]
        b_chunks.append(jnp.where(fwd_row, s16[NGR:2 * NGR, 0:L], s16[NGR:2 * NGR, L:2 * L]))
    b = jnp.concatenate(b_chunks, axis=1)
    cc = li - b
    fwd_all = lax.broadcasted_iota(jnp.int32, cc.shape, 0) < HEADS
    p_ref[P_M, :] = jnp.where(fwd_all, _running_max(cc, False), _running_max(cc, True))
    p_ref[P_B, :] = b
    pad = jnp.zeros((L - NGR, L), F32)
    for c in range(n_chunks):
        sl = slice(c * L, (c + 1) * L)
        b_c, cc_c = b_chunks[c], cc[:, sl]
        g = jnp.where(fwd_col, b_c[:, L - 1:L], b_c[:, 0:1])
        m_chunk = g + jnp.max(cc_c, axis=1, keepdims=True)
        p_ref[P_E, sl] = jnp.exp(g + cc_c - m_chunk)
        p_ref[P_G, sl] = jnp.broadcast_to(g, (NGR, L))
        p_ref[P_MC, sl] = jnp.broadcast_to(m_chunk, (NGR, L))
        ccol_ref[sl, :] = jnp.concatenate([cc_c, pad], axis=0).T


def _mix_in_kernel(layer, x_ref, gpre_ref, w_hbm, gbias_ref,
                   bg_ref, u_ref, qt_ref, k_ref, ksw_ref, vt_ref, o_ref, p_ref, ccol_ref,
                   wb_ref, stage_ref, sem):
    W, HK, HV = CONV_WIDTH, HEADS * DK, HEADS * DV
    q0 = 3 * W
    k0, v0 = q0 + HK, q0 + 2 * HK
    o0 = v0 + HV
    g0 = o0 + HV

    @pl.when(pl.program_id(0) == 0)
    def _():
        src = w_hbm.at[layer]
        rows = stage_ref.shape[1]
        n_chunks = g0 // rows

        def copy(c):
            n = rows if c < n_chunks else 2 * NGR
            return pltpu.make_async_copy(src.at[pl.ds(c * rows, n), :], stage_ref.at[c % 2, pl.ds(0, n), :],
                                         sem.at[c % 2])

        copy(0).start()
        for c in range(n_chunks + 1):
            if c < n_chunks:
                copy(c + 1).start()
            copy(c).wait()
            if c < n_chunks:
                wb_ref[:, c * rows:(c + 1) * rows] = stage_ref[c % 2].T.astype(BF16)
            else:
                gt = stage_ref[c % 2, 0:128, :].T
                lane = lax.broadcasted_iota(jnp.int32, gt.shape, 1)
                wb_ref[:, g0:g0 + 128] = jnp.where(lane < 2 * NGR, gt, 0.0).astype(BF16)

    xn = _rms(x_ref[...], gpre_ref[...]).astype(BF16)
    proj = lambda a, b: jnp.dot(xn, wb_ref[:, a:b], preferred_element_type=F32)

    zg = proj(g0, g0 + 128) + gbias_ref[...]
    bg_ref[...] = proj(0, W)
    u_ref[...] = proj(W, 2 * W) * proj(2 * W, 3 * W)
    _gate_rows(zg.T[0:2 * NGR], p_ref, ccol_ref)
    qt_ref[...] = (proj(q0, k0) * (DK ** -0.5)).T.astype(BF16)
    kk = proj(k0, v0)
    k_ref[...] = kk.astype(BF16)
    for pair in range(HEADS // 2):
        ps = slice(pair * 2 * DK, (pair + 1) * 2 * DK)
        ksw_ref[:, ps] = pltpu.roll(kk[:, ps], DK, 1).astype(BF16)
    vt_ref[...] = proj(v0, o0).T.astype(BF16)
    o_ref[...] = proj(o0, g0)


def _mix_in(x, gpre, w, gbias, layer, tm):
    T = x.shape[0]
    row = lambda w: pl.BlockSpec((tm, w), lambda i: (i, 0))
    col = lambda h: pl.BlockSpec((h, tm), lambda i: (0, i))
    HK, HV = HEADS * DK, HEADS * DV
    g0 = 3 * CONV_WIDTH + 2 * HK + 2 * HV
    return pl.pallas_call(
        functools.partial(_mix_in_kernel, layer),
        grid=(T // tm,),
        in_specs=[row(D_MODEL), _resident((1, D_MODEL)), pl.BlockSpec(memory_space=pl.ANY),
                  _resident((1, 128))],
        scratch_shapes=[pltpu.VMEM((D_MODEL, g0 + 128), BF16),
                        pltpu.VMEM((2, g0 // MIX_STAGE_CHUNKS, D_MODEL), F32),
                        pltpu.SemaphoreType.DMA((2,))],
        out_specs=[row(CONV_WIDTH), row(CONV_WIDTH), col(HK), row(HK), row(HK), col(HV), row(HV),
                   col(5 * NGR), row(128)],
        out_shape=[
            jax.ShapeDtypeStruct((T, CONV_WIDTH), F32),
            jax.ShapeDtypeStruct((T, CONV_WIDTH), F32),
            jax.ShapeDtypeStruct((HK, T), BF16),
            jax.ShapeDtypeStruct((T, HK), BF16),
            jax.ShapeDtypeStruct((T, HK), BF16),
            jax.ShapeDtypeStruct((HV, T), BF16),
            jax.ShapeDtypeStruct((T, HV), F32),
            jax.ShapeDtypeStruct((5 * NGR, T), F32),
            jax.ShapeDtypeStruct((T, 128), F32),
        ],
        compiler_params=pltpu.CompilerParams(
            dimension_semantics=("arbitrary",), vmem_limit_bytes=VMEM_LIMIT),
        name="mix_in",
    )(x, gpre, w, gbias)


def _state_half(d, h):
    return (h % 2) ^ d


def _scan_direction(d, vt_ref, k_ref, p_ref, ct_ref, mp_ref, ct_state, m_state):
    L = CHUNK
    n_chunks = k_ref.shape[0] // L
    lane_half = lax.broadcasted_iota(jnp.int32, (L, 128), 1) // DK
    lane_half_s = lax.broadcasted_iota(jnp.int32, (DVA, 128), 1) // DK
    ones_rows = jnp.ones((DVA - DV, L), BF16)

    cts = [ct_state[d, h] for h in range(HEADS)]
    m_prev = m_state[d]
    for chunk in (range(n_chunks - 1, -1, -1) if d == 1 else range(n_chunks)):
        sl = slice(chunk * L, (chunk + 1) * L)
        e, g, m_chunk = p_ref[P_E, sl], p_ref[P_G, sl], p_ref[P_MC, sl]
        mp_ref[:, sl] = m_prev
        m_new = jnp.maximum(g + m_prev, m_chunk)
        a_old = jnp.exp(g + m_prev - m_new)
        a_new = jnp.exp(m_chunk - m_new)
        new_cts = []
        for h in range(HEADS):
            r = HEADS * d + h
            vt_aug = jnp.concatenate([vt_ref[h * DV:(h + 1) * DV, sl], ones_rows], axis=0)
            vte = (vt_aug.astype(F32) * e[r:r + 1, :]).astype(BF16)
            pair = slice((h // 2) * 2 * DK, (h // 2 + 1) * 2 * DK)
            k_half = jnp.where(lane_half == _state_half(d, h), k_ref[sl, pair], jnp.zeros((L, 128), BF16))
            ct_chunk = jnp.dot(vte, k_half, preferred_element_type=F32)
            new_cts.append(a_old[r:r + 1, 0:1] * cts[h] + a_new[r:r + 1, 0:1] * ct_chunk)
        for pr in range(HEADS // 2):
            both = jnp.where(lane_half_s == _state_half(d, 2 * pr), cts[2 * pr], cts[2 * pr + 1])
            ct_ref[chunk, pr * DVA:(pr + 1) * DVA, :] = both.astype(BF16)
        cts = new_cts
        m_prev = m_new
    for h in range(HEADS):
        ct_state[d, h] = cts[h]
    m_state[d] = m_prev


def _mlstm_scan_kernel(vt_f, k_f, p_f, vt_b, ksw_b, p_b, ct_f, mp_f, ct_b, mp_b, ct_state, m_state):
    @pl.when(pl.program_id(1) == 0)
    def _():
        ct_state[...] = jnp.zeros(ct_state.shape, F32)
        m_state[...] = jnp.full(m_state.shape, NEG_INF, F32)

    _scan_direction(0, vt_f, k_f, p_f, ct_f, mp_f, ct_state, m_state)
    _scan_direction(1, vt_b, ksw_b, p_b, ct_b, mp_b, ct_state, m_state)


def _mlstm_out_kernel(qt_ref, k_ref, vt_ref, p_ref, mp_f, mp_b, ccol_ref, ct_f, ct_b,
                      o_ref, gain_ref, y_ref):
    L = CHUNK
    n_chunks = k_ref.shape[0] // L
    visible = (_visible(False), _visible(True))
    fwd_row = lax.broadcasted_iota(jnp.int32, (NGR, L), 0) < HEADS
    lane_half = lax.broadcasted_iota(jnp.int32, (DVA, 128), 1) // DK
    ones_rows = jnp.ones((DVA - DV, L), BF16)
    zq = jnp.zeros((DK, L), BF16)
    for chunk in range(n_chunks):
        sl = slice(chunk * L, (chunk + 1) * L)
        m_prev = jnp.where(fwd_row, mp_f[:, sl], mp_b[:, sl])
        n_t = jnp.maximum(m_prev, p_ref[P_M, sl])
        f_inter = jnp.exp(m_prev - n_t)
        e_min = jnp.exp(-(p_ref[P_B, sl] + n_t))
        ccol = ccol_ref[sl, :]
        for h in range(HEADS):
            hs = slice(h * DV, (h + 1) * DV)
            qt = qt_ref[h * DK:(h + 1) * DK, sl]
            k_pair = k_ref[sl, (h // 2) * 2 * DK:(h // 2 + 1) * 2 * DK]
            qt_pair = jnp.concatenate([qt, zq] if h % 2 == 0 else [zq, qt], axis=0)
            st = jnp.dot(k_pair, qt_pair, preferred_element_type=F32)
            pts, qfs = [], []
            for d in range(2):
                r = HEADS * d + h
                arg = jnp.where(visible[d], ccol[:, r:r + 1] - n_t[r:r + 1, :], NEG_INF)
                pts.append((jnp.exp(arg) * st).astype(BF16))
                qfs.append((qt.astype(F32) * f_inter[r:r + 1, :]).astype(BF16))
            inter = [jnp.concatenate([qfs[0], zq], axis=1), jnp.concatenate([zq, qfs[1]], axis=1)]
            first = 0 if _state_half(0, h) == 0 else 1
            rhs = jnp.concatenate([jnp.concatenate(pts, axis=1), inter[first], inter[1 - first]], axis=0)
            vt_aug = jnp.concatenate([vt_ref[hs, sl], ones_rows], axis=0)
            ps = slice((h // 2) * DVA, (h // 2 + 1) * DVA)
            blocks = (ct_f[chunk, ps, :], ct_b[chunk, ps, :])
            ct = jnp.where(lane_half == 0, blocks[first], blocks[1 - first])
            both = jnp.dot(jnp.concatenate([vt_aug, ct], axis=1), rhs, preferred_element_type=F32)
            ht = None
            for d in range(2):
                r = HEADS * d + h
                numer = both[0:DV, d * L:(d + 1) * L]
                denom = both[DV:DV + 1, d * L:(d + 1) * L]
                part = numer / jnp.maximum(jnp.abs(denom), e_min[r:r + 1, :])
                ht = part if ht is None else ht + part
            ms = jnp.mean(ht * ht, axis=0, keepdims=True)
            hn = ht * lax.rsqrt(ms + EPS) * gain_ref[hs, :]
            y_ref[sl, hs] = (jax.nn.sigmoid(o_ref[sl, hs]) * hn.T).astype(BF16)


def _mlstm(qt, k, ksw, vt, p, ccol, o, gain_b, batch, seq, scan_rows, rows):
    T = batch * seq
    ng = seq // scan_rows
    HK, HV = HEADS * DK, HEADS * DV

    def scan_specs(group_of):
        blk = lambda b, j: b * ng + group_of(j)
        ins = [pl.BlockSpec((HV, scan_rows), lambda b, j: (0, blk(b, j))),
               pl.BlockSpec((scan_rows, HK), lambda b, j: (blk(b, j), 0)),
               pl.BlockSpec((5 * NGR, scan_rows), lambda b, j: (0, blk(b, j)))]
        outs = [pl.BlockSpec((scan_rows // CHUNK, CT_ROWS, 128), lambda b, j: (blk(b, j), 0, 0)),
                pl.BlockSpec((NGR, scan_rows), lambda b, j: (0, blk(b, j)))]
        return ins, outs

    ins_f, outs_f = scan_specs(lambda j: j)
    ins_b, outs_b = scan_specs(lambda j: ng - 1 - j)
    state_shapes = [jax.ShapeDtypeStruct((T // CHUNK, CT_ROWS, 128), BF16),
                    jax.ShapeDtypeStruct((NGR, T), F32)]
    ct_f, mp_f, ct_b, mp_b = pl.pallas_call(
        _mlstm_scan_kernel,
        grid=(batch, ng),
        in_specs=ins_f + ins_b,
        out_specs=outs_f + outs_b,
        out_shape=state_shapes + state_shapes,
        scratch_shapes=[pltpu.VMEM((2, HEADS, DVA, 128), F32), pltpu.VMEM((2, NGR, 128), F32)],
        compiler_params=pltpu.CompilerParams(dimension_semantics=("arbitrary", "arbitrary"),
                                             vmem_limit_bytes=VMEM_LIMIT),
        name="mlstm_scan",
    )(vt, k, p, vt, ksw, p)

    row = lambda w: pl.BlockSpec((rows, w), lambda i: (i, 0))
    col = lambda h: pl.BlockSpec((h, rows), lambda i: (0, i))
    ctb = pl.BlockSpec((rows // CHUNK, CT_ROWS, 128), lambda i: (i, 0, 0))
    return pl.pallas_call(
        _mlstm_out_kernel,
        grid=(T // rows,),
        in_specs=[col(HK), row(HK), col(HV), col(5 * NGR), col(NGR), col(NGR), row(128), ctb, ctb,
                  row(HV), _resident((HV, 128))],
        out_specs=row(HV),
        out_shape=jax.ShapeDtypeStruct((T, HV), BF16),
        compiler_params=pltpu.CompilerParams(dimension_semantics=("arbitrary",),
                                             vmem_limit_bytes=VMEM_LIMIT),
        name="mlstm_out",
    )(qt, k, vt, p, mp_f, mp_b, ccol, ct_f, ct_b, o, gain_b)


def kernel(x, norm_ffn1_pre, norm_ffn1_post, w_ffn1_in, w_ffn1_out, norm_mix_pre, norm_mix_post,
           w_mix_in, conv_w, conv_b, gate_i_bias, gate_f_bias, mlstm_norm, w_mix_out,
           norm_ffn2_pre, norm_ffn2_post, w_ffn2_in, w_ffn2_out):
    batch, seq, _ = x.shape
    T = batch * seq
    depth = norm_ffn1_pre.shape[0]
    tm = 512
    xt = x.reshape(T, D_MODEL)
    HV = HEADS * DV
    for l in range(depth):
        xt = _ffn(xt, norm_ffn1_pre[l][None], norm_ffn1_post[l][None], w_ffn1_in, w_ffn1_out, l, 2 * tm)

        gbias = jnp.pad(jnp.concatenate([gate_i_bias[l], gate_f_bias[l]]), (0, 128 - 2 * NGR))[None]
        bg, u, qt, k, ksw, vt, o, p, ccol = _mix_in(
            xt, norm_mix_pre[l][None], jnp.swapaxes(w_mix_in, 1, 2), gbias, l, 2 * tm)
        gain_b = jnp.broadcast_to(mlstm_norm[l][:, None], (HV, 128))
        y_mlstm = _mlstm(qt, k, ksw, vt, p, ccol, o, gain_b, batch, seq, SCAN_ROWS, MLSTM_ROWS)
        xt = _mix_ffn(xt, bg, u, conv_w[l], conv_b[l][None], y_mlstm, norm_mix_post[l][None], w_mix_out,
                      norm_ffn2_pre[l][None], norm_ffn2_post[l][None], w_ffn2_in, w_ffn2_out, l, seq, tm)
    return xt.reshape(batch, seq, D_MODEL)
```

```python
import functools

import jax
import jax.numpy as jnp
from jax import lax
from jax.experimental import pallas as pl
from jax.experimental.pallas import tpu as pltpu

D_MODEL = 1024
D_FF = 2816
CONV_WIDTH = 512
HEADS = 4
DK = 64
DV = 128
CHUNK = 128
EPS = 1e-6
NEG_INF = -1e30

FF_TILE = 256
FF_STAGE_CHUNKS = 8
MIX_STAGE_CHUNKS = 8
MLSTM_ROWS = 2048
SCAN_ROWS = 2048
DVA = DV + 16
HALO_ROWS = 16
NGR = 2 * HEADS
CT_ROWS = (HEADS // 2) * DVA
VMEM_LIMIT = 56 * 1024 * 1024

P_E, P_M, P_B, P_G, P_MC = (slice(i * NGR, (i + 1) * NGR) for i in range(5))

F32 = jnp.float32
BF16 = jnp.bfloat16


def _rms(x, g):
    return x * lax.rsqrt(jnp.mean(x * x, axis=-1, keepdims=True) + EPS) * g


def _log_sigmoid(z):
    return jnp.minimum(z, 0.0) - jnp.log1p(jnp.exp(-jnp.abs(z)))


def _resident(shape):
    zeros = (0,) * len(shape)
    return pl.BlockSpec(shape, lambda *_: zeros, pipeline_mode=pl.Buffered(1))


def _split3(x):
    hi = x.astype(BF16)
    r1 = x - hi.astype(F32)
    mid = r1.astype(BF16)
    lo = (r1 - mid.astype(F32)).astype(BF16)
    return hi, mid, lo


def _visible(rev):
    s = lax.broadcasted_iota(jnp.int32, (CHUNK, CHUNK), 0)
    t = lax.broadcasted_iota(jnp.int32, (CHUNK, CHUNK), 1)
    return (s >= t) if rev else (s <= t)


def _running_max(x, rev):
    n = x.shape[1]
    pos = lax.broadcasted_iota(jnp.int32, x.shape, 1) & (CHUNK - 1)
    k = 1
    while k < CHUNK:
        if rev:
            shifted, ok = pltpu.roll(x, n - k, 1), pos < CHUNK - k
        else:
            shifted, ok = pltpu.roll(x, k, 1), pos >= k
        x = jnp.maximum(x, jnp.where(ok, shifted, NEG_INF))
        k *= 2
    return x


def _stage_bf16(src_hbm, dst_ref, stage_ref, sem, rows=None):
    rows = stage_ref.shape[1] if rows is None else rows
    n_chunks = src_hbm.shape[0] // rows

    def copy(c):
        return pltpu.make_async_copy(src_hbm.at[pl.ds(c * rows, rows), :],
                                     stage_ref.at[c % 2, pl.ds(0, rows), :], sem.at[c % 2])

    copy(0).start()
    for c in range(n_chunks):
        if c + 1 < n_chunks:
            copy(c + 1).start()
        copy(c).wait()
        dst_ref[c * rows:(c + 1) * rows, :] = stage_ref[c % 2, 0:rows, :].astype(BF16)


def _mix_out_tile(tiles_per_seq, x_ref, bg_ref, u_ref, uprev_ref, unext_ref, cw_ref, cb_ref,
                  y_ref, wmo_ref, gmix_ref):
    i = pl.program_id(0)
    tm = u_ref.shape[0]
    u = u_ref[...].astype(F32)
    has_prev = (i % tiles_per_seq != 0).astype(F32)
    has_next = (i % tiles_per_seq != tiles_per_seq - 1).astype(F32)
    halo = uprev_ref.shape[0]
    prev_row = uprev_ref[halo - 1:halo, :].astype(F32) * has_prev
    next_row = unext_ref[0:1, :].astype(F32) * has_next
    ri = lax.broadcasted_iota(jnp.int32, u.shape, 0)
    u_m1 = jnp.where(ri == 0, prev_row, pltpu.roll(u, 1, 0))
    u_p1 = jnp.where(ri == tm - 1, next_row, pltpu.roll(u, tm - 1, 0))
    conv = cw_ref[0:1, :] * u_m1 + cw_ref[1:2, :] * u + cw_ref[2:3, :] * u_p1
    y_conv = (bg_ref[...].astype(F32) * (conv + cb_ref[...])).astype(BF16)
    h = jnp.dot(y_conv, wmo_ref[0:CONV_WIDTH, :], preferred_element_type=F32) \
        + jnp.dot(y_ref[...], wmo_ref[CONV_WIDTH:, :], preferred_element_type=F32)
    return x_ref[...] + _rms(h, gmix_ref[...])


def _ffn_tile(x, gpre_ref, gpost_ref, win_ref, wout_ref, h_ref):
    xn = _rms(x, gpre_ref[...]).astype(BF16)
    for j in range(D_FF // FF_TILE):
        lo = j * FF_TILE
        gate = jnp.dot(xn, win_ref[:, lo:lo + FF_TILE], preferred_element_type=F32)
        up = jnp.dot(xn, win_ref[:, D_FF + lo:D_FF + lo + FF_TILE], preferred_element_type=F32)
        h_ref[:, lo:lo + FF_TILE] = (gate * jax.nn.sigmoid(gate) * up).astype(BF16)
    y = jnp.dot(h_ref[...], wout_ref[...], preferred_element_type=F32)
    return x + _rms(y, 0.5 * gpost_ref[...])


def _ffn_kernel(layer, x_ref, gpre_ref, gpost_ref, win_hbm, wout_hbm, o_ref,
                h_ref, win_ref, wout_ref, stage_in, stage_out, sem):
    @pl.when(pl.program_id(0) == 0)
    def _():
        _stage_bf16(win_hbm.at[layer], win_ref, stage_in, sem)
        _stage_bf16(wout_hbm.at[layer], wout_ref, stage_out, sem)

    o_ref[...] = _ffn_tile(x_ref[...], gpre_ref, gpost_ref, win_ref, wout_ref, h_ref)


def _mix_ffn_kernel(layer, tiles_per_seq, x_ref, bg_ref, u_ref, uprev_ref, unext_ref, cw_ref, cb_ref,
                    y_ref, gmix_ref, wmo_hbm, gpre_ref, gpost_ref, win_hbm, wout_hbm, o_ref,
                    h_ref, win_ref, wout_ref, wmo_ref, stage_in, stage_out, sem):
    @pl.when(pl.program_id(0) == 0)
    def _():
        _stage_bf16(wmo_hbm.at[layer], wmo_ref, stage_out, sem, rows=D_MODEL // 4)
        _stage_bf16(win_hbm.at[layer], win_ref, stage_in, sem)
        _stage_bf16(wout_hbm.at[layer], wout_ref, stage_out, sem)

    x = _mix_out_tile(tiles_per_seq, x_ref, bg_ref, u_ref, uprev_ref, unext_ref, cw_ref, cb_ref,
                      y_ref, wmo_ref, gmix_ref)
    o_ref[...] = _ffn_tile(x, gpre_ref, gpost_ref, win_ref, wout_ref, h_ref)


def _ffn_scratch(tm):
    return [
        pltpu.VMEM((tm, D_FF), BF16),
        pltpu.VMEM((D_MODEL, 2 * D_FF), BF16),
        pltpu.VMEM((D_FF, D_MODEL), BF16),
    ], [
        pltpu.VMEM((2, D_MODEL // FF_STAGE_CHUNKS, 2 * D_FF), F32),
        pltpu.VMEM((2, D_FF // FF_STAGE_CHUNKS, D_MODEL), F32),
        pltpu.SemaphoreType.DMA((2,)),
    ]


def _ffn(x, gpre, gpost, w_in, w_out, layer, tm):
    T = x.shape[0]
    resident, staging = _ffn_scratch(tm)
    return pl.pallas_call(
        functools.partial(_ffn_kernel, layer),
        grid=(T // tm,),
        in_specs=[
            pl.BlockSpec((tm, D_MODEL), lambda i: (i, 0)),
            _resident((1, D_MODEL)),
            _resident((1, D_MODEL)),
            pl.BlockSpec(memory_space=pl.ANY),
            pl.BlockSpec(memory_space=pl.ANY),
        ],
        out_specs=pl.BlockSpec((tm, D_MODEL), lambda i: (i, 0)),
        out_shape=jax.ShapeDtypeStruct((T, D_MODEL), F32),
        scratch_shapes=resident + staging,
        compiler_params=pltpu.CompilerParams(
            dimension_semantics=("arbitrary",), vmem_limit_bytes=VMEM_LIMIT),
        name="ffn",
    )(x, gpre, gpost, w_in, w_out)


def _mix_ffn(x, bg, u, conv_w, conv_b, y_mlstm, gmix, w_mix_out, gpre, gpost, w_in, w_out, layer, seq, tm):
    T = x.shape[0]
    tiles_per_seq = seq // tm
    sub = tm // HALO_ROWS
    last = T // HALO_ROWS - 1
    row = lambda w: pl.BlockSpec((tm, w), lambda i: (i, 0))
    hbm = pl.BlockSpec(memory_space=pl.ANY)
    resident, staging = _ffn_scratch(tm)
    return pl.pallas_call(
        functools.partial(_mix_ffn_kernel, layer, tiles_per_seq),
        grid=(T // tm,),
        in_specs=[
            row(D_MODEL), row(CONV_WIDTH), row(CONV_WIDTH),
            pl.BlockSpec((HALO_ROWS, CONV_WIDTH), lambda i: (jnp.maximum(i * sub - 1, 0), 0)),
            pl.BlockSpec((HALO_ROWS, CONV_WIDTH), lambda i: (jnp.minimum((i + 1) * sub, last), 0)),
            _resident((3, CONV_WIDTH)), _resident((1, CONV_WIDTH)),
            row(HEADS * DV), _resident((1, D_MODEL)), hbm,
            _resident((1, D_MODEL)), _resident((1, D_MODEL)), hbm, hbm,
        ],
        out_specs=row(D_MODEL),
        out_shape=jax.ShapeDtypeStruct((T, D_MODEL), F32),
        scratch_shapes=resident + [pltpu.VMEM((D_MODEL, D_MODEL), BF16)] + staging,
        compiler_params=pltpu.CompilerParams(
            dimension_semantics=("arbitrary",), vmem_limit_bytes=VMEM_LIMIT),
        name="mix_ffn",
    )(x, bg, u, u, u, conv_w, conv_b, y_mlstm, gmix, w_mix_out, gpre, gpost, w_in, w_out)


def _gate_rows(zr, p_ref, ccol_ref):
    L = CHUNK
    n_chunks = zr.shape[1] // L
    fwd_row = lax.broadcasted_iota(jnp.int32, (NGR, L), 0) < HEADS
    fwd_col = lax.broadcasted_iota(jnp.int32, (NGR, 1), 0) < HEADS
    li = zr[0:NGR]
    gates = jnp.concatenate([li, _log_sigmoid(zr[NGR:2 * NGR])], axis=0)
    x3 = jnp.concatenate(_split3(gates), axis=0)
    stacked = jnp.concatenate([x3[:, c * L:(c + 1) * L] for c in range(n_chunks)], axis=0)
    cum_ops = jnp.concatenate([_visible(False), _visible(True)], axis=1).astype(BF16)
    cum = jnp.dot(stacked, cum_ops, preferred_element_type=F32)
    b_chunks = []
    for c in range(n_chunks):
        blk = cum[c * 48:(c + 1) * 48]
        s16 = blk[0:16] + blk[16:32] + blk[32:48]
        b_chunks.append(jnp.where(fwd_row, s16[NGR:2 * NGR, 0:L], s16[NGR:2 * NGR, L:2 * L]))
    b = jnp.concatenate(b_chunks, axis=1)
    cc = li - b
    fwd_all = lax.broadcasted_iota(jnp.int32, cc.shape, 0) < HEADS
    p_ref[P_M, :] = jnp.where(fwd_all, _running_max(cc, False), _running_max(cc, True))
    p_ref[P_B, :] = b
    pad = jnp.zeros((L - NGR, L), F32)
    for c in range(n_chunks):
        sl = slice(c * L, (c + 1) * L)
        b_c, cc_c = b_chunks[c], cc[:, sl]
        g = jnp.where(fwd_col, b_c[:, L - 1:L], b_c[:, 0:1])
        m_chunk = g + jnp.max(cc_c, axis=1, keepdims=True)
        p_ref[P_E, sl] = jnp.exp(g + cc_c - m_chunk)
        p_ref[P_G, sl] = jnp.broadcast_to(g, (NGR, L))
        p_ref[P_MC, sl] = jnp.broadcast_to(m_chunk, (NGR, L))
        ccol_ref[sl, :] = jnp.concatenate([cc_c, pad], axis=0).T


def _mix_in_kernel(layer, x_ref, gpre_ref, w_hbm, gbias_ref,
                   bg_ref, u_ref, qt_ref, k_ref, ksw_ref, vt_ref, o_ref, p_ref, ccol_ref,
                   wb_ref, stage_ref, sem):
    W, HK, HV = CONV_WIDTH, HEADS * DK, HEADS * DV
    q0 = 3 * W
    k0, v0 = q0 + HK, q0 + 2 * HK
    o0 = v0 + HV
    g0 = o0 + HV

    @pl.when(pl.program_id(0) == 0)
    def _():
        src = w_hbm.at[layer]
        rows = stage_ref.shape[1]
        n_chunks = g0 // rows

        def copy(c):
            n = rows if c < n_chunks else 2 * NGR
            return pltpu.make_async_copy(src.at[pl.ds(c * rows, n), :], stage_ref.at[c % 2, pl.ds(0, n), :],
                                         sem.at[c % 2])

        copy(0).start()
        for c in range(n_chunks + 1):
            if c < n_chunks:
                copy(c + 1).start()
            copy(c).wait()
            if c < n_chunks:
                wb_ref[:, c * rows:(c + 1) * rows] = stage_ref[c % 2].T.astype(BF16)
            else:
                gt = stage_ref[c % 2, 0:128, :].T
                lane = lax.broadcasted_iota(jnp.int32, gt.shape, 1)
                wb_ref[:, g0:g0 + 128] = jnp.where(lane < 2 * NGR, gt, 0.0).astype(BF16)

    xn = _rms(x_ref[...], gpre_ref[...]).astype(BF16)
    proj = lambda a, b: jnp.dot(xn, wb_ref[:, a:b], preferred_element_type=F32)

    zg = proj(g0, g0 + 128) + gbias_ref[...]
    bg_ref[...] = proj(0, W).astype(BF16)
    u_ref[...] = (proj(W, 2 * W) * proj(2 * W, 3 * W)).astype(BF16)
    _gate_rows(zg.T[0:2 * NGR], p_ref, ccol_ref)
    qt_ref[...] = (proj(q0, k0) * (DK ** -0.5)).T.astype(BF16)
    kk = proj(k0, v0)
    k_ref[...] = kk.astype(BF16)
    for pair in range(HEADS // 2):
        ps = slice(pair * 2 * DK, (pair + 1) * 2 * DK)
        ksw_ref[:, ps] = pltpu.roll(kk[:, ps], DK, 1).astype(BF16)
    vt_ref[...] = proj(v0, o0).T.astype(BF16)
    o_ref[...] = proj(o0, g0).astype(BF16)


def _mix_in(x, gpre, w, gbias, layer, tm):
    T = x.shape[0]
    row = lambda w: pl.BlockSpec((tm, w), lambda i: (i, 0))
    col = lambda h: pl.BlockSpec((h, tm), lambda i: (0, i))
    HK, HV = HEADS * DK, HEADS * DV
    g0 = 3 * CONV_WIDTH + 2 * HK + 2 * HV
    return pl.pallas_call(
        functools.partial(_mix_in_kernel, layer),
        grid=(T // tm,),
        in_specs=[row(D_MODEL), _resident((1, D_MODEL)), pl.BlockSpec(memory_space=pl.ANY),
                  _resident((1, 128))],
        scratch_shapes=[pltpu.VMEM((D_MODEL, g0 + 128), BF16),
                        pltpu.VMEM((2, g0 // MIX_STAGE_CHUNKS, D_MODEL), F32),
                        pltpu.SemaphoreType.DMA((2,))],
        out_specs=[row(CONV_WIDTH), row(CONV_WIDTH), col(HK), row(HK), row(HK), col(HV), row(HV),
                   col(5 * NGR), row(128)],
        out_shape=[
            jax.ShapeDtypeStruct((T, CONV_WIDTH), BF16),
            jax.ShapeDtypeStruct((T, CONV_WIDTH), BF16),
            jax.ShapeDtypeStruct((HK, T), BF16),
            jax.ShapeDtypeStruct((T, HK), BF16),
            jax.ShapeDtypeStruct((T, HK), BF16),
            jax.ShapeDtypeStruct((HV, T), BF16),
            jax.ShapeDtypeStruct((T, HV), BF16),
            jax.ShapeDtypeStruct((5 * NGR, T), F32),
            jax.ShapeDtypeStruct((T, 128), F32),
        ],
        compiler_params=pltpu.CompilerParams(
            dimension_semantics=("arbitrary",), vmem_limit_bytes=VMEM_LIMIT),
        name="mix_in",
    )(x, gpre, w, gbias)


def _state_half(d, h):
    return (h % 2) ^ d


def _scan_direction(d, vt_ref, k_ref, p_ref, ct_ref, mp_ref, ct_state, m_state):
    L = CHUNK
    n_chunks = k_ref.shape[0] // L
    lane_half = lax.broadcasted_iota(jnp.int32, (L, 128), 1) // DK
    lane_half_s = lax.broadcasted_iota(jnp.int32, (DVA, 128), 1) // DK
    ones_rows = jnp.ones((DVA - DV, L), BF16)

    cts = [ct_state[d, h] for h in range(HEADS)]
    m_prev = m_state[d]
    for chunk in (range(n_chunks - 1, -1, -1) if d == 1 else range(n_chunks)):
        sl = slice(chunk * L, (chunk + 1) * L)
        e, g, m_chunk = p_ref[P_E, sl], p_ref[P_G, sl], p_ref[P_MC, sl]
        mp_ref[:, sl] = m_prev
        m_new = jnp.maximum(g + m_prev, m_chunk)
        a_old = jnp.exp(g + m_prev - m_new)
        a_new = jnp.exp(m_chunk - m_new)
        new_cts = []
        for h in range(HEADS):
            r = HEADS * d + h
            vt_aug = jnp.concatenate([vt_ref[h * DV:(h + 1) * DV, sl], ones_rows], axis=0)
            vte = (vt_aug.astype(F32) * e[r:r + 1, :]).astype(BF16)
            pair = slice((h // 2) * 2 * DK, (h // 2 + 1) * 2 * DK)
            k_half = jnp.where(lane_half == _state_half(d, h), k_ref[sl, pair], jnp.zeros((L, 128), BF16))
            ct_chunk = jnp.dot(vte, k_half, preferred_element_type=F32)
            new_cts.append(a_old[r:r + 1, 0:1] * cts[h] + a_new[r:r + 1, 0:1] * ct_chunk)
        for pr in range(HEADS // 2):
            both = jnp.where(lane_half_s == _state_half(d, 2 * pr), cts[2 * pr], cts[2 * pr + 1])
            ct_ref[chunk, pr * DVA:(pr + 1) * DVA, :] = both.astype(BF16)
        cts = new_cts
        m_prev = m_new
    for h in range(HEADS):
        ct_state[d, h] = cts[h]
    m_state[d] = m_prev


def _mlstm_scan_kernel(vt_f, k_f, p_f, vt_b, ksw_b, p_b, ct_f, mp_f, ct_b, mp_b, ct_state, m_state):
    @pl.when(pl.program_id(1) == 0)
    def _():
        ct_state[...] = jnp.zeros(ct_state.shape, F32)
        m_state[...] = jnp.full(m_state.shape, NEG_INF, F32)

    _scan_direction(0, vt_f, k_f, p_f, ct_f, mp_f, ct_state, m_state)
    _scan_direction(1, vt_b, ksw_b, p_b, ct_b, mp_b, ct_state, m_state)


def _mlstm_out_kernel(qt_ref, k_ref, vt_ref, p_ref, mp_f, mp_b, ccol_ref, ct_f, ct_b,
                      o_ref, gain_ref, y_ref):
    L = CHUNK
    n_chunks = k_ref.shape[0] // L
    visible = (_visible(False), _visible(True))
    fwd_row = lax.broadcasted_iota(jnp.int32, (NGR, L), 0) < HEADS
    lane_half = lax.broadcasted_iota(jnp.int32, (DVA, 128), 1) // DK
    ones_rows = jnp.ones((DVA - DV, L), BF16)
    zq = jnp.zeros((DK, L), BF16)
    for chunk in range(n_chunks):
        sl = slice(chunk * L, (chunk + 1) * L)
        m_prev = jnp.where(fwd_row, mp_f[:, sl], mp_b[:, sl])
        n_t = jnp.maximum(m_prev, p_ref[P_M, sl])
        f_inter = jnp.exp(m_prev - n_t)
        e_min = jnp.exp(-(p_ref[P_B, sl] + n_t))
        ccol = ccol_ref[sl, :]
        for h in range(HEADS):
            hs = slice(h * DV, (h + 1) * DV)
            qt = qt_ref[h * DK:(h + 1) * DK, sl]
            k_pair = k_ref[sl, (h // 2) * 2 * DK:(h // 2 + 1) * 2 * DK]
            qt_pair = jnp.concatenate([qt, zq] if h % 2 == 0 else [zq, qt], axis=0)
            st = jnp.dot(k_pair, qt_pair, preferred_element_type=F32)
            pts, qfs = [], []
            for d in range(2):
                r = HEADS * d + h
                arg = jnp.where(visible[d], ccol[:, r:r + 1] - n_t[r:r + 1, :], NEG_INF)
                pts.append((jnp.exp(arg) * st).astype(BF16))
                qfs.append((qt.astype(F32) * f_inter[r:r + 1, :]).astype(BF16))
            inter = [jnp.concatenate([qfs[0], zq], axis=1), jnp.concatenate([zq, qfs[1]], axis=1)]
            first = 0 if _state_half(0, h) == 0 else 1
            rhs = jnp.concatenate([jnp.concatenate(pts, axis=1), inter[first], inter[1 - first]], axis=0)
            vt_aug = jnp.concatenate([vt_ref[hs, sl], ones_rows], axis=0)
            ps = slice((h // 2) * DVA, (h // 2 + 1) * DVA)
            blocks = (ct_f[chunk, ps, :], ct_b[chunk, ps, :])
            ct = jnp.where(lane_half == 0, blocks[first], blocks[1 - first])
            both = jnp.dot(jnp.concatenate([vt_aug, ct], axis=1), rhs, preferred_element_type=F32)
            ht = None
            for d in range(2):
                r = HEADS * d + h
                numer = both[0:DV, d * L:(d + 1) * L]
                denom = both[DV:DV + 1, d * L:(d + 1) * L]
                part = numer / jnp.maximum(jnp.abs(denom), e_min[r:r + 1, :])
                ht = part if ht is None else ht + part
            ms = jnp.mean(ht * ht, axis=0, keepdims=True)
            hn = ht * lax.rsqrt(ms + EPS) * gain_ref[hs, :]
            y_ref[sl, hs] = (jax.nn.sigmoid(o_ref[sl, hs].astype(F32)) * hn.T).astype(BF16)


def _mlstm(qt, k, ksw, vt, p, ccol, o, gain_b, batch, seq, scan_rows, rows):
    T = batch * seq
    ng = seq // scan_rows
    HK, HV = HEADS * DK, HEADS * DV

    def scan_specs(group_of):
        blk = lambda b, j: b * ng + group_of(j)
        ins = [pl.BlockSpec((HV, scan_rows), lambda b, j: (0, blk(b, j))),
               pl.BlockSpec((scan_rows, HK), lambda b, j: (blk(b, j), 0)),
               pl.BlockSpec((5 * NGR, scan_rows), lambda b, j: (0, blk(b, j)))]
        outs = [pl.BlockSpec((scan_rows // CHUNK, CT_ROWS, 128), lambda b, j: (blk(b, j), 0, 0)),
                pl.BlockSpec((NGR, scan_rows), lambda b, j: (0, blk(b, j)))]
        return ins, outs

    ins_f, outs_f = scan_specs(lambda j: j)
    ins_b, outs_b = scan_specs(lambda j: ng - 1 - j)
    state_shapes = [jax.ShapeDtypeStruct((T // CHUNK, CT_ROWS, 128), BF16),
                    jax.ShapeDtypeStruct((NGR, T), F32)]
    ct_f, mp_f, ct_b, mp_b = pl.pallas_call(
        _mlstm_scan_kernel,
        grid=(batch, ng),
        in_specs=ins_f + ins_b,
        out_specs=outs_f + outs_b,
        out_shape=state_shapes + state_shapes,
        scratch_shapes=[pltpu.VMEM((2, HEADS, DVA, 128), F32), pltpu.VMEM((2, NGR, 128), F32)],
        compiler_params=pltpu.CompilerParams(dimension_semantics=("arbitrary", "arbitrary"),
                                             vmem_limit_bytes=VMEM_LIMIT),
        name="mlstm_scan",
    )(vt, k, p, vt, ksw, p)

    row = lambda w: pl.BlockSpec((rows, w), lambda i: (i, 0))
    col = lambda h: pl.BlockSpec((h, rows), lambda i: (0, i))
    ctb = pl.BlockSpec((rows // CHUNK, CT_ROWS, 128), lambda i: (i, 0, 0))
    return pl.pallas_call(
        _mlstm_out_kernel,
        grid=(T // rows,),
        in_specs=[col(HK), row(HK), col(HV), col(5 * NGR), col(NGR), col(NGR), row(128), ctb, ctb,
                  row(HV), _resident((HV, 128))],
        out_specs=row(HV),
        out_shape=jax.ShapeDtypeStruct((T, HV), BF16),
        compiler_params=pltpu.CompilerParams(dimension_semantics=("arbitrary",),
                                             vmem_limit_bytes=VMEM_LIMIT),
        name="mlstm_out",
    )(qt, k, vt, p, mp_f, mp_b, ccol, ct_f, ct_b, o, gain_b)


def kernel(x, norm_ffn1_pre, norm_ffn1_post, w_ffn1_in, w_ffn1_out, norm_mix_pre, norm_mix_post,
           w_mix_in, conv_w, conv_b, gate_i_bias, gate_f_bias, mlstm_norm, w_mix_out,
           norm_ffn2_pre, norm_ffn2_post, w_ffn2_in, w_ffn2_out):
    batch, seq, _ = x.shape
    T = batch * seq
    depth = norm_ffn1_pre.shape[0]
    tm = 512
    xt = x.reshape(T, D_MODEL)
    HV = HEADS * DV
    for l in range(depth):
        xt = _ffn(xt, norm_ffn1_pre[l][None], norm_ffn1_post[l][None], w_ffn1_in, w_ffn1_out, l, 2 * tm)

        gbias = jnp.pad(jnp.concatenate([gate_i_bias[l], gate_f_bias[l]]), (0, 128 - 2 * NGR))[None]
        bg, u, qt, k, ksw, vt, o, p, ccol = _mix_in(
            xt, norm_mix_pre[l][None], jnp.swapaxes(w_mix_in, 1, 2), gbias, l, 2 * tm)
        gain_b = jnp.broadcast_to(mlstm_norm[l][:, None], (HV, 128))
        y_mlstm = _mlstm(qt, k, ksw, vt, p, ccol, o, gain_b, batch, seq, SCAN_ROWS, MLSTM_ROWS)
        xt = _mix_ffn(xt, bg, u, conv_w[l], conv_b[l][None], y_mlstm, norm_mix_post[l][None], w_mix_out,
                      norm_ffn2_pre[l][None], norm_ffn2_post[l][None], w_ffn2_in, w_ffn2_out, l, seq, tm)
    return xt.reshape(batch, seq, D_MODEL)
```

```python
import functools

import jax
import jax.numpy as jnp
from jax import lax
from jax.experimental import pallas as pl
from jax.experimental.pallas import tpu as pltpu

D_MODEL = 1024
D_FF = 2816
CONV_WIDTH = 512
HEADS = 4
DK = 64
DV = 128
CHUNK = 128
EPS = 1e-6
NEG_INF = -1e30

FF_TILE = 256
FF_STAGE_CHUNKS = 8
FF_SIDE_CHUNKS = 8
MIX_STAGE_CHUNKS = 8
MLSTM_ROWS = 2048
SCAN_ROWS = 2048
DVA = DV + 16
HALO_ROWS = 8
NGR = 2 * HEADS
CT_ROWS = (HEADS // 2) * DVA
VMEM_LIMIT = 56 * 1024 * 1024

P_E, P_M, P_B, P_G, P_MC = (slice(i * NGR, (i + 1) * NGR) for i in range(5))

F32 = jnp.float32
BF16 = jnp.bfloat16


def _rms(x, g):
    return x * lax.rsqrt(jnp.mean(x * x, axis=-1, keepdims=True) + EPS) * g


def _log_sigmoid(z):
    return jnp.minimum(z, 0.0) - jnp.log1p(jnp.exp(-jnp.abs(z)))


def _resident(shape):
    zeros = (0,) * len(shape)
    return pl.BlockSpec(shape, lambda *_: zeros, pipeline_mode=pl.Buffered(1))


def _split3(x):
    hi = x.astype(BF16)
    r1 = x - hi.astype(F32)
    mid = r1.astype(BF16)
    lo = (r1 - mid.astype(F32)).astype(BF16)
    return hi, mid, lo


def _visible(rev):
    s = lax.broadcasted_iota(jnp.int32, (CHUNK, CHUNK), 0)
    t = lax.broadcasted_iota(jnp.int32, (CHUNK, CHUNK), 1)
    return (s >= t) if rev else (s <= t)


def _running_max(x, rev):
    n = x.shape[1]
    pos = lax.broadcasted_iota(jnp.int32, x.shape, 1) & (CHUNK - 1)
    k = 1
    while k < CHUNK:
        if rev:
            shifted, ok = pltpu.roll(x, n - k, 1), pos < CHUNK - k
        else:
            shifted, ok = pltpu.roll(x, k, 1), pos >= k
        x = jnp.maximum(x, jnp.where(ok, shifted, NEG_INF))
        k *= 2
    return x


def _stage_bf16(src_hbm, dst_ref, stage_ref, sem, rows=None):
    rows = stage_ref.shape[1] if rows is None else rows
    n_chunks = src_hbm.shape[0] // rows

    def copy(c):
        return pltpu.make_async_copy(src_hbm.at[pl.ds(c * rows, rows), :],
                                     stage_ref.at[c % 2, pl.ds(0, rows), :], sem.at[c % 2])

    copy(0).start()
    for c in range(n_chunks):
        if c + 1 < n_chunks:
            copy(c + 1).start()
        copy(c).wait()
        dst_ref[c * rows:(c + 1) * rows, :] = stage_ref[c % 2, 0:rows, :].astype(BF16)


def _mix_out_tile(tiles_per_seq, x_ref, bg_ref, u_ref, uprev_ref, unext_ref, cw_ref, cb_ref,
                  y_ref, wmo_ref, gmix_ref):
    i = pl.program_id(0)
    tm = u_ref.shape[0]
    u = u_ref[...]
    has_prev = (i % tiles_per_seq != 0).astype(F32)
    has_next = (i % tiles_per_seq != tiles_per_seq - 1).astype(F32)
    prev_row = uprev_ref[HALO_ROWS - 1:HALO_ROWS, :] * has_prev
    next_row = unext_ref[0:1, :] * has_next
    ri = lax.broadcasted_iota(jnp.int32, u.shape, 0)
    u_m1 = jnp.where(ri == 0, prev_row, pltpu.roll(u, 1, 0))
    u_p1 = jnp.where(ri == tm - 1, next_row, pltpu.roll(u, tm - 1, 0))
    conv = cw_ref[0:1, :] * u_m1 + cw_ref[1:2, :] * u + cw_ref[2:3, :] * u_p1
    y_conv = (bg_ref[...] * (conv + cb_ref[...])).astype(BF16)
    h = jnp.dot(y_conv, wmo_ref[0:CONV_WIDTH, :], preferred_element_type=F32) \
        + jnp.dot(y_ref[...], wmo_ref[CONV_WIDTH:, :], preferred_element_type=F32)
    return x_ref[...] + _rms(h, gmix_ref[...])


def _ffn_tile(x, gpre_ref, gpost_ref, win_ref, wout_ref, h_ref):
    xn = _rms(x, gpre_ref[...]).astype(BF16)
    for j in range(D_FF // FF_TILE):
        lo = j * FF_TILE
        gate = jnp.dot(xn, win_ref[:, lo:lo + FF_TILE], preferred_element_type=F32)
        up = jnp.dot(xn, win_ref[:, D_FF + lo:D_FF + lo + FF_TILE], preferred_element_type=F32)
        h_ref[:, lo:lo + FF_TILE] = (gate * jax.nn.sigmoid(gate) * up).astype(BF16)
    y = jnp.dot(h_ref[...], wout_ref[...], preferred_element_type=F32)
    return x + _rms(y, 0.5 * gpost_ref[...])


def _zero_after(v):
    u = pltpu.bitcast(v, jnp.uint32)
    acc = None
    for r in range(u.shape[0] // 8):
        for c in range(u.shape[1] // 128):
            t = u[r * 8:(r + 1) * 8, c * 128:(c + 1) * 128]
            acc = t if acc is None else acc | t
    return pltpu.bitcast((acc >> 16) >> 16, F32)


def _ffn_kernel(layer, n_tiles, xa_ref, xc_ref, gpre_ref, gpost_ref, win_hbm, wout_hbm, o_ref,
                h_ref, win_ref, wout_ref, stage_in, stage_out, sem, xn_ref, y_ref):
    i = pl.program_id(0)
    tm = xa_ref.shape[0]
    g_half = 0.5 * gpost_ref[...]

    @pl.when(i == 0)
    def _():
        _stage_bf16(win_hbm.at[layer], win_ref, stage_in, sem)
        _stage_bf16(wout_hbm.at[layer], wout_ref, stage_out, sem)
        xn_ref[0] = _rms(xa_ref[...], gpre_ref[...]).astype(BF16)
        y_ref[...] = jnp.zeros(y_ref.shape, F32)

    @pl.when((i >= 1) & (i <= n_tiles))
    def _():
        xn = xn_ref[(i - 1) % 2]
        rp = tm // FF_SIDE_CHUNKS
        for j in range(D_FF // FF_TILE):
            lo = j * FF_TILE
            gate = jnp.dot(xn, win_ref[:, lo:lo + FF_TILE], preferred_element_type=F32)
            up = jnp.dot(xn, win_ref[:, D_FF + lo:D_FF + lo + FF_TILE], preferred_element_type=F32)
            hm = gate * jax.nn.sigmoid(gate) * up
            h_ref[:, lo:lo + FF_TILE] = hm.astype(BF16)
            if j < FF_SIDE_CHUNKS:
                rs = slice(j * rp, (j + 1) * rp)
                out_rows = xc_ref[rs, :] + _rms(y_ref[rs, :], g_half)
                o_ref[rs, :] = out_rows
                xn_rows = _rms(xa_ref[rs, :], gpre_ref[...])
                xn_ref[i % 2, rs, :] = xn_rows.astype(BF16)
                zero = _zero_after(out_rows) + _zero_after(xn_rows)
                h_ref[0:8, lo:lo + 128] = (hm[0:8, 0:128] + zero).astype(BF16)
        y_ref[...] = jnp.dot(h_ref[...], wout_ref[...], preferred_element_type=F32)

    @pl.when(i == n_tiles + 1)
    def _():
        o_ref[...] = xc_ref[...] + _rms(y_ref[...], g_half)


def _mix_ffn_kernel(layer, tiles_per_seq, x_ref, bg_ref, u_ref, uprev_ref, unext_ref, cw_ref, cb_ref,
                    y_ref, gmix_ref, wmo_hbm, gpre_ref, gpost_ref, win_hbm, wout_hbm, o_ref,
                    h_ref, win_ref, wout_ref, wmo_ref, stage_in, stage_out, sem):
    @pl.when(pl.program_id(0) == 0)
    def _():
        _stage_bf16(wmo_hbm.at[layer], wmo_ref, stage_out, sem, rows=D_MODEL // 4)
        _stage_bf16(win_hbm.at[layer], win_ref, stage_in, sem)
        _stage_bf16(wout_hbm.at[layer], wout_ref, stage_out, sem)

    x = _mix_out_tile(tiles_per_seq, x_ref, bg_ref, u_ref, uprev_ref, unext_ref, cw_ref, cb_ref,
                      y_ref, wmo_ref, gmix_ref)
    o_ref[...] = _ffn_tile(x, gpre_ref, gpost_ref, win_ref, wout_ref, h_ref)


def _ffn_scratch(tm):
    return [
        pltpu.VMEM((tm, D_FF), BF16),
        pltpu.VMEM((D_MODEL, 2 * D_FF), BF16),
        pltpu.VMEM((D_FF, D_MODEL), BF16),
    ], [
        pltpu.VMEM((2, D_MODEL // FF_STAGE_CHUNKS, 2 * D_FF), F32),
        pltpu.VMEM((2, D_FF // FF_STAGE_CHUNKS, D_MODEL), F32),
        pltpu.SemaphoreType.DMA((2,)),
    ]


def _ffn(x, gpre, gpost, w_in, w_out, layer, tm):
    T = x.shape[0]
    nt = T // tm
    resident, staging = _ffn_scratch(tm)
    tile = lambda lag: pl.BlockSpec((tm, D_MODEL), lambda i: (jnp.clip(i - lag, 0, nt - 1), 0))
    return pl.pallas_call(
        functools.partial(_ffn_kernel, layer, nt),
        grid=(nt + 2,),
        in_specs=[
            tile(0), tile(2),
            _resident((1, D_MODEL)),
            _resident((1, D_MODEL)),
            pl.BlockSpec(memory_space=pl.ANY),
            pl.BlockSpec(memory_space=pl.ANY),
        ],
        out_specs=tile(2),
        out_shape=jax.ShapeDtypeStruct((T, D_MODEL), F32),
        scratch_shapes=resident + staging + [pltpu.VMEM((2, tm, D_MODEL), BF16),
                                             pltpu.VMEM((tm, D_MODEL), F32)],
        compiler_params=pltpu.CompilerParams(
            dimension_semantics=("arbitrary",), vmem_limit_bytes=VMEM_LIMIT),
        name="ffn",
    )(x, x, gpre, gpost, w_in, w_out)


def _mix_ffn(x, bg, u, conv_w, conv_b, y_mlstm, gmix, w_mix_out, gpre, gpost, w_in, w_out, layer, seq, tm):
    T = x.shape[0]
    tiles_per_seq = seq // tm
    sub = tm // HALO_ROWS
    last = T // HALO_ROWS - 1
    row = lambda w: pl.BlockSpec((tm, w), lambda i: (i, 0))
    hbm = pl.BlockSpec(memory_space=pl.ANY)
    resident, staging = _ffn_scratch(tm)
    return pl.pallas_call(
        functools.partial(_mix_ffn_kernel, layer, tiles_per_seq),
        grid=(T // tm,),
        in_specs=[
            row(D_MODEL), row(CONV_WIDTH), row(CONV_WIDTH),
            pl.BlockSpec((HALO_ROWS, CONV_WIDTH), lambda i: (jnp.maximum(i * sub - 1, 0), 0)),
            pl.BlockSpec((HALO_ROWS, CONV_WIDTH), lambda i: (jnp.minimum((i + 1) * sub, last), 0)),
            _resident((3, CONV_WIDTH)), _resident((1, CONV_WIDTH)),
            row(HEADS * DV), _resident((1, D_MODEL)), hbm,
            _resident((1, D_MODEL)), _resident((1, D_MODEL)), hbm, hbm,
        ],
        out_specs=row(D_MODEL),
        out_shape=jax.ShapeDtypeStruct((T, D_MODEL), F32),
        scratch_shapes=resident + [pltpu.VMEM((D_MODEL, D_MODEL), BF16)] + staging,
        compiler_params=pltpu.CompilerParams(
            dimension_semantics=("arbitrary",), vmem_limit_bytes=VMEM_LIMIT),
        name="mix_ffn",
    )(x, bg, u, u, u, conv_w, conv_b, y_mlstm, gmix, w_mix_out, gpre, gpost, w_in, w_out)


def _gate_rows(zr, p_ref, ccol_ref):
    L = CHUNK
    n_chunks = zr.shape[1] // L
    fwd_row = lax.broadcasted_iota(jnp.int32, (NGR, L), 0) < HEADS
    fwd_col = lax.broadcasted_iota(jnp.int32, (NGR, 1), 0) < HEADS
    li = zr[0:NGR]
    gates = jnp.concatenate([li, _log_sigmoid(zr[NGR:2 * NGR])], axis=0)
    x3 = jnp.concatenate(_split3(gates), axis=0)
    stacked = jnp.concatenate([x3[:, c * L:(c + 1) * L] for c in range(n_chunks)], axis=0)
    cum_ops = jnp.concatenate([_visible(False), _visible(True)], axis=1).astype(BF16)
    cum = jnp.dot(stacked, cum_ops, preferred_element_type=F32)
    b_chunks = []
    for c in range(n_chunks):
        blk = cum[c * 48:(c + 1) * 48]
        s16 = blk[0:16] + blk[16:32] + blk[32:48]
        b_chunks.append(jnp.where(fwd_row, s16[NGR:2 * NGR, 0:L], s16[NGR:2 * NGR, L:2 * L]))
    b = jnp.concatenate(b_chunks, axis=1)
    cc = li - b
    fwd_all = lax.broadcasted_iota(jnp.int32, cc.shape, 0) < HEADS
    p_ref[P_M, :] = jnp.where(fwd_all, _running_max(cc, False), _running_max(cc, True))
    p_ref[P_B, :] = b
    pad = jnp.zeros((L - NGR, L), F32)
    for c in range(n_chunks):
        sl = slice(c * L, (c + 1) * L)
        b_c, cc_c = b_chunks[c], cc[:, sl]
        g = jnp.where(fwd_col, b_c[:, L - 1:L], b_c[:, 0:1])
        m_chunk = g + jnp.max(cc_c, axis=1, keepdims=True)
        p_ref[P_E, sl] = jnp.exp(g + cc_c - m_chunk)
        p_ref[P_G, sl] = jnp.broadcast_to(g, (NGR, L))
        p_ref[P_MC, sl] = jnp.broadcast_to(m_chunk, (NGR, L))
        ccol_ref[sl, :] = jnp.concatenate([cc_c, pad], axis=0).T


def _mix_in_kernel(layer, x_ref, gpre_ref, w_hbm, gbias_ref,
                   bg_ref, u_ref, qt_ref, k_ref, ksw_ref, vt_ref, o_ref, p_ref, ccol_ref,
                   wb_ref, stage_ref, sem):
    W, HK, HV = CONV_WIDTH, HEADS * DK, HEADS * DV
    q0 = 3 * W
    k0, v0 = q0 + HK, q0 + 2 * HK
    o0 = v0 + HV
    g0 = o0 + HV

    @pl.when(pl.program_id(0) == 0)
    def _():
        src = w_hbm.at[layer]
        rows = stage_ref.shape[1]
        n_chunks = g0 // rows

        def copy(c):
            n = rows if c < n_chunks else 2 * NGR
            return pltpu.make_async_copy(src.at[pl.ds(c * rows, n), :], stage_ref.at[c % 2, pl.ds(0, n), :],
                                         sem.at[c % 2])

        copy(0).start()
        for c in range(n_chunks + 1):
            if c < n_chunks:
                copy(c + 1).start()
            copy(c).wait()
            if c < n_chunks:
                wb_ref[:, c * rows:(c + 1) * rows] = stage_ref[c % 2].T.astype(BF16)
            else:
                gt = stage_ref[c % 2, 0:128, :].T
                lane = lax.broadcasted_iota(jnp.int32, gt.shape, 1)
                wb_ref[:, g0:g0 + 128] = jnp.where(lane < 2 * NGR, gt, 0.0).astype(BF16)

    xn = _rms(x_ref[...], gpre_ref[...]).astype(BF16)
    proj = lambda a, b: jnp.dot(xn, wb_ref[:, a:b], preferred_element_type=F32)

    zg = proj(g0, g0 + 128) + gbias_ref[...]
    bg_ref[...] = proj(0, W)
    u_ref[...] = proj(W, 2 * W) * proj(2 * W, 3 * W)
    _gate_rows(zg.T[0:2 * NGR], p_ref, ccol_ref)
    qt_ref[...] = (proj(q0, k0) * (DK ** -0.5)).T.astype(BF16)
    kk = proj(k0, v0)
    k_ref[...] = kk.astype(BF16)
    for pair in range(HEADS // 2):
        ps = slice(pair * 2 * DK, (pair + 1) * 2 * DK)
        ksw_ref[:, ps] = pltpu.roll(kk[:, ps], DK, 1).astype(BF16)
    vt_ref[...] = proj(v0, o0).T.astype(BF16)
    o_ref[...] = proj(o0, g0)


def _mix_in(x, gpre, w, gbias, layer, tm):
    T = x.shape[0]
    row = lambda w: pl.BlockSpec((tm, w), lambda i: (i, 0))
    col = lambda h: pl.BlockSpec((h, tm), lambda i: (0, i))
    HK, HV = HEADS * DK, HEADS * DV
    g0 = 3 * CONV_WIDTH + 2 * HK + 2 * HV
    return pl.pallas_call(
        functools.partial(_mix_in_kernel, layer),
        grid=(T // tm,),
        in_specs=[row(D_MODEL), _resident((1, D_MODEL)), pl.BlockSpec(memory_space=pl.ANY),
                  _resident((1, 128))],
        scratch_shapes=[pltpu.VMEM((D_MODEL, g0 + 128), BF16),
                        pltpu.VMEM((2, g0 // MIX_STAGE_CHUNKS, D_MODEL), F32),
                        pltpu.SemaphoreType.DMA((2,))],
        out_specs=[row(CONV_WIDTH), row(CONV_WIDTH), col(HK), row(HK), row(HK), col(HV), row(HV),
                   col(5 * NGR), row(128)],
        out_shape=[
            jax.ShapeDtypeStruct((T, CONV_WIDTH), F32),
            jax.ShapeDtypeStruct((T, CONV_WIDTH), F32),
            jax.ShapeDtypeStruct((HK, T), BF16),
            jax.ShapeDtypeStruct((T, HK), BF16),
            jax.ShapeDtypeStruct((T, HK), BF16),
            jax.ShapeDtypeStruct((HV, T), BF16),
            jax.ShapeDtypeStruct((T, HV), F32),
            jax.ShapeDtypeStruct((5 * NGR, T), F32),
            jax.ShapeDtypeStruct((T, 128), F32),
        ],
        compiler_params=pltpu.CompilerParams(
            dimension_semantics=("arbitrary",), vmem_limit_bytes=VMEM_LIMIT),
        name="mix_in",
    )(x, gpre, w, gbias)


def _state_half(d, h):
    return (h % 2) ^ d


def _scan_direction(d, vt_ref, k_ref, p_ref, ct_ref, mp_ref, ct_state, m_state):
    L = CHUNK
    n_chunks = k_ref.shape[0] // L
    lane_half = lax.broadcasted_iota(jnp.int32, (L, 128), 1) // DK
    lane_half_s = lax.broadcasted_iota(jnp.int32, (DVA, 128), 1) // DK
    ones_rows = jnp.ones((DVA - DV, L), BF16)

    cts = [ct_state[d, h] for h in range(HEADS)]
    m_prev = m_state[d]
    for chunk in (range(n_chunks - 1, -1, -1) if d == 1 else range(n_chunks)):
        sl = slice(chunk * L, (chunk + 1) * L)
        e, g, m_chunk = p_ref[P_E, sl], p_ref[P_G, sl], p_ref[P_MC, sl]
        mp_ref[:, sl] = m_prev
        m_new = jnp.maximum(g + m_prev, m_chunk)
        a_old = jnp.exp(g + m_prev - m_new)
        a_new = jnp.exp(m_chunk - m_new)
        new_cts = []
        for h in range(HEADS):
            r = HEADS * d + h
            vt_aug = jnp.concatenate([vt_ref[h * DV:(h + 1) * DV, sl], ones_rows], axis=0)
            vte = (vt_aug.astype(F32) * e[r:r + 1, :]).astype(BF16)
            pair = slice((h // 2) * 2 * DK, (h // 2 + 1) * 2 * DK)
            k_half = jnp.where(lane_half == _state_half(d, h), k_ref[sl, pair], jnp.zeros((L, 128), BF16))
            ct_chunk = jnp.dot(vte, k_half, preferred_element_type=F32)
            new_cts.append(a_old[r:r + 1, 0:1] * cts[h] + a_new[r:r + 1, 0:1] * ct_chunk)
        for pr in range(HEADS // 2):
            both = jnp.where(lane_half_s == _state_half(d, 2 * pr), cts[2 * pr], cts[2 * pr + 1])
            ct_ref[chunk, pr * DVA:(pr + 1) * DVA, :] = both.astype(BF16)
        cts = new_cts
        m_prev = m_new
    for h in range(HEADS):
        ct_state[d, h] = cts[h]
    m_state[d] = m_prev


def _mlstm_scan_kernel(vt_f, k_f, p_f, vt_b, ksw_b, p_b, ct_f, mp_f, ct_b, mp_b, ct_state, m_state):
    @pl.when(pl.program_id(1) == 0)
    def _():
        ct_state[...] = jnp.zeros(ct_state.shape, F32)
        m_state[...] = jnp.full(m_state.shape, NEG_INF, F32)

    _scan_direction(0, vt_f, k_f, p_f, ct_f, mp_f, ct_state, m_state)
    _scan_direction(1, vt_b, ksw_b, p_b, ct_b, mp_b, ct_state, m_state)


def _mlstm_out_kernel(qt_ref, k_ref, vt_ref, p_ref, mp_f, mp_b, ccol_ref, ct_f, ct_b,
                      o_ref, gain_ref, y_ref):
    L = CHUNK
    n_chunks = k_ref.shape[0] // L
    visible = (_visible(False), _visible(True))
    fwd_row = lax.broadcasted_iota(jnp.int32, (NGR, L), 0) < HEADS
    lane_half = lax.broadcasted_iota(jnp.int32, (DVA, 128), 1) // DK
    ones_rows = jnp.ones((DVA - DV, L), BF16)
    zq = jnp.zeros((DK, L), BF16)
    for chunk in range(n_chunks):
        sl = slice(chunk * L, (chunk + 1) * L)
        m_prev = jnp.where(fwd_row, mp_f[:, sl], mp_b[:, sl])
        n_t = jnp.maximum(m_prev, p_ref[P_M, sl])
        f_inter = jnp.exp(m_prev - n_t)
        e_min = jnp.exp(-(p_ref[P_B, sl] + n_t))
        ccol = ccol_ref[sl, :]
        for h in range(HEADS):
            hs = slice(h * DV, (h + 1) * DV)
            qt = qt_ref[h * DK:(h + 1) * DK, sl]
            k_pair = k_ref[sl, (h // 2) * 2 * DK:(h // 2 + 1) * 2 * DK]
            qt_pair = jnp.concatenate([qt, zq] if h % 2 == 0 else [zq, qt], axis=0)
            st = jnp.dot(k_pair, qt_pair, preferred_element_type=F32)
            pts, qfs = [], []
            for d in range(2):
                r = HEADS * d + h
                arg = jnp.where(visible[d], ccol[:, r:r + 1] - n_t[r:r + 1, :], NEG_INF)
                pts.append((jnp.exp(arg) * st).astype(BF16))
                qfs.append((qt.astype(F32) * f_inter[r:r + 1, :]).astype(BF16))
            inter = [jnp.concatenate([qfs[0], zq], axis=1), jnp.concatenate([zq, qfs[1]], axis=1)]
            first = 0 if _state_half(0, h) == 0 else 1
            rhs = jnp.concatenate([jnp.concatenate(pts, axis=1), inter[first], inter[1 - first]], axis=0)
            vt_aug = jnp.concatenate([vt_ref[hs, sl], ones_rows], axis=0)
            ps = slice((h // 2) * DVA, (h // 2 + 1) * DVA)
            blocks = (ct_f[chunk, ps, :], ct_b[chunk, ps, :])
            ct = jnp.where(lane_half == 0, blocks[first], blocks[1 - first])
            both = jnp.dot(jnp.concatenate([vt_aug, ct], axis=1), rhs, preferred_element_type=F32)
            ht = None
            for d in range(2):
                r = HEADS * d + h
                numer = both[0:DV, d * L:(d + 1) * L]
                denom = both[DV:DV + 1, d * L:(d + 1) * L]
                part = numer / jnp.maximum(jnp.abs(denom), e_min[r:r + 1, :])
                ht = part if ht is None else ht + part
            ms = jnp.mean(ht * ht, axis=0, keepdims=True)
            hn = ht * lax.rsqrt(ms + EPS) * gain_ref[hs, :]
            y_ref[sl, hs] = (jax.nn.sigmoid(o_ref[sl, hs]) * hn.T).astype(BF16)


def _mlstm(qt, k, ksw, vt, p, ccol, o, gain_b, batch, seq, scan_rows, rows):
    T = batch * seq
    ng = seq // scan_rows
    HK, HV = HEADS * DK, HEADS * DV

    def scan_specs(group_of):
        blk = lambda b, j: b * ng + group_of(j)
        ins = [pl.BlockSpec((HV, scan_rows), lambda b, j: (0, blk(b, j))),
               pl.BlockSpec((scan_rows, HK), lambda b, j: (blk(b, j), 0)),
               pl.BlockSpec((5 * NGR, scan_rows), lambda b, j: (0, blk(b, j)))]
        outs = [pl.BlockSpec((scan_rows // CHUNK, CT_ROWS, 128), lambda b, j: (blk(b, j), 0, 0)),
                pl.BlockSpec((NGR, scan_rows), lambda b, j: (0, blk(b, j)))]
        return ins, outs

    ins_f, outs_f = scan_specs(lambda j: j)
    ins_b, outs_b = scan_specs(lambda j: ng - 1 - j)
    state_shapes = [jax.ShapeDtypeStruct((T // CHUNK, CT_ROWS, 128), BF16),
                    jax.ShapeDtypeStruct((NGR, T), F32)]
    ct_f, mp_f, ct_b, mp_b = pl.pallas_call(
        _mlstm_scan_kernel,
        grid=(batch, ng),
        in_specs=ins_f + ins_b,
        out_specs=outs_f + outs_b,
        out_shape=state_shapes + state_shapes,
        scratch_shapes=[pltpu.VMEM((2, HEADS, DVA, 128), F32), pltpu.VMEM((2, NGR, 128), F32)],
        compiler_params=pltpu.CompilerParams(dimension_semantics=("arbitrary", "arbitrary"),
                                             vmem_limit_bytes=VMEM_LIMIT),
        name="mlstm_scan",
    )(vt, k, p, vt, ksw, p)

    row = lambda w: pl.BlockSpec((rows, w), lambda i: (i, 0))
    col = lambda h: pl.BlockSpec((h, rows), lambda i: (0, i))
    ctb = pl.BlockSpec((rows // CHUNK, CT_ROWS, 128), lambda i: (i, 0, 0))
    return pl.pallas_call(
        _mlstm_out_kernel,
        grid=(T // rows,),
        in_specs=[col(HK), row(HK), col(HV), col(5 * NGR), col(NGR), col(NGR), row(128), ctb, ctb,
                  row(HV), _resident((HV, 128))],
        out_specs=row(HV),
        out_shape=jax.ShapeDtypeStruct((T, HV), BF16),
        compiler_params=pltpu.CompilerParams(dimension_semantics=("arbitrary",),
                                             vmem_limit_bytes=VMEM_LIMIT),
        name="mlstm_out",
    )(qt, k, vt, p, mp_f, mp_b, ccol, ct_f, ct_b, o, gain_b)


def kernel(x, norm_ffn1_pre, norm_ffn1_post, w_ffn1_in, w_ffn1_out, norm_mix_pre, norm_mix_post,
           w_mix_in, conv_w, conv_b, gate_i_bias, gate_f_bias, mlstm_norm, w_mix_out,
           norm_ffn2_pre, norm_ffn2_post, w_ffn2_in, w_ffn2_out):
    batch, seq, _ = x.shape
    T = batch * seq
    depth = norm_ffn1_pre.shape[0]
    tm = 512
    xt = x.reshape(T, D_MODEL)
    HV = HEADS * DV
    for l in range(depth):
        xt = _ffn(xt, norm_ffn1_pre[l][None], norm_ffn1_post[l][None], w_ffn1_in, w_ffn1_out, l, tm)

        gbias = jnp.pad(jnp.concatenate([gate_i_bias[l], gate_f_bias[l]]), (0, 128 - 2 * NGR))[None]
        bg, u, qt, k, ksw, vt, o, p, ccol = _mix_in(
            xt, norm_mix_pre[l][None], jnp.swapaxes(w_mix_in, 1, 2), gbias, l, 2 * tm)
        gain_b = jnp.broadcast_to(mlstm_norm[l][:, None], (HV, 128))
        y_mlstm = _mlstm(qt, k, ksw, vt, p, ccol, o, gain_b, batch, seq, SCAN_ROWS, MLSTM_ROWS)
        xt = _mix_ffn(xt, bg, u, conv_w[l], conv_b[l][None], y_mlstm, norm_mix_post[l][None], w_mix_out,
                      norm_ffn2_pre[l][None], norm_ffn2_post[l][None], w_ffn2_in, w_ffn2_out, l, seq, tm)
    return xt.reshape(batch, seq, D_MODEL)
```

```python
import functools

import jax
import jax.numpy as jnp
from jax import lax
from jax.experimental import pallas as pl
from jax.experimental.pallas import tpu as pltpu

D_MODEL = 1024
D_FF = 2816
CONV_WIDTH = 512
HEADS = 4
DK = 64
DV = 128
CHUNK = 128
EPS = 1e-6
NEG_INF = -1e30

FF_TILE = 256
FF_STAGE_CHUNKS = 16
FF_SIDE_CHUNKS = 8
MIX_STAGE_CHUNKS = 8
MLSTM_ROWS = 2048
SCAN_ROWS = 2048
DVA = DV + 16
HALO_ROWS = 8
NGR = 2 * HEADS
CT_ROWS = (HEADS // 2) * DVA
VMEM_LIMIT = 56 * 1024 * 1024

P_E, P_M, P_B, P_G, P_MC = (slice(i * NGR, (i + 1) * NGR) for i in range(5))

F32 = jnp.float32
BF16 = jnp.bfloat16


def _rms(x, g):
    return x * lax.rsqrt(jnp.mean(x * x, axis=-1, keepdims=True) + EPS) * g


def _log_sigmoid(z):
    return jnp.minimum(z, 0.0) - jnp.log1p(jnp.exp(-jnp.abs(z)))


def _resident(shape):
    zeros = (0,) * len(shape)
    return pl.BlockSpec(shape, lambda *_: zeros, pipeline_mode=pl.Buffered(1))


def _split3(x):
    hi = x.astype(BF16)
    r1 = x - hi.astype(F32)
    mid = r1.astype(BF16)
    lo = (r1 - mid.astype(F32)).astype(BF16)
    return hi, mid, lo


def _visible(rev):
    s = lax.broadcasted_iota(jnp.int32, (CHUNK, CHUNK), 0)
    t = lax.broadcasted_iota(jnp.int32, (CHUNK, CHUNK), 1)
    return (s >= t) if rev else (s <= t)


def _running_max(x, rev):
    n = x.shape[1]
    pos = lax.broadcasted_iota(jnp.int32, x.shape, 1) & (CHUNK - 1)
    k = 1
    while k < CHUNK:
        if rev:
            shifted, ok = pltpu.roll(x, n - k, 1), pos < CHUNK - k
        else:
            shifted, ok = pltpu.roll(x, k, 1), pos >= k
        x = jnp.maximum(x, jnp.where(ok, shifted, NEG_INF))
        k *= 2
    return x


def _stage_bf16(src_hbm, dst_ref, stage_ref, sem, rows=None):
    rows = stage_ref.shape[1] if rows is None else rows
    n_chunks = src_hbm.shape[0] // rows

    def copy(c):
        return pltpu.make_async_copy(src_hbm.at[pl.ds(c * rows, rows), :],
                                     stage_ref.at[c % 2, pl.ds(0, rows), :], sem.at[c % 2])

    copy(0).start()
    for c in range(n_chunks):
        if c + 1 < n_chunks:
            copy(c + 1).start()
        copy(c).wait()
        dst_ref[c * rows:(c + 1) * rows, :] = stage_ref[c % 2, 0:rows, :].astype(BF16)


def _mix_out_pre(tile, tiles_per_seq, bg_ref, u_ref, uprev_ref, unext_ref, cw_ref, cb_ref, y_ref, wmo_ref):
    tm = u_ref.shape[0]
    u = u_ref[...]
    has_prev = (tile % tiles_per_seq != 0).astype(F32)
    has_next = (tile % tiles_per_seq != tiles_per_seq - 1).astype(F32)
    prev_row = uprev_ref[HALO_ROWS - 1:HALO_ROWS, :] * has_prev
    next_row = unext_ref[0:1, :] * has_next
    ri = lax.broadcasted_iota(jnp.int32, u.shape, 0)
    u_m1 = jnp.where(ri == 0, prev_row, pltpu.roll(u, 1, 0))
    u_p1 = jnp.where(ri == tm - 1, next_row, pltpu.roll(u, tm - 1, 0))
    conv = cw_ref[0:1, :] * u_m1 + cw_ref[1:2, :] * u + cw_ref[2:3, :] * u_p1
    y_conv = (bg_ref[...] * (conv + cb_ref[...])).astype(BF16)
    return jnp.dot(y_conv, wmo_ref[0:CONV_WIDTH, :], preferred_element_type=F32) \
        + jnp.dot(y_ref[...], wmo_ref[CONV_WIDTH:, :], preferred_element_type=F32)


def _zero_after(v):
    u = pltpu.bitcast(v, jnp.uint32)
    acc = None
    for r in range(u.shape[0] // 8):
        for c in range(u.shape[1] // 128):
            t = u[r * 8:(r + 1) * 8, c * 128:(c + 1) * 128]
            acc = t if acc is None else acc | t
    return pltpu.bitcast((acc >> 16) >> 16, F32)


def _ffn_pipeline(n_tiles, tm, gpost_ref, win_ref, wout_ref, h_ref, xn_ref, y_ref, o_ref,
                  stage_weights, front_at, front_begin, front_rows, resid_rows):
    i = pl.program_id(0)
    g_half = 0.5 * gpost_ref[...]
    all_rows = slice(0, tm)

    @pl.when(i == 0)
    def _():
        stage_weights()
        xn_ref[0] = front_rows(front_begin(), all_rows).astype(BF16)
        y_ref[...] = jnp.zeros(y_ref.shape, F32)

    @pl.when((i >= 1) & (i <= n_tiles))
    def _():
        xn = xn_ref[(i + 1) % 2]
        rp = tm // FF_SIDE_CHUNKS
        ctx = None
        for j in range(D_FF // FF_TILE):
            lo = j * FF_TILE
            if j == front_at:
                ctx = front_begin()
            gate = jnp.dot(xn, win_ref[:, lo:lo + FF_TILE], preferred_element_type=F32)
            up = jnp.dot(xn, win_ref[:, D_FF + lo:D_FF + lo + FF_TILE], preferred_element_type=F32)
            hm = gate * jax.nn.sigmoid(gate) * up
            h_ref[:, lo:lo + FF_TILE] = hm.astype(BF16)
            zero = None
            if j < FF_SIDE_CHUNKS:
                rs = slice(j * rp, (j + 1) * rp)
                out_rows = resid_rows(2, rs) + _rms(y_ref[rs, :], g_half)
                o_ref[rs, :] = out_rows
                zero = _zero_after(out_rows)
            if front_at <= j < front_at + FF_SIDE_CHUNKS:
                rs = slice((j - front_at) * rp, (j - front_at + 1) * rp)
                xn_rows = front_rows(ctx, rs)
                xn_ref[i % 2, rs, :] = xn_rows.astype(BF16)
                zero = _zero_after(xn_rows) if zero is None else zero + _zero_after(xn_rows)
            if zero is not None:
                h_ref[0:8, lo:lo + 128] = (hm[0:8, 0:128] + zero).astype(BF16)
        y_ref[...] = jnp.dot(h_ref[...], wout_ref[...], preferred_element_type=F32)

    @pl.when(i == n_tiles + 1)
    def _():
        o_ref[...] = resid_rows(2, all_rows) + _rms(y_ref[...], g_half)


def _ffn_kernel(layer, n_tiles, xa_ref, xc_ref, gpre_ref, gpost_ref, win_hbm, wout_hbm, o_ref,
                h_ref, win_ref, wout_ref, stage_in, stage_out, sem, xn_ref, y_ref):
    def stage_weights():
        _stage_bf16(win_hbm.at[layer], win_ref, stage_in, sem)
        _stage_bf16(wout_hbm.at[layer], wout_ref, stage_out, sem)

    _ffn_pipeline(n_tiles, xa_ref.shape[0], gpost_ref, win_ref, wout_ref, h_ref, xn_ref, y_ref, o_ref,
                  stage_weights, front_at=0, front_begin=lambda: None,
                  front_rows=lambda ctx, rs: _rms(xa_ref[rs, :], gpre_ref[...]),
                  resid_rows=lambda lag, rs: xc_ref[rs, :])


def _mix_ffn_kernel(layer, n_tiles, tiles_per_seq, x_ref, bg_ref, u_ref, uprev_ref, unext_ref, cw_ref, cb_ref,
                    y_ref, gmix_ref, wmo_hbm, gpre_ref, gpost_ref, win_hbm, wout_hbm, o_ref,
                    h_ref, win_ref, wout_ref, wmo_ref, stage_in, stage_out, sem, xn_ref, yf_ref, xres_ref):
    i = pl.program_id(0)
    tile = jnp.minimum(i, n_tiles - 1)

    def stage_weights():
        _stage_bf16(wmo_hbm.at[layer], wmo_ref, stage_out, sem, rows=D_MODEL // 8)
        _stage_bf16(win_hbm.at[layer], win_ref, stage_in, sem)
        _stage_bf16(wout_hbm.at[layer], wout_ref, stage_out, sem)
        xres_ref[...] = jnp.zeros(xres_ref.shape, F32)

    def front_begin():
        return _mix_out_pre(tile, tiles_per_seq, bg_ref, u_ref, uprev_ref, unext_ref, cw_ref, cb_ref,
                            y_ref, wmo_ref)

    def front_rows(h, rs):
        x2 = x_ref[rs, :] + _rms(h[rs, :], gmix_ref[...])
        xres_ref[i % 3, rs, :] = x2
        return _rms(x2, gpre_ref[...])

    _ffn_pipeline(n_tiles, x_ref.shape[0], gpost_ref, win_ref, wout_ref, h_ref, xn_ref, yf_ref, o_ref,
                  stage_weights, front_at=3, front_begin=front_begin, front_rows=front_rows,
                  resid_rows=lambda lag, rs: xres_ref[(i + 3 - lag) % 3, rs, :])


def _ffn_scratch(tm):
    return [
        pltpu.VMEM((tm, D_FF), BF16),
        pltpu.VMEM((D_MODEL, 2 * D_FF), BF16),
        pltpu.VMEM((D_FF, D_MODEL), BF16),
    ], [
        pltpu.VMEM((2, D_MODEL // FF_STAGE_CHUNKS, 2 * D_FF), F32),
        pltpu.VMEM((2, D_FF // FF_STAGE_CHUNKS, D_MODEL), F32),
        pltpu.SemaphoreType.DMA((2,)),
    ]


def _ffn(x, gpre, gpost, w_in, w_out, layer, tm):
    T = x.shape[0]
    nt = T // tm
    resident, staging = _ffn_scratch(tm)
    tile = lambda lag: pl.BlockSpec((tm, D_MODEL), lambda i: (jnp.clip(i - lag, 0, nt - 1), 0))
    return pl.pallas_call(
        functools.partial(_ffn_kernel, layer, nt),
        grid=(nt + 2,),
        in_specs=[
            tile(0), tile(2),
            _resident((1, D_MODEL)),
            _resident((1, D_MODEL)),
            pl.BlockSpec(memory_space=pl.ANY),
            pl.BlockSpec(memory_space=pl.ANY),
        ],
        out_specs=tile(2),
        out_shape=jax.ShapeDtypeStruct((T, D_MODEL), F32),
        scratch_shapes=resident + staging + [pltpu.VMEM((2, tm, D_MODEL), BF16),
                                             pltpu.VMEM((tm, D_MODEL), F32)],
        compiler_params=pltpu.CompilerParams(
            dimension_semantics=("arbitrary",), vmem_limit_bytes=VMEM_LIMIT),
        name="ffn",
    )(x, x, gpre, gpost, w_in, w_out)


def _mix_ffn(x, bg, u, conv_w, conv_b, y_mlstm, gmix, w_mix_out, gpre, gpost, w_in, w_out, layer, seq, tm):
    T = x.shape[0]
    nt = T // tm
    tiles_per_seq = seq // tm
    sub = tm // HALO_ROWS
    last = T // HALO_ROWS - 1
    front = lambda i: jnp.minimum(i, nt - 1)
    row = lambda w: pl.BlockSpec((tm, w), lambda i: (front(i), 0))
    hbm = pl.BlockSpec(memory_space=pl.ANY)
    resident, staging = _ffn_scratch(tm)
    return pl.pallas_call(
        functools.partial(_mix_ffn_kernel, layer, nt, tiles_per_seq),
        grid=(nt + 2,),
        in_specs=[
            row(D_MODEL), row(CONV_WIDTH), row(CONV_WIDTH),
            pl.BlockSpec((HALO_ROWS, CONV_WIDTH), lambda i: (jnp.maximum(front(i) * sub - 1, 0), 0)),
            pl.BlockSpec((HALO_ROWS, CONV_WIDTH), lambda i: (jnp.minimum((front(i) + 1) * sub, last), 0)),
            _resident((3, CONV_WIDTH)), _resident((1, CONV_WIDTH)),
            row(HEADS * DV), _resident((1, D_MODEL)), hbm,
            _resident((1, D_MODEL)), _resident((1, D_MODEL)), hbm, hbm,
        ],
        out_specs=pl.BlockSpec((tm, D_MODEL), lambda i: (jnp.clip(i - 2, 0, nt - 1), 0)),
        out_shape=jax.ShapeDtypeStruct((T, D_MODEL), F32),
        scratch_shapes=resident + [pltpu.VMEM((D_MODEL, D_MODEL), BF16)] + staging + [
            pltpu.VMEM((2, tm, D_MODEL), BF16),
            pltpu.VMEM((tm, D_MODEL), F32),
            pltpu.VMEM((3, tm, D_MODEL), F32)],
        compiler_params=pltpu.CompilerParams(
            dimension_semantics=("arbitrary",), vmem_limit_bytes=VMEM_LIMIT),
        name="mix_ffn",
    )(x, bg, u, u, u, conv_w, conv_b, y_mlstm, gmix, w_mix_out, gpre, gpost, w_in, w_out)


def _gate_rows(zr, p_ref, ccol_ref):
    L = CHUNK
    n_chunks = zr.shape[1] // L
    fwd_row = lax.broadcasted_iota(jnp.int32, (NGR, L), 0) < HEADS
    fwd_col = lax.broadcasted_iota(jnp.int32, (NGR, 1), 0) < HEADS
    li = zr[0:NGR]
    gates = jnp.concatenate([li, _log_sigmoid(zr[NGR:2 * NGR])], axis=0)
    x3 = jnp.concatenate(_split3(gates), axis=0)
    stacked = jnp.concatenate([x3[:, c * L:(c + 1) * L] for c in range(n_chunks)], axis=0)
    cum_ops = jnp.concatenate([_visible(False), _visible(True)], axis=1).astype(BF16)
    cum = jnp.dot(stacked, cum_ops, preferred_element_type=F32)
    b_chunks = []
    for c in range(n_chunks):
        blk = cum[c * 48:(c + 1) * 48]
        s16 = blk[0:16] + blk[16:32] + blk[32:48]
        b_chunks.append(jnp.where(fwd_row, s16[NGR:2 * NGR, 0:L], s16[NGR:2 * NGR, L:2 * L]))
    b = jnp.concatenate(b_chunks, axis=1)
    cc = li - b
    fwd_all = lax.broadcasted_iota(jnp.int32, cc.shape, 0) < HEADS
    p_ref[P_M, :] = jnp.where(fwd_all, _running_max(cc, False), _running_max(cc, True))
    p_ref[P_B, :] = b
    pad = jnp.zeros((L - NGR, L), F32)
    for c in range(n_chunks):
        sl = slice(c * L, (c + 1) * L)
        b_c, cc_c = b_chunks[c], cc[:, sl]
        g = jnp.where(fwd_col, b_c[:, L - 1:L], b_c[:, 0:1])
        m_chunk = g + jnp.max(cc_c, axis=1, keepdims=True)
        p_ref[P_E, sl] = jnp.exp(g + cc_c - m_chunk)
        p_ref[P_G, sl] = jnp.broadcast_to(g, (NGR, L))
        p_ref[P_MC, sl] = jnp.broadcast_to(m_chunk, (NGR, L))
        ccol_ref[sl, :] = jnp.concatenate([cc_c, pad], axis=0).T


def _mix_in_kernel(layer, x_ref, gpre_ref, w_hbm, gbias_ref,
                   bg_ref, u_ref, qt_ref, k_ref, ksw_ref, vt_ref, o_ref, p_ref, ccol_ref,
                   wb_ref, stage_ref, sem):
    W, HK, HV = CONV_WIDTH, HEADS * DK, HEADS * DV
    q0 = 3 * W
    k0, v0 = q0 + HK, q0 + 2 * HK
    o0 = v0 + HV
    g0 = o0 + HV

    @pl.when(pl.program_id(0) == 0)
    def _():
        src = w_hbm.at[layer]
        rows = stage_ref.shape[1]
        n_chunks = g0 // rows

        def copy(c):
            n = rows if c < n_chunks else 2 * NGR
            return pltpu.make_async_copy(src.at[pl.ds(c * rows, n), :], stage_ref.at[c % 2, pl.ds(0, n), :],
                                         sem.at[c % 2])

        copy(0).start()
        for c in range(n_chunks + 1):
            if c < n_chunks:
                copy(c + 1).start()
            copy(c).wait()
            if c < n_chunks:
                wb_ref[:, c * rows:(c + 1) * rows] = stage_ref[c % 2].T.astype(BF16)
            else:
                gt = stage_ref[c % 2, 0:128, :].T
                lane = lax.broadcasted_iota(jnp.int32, gt.shape, 1)
                wb_ref[:, g0:g0 + 128] = jnp.where(lane < 2 * NGR, gt, 0.0).astype(BF16)

    xn = _rms(x_ref[...], gpre_ref[...]).astype(BF16)
    proj = lambda a, b: jnp.dot(xn, wb_ref[:, a:b], preferred_element_type=F32)

    zg = proj(g0, g0 + 128) + gbias_ref[...]
    bg_ref[...] = proj(0, W)
    u_ref[...] = proj(W, 2 * W) * proj(2 * W, 3 * W)
    _gate_rows(zg.T[0:2 * NGR], p_ref, ccol_ref)
    qt_ref[...] = (proj(q0, k0) * (DK ** -0.5)).T.astype(BF16)
    kk = proj(k0, v0)
    k_ref[...] = kk.astype(BF16)
    for pair in range(HEADS // 2):
        ps = slice(pair * 2 * DK, (pair + 1) * 2 * DK)
        ksw_ref[:, ps] = pltpu.roll(kk[:, ps], DK, 1).astype(BF16)
    vt_ref[...] = proj(v0, o0).T.astype(BF16)
    o_ref[...] = proj(o0, g0)


def _mix_in(x, gpre, w, gbias, layer, tm):
    T = x.shape[0]
    row = lambda w: pl.BlockSpec((tm, w), lambda i: (i, 0))
    col = lambda h: pl.BlockSpec((h, tm), lambda i: (0, i))
    HK, HV = HEADS * DK, HEADS * DV
    g0 = 3 * CONV_WIDTH + 2 * HK + 2 * HV
    return pl.pallas_call(
        functools.partial(_mix_in_kernel, layer),
        grid=(T // tm,),
        in_specs=[row(D_MODEL), _resident((1, D_MODEL)), pl.BlockSpec(memory_space=pl.ANY),
                  _resident((1, 128))],
        scratch_shapes=[pltpu.VMEM((D_MODEL, g0 + 128), BF16),
                        pltpu.VMEM((2, g0 // MIX_STAGE_CHUNKS, D_MODEL), F32),
                        pltpu.SemaphoreType.DMA((2,))],
        out_specs=[row(CONV_WIDTH), row(CONV_WIDTH), col(HK), row(HK), row(HK), col(HV), row(HV),
                   col(5 * NGR), row(128)],
        out_shape=[
            jax.ShapeDtypeStruct((T, CONV_WIDTH), F32),
            jax.ShapeDtypeStruct((T, CONV_WIDTH), F32),
            jax.ShapeDtypeStruct((HK, T), BF16),
            jax.ShapeDtypeStruct((T, HK), BF16),
            jax.ShapeDtypeStruct((T, HK), BF16),
            jax.ShapeDtypeStruct((HV, T), BF16),
            jax.ShapeDtypeStruct((T, HV), F32),
            jax.ShapeDtypeStruct((5 * NGR, T), F32),
            jax.ShapeDtypeStruct((T, 128), F32),
        ],
        compiler_params=pltpu.CompilerParams(
            dimension_semantics=("arbitrary",), vmem_limit_bytes=VMEM_LIMIT),
        name="mix_in",
    )(x, gpre, w, gbias)


def _state_half(d, h):
    return (h % 2) ^ d


def _scan_direction(d, vt_ref, k_ref, p_ref, ct_ref, mp_ref, ct_state, m_state):
    L = CHUNK
    n_chunks = k_ref.shape[0] // L
    lane_half = lax.broadcasted_iota(jnp.int32, (L, 128), 1) // DK
    lane_half_s = lax.broadcasted_iota(jnp.int32, (DVA, 128), 1) // DK
    ones_rows = jnp.ones((DVA - DV, L), BF16)

    cts = [ct_state[d, h] for h in range(HEADS)]
    m_prev = m_state[d]
    for chunk in (range(n_chunks - 1, -1, -1) if d == 1 else range(n_chunks)):
        sl = slice(chunk * L, (chunk + 1) * L)
        e, g, m_chunk = p_ref[P_E, sl], p_ref[P_G, sl], p_ref[P_MC, sl]
        mp_ref[:, sl] = m_prev
        m_new = jnp.maximum(g + m_prev, m_chunk)
        a_old = jnp.exp(g + m_prev - m_new)
        a_new = jnp.exp(m_chunk - m_new)
        new_cts = []
        for h in range(HEADS):
            r = HEADS * d + h
            vt_aug = jnp.concatenate([vt_ref[h * DV:(h + 1) * DV, sl], ones_rows], axis=0)
            vte = (vt_aug.astype(F32) * e[r:r + 1, :]).astype(BF16)
            pair = slice((h // 2) * 2 * DK, (h // 2 + 1) * 2 * DK)
            k_half = jnp.where(lane_half == _state_half(d, h), k_ref[sl, pair], jnp.zeros((L, 128), BF16))
            ct_chunk = jnp.dot(vte, k_half, preferred_element_type=F32)
            new_cts.append(a_old[r:r + 1, 0:1] * cts[h] + a_new[r:r + 1, 0:1] * ct_chunk)
        for pr in range(HEADS // 2):
            both = jnp.where(lane_half_s == _state_half(d, 2 * pr), cts[2 * pr], cts[2 * pr + 1])
            ct_ref[chunk, pr * DVA:(pr + 1) * DVA, :] = both.astype(BF16)
        cts = new_cts
        m_prev = m_new
    for h in range(HEADS):
        ct_state[d, h] = cts[h]
    m_state[d] = m_prev


def _mlstm_scan_kernel(vt_f, k_f, p_f, vt_b, ksw_b, p_b, ct_f, mp_f, ct_b, mp_b, ct_state, m_state):
    @pl.when(pl.program_id(1) == 0)
    def _():
        ct_state[...] = jnp.zeros(ct_state.shape, F32)
        m_state[...] = jnp.full(m_state.shape, NEG_INF, F32)

    _scan_direction(0, vt_f, k_f, p_f, ct_f, mp_f, ct_state, m_state)
    _scan_direction(1, vt_b, ksw_b, p_b, ct_b, mp_b, ct_state, m_state)


def _mlstm_out_kernel(qt_ref, k_ref, vt_ref, p_ref, mp_f, mp_b, ccol_ref, ct_f, ct_b,
                      o_ref, gain_ref, y_ref):
    L = CHUNK
    n_chunks = k_ref.shape[0] // L
    visible = (_visible(False), _visible(True))
    fwd_row = lax.broadcasted_iota(jnp.int32, (NGR, L), 0) < HEADS
    lane_half = lax.broadcasted_iota(jnp.int32, (DVA, 128), 1) // DK
    ones_rows = jnp.ones((DVA - DV, L), BF16)
    zq = jnp.zeros((DK, L), BF16)
    for chunk in range(n_chunks):
        sl = slice(chunk * L, (chunk + 1) * L)
        m_prev = jnp.where(fwd_row, mp_f[:, sl], mp_b[:, sl])
        n_t = jnp.maximum(m_prev, p_ref[P_M, sl])
        f_inter = jnp.exp(m_prev - n_t)
        e_min = jnp.exp(-(p_ref[P_B, sl] + n_t))
        ccol = ccol_ref[sl, :]
        for h in range(HEADS):
            hs = slice(h * DV, (h + 1) * DV)
            qt = qt_ref[h * DK:(h + 1) * DK, sl]
            k_pair = k_ref[sl, (h // 2) * 2 * DK:(h // 2 + 1) * 2 * DK]
            qt_pair = jnp.concatenate([qt, zq] if h % 2 == 0 else [zq, qt], axis=0)
            st = jnp.dot(k_pair, qt_pair, preferred_element_type=F32)
            pts, qfs = [], []
            for d in range(2):
                r = HEADS * d + h
                arg = jnp.where(visible[d], ccol[:, r:r + 1] - n_t[r:r + 1, :], NEG_INF)
                pts.append((jnp.exp(arg) * st).astype(BF16))
                qfs.append((qt.astype(F32) * f_inter[r:r + 1, :]).astype(BF16))
            inter = [jnp.concatenate([qfs[0], zq], axis=1), jnp.concatenate([zq, qfs[1]], axis=1)]
            first = 0 if _state_half(0, h) == 0 else 1
            rhs = jnp.concatenate([jnp.concatenate(pts, axis=1), inter[first], inter[1 - first]], axis=0)
            vt_aug = jnp.concatenate([vt_ref[hs, sl], ones_rows], axis=0)
            ps = slice((h // 2) * DVA, (h // 2 + 1) * DVA)
            blocks = (ct_f[chunk, ps, :], ct_b[chunk, ps, :])
            ct = jnp.where(lane_half == 0, blocks[first], blocks[1 - first])
            both = jnp.dot(jnp.concatenate([vt_aug, ct], axis=1), rhs, preferred_element_type=F32)
            ht = None
            for d in range(2):
                r = HEADS * d + h
                numer = both[0:DV, d * L:(d + 1) * L]
                denom = both[DV:DV + 1, d * L:(d + 1) * L]
                part = numer / jnp.maximum(jnp.abs(denom), e_min[r:r + 1, :])
                ht = part if ht is None else ht + part
            ms = jnp.mean(ht * ht, axis=0, keepdims=True)
            hn = ht * lax.rsqrt(ms + EPS) * gain_ref[hs, :]
            y_ref[sl, hs] = (jax.nn.sigmoid(o_ref[sl, hs]) * hn.T).astype(BF16)


def _mlstm(qt, k, ksw, vt, p, ccol, o, gain_b, batch, seq, scan_rows, rows):
    T = batch * seq
    ng = seq // scan_rows
    HK, HV = HEADS * DK, HEADS * DV

    def scan_specs(group_of):
        blk = lambda b, j: b * ng + group_of(j)
        ins = [pl.BlockSpec((HV, scan_rows), lambda b, j: (0, blk(b, j))),
               pl.BlockSpec((scan_rows, HK), lambda b, j: (blk(b, j), 0)),
               pl.BlockSpec((5 * NGR, scan_rows), lambda b, j: (0, blk(b, j)))]
        outs = [pl.BlockSpec((scan_rows // CHUNK, CT_ROWS, 128), lambda b, j: (blk(b, j), 0, 0)),
                pl.BlockSpec((NGR, scan_rows), lambda b, j: (0, blk(b, j)))]
        return ins, outs

    ins_f, outs_f = scan_specs(lambda j: j)
    ins_b, outs_b = scan_specs(lambda j: ng - 1 - j)
    state_shapes = [jax.ShapeDtypeStruct((T // CHUNK, CT_ROWS, 128), BF16),
                    jax.ShapeDtypeStruct((NGR, T), F32)]
    ct_f, mp_f, ct_b, mp_b = pl.pallas_call(
        _mlstm_scan_kernel,
        grid=(batch, ng),
        in_specs=ins_f + ins_b,
        out_specs=outs_f + outs_b,
        out_shape=state_shapes + state_shapes,
        scratch_shapes=[pltpu.VMEM((2, HEADS, DVA, 128), F32), pltpu.VMEM((2, NGR, 128), F32)],
        compiler_params=pltpu.CompilerParams(dimension_semantics=("arbitrary", "arbitrary"),
                                             vmem_limit_bytes=VMEM_LIMIT),
        name="mlstm_scan",
    )(vt, k, p, vt, ksw, p)

    row = lambda w: pl.BlockSpec((rows, w), lambda i: (i, 0))
    col = lambda h: pl.BlockSpec((h, rows), lambda i: (0, i))
    ctb = pl.BlockSpec((rows // CHUNK, CT_ROWS, 128), lambda i: (i, 0, 0))
    return pl.pallas_call(
        _mlstm_out_kernel,
        grid=(T // rows,),
        in_specs=[col(HK), row(HK), col(HV), col(5 * NGR), col(NGR), col(NGR), row(128), ctb, ctb,
                  row(HV), _resident((HV, 128))],
        out_specs=row(HV),
        out_shape=jax.ShapeDtypeStruct((T, HV), BF16),
        compiler_params=pltpu.CompilerParams(dimension_semantics=("arbitrary",),
                                             vmem_limit_bytes=VMEM_LIMIT),
        name="mlstm_out",
    )(qt, k, vt, p, mp_f, mp_b, ccol, ct_f, ct_b, o, gain_b)


def kernel(x, norm_ffn1_pre, norm_ffn1_post, w_ffn1_in, w_ffn1_out, norm_mix_pre, norm_mix_post,
           w_mix_in, conv_w, conv_b, gate_i_bias, gate_f_bias, mlstm_norm, w_mix_out,
           norm_ffn2_pre, norm_ffn2_post, w_ffn2_in, w_ffn2_out):
    batch, seq, _ = x.shape
    T = batch * seq
    depth = norm_ffn1_pre.shape[0]
    tm = 512
    xt = x.reshape(T, D_MODEL)
    HV = HEADS * DV
    for l in range(depth):
        xt = _ffn(xt, norm_ffn1_pre[l][None], norm_ffn1_post[l][None], w_ffn1_in, w_ffn1_out, l, tm)

        gbias = jnp.pad(jnp.concatenate([gate_i_bias[l], gate_f_bias[l]]), (0, 128 - 2 * NGR))[None]
        bg, u, qt, k, ksw, vt, o, p, ccol = _mix_in(
            xt, norm_mix_pre[l][None], jnp.swapaxes(w_mix_in, 1, 2), gbias, l, 2 * tm)
        gain_b = jnp.broadcast_to(mlstm_norm[l][:, None], (HV, 128))
        y_mlstm = _mlstm(qt, k, ksw, vt, p, ccol, o, gain_b, batch, seq, SCAN_ROWS, MLSTM_ROWS)
        xt = _mix_ffn(xt, bg, u, conv_w[l], conv_b[l][None], y_mlstm, norm_mix_post[l][None], w_mix_out,
                      norm_ffn2_pre[l][None], norm_ffn2_post[l][None], w_ffn2_in, w_ffn2_out, l, seq, tm)
    return xt.reshape(batch, seq, D_MODEL)
```

```python
import functools

import jax
import jax.numpy as jnp
from jax import lax
from jax.experimental import pallas as pl
from jax.experimental.pallas import tpu as pltpu

D_MODEL = 1024
D_FF = 2816
CONV_WIDTH = 512
HEADS = 4
DK = 64
DV = 128
CHUNK = 128
EPS = 1e-6
NEG_INF = -1e30

FF_TILE = 256
FF_SIDE_CHUNKS = 8
MIX_STAGE_CHUNKS = 8
MLSTM_ROWS = 2048
SCAN_ROWS = 2048
DVA = DV + 16
HALO_ROWS = 8
NGR = 2 * HEADS
CT_ROWS = (HEADS // 2) * DVA
VMEM_LIMIT = 56 * 1024 * 1024

P_E, P_M, P_B, P_G, P_MC = (slice(i * NGR, (i + 1) * NGR) for i in range(5))

F32 = jnp.float32
BF16 = jnp.bfloat16


def _rms(x, g):
    return x * lax.rsqrt(jnp.mean(x * x, axis=-1, keepdims=True) + EPS) * g


def _log_sigmoid(z):
    return jnp.minimum(z, 0.0) - jnp.log1p(jnp.exp(-jnp.abs(z)))


def _resident(shape):
    zeros = (0,) * len(shape)
    return pl.BlockSpec(shape, lambda *_: zeros, pipeline_mode=pl.Buffered(1))


def _split3(x):
    hi = x.astype(BF16)
    r1 = x - hi.astype(F32)
    mid = r1.astype(BF16)
    lo = (r1 - mid.astype(F32)).astype(BF16)
    return hi, mid, lo


def _visible(rev):
    s = lax.broadcasted_iota(jnp.int32, (CHUNK, CHUNK), 0)
    t = lax.broadcasted_iota(jnp.int32, (CHUNK, CHUNK), 1)
    return (s >= t) if rev else (s <= t)


def _running_max(x, rev):
    n = x.shape[1]
    pos = lax.broadcasted_iota(jnp.int32, x.shape, 1) & (CHUNK - 1)
    k = 1
    while k < CHUNK:
        if rev:
            shifted, ok = pltpu.roll(x, n - k, 1), pos < CHUNK - k
        else:
            shifted, ok = pltpu.roll(x, k, 1), pos >= k
        x = jnp.maximum(x, jnp.where(ok, shifted, NEG_INF))
        k *= 2
    return x


def _stage_bf16(src_hbm, dst_ref, stage_ref, sem, rows=None):
    rows = stage_ref.shape[1] if rows is None else rows
    n_chunks = src_hbm.shape[0] // rows

    def copy(c):
        return pltpu.make_async_copy(src_hbm.at[pl.ds(c * rows, rows), :],
                                     stage_ref.at[c % 2, pl.ds(0, rows), :], sem.at[c % 2])

    copy(0).start()
    for c in range(n_chunks):
        if c + 1 < n_chunks:
            copy(c + 1).start()
        copy(c).wait()
        dst_ref[c * rows:(c + 1) * rows, :] = stage_ref[c % 2, 0:rows, :].astype(BF16)


def _mix_out_pre(tile, tiles_per_seq, bg_ref, u_ref, uprev_ref, unext_ref, cw_ref, cb_ref, y_ref, wmo_ref):
    tm = u_ref.shape[0]
    u = u_ref[...]
    has_prev = (tile % tiles_per_seq != 0).astype(F32)
    has_next = (tile % tiles_per_seq != tiles_per_seq - 1).astype(F32)
    prev_row = uprev_ref[HALO_ROWS - 1:HALO_ROWS, :] * has_prev
    next_row = unext_ref[0:1, :] * has_next
    ri = lax.broadcasted_iota(jnp.int32, u.shape, 0)
    u_m1 = jnp.where(ri == 0, prev_row, pltpu.roll(u, 1, 0))
    u_p1 = jnp.where(ri == tm - 1, next_row, pltpu.roll(u, tm - 1, 0))
    conv = cw_ref[0:1, :] * u_m1 + cw_ref[1:2, :] * u + cw_ref[2:3, :] * u_p1
    y_conv = (bg_ref[...] * (conv + cb_ref[...])).astype(BF16)
    return jnp.dot(y_conv, wmo_ref[0:CONV_WIDTH, :], preferred_element_type=F32) \
        + jnp.dot(y_ref[...], wmo_ref[CONV_WIDTH:, :], preferred_element_type=F32)


def _zero_after(v):
    u = pltpu.bitcast(v, jnp.uint32)
    acc = None
    for r in range(u.shape[0] // 8):
        for c in range(u.shape[1] // 128):
            t = u[r * 8:(r + 1) * 8, c * 128:(c + 1) * 128]
            acc = t if acc is None else acc | t
    return pltpu.bitcast((acc >> 16) >> 16, F32)


def _held(v, zero):
    return v if zero is None else v + jnp.tile(zero, (v.shape[0] // 8, v.shape[1] // 128))


def _ffn_pipeline(n_tiles, tm, gpost_ref, win_ref, wout_ref, h_ref, xn_ref, y_ref, o_ref,
                  stage_weights, front_at, front_begin, front_rows, resid_rows):
    i = pl.program_id(0)
    g_half = 0.5 * gpost_ref[...]
    all_rows = slice(0, tm)

    @pl.when(i == 0)
    def _():
        stage_weights()
        xn_ref[0] = front_rows(front_begin(), all_rows, None).astype(BF16)
        y_ref[...] = jnp.zeros(y_ref.shape, F32)

    @pl.when((i >= 1) & (i <= n_tiles))
    def _():
        xn = xn_ref[(i + 1) % 2]
        rp = tm // FF_SIDE_CHUNKS
        ctx = pending = None
        for j in range(D_FF // FF_TILE):
            lo = j * FF_TILE
            if j == front_at:
                ctx = front_begin()
            gate = jnp.dot(xn, win_ref[:, lo:lo + FF_TILE], preferred_element_type=F32)
            up = jnp.dot(xn, win_ref[:, D_FF + lo:D_FF + lo + FF_TILE], preferred_element_type=F32)
            hm = gate * jax.nn.sigmoid(gate) * up
            h_ref[:, lo:lo + FF_TILE] = hm.astype(BF16)
            if pending is not None:
                h_ref[0:8, lo:lo + 128] = (hm[0:8, 0:128] + pending).astype(BF16)
                pending = None
            begin = _zero_after(hm[0:8, 0:128])
            if j < FF_SIDE_CHUNKS:
                rs = slice(j * rp, (j + 1) * rp)
                out_rows = resid_rows(2, rs) + _rms(_held(y_ref[rs, :], begin), g_half)
                o_ref[rs, :] = out_rows
                pending = _zero_after(out_rows)
            if front_at <= j < front_at + FF_SIDE_CHUNKS:
                rs = slice((j - front_at) * rp, (j - front_at + 1) * rp)
                xn_rows = front_rows(ctx, rs, begin)
                xn_ref[i % 2, rs, :] = xn_rows.astype(BF16)
                pending = _zero_after(xn_rows) if pending is None else pending + _zero_after(xn_rows)
        assert pending is None
        y_ref[...] = jnp.dot(h_ref[...], wout_ref[...], preferred_element_type=F32)

    @pl.when(i == n_tiles + 1)
    def _():
        o_ref[...] = resid_rows(2, all_rows) + _rms(y_ref[...], g_half)


def _ffn_kernel(layer, n_tiles, xa_ref, xc_ref, gpre_ref, gpost_ref, win_hbm, wout_hbm, o_ref,
                h_ref, win_ref, wout_ref, stage_in, stage_out, sem, xn_ref, y_ref):
    def stage_weights():
        _stage_bf16(win_hbm.at[layer], win_ref, stage_in, sem)
        _stage_bf16(wout_hbm.at[layer], wout_ref, stage_out, sem)

    _ffn_pipeline(n_tiles, xa_ref.shape[0], gpost_ref, win_ref, wout_ref, h_ref, xn_ref, y_ref, o_ref,
                  stage_weights, front_at=0, front_begin=lambda: None,
                  front_rows=lambda ctx, rs, begin: _rms(_held(xa_ref[rs, :], begin), gpre_ref[...]),
                  resid_rows=lambda lag, rs: xc_ref[rs, :])


def _mix_ffn_kernel(layer, n_tiles, tiles_per_seq, x_ref, bg_ref, u_ref, uprev_ref, unext_ref, cw_ref, cb_ref,
                    y_ref, gmix_ref, wmo_hbm, gpre_ref, gpost_ref, win_hbm, wout_hbm, o_ref,
                    h_ref, win_ref, wout_ref, wmo_ref, stage_in, stage_out, sem, xn_ref, yf_ref, xres_ref):
    i = pl.program_id(0)
    tile = jnp.minimum(i, n_tiles - 1)

    def stage_weights():
        _stage_bf16(wmo_hbm.at[layer], wmo_ref, stage_out, sem, rows=D_MODEL // 8)
        _stage_bf16(win_hbm.at[layer], win_ref, stage_in, sem)
        _stage_bf16(wout_hbm.at[layer], wout_ref, stage_out, sem)
        xres_ref[...] = jnp.zeros(xres_ref.shape, F32)

    def front_begin():
        return _mix_out_pre(tile, tiles_per_seq, bg_ref, u_ref, uprev_ref, unext_ref, cw_ref, cb_ref,
                            y_ref, wmo_ref)

    def front_rows(h, rs, begin):
        x2 = x_ref[rs, :] + _rms(_held(h[rs, :], begin), gmix_ref[...])
        xres_ref[i % 3, rs, :] = x2
        return _rms(x2, gpre_ref[...])

    _ffn_pipeline(n_tiles, x_ref.shape[0], gpost_ref, win_ref, wout_ref, h_ref, xn_ref, yf_ref, o_ref,
                  stage_weights, front_at=2, front_begin=front_begin, front_rows=front_rows,
                  resid_rows=lambda lag, rs: xres_ref[(i + 3 - lag) % 3, rs, :])


def _ffn_scratch(tm, stage_chunks):
    return [
        pltpu.VMEM((tm, D_FF), BF16),
        pltpu.VMEM((D_MODEL, 2 * D_FF), BF16),
        pltpu.VMEM((D_FF, D_MODEL), BF16),
    ], [
        pltpu.VMEM((2, D_MODEL // stage_chunks, 2 * D_FF), F32),
        pltpu.VMEM((2, D_FF // stage_chunks, D_MODEL), F32),
        pltpu.SemaphoreType.DMA((2,)),
    ]


def _ffn(x, gpre, gpost, w_in, w_out, layer, tm):
    T = x.shape[0]
    nt = T // tm
    resident, staging = _ffn_scratch(tm, stage_chunks=8)
    tile = lambda lag: pl.BlockSpec((tm, D_MODEL), lambda i: (jnp.clip(i - lag, 0, nt - 1), 0))
    return pl.pallas_call(
        functools.partial(_ffn_kernel, layer, nt),
        grid=(nt + 2,),
        in_specs=[
            tile(0), tile(2),
            _resident((1, D_MODEL)),
            _resident((1, D_MODEL)),
            pl.BlockSpec(memory_space=pl.ANY),
            pl.BlockSpec(memory_space=pl.ANY),
        ],
        out_specs=tile(2),
        out_shape=jax.ShapeDtypeStruct((T, D_MODEL), F32),
        scratch_shapes=resident + staging + [pltpu.VMEM((2, tm, D_MODEL), BF16),
                                             pltpu.VMEM((tm, D_MODEL), F32)],
        compiler_params=pltpu.CompilerParams(
            dimension_semantics=("arbitrary",), vmem_limit_bytes=VMEM_LIMIT),
        name="ffn",
    )(x, x, gpre, gpost, w_in, w_out)


def _mix_ffn(x, bg, u, conv_w, conv_b, y_mlstm, gmix, w_mix_out, gpre, gpost, w_in, w_out, layer, seq, tm):
    T = x.shape[0]
    nt = T // tm
    tiles_per_seq = seq // tm
    sub = tm // HALO_ROWS
    last = T // HALO_ROWS - 1
    front = lambda i: jnp.minimum(i, nt - 1)
    row = lambda w: pl.BlockSpec((tm, w), lambda i: (front(i), 0))
    hbm = pl.BlockSpec(memory_space=pl.ANY)
    resident, staging = _ffn_scratch(tm, stage_chunks=16)
    return pl.pallas_call(
        functools.partial(_mix_ffn_kernel, layer, nt, tiles_per_seq),
        grid=(nt + 2,),
        in_specs=[
            row(D_MODEL), row(CONV_WIDTH), row(CONV_WIDTH),
            pl.BlockSpec((HALO_ROWS, CONV_WIDTH), lambda i: (jnp.maximum(front(i) * sub - 1, 0), 0)),
            pl.BlockSpec((HALO_ROWS, CONV_WIDTH), lambda i: (jnp.minimum((front(i) + 1) * sub, last), 0)),
            _resident((3, CONV_WIDTH)), _resident((1, CONV_WIDTH)),
            row(HEADS * DV), _resident((1, D_MODEL)), hbm,
            _resident((1, D_MODEL)), _resident((1, D_MODEL)), hbm, hbm,
        ],
        out_specs=pl.BlockSpec((tm, D_MODEL), lambda i: (jnp.clip(i - 2, 0, nt - 1), 0)),
        out_shape=jax.ShapeDtypeStruct((T, D_MODEL), F32),
        scratch_shapes=resident + [pltpu.VMEM((D_MODEL, D_MODEL), BF16)] + staging + [
            pltpu.VMEM((2, tm, D_MODEL), BF16),
            pltpu.VMEM((tm, D_MODEL), F32),
            pltpu.VMEM((3, tm, D_MODEL), F32)],
        compiler_params=pltpu.CompilerParams(
            dimension_semantics=("arbitrary",), vmem_limit_bytes=VMEM_LIMIT),
        name="mix_ffn",
    )(x, bg, u, u, u, conv_w, conv_b, y_mlstm, gmix, w_mix_out, gpre, gpost, w_in, w_out)


def _gate_rows(zr, p_ref, ccol_ref):
    L = CHUNK
    n_chunks = zr.shape[1] // L
    fwd_row = lax.broadcasted_iota(jnp.int32, (NGR, L), 0) < HEADS
    fwd_col = lax.broadcasted_iota(jnp.int32, (NGR, 1), 0) < HEADS
    li = zr[0:NGR]
    gates = jnp.concatenate([li, _log_sigmoid(zr[NGR:2 * NGR])], axis=0)
    x3 = jnp.concatenate(_split3(gates), axis=0)
    stacked = jnp.concatenate([x3[:, c * L:(c + 1) * L] for c in range(n_chunks)], axis=0)
    cum_ops = jnp.concatenate([_visible(False), _visible(True)], axis=1).astype(BF16)
    cum = jnp.dot(stacked, cum_ops, preferred_element_type=F32)
    b_chunks = []
    for c in range(n_chunks):
        blk = cum[c * 48:(c + 1) * 48]
        s16 = blk[0:16] + blk[16:32] + blk[32:48]
        b_chunks.append(jnp.where(fwd_row, s16[NGR:2 * NGR, 0:L], s16[NGR:2 * NGR, L:2 * L]))
    b = jnp.concatenate(b_chunks, axis=1)
    cc = li - b
    fwd_all = lax.broadcasted_iota(jnp.int32, cc.shape, 0) < HEADS
    p_ref[P_M, :] = jnp.where(fwd_all, _running_max(cc, False), _running_max(cc, True))
    p_ref[P_B, :] = b
    pad = jnp.zeros((L - NGR, L), F32)
    for c in range(n_chunks):
        sl = slice(c * L, (c + 1) * L)
        b_c, cc_c = b_chunks[c], cc[:, sl]
        g = jnp.where(fwd_col, b_c[:, L - 1:L], b_c[:, 0:1])
        m_chunk = g + jnp.max(cc_c, axis=1, keepdims=True)
        p_ref[P_E, sl] = jnp.exp(g + cc_c - m_chunk)
        p_ref[P_G, sl] = jnp.broadcast_to(g, (NGR, L))
        p_ref[P_MC, sl] = jnp.broadcast_to(m_chunk, (NGR, L))
        ccol_ref[sl, :] = jnp.concatenate([cc_c, pad], axis=0).T


def _mix_in_kernel(layer, x_ref, gpre_ref, w_hbm, gbias_ref,
                   bg_ref, u_ref, qt_ref, k_ref, ksw_ref, vt_ref, o_ref, p_ref, ccol_ref,
                   wb_ref, stage_ref, sem):
    W, HK, HV = CONV_WIDTH, HEADS * DK, HEADS * DV
    q0 = 3 * W
    k0, v0 = q0 + HK, q0 + 2 * HK
    o0 = v0 + HV
    g0 = o0 + HV

    @pl.when(pl.program_id(0) == 0)
    def _():
        src = w_hbm.at[layer]
        rows = stage_ref.shape[1]
        n_chunks = g0 // rows

        def copy(c):
            n = rows if c < n_chunks else 2 * NGR
            return pltpu.make_async_copy(src.at[pl.ds(c * rows, n), :], stage_ref.at[c % 2, pl.ds(0, n), :],
                                         sem.at[c % 2])

        copy(0).start()
        for c in range(n_chunks + 1):
            if c < n_chunks:
                copy(c + 1).start()
            copy(c).wait()
            if c < n_chunks:
                wb_ref[:, c * rows:(c + 1) * rows] = stage_ref[c % 2].T.astype(BF16)
            else:
                gt = stage_ref[c % 2, 0:128, :].T
                lane = lax.broadcasted_iota(jnp.int32, gt.shape, 1)
                wb_ref[:, g0:g0 + 128] = jnp.where(lane < 2 * NGR, gt, 0.0).astype(BF16)

    xn = _rms(x_ref[...], gpre_ref[...]).astype(BF16)
    proj = lambda a, b: jnp.dot(xn, wb_ref[:, a:b], preferred_element_type=F32)

    zg = proj(g0, g0 + 128) + gbias_ref[...]
    bg_ref[...] = proj(0, W)
    u_ref[...] = proj(W, 2 * W) * proj(2 * W, 3 * W)
    _gate_rows(zg.T[0:2 * NGR], p_ref, ccol_ref)
    qt_ref[...] = (proj(q0, k0) * (DK ** -0.5)).T.astype(BF16)
    kk = proj(k0, v0)
    k_ref[...] = kk.astype(BF16)
    for pair in range(HEADS // 2):
        ps = slice(pair * 2 * DK, (pair + 1) * 2 * DK)
        ksw_ref[:, ps] = pltpu.roll(kk[:, ps], DK, 1).astype(BF16)
    vt_ref[...] = proj(v0, o0).T.astype(BF16)
    o_ref[...] = proj(o0, g0)


def _mix_in(x, gpre, w, gbias, layer, tm):
    T = x.shape[0]
    row = lambda w: pl.BlockSpec((tm, w), lambda i: (i, 0))
    col = lambda h: pl.BlockSpec((h, tm), lambda i: (0, i))
    HK, HV = HEADS * DK, HEADS * DV
    g0 = 3 * CONV_WIDTH + 2 * HK + 2 * HV
    return pl.pallas_call(
        functools.partial(_mix_in_kernel, layer),
        grid=(T // tm,),
        in_specs=[row(D_MODEL), _resident((1, D_MODEL)), pl.BlockSpec(memory_space=pl.ANY),
                  _resident((1, 128))],
        scratch_shapes=[pltpu.VMEM((D_MODEL, g0 + 128), BF16),
                        pltpu.VMEM((2, g0 // MIX_STAGE_CHUNKS, D_MODEL), F32),
                        pltpu.SemaphoreType.DMA((2,))],
        out_specs=[row(CONV_WIDTH), row(CONV_WIDTH), col(HK), row(HK), row(HK), col(HV), row(HV),
                   col(5 * NGR), row(128)],
        out_shape=[
            jax.ShapeDtypeStruct((T, CONV_WIDTH), F32),
            jax.ShapeDtypeStruct((T, CONV_WIDTH), F32),
            jax.ShapeDtypeStruct((HK, T), BF16),
            jax.ShapeDtypeStruct((T, HK), BF16),
            jax.ShapeDtypeStruct((T, HK), BF16),
            jax.ShapeDtypeStruct((HV, T), BF16),
            jax.ShapeDtypeStruct((T, HV), F32),
            jax.ShapeDtypeStruct((5 * NGR, T), F32),
            jax.ShapeDtypeStruct((T, 128), F32),
        ],
        compiler_params=pltpu.CompilerParams(
            dimension_semantics=("arbitrary",), vmem_limit_bytes=VMEM_LIMIT),
        name="mix_in",
    )(x, gpre, w, gbias)


def _state_half(d, h):
    return (h % 2) ^ d


def _scan_direction(d, vt_ref, k_ref, p_ref, ct_ref, mp_ref, ct_state, m_state):
    L = CHUNK
    n_chunks = k_ref.shape[0] // L
    lane_half = lax.broadcasted_iota(jnp.int32, (L, 128), 1) // DK
    lane_half_s = lax.broadcasted_iota(jnp.int32, (DVA, 128), 1) // DK
    ones_rows = jnp.ones((DVA - DV, L), BF16)

    cts = [ct_state[d, h] for h in range(HEADS)]
    m_prev = m_state[d]
    for chunk in (range(n_chunks - 1, -1, -1) if d == 1 else range(n_chunks)):
        sl = slice(chunk * L, (chunk + 1) * L)
        e, g, m_chunk = p_ref[P_E, sl], p_ref[P_G, sl], p_ref[P_MC, sl]
        mp_ref[:, sl] = m_prev
        m_new = jnp.maximum(g + m_prev, m_chunk)
        a_old = jnp.exp(g + m_prev - m_new)
        a_new = jnp.exp(m_chunk - m_new)
        new_cts = []
        for h in range(HEADS):
            r = HEADS * d + h
            vt_aug = jnp.concatenate([vt_ref[h * DV:(h + 1) * DV, sl], ones_rows], axis=0)
            vte = (vt_aug.astype(F32) * e[r:r + 1, :]).astype(BF16)
            pair = slice((h // 2) * 2 * DK, (h // 2 + 1) * 2 * DK)
            k_half = jnp.where(lane_half == _state_half(d, h), k_ref[sl, pair], jnp.zeros((L, 128), BF16))
            ct_chunk = jnp.dot(vte, k_half, preferred_element_type=F32)
            new_cts.append(a_old[r:r + 1, 0:1] * cts[h] + a_new[r:r + 1, 0:1] * ct_chunk)
        for pr in range(HEADS // 2):
            both = jnp.where(lane_half_s == _state_half(d, 2 * pr), cts[2 * pr], cts[2 * pr + 1])
            ct_ref[chunk, pr * DVA:(pr + 1) * DVA, :] = both.astype(BF16)
        cts = new_cts
        m_prev = m_new
    for h in range(HEADS):
        ct_state[d, h] = cts[h]
    m_state[d] = m_prev


def _mlstm_scan_kernel(vt_f, k_f, p_f, vt_b, ksw_b, p_b, ct_f, mp_f, ct_b, mp_b, ct_state, m_state):
    @pl.when(pl.program_id(1) == 0)
    def _():
        ct_state[...] = jnp.zeros(ct_state.shape, F32)
        m_state[...] = jnp.full(m_state.shape, NEG_INF, F32)

    _scan_direction(0, vt_f, k_f, p_f, ct_f, mp_f, ct_state, m_state)
    _scan_direction(1, vt_b, ksw_b, p_b, ct_b, mp_b, ct_state, m_state)


def _mlstm_out_kernel(qt_ref, k_ref, vt_ref, p_ref, mp_f, mp_b, ccol_ref, ct_f, ct_b,
                      o_ref, gain_ref, y_ref):
    L = CHUNK
    n_chunks = k_ref.shape[0] // L
    visible = (_visible(False), _visible(True))
    fwd_row = lax.broadcasted_iota(jnp.int32, (NGR, L), 0) < HEADS
    lane_half = lax.broadcasted_iota(jnp.int32, (DVA, 128), 1) // DK
    ones_rows = jnp.ones((DVA - DV, L), BF16)
    zq = jnp.zeros((DK, L), BF16)
    for chunk in range(n_chunks):
        sl = slice(chunk * L, (chunk + 1) * L)
        m_prev = jnp.where(fwd_row, mp_f[:, sl], mp_b[:, sl])
        n_t = jnp.maximum(m_prev, p_ref[P_M, sl])
        f_inter = jnp.exp(m_prev - n_t)
        e_min = jnp.exp(-(p_ref[P_B, sl] + n_t))
        ccol = ccol_ref[sl, :]
        for h in range(HEADS):
            hs = slice(h * DV, (h + 1) * DV)
            qt = qt_ref[h * DK:(h + 1) * DK, sl]
            k_pair = k_ref[sl, (h // 2) * 2 * DK:(h // 2 + 1) * 2 * DK]
            qt_pair = jnp.concatenate([qt, zq] if h % 2 == 0 else [zq, qt], axis=0)
            st = jnp.dot(k_pair, qt_pair, preferred_element_type=F32)
            pts, qfs = [], []
            for d in range(2):
                r = HEADS * d + h
                arg = jnp.where(visible[d], ccol[:, r:r + 1] - n_t[r:r + 1, :], NEG_INF)
                pts.append((jnp.exp(arg) * st).astype(BF16))
                qfs.append((qt.astype(F32) * f_inter[r:r + 1, :]).astype(BF16))
            inter = [jnp.concatenate([qfs[0], zq], axis=1), jnp.concatenate([zq, qfs[1]], axis=1)]
            first = 0 if _state_half(0, h) == 0 else 1
            rhs = jnp.concatenate([jnp.concatenate(pts, axis=1), inter[first], inter[1 - first]], axis=0)
            vt_aug = jnp.concatenate([vt_ref[hs, sl], ones_rows], axis=0)
            ps = slice((h // 2) * DVA, (h // 2 + 1) * DVA)
            blocks = (ct_f[chunk, ps, :], ct_b[chunk, ps, :])
            ct = jnp.where(lane_half == 0, blocks[first], blocks[1 - first])
            both = jnp.dot(jnp.concatenate([vt_aug, ct], axis=1), rhs, preferred_element_type=F32)
            ht = None
            for d in range(2):
                r = HEADS * d + h
                numer = both[0:DV, d * L:(d + 1) * L]
                denom = both[DV:DV + 1, d * L:(d + 1) * L]
                part = numer / jnp.maximum(jnp.abs(denom), e_min[r:r + 1, :])
                ht = part if ht is None else ht + part
            ms = jnp.mean(ht * ht, axis=0, keepdims=True)
            hn = ht * lax.rsqrt(ms + EPS) * gain_ref[hs, :]
            y_ref[sl, hs] = (jax.nn.sigmoid(o_ref[sl, hs]) * hn.T).astype(BF16)


def _mlstm(qt, k, ksw, vt, p, ccol, o, gain_b, batch, seq, scan_rows, rows):
    T = batch * seq
    ng = seq // scan_rows
    HK, HV = HEADS * DK, HEADS * DV

    def scan_specs(group_of):
        blk = lambda b, j: b * ng + group_of(j)
        ins = [pl.BlockSpec((HV, scan_rows), lambda b, j: (0, blk(b, j))),
               pl.BlockSpec((scan_rows, HK), lambda b, j: (blk(b, j), 0)),
               pl.BlockSpec((5 * NGR, scan_rows), lambda b, j: (0, blk(b, j)))]
        outs = [pl.BlockSpec((scan_rows // CHUNK, CT_ROWS, 128), lambda b, j: (blk(b, j), 0, 0)),
                pl.BlockSpec((NGR, scan_rows), lambda b, j: (0, blk(b, j)))]
        return ins, outs

    ins_f, outs_f = scan_specs(lambda j: j)
    ins_b, outs_b = scan_specs(lambda j: ng - 1 - j)
    state_shapes = [jax.ShapeDtypeStruct((T // CHUNK, CT_ROWS, 128), BF16),
                    jax.ShapeDtypeStruct((NGR, T), F32)]
    ct_f, mp_f, ct_b, mp_b = pl.pallas_call(
        _mlstm_scan_kernel,
        grid=(batch, ng),
        in_specs=ins_f + ins_b,
        out_specs=outs_f + outs_b,
        out_shape=state_shapes + state_shapes,
        scratch_shapes=[pltpu.VMEM((2, HEADS, DVA, 128), F32), pltpu.VMEM((2, NGR, 128), F32)],
        compiler_params=pltpu.CompilerParams(dimension_semantics=("arbitrary", "arbitrary"),
                                             vmem_limit_bytes=VMEM_LIMIT),
        name="mlstm_scan",
    )(vt, k, p, vt, ksw, p)

    row = lambda w: pl.BlockSpec((rows, w), lambda i: (i, 0))
    col = lambda h: pl.BlockSpec((h, rows), lambda i: (0, i))
    ctb = pl.BlockSpec((rows // CHUNK, CT_ROWS, 128), lambda i: (i, 0, 0))
    return pl.pallas_call(
        _mlstm_out_kernel,
        grid=(T // rows,),
        in_specs=[col(HK), row(HK), col(HV), col(5 * NGR), col(NGR), col(NGR), row(128), ctb, ctb,
                  row(HV), _resident((HV, 128))],
        out_specs=row(HV),
        out_shape=jax.ShapeDtypeStruct((T, HV), BF16),
        compiler_params=pltpu.CompilerParams(dimension_semantics=("arbitrary",),
                                             vmem_limit_bytes=VMEM_LIMIT),
        name="mlstm_out",
    )(qt, k, vt, p, mp_f, mp_b, ccol, ct_f, ct_b, o, gain_b)


def kernel(x, norm_ffn1_pre, norm_ffn1_post, w_ffn1_in, w_ffn1_out, norm_mix_pre, norm_mix_post,
           w_mix_in, conv_w, conv_b, gate_i_bias, gate_f_bias, mlstm_norm, w_mix_out,
           norm_ffn2_pre, norm_ffn2_post, w_ffn2_in, w_ffn2_out):
    batch, seq, _ = x.shape
    T = batch * seq
    depth = norm_ffn1_pre.shape[0]
    tm = 512
    xt = x.reshape(T, D_MODEL)
    HV = HEADS * DV
    for l in range(depth):
        xt = _ffn(xt, norm_ffn1_pre[l][None], norm_ffn1_post[l][None], w_ffn1_in, w_ffn1_out, l, tm)

        gbias = jnp.pad(jnp.concatenate([gate_i_bias[l], gate_f_bias[l]]), (0, 128 - 2 * NGR))[None]
        bg, u, qt, k, ksw, vt, o, p, ccol = _mix_in(
            xt, norm_mix_pre[l][None], jnp.swapaxes(w_mix_in, 1, 2), gbias, l, 2 * tm)
        gain_b = jnp.broadcast_to(mlstm_norm[l][:, None], (HV, 128))
        y_mlstm = _mlstm(qt, k, ksw, vt, p, ccol, o, gain_b, batch, seq, SCAN_ROWS, MLSTM_ROWS)
        xt = _mix_ffn(xt, bg, u, conv_w[l], conv_b[l][None], y_mlstm, norm_mix_post[l][None], w_mix_out,
                      norm_ffn2_pre[l][None], norm_ffn2_post[l][None], w_ffn2_in, w_ffn2_out, l, seq, tm)
    return xt.reshape(batch, seq, D_MODEL)
```

```python
import functools

import jax
import jax.numpy as jnp
from jax import lax
from jax.experimental import pallas as pl
from jax.experimental.pallas import tpu as pltpu

D_MODEL = 1024
D_FF = 2816
CONV_WIDTH = 512
HEADS = 4
DK = 64
DV = 128
CHUNK = 128
EPS = 1e-6
NEG_INF = -1e30

FF_TILE = 256
FF_STAGE_CHUNKS = 8
FF_SIDE_CHUNKS = 8
MIX_OUT_PIECES = 2
MIX_STAGE_CHUNKS = 8
MLSTM_ROWS = 2048
SCAN_ROWS = 2048
DVA = DV + 16
HALO_ROWS = 8
NGR = 2 * HEADS
CT_ROWS = (HEADS // 2) * DVA
VMEM_LIMIT = 56 * 1024 * 1024

P_E, P_M, P_B, P_G, P_MC = (slice(i * NGR, (i + 1) * NGR) for i in range(5))

F32 = jnp.float32
BF16 = jnp.bfloat16


def _rms(x, g):
    return x * lax.rsqrt(jnp.mean(x * x, axis=-1, keepdims=True) + EPS) * g


def _log_sigmoid(z):
    return jnp.minimum(z, 0.0) - jnp.log1p(jnp.exp(-jnp.abs(z)))


def _resident(shape):
    zeros = (0,) * len(shape)
    return pl.BlockSpec(shape, lambda *_: zeros, pipeline_mode=pl.Buffered(1))


def _split3(x):
    hi = x.astype(BF16)
    r1 = x - hi.astype(F32)
    mid = r1.astype(BF16)
    lo = (r1 - mid.astype(F32)).astype(BF16)
    return hi, mid, lo


def _visible(rev):
    s = lax.broadcasted_iota(jnp.int32, (CHUNK, CHUNK), 0)
    t = lax.broadcasted_iota(jnp.int32, (CHUNK, CHUNK), 1)
    return (s >= t) if rev else (s <= t)


def _running_max(x, rev):
    n = x.shape[1]
    pos = lax.broadcasted_iota(jnp.int32, x.shape, 1) & (CHUNK - 1)
    k = 1
    while k < CHUNK:
        if rev:
            shifted, ok = pltpu.roll(x, n - k, 1), pos < CHUNK - k
        else:
            shifted, ok = pltpu.roll(x, k, 1), pos >= k
        x = jnp.maximum(x, jnp.where(ok, shifted, NEG_INF))
        k *= 2
    return x


def _stage_bf16(src_hbm, dst_ref, stage_ref, sem, rows=None):
    rows = stage_ref.shape[1] if rows is None else rows
    n_chunks = src_hbm.shape[0] // rows

    def copy(c):
        return pltpu.make_async_copy(src_hbm.at[pl.ds(c * rows, rows), :],
                                     stage_ref.at[c % 2, pl.ds(0, rows), :], sem.at[c % 2])

    copy(0).start()
    for c in range(n_chunks):
        if c + 1 < n_chunks:
            copy(c + 1).start()
        copy(c).wait()
        dst_ref[c * rows:(c + 1) * rows, :] = stage_ref[c % 2, 0:rows, :].astype(BF16)


def _mix_out_tile(tiles_per_seq, x_ref, bg_ref, u_ref, uprev_ref, unext_ref, cw_ref, cb_ref,
                  y_ref, wmo_ref, gmix_ref, gpre_ref):
    i = pl.program_id(0)
    tm = u_ref.shape[0]
    u = u_ref[...]
    has_prev = (i % tiles_per_seq != 0).astype(F32)
    has_next = (i % tiles_per_seq != tiles_per_seq - 1).astype(F32)
    prev_row = uprev_ref[HALO_ROWS - 1:HALO_ROWS, :] * has_prev
    next_row = unext_ref[0:1, :] * has_next
    ri = lax.broadcasted_iota(jnp.int32, u.shape, 0)
    u_m1 = jnp.where(ri == 0, prev_row, pltpu.roll(u, 1, 0))
    u_p1 = jnp.where(ri == tm - 1, next_row, pltpu.roll(u, tm - 1, 0))
    conv = cw_ref[0:1, :] * u_m1 + cw_ref[1:2, :] * u + cw_ref[2:3, :] * u_p1
    y_conv = (bg_ref[...] * (conv + cb_ref[...])).astype(BF16)
    rp = tm // MIX_OUT_PIECES
    xs, xns = [], []
    for r in range(MIX_OUT_PIECES):
        rs = slice(r * rp, (r + 1) * rp)
        h = jnp.dot(y_conv[rs, :], wmo_ref[0:CONV_WIDTH, :], preferred_element_type=F32) \
            + jnp.dot(y_ref[rs, :], wmo_ref[CONV_WIDTH:, :], preferred_element_type=F32)
        xs.append(x_ref[rs, :] + _rms(h, gmix_ref[...]))
        xns.append(_rms(xs[-1], gpre_ref[...]).astype(BF16))
    return jnp.concatenate(xs, axis=0), jnp.concatenate(xns, axis=0)


def _ffn_tile(x, xn, gpost_ref, win_ref, wout_ref, h_ref):
    for j in range(D_FF // FF_TILE):
        lo = j * FF_TILE
        gate = jnp.dot(xn, win_ref[:, lo:lo + FF_TILE], preferred_element_type=F32)
        up = jnp.dot(xn, win_ref[:, D_FF + lo:D_FF + lo + FF_TILE], preferred_element_type=F32)
        h_ref[:, lo:lo + FF_TILE] = (gate * jax.nn.sigmoid(gate) * up).astype(BF16)
    y = jnp.dot(h_ref[...], wout_ref[...], preferred_element_type=F32)
    return x + _rms(y, 0.5 * gpost_ref[...])


def _zero_after(v):
    u = pltpu.bitcast(v, jnp.uint32)
    acc = None
    for r in range(u.shape[0] // 8):
        for c in range(u.shape[1] // 128):
            t = u[r * 8:(r + 1) * 8, c * 128:(c + 1) * 128]
            acc = t if acc is None else acc | t
    return pltpu.bitcast((acc >> 16) >> 16, F32)


def _ffn_kernel(layer, n_tiles, xa_ref, xc_ref, gpre_ref, gpost_ref, win_hbm, wout_hbm, o_ref,
                h_ref, win_ref, wout_ref, stage_in, stage_out, sem, xn_ref, y_ref):
    i = pl.program_id(0)
    tm = xa_ref.shape[0]
    g_half = 0.5 * gpost_ref[...]

    @pl.when(i == 0)
    def _():
        _stage_bf16(win_hbm.at[layer], win_ref, stage_in, sem)
        _stage_bf16(wout_hbm.at[layer], wout_ref, stage_out, sem)
        xn_ref[0] = _rms(xa_ref[...], gpre_ref[...]).astype(BF16)
        y_ref[...] = jnp.zeros(y_ref.shape, F32)

    @pl.when((i >= 1) & (i <= n_tiles))
    def _():
        xn = xn_ref[(i - 1) % 2]
        rp = tm // FF_SIDE_CHUNKS
        for j in range(D_FF // FF_TILE):
            lo = j * FF_TILE
            gate = jnp.dot(xn, win_ref[:, lo:lo + FF_TILE], preferred_element_type=F32)
            up = jnp.dot(xn, win_ref[:, D_FF + lo:D_FF + lo + FF_TILE], preferred_element_type=F32)
            hm = gate * jax.nn.sigmoid(gate) * up
            h_ref[:, lo:lo + FF_TILE] = hm.astype(BF16)
            if j < FF_SIDE_CHUNKS:
                rs = slice(j * rp, (j + 1) * rp)
                out_rows = xc_ref[rs, :] + _rms(y_ref[rs, :], g_half)
                o_ref[rs, :] = out_rows
                xn_rows = _rms(xa_ref[rs, :], gpre_ref[...])
                xn_ref[i % 2, rs, :] = xn_rows.astype(BF16)
                zero = _zero_after(out_rows) + _zero_after(xn_rows)
                h_ref[0:8, lo:lo + 128] = (hm[0:8, 0:128] + zero).astype(BF16)
        y_ref[...] = jnp.dot(h_ref[...], wout_ref[...], preferred_element_type=F32)

    @pl.when(i == n_tiles + 1)
    def _():
        o_ref[...] = xc_ref[...] + _rms(y_ref[...], g_half)


def _mix_ffn_kernel(layer, tiles_per_seq, x_ref, bg_ref, u_ref, uprev_ref, unext_ref, cw_ref, cb_ref,
                    y_ref, gmix_ref, wmo_hbm, gpre_ref, gpost_ref, win_hbm, wout_hbm, o_ref,
                    h_ref, win_ref, wout_ref, wmo_ref, stage_in, stage_out, sem):
    @pl.when(pl.program_id(0) == 0)
    def _():
        _stage_bf16(wmo_hbm.at[layer], wmo_ref, stage_out, sem, rows=D_MODEL // 4)
        _stage_bf16(win_hbm.at[layer], win_ref, stage_in, sem)
        _stage_bf16(wout_hbm.at[layer], wout_ref, stage_out, sem)

    x, xn = _mix_out_tile(tiles_per_seq, x_ref, bg_ref, u_ref, uprev_ref, unext_ref, cw_ref, cb_ref,
                          y_ref, wmo_ref, gmix_ref, gpre_ref)
    o_ref[...] = _ffn_tile(x, xn, gpost_ref, win_ref, wout_ref, h_ref)


def _ffn_scratch(tm):
    return [
        pltpu.VMEM((tm, D_FF), BF16),
        pltpu.VMEM((D_MODEL, 2 * D_FF), BF16),
        pltpu.VMEM((D_FF, D_MODEL), BF16),
    ], [
        pltpu.VMEM((2, D_MODEL // FF_STAGE_CHUNKS, 2 * D_FF), F32),
        pltpu.VMEM((2, D_FF // FF_STAGE_CHUNKS, D_MODEL), F32),
        pltpu.SemaphoreType.DMA((2,)),
    ]


def _ffn(x, gpre, gpost, w_in, w_out, layer, tm):
    T = x.shape[0]
    nt = T // tm
    resident, staging = _ffn_scratch(tm)
    tile = lambda lag: pl.BlockSpec((tm, D_MODEL), lambda i: (jnp.clip(i - lag, 0, nt - 1), 0))
    return pl.pallas_call(
        functools.partial(_ffn_kernel, layer, nt),
        grid=(nt + 2,),
        in_specs=[
            tile(0), tile(2),
            _resident((1, D_MODEL)),
            _resident((1, D_MODEL)),
            pl.BlockSpec(memory_space=pl.ANY),
            pl.BlockSpec(memory_space=pl.ANY),
        ],
        out_specs=tile(2),
        out_shape=jax.ShapeDtypeStruct((T, D_MODEL), F32),
        scratch_shapes=resident + staging + [pltpu.VMEM((2, tm, D_MODEL), BF16),
                                             pltpu.VMEM((tm, D_MODEL), F32)],
        compiler_params=pltpu.CompilerParams(
            dimension_semantics=("arbitrary",), vmem_limit_bytes=VMEM_LIMIT),
        name="ffn",
    )(x, x, gpre, gpost, w_in, w_out)


def _mix_ffn(x, bg, u, conv_w, conv_b, y_mlstm, gmix, w_mix_out, gpre, gpost, w_in, w_out, layer, seq, tm):
    T = x.shape[0]
    tiles_per_seq = seq // tm
    sub = tm // HALO_ROWS
    last = T // HALO_ROWS - 1
    row = lambda w: pl.BlockSpec((tm, w), lambda i: (i, 0))
    hbm = pl.BlockSpec(memory_space=pl.ANY)
    resident, staging = _ffn_scratch(tm)
    return pl.pallas_call(
        functools.partial(_mix_ffn_kernel, layer, tiles_per_seq),
        grid=(T // tm,),
        in_specs=[
            row(D_MODEL), row(CONV_WIDTH), row(CONV_WIDTH),
            pl.BlockSpec((HALO_ROWS, CONV_WIDTH), lambda i: (jnp.maximum(i * sub - 1, 0), 0)),
            pl.BlockSpec((HALO_ROWS, CONV_WIDTH), lambda i: (jnp.minimum((i + 1) * sub, last), 0)),
            _resident((3, CONV_WIDTH)), _resident((1, CONV_WIDTH)),
            row(HEADS * DV), _resident((1, D_MODEL)), hbm,
            _resident((1, D_MODEL)), _resident((1, D_MODEL)), hbm, hbm,
        ],
        out_specs=row(D_MODEL),
        out_shape=jax.ShapeDtypeStruct((T, D_MODEL), F32),
        scratch_shapes=resident + [pltpu.VMEM((D_MODEL, D_MODEL), BF16)] + staging,
        compiler_params=pltpu.CompilerParams(
            dimension_semantics=("arbitrary",), vmem_limit_bytes=VMEM_LIMIT),
        name="mix_ffn",
    )(x, bg, u, u, u, conv_w, conv_b, y_mlstm, gmix, w_mix_out, gpre, gpost, w_in, w_out)


def _gate_rows(zr, p_ref, ccol_ref):
    L = CHUNK
    n_chunks = zr.shape[1] // L
    fwd_row = lax.broadcasted_iota(jnp.int32, (NGR, L), 0) < HEADS
    fwd_col = lax.broadcasted_iota(jnp.int32, (NGR, 1), 0) < HEADS
    li = zr[0:NGR]
    gates = jnp.concatenate([li, _log_sigmoid(zr[NGR:2 * NGR])], axis=0)
    x3 = jnp.concatenate(_split3(gates), axis=0)
    stacked = jnp.concatenate([x3[:, c * L:(c + 1) * L] for c in range(n_chunks)], axis=0)
    cum_ops = jnp.concatenate([_visible(False), _visible(True)], axis=1).astype(BF16)
    cum = jnp.dot(stacked, cum_ops, preferred_element_type=F32)
    b_chunks = []
    for c in range(n_chunks):
        blk = cum[c * 48:(c + 1) * 48]
        s16 = blk[0:16] + blk[16:32] + blk[32:48]
        b_chunks.append(jnp.where(fwd_row, s16[NGR:2 * NGR, 0:L], s16[NGR:2 * NGR, L:2 * L]))
    b = jnp.concatenate(b_chunks, axis=1)
    cc = li - b
    fwd_all = lax.broadcasted_iota(jnp.int32, cc.shape, 0) < HEADS
    p_ref[P_M, :] = jnp.where(fwd_all, _running_max(cc, False), _running_max(cc, True))
    p_ref[P_B, :] = b
    pad = jnp.zeros((L - NGR, L), F32)
    for c in range(n_chunks):
        sl = slice(c * L, (c + 1) * L)
        b_c, cc_c = b_chunks[c], cc[:, sl]
        g = jnp.where(fwd_col, b_c[:, L - 1:L], b_c[:, 0:1])
        m_chunk = g + jnp.max(cc_c, axis=1, keepdims=True)
        p_ref[P_E, sl] = jnp.exp(g + cc_c - m_chunk)
        p_ref[P_G, sl] = jnp.broadcast_to(g, (NGR, L))
        p_ref[P_MC, sl] = jnp.broadcast_to(m_chunk, (NGR, L))
        ccol_ref[sl, :] = jnp.concatenate([cc_c, pad], axis=0).T


def _mix_in_kernel(layer, x_ref, gpre_ref, w_hbm, gbias_ref,
                   bg_ref, u_ref, qt_ref, k_ref, ksw_ref, vt_ref, o_ref, p_ref, ccol_ref,
                   wb_ref, stage_ref, sem):
    W, HK, HV = CONV_WIDTH, HEADS * DK, HEADS * DV
    q0 = 3 * W
    k0, v0 = q0 + HK, q0 + 2 * HK
    o0 = v0 + HV
    g0 = o0 + HV

    @pl.when(pl.program_id(0) == 0)
    def _():
        src = w_hbm.at[layer]
        rows = stage_ref.shape[1]
        n_chunks = g0 // rows

        def copy(c):
            n = rows if c < n_chunks else 2 * NGR
            return pltpu.make_async_copy(src.at[pl.ds(c * rows, n), :], stage_ref.at[c % 2, pl.ds(0, n), :],
                                         sem.at[c % 2])

        copy(0).start()
        for c in range(n_chunks + 1):
            if c < n_chunks:
                copy(c + 1).start()
            copy(c).wait()
            if c < n_chunks:
                wb_ref[:, c * rows:(c + 1) * rows] = stage_ref[c % 2].T.astype(BF16)
            else:
                gt = stage_ref[c % 2, 0:128, :].T
                lane = lax.broadcasted_iota(jnp.int32, gt.shape, 1)
                wb_ref[:, g0:g0 + 128] = jnp.where(lane < 2 * NGR, gt, 0.0).astype(BF16)

    xn = _rms(x_ref[...], gpre_ref[...]).astype(BF16)
    proj = lambda a, b: jnp.dot(xn, wb_ref[:, a:b], preferred_element_type=F32)

    zg = proj(g0, g0 + 128) + gbias_ref[...]
    bg_ref[...] = proj(0, W)
    u_ref[...] = proj(W, 2 * W) * proj(2 * W, 3 * W)
    _gate_rows(zg.T[0:2 * NGR], p_ref, ccol_ref)
    qt_ref[...] = (proj(q0, k0) * (DK ** -0.5)).T.astype(BF16)
    kk = proj(k0, v0)
    k_ref[...] = kk.astype(BF16)
    for pair in range(HEADS // 2):
        ps = slice(pair * 2 * DK, (pair + 1) * 2 * DK)
        ksw_ref[:, ps] = pltpu.roll(kk[:, ps], DK, 1).astype(BF16)
    vt_ref[...] = proj(v0, o0).T.astype(BF16)
    o_ref[...] = proj(o0, g0)


def _mix_in(x, gpre, w, gbias, layer, tm):
    T = x.shape[0]
    row = lambda w: pl.BlockSpec((tm, w), lambda i: (i, 0))
    col = lambda h: pl.BlockSpec((h, tm), lambda i: (0, i))
    HK, HV = HEADS * DK, HEADS * DV
    g0 = 3 * CONV_WIDTH + 2 * HK + 2 * HV
    return pl.pallas_call(
        functools.partial(_mix_in_kernel, layer),
        grid=(T // tm,),
        in_specs=[row(D_MODEL), _resident((1, D_MODEL)), pl.BlockSpec(memory_space=pl.ANY),
                  _resident((1, 128))],
        scratch_shapes=[pltpu.VMEM((D_MODEL, g0 + 128), BF16),
                        pltpu.VMEM((2, g0 // MIX_STAGE_CHUNKS, D_MODEL), F32),
                        pltpu.SemaphoreType.DMA((2,))],
        out_specs=[row(CONV_WIDTH), row(CONV_WIDTH), col(HK), row(HK), row(HK), col(HV), row(HV),
                   col(5 * NGR), row(128)],
        out_shape=[
            jax.ShapeDtypeStruct((T, CONV_WIDTH), F32),
            jax.ShapeDtypeStruct((T, CONV_WIDTH), F32),
            jax.ShapeDtypeStruct((HK, T), BF16),
            jax.ShapeDtypeStruct((T, HK), BF16),
            jax.ShapeDtypeStruct((T, HK), BF16),
            jax.ShapeDtypeStruct((HV, T), BF16),
            jax.ShapeDtypeStruct((T, HV), F32),
            jax.ShapeDtypeStruct((5 * NGR, T), F32),
            jax.ShapeDtypeStruct((T, 128), F32),
        ],
        compiler_params=pltpu.CompilerParams(
            dimension_semantics=("arbitrary",), vmem_limit_bytes=VMEM_LIMIT),
        name="mix_in",
    )(x, gpre, w, gbias)


def _state_half(d, h):
    return (h % 2) ^ d


def _scan_direction(d, vt_ref, k_ref, p_ref, ct_ref, mp_ref, ct_state, m_state):
    L = CHUNK
    n_chunks = k_ref.shape[0] // L
    lane_half = lax.broadcasted_iota(jnp.int32, (L, 128), 1) // DK
    lane_half_s = lax.broadcasted_iota(jnp.int32, (DVA, 128), 1) // DK
    ones_rows = jnp.ones((DVA - DV, L), BF16)

    cts = [ct_state[d, h] for h in range(HEADS)]
    m_prev = m_state[d]
    for chunk in (range(n_chunks - 1, -1, -1) if d == 1 else range(n_chunks)):
        sl = slice(chunk * L, (chunk + 1) * L)
        e, g, m_chunk = p_ref[P_E, sl], p_ref[P_G, sl], p_ref[P_MC, sl]
        mp_ref[:, sl] = m_prev
        m_new = jnp.maximum(g + m_prev, m_chunk)
        a_old = jnp.exp(g + m_prev - m_new)
        a_new = jnp.exp(m_chunk - m_new)
        new_cts = []
        for h in range(HEADS):
            r = HEADS * d + h
            vt_aug = jnp.concatenate([vt_ref[h * DV:(h + 1) * DV, sl], ones_rows], axis=0)
            vte = (vt_aug.astype(F32) * e[r:r + 1, :]).astype(BF16)
            pair = slice((h // 2) * 2 * DK, (h // 2 + 1) * 2 * DK)
            k_half = jnp.where(lane_half == _state_half(d, h), k_ref[sl, pair], jnp.zeros((L, 128), BF16))
            ct_chunk = jnp.dot(vte, k_half, preferred_element_type=F32)
            new_cts.append(a_old[r:r + 1, 0:1] * cts[h] + a_new[r:r + 1, 0:1] * ct_chunk)
        for pr in range(HEADS // 2):
            both = jnp.where(lane_half_s == _state_half(d, 2 * pr), cts[2 * pr], cts[2 * pr + 1])
            ct_ref[chunk, pr * DVA:(pr + 1) * DVA, :] = both.astype(BF16)
        cts = new_cts
        m_prev = m_new
    for h in range(HEADS):
        ct_state[d, h] = cts[h]
    m_state[d] = m_prev


def _mlstm_scan_kernel(vt_f, k_f, p_f, vt_b, ksw_b, p_b, ct_f, mp_f, ct_b, mp_b, ct_state, m_state):
    @pl.when(pl.program_id(1) == 0)
    def _():
        ct_state[...] = jnp.zeros(ct_state.shape, F32)
        m_state[...] = jnp.full(m_state.shape, NEG_INF, F32)

    _scan_direction(0, vt_f, k_f, p_f, ct_f, mp_f, ct_state, m_state)
    _scan_direction(1, vt_b, ksw_b, p_b, ct_b, mp_b, ct_state, m_state)


def _mlstm_out_kernel(qt_ref, k_ref, vt_ref, p_ref, mp_f, mp_b, ccol_ref, ct_f, ct_b,
                      o_ref, gain_ref, y_ref):
    L = CHUNK
    n_chunks = k_ref.shape[0] // L
    visible = (_visible(False), _visible(True))
    fwd_row = lax.broadcasted_iota(jnp.int32, (NGR, L), 0) < HEADS
    lane_half = lax.broadcasted_iota(jnp.int32, (DVA, 128), 1) // DK
    ones_rows = jnp.ones((DVA - DV, L), BF16)
    zq = jnp.zeros((DK, L), BF16)
    for chunk in range(n_chunks):
        sl = slice(chunk * L, (chunk + 1) * L)
        m_prev = jnp.where(fwd_row, mp_f[:, sl], mp_b[:, sl])
        n_t = jnp.maximum(m_prev, p_ref[P_M, sl])
        f_inter = jnp.exp(m_prev - n_t)
        e_min = jnp.exp(-(p_ref[P_B, sl] + n_t))
        ccol = ccol_ref[sl, :]
        for h in range(HEADS):
            hs = slice(h * DV, (h + 1) * DV)
            qt = qt_ref[h * DK:(h + 1) * DK, sl]
            k_pair = k_ref[sl, (h // 2) * 2 * DK:(h // 2 + 1) * 2 * DK]
            qt_pair = jnp.concatenate([qt, zq] if h % 2 == 0 else [zq, qt], axis=0)
            st = jnp.dot(k_pair, qt_pair, preferred_element_type=F32)
            pts, qfs = [], []
            for d in range(2):
                r = HEADS * d + h
                arg = jnp.where(visible[d], ccol[:, r:r + 1] - n_t[r:r + 1, :], NEG_INF)
                pts.append((jnp.exp(arg) * st).astype(BF16))
                qfs.append((qt.astype(F32) * f_inter[r:r + 1, :]).astype(BF16))
            inter = [jnp.concatenate([qfs[0], zq], axis=1), jnp.concatenate([zq, qfs[1]], axis=1)]
            first = 0 if _state_half(0, h) == 0 else 1
            rhs = jnp.concatenate([jnp.concatenate(pts, axis=1), inter[first], inter[1 - first]], axis=0)
            vt_aug = jnp.concatenate([vt_ref[hs, sl], ones_rows], axis=0)
            ps = slice((h // 2) * DVA, (h // 2 + 1) * DVA)
            blocks = (ct_f[chunk, ps, :], ct_b[chunk, ps, :])
            ct = jnp.where(lane_half == 0, blocks[first], blocks[1 - first])
            both = jnp.dot(jnp.concatenate([vt_aug, ct], axis=1), rhs, preferred_element_type=F32)
            ht = None
            for d in range(2):
                r = HEADS * d + h
                numer = both[0:DV, d * L:(d + 1) * L]
                denom = both[DV:DV + 1, d * L:(d + 1) * L]
                part = numer / jnp.maximum(jnp.abs(denom), e_min[r:r + 1, :])
                ht = part if ht is None else ht + part
            ms = jnp.mean(ht * ht, axis=0, keepdims=True)
            hn = ht * lax.rsqrt(ms + EPS) * gain_ref[hs, :]
            y_ref[sl, hs] = (jax.nn.sigmoid(o_ref[sl, hs]) * hn.T).astype(BF16)


def _mlstm(qt, k, ksw, vt, p, ccol, o, gain_b, batch, seq, scan_rows, rows):
    T = batch * seq
    ng = seq // scan_rows
    HK, HV = HEADS * DK, HEADS * DV

    def scan_specs(group_of):
        blk = lambda b, j: b * ng + group_of(j)
        ins = [pl.BlockSpec((HV, scan_rows), lambda b, j: (0, blk(b, j))),
               pl.BlockSpec((scan_rows, HK), lambda b, j: (blk(b, j), 0)),
               pl.BlockSpec((5 * NGR, scan_rows), lambda b, j: (0, blk(b, j)))]
        outs = [pl.BlockSpec((scan_rows // CHUNK, CT_ROWS, 128), lambda b, j: (blk(b, j), 0, 0)),
                pl.BlockSpec((NGR, scan_rows), lambda b, j: (0, blk(b, j)))]
        return ins, outs

    ins_f, outs_f = scan_specs(lambda j: j)
    ins_b, outs_b = scan_specs(lambda j: ng - 1 - j)
    state_shapes = [jax.ShapeDtypeStruct((T // CHUNK, CT_ROWS, 128), BF16),
                    jax.ShapeDtypeStruct((NGR, T), F32)]
    ct_f, mp_f, ct_b, mp_b = pl.pallas_call(
        _mlstm_scan_kernel,
        grid=(batch, ng),
        in_specs=ins_f + ins_b,
        out_specs=outs_f + outs_b,
        out_shape=state_shapes + state_shapes,
        scratch_shapes=[pltpu.VMEM((2, HEADS, DVA, 128), F32), pltpu.VMEM((2, NGR, 128), F32)],
        compiler_params=pltpu.CompilerParams(dimension_semantics=("arbitrary", "arbitrary"),
                                             vmem_limit_bytes=VMEM_LIMIT),
        name="mlstm_scan",
    )(vt, k, p, vt, ksw, p)

    row = lambda w: pl.BlockSpec((rows, w), lambda i: (i, 0))
    col = lambda h: pl.BlockSpec((h, rows), lambda i: (0, i))
    ctb = pl.BlockSpec((rows // CHUNK, CT_ROWS, 128), lambda i: (i, 0, 0))
    return pl.pallas_call(
        _mlstm_out_kernel,
        grid=(T // rows,),
        in_specs=[col(HK), row(HK), col(HV), col(5 * NGR), col(NGR), col(NGR), row(128), ctb, ctb,
                  row(HV), _resident((HV, 128))],
        out_specs=row(HV),
        out_shape=jax.ShapeDtypeStruct((T, HV), BF16),
        compiler_params=pltpu.CompilerParams(dimension_semantics=("arbitrary",),
                                             vmem_limit_bytes=VMEM_LIMIT),
        name="mlstm_out",
    )(qt, k, vt, p, mp_f, mp_b, ccol, ct_f, ct_b, o, gain_b)


def kernel(x, norm_ffn1_pre, norm_ffn1_post, w_ffn1_in, w_ffn1_out, norm_mix_pre, norm_mix_post,
           w_mix_in, conv_w, conv_b, gate_i_bias, gate_f_bias, mlstm_norm, w_mix_out,
           norm_ffn2_pre, norm_ffn2_post, w_ffn2_in, w_ffn2_out):
    batch, seq, _ = x.shape
    T = batch * seq
    depth = norm_ffn1_pre.shape[0]
    tm = 512
    xt = x.reshape(T, D_MODEL)
    HV = HEADS * DV
    for l in range(depth):
        xt = _ffn(xt, norm_ffn1_pre[l][None], norm_ffn1_post[l][None], w_ffn1_in, w_ffn1_out, l, tm)

        gbias = jnp.pad(jnp.concatenate([gate_i_bias[l], gate_f_bias[l]]), (0, 128 - 2 * NGR))[None]
        bg, u, qt, k, ksw, vt, o, p, ccol = _mix_in(
            xt, norm_mix_pre[l][None], jnp.swapaxes(w_mix_in, 1, 2), gbias, l, 2 * tm)
        gain_b = jnp.broadcast_to(mlstm_norm[l][:, None], (HV, 128))
        y_mlstm = _mlstm(qt, k, ksw, vt, p, ccol, o, gain_b, batch, seq, SCAN_ROWS, MLSTM_ROWS)
        xt = _mix_ffn(xt, bg, u, conv_w[l], conv_b[l][None], y_mlstm, norm_mix_post[l][None], w_mix_out,
                      norm_ffn2_pre[l][None], norm_ffn2_post[l][None], w_ffn2_in, w_ffn2_out, l, seq, tm)
    return xt.reshape(batch, seq, D_MODEL)
```

```python
import functools

import jax
import jax.numpy as jnp
from jax import lax
from jax.experimental import pallas as pl
from jax.experimental.pallas import tpu as pltpu

D_MODEL = 1024
D_FF = 2816
CONV_WIDTH = 512
HEADS = 4
DK = 64
DV = 128
CHUNK = 128
EPS = 1e-6
NEG_INF = -1e30

FF_TILE = 256
FF_STAGE_CHUNKS = 8
FF_SIDE_CHUNKS = 8
MIX_OUT_PIECES = 4
FF_LEAD_TILES = 1
MIX_STAGE_CHUNKS = 8
MLSTM_ROWS = 2048
SCAN_ROWS = 2048
DVA = DV + 16
HALO_ROWS = 8
NGR = 2 * HEADS
CT_ROWS = (HEADS // 2) * DVA
VMEM_LIMIT = 56 * 1024 * 1024

P_E, P_M, P_B, P_G, P_MC = (slice(i * NGR, (i + 1) * NGR) for i in range(5))

F32 = jnp.float32
BF16 = jnp.bfloat16


def _rms(x, g):
    return x * lax.rsqrt(jnp.mean(x * x, axis=-1, keepdims=True) + EPS) * g


def _log_sigmoid(z):
    return jnp.minimum(z, 0.0) - jnp.log1p(jnp.exp(-jnp.abs(z)))


def _resident(shape):
    zeros = (0,) * len(shape)
    return pl.BlockSpec(shape, lambda *_: zeros, pipeline_mode=pl.Buffered(1))


def _split3(x):
    hi = x.astype(BF16)
    r1 = x - hi.astype(F32)
    mid = r1.astype(BF16)
    lo = (r1 - mid.astype(F32)).astype(BF16)
    return hi, mid, lo


def _visible(rev):
    s = lax.broadcasted_iota(jnp.int32, (CHUNK, CHUNK), 0)
    t = lax.broadcasted_iota(jnp.int32, (CHUNK, CHUNK), 1)
    return (s >= t) if rev else (s <= t)


def _running_max(x, rev):
    n = x.shape[1]
    pos = lax.broadcasted_iota(jnp.int32, x.shape, 1) & (CHUNK - 1)
    k = 1
    while k < CHUNK:
        if rev:
            shifted, ok = pltpu.roll(x, n - k, 1), pos < CHUNK - k
        else:
            shifted, ok = pltpu.roll(x, k, 1), pos >= k
        x = jnp.maximum(x, jnp.where(ok, shifted, NEG_INF))
        k *= 2
    return x


def _stage_bf16(src_hbm, dst_ref, stage_ref, sem, rows=None):
    rows = stage_ref.shape[1] if rows is None else rows
    n_chunks = src_hbm.shape[0] // rows

    def copy(c):
        return pltpu.make_async_copy(src_hbm.at[pl.ds(c * rows, rows), :],
                                     stage_ref.at[c % 2, pl.ds(0, rows), :], sem.at[c % 2])

    copy(0).start()
    for c in range(n_chunks):
        if c + 1 < n_chunks:
            copy(c + 1).start()
        copy(c).wait()
        dst_ref[c * rows:(c + 1) * rows, :] = stage_ref[c % 2, 0:rows, :].astype(BF16)


def _mix_out_tile(tiles_per_seq, x_ref, bg_ref, u_ref, uprev_ref, unext_ref, cw_ref, cb_ref,
                  y_ref, wmo_ref, gmix_ref, gpre_ref):
    i = pl.program_id(0)
    tm = u_ref.shape[0]
    u = u_ref[...]
    has_prev = (i % tiles_per_seq != 0).astype(F32)
    has_next = (i % tiles_per_seq != tiles_per_seq - 1).astype(F32)
    prev_row = uprev_ref[HALO_ROWS - 1:HALO_ROWS, :] * has_prev
    next_row = unext_ref[0:1, :] * has_next
    ri = lax.broadcasted_iota(jnp.int32, u.shape, 0)
    u_m1 = jnp.where(ri == 0, prev_row, pltpu.roll(u, 1, 0))
    u_p1 = jnp.where(ri == tm - 1, next_row, pltpu.roll(u, tm - 1, 0))
    conv = cw_ref[0:1, :] * u_m1 + cw_ref[1:2, :] * u + cw_ref[2:3, :] * u_p1
    y_conv = (bg_ref[...] * (conv + cb_ref[...])).astype(BF16)
    rp = tm // MIX_OUT_PIECES
    xs, xns = [], []
    for r in range(MIX_OUT_PIECES):
        rs = slice(r * rp, (r + 1) * rp)
        h = jnp.dot(y_ref[rs, :], wmo_ref[CONV_WIDTH:, :], preferred_element_type=F32) \
            + jnp.dot(y_conv[rs, :], wmo_ref[0:CONV_WIDTH, :], preferred_element_type=F32)
        xs.append(x_ref[rs, :] + _rms(h, gmix_ref[...]))
        xns.append(_rms(xs[-1], gpre_ref[...]).astype(BF16))
    return jnp.concatenate(xs, axis=0), xns


def _ffn_tile(x, xn_pieces, gpost_ref, win_ref, wout_ref, h_ref):
    rp = xn_pieces[0].shape[0]
    xn = jnp.concatenate(xn_pieces, axis=0)
    for j in range(D_FF // FF_TILE):
        lo = j * FF_TILE
        lhs = list(enumerate(xn_pieces)) if j < FF_LEAD_TILES and len(xn_pieces) > 1 else [(None, xn)]
        for r, xr in lhs:
            rs = slice(None) if r is None else slice(r * rp, (r + 1) * rp)
            gate = jnp.dot(xr, win_ref[:, lo:lo + FF_TILE], preferred_element_type=F32)
            up = jnp.dot(xr, win_ref[:, D_FF + lo:D_FF + lo + FF_TILE], preferred_element_type=F32)
            h_ref[rs, lo:lo + FF_TILE] = (gate * jax.nn.sigmoid(gate) * up).astype(BF16)
    y = jnp.dot(h_ref[...], wout_ref[...], preferred_element_type=F32)
    return x + _rms(y, 0.5 * gpost_ref[...])


def _zero_after(v):
    u = pltpu.bitcast(v, jnp.uint32)
    acc = None
    for r in range(u.shape[0] // 8):
        for c in range(u.shape[1] // 128):
            t = u[r * 8:(r + 1) * 8, c * 128:(c + 1) * 128]
            acc = t if acc is None else acc | t
    return pltpu.bitcast((acc >> 16) >> 16, F32)


def _ffn_kernel(layer, n_tiles, xa_ref, xc_ref, gpre_ref, gpost_ref, win_hbm, wout_hbm, o_ref,
                h_ref, win_ref, wout_ref, stage_in, stage_out, sem, xn_ref, y_ref):
    i = pl.program_id(0)
    tm = xa_ref.shape[0]
    g_half = 0.5 * gpost_ref[...]

    @pl.when(i == 0)
    def _():
        _stage_bf16(win_hbm.at[layer], win_ref, stage_in, sem)
        _stage_bf16(wout_hbm.at[layer], wout_ref, stage_out, sem)
        xn_ref[0] = _rms(xa_ref[...], gpre_ref[...]).astype(BF16)
        y_ref[...] = jnp.zeros(y_ref.shape, F32)

    @pl.when((i >= 1) & (i <= n_tiles))
    def _():
        xn = xn_ref[(i - 1) % 2]
        rp = tm // FF_SIDE_CHUNKS
        for j in range(D_FF // FF_TILE):
            lo = j * FF_TILE
            gate = jnp.dot(xn, win_ref[:, lo:lo + FF_TILE], preferred_element_type=F32)
            up = jnp.dot(xn, win_ref[:, D_FF + lo:D_FF + lo + FF_TILE], preferred_element_type=F32)
            hm = gate * jax.nn.sigmoid(gate) * up
            h_ref[:, lo:lo + FF_TILE] = hm.astype(BF16)
            if j < FF_SIDE_CHUNKS:
                rs = slice(j * rp, (j + 1) * rp)
                out_rows = xc_ref[rs, :] + _rms(y_ref[rs, :], g_half)
                o_ref[rs, :] = out_rows
                xn_rows = _rms(xa_ref[rs, :], gpre_ref[...])
                xn_ref[i % 2, rs, :] = xn_rows.astype(BF16)
                zero = _zero_after(out_rows) + _zero_after(xn_rows)
                h_ref[0:8, lo:lo + 128] = (hm[0:8, 0:128] + zero).astype(BF16)
        y_ref[...] = jnp.dot(h_ref[...], wout_ref[...], preferred_element_type=F32)

    @pl.when(i == n_tiles + 1)
    def _():
        o_ref[...] = xc_ref[...] + _rms(y_ref[...], g_half)


def _mix_ffn_kernel(layer, tiles_per_seq, x_ref, bg_ref, u_ref, uprev_ref, unext_ref, cw_ref, cb_ref,
                    y_ref, gmix_ref, wmo_hbm, gpre_ref, gpost_ref, win_hbm, wout_hbm, o_ref,
                    h_ref, win_ref, wout_ref, wmo_ref, stage_in, stage_out, sem):
    @pl.when(pl.program_id(0) == 0)
    def _():
        _stage_bf16(wmo_hbm.at[layer], wmo_ref, stage_out, sem, rows=D_MODEL // 4)
        _stage_bf16(win_hbm.at[layer], win_ref, stage_in, sem)
        _stage_bf16(wout_hbm.at[layer], wout_ref, stage_out, sem)

    x, xn = _mix_out_tile(tiles_per_seq, x_ref, bg_ref, u_ref, uprev_ref, unext_ref, cw_ref, cb_ref,
                          y_ref, wmo_ref, gmix_ref, gpre_ref)
    o_ref[...] = _ffn_tile(x, xn, gpost_ref, win_ref, wout_ref, h_ref)


def _ffn_scratch(tm):
    return [
        pltpu.VMEM((tm, D_FF), BF16),
        pltpu.VMEM((D_MODEL, 2 * D_FF), BF16),
        pltpu.VMEM((D_FF, D_MODEL), BF16),
    ], [
        pltpu.VMEM((2, D_MODEL // FF_STAGE_CHUNKS, 2 * D_FF), F32),
        pltpu.VMEM((2, D_FF // FF_STAGE_CHUNKS, D_MODEL), F32),
        pltpu.SemaphoreType.DMA((2,)),
    ]


def _ffn(x, gpre, gpost, w_in, w_out, layer, tm):
    T = x.shape[0]
    nt = T // tm
    resident, staging = _ffn_scratch(tm)
    tile = lambda lag: pl.BlockSpec((tm, D_MODEL), lambda i: (jnp.clip(i - lag, 0, nt - 1), 0))
    return pl.pallas_call(
        functools.partial(_ffn_kernel, layer, nt),
        grid=(nt + 2,),
        in_specs=[
            tile(0), tile(2),
            _resident((1, D_MODEL)),
            _resident((1, D_MODEL)),
            pl.BlockSpec(memory_space=pl.ANY),
            pl.BlockSpec(memory_space=pl.ANY),
        ],
        out_specs=tile(2),
        out_shape=jax.ShapeDtypeStruct((T, D_MODEL), F32),
        scratch_shapes=resident + staging + [pltpu.VMEM((2, tm, D_MODEL), BF16),
                                             pltpu.VMEM((tm, D_MODEL), F32)],
        compiler_params=pltpu.CompilerParams(
            dimension_semantics=("arbitrary",), vmem_limit_bytes=VMEM_LIMIT),
        name="ffn",
    )(x, x, gpre, gpost, w_in, w_out)


def _mix_ffn(x, bg, u, conv_w, conv_b, y_mlstm, gmix, w_mix_out, gpre, gpost, w_in, w_out, layer, seq, tm):
    T = x.shape[0]
    tiles_per_seq = seq // tm
    sub = tm // HALO_ROWS
    last = T // HALO_ROWS - 1
    row = lambda w: pl.BlockSpec((tm, w), lambda i: (i, 0))
    hbm = pl.BlockSpec(memory_space=pl.ANY)
    resident, staging = _ffn_scratch(tm)
    return pl.pallas_call(
        functools.partial(_mix_ffn_kernel, layer, tiles_per_seq),
        grid=(T // tm,),
        in_specs=[
            row(D_MODEL), row(CONV_WIDTH), row(CONV_WIDTH),
            pl.BlockSpec((HALO_ROWS, CONV_WIDTH), lambda i: (jnp.maximum(i * sub - 1, 0), 0)),
            pl.BlockSpec((HALO_ROWS, CONV_WIDTH), lambda i: (jnp.minimum((i + 1) * sub, last), 0)),
            _resident((3, CONV_WIDTH)), _resident((1, CONV_WIDTH)),
            row(HEADS * DV), _resident((1, D_MODEL)), hbm,
            _resident((1, D_MODEL)), _resident((1, D_MODEL)), hbm, hbm,
        ],
        out_specs=row(D_MODEL),
        out_shape=jax.ShapeDtypeStruct((T, D_MODEL), F32),
        scratch_shapes=resident + [pltpu.VMEM((D_MODEL, D_MODEL), BF16)] + staging,
        compiler_params=pltpu.CompilerParams(
            dimension_semantics=("arbitrary",), vmem_limit_bytes=VMEM_LIMIT),
        name="mix_ffn",
    )(x, bg, u, u, u, conv_w, conv_b, y_mlstm, gmix, w_mix_out, gpre, gpost, w_in, w_out)


def _gate_rows(zr, p_ref, ccol_ref):
    L = CHUNK
    n_chunks = zr.shape[1] // L
    fwd_row = lax.broadcasted_iota(jnp.int32, (NGR, L), 0) < HEADS
    fwd_col = lax.broadcasted_iota(jnp.int32, (NGR, 1), 0) < HEADS
    li = zr[0:NGR]
    gates = jnp.concatenate([li, _log_sigmoid(zr[NGR:2 * NGR])], axis=0)
    x3 = jnp.concatenate(_split3(gates), axis=0)
    stacked = jnp.concatenate([x3[:, c * L:(c + 1) * L] for c in range(n_chunks)], axis=0)
    cum_ops = jnp.concatenate([_visible(False), _visible(True)], axis=1).astype(BF16)
    cum = jnp.dot(stacked, cum_ops, preferred_element_type=F32)
    b_chunks = []
    for c in range(n_chunks):
        blk = cum[c * 48:(c + 1) * 48]
        s16 = blk[0:16] + blk[16:32] + blk[32:48]
        b_chunks.append(jnp.where(fwd_row, s16[NGR:2 * NGR, 0:L], s16[NGR:2 * NGR, L:2 * L]))
    b = jnp.concatenate(b_chunks, axis=1)
    cc = li - b
    fwd_all = lax.broadcasted_iota(jnp.int32, cc.shape, 0) < HEADS
    p_ref[P_M, :] = jnp.where(fwd_all, _running_max(cc, False), _running_max(cc, True))
    p_ref[P_B, :] = b
    pad = jnp.zeros((L - NGR, L), F32)
    for c in range(n_chunks):
        sl = slice(c * L, (c + 1) * L)
        b_c, cc_c = b_chunks[c], cc[:, sl]
        g = jnp.where(fwd_col, b_c[:, L - 1:L], b_c[:, 0:1])
        m_chunk = g + jnp.max(cc_c, axis=1, keepdims=True)
        p_ref[P_E, sl] = jnp.exp(g + cc_c - m_chunk)
        p_ref[P_G, sl] = jnp.broadcast_to(g, (NGR, L))
        p_ref[P_MC, sl] = jnp.broadcast_to(m_chunk, (NGR, L))
        ccol_ref[sl, :] = jnp.concatenate([cc_c, pad], axis=0).T


def _mix_in_kernel(layer, x_ref, gpre_ref, w_hbm, gbias_ref,
                   bg_ref, u_ref, qt_ref, k_ref, ksw_ref, vt_ref, o_ref, p_ref, ccol_ref,
                   wb_ref, stage_ref, sem):
    W, HK, HV = CONV_WIDTH, HEADS * DK, HEADS * DV
    q0 = 3 * W
    k0, v0 = q0 + HK, q0 + 2 * HK
    o0 = v0 + HV
    g0 = o0 + HV

    @pl.when(pl.program_id(0) == 0)
    def _():
        src = w_hbm.at[layer]
        rows = stage_ref.shape[1]
        n_chunks = g0 // rows

        def copy(c):
            n = rows if c < n_chunks else 2 * NGR
            return pltpu.make_async_copy(src.at[pl.ds(c * rows, n), :], stage_ref.at[c % 2, pl.ds(0, n), :],
                                         sem.at[c % 2])

        copy(0).start()
        for c in range(n_chunks + 1):
            if c < n_chunks:
                copy(c + 1).start()
            copy(c).wait()
            if c < n_chunks:
                wb_ref[:, c * rows:(c + 1) * rows] = stage_ref[c % 2].T.astype(BF16)
            else:
                gt = stage_ref[c % 2, 0:128, :].T
                lane = lax.broadcasted_iota(jnp.int32, gt.shape, 1)
                wb_ref[:, g0:g0 + 128] = jnp.where(lane < 2 * NGR, gt, 0.0).astype(BF16)

    xn = _rms(x_ref[...], gpre_ref[...]).astype(BF16)
    proj = lambda a, b: jnp.dot(xn, wb_ref[:, a:b], preferred_element_type=F32)

    zg = proj(g0, g0 + 128) + gbias_ref[...]
    bg_ref[...] = proj(0, W)
    u_ref[...] = proj(W, 2 * W) * proj(2 * W, 3 * W)
    _gate_rows(zg.T[0:2 * NGR], p_ref, ccol_ref)
    qt_ref[...] = (proj(q0, k0) * (DK ** -0.5)).T.astype(BF16)
    kk = proj(k0, v0)
    k_ref[...] = kk.astype(BF16)
    for pair in range(HEADS // 2):
        ps = slice(pair * 2 * DK, (pair + 1) * 2 * DK)
        ksw_ref[:, ps] = pltpu.roll(kk[:, ps], DK, 1).astype(BF16)
    vt_ref[...] = proj(v0, o0).T.astype(BF16)
    o_ref[...] = proj(o0, g0)


def _mix_in(x, gpre, w, gbias, layer, tm):
    T = x.shape[0]
    row = lambda w: pl.BlockSpec((tm, w), lambda i: (i, 0))
    col = lambda h: pl.BlockSpec((h, tm), lambda i: (0, i))
    HK, HV = HEADS * DK, HEADS * DV
    g0 = 3 * CONV_WIDTH + 2 * HK + 2 * HV
    return pl.pallas_call(
        functools.partial(_mix_in_kernel, layer),
        grid=(T // tm,),
        in_specs=[row(D_MODEL), _resident((1, D_MODEL)), pl.BlockSpec(memory_space=pl.ANY),
                  _resident((1, 128))],
        scratch_shapes=[pltpu.VMEM((D_MODEL, g0 + 128), BF16),
                        pltpu.VMEM((2, g0 // MIX_STAGE_CHUNKS, D_MODEL), F32),
                        pltpu.SemaphoreType.DMA((2,))],
        out_specs=[row(CONV_WIDTH), row(CONV_WIDTH), col(HK), row(HK), row(HK), col(HV), row(HV),
                   col(5 * NGR), row(128)],
        out_shape=[
            jax.ShapeDtypeStruct((T, CONV_WIDTH), F32),
            jax.ShapeDtypeStruct((T, CONV_WIDTH), F32),
            jax.ShapeDtypeStruct((HK, T), BF16),
            jax.ShapeDtypeStruct((T, HK), BF16),
            jax.ShapeDtypeStruct((T, HK), BF16),
            jax.ShapeDtypeStruct((HV, T), BF16),
            jax.ShapeDtypeStruct((T, HV), F32),
            jax.ShapeDtypeStruct((5 * NGR, T), F32),
            jax.ShapeDtypeStruct((T, 128), F32),
        ],
        compiler_params=pltpu.CompilerParams(
            dimension_semantics=("arbitrary",), vmem_limit_bytes=VMEM_LIMIT),
        name="mix_in",
    )(x, gpre, w, gbias)


def _state_half(d, h):
    return (h % 2) ^ d


def _scan_direction(d, vt_ref, k_ref, p_ref, ct_ref, mp_ref, ct_state, m_state):
    L = CHUNK
    n_chunks = k_ref.shape[0] // L
    lane_half = lax.broadcasted_iota(jnp.int32, (L, 128), 1) // DK
    lane_half_s = lax.broadcasted_iota(jnp.int32, (DVA, 128), 1) // DK
    ones_rows = jnp.ones((DVA - DV, L), BF16)

    cts = [ct_state[d, h] for h in range(HEADS)]
    m_prev = m_state[d]
    for chunk in (range(n_chunks - 1, -1, -1) if d == 1 else range(n_chunks)):
        sl = slice(chunk * L, (chunk + 1) * L)
        e, g, m_chunk = p_ref[P_E, sl], p_ref[P_G, sl], p_ref[P_MC, sl]
        mp_ref[:, sl] = m_prev
        m_new = jnp.maximum(g + m_prev, m_chunk)
        a_old = jnp.exp(g + m_prev - m_new)
        a_new = jnp.exp(m_chunk - m_new)
        new_cts = []
        for h in range(HEADS):
            r = HEADS * d + h
            vt_aug = jnp.concatenate([vt_ref[h * DV:(h + 1) * DV, sl], ones_rows], axis=0)
            vte = (vt_aug.astype(F32) * e[r:r + 1, :]).astype(BF16)
            pair = slice((h // 2) * 2 * DK, (h // 2 + 1) * 2 * DK)
            k_half = jnp.where(lane_half == _state_half(d, h), k_ref[sl, pair], jnp.zeros((L, 128), BF16))
            ct_chunk = jnp.dot(vte, k_half, preferred_element_type=F32)
            new_cts.append(a_old[r:r + 1, 0:1] * cts[h] + a_new[r:r + 1, 0:1] * ct_chunk)
        for pr in range(HEADS // 2):
            both = jnp.where(lane_half_s == _state_half(d, 2 * pr), cts[2 * pr], cts[2 * pr + 1])
            ct_ref[chunk, pr * DVA:(pr + 1) * DVA, :] = both.astype(BF16)
        cts = new_cts
        m_prev = m_new
    for h in range(HEADS):
        ct_state[d, h] = cts[h]
    m_state[d] = m_prev


def _mlstm_scan_kernel(vt_f, k_f, p_f, vt_b, ksw_b, p_b, ct_f, mp_f, ct_b, mp_b, ct_state, m_state):
    @pl.when(pl.program_id(1) == 0)
    def _():
        ct_state[...] = jnp.zeros(ct_state.shape, F32)
        m_state[...] = jnp.full(m_state.shape, NEG_INF, F32)

    _scan_direction(0, vt_f, k_f, p_f, ct_f, mp_f, ct_state, m_state)
    _scan_direction(1, vt_b, ksw_b, p_b, ct_b, mp_b, ct_state, m_state)


def _mlstm_out_kernel(qt_ref, k_ref, vt_ref, p_ref, mp_f, mp_b, ccol_ref, ct_f, ct_b,
                      o_ref, gain_ref, y_ref):
    L = CHUNK
    n_chunks = k_ref.shape[0] // L
    visible = (_visible(False), _visible(True))
    fwd_row = lax.broadcasted_iota(jnp.int32, (NGR, L), 0) < HEADS
    lane_half = lax.broadcasted_iota(jnp.int32, (DVA, 128), 1) // DK
    ones_rows = jnp.ones((DVA - DV, L), BF16)
    zq = jnp.zeros((DK, L), BF16)
    for chunk in range(n_chunks):
        sl = slice(chunk * L, (chunk + 1) * L)
        m_prev = jnp.where(fwd_row, mp_f[:, sl], mp_b[:, sl])
        n_t = jnp.maximum(m_prev, p_ref[P_M, sl])
        f_inter = jnp.exp(m_prev - n_t)
        e_min = jnp.exp(-(p_ref[P_B, sl] + n_t))
        ccol = ccol_ref[sl, :]
        for h in range(HEADS):
            hs = slice(h * DV, (h + 1) * DV)
            qt = qt_ref[h * DK:(h + 1) * DK, sl]
            k_pair = k_ref[sl, (h // 2) * 2 * DK:(h // 2 + 1) * 2 * DK]
            qt_pair = jnp.concatenate([qt, zq] if h % 2 == 0 else [zq, qt], axis=0)
            st = jnp.dot(k_pair, qt_pair, preferred_element_type=F32)
            pts, qfs = [], []
            for d in range(2):
                r = HEADS * d + h
                arg = jnp.where(visible[d], ccol[:, r:r + 1] - n_t[r:r + 1, :], NEG_INF)
                pts.append((jnp.exp(arg) * st).astype(BF16))
                qfs.append((qt.astype(F32) * f_inter[r:r + 1, :]).astype(BF16))
            inter = [jnp.concatenate([qfs[0], zq], axis=1), jnp.concatenate([zq, qfs[1]], axis=1)]
            first = 0 if _state_half(0, h) == 0 else 1
            rhs = jnp.concatenate([jnp.concatenate(pts, axis=1), inter[first], inter[1 - first]], axis=0)
            vt_aug = jnp.concatenate([vt_ref[hs, sl], ones_rows], axis=0)
            ps = slice((h // 2) * DVA, (h // 2 + 1) * DVA)
            blocks = (ct_f[chunk, ps, :], ct_b[chunk, ps, :])
            ct = jnp.where(lane_half == 0, blocks[first], blocks[1 - first])
            both = jnp.dot(jnp.concatenate([vt_aug, ct], axis=1), rhs, preferred_element_type=F32)
            ht = None
            for d in range(2):
                r = HEADS * d + h
                numer = both[0:DV, d * L:(d + 1) * L]
                denom = both[DV:DV + 1, d * L:(d + 1) * L]
                part = numer / jnp.maximum(jnp.abs(denom), e_min[r:r + 1, :])
                ht = part if ht is None else ht + part
            ms = jnp.mean(ht * ht, axis=0, keepdims=True)
            hn = ht * lax.rsqrt(ms + EPS) * gain_ref[hs, :]
            y_ref[sl, hs] = (jax.nn.sigmoid(o_ref[sl, hs]) * hn.T).astype(BF16)


def _mlstm(qt, k, ksw, vt, p, ccol, o, gain_b, batch, seq, scan_rows, rows):
    T = batch * seq
    ng = seq // scan_rows
    HK, HV = HEADS * DK, HEADS * DV

    def scan_specs(group_of):
        blk = lambda b, j: b * ng + group_of(j)
        ins = [pl.BlockSpec((HV, scan_rows), lambda b, j: (0, blk(b, j))),
               pl.BlockSpec((scan_rows, HK), lambda b, j: (blk(b, j), 0)),
               pl.BlockSpec((5 * NGR, scan_rows), lambda b, j: (0, blk(b, j)))]
        outs = [pl.BlockSpec((scan_rows // CHUNK, CT_ROWS, 128), lambda b, j: (blk(b, j), 0, 0)),
                pl.BlockSpec((NGR, scan_rows), lambda b, j: (0, blk(b, j)))]
        return ins, outs

    ins_f, outs_f = scan_specs(lambda j: j)
    ins_b, outs_b = scan_specs(lambda j: ng - 1 - j)
    state_shapes = [jax.ShapeDtypeStruct((T // CHUNK, CT_ROWS, 128), BF16),
                    jax.ShapeDtypeStruct((NGR, T), F32)]
    ct_f, mp_f, ct_b, mp_b = pl.pallas_call(
        _mlstm_scan_kernel,
        grid=(batch, ng),
        in_specs=ins_f + ins_b,
        out_specs=outs_f + outs_b,
        out_shape=state_shapes + state_shapes,
        scratch_shapes=[pltpu.VMEM((2, HEADS, DVA, 128), F32), pltpu.VMEM((2, NGR, 128), F32)],
        compiler_params=pltpu.CompilerParams(dimension_semantics=("arbitrary", "arbitrary"),
                                             vmem_limit_bytes=VMEM_LIMIT),
        name="mlstm_scan",
    )(vt, k, p, vt, ksw, p)

    row = lambda w: pl.BlockSpec((rows, w), lambda i: (i, 0))
    col = lambda h: pl.BlockSpec((h, rows), lambda i: (0, i))
    ctb = pl.BlockSpec((rows // CHUNK, CT_ROWS, 128), lambda i: (i, 0, 0))
    return pl.pallas_call(
        _mlstm_out_kernel,
        grid=(T // rows,),
        in_specs=[col(HK), row(HK), col(HV), col(5 * NGR), col(NGR), col(NGR), row(128), ctb, ctb,
                  row(HV), _resident((HV, 128))],
        out_specs=row(HV),
        out_shape=jax.ShapeDtypeStruct((T, HV), BF16),
        compiler_params=pltpu.CompilerParams(dimension_semantics=("arbitrary",),
                                             vmem_limit_bytes=VMEM_LIMIT),
        name="mlstm_out",
    )(qt, k, vt, p, mp_f, mp_b, ccol, ct_f, ct_b, o, gain_b)


def kernel(x, norm_ffn1_pre, norm_ffn1_post, w_ffn1_in, w_ffn1_out, norm_mix_pre, norm_mix_post,
           w_mix_in, conv_w, conv_b, gate_i_bias, gate_f_bias, mlstm_norm, w_mix_out,
           norm_ffn2_pre, norm_ffn2_post, w_ffn2_in, w_ffn2_out):
    batch, seq, _ = x.shape
    T = batch * seq
    depth = norm_ffn1_pre.shape[0]
    tm = 512
    xt = x.reshape(T, D_MODEL)
    HV = HEADS * DV
    for l in range(depth):
        xt = _ffn(xt, norm_ffn1_pre[l][None], norm_ffn1_post[l][None], w_ffn1_in, w_ffn1_out, l, tm)

        gbias = jnp.pad(jnp.concatenate([gate_i_bias[l], gate_f_bias[l]]), (0, 128 - 2 * NGR))[None]
        bg, u, qt, k, ksw, vt, o, p, ccol = _mix_in(
            xt, norm_mix_pre[l][None], jnp.swapaxes(w_mix_in, 1, 2), gbias, l, 2 * tm)
        gain_b = jnp.broadcast_to(mlstm_norm[l][:, None], (HV, 128))
        y_mlstm = _mlstm(qt, k, ksw, vt, p, ccol, o, gain_b, batch, seq, SCAN_ROWS, MLSTM_ROWS)
        xt = _mix_ffn(xt, bg, u, conv_w[l], conv_b[l][None], y_mlstm, norm_mix_post[l][None], w_mix_out,
                      norm_ffn2_pre[l][None], norm_ffn2_post[l][None], w_ffn2_in, w_ffn2_out, l, seq, tm)
    return xt.reshape(batch, seq, D_MODEL)
```

```python
import functools

import jax
import jax.numpy as jnp
from jax import lax
from jax.experimental import pallas as pl
from jax.experimental.pallas import tpu as pltpu

D_MODEL = 1024
D_FF = 2816
CONV_WIDTH = 512
HEADS = 4
DK = 64
DV = 128
CHUNK = 128
EPS = 1e-6
NEG_INF = -1e30

FF_TILE = 256
FF_STAGE_CHUNKS = 8
FF_SIDE_CHUNKS = 8
MIX_OUT_PIECES = 4
FF_OUT_PIECES = 2
FF_LEAD_TILES = 1
MIX_STAGE_CHUNKS = 8
MLSTM_ROWS = 2048
SCAN_ROWS = 2048
DVA = DV + 16
HALO_ROWS = 8
NGR = 2 * HEADS
CT_ROWS = (HEADS // 2) * DVA
VMEM_LIMIT = 56 * 1024 * 1024

P_E, P_M, P_B, P_G, P_MC = (slice(i * NGR, (i + 1) * NGR) for i in range(5))

F32 = jnp.float32
BF16 = jnp.bfloat16


def _rms(x, g):
    return x * lax.rsqrt(jnp.mean(x * x, axis=-1, keepdims=True) + EPS) * g


def _log_sigmoid(z):
    return jnp.minimum(z, 0.0) - jnp.log1p(jnp.exp(-jnp.abs(z)))


def _resident(shape):
    zeros = (0,) * len(shape)
    return pl.BlockSpec(shape, lambda *_: zeros, pipeline_mode=pl.Buffered(1))


def _split3(x):
    hi = x.astype(BF16)
    r1 = x - hi.astype(F32)
    mid = r1.astype(BF16)
    lo = (r1 - mid.astype(F32)).astype(BF16)
    return hi, mid, lo


def _visible(rev):
    s = lax.broadcasted_iota(jnp.int32, (CHUNK, CHUNK), 0)
    t = lax.broadcasted_iota(jnp.int32, (CHUNK, CHUNK), 1)
    return (s >= t) if rev else (s <= t)


def _running_max(x, rev):
    n = x.shape[1]
    pos = lax.broadcasted_iota(jnp.int32, x.shape, 1) & (CHUNK - 1)
    k = 1
    while k < CHUNK:
        if rev:
            shifted, ok = pltpu.roll(x, n - k, 1), pos < CHUNK - k
        else:
            shifted, ok = pltpu.roll(x, k, 1), pos >= k
        x = jnp.maximum(x, jnp.where(ok, shifted, NEG_INF))
        k *= 2
    return x


def _stage_bf16(src_hbm, dst_ref, stage_ref, sem, rows=None):
    rows = stage_ref.shape[1] if rows is None else rows
    n_chunks = src_hbm.shape[0] // rows

    def copy(c):
        return pltpu.make_async_copy(src_hbm.at[pl.ds(c * rows, rows), :],
                                     stage_ref.at[c % 2, pl.ds(0, rows), :], sem.at[c % 2])

    copy(0).start()
    for c in range(n_chunks):
        if c + 1 < n_chunks:
            copy(c + 1).start()
        copy(c).wait()
        dst_ref[c * rows:(c + 1) * rows, :] = stage_ref[c % 2, 0:rows, :].astype(BF16)


def _mix_out_tile(tiles_per_seq, x_ref, bg_ref, u_ref, uprev_ref, unext_ref, cw_ref, cb_ref,
                  y_ref, wmo_ref, gmix_ref, gpre_ref):
    i = pl.program_id(0)
    tm = u_ref.shape[0]
    u = u_ref[...]
    has_prev = (i % tiles_per_seq != 0).astype(F32)
    has_next = (i % tiles_per_seq != tiles_per_seq - 1).astype(F32)
    prev_row = uprev_ref[HALO_ROWS - 1:HALO_ROWS, :] * has_prev
    next_row = unext_ref[0:1, :] * has_next
    ri = lax.broadcasted_iota(jnp.int32, u.shape, 0)
    u_m1 = jnp.where(ri == 0, prev_row, pltpu.roll(u, 1, 0))
    u_p1 = jnp.where(ri == tm - 1, next_row, pltpu.roll(u, tm - 1, 0))
    conv = cw_ref[0:1, :] * u_m1 + cw_ref[1:2, :] * u + cw_ref[2:3, :] * u_p1
    y_conv = (bg_ref[...] * (conv + cb_ref[...])).astype(BF16)
    rp = tm // MIX_OUT_PIECES
    xs, xns = [], []
    for r in range(MIX_OUT_PIECES):
        rs = slice(r * rp, (r + 1) * rp)
        h = jnp.dot(y_ref[rs, :], wmo_ref[CONV_WIDTH:, :], preferred_element_type=F32) \
            + jnp.dot(y_conv[rs, :], wmo_ref[0:CONV_WIDTH, :], preferred_element_type=F32)
        xs.append(x_ref[rs, :] + _rms(h, gmix_ref[...]))
        xns.append(_rms(xs[-1], gpre_ref[...]).astype(BF16))
    return jnp.concatenate(xs, axis=0), xns


def _ffn_tile(x, xn_pieces, gpost_ref, win_ref, wout_ref, h_ref):
    rp = xn_pieces[0].shape[0]
    xn = jnp.concatenate(xn_pieces, axis=0)
    for j in range(D_FF // FF_TILE):
        lo = j * FF_TILE
        lhs = list(enumerate(xn_pieces)) if j < FF_LEAD_TILES and len(xn_pieces) > 1 else [(None, xn)]
        for r, xr in lhs:
            rs = slice(None) if r is None else slice(r * rp, (r + 1) * rp)
            gate = jnp.dot(xr, win_ref[:, lo:lo + FF_TILE], preferred_element_type=F32)
            up = jnp.dot(xr, win_ref[:, D_FF + lo:D_FF + lo + FF_TILE], preferred_element_type=F32)
            h_ref[rs, lo:lo + FF_TILE] = (gate * jax.nn.sigmoid(gate) * up).astype(BF16)
    g_half = 0.5 * gpost_ref[...]
    op = x.shape[0] // FF_OUT_PIECES
    outs = []
    for r in range(FF_OUT_PIECES):
        rs = slice(r * op, (r + 1) * op)
        y = jnp.dot(h_ref[rs, :], wout_ref[...], preferred_element_type=F32)
        outs.append(x[rs, :] + _rms(y, g_half))
    return jnp.concatenate(outs, axis=0)


def _zero_after(v):
    u = pltpu.bitcast(v, jnp.uint32)
    acc = None
    for r in range(u.shape[0] // 8):
        for c in range(u.shape[1] // 128):
            t = u[r * 8:(r + 1) * 8, c * 128:(c + 1) * 128]
            acc = t if acc is None else acc | t
    return pltpu.bitcast((acc >> 16) >> 16, F32)


def _ffn_kernel(layer, n_tiles, xa_ref, xc_ref, gpre_ref, gpost_ref, win_hbm, wout_hbm, o_ref,
                h_ref, win_ref, wout_ref, stage_in, stage_out, sem, xn_ref, y_ref):
    i = pl.program_id(0)
    tm = xa_ref.shape[0]
    g_half = 0.5 * gpost_ref[...]

    @pl.when(i == 0)
    def _():
        _stage_bf16(win_hbm.at[layer], win_ref, stage_in, sem)
        _stage_bf16(wout_hbm.at[layer], wout_ref, stage_out, sem)
        xn_ref[0] = _rms(xa_ref[...], gpre_ref[...]).astype(BF16)
        y_ref[...] = jnp.zeros(y_ref.shape, F32)

    @pl.when((i >= 1) & (i <= n_tiles))
    def _():
        xn = xn_ref[(i - 1) % 2]
        rp = tm // FF_SIDE_CHUNKS
        for j in range(D_FF // FF_TILE):
            lo = j * FF_TILE
            gate = jnp.dot(xn, win_ref[:, lo:lo + FF_TILE], preferred_element_type=F32)
            up = jnp.dot(xn, win_ref[:, D_FF + lo:D_FF + lo + FF_TILE], preferred_element_type=F32)
            hm = gate * jax.nn.sigmoid(gate) * up
            h_ref[:, lo:lo + FF_TILE] = hm.astype(BF16)
            if j < FF_SIDE_CHUNKS:
                rs = slice(j * rp, (j + 1) * rp)
                out_rows = xc_ref[rs, :] + _rms(y_ref[rs, :], g_half)
                o_ref[rs, :] = out_rows
                xn_rows = _rms(xa_ref[rs, :], gpre_ref[...])
                xn_ref[i % 2, rs, :] = xn_rows.astype(BF16)
                zero = _zero_after(out_rows) + _zero_after(xn_rows)
                h_ref[0:8, lo:lo + 128] = (hm[0:8, 0:128] + zero).astype(BF16)
        y_ref[...] = jnp.dot(h_ref[...], wout_ref[...], preferred_element_type=F32)

    @pl.when(i == n_tiles + 1)
    def _():
        o_ref[...] = xc_ref[...] + _rms(y_ref[...], g_half)


def _mix_ffn_kernel(layer, tiles_per_seq, x_ref, bg_ref, u_ref, uprev_ref, unext_ref, cw_ref, cb_ref,
                    y_ref, gmix_ref, wmo_hbm, gpre_ref, gpost_ref, win_hbm, wout_hbm, o_ref,
                    h_ref, win_ref, wout_ref, wmo_ref, stage_in, stage_out, sem):
    @pl.when(pl.program_id(0) == 0)
    def _():
        _stage_bf16(wmo_hbm.at[layer], wmo_ref, stage_out, sem, rows=D_MODEL // 4)
        _stage_bf16(win_hbm.at[layer], win_ref, stage_in, sem)
        _stage_bf16(wout_hbm.at[layer], wout_ref, stage_out, sem)

    x, xn = _mix_out_tile(tiles_per_seq, x_ref, bg_ref, u_ref, uprev_ref, unext_ref, cw_ref, cb_ref,
                          y_ref, wmo_ref, gmix_ref, gpre_ref)
    o_ref[...] = _ffn_tile(x, xn, gpost_ref, win_ref, wout_ref, h_ref)


def _ffn_scratch(tm):
    return [
        pltpu.VMEM((tm, D_FF), BF16),
        pltpu.VMEM((D_MODEL, 2 * D_FF), BF16),
        pltpu.VMEM((D_FF, D_MODEL), BF16),
    ], [
        pltpu.VMEM((2, D_MODEL // FF_STAGE_CHUNKS, 2 * D_FF), F32),
        pltpu.VMEM((2, D_FF // FF_STAGE_CHUNKS, D_MODEL), F32),
        pltpu.SemaphoreType.DMA((2,)),
    ]


def _ffn(x, gpre, gpost, w_in, w_out, layer, tm):
    T = x.shape[0]
    nt = T // tm
    resident, staging = _ffn_scratch(tm)
    tile = lambda lag: pl.BlockSpec((tm, D_MODEL), lambda i: (jnp.clip(i - lag, 0, nt - 1), 0))
    return pl.pallas_call(
        functools.partial(_ffn_kernel, layer, nt),
        grid=(nt + 2,),
        in_specs=[
            tile(0), tile(2),
            _resident((1, D_MODEL)),
            _resident((1, D_MODEL)),
            pl.BlockSpec(memory_space=pl.ANY),
            pl.BlockSpec(memory_space=pl.ANY),
        ],
        out_specs=tile(2),
        out_shape=jax.ShapeDtypeStruct((T, D_MODEL), F32),
        scratch_shapes=resident + staging + [pltpu.VMEM((2, tm, D_MODEL), BF16),
                                             pltpu.VMEM((tm, D_MODEL), F32)],
        compiler_params=pltpu.CompilerParams(
            dimension_semantics=("arbitrary",), vmem_limit_bytes=VMEM_LIMIT),
        name="ffn",
    )(x, x, gpre, gpost, w_in, w_out)


def _mix_ffn(x, bg, u, conv_w, conv_b, y_mlstm, gmix, w_mix_out, gpre, gpost, w_in, w_out, layer, seq, tm):
    T = x.shape[0]
    tiles_per_seq = seq // tm
    sub = tm // HALO_ROWS
    last = T // HALO_ROWS - 1
    row = lambda w: pl.BlockSpec((tm, w), lambda i: (i, 0))
    hbm = pl.BlockSpec(memory_space=pl.ANY)
    resident, staging = _ffn_scratch(tm)
    return pl.pallas_call(
        functools.partial(_mix_ffn_kernel, layer, tiles_per_seq),
        grid=(T // tm,),
        in_specs=[
            row(D_MODEL), row(CONV_WIDTH), row(CONV_WIDTH),
            pl.BlockSpec((HALO_ROWS, CONV_WIDTH), lambda i: (jnp.maximum(i * sub - 1, 0), 0)),
            pl.BlockSpec((HALO_ROWS, CONV_WIDTH), lambda i: (jnp.minimum((i + 1) * sub, last), 0)),
            _resident((3, CONV_WIDTH)), _resident((1, CONV_WIDTH)),
            row(HEADS * DV), _resident((1, D_MODEL)), hbm,
            _resident((1, D_MODEL)), _resident((1, D_MODEL)), hbm, hbm,
        ],
        out_specs=row(D_MODEL),
        out_shape=jax.ShapeDtypeStruct((T, D_MODEL), F32),
        scratch_shapes=resident + [pltpu.VMEM((D_MODEL, D_MODEL), BF16)] + staging,
        compiler_params=pltpu.CompilerParams(
            dimension_semantics=("arbitrary",), vmem_limit_bytes=VMEM_LIMIT),
        name="mix_ffn",
    )(x, bg, u, u, u, conv_w, conv_b, y_mlstm, gmix, w_mix_out, gpre, gpost, w_in, w_out)


def _gate_rows(zr, p_ref, ccol_ref):
    L = CHUNK
    n_chunks = zr.shape[1] // L
    fwd_row = lax.broadcasted_iota(jnp.int32, (NGR, L), 0) < HEADS
    fwd_col = lax.broadcasted_iota(jnp.int32, (NGR, 1), 0) < HEADS
    li = zr[0:NGR]
    gates = jnp.concatenate([li, _log_sigmoid(zr[NGR:2 * NGR])], axis=0)
    x3 = jnp.concatenate(_split3(gates), axis=0)
    stacked = jnp.concatenate([x3[:, c * L:(c + 1) * L] for c in range(n_chunks)], axis=0)
    cum_ops = jnp.concatenate([_visible(False), _visible(True)], axis=1).astype(BF16)
    cum = jnp.dot(stacked, cum_ops, preferred_element_type=F32)
    b_chunks = []
    for c in range(n_chunks):
        blk = cum[c * 48:(c + 1) * 48]
        s16 = blk[0:16] + blk[16:32] + blk[32:48]
        b_chunks.append(jnp.where(fwd_row, s16[NGR:2 * NGR, 0:L], s16[NGR:2 * NGR, L:2 * L]))
    b = jnp.concatenate(b_chunks, axis=1)
    cc = li - b
    fwd_all = lax.broadcasted_iota(jnp.int32, cc.shape, 0) < HEADS
    p_ref[P_M, :] = jnp.where(fwd_all, _running_max(cc, False), _running_max(cc, True))
    p_ref[P_B, :] = b
    pad = jnp.zeros((L - NGR, L), F32)
    for c in range(n_chunks):
        sl = slice(c * L, (c + 1) * L)
        b_c, cc_c = b_chunks[c], cc[:, sl]
        g = jnp.where(fwd_col, b_c[:, L - 1:L], b_c[:, 0:1])
        m_chunk = g + jnp.max(cc_c, axis=1, keepdims=True)
        p_ref[P_E, sl] = jnp.exp(g + cc_c - m_chunk)
        p_ref[P_G, sl] = jnp.broadcast_to(g, (NGR, L))
        p_ref[P_MC, sl] = jnp.broadcast_to(m_chunk, (NGR, L))
        ccol_ref[sl, :] = jnp.concatenate([cc_c, pad], axis=0).T


def _mix_in_kernel(layer, x_ref, gpre_ref, w_hbm, gbias_ref,
                   bg_ref, u_ref, qt_ref, k_ref, ksw_ref, vt_ref, o_ref, p_ref, ccol_ref,
                   wb_ref, stage_ref, sem):
    W, HK, HV = CONV_WIDTH, HEADS * DK, HEADS * DV
    q0 = 3 * W
    k0, v0 = q0 + HK, q0 + 2 * HK
    o0 = v0 + HV
    g0 = o0 + HV

    @pl.when(pl.program_id(0) == 0)
    def _():
        src = w_hbm.at[layer]
        rows = stage_ref.shape[1]
        n_chunks = g0 // rows

        def copy(c):
            n = rows if c < n_chunks else 2 * NGR
            return pltpu.make_async_copy(src.at[pl.ds(c * rows, n), :], stage_ref.at[c % 2, pl.ds(0, n), :],
                                         sem.at[c % 2])

        copy(0).start()
        for c in range(n_chunks + 1):
            if c < n_chunks:
                copy(c + 1).start()
            copy(c).wait()
            if c < n_chunks:
                wb_ref[:, c * rows:(c + 1) * rows] = stage_ref[c % 2].T.astype(BF16)
            else:
                gt = stage_ref[c % 2, 0:128, :].T
                lane = lax.broadcasted_iota(jnp.int32, gt.shape, 1)
                wb_ref[:, g0:g0 + 128] = jnp.where(lane < 2 * NGR, gt, 0.0).astype(BF16)

    xn = _rms(x_ref[...], gpre_ref[...]).astype(BF16)
    proj = lambda a, b: jnp.dot(xn, wb_ref[:, a:b], preferred_element_type=F32)

    zg = proj(g0, g0 + 128) + gbias_ref[...]
    bg_ref[...] = proj(0, W)
    u_ref[...] = proj(W, 2 * W) * proj(2 * W, 3 * W)
    _gate_rows(zg.T[0:2 * NGR], p_ref, ccol_ref)
    qt_ref[...] = (proj(q0, k0) * (DK ** -0.5)).T.astype(BF16)
    kk = proj(k0, v0)
    k_ref[...] = kk.astype(BF16)
    for pair in range(HEADS // 2):
        ps = slice(pair * 2 * DK, (pair + 1) * 2 * DK)
        ksw_ref[:, ps] = pltpu.roll(kk[:, ps], DK, 1).astype(BF16)
    vt_ref[...] = proj(v0, o0).T.astype(BF16)
    o_ref[...] = proj(o0, g0)


def _mix_in(x, gpre, w, gbias, layer, tm):
    T = x.shape[0]
    row = lambda w: pl.BlockSpec((tm, w), lambda i: (i, 0))
    col = lambda h: pl.BlockSpec((h, tm), lambda i: (0, i))
    HK, HV = HEADS * DK, HEADS * DV
    g0 = 3 * CONV_WIDTH + 2 * HK + 2 * HV
    return pl.pallas_call(
        functools.partial(_mix_in_kernel, layer),
        grid=(T // tm,),
        in_specs=[row(D_MODEL), _resident((1, D_MODEL)), pl.BlockSpec(memory_space=pl.ANY),
                  _resident((1, 128))],
        scratch_shapes=[pltpu.VMEM((D_MODEL, g0 + 128), BF16),
                        pltpu.VMEM((2, g0 // MIX_STAGE_CHUNKS, D_MODEL), F32),
                        pltpu.SemaphoreType.DMA((2,))],
        out_specs=[row(CONV_WIDTH), row(CONV_WIDTH), col(HK), row(HK), row(HK), col(HV), row(HV),
                   col(5 * NGR), row(128)],
        out_shape=[
            jax.ShapeDtypeStruct((T, CONV_WIDTH), F32),
            jax.ShapeDtypeStruct((T, CONV_WIDTH), F32),
            jax.ShapeDtypeStruct((HK, T), BF16),
            jax.ShapeDtypeStruct((T, HK), BF16),
            jax.ShapeDtypeStruct((T, HK), BF16),
            jax.ShapeDtypeStruct((HV, T), BF16),
            jax.ShapeDtypeStruct((T, HV), F32),
            jax.ShapeDtypeStruct((5 * NGR, T), F32),
            jax.ShapeDtypeStruct((T, 128), F32),
        ],
        compiler_params=pltpu.CompilerParams(
            dimension_semantics=("arbitrary",), vmem_limit_bytes=VMEM_LIMIT),
        name="mix_in",
    )(x, gpre, w, gbias)


def _state_half(d, h):
    return (h % 2) ^ d


def _scan_direction(d, vt_ref, k_ref, p_ref, ct_ref, mp_ref, ct_state, m_state):
    L = CHUNK
    n_chunks = k_ref.shape[0] // L
    lane_half = lax.broadcasted_iota(jnp.int32, (L, 128), 1) // DK
    lane_half_s = lax.broadcasted_iota(jnp.int32, (DVA, 128), 1) // DK
    ones_rows = jnp.ones((DVA - DV, L), BF16)

    cts = [ct_state[d, h] for h in range(HEADS)]
    m_prev = m_state[d]
    for chunk in (range(n_chunks - 1, -1, -1) if d == 1 else range(n_chunks)):
        sl = slice(chunk * L, (chunk + 1) * L)
        e, g, m_chunk = p_ref[P_E, sl], p_ref[P_G, sl], p_ref[P_MC, sl]
        mp_ref[:, sl] = m_prev
        m_new = jnp.maximum(g + m_prev, m_chunk)
        a_old = jnp.exp(g + m_prev - m_new)
        a_new = jnp.exp(m_chunk - m_new)
        new_cts = []
        for h in range(HEADS):
            r = HEADS * d + h
            vt_aug = jnp.concatenate([vt_ref[h * DV:(h + 1) * DV, sl], ones_rows], axis=0)
            vte = (vt_aug.astype(F32) * e[r:r + 1, :]).astype(BF16)
            pair = slice((h // 2) * 2 * DK, (h // 2 + 1) * 2 * DK)
            k_half = jnp.where(lane_half == _state_half(d, h), k_ref[sl, pair], jnp.zeros((L, 128), BF16))
            ct_chunk = jnp.dot(vte, k_half, preferred_element_type=F32)
            new_cts.append(a_old[r:r + 1, 0:1] * cts[h] + a_new[r:r + 1, 0:1] * ct_chunk)
        for pr in range(HEADS // 2):
            both = jnp.where(lane_half_s == _state_half(d, 2 * pr), cts[2 * pr], cts[2 * pr + 1])
            ct_ref[chunk, pr * DVA:(pr + 1) * DVA, :] = both.astype(BF16)
        cts = new_cts
        m_prev = m_new
    for h in range(HEADS):
        ct_state[d, h] = cts[h]
    m_state[d] = m_prev


def _mlstm_scan_kernel(vt_f, k_f, p_f, vt_b, ksw_b, p_b, ct_f, mp_f, ct_b, mp_b, ct_state, m_state):
    @pl.when(pl.program_id(1) == 0)
    def _():
        ct_state[...] = jnp.zeros(ct_state.shape, F32)
        m_state[...] = jnp.full(m_state.shape, NEG_INF, F32)

    _scan_direction(0, vt_f, k_f, p_f, ct_f, mp_f, ct_state, m_state)
    _scan_direction(1, vt_b, ksw_b, p_b, ct_b, mp_b, ct_state, m_state)


def _mlstm_out_kernel(qt_ref, k_ref, vt_ref, p_ref, mp_f, mp_b, ccol_ref, ct_f, ct_b,
                      o_ref, gain_ref, y_ref):
    L = CHUNK
    n_chunks = k_ref.shape[0] // L
    visible = (_visible(False), _visible(True))
    fwd_row = lax.broadcasted_iota(jnp.int32, (NGR, L), 0) < HEADS
    lane_half = lax.broadcasted_iota(jnp.int32, (DVA, 128), 1) // DK
    ones_rows = jnp.ones((DVA - DV, L), BF16)
    zq = jnp.zeros((DK, L), BF16)
    for chunk in range(n_chunks):
        sl = slice(chunk * L, (chunk + 1) * L)
        m_prev = jnp.where(fwd_row, mp_f[:, sl], mp_b[:, sl])
        n_t = jnp.maximum(m_prev, p_ref[P_M, sl])
        f_inter = jnp.exp(m_prev - n_t)
        e_min = jnp.exp(-(p_ref[P_B, sl] + n_t))
        ccol = ccol_ref[sl, :]
        for h in range(HEADS):
            hs = slice(h * DV, (h + 1) * DV)
            qt = qt_ref[h * DK:(h + 1) * DK, sl]
            k_pair = k_ref[sl, (h // 2) * 2 * DK:(h // 2 + 1) * 2 * DK]
            qt_pair = jnp.concatenate([qt, zq] if h % 2 == 0 else [zq, qt], axis=0)
            st = jnp.dot(k_pair, qt_pair, preferred_element_type=F32)
            pts, qfs = [], []
            for d in range(2):
                r = HEADS * d + h
                arg = jnp.where(visible[d], ccol[:, r:r + 1] - n_t[r:r + 1, :], NEG_INF)
                pts.append((jnp.exp(arg) * st).astype(BF16))
                qfs.append((qt.astype(F32) * f_inter[r:r + 1, :]).astype(BF16))
            inter = [jnp.concatenate([qfs[0], zq], axis=1), jnp.concatenate([zq, qfs[1]], axis=1)]
            first = 0 if _state_half(0, h) == 0 else 1
            rhs = jnp.concatenate([jnp.concatenate(pts, axis=1), inter[first], inter[1 - first]], axis=0)
            vt_aug = jnp.concatenate([vt_ref[hs, sl], ones_rows], axis=0)
            ps = slice((h // 2) * DVA, (h // 2 + 1) * DVA)
            blocks = (ct_f[chunk, ps, :], ct_b[chunk, ps, :])
            ct = jnp.where(lane_half == 0, blocks[first], blocks[1 - first])
            both = jnp.dot(jnp.concatenate([vt_aug, ct], axis=1), rhs, preferred_element_type=F32)
            ht = None
            for d in range(2):
                r = HEADS * d + h
                numer = both[0:DV, d * L:(d + 1) * L]
                denom = both[DV:DV + 1, d * L:(d + 1) * L]
                part = numer / jnp.maximum(jnp.abs(denom), e_min[r:r + 1, :])
                ht = part if ht is None else ht + part
            ms = jnp.mean(ht * ht, axis=0, keepdims=True)
            hn = ht * lax.rsqrt(ms + EPS) * gain_ref[hs, :]
            y_ref[sl, hs] = (jax.nn.sigmoid(o_ref[sl, hs]) * hn.T).astype(BF16)


def _mlstm(qt, k, ksw, vt, p, ccol, o, gain_b, batch, seq, scan_rows, rows):
    T = batch * seq
    ng = seq // scan_rows
    HK, HV = HEADS * DK, HEADS * DV

    def scan_specs(group_of):
        blk = lambda b, j: b * ng + group_of(j)
        ins = [pl.BlockSpec((HV, scan_rows), lambda b, j: (0, blk(b, j))),
               pl.BlockSpec((scan_rows, HK), lambda b, j: (blk(b, j), 0)),
               pl.BlockSpec((5 * NGR, scan_rows), lambda b, j: (0, blk(b, j)))]
        outs = [pl.BlockSpec((scan_rows // CHUNK, CT_ROWS, 128), lambda b, j: (blk(b, j), 0, 0)),
                pl.BlockSpec((NGR, scan_rows), lambda b, j: (0, blk(b, j)))]
        return ins, outs

    ins_f, outs_f = scan_specs(lambda j: j)
    ins_b, outs_b = scan_specs(lambda j: ng - 1 - j)
    state_shapes = [jax.ShapeDtypeStruct((T // CHUNK, CT_ROWS, 128), BF16),
                    jax.ShapeDtypeStruct((NGR, T), F32)]
    ct_f, mp_f, ct_b, mp_b = pl.pallas_call(
        _mlstm_scan_kernel,
        grid=(batch, ng),
        in_specs=ins_f + ins_b,
        out_specs=outs_f + outs_b,
        out_shape=state_shapes + state_shapes,
        scratch_shapes=[pltpu.VMEM((2, HEADS, DVA, 128), F32), pltpu.VMEM((2, NGR, 128), F32)],
        compiler_params=pltpu.CompilerParams(dimension_semantics=("arbitrary", "arbitrary"),
                                             vmem_limit_bytes=VMEM_LIMIT),
        name="mlstm_scan",
    )(vt, k, p, vt, ksw, p)

    row = lambda w: pl.BlockSpec((rows, w), lambda i: (i, 0))
    col = lambda h: pl.BlockSpec((h, rows), lambda i: (0, i))
    ctb = pl.BlockSpec((rows // CHUNK, CT_ROWS, 128), lambda i: (i, 0, 0))
    return pl.pallas_call(
        _mlstm_out_kernel,
        grid=(T // rows,),
        in_specs=[col(HK), row(HK), col(HV), col(5 * NGR), col(NGR), col(NGR), row(128), ctb, ctb,
                  row(HV), _resident((HV, 128))],
        out_specs=row(HV),
        out_shape=jax.ShapeDtypeStruct((T, HV), BF16),
        compiler_params=pltpu.CompilerParams(dimension_semantics=("arbitrary",),
                                             vmem_limit_bytes=VMEM_LIMIT),
        name="mlstm_out",
    )(qt, k, vt, p, mp_f, mp_b, ccol, ct_f, ct_b, o, gain_b)


def kernel(x, norm_ffn1_pre, norm_ffn1_post, w_ffn1_in, w_ffn1_out, norm_mix_pre, norm_mix_post,
           w_mix_in, conv_w, conv_b, gate_i_bias, gate_f_bias, mlstm_norm, w_mix_out,
           norm_ffn2_pre, norm_ffn2_post, w_ffn2_in, w_ffn2_out):
    batch, seq, _ = x.shape
    T = batch * seq
    depth = norm_ffn1_pre.shape[0]
    tm = 512
    xt = x.reshape(T, D_MODEL)
    HV = HEADS * DV
    for l in range(depth):
        xt = _ffn(xt, norm_ffn1_pre[l][None], norm_ffn1_post[l][None], w_ffn1_in, w_ffn1_out, l, tm)

        gbias = jnp.pad(jnp.concatenate([gate_i_bias[l], gate_f_bias[l]]), (0, 128 - 2 * NGR))[None]
        bg, u, qt, k, ksw, vt, o, p, ccol = _mix_in(
            xt, norm_mix_pre[l][None], jnp.swapaxes(w_mix_in, 1, 2), gbias, l, 2 * tm)
        gain_b = jnp.broadcast_to(mlstm_norm[l][:, None], (HV, 128))
        y_mlstm = _mlstm(qt, k, ksw, vt, p, ccol, o, gain_b, batch, seq, SCAN_ROWS, MLSTM_ROWS)
        xt = _mix_ffn(xt, bg, u, conv_w[l], conv_b[l][None], y_mlstm, norm_mix_post[l][None], w_mix_out,
                      norm_ffn2_pre[l][None], norm_ffn2_post[l][None], w_ffn2_in, w_ffn2_out, l, seq, tm)
    return xt.reshape(batch, seq, D_MODEL)
```

```python
import functools

import jax
import jax.numpy as jnp
from jax import lax
from jax.experimental import pallas as pl
from jax.experimental.pallas import tpu as pltpu

D_MODEL = 1024
D_FF = 2816
CONV_WIDTH = 512
HEADS = 4
DK = 64
DV = 128
CHUNK = 128
EPS = 1e-6
NEG_INF = -1e30

FF_TILE = 256
FF_STAGE_CHUNKS = 8
FF_SIDE_CHUNKS = 8
MIX_OUT_PIECES = 4
FF_LEAD_TILES = 1
MIX_STAGE_CHUNKS = 8
MLSTM_ROWS = 2048
SCAN_ROWS = 4096
DVA = DV + 16
HALO_ROWS = 8
NGR = 2 * HEADS
CT_ROWS = (HEADS // 2) * DVA
VMEM_LIMIT = 56 * 1024 * 1024

P_E, P_M, P_B, P_G, P_MC = (slice(i * NGR, (i + 1) * NGR) for i in range(5))

F32 = jnp.float32
BF16 = jnp.bfloat16


def _rms(x, g):
    return x * lax.rsqrt(jnp.mean(x * x, axis=-1, keepdims=True) + EPS) * g


def _log_sigmoid(z):
    return jnp.minimum(z, 0.0) - jnp.log1p(jnp.exp(-jnp.abs(z)))


def _resident(shape):
    zeros = (0,) * len(shape)
    return pl.BlockSpec(shape, lambda *_: zeros, pipeline_mode=pl.Buffered(1))


def _split3(x):
    hi = x.astype(BF16)
    r1 = x - hi.astype(F32)
    mid = r1.astype(BF16)
    lo = (r1 - mid.astype(F32)).astype(BF16)
    return hi, mid, lo


def _visible(rev):
    s = lax.broadcasted_iota(jnp.int32, (CHUNK, CHUNK), 0)
    t = lax.broadcasted_iota(jnp.int32, (CHUNK, CHUNK), 1)
    return (s >= t) if rev else (s <= t)


def _running_max(x, rev):
    n = x.shape[1]
    pos = lax.broadcasted_iota(jnp.int32, x.shape, 1) & (CHUNK - 1)
    k = 1
    while k < CHUNK:
        if rev:
            shifted, ok = pltpu.roll(x, n - k, 1), pos < CHUNK - k
        else:
            shifted, ok = pltpu.roll(x, k, 1), pos >= k
        x = jnp.maximum(x, jnp.where(ok, shifted, NEG_INF))
        k *= 2
    return x


def _stage_bf16(src_hbm, dst_ref, stage_ref, sem, rows=None):
    rows = stage_ref.shape[1] if rows is None else rows
    n_chunks = src_hbm.shape[0] // rows

    def copy(c):
        return pltpu.make_async_copy(src_hbm.at[pl.ds(c * rows, rows), :],
                                     stage_ref.at[c % 2, pl.ds(0, rows), :], sem.at[c % 2])

    copy(0).start()
    for c in range(n_chunks):
        if c + 1 < n_chunks:
            copy(c + 1).start()
        copy(c).wait()
        dst_ref[c * rows:(c + 1) * rows, :] = stage_ref[c % 2, 0:rows, :].astype(BF16)


def _mix_out_tile(tiles_per_seq, x_ref, bg_ref, u_ref, uprev_ref, unext_ref, cw_ref, cb_ref,
                  y_ref, wmo_ref, gmix_ref, gpre_ref):
    i = pl.program_id(0)
    tm = u_ref.shape[0]
    u = u_ref[...]
    has_prev = (i % tiles_per_seq != 0).astype(F32)
    has_next = (i % tiles_per_seq != tiles_per_seq - 1).astype(F32)
    prev_row = uprev_ref[HALO_ROWS - 1:HALO_ROWS, :] * has_prev
    next_row = unext_ref[0:1, :] * has_next
    ri = lax.broadcasted_iota(jnp.int32, u.shape, 0)
    u_m1 = jnp.where(ri == 0, prev_row, pltpu.roll(u, 1, 0))
    u_p1 = jnp.where(ri == tm - 1, next_row, pltpu.roll(u, tm - 1, 0))
    conv = cw_ref[0:1, :] * u_m1 + cw_ref[1:2, :] * u + cw_ref[2:3, :] * u_p1
    y_conv = (bg_ref[...] * (conv + cb_ref[...])).astype(BF16)
    rp = tm // MIX_OUT_PIECES
    xs, xns = [], []
    for r in range(MIX_OUT_PIECES):
        rs = slice(r * rp, (r + 1) * rp)
        h = jnp.dot(y_ref[rs, :], wmo_ref[CONV_WIDTH:, :], preferred_element_type=F32) \
            + jnp.dot(y_conv[rs, :], wmo_ref[0:CONV_WIDTH, :], preferred_element_type=F32)
        xs.append(x_ref[rs, :] + _rms(h, gmix_ref[...]))
        xns.append(_rms(xs[-1], gpre_ref[...]).astype(BF16))
    return jnp.concatenate(xs, axis=0), xns


def _ffn_tile(x, xn_pieces, gpost_ref, win_ref, wout_ref, h_ref):
    rp = xn_pieces[0].shape[0]
    xn = jnp.concatenate(xn_pieces, axis=0)
    for j in range(D_FF // FF_TILE):
        lo = j * FF_TILE
        lhs = list(enumerate(xn_pieces)) if j < FF_LEAD_TILES and len(xn_pieces) > 1 else [(None, xn)]
        for r, xr in lhs:
            rs = slice(None) if r is None else slice(r * rp, (r + 1) * rp)
            gate = jnp.dot(xr, win_ref[:, lo:lo + FF_TILE], preferred_element_type=F32)
            up = jnp.dot(xr, win_ref[:, D_FF + lo:D_FF + lo + FF_TILE], preferred_element_type=F32)
            h_ref[rs, lo:lo + FF_TILE] = (gate * jax.nn.sigmoid(gate) * up).astype(BF16)
    y = jnp.dot(h_ref[...], wout_ref[...], preferred_element_type=F32)
    return x + _rms(y, 0.5 * gpost_ref[...])


def _zero_after(v):
    u = pltpu.bitcast(v, jnp.uint32)
    acc = None
    for r in range(u.shape[0] // 8):
        for c in range(u.shape[1] // 128):
            t = u[r * 8:(r + 1) * 8, c * 128:(c + 1) * 128]
            acc = t if acc is None else acc | t
    return pltpu.bitcast((acc >> 16) >> 16, F32)


def _ffn_kernel(layer, n_tiles, xa_ref, xc_ref, gpre_ref, gpost_ref, win_hbm, wout_hbm, o_ref,
                h_ref, win_ref, wout_ref, stage_in, stage_out, sem, xn_ref, y_ref):
    i = pl.program_id(0)
    tm = xa_ref.shape[0]
    g_half = 0.5 * gpost_ref[...]

    @pl.when(i == 0)
    def _():
        _stage_bf16(win_hbm.at[layer], win_ref, stage_in, sem)
        _stage_bf16(wout_hbm.at[layer], wout_ref, stage_out, sem)
        xn_ref[0] = _rms(xa_ref[...], gpre_ref[...]).astype(BF16)
        y_ref[...] = jnp.zeros(y_ref.shape, F32)

    @pl.when((i >= 1) & (i <= n_tiles))
    def _():
        xn = xn_ref[(i - 1) % 2]
        rp = tm // FF_SIDE_CHUNKS
        for j in range(D_FF // FF_TILE):
            lo = j * FF_TILE
            gate = jnp.dot(xn, win_ref[:, lo:lo + FF_TILE], preferred_element_type=F32)
            up = jnp.dot(xn, win_ref[:, D_FF + lo:D_FF + lo + FF_TILE], preferred_element_type=F32)
            hm = gate * jax.nn.sigmoid(gate) * up
            h_ref[:, lo:lo + FF_TILE] = hm.astype(BF16)
            if j < FF_SIDE_CHUNKS:
                rs = slice(j * rp, (j + 1) * rp)
                out_rows = xc_ref[rs, :] + _rms(y_ref[rs, :], g_half)
                o_ref[rs, :] = out_rows
                xn_rows = _rms(xa_ref[rs, :], gpre_ref[...])
                xn_ref[i % 2, rs, :] = xn_rows.astype(BF16)
                zero = _zero_after(out_rows) + _zero_after(xn_rows)
                h_ref[0:8, lo:lo + 128] = (hm[0:8, 0:128] + zero).astype(BF16)
        y_ref[...] = jnp.dot(h_ref[...], wout_ref[...], preferred_element_type=F32)

    @pl.when(i == n_tiles + 1)
    def _():
        o_ref[...] = xc_ref[...] + _rms(y_ref[...], g_half)


def _mix_ffn_kernel(layer, tiles_per_seq, x_ref, bg_ref, u_ref, uprev_ref, unext_ref, cw_ref, cb_ref,
                    y_ref, gmix_ref, wmo_hbm, gpre_ref, gpost_ref, win_hbm, wout_hbm, o_ref,
                    h_ref, win_ref, wout_ref, wmo_ref, stage_in, stage_out, sem):
    @pl.when(pl.program_id(0) == 0)
    def _():
        _stage_bf16(wmo_hbm.at[layer], wmo_ref, stage_out, sem, rows=D_MODEL // 4)
        _stage_bf16(win_hbm.at[layer], win_ref, stage_in, sem)
        _stage_bf16(wout_hbm.at[layer], wout_ref, stage_out, sem)

    x, xn = _mix_out_tile(tiles_per_seq, x_ref, bg_ref, u_ref, uprev_ref, unext_ref, cw_ref, cb_ref,
                          y_ref, wmo_ref, gmix_ref, gpre_ref)
    o_ref[...] = _ffn_tile(x, xn, gpost_ref, win_ref, wout_ref, h_ref)


def _ffn_scratch(tm):
    return [
        pltpu.VMEM((tm, D_FF), BF16),
        pltpu.VMEM((D_MODEL, 2 * D_FF), BF16),
        pltpu.VMEM((D_FF, D_MODEL), BF16),
    ], [
        pltpu.VMEM((2, D_MODEL // FF_STAGE_CHUNKS, 2 * D_FF), F32),
        pltpu.VMEM((2, D_FF // FF_STAGE_CHUNKS, D_MODEL), F32),
        pltpu.SemaphoreType.DMA((2,)),
    ]


def _ffn(x, gpre, gpost, w_in, w_out, layer, tm):
    T = x.shape[0]
    nt = T // tm
    resident, staging = _ffn_scratch(tm)
    tile = lambda lag: pl.BlockSpec((tm, D_MODEL), lambda i: (jnp.clip(i - lag, 0, nt - 1), 0))
    return pl.pallas_call(
        functools.partial(_ffn_kernel, layer, nt),
        grid=(nt + 2,),
        in_specs=[
            tile(0), tile(2),
            _resident((1, D_MODEL)),
            _resident((1, D_MODEL)),
            pl.BlockSpec(memory_space=pl.ANY),
            pl.BlockSpec(memory_space=pl.ANY),
        ],
        out_specs=tile(2),
        out_shape=jax.ShapeDtypeStruct((T, D_MODEL), F32),
        scratch_shapes=resident + staging + [pltpu.VMEM((2, tm, D_MODEL), BF16),
                                             pltpu.VMEM((tm, D_MODEL), F32)],
        compiler_params=pltpu.CompilerParams(
            dimension_semantics=("arbitrary",), vmem_limit_bytes=VMEM_LIMIT),
        name="ffn",
    )(x, x, gpre, gpost, w_in, w_out)


def _mix_ffn(x, bg, u, conv_w, conv_b, y_mlstm, gmix, w_mix_out, gpre, gpost, w_in, w_out, layer, seq, tm):
    T = x.shape[0]
    tiles_per_seq = seq // tm
    sub = tm // HALO_ROWS
    last = T // HALO_ROWS - 1
    row = lambda w: pl.BlockSpec((tm, w), lambda i: (i, 0))
    hbm = pl.BlockSpec(memory_space=pl.ANY)
    resident, staging = _ffn_scratch(tm)
    return pl.pallas_call(
        functools.partial(_mix_ffn_kernel, layer, tiles_per_seq),
        grid=(T // tm,),
        in_specs=[
            row(D_MODEL), row(CONV_WIDTH), row(CONV_WIDTH),
            pl.BlockSpec((HALO_ROWS, CONV_WIDTH), lambda i: (jnp.maximum(i * sub - 1, 0), 0)),
            pl.BlockSpec((HALO_ROWS, CONV_WIDTH), lambda i: (jnp.minimum((i + 1) * sub, last), 0)),
            _resident((3, CONV_WIDTH)), _resident((1, CONV_WIDTH)),
            row(HEADS * DV), _resident((1, D_MODEL)), hbm,
            _resident((1, D_MODEL)), _resident((1, D_MODEL)), hbm, hbm,
        ],
        out_specs=row(D_MODEL),
        out_shape=jax.ShapeDtypeStruct((T, D_MODEL), F32),
        scratch_shapes=resident + [pltpu.VMEM((D_MODEL, D_MODEL), BF16)] + staging,
        compiler_params=pltpu.CompilerParams(
            dimension_semantics=("arbitrary",), vmem_limit_bytes=VMEM_LIMIT),
        name="mix_ffn",
    )(x, bg, u, u, u, conv_w, conv_b, y_mlstm, gmix, w_mix_out, gpre, gpost, w_in, w_out)


def _gate_rows(zr, p_ref, ccol_ref):
    L = CHUNK
    n_chunks = zr.shape[1] // L
    fwd_row = lax.broadcasted_iota(jnp.int32, (NGR, L), 0) < HEADS
    fwd_col = lax.broadcasted_iota(jnp.int32, (NGR, 1), 0) < HEADS
    li = zr[0:NGR]
    gates = jnp.concatenate([li, _log_sigmoid(zr[NGR:2 * NGR])], axis=0)
    x3 = jnp.concatenate(_split3(gates), axis=0)
    stacked = jnp.concatenate([x3[:, c * L:(c + 1) * L] for c in range(n_chunks)], axis=0)
    cum_ops = jnp.concatenate([_visible(False), _visible(True)], axis=1).astype(BF16)
    cum = jnp.dot(stacked, cum_ops, preferred_element_type=F32)
    b_chunks = []
    for c in range(n_chunks):
        blk = cum[c * 48:(c + 1) * 48]
        s16 = blk[0:16] + blk[16:32] + blk[32:48]
        b_chunks.append(jnp.where(fwd_row, s16[NGR:2 * NGR, 0:L], s16[NGR:2 * NGR, L:2 * L]))
    b = jnp.concatenate(b_chunks, axis=1)
    cc = li - b
    fwd_all = lax.broadcasted_iota(jnp.int32, cc.shape, 0) < HEADS
    p_ref[P_M, :] = jnp.where(fwd_all, _running_max(cc, False), _running_max(cc, True))
    p_ref[P_B, :] = b
    pad = jnp.zeros((L - NGR, L), F32)
    for c in range(n_chunks):
        sl = slice(c * L, (c + 1) * L)
        b_c, cc_c = b_chunks[c], cc[:, sl]
        g = jnp.where(fwd_col, b_c[:, L - 1:L], b_c[:, 0:1])
        m_chunk = g + jnp.max(cc_c, axis=1, keepdims=True)
        p_ref[P_E, sl] = jnp.exp(g + cc_c - m_chunk)
        p_ref[P_G, sl] = jnp.broadcast_to(g, (NGR, L))
        p_ref[P_MC, sl] = jnp.broadcast_to(m_chunk, (NGR, L))
        ccol_ref[sl, :] = jnp.concatenate([cc_c, pad], axis=0).T


def _mix_in_kernel(layer, x_ref, gpre_ref, w_hbm, gbias_ref,
                   bg_ref, u_ref, qt_ref, k_ref, ksw_ref, vt_ref, o_ref, p_ref, ccol_ref,
                   wb_ref, stage_ref, sem):
    W, HK, HV = CONV_WIDTH, HEADS * DK, HEADS * DV
    q0 = 3 * W
    k0, v0 = q0 + HK, q0 + 2 * HK
    o0 = v0 + HV
    g0 = o0 + HV

    @pl.when(pl.program_id(0) == 0)
    def _():
        src = w_hbm.at[layer]
        rows = stage_ref.shape[1]
        n_chunks = g0 // rows

        def copy(c):
            n = rows if c < n_chunks else 2 * NGR
            return pltpu.make_async_copy(src.at[pl.ds(c * rows, n), :], stage_ref.at[c % 2, pl.ds(0, n), :],
                                         sem.at[c % 2])

        copy(0).start()
        for c in range(n_chunks + 1):
            if c < n_chunks:
                copy(c + 1).start()
            copy(c).wait()
            if c < n_chunks:
                wb_ref[:, c * rows:(c + 1) * rows] = stage_ref[c % 2].T.astype(BF16)
            else:
                gt = stage_ref[c % 2, 0:128, :].T
                lane = lax.broadcasted_iota(jnp.int32, gt.shape, 1)
                wb_ref[:, g0:g0 + 128] = jnp.where(lane < 2 * NGR, gt, 0.0).astype(BF16)

    xn = _rms(x_ref[...], gpre_ref[...]).astype(BF16)
    proj = lambda a, b: jnp.dot(xn, wb_ref[:, a:b], preferred_element_type=F32)

    zg = proj(g0, g0 + 128) + gbias_ref[...]
    bg_ref[...] = proj(0, W)
    u_ref[...] = proj(W, 2 * W) * proj(2 * W, 3 * W)
    _gate_rows(zg.T[0:2 * NGR], p_ref, ccol_ref)
    qt_ref[...] = (proj(q0, k0) * (DK ** -0.5)).T.astype(BF16)
    kk = proj(k0, v0)
    k_ref[...] = kk.astype(BF16)
    for pair in range(HEADS // 2):
        ps = slice(pair * 2 * DK, (pair + 1) * 2 * DK)
        ksw_ref[:, ps] = pltpu.roll(kk[:, ps], DK, 1).astype(BF16)
    vt_ref[...] = proj(v0, o0).T.astype(BF16)
    o_ref[...] = proj(o0, g0)


def _mix_in(x, gpre, w, gbias, layer, tm):
    T = x.shape[0]
    row = lambda w: pl.BlockSpec((tm, w), lambda i: (i, 0))
    col = lambda h: pl.BlockSpec((h, tm), lambda i: (0, i))
    HK, HV = HEADS * DK, HEADS * DV
    g0 = 3 * CONV_WIDTH + 2 * HK + 2 * HV
    return pl.pallas_call(
        functools.partial(_mix_in_kernel, layer),
        grid=(T // tm,),
        in_specs=[row(D_MODEL), _resident((1, D_MODEL)), pl.BlockSpec(memory_space=pl.ANY),
                  _resident((1, 128))],
        scratch_shapes=[pltpu.VMEM((D_MODEL, g0 + 128), BF16),
                        pltpu.VMEM((2, g0 // MIX_STAGE_CHUNKS, D_MODEL), F32),
                        pltpu.SemaphoreType.DMA((2,))],
        out_specs=[row(CONV_WIDTH), row(CONV_WIDTH), col(HK), row(HK), row(HK), col(HV), row(HV),
                   col(5 * NGR), row(128)],
        out_shape=[
            jax.ShapeDtypeStruct((T, CONV_WIDTH), F32),
            jax.ShapeDtypeStruct((T, CONV_WIDTH), F32),
            jax.ShapeDtypeStruct((HK, T), BF16),
            jax.ShapeDtypeStruct((T, HK), BF16),
            jax.ShapeDtypeStruct((T, HK), BF16),
            jax.ShapeDtypeStruct((HV, T), BF16),
            jax.ShapeDtypeStruct((T, HV), F32),
            jax.ShapeDtypeStruct((5 * NGR, T), F32),
            jax.ShapeDtypeStruct((T, 128), F32),
        ],
        compiler_params=pltpu.CompilerParams(
            dimension_semantics=("arbitrary",), vmem_limit_bytes=VMEM_LIMIT),
        name="mix_in",
    )(x, gpre, w, gbias)


def _state_half(d, h):
    return (h % 2) ^ d


def _scan_direction(d, vt_ref, k_ref, p_ref, ct_ref, mp_ref, ct_state, m_state):
    L = CHUNK
    n_chunks = k_ref.shape[0] // L
    lane_half = lax.broadcasted_iota(jnp.int32, (L, 128), 1) // DK
    lane_half_s = lax.broadcasted_iota(jnp.int32, (DVA, 128), 1) // DK
    ones_rows = jnp.ones((DVA - DV, L), BF16)

    cts = [ct_state[d, h] for h in range(HEADS)]
    m_prev = m_state[d]
    for chunk in (range(n_chunks - 1, -1, -1) if d == 1 else range(n_chunks)):
        sl = slice(chunk * L, (chunk + 1) * L)
        e, g, m_chunk = p_ref[P_E, sl], p_ref[P_G, sl], p_ref[P_MC, sl]
        mp_ref[:, sl] = m_prev
        m_new = jnp.maximum(g + m_prev, m_chunk)
        a_old = jnp.exp(g + m_prev - m_new)
        a_new = jnp.exp(m_chunk - m_new)
        new_cts = []
        for h in range(HEADS):
            r = HEADS * d + h
            vt_aug = jnp.concatenate([vt_ref[h * DV:(h + 1) * DV, sl], ones_rows], axis=0)
            vte = (vt_aug.astype(F32) * e[r:r + 1, :]).astype(BF16)
            pair = slice((h // 2) * 2 * DK, (h // 2 + 1) * 2 * DK)
            k_half = jnp.where(lane_half == _state_half(d, h), k_ref[sl, pair], jnp.zeros((L, 128), BF16))
            ct_chunk = jnp.dot(vte, k_half, preferred_element_type=F32)
            new_cts.append(a_old[r:r + 1, 0:1] * cts[h] + a_new[r:r + 1, 0:1] * ct_chunk)
        for pr in range(HEADS // 2):
            both = jnp.where(lane_half_s == _state_half(d, 2 * pr), cts[2 * pr], cts[2 * pr + 1])
            ct_ref[chunk, pr * DVA:(pr + 1) * DVA, :] = both.astype(BF16)
        cts = new_cts
        m_prev = m_new
    for h in range(HEADS):
        ct_state[d, h] = cts[h]
    m_state[d] = m_prev


def _mlstm_scan_kernel(vt_f, k_f, p_f, vt_b, ksw_b, p_b, ct_f, mp_f, ct_b, mp_b, ct_state, m_state):
    @pl.when(pl.program_id(1) == 0)
    def _():
        ct_state[...] = jnp.zeros(ct_state.shape, F32)
        m_state[...] = jnp.full(m_state.shape, NEG_INF, F32)

    _scan_direction(0, vt_f, k_f, p_f, ct_f, mp_f, ct_state, m_state)
    _scan_direction(1, vt_b, ksw_b, p_b, ct_b, mp_b, ct_state, m_state)


def _mlstm_out_kernel(qt_ref, k_ref, vt_ref, p_ref, mp_f, mp_b, ccol_ref, ct_f, ct_b,
                      o_ref, gain_ref, y_ref):
    L = CHUNK
    n_chunks = k_ref.shape[0] // L
    visible = (_visible(False), _visible(True))
    fwd_row = lax.broadcasted_iota(jnp.int32, (NGR, L), 0) < HEADS
    lane_half = lax.broadcasted_iota(jnp.int32, (DVA, 128), 1) // DK
    ones_rows = jnp.ones((DVA - DV, L), BF16)
    zq = jnp.zeros((DK, L), BF16)
    for chunk in range(n_chunks):
        sl = slice(chunk * L, (chunk + 1) * L)
        m_prev = jnp.where(fwd_row, mp_f[:, sl], mp_b[:, sl])
        n_t = jnp.maximum(m_prev, p_ref[P_M, sl])
        f_inter = jnp.exp(m_prev - n_t)
        e_min = jnp.exp(-(p_ref[P_B, sl] + n_t))
        ccol = ccol_ref[sl, :]
        for h in range(HEADS):
            hs = slice(h * DV, (h + 1) * DV)
            qt = qt_ref[h * DK:(h + 1) * DK, sl]
            k_pair = k_ref[sl, (h // 2) * 2 * DK:(h // 2 + 1) * 2 * DK]
            qt_pair = jnp.concatenate([qt, zq] if h % 2 == 0 else [zq, qt], axis=0)
            st = jnp.dot(k_pair, qt_pair, preferred_element_type=F32)
            pts, qfs = [], []
            for d in range(2):
                r = HEADS * d + h
                arg = jnp.where(visible[d], ccol[:, r:r + 1] - n_t[r:r + 1, :], NEG_INF)
                pts.append((jnp.exp(arg) * st).astype(BF16))
                qfs.append((qt.astype(F32) * f_inter[r:r + 1, :]).astype(BF16))
            inter = [jnp.concatenate([qfs[0], zq], axis=1), jnp.concatenate([zq, qfs[1]], axis=1)]
            first = 0 if _state_half(0, h) == 0 else 1
            rhs = jnp.concatenate([jnp.concatenate(pts, axis=1), inter[first], inter[1 - first]], axis=0)
            vt_aug = jnp.concatenate([vt_ref[hs, sl], ones_rows], axis=0)
            ps = slice((h // 2) * DVA, (h // 2 + 1) * DVA)
            blocks = (ct_f[chunk, ps, :], ct_b[chunk, ps, :])
            ct = jnp.where(lane_half == 0, blocks[first], blocks[1 - first])
            both = jnp.dot(jnp.concatenate([vt_aug, ct], axis=1), rhs, preferred_element_type=F32)
            ht = None
            for d in range(2):
                r = HEADS * d + h
                numer = both[0:DV, d * L:(d + 1) * L]
                denom = both[DV:DV + 1, d * L:(d + 1) * L]
                part = numer / jnp.maximum(jnp.abs(denom), e_min[r:r + 1, :])
                ht = part if ht is None else ht + part
            ms = jnp.mean(ht * ht, axis=0, keepdims=True)
            hn = ht * lax.rsqrt(ms + EPS) * gain_ref[hs, :]
            y_ref[sl, hs] = (jax.nn.sigmoid(o_ref[sl, hs]) * hn.T).astype(BF16)


def _mlstm(qt, k, ksw, vt, p, ccol, o, gain_b, batch, seq, scan_rows, rows):
    T = batch * seq
    ng = seq // scan_rows
    HK, HV = HEADS * DK, HEADS * DV

    def scan_specs(group_of):
        blk = lambda b, j: b * ng + group_of(j)
        ins = [pl.BlockSpec((HV, scan_rows), lambda b, j: (0, blk(b, j))),
               pl.BlockSpec((scan_rows, HK), lambda b, j: (blk(b, j), 0)),
               pl.BlockSpec((5 * NGR, scan_rows), lambda b, j: (0, blk(b, j)))]
        outs = [pl.BlockSpec((scan_rows // CHUNK, CT_ROWS, 128), lambda b, j: (blk(b, j), 0, 0)),
                pl.BlockSpec((NGR, scan_rows), lambda b, j: (0, blk(b, j)))]
        return ins, outs

    ins_f, outs_f = scan_specs(lambda j: j)
    ins_b, outs_b = scan_specs(lambda j: ng - 1 - j)
    state_shapes = [jax.ShapeDtypeStruct((T // CHUNK, CT_ROWS, 128), BF16),
                    jax.ShapeDtypeStruct((NGR, T), F32)]
    if ng == 1:
        scan_kernel = lambda vt_r, k_r, ksw_r, p_r, *rest: _mlstm_scan_kernel(vt_r, k_r, p_r, vt_r, ksw_r, p_r, *rest)
        scan_in_specs, scan_args = [ins_f[0], ins_f[1], ins_f[1], ins_f[2]], (vt, k, ksw, p)
    else:
        scan_kernel, scan_in_specs, scan_args = _mlstm_scan_kernel, ins_f + ins_b, (vt, k, p, vt, ksw, p)
    ct_f, mp_f, ct_b, mp_b = pl.pallas_call(
        scan_kernel,
        grid=(batch, ng),
        in_specs=scan_in_specs,
        out_specs=outs_f + outs_b,
        out_shape=state_shapes + state_shapes,
        scratch_shapes=[pltpu.VMEM((2, HEADS, DVA, 128), F32), pltpu.VMEM((2, NGR, 128), F32)],
        compiler_params=pltpu.CompilerParams(dimension_semantics=("arbitrary", "arbitrary"),
                                             vmem_limit_bytes=VMEM_LIMIT),
        name="mlstm_scan",
    )(*scan_args)

    row = lambda w: pl.BlockSpec((rows, w), lambda i: (i, 0))
    col = lambda h: pl.BlockSpec((h, rows), lambda i: (0, i))
    ctb = pl.BlockSpec((rows // CHUNK, CT_ROWS, 128), lambda i: (i, 0, 0))
    return pl.pallas_call(
        _mlstm_out_kernel,
        grid=(T // rows,),
        in_specs=[col(HK), row(HK), col(HV), col(5 * NGR), col(NGR), col(NGR), row(128), ctb, ctb,
                  row(HV), _resident((HV, 128))],
        out_specs=row(HV),
        out_shape=jax.ShapeDtypeStruct((T, HV), BF16),
        compiler_params=pltpu.CompilerParams(dimension_semantics=("arbitrary",),
                                             vmem_limit_bytes=VMEM_LIMIT),
        name="mlstm_out",
    )(qt, k, vt, p, mp_f, mp_b, ccol, ct_f, ct_b, o, gain_b)


def kernel(x, norm_ffn1_pre, norm_ffn1_post, w_ffn1_in, w_ffn1_out, norm_mix_pre, norm_mix_post,
           w_mix_in, conv_w, conv_b, gate_i_bias, gate_f_bias, mlstm_norm, w_mix_out,
           norm_ffn2_pre, norm_ffn2_post, w_ffn2_in, w_ffn2_out):
    batch, seq, _ = x.shape
    T = batch * seq
    depth = norm_ffn1_pre.shape[0]
    tm = 512
    xt = x.reshape(T, D_MODEL)
    HV = HEADS * DV
    for l in range(depth):
        xt = _ffn(xt, norm_ffn1_pre[l][None], norm_ffn1_post[l][None], w_ffn1_in, w_ffn1_out, l, tm)

        gbias = jnp.pad(jnp.concatenate([gate_i_bias[l], gate_f_bias[l]]), (0, 128 - 2 * NGR))[None]
        bg, u, qt, k, ksw, vt, o, p, ccol = _mix_in(
            xt, norm_mix_pre[l][None], jnp.swapaxes(w_mix_in, 1, 2), gbias, l, 2 * tm)
        gain_b = jnp.broadcast_to(mlstm_norm[l][:, None], (HV, 128))
        y_mlstm = _mlstm(qt, k, ksw, vt, p, ccol, o, gain_b, batch, seq, SCAN_ROWS, MLSTM_ROWS)
        xt = _mix_ffn(xt, bg, u, conv_w[l], conv_b[l][None], y_mlstm, norm_mix_post[l][None], w_mix_out,
                      norm_ffn2_pre[l][None], norm_ffn2_post[l][None], w_ffn2_in, w_ffn2_out, l, seq, tm)
    return xt.reshape(batch, seq, D_MODEL)
```

```python
import functools

import jax
import jax.numpy as jnp
from jax import lax
from jax.experimental import pallas as pl
from jax.experimental.pallas import tpu as pltpu

D_MODEL = 1024
D_FF = 2816
CONV_WIDTH = 512
HEADS = 4
DK = 64
DV = 128
CHUNK = 128
EPS = 1e-6
NEG_INF = -1e30

FF_TILE = 256
FF_STAGE_CHUNKS = 8
FF_SIDE_CHUNKS = 8
MIX_OUT_PIECES = 4
FF_LEAD_TILES = 1
MIX_W_BLOCKS = (3 * CONV_WIDTH + 2 * HEADS * DK + 2 * HEADS * DV) // 128
MLSTM_ROWS = 2048
SCAN_ROWS = 4096
DVA = DV + 16
HALO_ROWS = 8
NGR = 2 * HEADS
CT_ROWS = (HEADS // 2) * DVA
VMEM_LIMIT = 56 * 1024 * 1024

P_E, P_M, P_B, P_G, P_MC = (slice(i * NGR, (i + 1) * NGR) for i in range(5))

F32 = jnp.float32
BF16 = jnp.bfloat16


def _rms(x, g):
    return x * lax.rsqrt(jnp.mean(x * x, axis=-1, keepdims=True) + EPS) * g


def _log_sigmoid(z):
    return jnp.minimum(z, 0.0) - jnp.log1p(jnp.exp(-jnp.abs(z)))


def _resident(shape):
    zeros = (0,) * len(shape)
    return pl.BlockSpec(shape, lambda *_: zeros, pipeline_mode=pl.Buffered(1))


def _split3(x):
    hi = x.astype(BF16)
    r1 = x - hi.astype(F32)
    mid = r1.astype(BF16)
    lo = (r1 - mid.astype(F32)).astype(BF16)
    return hi, mid, lo


def _visible(rev):
    s = lax.broadcasted_iota(jnp.int32, (CHUNK, CHUNK), 0)
    t = lax.broadcasted_iota(jnp.int32, (CHUNK, CHUNK), 1)
    return (s >= t) if rev else (s <= t)


def _running_max(x, rev):
    n = x.shape[1]
    pos = lax.broadcasted_iota(jnp.int32, x.shape, 1) & (CHUNK - 1)
    k = 1
    while k < CHUNK:
        if rev:
            shifted, ok = pltpu.roll(x, n - k, 1), pos < CHUNK - k
        else:
            shifted, ok = pltpu.roll(x, k, 1), pos >= k
        x = jnp.maximum(x, jnp.where(ok, shifted, NEG_INF))
        k *= 2
    return x


def _stage_bf16(src_hbm, dst_ref, stage_ref, sem, rows=None):
    rows = stage_ref.shape[1] if rows is None else rows
    n_chunks = src_hbm.shape[0] // rows

    def copy(c):
        return pltpu.make_async_copy(src_hbm.at[pl.ds(c * rows, rows), :],
                                     stage_ref.at[c % 2, pl.ds(0, rows), :], sem.at[c % 2])

    copy(0).start()
    for c in range(n_chunks):
        if c + 1 < n_chunks:
            copy(c + 1).start()
        copy(c).wait()
        dst_ref[c * rows:(c + 1) * rows, :] = stage_ref[c % 2, 0:rows, :].astype(BF16)


def _mix_out_tile(tiles_per_seq, x_ref, bg_ref, u_ref, uprev_ref, unext_ref, cw_ref, cb_ref,
                  y_ref, wmo_ref, gmix_ref, gpre_ref):
    i = pl.program_id(0)
    tm = u_ref.shape[0]
    u = u_ref[...]
    has_prev = (i % tiles_per_seq != 0).astype(F32)
    has_next = (i % tiles_per_seq != tiles_per_seq - 1).astype(F32)
    prev_row = uprev_ref[HALO_ROWS - 1:HALO_ROWS, :] * has_prev
    next_row = unext_ref[0:1, :] * has_next
    ri = lax.broadcasted_iota(jnp.int32, u.shape, 0)
    u_m1 = jnp.where(ri == 0, prev_row, pltpu.roll(u, 1, 0))
    u_p1 = jnp.where(ri == tm - 1, next_row, pltpu.roll(u, tm - 1, 0))
    conv = cw_ref[0:1, :] * u_m1 + cw_ref[1:2, :] * u + cw_ref[2:3, :] * u_p1
    y_conv = (bg_ref[...] * (conv + cb_ref[...])).astype(BF16)
    rp = tm // MIX_OUT_PIECES
    xs, xns = [], []
    for r in range(MIX_OUT_PIECES):
        rs = slice(r * rp, (r + 1) * rp)
        h = jnp.dot(y_ref[rs, :], wmo_ref[CONV_WIDTH:, :], preferred_element_type=F32) \
            + jnp.dot(y_conv[rs, :], wmo_ref[0:CONV_WIDTH, :], preferred_element_type=F32)
        xs.append(x_ref[rs, :] + _rms(h, gmix_ref[...]))
        xns.append(_rms(xs[-1], gpre_ref[...]).astype(BF16))
    return jnp.concatenate(xs, axis=0), xns


def _ffn_tile(x, xn_pieces, gpost_ref, win_ref, wout_ref, h_ref):
    rp = xn_pieces[0].shape[0]
    xn = jnp.concatenate(xn_pieces, axis=0)
    for j in range(D_FF // FF_TILE):
        lo = j * FF_TILE
        lhs = list(enumerate(xn_pieces)) if j < FF_LEAD_TILES and len(xn_pieces) > 1 else [(None, xn)]
        for r, xr in lhs:
            rs = slice(None) if r is None else slice(r * rp, (r + 1) * rp)
            gate = jnp.dot(xr, win_ref[:, lo:lo + FF_TILE], preferred_element_type=F32)
            up = jnp.dot(xr, win_ref[:, D_FF + lo:D_FF + lo + FF_TILE], preferred_element_type=F32)
            h_ref[rs, lo:lo + FF_TILE] = (gate * jax.nn.sigmoid(gate) * up).astype(BF16)
    y = jnp.dot(h_ref[...], wout_ref[...], preferred_element_type=F32)
    return x + _rms(y, 0.5 * gpost_ref[...])


def _zero_after(v):
    u = pltpu.bitcast(v, jnp.uint32)
    acc = None
    for r in range(u.shape[0] // 8):
        for c in range(u.shape[1] // 128):
            t = u[r * 8:(r + 1) * 8, c * 128:(c + 1) * 128]
            acc = t if acc is None else acc | t
    return pltpu.bitcast((acc >> 16) >> 16, F32)


def _convert_later_weights(i, w2in_blk, w2out_blk, wmo_blk, wt_blk, wtg_blk, w2in_bf, w2out_bf, wmo_bf, wb_blk):
    zeros = []
    for src, dst in ((w2in_blk, w2in_bf), (w2out_blk, w2out_bf), (wmo_blk, wmo_bf)):
        v = src[...].astype(BF16)
        dst[...] = v
        zeros.append(_zero_after(v))
    gates = jnp.concatenate([wtg_blk[...], jnp.zeros((128 - 2 * NGR, D_MODEL), F32)], axis=0)
    last = jnp.clip(i - 1, 0, MIX_W_BLOCKS) == MIX_W_BLOCKS
    blk = jnp.where(last, gates, wt_blk[...]).T
    wb_blk[...] = blk.astype(BF16)
    zeros.append(_zero_after(blk))
    return zeros


def _ffn_kernel(layer, n_tiles, xa_ref, xc_ref, gpre_ref, gpost_ref, win_hbm, wout_hbm,
                w2in_blk, w2out_blk, wmo_blk, wt_blk, wtg_blk,
                o_ref, w2in_bf, w2out_bf, wmo_bf, wb_blk,
                h_ref, win_ref, wout_ref, stage_in, stage_out, sem, xn_ref, y_ref):
    i = pl.program_id(0)
    tm = xa_ref.shape[0]
    g_half = 0.5 * gpost_ref[...]

    @pl.when(i == 0)
    def _():
        _stage_bf16(win_hbm.at[layer], win_ref, stage_in, sem)
        _stage_bf16(wout_hbm.at[layer], wout_ref, stage_out, sem)
        xn_ref[0] = _rms(xa_ref[...], gpre_ref[...]).astype(BF16)
        y_ref[...] = jnp.zeros(y_ref.shape, F32)

    @pl.when((i >= 1) & (i <= n_tiles))
    def _():
        xn = xn_ref[(i - 1) % 2]
        rp = tm // FF_SIDE_CHUNKS
        for j in range(D_FF // FF_TILE):
            lo = j * FF_TILE
            gate = jnp.dot(xn, win_ref[:, lo:lo + FF_TILE], preferred_element_type=F32)
            up = jnp.dot(xn, win_ref[:, D_FF + lo:D_FF + lo + FF_TILE], preferred_element_type=F32)
            hm = gate * jax.nn.sigmoid(gate) * up
            h_ref[:, lo:lo + FF_TILE] = hm.astype(BF16)
            if j < FF_SIDE_CHUNKS:
                rs = slice(j * rp, (j + 1) * rp)
                out_rows = xc_ref[rs, :] + _rms(y_ref[rs, :], g_half)
                o_ref[rs, :] = out_rows
                xn_rows = _rms(xa_ref[rs, :], gpre_ref[...])
                xn_ref[i % 2, rs, :] = xn_rows.astype(BF16)
                zero = _zero_after(out_rows) + _zero_after(xn_rows)
                h_ref[0:8, lo:lo + 128] = (hm[0:8, 0:128] + zero).astype(BF16)
            else:
                if j == FF_SIDE_CHUNKS:
                    side = _convert_later_weights(i, w2in_blk, w2out_blk, wmo_blk, wt_blk, wtg_blk,
                                                  w2in_bf, w2out_bf, wmo_bf, wb_blk)
                    side = [side[0], side[1] + side[2], side[3]]
                h_ref[0:8, lo:lo + 128] = (hm[0:8, 0:128] + side[j - FF_SIDE_CHUNKS]).astype(BF16)
        y_ref[...] = jnp.dot(h_ref[...], wout_ref[...], preferred_element_type=F32)

    @pl.when(i == n_tiles + 1)
    def _():
        o_ref[...] = xc_ref[...] + _rms(y_ref[...], g_half)


def _mix_ffn_kernel(tiles_per_seq, x_ref, bg_ref, u_ref, uprev_ref, unext_ref, cw_ref, cb_ref,
                    y_ref, gmix_ref, wmo_ref, gpre_ref, gpost_ref, win_ref, wout_ref, o_ref, h_ref):
    x, xn = _mix_out_tile(tiles_per_seq, x_ref, bg_ref, u_ref, uprev_ref, unext_ref, cw_ref, cb_ref,
                          y_ref, wmo_ref, gmix_ref, gpre_ref)
    o_ref[...] = _ffn_tile(x, xn, gpost_ref, win_ref, wout_ref, h_ref)


def _ffn_scratch(tm):
    return [
        pltpu.VMEM((tm, D_FF), BF16),
        pltpu.VMEM((D_MODEL, 2 * D_FF), BF16),
        pltpu.VMEM((D_FF, D_MODEL), BF16),
    ], [
        pltpu.VMEM((2, D_MODEL // FF_STAGE_CHUNKS, 2 * D_FF), F32),
        pltpu.VMEM((2, D_FF // FF_STAGE_CHUNKS, D_MODEL), F32),
        pltpu.SemaphoreType.DMA((2,)),
    ]


def _ffn(x, gpre, gpost, w_in, w_out, w2_in, w2_out, w_mix_out, w_mix_in_t, layer, tm):
    T = x.shape[0]
    nt = T // tm
    g0 = MIX_W_BLOCKS * 128
    resident, staging = _ffn_scratch(tm)
    tile = lambda lag: pl.BlockSpec((tm, D_MODEL), lambda i: (jnp.clip(i - lag, 0, nt - 1), 0))
    rows_in = lambda n, w, last: pl.BlockSpec((None, n, w), lambda i: (layer, jnp.clip(i - 1, 0, last), 0))
    rows_out = lambda n, w, last: pl.BlockSpec((n, w), lambda i: (jnp.clip(i - 1, 0, last), 0))
    r_in, r_out, r_mo = D_MODEL // nt, D_FF // (nt // 2), D_MODEL // (nt // 2)
    return pl.pallas_call(
        functools.partial(_ffn_kernel, layer, nt),
        grid=(nt + 2,),
        in_specs=[
            tile(0), tile(2),
            _resident((1, D_MODEL)),
            _resident((1, D_MODEL)),
            pl.BlockSpec(memory_space=pl.ANY),
            pl.BlockSpec(memory_space=pl.ANY),
            rows_in(r_in, 2 * D_FF, nt - 1),
            rows_in(r_out, D_MODEL, nt // 2 - 1),
            rows_in(r_mo, D_MODEL, nt // 2 - 1),
            rows_in(128, D_MODEL, MIX_W_BLOCKS - 1),
            pl.BlockSpec((None, 2 * NGR, D_MODEL), lambda i: (layer, g0 // (2 * NGR), 0)),
        ],
        out_specs=[
            tile(2),
            rows_out(r_in, 2 * D_FF, nt - 1),
            rows_out(r_out, D_MODEL, nt // 2 - 1),
            rows_out(r_mo, D_MODEL, nt // 2 - 1),
            pl.BlockSpec((D_MODEL, 128), lambda i: (0, jnp.clip(i - 1, 0, MIX_W_BLOCKS))),
        ],
        out_shape=[
            jax.ShapeDtypeStruct((T, D_MODEL), F32),
            jax.ShapeDtypeStruct((D_MODEL, 2 * D_FF), BF16),
            jax.ShapeDtypeStruct((D_FF, D_MODEL), BF16),
            jax.ShapeDtypeStruct((D_MODEL, D_MODEL), BF16),
            jax.ShapeDtypeStruct((D_MODEL, g0 + 128), BF16),
        ],
        scratch_shapes=resident + staging + [pltpu.VMEM((2, tm, D_MODEL), BF16),
                                             pltpu.VMEM((tm, D_MODEL), F32)],
        compiler_params=pltpu.CompilerParams(
            dimension_semantics=("arbitrary",), vmem_limit_bytes=VMEM_LIMIT),
        name="ffn",
    )(x, x, gpre, gpost, w_in, w_out, w2_in, w2_out, w_mix_out, w_mix_in_t, w_mix_in_t)


def _mix_ffn(x, bg, u, conv_w, conv_b, y_mlstm, gmix, w_mix_out, gpre, gpost, w_in, w_out, seq, tm):
    T = x.shape[0]
    tiles_per_seq = seq // tm
    sub = tm // HALO_ROWS
    last = T // HALO_ROWS - 1
    row = lambda w: pl.BlockSpec((tm, w), lambda i: (i, 0))
    return pl.pallas_call(
        functools.partial(_mix_ffn_kernel, tiles_per_seq),
        grid=(T // tm,),
        in_specs=[
            row(D_MODEL), row(CONV_WIDTH), row(CONV_WIDTH),
            pl.BlockSpec((HALO_ROWS, CONV_WIDTH), lambda i: (jnp.maximum(i * sub - 1, 0), 0)),
            pl.BlockSpec((HALO_ROWS, CONV_WIDTH), lambda i: (jnp.minimum((i + 1) * sub, last), 0)),
            _resident((3, CONV_WIDTH)), _resident((1, CONV_WIDTH)),
            row(HEADS * DV), _resident((1, D_MODEL)), _resident(w_mix_out.shape),
            _resident((1, D_MODEL)), _resident((1, D_MODEL)), _resident(w_in.shape), _resident(w_out.shape),
        ],
        out_specs=row(D_MODEL),
        out_shape=jax.ShapeDtypeStruct((T, D_MODEL), F32),
        scratch_shapes=[pltpu.VMEM((tm, D_FF), BF16)],
        compiler_params=pltpu.CompilerParams(
            dimension_semantics=("arbitrary",), vmem_limit_bytes=VMEM_LIMIT),
        name="mix_ffn",
    )(x, bg, u, u, u, conv_w, conv_b, y_mlstm, gmix, w_mix_out, gpre, gpost, w_in, w_out)


def _gate_rows(zr, p_ref, ccol_ref):
    L = CHUNK
    n_chunks = zr.shape[1] // L
    fwd_row = lax.broadcasted_iota(jnp.int32, (NGR, L), 0) < HEADS
    fwd_col = lax.broadcasted_iota(jnp.int32, (NGR, 1), 0) < HEADS
    li = zr[0:NGR]
    gates = jnp.concatenate([li, _log_sigmoid(zr[NGR:2 * NGR])], axis=0)
    x3 = jnp.concatenate(_split3(gates), axis=0)
    stacked = jnp.concatenate([x3[:, c * L:(c + 1) * L] for c in range(n_chunks)], axis=0)
    cum_ops = jnp.concatenate([_visible(False), _visible(True)], axis=1).astype(BF16)
    cum = jnp.dot(stacked, cum_ops, preferred_element_type=F32)
    b_chunks = []
    for c in range(n_chunks):
        blk = cum[c * 48:(c + 1) * 48]
        s16 = blk[0:16] + blk[16:32] + blk[32:48]
        b_chunks.append(jnp.where(fwd_row, s16[NGR:2 * NGR, 0:L], s16[NGR:2 * NGR, L:2 * L]))
    b = jnp.concatenate(b_chunks, axis=1)
    cc = li - b
    fwd_all = lax.broadcasted_iota(jnp.int32, cc.shape, 0) < HEADS
    p_ref[P_M, :] = jnp.where(fwd_all, _running_max(cc, False), _running_max(cc, True))
    p_ref[P_B, :] = b
    pad = jnp.zeros((L - NGR, L), F32)
    for c in range(n_chunks):
        sl = slice(c * L, (c + 1) * L)
        b_c, cc_c = b_chunks[c], cc[:, sl]
        g = jnp.where(fwd_col, b_c[:, L - 1:L], b_c[:, 0:1])
        m_chunk = g + jnp.max(cc_c, axis=1, keepdims=True)
        p_ref[P_E, sl] = jnp.exp(g + cc_c - m_chunk)
        p_ref[P_G, sl] = jnp.broadcast_to(g, (NGR, L))
        p_ref[P_MC, sl] = jnp.broadcast_to(m_chunk, (NGR, L))
        ccol_ref[sl, :] = jnp.concatenate([cc_c, pad], axis=0).T


def _mix_in_kernel(x_ref, gpre_ref, wb_ref, gbias_ref,
                   bg_ref, u_ref, qt_ref, k_ref, ksw_ref, vt_ref, o_ref, p_ref, ccol_ref):
    W, HK, HV = CONV_WIDTH, HEADS * DK, HEADS * DV
    q0 = 3 * W
    k0, v0 = q0 + HK, q0 + 2 * HK
    o0 = v0 + HV
    g0 = o0 + HV

    xn = _rms(x_ref[...], gpre_ref[...]).astype(BF16)
    proj = lambda a, b: jnp.dot(xn, wb_ref[:, a:b], preferred_element_type=F32)

    zg = proj(g0, g0 + 128) + gbias_ref[...]
    bg_ref[...] = proj(0, W)
    u_ref[...] = proj(W, 2 * W) * proj(2 * W, 3 * W)
    _gate_rows(zg.T[0:2 * NGR], p_ref, ccol_ref)
    qt_ref[...] = (proj(q0, k0) * (DK ** -0.5)).T.astype(BF16)
    kk = proj(k0, v0)
    k_ref[...] = kk.astype(BF16)
    for pair in range(HEADS // 2):
        ps = slice(pair * 2 * DK, (pair + 1) * 2 * DK)
        ksw_ref[:, ps] = pltpu.roll(kk[:, ps], DK, 1).astype(BF16)
    vt_ref[...] = proj(v0, o0).T.astype(BF16)
    o_ref[...] = proj(o0, g0)


def _mix_in(x, gpre, wb, gbias, tm):
    T = x.shape[0]
    row = lambda w: pl.BlockSpec((tm, w), lambda i: (i, 0))
    col = lambda h: pl.BlockSpec((h, tm), lambda i: (0, i))
    HK, HV = HEADS * DK, HEADS * DV
    return pl.pallas_call(
        _mix_in_kernel,
        grid=(T // tm,),
        in_specs=[row(D_MODEL), _resident((1, D_MODEL)), _resident(wb.shape), _resident((1, 128))],
        out_specs=[row(CONV_WIDTH), row(CONV_WIDTH), col(HK), row(HK), row(HK), col(HV), row(HV),
                   col(5 * NGR), row(128)],
        out_shape=[
            jax.ShapeDtypeStruct((T, CONV_WIDTH), F32),
            jax.ShapeDtypeStruct((T, CONV_WIDTH), F32),
            jax.ShapeDtypeStruct((HK, T), BF16),
            jax.ShapeDtypeStruct((T, HK), BF16),
            jax.ShapeDtypeStruct((T, HK), BF16),
            jax.ShapeDtypeStruct((HV, T), BF16),
            jax.ShapeDtypeStruct((T, HV), F32),
            jax.ShapeDtypeStruct((5 * NGR, T), F32),
            jax.ShapeDtypeStruct((T, 128), F32),
        ],
        compiler_params=pltpu.CompilerParams(
            dimension_semantics=("arbitrary",), vmem_limit_bytes=VMEM_LIMIT),
        name="mix_in",
    )(x, gpre, wb, gbias)


def _state_half(d, h):
    return (h % 2) ^ d


def _scan_direction(d, vt_ref, k_ref, p_ref, ct_ref, mp_ref, ct_state, m_state):
    L = CHUNK
    n_chunks = k_ref.shape[0] // L
    lane_half = lax.broadcasted_iota(jnp.int32, (L, 128), 1) // DK
    lane_half_s = lax.broadcasted_iota(jnp.int32, (DVA, 128), 1) // DK
    ones_rows = jnp.ones((DVA - DV, L), BF16)

    cts = [ct_state[d, h] for h in range(HEADS)]
    m_prev = m_state[d]
    for chunk in (range(n_chunks - 1, -1, -1) if d == 1 else range(n_chunks)):
        sl = slice(chunk * L, (chunk + 1) * L)
        e, g, m_chunk = p_ref[P_E, sl], p_ref[P_G, sl], p_ref[P_MC, sl]
        mp_ref[:, sl] = m_prev
        m_new = jnp.maximum(g + m_prev, m_chunk)
        a_old = jnp.exp(g + m_prev - m_new)
        a_new = jnp.exp(m_chunk - m_new)
        new_cts = []
        for h in range(HEADS):
            r = HEADS * d + h
            vt_aug = jnp.concatenate([vt_ref[h * DV:(h + 1) * DV, sl], ones_rows], axis=0)
            vte = (vt_aug.astype(F32) * e[r:r + 1, :]).astype(BF16)
            pair = slice((h // 2) * 2 * DK, (h // 2 + 1) * 2 * DK)
            k_half = jnp.where(lane_half == _state_half(d, h), k_ref[sl, pair], jnp.zeros((L, 128), BF16))
            ct_chunk = jnp.dot(vte, k_half, preferred_element_type=F32)
            new_cts.append(a_old[r:r + 1, 0:1] * cts[h] + a_new[r:r + 1, 0:1] * ct_chunk)
        for pr in range(HEADS // 2):
            both = jnp.where(lane_half_s == _state_half(d, 2 * pr), cts[2 * pr], cts[2 * pr + 1])
            ct_ref[chunk, pr * DVA:(pr + 1) * DVA, :] = both.astype(BF16)
        cts = new_cts
        m_prev = m_new
    for h in range(HEADS):
        ct_state[d, h] = cts[h]
    m_state[d] = m_prev


def _mlstm_scan_kernel(vt_f, k_f, p_f, vt_b, ksw_b, p_b, ct_f, mp_f, ct_b, mp_b, ct_state, m_state):
    @pl.when(pl.program_id(1) == 0)
    def _():
        ct_state[...] = jnp.zeros(ct_state.shape, F32)
        m_state[...] = jnp.full(m_state.shape, NEG_INF, F32)

    _scan_direction(0, vt_f, k_f, p_f, ct_f, mp_f, ct_state, m_state)
    _scan_direction(1, vt_b, ksw_b, p_b, ct_b, mp_b, ct_state, m_state)


def _mlstm_out_kernel(qt_ref, k_ref, vt_ref, p_ref, mp_f, mp_b, ccol_ref, ct_f, ct_b,
                      o_ref, gain_ref, y_ref):
    L = CHUNK
    n_chunks = k_ref.shape[0] // L
    visible = (_visible(False), _visible(True))
    fwd_row = lax.broadcasted_iota(jnp.int32, (NGR, L), 0) < HEADS
    lane_half = lax.broadcasted_iota(jnp.int32, (DVA, 128), 1) // DK
    ones_rows = jnp.ones((DVA - DV, L), BF16)
    zq = jnp.zeros((DK, L), BF16)
    for chunk in range(n_chunks):
        sl = slice(chunk * L, (chunk + 1) * L)
        m_prev = jnp.where(fwd_row, mp_f[:, sl], mp_b[:, sl])
        n_t = jnp.maximum(m_prev, p_ref[P_M, sl])
        f_inter = jnp.exp(m_prev - n_t)
        e_min = jnp.exp(-(p_ref[P_B, sl] + n_t))
        ccol = ccol_ref[sl, :]
        for h in range(HEADS):
            hs = slice(h * DV, (h + 1) * DV)
            qt = qt_ref[h * DK:(h + 1) * DK, sl]
            k_pair = k_ref[sl, (h // 2) * 2 * DK:(h // 2 + 1) * 2 * DK]
            qt_pair = jnp.concatenate([qt, zq] if h % 2 == 0 else [zq, qt], axis=0)
            st = jnp.dot(k_pair, qt_pair, preferred_element_type=F32)
            pts, qfs = [], []
            for d in range(2):
                r = HEADS * d + h
                arg = jnp.where(visible[d], ccol[:, r:r + 1] - n_t[r:r + 1, :], NEG_INF)
                pts.append((jnp.exp(arg) * st).astype(BF16))
                qfs.append((qt.astype(F32) * f_inter[r:r + 1, :]).astype(BF16))
            inter = [jnp.concatenate([qfs[0], zq], axis=1), jnp.concatenate([zq, qfs[1]], axis=1)]
            first = 0 if _state_half(0, h) == 0 else 1
            rhs = jnp.concatenate([jnp.concatenate(pts, axis=1), inter[first], inter[1 - first]], axis=0)
            vt_aug = jnp.concatenate([vt_ref[hs, sl], ones_rows], axis=0)
            ps = slice((h // 2) * DVA, (h // 2 + 1) * DVA)
            blocks = (ct_f[chunk, ps, :], ct_b[chunk, ps, :])
            ct = jnp.where(lane_half == 0, blocks[first], blocks[1 - first])
            both = jnp.dot(jnp.concatenate([vt_aug, ct], axis=1), rhs, preferred_element_type=F32)
            ht = None
            for d in range(2):
                r = HEADS * d + h
                numer = both[0:DV, d * L:(d + 1) * L]
                denom = both[DV:DV + 1, d * L:(d + 1) * L]
                part = numer / jnp.maximum(jnp.abs(denom), e_min[r:r + 1, :])
                ht = part if ht is None else ht + part
            ms = jnp.mean(ht * ht, axis=0, keepdims=True)
            hn = ht * lax.rsqrt(ms + EPS) * gain_ref[hs, :]
            y_ref[sl, hs] = (jax.nn.sigmoid(o_ref[sl, hs]) * hn.T).astype(BF16)


def _mlstm(qt, k, ksw, vt, p, ccol, o, gain_b, batch, seq, scan_rows, rows):
    T = batch * seq
    ng = seq // scan_rows
    HK, HV = HEADS * DK, HEADS * DV

    def scan_specs(group_of):
        blk = lambda b, j: b * ng + group_of(j)
        ins = [pl.BlockSpec((HV, scan_rows), lambda b, j: (0, blk(b, j))),
               pl.BlockSpec((scan_rows, HK), lambda b, j: (blk(b, j), 0)),
               pl.BlockSpec((5 * NGR, scan_rows), lambda b, j: (0, blk(b, j)))]
        outs = [pl.BlockSpec((scan_rows // CHUNK, CT_ROWS, 128), lambda b, j: (blk(b, j), 0, 0)),
                pl.BlockSpec((NGR, scan_rows), lambda b, j: (0, blk(b, j)))]
        return ins, outs

    ins_f, outs_f = scan_specs(lambda j: j)
    ins_b, outs_b = scan_specs(lambda j: ng - 1 - j)
    state_shapes = [jax.ShapeDtypeStruct((T // CHUNK, CT_ROWS, 128), BF16),
                    jax.ShapeDtypeStruct((NGR, T), F32)]
    if ng == 1:
        scan_kernel = lambda vt_r, k_r, ksw_r, p_r, *rest: _mlstm_scan_kernel(vt_r, k_r, p_r, vt_r, ksw_r, p_r, *rest)
        scan_in_specs, scan_args = [ins_f[0], ins_f[1], ins_f[1], ins_f[2]], (vt, k, ksw, p)
    else:
        scan_kernel, scan_in_specs, scan_args = _mlstm_scan_kernel, ins_f + ins_b, (vt, k, p, vt, ksw, p)
    ct_f, mp_f, ct_b, mp_b = pl.pallas_call(
        scan_kernel,
        grid=(batch, ng),
        in_specs=scan_in_specs,
        out_specs=outs_f + outs_b,
        out_shape=state_shapes + state_shapes,
        scratch_shapes=[pltpu.VMEM((2, HEADS, DVA, 128), F32), pltpu.VMEM((2, NGR, 128), F32)],
        compiler_params=pltpu.CompilerParams(dimension_semantics=("arbitrary", "arbitrary"),
                                             vmem_limit_bytes=VMEM_LIMIT),
        name="mlstm_scan",
    )(*scan_args)

    row = lambda w: pl.BlockSpec((rows, w), lambda i: (i, 0))
    col = lambda h: pl.BlockSpec((h, rows), lambda i: (0, i))
    ctb = pl.BlockSpec((rows // CHUNK, CT_ROWS, 128), lambda i: (i, 0, 0))
    return pl.pallas_call(
        _mlstm_out_kernel,
        grid=(T // rows,),
        in_specs=[col(HK), row(HK), col(HV), col(5 * NGR), col(NGR), col(NGR), row(128), ctb, ctb,
                  row(HV), _resident((HV, 128))],
        out_specs=row(HV),
        out_shape=jax.ShapeDtypeStruct((T, HV), BF16),
        compiler_params=pltpu.CompilerParams(dimension_semantics=("arbitrary",),
                                             vmem_limit_bytes=VMEM_LIMIT),
        name="mlstm_out",
    )(qt, k, vt, p, mp_f, mp_b, ccol, ct_f, ct_b, o, gain_b)


def kernel(x, norm_ffn1_pre, norm_ffn1_post, w_ffn1_in, w_ffn1_out, norm_mix_pre, norm_mix_post,
           w_mix_in, conv_w, conv_b, gate_i_bias, gate_f_bias, mlstm_norm, w_mix_out,
           norm_ffn2_pre, norm_ffn2_post, w_ffn2_in, w_ffn2_out):
    batch, seq, _ = x.shape
    T = batch * seq
    depth = norm_ffn1_pre.shape[0]
    tm = 512
    xt = x.reshape(T, D_MODEL)
    HV = HEADS * DV
    for l in range(depth):
        xt, w2_in, w2_out, w_mo, w_mi = _ffn(
            xt, norm_ffn1_pre[l][None], norm_ffn1_post[l][None], w_ffn1_in, w_ffn1_out,
            w_ffn2_in, w_ffn2_out, w_mix_out, jnp.swapaxes(w_mix_in, 1, 2), l, tm)

        gbias = jnp.pad(jnp.concatenate([gate_i_bias[l], gate_f_bias[l]]), (0, 128 - 2 * NGR))[None]
        bg, u, qt, k, ksw, vt, o, p, ccol = _mix_in(xt, norm_mix_pre[l][None], w_mi, gbias, 2 * tm)
        gain_b = jnp.broadcast_to(mlstm_norm[l][:, None], (HV, 128))
        y_mlstm = _mlstm(qt, k, ksw, vt, p, ccol, o, gain_b, batch, seq, SCAN_ROWS, MLSTM_ROWS)
        xt = _mix_ffn(xt, bg, u, conv_w[l], conv_b[l][None], y_mlstm, norm_mix_post[l][None], w_mo,
                      norm_ffn2_pre[l][None], norm_ffn2_post[l][None], w2_in, w2_out, seq, tm)
    return xt.reshape(batch, seq, D_MODEL)
```

```python
import functools

import jax
import jax.numpy as jnp
from jax import lax
from jax.experimental import pallas as pl
from jax.experimental.pallas import tpu as pltpu

D_MODEL = 1024
D_FF = 2816
CONV_WIDTH = 512
HEADS = 4
DK = 64
DV = 128
CHUNK = 128
EPS = 1e-6
NEG_INF = -1e30

FF_TILE = 256
FF_STAGE_CHUNKS = 8
FF_SIDE_CHUNKS = 8
MIX_OUT_PIECES = 4
FF_SIDE_FIRST_TILE = 2
FF_LEAD_TILES = 1
MIX_W_BLOCKS = (3 * CONV_WIDTH + 2 * HEADS * DK + 2 * HEADS * DV) // 128
MLSTM_ROWS = 2048
SCAN_ROWS = 4096
DVA = DV + 16
HALO_ROWS = 8
NGR = 2 * HEADS
CT_ROWS = (HEADS // 2) * DVA
VMEM_LIMIT = 56 * 1024 * 1024

P_E, P_M, P_B, P_G, P_MC = (slice(i * NGR, (i + 1) * NGR) for i in range(5))

F32 = jnp.float32
BF16 = jnp.bfloat16


def _rms(x, g):
    return x * lax.rsqrt(jnp.mean(x * x, axis=-1, keepdims=True) + EPS) * g


def _log_sigmoid(z):
    return jnp.minimum(z, 0.0) - jnp.log1p(jnp.exp(-jnp.abs(z)))


def _resident(shape):
    zeros = (0,) * len(shape)
    return pl.BlockSpec(shape, lambda *_: zeros, pipeline_mode=pl.Buffered(1))


def _split3(x):
    hi = x.astype(BF16)
    r1 = x - hi.astype(F32)
    mid = r1.astype(BF16)
    lo = (r1 - mid.astype(F32)).astype(BF16)
    return hi, mid, lo


def _visible(rev):
    s = lax.broadcasted_iota(jnp.int32, (CHUNK, CHUNK), 0)
    t = lax.broadcasted_iota(jnp.int32, (CHUNK, CHUNK), 1)
    return (s >= t) if rev else (s <= t)


def _running_max(x, rev):
    n = x.shape[1]
    pos = lax.broadcasted_iota(jnp.int32, x.shape, 1) & (CHUNK - 1)
    k = 1
    while k < CHUNK:
        if rev:
            shifted, ok = pltpu.roll(x, n - k, 1), pos < CHUNK - k
        else:
            shifted, ok = pltpu.roll(x, k, 1), pos >= k
        x = jnp.maximum(x, jnp.where(ok, shifted, NEG_INF))
        k *= 2
    return x


def _stage_bf16(src_hbm, dst_ref, stage_ref, sem, rows=None):
    rows = stage_ref.shape[1] if rows is None else rows
    n_chunks = src_hbm.shape[0] // rows

    def copy(c):
        return pltpu.make_async_copy(src_hbm.at[pl.ds(c * rows, rows), :],
                                     stage_ref.at[c % 2, pl.ds(0, rows), :], sem.at[c % 2])

    copy(0).start()
    for c in range(n_chunks):
        if c + 1 < n_chunks:
            copy(c + 1).start()
        copy(c).wait()
        dst_ref[c * rows:(c + 1) * rows, :] = stage_ref[c % 2, 0:rows, :].astype(BF16)


def _mix_out_tile(i, tiles_per_seq, x_ref, bg_ref, u_ref, uprev_ref, unext_ref, cw_ref, cb_ref,
                  y_ref, wmo_ref, gmix_ref, gpre_ref):
    tm = u_ref.shape[0]
    u = u_ref[...]
    has_prev = (i % tiles_per_seq != 0).astype(F32)
    has_next = (i % tiles_per_seq != tiles_per_seq - 1).astype(F32)
    prev_row = uprev_ref[HALO_ROWS - 1:HALO_ROWS, :] * has_prev
    next_row = unext_ref[0:1, :] * has_next
    ri = lax.broadcasted_iota(jnp.int32, u.shape, 0)
    u_m1 = jnp.where(ri == 0, prev_row, pltpu.roll(u, 1, 0))
    u_p1 = jnp.where(ri == tm - 1, next_row, pltpu.roll(u, tm - 1, 0))
    conv = cw_ref[0:1, :] * u_m1 + cw_ref[1:2, :] * u + cw_ref[2:3, :] * u_p1
    y_conv = (bg_ref[...] * (conv + cb_ref[...])).astype(BF16)
    rp = tm // MIX_OUT_PIECES
    xs, xns = [], []
    for r in range(MIX_OUT_PIECES):
        rs = slice(r * rp, (r + 1) * rp)
        h = jnp.dot(y_ref[rs, :], wmo_ref[CONV_WIDTH:, :], preferred_element_type=F32) \
            + jnp.dot(y_conv[rs, :], wmo_ref[0:CONV_WIDTH, :], preferred_element_type=F32)
        xs.append(x_ref[rs, :] + _rms(h, gmix_ref[...]))
        xns.append(_rms(xs[-1], gpre_ref[...]).astype(BF16))
    return jnp.concatenate(xs, axis=0), xns


def _ffn_matmuls(xn_pieces, win_ref, wout_ref, h_ref, side):
    rp = xn_pieces[0].shape[0]
    xn = jnp.concatenate(xn_pieces, axis=0)
    pending = None
    for j in range(D_FF // FF_TILE):
        lo = j * FF_TILE
        piecewise = j < FF_LEAD_TILES and len(xn_pieces) > 1
        for r, xr in (enumerate(xn_pieces) if piecewise else [(None, xn)]):
            rs = slice(None) if r is None else slice(r * rp, (r + 1) * rp)
            gate = jnp.dot(xr, win_ref[:, lo:lo + FF_TILE], preferred_element_type=F32)
            up = jnp.dot(xr, win_ref[:, D_FF + lo:D_FF + lo + FF_TILE], preferred_element_type=F32)
            hm = gate * jax.nn.sigmoid(gate) * up
            h_ref[rs, lo:lo + FF_TILE] = hm.astype(BF16)
        if pending is not None:
            h_ref[0:8, lo:lo + 128] = (hm[0:8, 0:128] + pending).astype(BF16)
        pending = None if piecewise else side(j, _zero_after(hm[0:8, 0:128]))
    assert pending is None
    return jnp.dot(h_ref[...], wout_ref[...], preferred_element_type=F32)


def _held(v, zero):
    return v + jnp.tile(zero, (v.shape[0] // 8, v.shape[1] // 128))


def _zero_after(v):
    u = pltpu.bitcast(v, jnp.uint32)
    acc = None
    for r in range(u.shape[0] // 8):
        for c in range(u.shape[1] // 128):
            t = u[r * 8:(r + 1) * 8, c * 128:(c + 1) * 128]
            acc = t if acc is None else acc | t
    return pltpu.bitcast((acc >> 16) >> 16, F32)


def _convert_later_weights(i, w2in_blk, w2out_blk, wmo_blk, wt_blk, wtg_blk, w2in_bf, w2out_bf, wmo_bf, wb_blk):
    zeros = []
    for src, dst in ((w2in_blk, w2in_bf), (w2out_blk, w2out_bf), (wmo_blk, wmo_bf)):
        v = src[...].astype(BF16)
        dst[...] = v
        zeros.append(_zero_after(v))
    gates = jnp.concatenate([wtg_blk[...], jnp.zeros((128 - 2 * NGR, D_MODEL), F32)], axis=0)
    last = jnp.clip(i - 1, 0, MIX_W_BLOCKS) == MIX_W_BLOCKS
    blk = jnp.where(last, gates, wt_blk[...]).T
    wb_blk[...] = blk.astype(BF16)
    zeros.append(_zero_after(blk))
    return zeros


def _ffn_kernel(layer, n_tiles, xa_ref, xc_ref, gpre_ref, gpost_ref, win_hbm, wout_hbm,
                w2in_blk, w2out_blk, wmo_blk, wt_blk, wtg_blk,
                o_ref, w2in_bf, w2out_bf, wmo_bf, wb_blk,
                h_ref, win_ref, wout_ref, stage_in, stage_out, sem, xn_ref, y_ref):
    i = pl.program_id(0)
    tm = xa_ref.shape[0]
    g_half = 0.5 * gpost_ref[...]

    @pl.when(i == 0)
    def _():
        _stage_bf16(win_hbm.at[layer], win_ref, stage_in, sem)
        _stage_bf16(wout_hbm.at[layer], wout_ref, stage_out, sem)
        xn_ref[0] = _rms(xa_ref[...], gpre_ref[...]).astype(BF16)
        y_ref[...] = jnp.zeros(y_ref.shape, F32)

    @pl.when((i >= 1) & (i <= n_tiles))
    def _():
        xn = xn_ref[(i - 1) % 2]
        rp = tm // FF_SIDE_CHUNKS
        for j in range(D_FF // FF_TILE):
            lo = j * FF_TILE
            gate = jnp.dot(xn, win_ref[:, lo:lo + FF_TILE], preferred_element_type=F32)
            up = jnp.dot(xn, win_ref[:, D_FF + lo:D_FF + lo + FF_TILE], preferred_element_type=F32)
            hm = gate * jax.nn.sigmoid(gate) * up
            h_ref[:, lo:lo + FF_TILE] = hm.astype(BF16)
            if j < FF_SIDE_CHUNKS:
                rs = slice(j * rp, (j + 1) * rp)
                out_rows = xc_ref[rs, :] + _rms(y_ref[rs, :], g_half)
                o_ref[rs, :] = out_rows
                xn_rows = _rms(xa_ref[rs, :], gpre_ref[...])
                xn_ref[i % 2, rs, :] = xn_rows.astype(BF16)
                zero = _zero_after(out_rows) + _zero_after(xn_rows)
                h_ref[0:8, lo:lo + 128] = (hm[0:8, 0:128] + zero).astype(BF16)
            else:
                if j == FF_SIDE_CHUNKS:
                    side = _convert_later_weights(i, w2in_blk, w2out_blk, wmo_blk, wt_blk, wtg_blk,
                                                  w2in_bf, w2out_bf, wmo_bf, wb_blk)
                    side = [side[0], side[1] + side[2], side[3]]
                h_ref[0:8, lo:lo + 128] = (hm[0:8, 0:128] + side[j - FF_SIDE_CHUNKS]).astype(BF16)
        y_ref[...] = jnp.dot(h_ref[...], wout_ref[...], preferred_element_type=F32)

    @pl.when(i == n_tiles + 1)
    def _():
        o_ref[...] = xc_ref[...] + _rms(y_ref[...], g_half)


def _mix_ffn_kernel(n_tiles, tiles_per_seq, x_ref, bg_ref, u_ref, uprev_ref, unext_ref, cw_ref, cb_ref,
                    y_ref, gmix_ref, wmo_ref, gpre_ref, gpost_ref, win_ref, wout_ref, o_ref,
                    h_ref, xres_ref, yf_ref):
    i = pl.program_id(0)
    tm = x_ref.shape[0]
    g_half = 0.5 * gpost_ref[...]
    rp = tm // FF_SIDE_CHUNKS

    @pl.when(i == 0)
    def _():
        xres_ref[...] = jnp.zeros(xres_ref.shape, F32)
        yf_ref[...] = jnp.zeros(yf_ref.shape, F32)

    @pl.when(i < n_tiles)
    def _():
        x, xn = _mix_out_tile(i, tiles_per_seq, x_ref, bg_ref, u_ref, uprev_ref, unext_ref, cw_ref, cb_ref,
                              y_ref, wmo_ref, gmix_ref, gpre_ref)
        xres_ref[i % 2] = x

        def previous_tile_out(j, begin):
            c = j - FF_SIDE_FIRST_TILE
            if not 0 <= c < FF_SIDE_CHUNKS:
                return None
            rs = slice(c * rp, (c + 1) * rp)
            out_rows = xres_ref[(i + 1) % 2, rs, :] + _rms(_held(yf_ref[rs, :], begin), g_half)
            o_ref[rs, :] = out_rows
            return _zero_after(out_rows)

        yf_ref[...] = _ffn_matmuls(xn, win_ref, wout_ref, h_ref, previous_tile_out)

    @pl.when(i == n_tiles)
    def _():
        o_ref[...] = xres_ref[(n_tiles - 1) % 2] + _rms(yf_ref[...], g_half)


def _ffn_scratch(tm):
    return [
        pltpu.VMEM((tm, D_FF), BF16),
        pltpu.VMEM((D_MODEL, 2 * D_FF), BF16),
        pltpu.VMEM((D_FF, D_MODEL), BF16),
    ], [
        pltpu.VMEM((2, D_MODEL // FF_STAGE_CHUNKS, 2 * D_FF), F32),
        pltpu.VMEM((2, D_FF // FF_STAGE_CHUNKS, D_MODEL), F32),
        pltpu.SemaphoreType.DMA((2,)),
    ]


def _ffn(x, gpre, gpost, w_in, w_out, w2_in, w2_out, w_mix_out, w_mix_in_t, layer, tm):
    T = x.shape[0]
    nt = T // tm
    g0 = MIX_W_BLOCKS * 128
    resident, staging = _ffn_scratch(tm)
    tile = lambda lag: pl.BlockSpec((tm, D_MODEL), lambda i: (jnp.clip(i - lag, 0, nt - 1), 0))
    rows_in = lambda n, w, last: pl.BlockSpec((None, n, w), lambda i: (layer, jnp.clip(i - 1, 0, last), 0))
    rows_out = lambda n, w, last: pl.BlockSpec((n, w), lambda i: (jnp.clip(i - 1, 0, last), 0))
    r_in, r_out, r_mo = D_MODEL // nt, D_FF // (nt // 2), D_MODEL // (nt // 2)
    return pl.pallas_call(
        functools.partial(_ffn_kernel, layer, nt),
        grid=(nt + 2,),
        in_specs=[
            tile(0), tile(2),
            _resident((1, D_MODEL)),
            _resident((1, D_MODEL)),
            pl.BlockSpec(memory_space=pl.ANY),
            pl.BlockSpec(memory_space=pl.ANY),
            rows_in(r_in, 2 * D_FF, nt - 1),
            rows_in(r_out, D_MODEL, nt // 2 - 1),
            rows_in(r_mo, D_MODEL, nt // 2 - 1),
            rows_in(128, D_MODEL, MIX_W_BLOCKS - 1),
            pl.BlockSpec((None, 2 * NGR, D_MODEL), lambda i: (layer, g0 // (2 * NGR), 0)),
        ],
        out_specs=[
            tile(2),
            rows_out(r_in, 2 * D_FF, nt - 1),
            rows_out(r_out, D_MODEL, nt // 2 - 1),
            rows_out(r_mo, D_MODEL, nt // 2 - 1),
            pl.BlockSpec((D_MODEL, 128), lambda i: (0, jnp.clip(i - 1, 0, MIX_W_BLOCKS))),
        ],
        out_shape=[
            jax.ShapeDtypeStruct((T, D_MODEL), F32),
            jax.ShapeDtypeStruct((D_MODEL, 2 * D_FF), BF16),
            jax.ShapeDtypeStruct((D_FF, D_MODEL), BF16),
            jax.ShapeDtypeStruct((D_MODEL, D_MODEL), BF16),
            jax.ShapeDtypeStruct((D_MODEL, g0 + 128), BF16),
        ],
        scratch_shapes=resident + staging + [pltpu.VMEM((2, tm, D_MODEL), BF16),
                                             pltpu.VMEM((tm, D_MODEL), F32)],
        compiler_params=pltpu.CompilerParams(
            dimension_semantics=("arbitrary",), vmem_limit_bytes=VMEM_LIMIT),
        name="ffn",
    )(x, x, gpre, gpost, w_in, w_out, w2_in, w2_out, w_mix_out, w_mix_in_t, w_mix_in_t)


def _mix_ffn(x, bg, u, conv_w, conv_b, y_mlstm, gmix, w_mix_out, gpre, gpost, w_in, w_out, seq, tm):
    T = x.shape[0]
    nt = T // tm
    tiles_per_seq = seq // tm
    sub = tm // HALO_ROWS
    last = T // HALO_ROWS - 1
    front = lambda i: jnp.minimum(i, nt - 1)
    row = lambda w: pl.BlockSpec((tm, w), lambda i: (front(i), 0))
    return pl.pallas_call(
        functools.partial(_mix_ffn_kernel, nt, tiles_per_seq),
        grid=(nt + 1,),
        in_specs=[
            row(D_MODEL), row(CONV_WIDTH), row(CONV_WIDTH),
            pl.BlockSpec((HALO_ROWS, CONV_WIDTH), lambda i: (jnp.maximum(front(i) * sub - 1, 0), 0)),
            pl.BlockSpec((HALO_ROWS, CONV_WIDTH), lambda i: (jnp.minimum((front(i) + 1) * sub, last), 0)),
            _resident((3, CONV_WIDTH)), _resident((1, CONV_WIDTH)),
            row(HEADS * DV), _resident((1, D_MODEL)), _resident(w_mix_out.shape),
            _resident((1, D_MODEL)), _resident((1, D_MODEL)), _resident(w_in.shape), _resident(w_out.shape),
        ],
        out_specs=pl.BlockSpec((tm, D_MODEL), lambda i: (jnp.maximum(i - 1, 0), 0)),
        out_shape=jax.ShapeDtypeStruct((T, D_MODEL), F32),
        scratch_shapes=[pltpu.VMEM((tm, D_FF), BF16),
                        pltpu.VMEM((2, tm, D_MODEL), F32),
                        pltpu.VMEM((tm, D_MODEL), F32)],
        compiler_params=pltpu.CompilerParams(
            dimension_semantics=("arbitrary",), vmem_limit_bytes=VMEM_LIMIT),
        name="mix_ffn",
    )(x, bg, u, u, u, conv_w, conv_b, y_mlstm, gmix, w_mix_out, gpre, gpost, w_in, w_out)


def _gate_rows(zr, p_ref, ccol_ref):
    L = CHUNK
    n_chunks = zr.shape[1] // L
    fwd_row = lax.broadcasted_iota(jnp.int32, (NGR, L), 0) < HEADS
    fwd_col = lax.broadcasted_iota(jnp.int32, (NGR, 1), 0) < HEADS
    li = zr[0:NGR]
    gates = jnp.concatenate([li, _log_sigmoid(zr[NGR:2 * NGR])], axis=0)
    x3 = jnp.concatenate(_split3(gates), axis=0)
    stacked = jnp.concatenate([x3[:, c * L:(c + 1) * L] for c in range(n_chunks)], axis=0)
    cum_ops = jnp.concatenate([_visible(False), _visible(True)], axis=1).astype(BF16)
    cum = jnp.dot(stacked, cum_ops, preferred_element_type=F32)
    b_chunks = []
    for c in range(n_chunks):
        blk = cum[c * 48:(c + 1) * 48]
        s16 = blk[0:16] + blk[16:32] + blk[32:48]
        b_chunks.append(jnp.where(fwd_row, s16[NGR:2 * NGR, 0:L], s16[NGR:2 * NGR, L:2 * L]))
    b = jnp.concatenate(b_chunks, axis=1)
    cc = li - b
    fwd_all = lax.broadcasted_iota(jnp.int32, cc.shape, 0) < HEADS
    p_ref[P_M, :] = jnp.where(fwd_all, _running_max(cc, False), _running_max(cc, True))
    p_ref[P_B, :] = b
    pad = jnp.zeros((L - NGR, L), F32)
    for c in range(n_chunks):
        sl = slice(c * L, (c + 1) * L)
        b_c, cc_c = b_chunks[c], cc[:, sl]
        g = jnp.where(fwd_col, b_c[:, L - 1:L], b_c[:, 0:1])
        m_chunk = g + jnp.max(cc_c, axis=1, keepdims=True)
        p_ref[P_E, sl] = jnp.exp(g + cc_c - m_chunk)
        p_ref[P_G, sl] = jnp.broadcast_to(g, (NGR, L))
        p_ref[P_MC, sl] = jnp.broadcast_to(m_chunk, (NGR, L))
        ccol_ref[sl, :] = jnp.concatenate([cc_c, pad], axis=0).T


def _mix_in_kernel(x_ref, gpre_ref, wb_ref, gbias_ref,
                   bg_ref, u_ref, qt_ref, k_ref, ksw_ref, vt_ref, o_ref, p_ref, ccol_ref):
    W, HK, HV = CONV_WIDTH, HEADS * DK, HEADS * DV
    q0 = 3 * W
    k0, v0 = q0 + HK, q0 + 2 * HK
    o0 = v0 + HV
    g0 = o0 + HV

    xn = _rms(x_ref[...], gpre_ref[...]).astype(BF16)
    proj = lambda a, b: jnp.dot(xn, wb_ref[:, a:b], preferred_element_type=F32)

    zg = proj(g0, g0 + 128) + gbias_ref[...]
    bg_ref[...] = proj(0, W)
    u_ref[...] = proj(W, 2 * W) * proj(2 * W, 3 * W)
    _gate_rows(zg.T[0:2 * NGR], p_ref, ccol_ref)
    qt_ref[...] = (proj(q0, k0) * (DK ** -0.5)).T.astype(BF16)
    kk = proj(k0, v0)
    k_ref[...] = kk.astype(BF16)
    for pair in range(HEADS // 2):
        ps = slice(pair * 2 * DK, (pair + 1) * 2 * DK)
        ksw_ref[:, ps] = pltpu.roll(kk[:, ps], DK, 1).astype(BF16)
    vt_ref[...] = proj(v0, o0).T.astype(BF16)
    o_ref[...] = proj(o0, g0)


def _mix_in(x, gpre, wb, gbias, tm):
    T = x.shape[0]
    row = lambda w: pl.BlockSpec((tm, w), lambda i: (i, 0))
    col = lambda h: pl.BlockSpec((h, tm), lambda i: (0, i))
    HK, HV = HEADS * DK, HEADS * DV
    return pl.pallas_call(
        _mix_in_kernel,
        grid=(T // tm,),
        in_specs=[row(D_MODEL), _resident((1, D_MODEL)), _resident(wb.shape), _resident((1, 128))],
        out_specs=[row(CONV_WIDTH), row(CONV_WIDTH), col(HK), row(HK), row(HK), col(HV), row(HV),
                   col(5 * NGR), row(128)],
        out_shape=[
            jax.ShapeDtypeStruct((T, CONV_WIDTH), F32),
            jax.ShapeDtypeStruct((T, CONV_WIDTH), F32),
            jax.ShapeDtypeStruct((HK, T), BF16),
            jax.ShapeDtypeStruct((T, HK), BF16),
            jax.ShapeDtypeStruct((T, HK), BF16),
            jax.ShapeDtypeStruct((HV, T), BF16),
            jax.ShapeDtypeStruct((T, HV), F32),
            jax.ShapeDtypeStruct((5 * NGR, T), F32),
            jax.ShapeDtypeStruct((T, 128), F32),
        ],
        compiler_params=pltpu.CompilerParams(
            dimension_semantics=("arbitrary",), vmem_limit_bytes=VMEM_LIMIT),
        name="mix_in",
    )(x, gpre, wb, gbias)


def _state_half(d, h):
    return (h % 2) ^ d


def _scan_direction(d, vt_ref, k_ref, p_ref, ct_ref, mp_ref, ct_state, m_state):
    L = CHUNK
    n_chunks = k_ref.shape[0] // L
    lane_half = lax.broadcasted_iota(jnp.int32, (L, 128), 1) // DK
    lane_half_s = lax.broadcasted_iota(jnp.int32, (DVA, 128), 1) // DK
    ones_rows = jnp.ones((DVA - DV, L), BF16)

    cts = [ct_state[d, h] for h in range(HEADS)]
    m_prev = m_state[d]
    for chunk in (range(n_chunks - 1, -1, -1) if d == 1 else range(n_chunks)):
        sl = slice(chunk * L, (chunk + 1) * L)
        e, g, m_chunk = p_ref[P_E, sl], p_ref[P_G, sl], p_ref[P_MC, sl]
        mp_ref[:, sl] = m_prev
        m_new = jnp.maximum(g + m_prev, m_chunk)
        a_old = jnp.exp(g + m_prev - m_new)
        a_new = jnp.exp(m_chunk - m_new)
        new_cts = []
        for h in range(HEADS):
            r = HEADS * d + h
            vt_aug = jnp.concatenate([vt_ref[h * DV:(h + 1) * DV, sl], ones_rows], axis=0)
            vte = (vt_aug.astype(F32) * e[r:r + 1, :]).astype(BF16)
            pair = slice((h // 2) * 2 * DK, (h // 2 + 1) * 2 * DK)
            k_half = jnp.where(lane_half == _state_half(d, h), k_ref[sl, pair], jnp.zeros((L, 128), BF16))
            ct_chunk = jnp.dot(vte, k_half, preferred_element_type=F32)
            new_cts.append(a_old[r:r + 1, 0:1] * cts[h] + a_new[r:r + 1, 0:1] * ct_chunk)
        for pr in range(HEADS // 2):
            both = jnp.where(lane_half_s == _state_half(d, 2 * pr), cts[2 * pr], cts[2 * pr + 1])
            ct_ref[chunk, pr * DVA:(pr + 1) * DVA, :] = both.astype(BF16)
        cts = new_cts
        m_prev = m_new
    for h in range(HEADS):
        ct_state[d, h] = cts[h]
    m_state[d] = m_prev


def _mlstm_scan_kernel(vt_f, k_f, p_f, vt_b, ksw_b, p_b, ct_f, mp_f, ct_b, mp_b, ct_state, m_state):
    @pl.when(pl.program_id(1) == 0)
    def _():
        ct_state[...] = jnp.zeros(ct_state.shape, F32)
        m_state[...] = jnp.full(m_state.shape, NEG_INF, F32)

    _scan_direction(0, vt_f, k_f, p_f, ct_f, mp_f, ct_state, m_state)
    _scan_direction(1, vt_b, ksw_b, p_b, ct_b, mp_b, ct_state, m_state)


def _mlstm_out_kernel(qt_ref, k_ref, vt_ref, p_ref, mp_f, mp_b, ccol_ref, ct_f, ct_b,
                      o_ref, gain_ref, y_ref):
    L = CHUNK
    n_chunks = k_ref.shape[0] // L
    visible = (_visible(False), _visible(True))
    fwd_row = lax.broadcasted_iota(jnp.int32, (NGR, L), 0) < HEADS
    lane_half = lax.broadcasted_iota(jnp.int32, (DVA, 128), 1) // DK
    ones_rows = jnp.ones((DVA - DV, L), BF16)
    zq = jnp.zeros((DK, L), BF16)
    for chunk in range(n_chunks):
        sl = slice(chunk * L, (chunk + 1) * L)
        m_prev = jnp.where(fwd_row, mp_f[:, sl], mp_b[:, sl])
        n_t = jnp.maximum(m_prev, p_ref[P_M, sl])
        f_inter = jnp.exp(m_prev - n_t)
        e_min = jnp.exp(-(p_ref[P_B, sl] + n_t))
        ccol = ccol_ref[sl, :]
        for h in range(HEADS):
            hs = slice(h * DV, (h + 1) * DV)
            qt = qt_ref[h * DK:(h + 1) * DK, sl]
            k_pair = k_ref[sl, (h // 2) * 2 * DK:(h // 2 + 1) * 2 * DK]
            qt_pair = jnp.concatenate([qt, zq] if h % 2 == 0 else [zq, qt], axis=0)
            st = jnp.dot(k_pair, qt_pair, preferred_element_type=F32)
            pts, qfs = [], []
            for d in range(2):
                r = HEADS * d + h
                arg = jnp.where(visible[d], ccol[:, r:r + 1] - n_t[r:r + 1, :], NEG_INF)
                pts.append((jnp.exp(arg) * st).astype(BF16))
                qfs.append((qt.astype(F32) * f_inter[r:r + 1, :]).astype(BF16))
            inter = [jnp.concatenate([qfs[0], zq], axis=1), jnp.concatenate([zq, qfs[1]], axis=1)]
            first = 0 if _state_half(0, h) == 0 else 1
            rhs = jnp.concatenate([jnp.concatenate(pts, axis=1), inter[first], inter[1 - first]], axis=0)
            vt_aug = jnp.concatenate([vt_ref[hs, sl], ones_rows], axis=0)
            ps = slice((h // 2) * DVA, (h // 2 + 1) * DVA)
            blocks = (ct_f[chunk, ps, :], ct_b[chunk, ps, :])
            ct = jnp.where(lane_half == 0, blocks[first], blocks[1 - first])
            both = jnp.dot(jnp.concatenate([vt_aug, ct], axis=1), rhs, preferred_element_type=F32)
            ht = None
            for d in range(2):
                r = HEADS * d + h
                numer = both[0:DV, d * L:(d + 1) * L]
                denom = both[DV:DV + 1, d * L:(d + 1) * L]
                part = numer / jnp.maximum(jnp.abs(denom), e_min[r:r + 1, :])
                ht = part if ht is None else ht + part
            ms = jnp.mean(ht * ht, axis=0, keepdims=True)
            hn = ht * lax.rsqrt(ms + EPS) * gain_ref[hs, :]
            y_ref[sl, hs] = (jax.nn.sigmoid(o_ref[sl, hs]) * hn.T).astype(BF16)


def _mlstm(qt, k, ksw, vt, p, ccol, o, gain_b, batch, seq, scan_rows, rows):
    T = batch * seq
    ng = seq // scan_rows
    HK, HV = HEADS * DK, HEADS * DV

    def scan_specs(group_of):
        blk = lambda b, j: b * ng + group_of(j)
        ins = [pl.BlockSpec((HV, scan_rows), lambda b, j: (0, blk(b, j))),
               pl.BlockSpec((scan_rows, HK), lambda b, j: (blk(b, j), 0)),
               pl.BlockSpec((5 * NGR, scan_rows), lambda b, j: (0, blk(b, j)))]
        outs = [pl.BlockSpec((scan_rows // CHUNK, CT_ROWS, 128), lambda b, j: (blk(b, j), 0, 0)),
                pl.BlockSpec((NGR, scan_rows), lambda b, j: (0, blk(b, j)))]
        return ins, outs

    ins_f, outs_f = scan_specs(lambda j: j)
    ins_b, outs_b = scan_specs(lambda j: ng - 1 - j)
    state_shapes = [jax.ShapeDtypeStruct((T // CHUNK, CT_ROWS, 128), BF16),
                    jax.ShapeDtypeStruct((NGR, T), F32)]
    if ng == 1:
        scan_kernel = lambda vt_r, k_r, ksw_r, p_r, *rest: _mlstm_scan_kernel(vt_r, k_r, p_r, vt_r, ksw_r, p_r, *rest)
        scan_in_specs, scan_args = [ins_f[0], ins_f[1], ins_f[1], ins_f[2]], (vt, k, ksw, p)
    else:
        scan_kernel, scan_in_specs, scan_args = _mlstm_scan_kernel, ins_f + ins_b, (vt, k, p, vt, ksw, p)
    ct_f, mp_f, ct_b, mp_b = pl.pallas_call(
        scan_kernel,
        grid=(batch, ng),
        in_specs=scan_in_specs,
        out_specs=outs_f + outs_b,
        out_shape=state_shapes + state_shapes,
        scratch_shapes=[pltpu.VMEM((2, HEADS, DVA, 128), F32), pltpu.VMEM((2, NGR, 128), F32)],
        compiler_params=pltpu.CompilerParams(dimension_semantics=("arbitrary", "arbitrary"),
                                             vmem_limit_bytes=VMEM_LIMIT),
        name="mlstm_scan",
    )(*scan_args)

    row = lambda w: pl.BlockSpec((rows, w), lambda i: (i, 0))
    col = lambda h: pl.BlockSpec((h, rows), lambda i: (0, i))
    ctb = pl.BlockSpec((rows // CHUNK, CT_ROWS, 128), lambda i: (i, 0, 0))
    return pl.pallas_call(
        _mlstm_out_kernel,
        grid=(T // rows,),
        in_specs=[col(HK), row(HK), col(HV), col(5 * NGR), col(NGR), col(NGR), row(128), ctb, ctb,
                  row(HV), _resident((HV, 128))],
        out_specs=row(HV),
        out_shape=jax.ShapeDtypeStruct((T, HV), BF16),
        compiler_params=pltpu.CompilerParams(dimension_semantics=("arbitrary",),
                                             vmem_limit_bytes=VMEM_LIMIT),
        name="mlstm_out",
    )(qt, k, vt, p, mp_f, mp_b, ccol, ct_f, ct_b, o, gain_b)


def kernel(x, norm_ffn1_pre, norm_ffn1_post, w_ffn1_in, w_ffn1_out, norm_mix_pre, norm_mix_post,
           w_mix_in, conv_w, conv_b, gate_i_bias, gate_f_bias, mlstm_norm, w_mix_out,
           norm_ffn2_pre, norm_ffn2_post, w_ffn2_in, w_ffn2_out):
    batch, seq, _ = x.shape
    T = batch * seq
    depth = norm_ffn1_pre.shape[0]
    tm = 512
    xt = x.reshape(T, D_MODEL)
    HV = HEADS * DV
    for l in range(depth):
        xt, w2_in, w2_out, w_mo, w_mi = _ffn(
            xt, norm_ffn1_pre[l][None], norm_ffn1_post[l][None], w_ffn1_in, w_ffn1_out,
            w_ffn2_in, w_ffn2_out, w_mix_out, jnp.swapaxes(w_mix_in, 1, 2), l, tm)

        gbias = jnp.pad(jnp.concatenate([gate_i_bias[l], gate_f_bias[l]]), (0, 128 - 2 * NGR))[None]
        bg, u, qt, k, ksw, vt, o, p, ccol = _mix_in(xt, norm_mix_pre[l][None], w_mi, gbias, 2 * tm)
        gain_b = jnp.broadcast_to(mlstm_norm[l][:, None], (HV, 128))
        y_mlstm = _mlstm(qt, k, ksw, vt, p, ccol, o, gain_b, batch, seq, SCAN_ROWS, MLSTM_ROWS)
        xt = _mix_ffn(xt, bg, u, conv_w[l], conv_b[l][None], y_mlstm, norm_mix_post[l][None], w_mo,
                      norm_ffn2_pre[l][None], norm_ffn2_post[l][None], w2_in, w2_out, seq, tm)
    return xt.reshape(batch, seq, D_MODEL)
```

```python
import functools

import jax
import jax.numpy as jnp
from jax import lax
from jax.experimental import pallas as pl
from jax.experimental.pallas import tpu as pltpu

D_MODEL = 1024
D_FF = 2816
CONV_WIDTH = 512
HEADS = 4
DK = 64
DV = 128
CHUNK = 128
EPS = 1e-6
NEG_INF = -1e30

FF_TILE = 256
FF_STAGE_CHUNKS = 8
FF_SIDE_CHUNKS = 8
MIX_OUT_PIECES = 4
FF_LEAD_TILES = 1
MIX_W_BLOCKS = (3 * CONV_WIDTH + 2 * HEADS * DK + 2 * HEADS * DV) // 128
MLSTM_ROWS = 2048
SCAN_ROWS = 4096
DVA = DV + 16
HALO_ROWS = 8
NGR = 2 * HEADS
CT_ROWS = (HEADS // 2) * DVA
VMEM_LIMIT = 56 * 1024 * 1024

P_E, P_M, P_B, P_G, P_MC = (slice(i * NGR, (i + 1) * NGR) for i in range(5))

F32 = jnp.float32
BF16 = jnp.bfloat16


def _rms(x, g):
    return x * lax.rsqrt(jnp.mean(x * x, axis=-1, keepdims=True) + EPS) * g


def _log_sigmoid(z):
    return jnp.minimum(z, 0.0) - jnp.log1p(jnp.exp(-jnp.abs(z)))


def _resident(shape):
    zeros = (0,) * len(shape)
    return pl.BlockSpec(shape, lambda *_: zeros, pipeline_mode=pl.Buffered(1))


def _split3(x):
    hi = x.astype(BF16)
    r1 = x - hi.astype(F32)
    mid = r1.astype(BF16)
    lo = (r1 - mid.astype(F32)).astype(BF16)
    return hi, mid, lo


def _visible(rev):
    s = lax.broadcasted_iota(jnp.int32, (CHUNK, CHUNK), 0)
    t = lax.broadcasted_iota(jnp.int32, (CHUNK, CHUNK), 1)
    return (s >= t) if rev else (s <= t)


def _running_max(x, rev):
    n = x.shape[1]
    pos = lax.broadcasted_iota(jnp.int32, x.shape, 1) & (CHUNK - 1)
    k = 1
    while k < CHUNK:
        if rev:
            shifted, ok = pltpu.roll(x, n - k, 1), pos < CHUNK - k
        else:
            shifted, ok = pltpu.roll(x, k, 1), pos >= k
        x = jnp.maximum(x, jnp.where(ok, shifted, NEG_INF))
        k *= 2
    return x


def _stage_bf16(src_hbm, dst_ref, stage_ref, sem, rows=None):
    rows = stage_ref.shape[1] if rows is None else rows
    n_chunks = src_hbm.shape[0] // rows

    def copy(c):
        return pltpu.make_async_copy(src_hbm.at[pl.ds(c * rows, rows), :],
                                     stage_ref.at[c % 2, pl.ds(0, rows), :], sem.at[c % 2])

    copy(0).start()
    for c in range(n_chunks):
        if c + 1 < n_chunks:
            copy(c + 1).start()
        copy(c).wait()
        dst_ref[c * rows:(c + 1) * rows, :] = stage_ref[c % 2, 0:rows, :].astype(BF16)


def _mix_out_tile(tiles_per_seq, x_ref, bg_ref, u_ref, uprev_ref, unext_ref, cw_ref, cb_ref,
                  y_ref, wmo_ref, gmix_ref, gpre_ref):
    i = pl.program_id(0)
    tm = u_ref.shape[0]
    u = u_ref[...]
    has_prev = (i % tiles_per_seq != 0).astype(F32)
    has_next = (i % tiles_per_seq != tiles_per_seq - 1).astype(F32)
    prev_row = uprev_ref[HALO_ROWS - 1:HALO_ROWS, :] * has_prev
    next_row = unext_ref[0:1, :] * has_next
    ri = lax.broadcasted_iota(jnp.int32, u.shape, 0)
    u_m1 = jnp.where(ri == 0, prev_row, pltpu.roll(u, 1, 0))
    u_p1 = jnp.where(ri == tm - 1, next_row, pltpu.roll(u, tm - 1, 0))
    conv = cw_ref[0:1, :] * u_m1 + cw_ref[1:2, :] * u + cw_ref[2:3, :] * u_p1
    y_conv = (bg_ref[...] * (conv + cb_ref[...])).astype(BF16)
    rp = tm // MIX_OUT_PIECES
    xs, xns = [], []
    for r in range(MIX_OUT_PIECES):
        rs = slice(r * rp, (r + 1) * rp)
        h = jnp.dot(y_ref[rs, :], wmo_ref[CONV_WIDTH:, :], preferred_element_type=F32) \
            + jnp.dot(y_conv[rs, :], wmo_ref[0:CONV_WIDTH, :], preferred_element_type=F32)
        xs.append(x_ref[rs, :] + _rms(h, gmix_ref[...]))
        xns.append(_rms(xs[-1], gpre_ref[...]).astype(BF16))
    return jnp.concatenate(xs, axis=0), xns


def _ffn_tile(x, xn_pieces, gpost_ref, win_ref, wout_ref, h_ref):
    rp = xn_pieces[0].shape[0]
    xn = jnp.concatenate(xn_pieces, axis=0)
    for j in range(D_FF // FF_TILE):
        lo = j * FF_TILE
        lhs = list(enumerate(xn_pieces)) if j < FF_LEAD_TILES and len(xn_pieces) > 1 else [(None, xn)]
        for r, xr in lhs:
            rs = slice(None) if r is None else slice(r * rp, (r + 1) * rp)
            gate = jnp.dot(xr, win_ref[:, lo:lo + FF_TILE], preferred_element_type=F32)
            up = jnp.dot(xr, win_ref[:, D_FF + lo:D_FF + lo + FF_TILE], preferred_element_type=F32)
            h_ref[rs, lo:lo + FF_TILE] = (gate * jax.nn.sigmoid(gate) * up).astype(BF16)
    y = jnp.dot(h_ref[...], wout_ref[...], preferred_element_type=F32)
    return x + _rms(y, 0.5 * gpost_ref[...])


def _zero_after(v):
    u = pltpu.bitcast(v, jnp.uint32)
    acc = None
    for r in range(u.shape[0] // 8):
        for c in range(u.shape[1] // 128):
            t = u[r * 8:(r + 1) * 8, c * 128:(c + 1) * 128]
            acc = t if acc is None else acc | t
    return pltpu.bitcast((acc >> 16) >> 16, F32)


def _convert_later_weights(i, w2in_blk, w2out_blk, wmo_blk, wt_blk, wtg_blk, w2in_bf, w2out_bf, wmo_bf, wb_blk):
    zeros = []
    for src, dst in ((w2in_blk, w2in_bf), (w2out_blk, w2out_bf), (wmo_blk, wmo_bf)):
        v = src[...].astype(BF16)
        dst[...] = v
        zeros.append(_zero_after(v))
    gates = jnp.concatenate([wtg_blk[...], jnp.zeros((128 - 2 * NGR, D_MODEL), F32)], axis=0)
    last = jnp.clip(i - 1, 0, MIX_W_BLOCKS) == MIX_W_BLOCKS
    blk = jnp.where(last, gates, wt_blk[...]).T
    wb_blk[...] = blk.astype(BF16)
    zeros.append(_zero_after(blk))
    return zeros


def _ffn_kernel(layer, n_tiles, xa_ref, xc_ref, gpre_ref, gpost_ref, win_hbm, wout_hbm,
                w2in_blk, w2out_blk, wmo_blk, wt_blk, wtg_blk,
                o_ref, w2in_bf, w2out_bf, wmo_bf, wb_blk,
                h_ref, win_ref, wout_ref, stage_in, stage_out, sem, xn_ref, y_ref):
    i = pl.program_id(0)
    tm = xa_ref.shape[0]
    g_half = 0.5 * gpost_ref[...]

    @pl.when(i == 0)
    def _():
        _stage_bf16(win_hbm.at[layer], win_ref, stage_in, sem)
        _stage_bf16(wout_hbm.at[layer], wout_ref, stage_out, sem)
        xn_ref[0] = _rms(xa_ref[...], gpre_ref[...]).astype(BF16)
        y_ref[...] = jnp.zeros(y_ref.shape, F32)

    @pl.when((i >= 1) & (i <= n_tiles))
    def _():
        xn = xn_ref[(i - 1) % 2]
        rp = tm // FF_SIDE_CHUNKS
        for j in range(D_FF // FF_TILE):
            lo = j * FF_TILE
            gate = jnp.dot(xn, win_ref[:, lo:lo + FF_TILE], preferred_element_type=F32)
            up = jnp.dot(xn, win_ref[:, D_FF + lo:D_FF + lo + FF_TILE], preferred_element_type=F32)
            hm = gate * jax.nn.sigmoid(gate) * up
            h_ref[:, lo:lo + FF_TILE] = hm.astype(BF16)
            if j < FF_SIDE_CHUNKS:
                rs = slice(j * rp, (j + 1) * rp)
                out_rows = xc_ref[rs, :] + _rms(y_ref[rs, :], g_half)
                o_ref[rs, :] = out_rows
                xn_rows = _rms(xa_ref[rs, :], gpre_ref[...])
                xn_ref[i % 2, rs, :] = xn_rows.astype(BF16)
                zero = _zero_after(out_rows) + _zero_after(xn_rows)
                h_ref[0:8, lo:lo + 128] = (hm[0:8, 0:128] + zero).astype(BF16)
            else:
                if j == FF_SIDE_CHUNKS:
                    side = _convert_later_weights(i, w2in_blk, w2out_blk, wmo_blk, wt_blk, wtg_blk,
                                                  w2in_bf, w2out_bf, wmo_bf, wb_blk)
                    side = [side[0], side[1] + side[2], side[3]]
                h_ref[0:8, lo:lo + 128] = (hm[0:8, 0:128] + side[j - FF_SIDE_CHUNKS]).astype(BF16)
        y_ref[...] = jnp.dot(h_ref[...], wout_ref[...], preferred_element_type=F32)

    @pl.when(i == n_tiles + 1)
    def _():
        o_ref[...] = xc_ref[...] + _rms(y_ref[...], g_half)


def _mix_ffn_kernel(tiles_per_seq, x_ref, bg_ref, u_ref, uprev_ref, unext_ref, cw_ref, cb_ref,
                    y_ref, gmix_ref, wmo_ref, gpre_ref, gpost_ref, win_ref, wout_ref, o_ref, h_ref):
    x, xn = _mix_out_tile(tiles_per_seq, x_ref, bg_ref, u_ref, uprev_ref, unext_ref, cw_ref, cb_ref,
                          y_ref, wmo_ref, gmix_ref, gpre_ref)
    o_ref[...] = _ffn_tile(x, xn, gpost_ref, win_ref, wout_ref, h_ref)


def _ffn_scratch(tm):
    return [
        pltpu.VMEM((tm, D_FF), BF16),
        pltpu.VMEM((D_MODEL, 2 * D_FF), BF16),
        pltpu.VMEM((D_FF, D_MODEL), BF16),
    ], [
        pltpu.VMEM((2, D_MODEL // FF_STAGE_CHUNKS, 2 * D_FF), F32),
        pltpu.VMEM((2, D_FF // FF_STAGE_CHUNKS, D_MODEL), F32),
        pltpu.SemaphoreType.DMA((2,)),
    ]


def _ffn(x, gpre, gpost, w_in, w_out, w2_in, w2_out, w_mix_out, w_mix_in_t, layer, tm):
    T = x.shape[0]
    nt = T // tm
    assert nt > MIX_W_BLOCKS and nt % 2 == 0, f"first FFN needs more than {MIX_W_BLOCKS} token tiles, got {nt}"
    g0 = MIX_W_BLOCKS * 128
    resident, staging = _ffn_scratch(tm)
    tile = lambda lag: pl.BlockSpec((tm, D_MODEL), lambda i: (jnp.clip(i - lag, 0, nt - 1), 0))
    rows_in = lambda n, w, last: pl.BlockSpec((None, n, w), lambda i: (layer, jnp.clip(i - 1, 0, last), 0))
    rows_out = lambda n, w, last: pl.BlockSpec((n, w), lambda i: (jnp.clip(i - 1, 0, last), 0))
    r_in, r_out, r_mo = D_MODEL // nt, D_FF // (nt // 2), D_MODEL // (nt // 2)
    return pl.pallas_call(
        functools.partial(_ffn_kernel, layer, nt),
        grid=(nt + 2,),
        in_specs=[
            tile(0), tile(2),
            _resident((1, D_MODEL)),
            _resident((1, D_MODEL)),
            pl.BlockSpec(memory_space=pl.ANY),
            pl.BlockSpec(memory_space=pl.ANY),
            rows_in(r_in, 2 * D_FF, nt - 1),
            rows_in(r_out, D_MODEL, nt // 2 - 1),
            rows_in(r_mo, D_MODEL, nt // 2 - 1),
            rows_in(128, D_MODEL, MIX_W_BLOCKS - 1),
            pl.BlockSpec((None, 2 * NGR, D_MODEL), lambda i: (layer, g0 // (2 * NGR), 0)),
        ],
        out_specs=[
            tile(2),
            rows_out(r_in, 2 * D_FF, nt - 1),
            rows_out(r_out, D_MODEL, nt // 2 - 1),
            rows_out(r_mo, D_MODEL, nt // 2 - 1),
            pl.BlockSpec((D_MODEL, 128), lambda i: (0, jnp.clip(i - 1, 0, MIX_W_BLOCKS))),
        ],
        out_shape=[
            jax.ShapeDtypeStruct((T, D_MODEL), F32),
            jax.ShapeDtypeStruct((D_MODEL, 2 * D_FF), BF16),
            jax.ShapeDtypeStruct((D_FF, D_MODEL), BF16),
            jax.ShapeDtypeStruct((D_MODEL, D_MODEL), BF16),
            jax.ShapeDtypeStruct((D_MODEL, g0 + 128), BF16),
        ],
        scratch_shapes=resident + staging + [pltpu.VMEM((2, tm, D_MODEL), BF16),
                                             pltpu.VMEM((tm, D_MODEL), F32)],
        compiler_params=pltpu.CompilerParams(
            dimension_semantics=("arbitrary",), vmem_limit_bytes=VMEM_LIMIT),
        name="ffn",
    )(x, x, gpre, gpost, w_in, w_out, w2_in, w2_out, w_mix_out, w_mix_in_t, w_mix_in_t)


def _mix_ffn(x, bg, u, conv_w, conv_b, y_mlstm, gmix, w_mix_out, gpre, gpost, w_in, w_out, seq, tm):
    T = x.shape[0]
    tiles_per_seq = seq // tm
    sub = tm // HALO_ROWS
    last = T // HALO_ROWS - 1
    row = lambda w: pl.BlockSpec((tm, w), lambda i: (i, 0))
    return pl.pallas_call(
        functools.partial(_mix_ffn_kernel, tiles_per_seq),
        grid=(T // tm,),
        in_specs=[
            row(D_MODEL), row(CONV_WIDTH), row(CONV_WIDTH),
            pl.BlockSpec((HALO_ROWS, CONV_WIDTH), lambda i: (jnp.maximum(i * sub - 1, 0), 0)),
            pl.BlockSpec((HALO_ROWS, CONV_WIDTH), lambda i: (jnp.minimum((i + 1) * sub, last), 0)),
            _resident((3, CONV_WIDTH)), _resident((1, CONV_WIDTH)),
            row(HEADS * DV), _resident((1, D_MODEL)), _resident(w_mix_out.shape),
            _resident((1, D_MODEL)), _resident((1, D_MODEL)), _resident(w_in.shape), _resident(w_out.shape),
        ],
        out_specs=row(D_MODEL),
        out_shape=jax.ShapeDtypeStruct((T, D_MODEL), F32),
        scratch_shapes=[pltpu.VMEM((tm, D_FF), BF16)],
        compiler_params=pltpu.CompilerParams(
            dimension_semantics=("arbitrary",), vmem_limit_bytes=VMEM_LIMIT),
        name="mix_ffn",
    )(x, bg, u, u, u, conv_w, conv_b, y_mlstm, gmix, w_mix_out, gpre, gpost, w_in, w_out)


def _gate_rows(zr, p_ref, ccol_ref):
    L = CHUNK
    n_chunks = zr.shape[1] // L
    fwd_row = lax.broadcasted_iota(jnp.int32, (NGR, L), 0) < HEADS
    fwd_col = lax.broadcasted_iota(jnp.int32, (NGR, 1), 0) < HEADS
    li = zr[0:NGR]
    gates = jnp.concatenate([li, _log_sigmoid(zr[NGR:2 * NGR])], axis=0)
    x3 = jnp.concatenate(_split3(gates), axis=0)
    stacked = jnp.concatenate([x3[:, c * L:(c + 1) * L] for c in range(n_chunks)], axis=0)
    cum_ops = jnp.concatenate([_visible(False), _visible(True)], axis=1).astype(BF16)
    cum = jnp.dot(stacked, cum_ops, preferred_element_type=F32)
    b_chunks = []
    for c in range(n_chunks):
        blk = cum[c * 48:(c + 1) * 48]
        s16 = blk[0:16] + blk[16:32] + blk[32:48]
        b_chunks.append(jnp.where(fwd_row, s16[NGR:2 * NGR, 0:L], s16[NGR:2 * NGR, L:2 * L]))
    b = jnp.concatenate(b_chunks, axis=1)
    cc = li - b
    fwd_all = lax.broadcasted_iota(jnp.int32, cc.shape, 0) < HEADS
    p_ref[P_M, :] = jnp.where(fwd_all, _running_max(cc, False), _running_max(cc, True))
    p_ref[P_B, :] = b
    pad = jnp.zeros((L - NGR, L), F32)
    for c in range(n_chunks):
        sl = slice(c * L, (c + 1) * L)
        b_c, cc_c = b_chunks[c], cc[:, sl]
        g = jnp.where(fwd_col, b_c[:, L - 1:L], b_c[:, 0:1])
        m_chunk = g + jnp.max(cc_c, axis=1, keepdims=True)
        p_ref[P_E, sl] = jnp.exp(g + cc_c - m_chunk)
        p_ref[P_G, sl] = jnp.broadcast_to(g, (NGR, L))
        p_ref[P_MC, sl] = jnp.broadcast_to(m_chunk, (NGR, L))
        ccol_ref[sl, :] = jnp.concatenate([cc_c, pad], axis=0).T


def _mix_in_kernel(x_ref, gpre_ref, wb_ref, gbias_ref,
                   bg_ref, u_ref, qt_ref, k_ref, ksw_ref, vt_ref, o_ref, p_ref, ccol_ref):
    W, HK, HV = CONV_WIDTH, HEADS * DK, HEADS * DV
    q0 = 3 * W
    k0, v0 = q0 + HK, q0 + 2 * HK
    o0 = v0 + HV
    g0 = o0 + HV

    xn = _rms(x_ref[...], gpre_ref[...]).astype(BF16)
    proj = lambda a, b: jnp.dot(xn, wb_ref[:, a:b], preferred_element_type=F32)

    zg = proj(g0, g0 + 128) + gbias_ref[...]
    bg_ref[...] = proj(0, W)
    u_ref[...] = proj(W, 2 * W) * proj(2 * W, 3 * W)
    _gate_rows(zg.T[0:2 * NGR], p_ref, ccol_ref)
    qt_ref[...] = (proj(q0, k0) * (DK ** -0.5)).T.astype(BF16)
    kk = proj(k0, v0)
    k_ref[...] = kk.astype(BF16)
    for pair in range(HEADS // 2):
        ps = slice(pair * 2 * DK, (pair + 1) * 2 * DK)
        ksw_ref[:, ps] = pltpu.roll(kk[:, ps], DK, 1).astype(BF16)
    vt_ref[...] = proj(v0, o0).T.astype(BF16)
    o_ref[...] = proj(o0, g0)


def _mix_in(x, gpre, wb, gbias, tm):
    T = x.shape[0]
    row = lambda w: pl.BlockSpec((tm, w), lambda i: (i, 0))
    col = lambda h: pl.BlockSpec((h, tm), lambda i: (0, i))
    HK, HV = HEADS * DK, HEADS * DV
    return pl.pallas_call(
        _mix_in_kernel,
        grid=(T // tm,),
        in_specs=[row(D_MODEL), _resident((1, D_MODEL)), _resident(wb.shape), _resident((1, 128))],
        out_specs=[row(CONV_WIDTH), row(CONV_WIDTH), col(HK), row(HK), row(HK), col(HV), row(HV),
                   col(5 * NGR), row(128)],
        out_shape=[
            jax.ShapeDtypeStruct((T, CONV_WIDTH), F32),
            jax.ShapeDtypeStruct((T, CONV_WIDTH), F32),
            jax.ShapeDtypeStruct((HK, T), BF16),
            jax.ShapeDtypeStruct((T, HK), BF16),
            jax.ShapeDtypeStruct((T, HK), BF16),
            jax.ShapeDtypeStruct((HV, T), BF16),
            jax.ShapeDtypeStruct((T, HV), F32),
            jax.ShapeDtypeStruct((5 * NGR, T), F32),
            jax.ShapeDtypeStruct((T, 128), F32),
        ],
        compiler_params=pltpu.CompilerParams(
            dimension_semantics=("arbitrary",), vmem_limit_bytes=VMEM_LIMIT),
        name="mix_in",
    )(x, gpre, wb, gbias)


def _state_half(d, h):
    return (h % 2) ^ d


def _scan_direction(d, vt_ref, k_ref, p_ref, ct_ref, mp_ref, ct_state, m_state):
    L = CHUNK
    n_chunks = k_ref.shape[0] // L
    lane_half = lax.broadcasted_iota(jnp.int32, (L, 128), 1) // DK
    lane_half_s = lax.broadcasted_iota(jnp.int32, (DVA, 128), 1) // DK
    ones_rows = jnp.ones((DVA - DV, L), BF16)

    cts = [ct_state[d, h] for h in range(HEADS)]
    m_prev = m_state[d]
    for chunk in (range(n_chunks - 1, -1, -1) if d == 1 else range(n_chunks)):
        sl = slice(chunk * L, (chunk + 1) * L)
        e, g, m_chunk = p_ref[P_E, sl], p_ref[P_G, sl], p_ref[P_MC, sl]
        mp_ref[:, sl] = m_prev
        m_new = jnp.maximum(g + m_prev, m_chunk)
        a_old = jnp.exp(g + m_prev - m_new)
        a_new = jnp.exp(m_chunk - m_new)
        new_cts = []
        for h in range(HEADS):
            r = HEADS * d + h
            vt_aug = jnp.concatenate([vt_ref[h * DV:(h + 1) * DV, sl], ones_rows], axis=0)
            vte = (vt_aug.astype(F32) * e[r:r + 1, :]).astype(BF16)
            pair = slice((h // 2) * 2 * DK, (h // 2 + 1) * 2 * DK)
            k_half = jnp.where(lane_half == _state_half(d, h), k_ref[sl, pair], jnp.zeros((L, 128), BF16))
            ct_chunk = jnp.dot(vte, k_half, preferred_element_type=F32)
            new_cts.append(a_old[r:r + 1, 0:1] * cts[h] + a_new[r:r + 1, 0:1] * ct_chunk)
        for pr in range(HEADS // 2):
            both = jnp.where(lane_half_s == _state_half(d, 2 * pr), cts[2 * pr], cts[2 * pr + 1])
            ct_ref[chunk, pr * DVA:(pr + 1) * DVA, :] = both.astype(BF16)
        cts = new_cts
        m_prev = m_new
    for h in range(HEADS):
        ct_state[d, h] = cts[h]
    m_state[d] = m_prev


def _mlstm_scan_kernel(vt_f, k_f, p_f, vt_b, ksw_b, p_b, ct_f, mp_f, ct_b, mp_b, ct_state, m_state):
    @pl.when(pl.program_id(1) == 0)
    def _():
        ct_state[...] = jnp.zeros(ct_state.shape, F32)
        m_state[...] = jnp.full(m_state.shape, NEG_INF, F32)

    _scan_direction(0, vt_f, k_f, p_f, ct_f, mp_f, ct_state, m_state)
    _scan_direction(1, vt_b, ksw_b, p_b, ct_b, mp_b, ct_state, m_state)


def _mlstm_out_kernel(qt_ref, k_ref, vt_ref, p_ref, mp_f, mp_b, ccol_ref, ct_f, ct_b,
                      o_ref, gain_ref, y_ref):
    L = CHUNK
    n_chunks = k_ref.shape[0] // L
    visible = (_visible(False), _visible(True))
    fwd_row = lax.broadcasted_iota(jnp.int32, (NGR, L), 0) < HEADS
    lane_half = lax.broadcasted_iota(jnp.int32, (DVA, 128), 1) // DK
    ones_rows = jnp.ones((DVA - DV, L), BF16)
    zq = jnp.zeros((DK, L), BF16)
    for chunk in range(n_chunks):
        sl = slice(chunk * L, (chunk + 1) * L)
        m_prev = jnp.where(fwd_row, mp_f[:, sl], mp_b[:, sl])
        n_t = jnp.maximum(m_prev, p_ref[P_M, sl])
        f_inter = jnp.exp(m_prev - n_t)
        e_min = jnp.exp(-(p_ref[P_B, sl] + n_t))
        ccol = ccol_ref[sl, :]
        for h in range(HEADS):
            hs = slice(h * DV, (h + 1) * DV)
            qt = qt_ref[h * DK:(h + 1) * DK, sl]
            k_pair = k_ref[sl, (h // 2) * 2 * DK:(h // 2 + 1) * 2 * DK]
            qt_pair = jnp.concatenate([qt, zq] if h % 2 == 0 else [zq, qt], axis=0)
            st = jnp.dot(k_pair, qt_pair, preferred_element_type=F32)
            pts, qfs = [], []
            for d in range(2):
                r = HEADS * d + h
                arg = jnp.where(visible[d], ccol[:, r:r + 1] - n_t[r:r + 1, :], NEG_INF)
                pts.append((jnp.exp(arg) * st).astype(BF16))
                qfs.append((qt.astype(F32) * f_inter[r:r + 1, :]).astype(BF16))
            inter = [jnp.concatenate([qfs[0], zq], axis=1), jnp.concatenate([zq, qfs[1]], axis=1)]
            first = 0 if _state_half(0, h) == 0 else 1
            rhs = jnp.concatenate([jnp.concatenate(pts, axis=1), inter[first], inter[1 - first]], axis=0)
            vt_aug = jnp.concatenate([vt_ref[hs, sl], ones_rows], axis=0)
            ps = slice((h // 2) * DVA, (h // 2 + 1) * DVA)
            blocks = (ct_f[chunk, ps, :], ct_b[chunk, ps, :])
            ct = jnp.where(lane_half == 0, blocks[first], blocks[1 - first])
            both = jnp.dot(jnp.concatenate([vt_aug, ct], axis=1), rhs, preferred_element_type=F32)
            ht = None
            for d in range(2):
                r = HEADS * d + h
                numer = both[0:DV, d * L:(d + 1) * L]
                denom = both[DV:DV + 1, d * L:(d + 1) * L]
                part = numer / jnp.maximum(jnp.abs(denom), e_min[r:r + 1, :])
                ht = part if ht is None else ht + part
            ms = jnp.mean(ht * ht, axis=0, keepdims=True)
            hn = ht * lax.rsqrt(ms + EPS) * gain_ref[hs, :]
            y_ref[sl, hs] = (jax.nn.sigmoid(o_ref[sl, hs]) * hn.T).astype(BF16)


def _mlstm(qt, k, ksw, vt, p, ccol, o, gain_b, batch, seq, scan_rows, rows):
    T = batch * seq
    ng = seq // scan_rows
    HK, HV = HEADS * DK, HEADS * DV

    def scan_specs(group_of):
        blk = lambda b, j: b * ng + group_of(j)
        ins = [pl.BlockSpec((HV, scan_rows), lambda b, j: (0, blk(b, j))),
               pl.BlockSpec((scan_rows, HK), lambda b, j: (blk(b, j), 0)),
               pl.BlockSpec((5 * NGR, scan_rows), lambda b, j: (0, blk(b, j)))]
        outs = [pl.BlockSpec((scan_rows // CHUNK, CT_ROWS, 128), lambda b, j: (blk(b, j), 0, 0)),
                pl.BlockSpec((NGR, scan_rows), lambda b, j: (0, blk(b, j)))]
        return ins, outs

    ins_f, outs_f = scan_specs(lambda j: j)
    ins_b, outs_b = scan_specs(lambda j: ng - 1 - j)
    state_shapes = [jax.ShapeDtypeStruct((T // CHUNK, CT_ROWS, 128), BF16),
                    jax.ShapeDtypeStruct((NGR, T), F32)]
    if ng == 1:
        scan_kernel = lambda vt_r, k_r, ksw_r, p_r, *rest: _mlstm_scan_kernel(vt_r, k_r, p_r, vt_r, ksw_r, p_r, *rest)
        scan_in_specs, scan_args = [ins_f[0], ins_f[1], ins_f[1], ins_f[2]], (vt, k, ksw, p)
    else:
        scan_kernel, scan_in_specs, scan_args = _mlstm_scan_kernel, ins_f + ins_b, (vt, k, p, vt, ksw, p)
    ct_f, mp_f, ct_b, mp_b = pl.pallas_call(
        scan_kernel,
        grid=(batch, ng),
        in_specs=scan_in_specs,
        out_specs=outs_f + outs_b,
        out_shape=state_shapes + state_shapes,
        scratch_shapes=[pltpu.VMEM((2, HEADS, DVA, 128), F32), pltpu.VMEM((2, NGR, 128), F32)],
        compiler_params=pltpu.CompilerParams(dimension_semantics=("arbitrary", "arbitrary"),
                                             vmem_limit_bytes=VMEM_LIMIT),
        name="mlstm_scan",
    )(*scan_args)

    row = lambda w: pl.BlockSpec((rows, w), lambda i: (i, 0))
    col = lambda h: pl.BlockSpec((h, rows), lambda i: (0, i))
    ctb = pl.BlockSpec((rows // CHUNK, CT_ROWS, 128), lambda i: (i, 0, 0))
    return pl.pallas_call(
        _mlstm_out_kernel,
        grid=(T // rows,),
        in_specs=[col(HK), row(HK), col(HV), col(5 * NGR), col(NGR), col(NGR), row(128), ctb, ctb,
                  row(HV), _resident((HV, 128))],
        out_specs=row(HV),
        out_shape=jax.ShapeDtypeStruct((T, HV), BF16),
        compiler_params=pltpu.CompilerParams(dimension_semantics=("arbitrary",),
                                             vmem_limit_bytes=VMEM_LIMIT),
        name="mlstm_out",
    )(qt, k, vt, p, mp_f, mp_b, ccol, ct_f, ct_b, o, gain_b)


def kernel(x, norm_ffn1_pre, norm_ffn1_post, w_ffn1_in, w_ffn1_out, norm_mix_pre, norm_mix_post,
           w_mix_in, conv_w, conv_b, gate_i_bias, gate_f_bias, mlstm_norm, w_mix_out,
           norm_ffn2_pre, norm_ffn2_post, w_ffn2_in, w_ffn2_out):
    batch, seq, _ = x.shape
    T = batch * seq
    depth = norm_ffn1_pre.shape[0]
    tm = 512
    xt = x.reshape(T, D_MODEL)
    HV = HEADS * DV
    for l in range(depth):
        xt, w2_in, w2_out, w_mo, w_mi = _ffn(
            xt, norm_ffn1_pre[l][None], norm_ffn1_post[l][None], w_ffn1_in, w_ffn1_out,
            w_ffn2_in, w_ffn2_out, w_mix_out, jnp.swapaxes(w_mix_in, 1, 2), l, tm)

        gbias = jnp.pad(jnp.concatenate([gate_i_bias[l], gate_f_bias[l]]), (0, 128 - 2 * NGR))[None]
        bg, u, qt, k, ksw, vt, o, p, ccol = _mix_in(xt, norm_mix_pre[l][None], w_mi, gbias, 2 * tm)
        gain_b = jnp.broadcast_to(mlstm_norm[l][:, None], (HV, 128))
        y_mlstm = _mlstm(qt, k, ksw, vt, p, ccol, o, gain_b, batch, seq, SCAN_ROWS, MLSTM_ROWS)
        xt = _mix_ffn(xt, bg, u, conv_w[l], conv_b[l][None], y_mlstm, norm_mix_post[l][None], w_mo,
                      norm_ffn2_pre[l][None], norm_ffn2_post[l][None], w2_in, w2_out, seq, tm)
    return xt.reshape(batch, seq, D_MODEL)
```

```python
import functools

import jax
import jax.numpy as jnp
from jax import lax
from jax.experimental import pallas as pl
from jax.experimental.pallas import tpu as pltpu

D_MODEL = 1024
D_FF = 2816
CONV_WIDTH = 512
HEADS = 4
DK = 64
DV = 128
CHUNK = 128
EPS = 1e-6
NEG_INF = -1e30

FF_TILE = 256
FF_STAGE_CHUNKS = 8
FF_SIDE_CHUNKS = 8
MIX_OUT_PIECES = 4
FF_LEAD_TILES = 1
MIX_W_BLOCKS = (3 * CONV_WIDTH + 2 * HEADS * DK + 2 * HEADS * DV) // 128
MLSTM_ROWS = 2048
SCAN_ROWS = 4096
DVA = DV + 16
HALO_ROWS = 8
NGR = 2 * HEADS
CT_ROWS = (HEADS // 2) * DVA
VMEM_LIMIT = 56 * 1024 * 1024

P_E, P_M, P_B, P_G, P_MC = (slice(i * NGR, (i + 1) * NGR) for i in range(5))

F32 = jnp.float32
BF16 = jnp.bfloat16


def _rms(x, g):
    return x * lax.rsqrt(jnp.mean(x * x, axis=-1, keepdims=True) + EPS) * g


def _log_sigmoid(z):
    return jnp.minimum(z, 0.0) - jnp.log1p(jnp.exp(-jnp.abs(z)))


def _resident(shape):
    zeros = (0,) * len(shape)
    return pl.BlockSpec(shape, lambda *_: zeros, pipeline_mode=pl.Buffered(1))


def _split3(x):
    hi = x.astype(BF16)
    r1 = x - hi.astype(F32)
    mid = r1.astype(BF16)
    lo = (r1 - mid.astype(F32)).astype(BF16)
    return hi, mid, lo


def _visible(rev):
    s = lax.broadcasted_iota(jnp.int32, (CHUNK, CHUNK), 0)
    t = lax.broadcasted_iota(jnp.int32, (CHUNK, CHUNK), 1)
    return (s >= t) if rev else (s <= t)


def _running_max(x, rev):
    n = x.shape[1]
    pos = lax.broadcasted_iota(jnp.int32, x.shape, 1) & (CHUNK - 1)
    k = 1
    while k < CHUNK:
        if rev:
            shifted, ok = pltpu.roll(x, n - k, 1), pos < CHUNK - k
        else:
            shifted, ok = pltpu.roll(x, k, 1), pos >= k
        x = jnp.maximum(x, jnp.where(ok, shifted, NEG_INF))
        k *= 2
    return x


def _stage_bf16(src_hbm, dst_ref, stage_ref, sem, rows=None):
    rows = stage_ref.shape[1] if rows is None else rows
    n_chunks = src_hbm.shape[0] // rows

    def copy(c):
        return pltpu.make_async_copy(src_hbm.at[pl.ds(c * rows, rows), :],
                                     stage_ref.at[c % 2, pl.ds(0, rows), :], sem.at[c % 2])

    copy(0).start()
    for c in range(n_chunks):
        if c + 1 < n_chunks:
            copy(c + 1).start()
        copy(c).wait()
        dst_ref[c * rows:(c + 1) * rows, :] = stage_ref[c % 2, 0:rows, :].astype(BF16)


def _mix_out_tile(tiles_per_seq, x_ref, bg_ref, u_ref, uprev_ref, unext_ref, cw_ref, cb_ref,
                  y_ref, wmo_ref, gmix_ref, gpre_ref):
    i = pl.program_id(0)
    tm = u_ref.shape[0]
    u = u_ref[...]
    has_prev = (i % tiles_per_seq != 0).astype(F32)
    has_next = (i % tiles_per_seq != tiles_per_seq - 1).astype(F32)
    prev_row = uprev_ref[HALO_ROWS - 1:HALO_ROWS, :] * has_prev
    next_row = unext_ref[0:1, :] * has_next
    ri = lax.broadcasted_iota(jnp.int32, u.shape, 0)
    u_m1 = jnp.where(ri == 0, prev_row, pltpu.roll(u, 1, 0))
    u_p1 = jnp.where(ri == tm - 1, next_row, pltpu.roll(u, tm - 1, 0))
    conv = cw_ref[0:1, :] * u_m1 + cw_ref[1:2, :] * u + cw_ref[2:3, :] * u_p1
    y_conv = (bg_ref[...] * (conv + cb_ref[...])).astype(BF16)
    rp = tm // MIX_OUT_PIECES
    xs, xns = [], []
    for r in range(MIX_OUT_PIECES):
        rs = slice(r * rp, (r + 1) * rp)
        h = jnp.dot(y_ref[rs, :], wmo_ref[CONV_WIDTH:, :], preferred_element_type=F32) \
            + jnp.dot(y_conv[rs, :], wmo_ref[0:CONV_WIDTH, :], preferred_element_type=F32)
        xs.append(x_ref[rs, :] + _rms(h, gmix_ref[...]))
        xns.append(_rms(xs[-1], gpre_ref[...]).astype(BF16))
    return jnp.concatenate(xs, axis=0), xns


def _ffn_tile(x, xn_pieces, gpost_ref, win_ref, wout_ref, h_ref):
    rp = xn_pieces[0].shape[0]
    xn = jnp.concatenate(xn_pieces, axis=0)
    for j in range(D_FF // FF_TILE):
        lo = j * FF_TILE
        lhs = list(enumerate(xn_pieces)) if j < FF_LEAD_TILES and len(xn_pieces) > 1 else [(None, xn)]
        for r, xr in lhs:
            rs = slice(None) if r is None else slice(r * rp, (r + 1) * rp)
            gate = jnp.dot(xr, win_ref[:, lo:lo + FF_TILE], preferred_element_type=F32)
            up = jnp.dot(xr, win_ref[:, D_FF + lo:D_FF + lo + FF_TILE], preferred_element_type=F32)
            h_ref[rs, lo:lo + FF_TILE] = (gate * jax.nn.sigmoid(gate) * up).astype(BF16)
    y = jnp.dot(h_ref[...], wout_ref[...], preferred_element_type=F32)
    return x + _rms(y, 0.5 * gpost_ref[...])


def _zero_after(v):
    u = pltpu.bitcast(v, jnp.uint32)
    acc = None
    for r in range(u.shape[0] // 8):
        for c in range(u.shape[1] // 128):
            t = u[r * 8:(r + 1) * 8, c * 128:(c + 1) * 128]
            acc = t if acc is None else acc | t
    return pltpu.bitcast((acc >> 16) >> 16, F32)


def _convert_later_weights(i, w2in_blk, w2out_blk, wmo_blk, wt_blk, wtg_blk, w2in_bf, w2out_bf, wmo_bf, wb_blk):
    zeros = []
    for src, dst in ((w2in_blk, w2in_bf), (w2out_blk, w2out_bf), (wmo_blk, wmo_bf)):
        v = src[...].astype(BF16)
        dst[...] = v
        zeros.append(_zero_after(v[0:16, 0:128]))
    gates = jnp.concatenate([wtg_blk[...], jnp.zeros((128 - 2 * NGR, D_MODEL), F32)], axis=0)
    last = jnp.clip(i - 1, 0, MIX_W_BLOCKS) == MIX_W_BLOCKS
    blk = jnp.where(last, gates, wt_blk[...]).T
    wb_blk[...] = blk.astype(BF16)
    zeros.append(_zero_after(blk))
    return zeros


def _ffn_kernel(layer, n_tiles, xa_ref, xc_ref, gpre_ref, gpost_ref, win_hbm, wout_hbm,
                w2in_blk, w2out_blk, wmo_blk, wt_blk, wtg_blk,
                o_ref, w2in_bf, w2out_bf, wmo_bf, wb_blk,
                h_ref, win_ref, wout_ref, stage_in, stage_out, sem, xn_ref, y_ref):
    i = pl.program_id(0)
    tm = xa_ref.shape[0]
    g_half = 0.5 * gpost_ref[...]

    @pl.when(i == 0)
    def _():
        _stage_bf16(win_hbm.at[layer], win_ref, stage_in, sem)
        _stage_bf16(wout_hbm.at[layer], wout_ref, stage_out, sem)
        xn_ref[0] = _rms(xa_ref[...], gpre_ref[...]).astype(BF16)
        y_ref[...] = jnp.zeros(y_ref.shape, F32)

    @pl.when((i >= 1) & (i <= n_tiles))
    def _():
        xn = xn_ref[(i - 1) % 2]
        rp = tm // FF_SIDE_CHUNKS
        for j in range(D_FF // FF_TILE):
            lo = j * FF_TILE
            gate = jnp.dot(xn, win_ref[:, lo:lo + FF_TILE], preferred_element_type=F32)
            up = jnp.dot(xn, win_ref[:, D_FF + lo:D_FF + lo + FF_TILE], preferred_element_type=F32)
            hm = gate * jax.nn.sigmoid(gate) * up
            h_ref[:, lo:lo + FF_TILE] = hm.astype(BF16)
            if j < FF_SIDE_CHUNKS:
                rs = slice(j * rp, (j + 1) * rp)
                out_rows = xc_ref[rs, :] + _rms(y_ref[rs, :], g_half)
                o_ref[rs, :] = out_rows
                xn_rows = _rms(xa_ref[rs, :], gpre_ref[...])
                xn_ref[i % 2, rs, :] = xn_rows.astype(BF16)
                zero = _zero_after(out_rows) + _zero_after(xn_rows)
                h_ref[0:8, lo:lo + 128] = (hm[0:8, 0:128] + zero).astype(BF16)
            else:
                if j == FF_SIDE_CHUNKS:
                    side = _convert_later_weights(i, w2in_blk, w2out_blk, wmo_blk, wt_blk, wtg_blk,
                                                  w2in_bf, w2out_bf, wmo_bf, wb_blk)
                    side = [side[0], side[1] + side[2], side[3]]
                h_ref[0:8, lo:lo + 128] = (hm[0:8, 0:128] + side[j - FF_SIDE_CHUNKS]).astype(BF16)
        y_ref[...] = jnp.dot(h_ref[...], wout_ref[...], preferred_element_type=F32)

    @pl.when(i == n_tiles + 1)
    def _():
        o_ref[...] = xc_ref[...] + _rms(y_ref[...], g_half)


def _mix_ffn_kernel(tiles_per_seq, x_ref, bg_ref, u_ref, uprev_ref, unext_ref, cw_ref, cb_ref,
                    y_ref, gmix_ref, wmo_ref, gpre_ref, gpost_ref, win_ref, wout_ref, o_ref, h_ref):
    x, xn = _mix_out_tile(tiles_per_seq, x_ref, bg_ref, u_ref, uprev_ref, unext_ref, cw_ref, cb_ref,
                          y_ref, wmo_ref, gmix_ref, gpre_ref)
    o_ref[...] = _ffn_tile(x, xn, gpost_ref, win_ref, wout_ref, h_ref)


def _ffn_scratch(tm):
    return [
        pltpu.VMEM((tm, D_FF), BF16),
        pltpu.VMEM((D_MODEL, 2 * D_FF), BF16),
        pltpu.VMEM((D_FF, D_MODEL), BF16),
    ], [
        pltpu.VMEM((2, D_MODEL // FF_STAGE_CHUNKS, 2 * D_FF), F32),
        pltpu.VMEM((2, D_FF // FF_STAGE_CHUNKS, D_MODEL), F32),
        pltpu.SemaphoreType.DMA((2,)),
    ]


def _ffn(x, gpre, gpost, w_in, w_out, w2_in, w2_out, w_mix_out, w_mix_in_t, layer, tm):
    T = x.shape[0]
    nt = T // tm
    assert nt > MIX_W_BLOCKS and nt % 2 == 0, f"first FFN needs more than {MIX_W_BLOCKS} token tiles, got {nt}"
    g0 = MIX_W_BLOCKS * 128
    resident, staging = _ffn_scratch(tm)
    tile = lambda lag: pl.BlockSpec((tm, D_MODEL), lambda i: (jnp.clip(i - lag, 0, nt - 1), 0))
    rows_in = lambda n, w, last: pl.BlockSpec((None, n, w), lambda i: (layer, jnp.clip(i - 1, 0, last), 0))
    rows_out = lambda n, w, last: pl.BlockSpec((n, w), lambda i: (jnp.clip(i - 1, 0, last), 0))
    r_in, r_out, r_mo = D_MODEL // nt, D_FF // (nt // 2), D_MODEL // (nt // 2)
    return pl.pallas_call(
        functools.partial(_ffn_kernel, layer, nt),
        grid=(nt + 2,),
        in_specs=[
            tile(0), tile(2),
            _resident((1, D_MODEL)),
            _resident((1, D_MODEL)),
            pl.BlockSpec(memory_space=pl.ANY),
            pl.BlockSpec(memory_space=pl.ANY),
            rows_in(r_in, 2 * D_FF, nt - 1),
            rows_in(r_out, D_MODEL, nt // 2 - 1),
            rows_in(r_mo, D_MODEL, nt // 2 - 1),
            rows_in(128, D_MODEL, MIX_W_BLOCKS - 1),
            pl.BlockSpec((None, 2 * NGR, D_MODEL), lambda i: (layer, g0 // (2 * NGR), 0)),
        ],
        out_specs=[
            tile(2),
            rows_out(r_in, 2 * D_FF, nt - 1),
            rows_out(r_out, D_MODEL, nt // 2 - 1),
            rows_out(r_mo, D_MODEL, nt // 2 - 1),
            pl.BlockSpec((D_MODEL, 128), lambda i: (0, jnp.clip(i - 1, 0, MIX_W_BLOCKS))),
        ],
        out_shape=[
            jax.ShapeDtypeStruct((T, D_MODEL), F32),
            jax.ShapeDtypeStruct((D_MODEL, 2 * D_FF), BF16),
            jax.ShapeDtypeStruct((D_FF, D_MODEL), BF16),
            jax.ShapeDtypeStruct((D_MODEL, D_MODEL), BF16),
            jax.ShapeDtypeStruct((D_MODEL, g0 + 128), BF16),
        ],
        scratch_shapes=resident + staging + [pltpu.VMEM((2, tm, D_MODEL), BF16),
                                             pltpu.VMEM((tm, D_MODEL), F32)],
        compiler_params=pltpu.CompilerParams(
            dimension_semantics=("arbitrary",), vmem_limit_bytes=VMEM_LIMIT),
        name="ffn",
    )(x, x, gpre, gpost, w_in, w_out, w2_in, w2_out, w_mix_out, w_mix_in_t, w_mix_in_t)


def _mix_ffn(x, bg, u, conv_w, conv_b, y_mlstm, gmix, w_mix_out, gpre, gpost, w_in, w_out, seq, tm):
    T = x.shape[0]
    tiles_per_seq = seq // tm
    sub = tm // HALO_ROWS
    last = T // HALO_ROWS - 1
    row = lambda w: pl.BlockSpec((tm, w), lambda i: (i, 0))
    return pl.pallas_call(
        functools.partial(_mix_ffn_kernel, tiles_per_seq),
        grid=(T // tm,),
        in_specs=[
            row(D_MODEL), row(CONV_WIDTH), row(CONV_WIDTH),
            pl.BlockSpec((HALO_ROWS, CONV_WIDTH), lambda i: (jnp.maximum(i * sub - 1, 0), 0)),
            pl.BlockSpec((HALO_ROWS, CONV_WIDTH), lambda i: (jnp.minimum((i + 1) * sub, last), 0)),
            _resident((3, CONV_WIDTH)), _resident((1, CONV_WIDTH)),
            row(HEADS * DV), _resident((1, D_MODEL)), _resident(w_mix_out.shape),
            _resident((1, D_MODEL)), _resident((1, D_MODEL)), _resident(w_in.shape), _resident(w_out.shape),
        ],
        out_specs=row(D_MODEL),
        out_shape=jax.ShapeDtypeStruct((T, D_MODEL), F32),
        scratch_shapes=[pltpu.VMEM((tm, D_FF), BF16)],
        compiler_params=pltpu.CompilerParams(
            dimension_semantics=("arbitrary",), vmem_limit_bytes=VMEM_LIMIT),
        name="mix_ffn",
    )(x, bg, u, u, u, conv_w, conv_b, y_mlstm, gmix, w_mix_out, gpre, gpost, w_in, w_out)


def _gate_rows(zr, p_ref, ccol_ref):
    L = CHUNK
    n_chunks = zr.shape[1] // L
    fwd_row = lax.broadcasted_iota(jnp.int32, (NGR, L), 0) < HEADS
    fwd_col = lax.broadcasted_iota(jnp.int32, (NGR, 1), 0) < HEADS
    li = zr[0:NGR]
    gates = jnp.concatenate([li, _log_sigmoid(zr[NGR:2 * NGR])], axis=0)
    x3 = jnp.concatenate(_split3(gates), axis=0)
    stacked = jnp.concatenate([x3[:, c * L:(c + 1) * L] for c in range(n_chunks)], axis=0)
    cum_ops = jnp.concatenate([_visible(False), _visible(True)], axis=1).astype(BF16)
    cum = jnp.dot(stacked, cum_ops, preferred_element_type=F32)
    b_chunks = []
    for c in range(n_chunks):
        blk = cum[c * 48:(c + 1) * 48]
        s16 = blk[0:16] + blk[16:32] + blk[32:48]
        b_chunks.append(jnp.where(fwd_row, s16[NGR:2 * NGR, 0:L], s16[NGR:2 * NGR, L:2 * L]))
    b = jnp.concatenate(b_chunks, axis=1)
    cc = li - b
    fwd_all = lax.broadcasted_iota(jnp.int32, cc.shape, 0) < HEADS
    p_ref[P_M, :] = jnp.where(fwd_all, _running_max(cc, False), _running_max(cc, True))
    p_ref[P_B, :] = b
    pad = jnp.zeros((L - NGR, L), F32)
    for c in range(n_chunks):
        sl = slice(c * L, (c + 1) * L)
        b_c, cc_c = b_chunks[c], cc[:, sl]
        g = jnp.where(fwd_col, b_c[:, L - 1:L], b_c[:, 0:1])
        m_chunk = g + jnp.max(cc_c, axis=1, keepdims=True)
        p_ref[P_E, sl] = jnp.exp(g + cc_c - m_chunk)
        p_ref[P_G, sl] = jnp.broadcast_to(g, (NGR, L))
        p_ref[P_MC, sl] = jnp.broadcast_to(m_chunk, (NGR, L))
        ccol_ref[sl, :] = jnp.concatenate([cc_c, pad], axis=0).T


def _mix_in_kernel(x_ref, gpre_ref, wb_ref, gbias_ref,
                   bg_ref, u_ref, qt_ref, k_ref, ksw_ref, vt_ref, o_ref, p_ref, ccol_ref):
    W, HK, HV = CONV_WIDTH, HEADS * DK, HEADS * DV
    q0 = 3 * W
    k0, v0 = q0 + HK, q0 + 2 * HK
    o0 = v0 + HV
    g0 = o0 + HV

    xn = _rms(x_ref[...], gpre_ref[...]).astype(BF16)
    proj = lambda a, b: jnp.dot(xn, wb_ref[:, a:b], preferred_element_type=F32)

    zg = proj(g0, g0 + 128) + gbias_ref[...]
    bg_ref[...] = proj(0, W)
    u_ref[...] = proj(W, 2 * W) * proj(2 * W, 3 * W)
    _gate_rows(zg.T[0:2 * NGR], p_ref, ccol_ref)
    qt_ref[...] = (proj(q0, k0) * (DK ** -0.5)).T.astype(BF16)
    kk = proj(k0, v0)
    k_ref[...] = kk.astype(BF16)
    for pair in range(HEADS // 2):
        ps = slice(pair * 2 * DK, (pair + 1) * 2 * DK)
        ksw_ref[:, ps] = pltpu.roll(kk[:, ps], DK, 1).astype(BF16)
    vt_ref[...] = proj(v0, o0).T.astype(BF16)
    o_ref[...] = proj(o0, g0)


def _mix_in(x, gpre, wb, gbias, tm):
    T = x.shape[0]
    row = lambda w: pl.BlockSpec((tm, w), lambda i: (i, 0))
    col = lambda h: pl.BlockSpec((h, tm), lambda i: (0, i))
    HK, HV = HEADS * DK, HEADS * DV
    return pl.pallas_call(
        _mix_in_kernel,
        grid=(T // tm,),
        in_specs=[row(D_MODEL), _resident((1, D_MODEL)), _resident(wb.shape), _resident((1, 128))],
        out_specs=[row(CONV_WIDTH), row(CONV_WIDTH), col(HK), row(HK), row(HK), col(HV), row(HV),
                   col(5 * NGR), row(128)],
        out_shape=[
            jax.ShapeDtypeStruct((T, CONV_WIDTH), F32),
            jax.ShapeDtypeStruct((T, CONV_WIDTH), F32),
            jax.ShapeDtypeStruct((HK, T), BF16),
            jax.ShapeDtypeStruct((T, HK), BF16),
            jax.ShapeDtypeStruct((T, HK), BF16),
            jax.ShapeDtypeStruct((HV, T), BF16),
            jax.ShapeDtypeStruct((T, HV), F32),
            jax.ShapeDtypeStruct((5 * NGR, T), F32),
            jax.ShapeDtypeStruct((T, 128), F32),
        ],
        compiler_params=pltpu.CompilerParams(
            dimension_semantics=("arbitrary",), vmem_limit_bytes=VMEM_LIMIT),
        name="mix_in",
    )(x, gpre, wb, gbias)


def _state_half(d, h):
    return (h % 2) ^ d


def _scan_direction(d, vt_ref, k_ref, p_ref, ct_ref, mp_ref, ct_state, m_state):
    L = CHUNK
    n_chunks = k_ref.shape[0] // L
    lane_half = lax.broadcasted_iota(jnp.int32, (L, 128), 1) // DK
    lane_half_s = lax.broadcasted_iota(jnp.int32, (DVA, 128), 1) // DK
    ones_rows = jnp.ones((DVA - DV, L), BF16)

    cts = [ct_state[d, h] for h in range(HEADS)]
    m_prev = m_state[d]
    for chunk in (range(n_chunks - 1, -1, -1) if d == 1 else range(n_chunks)):
        sl = slice(chunk * L, (chunk + 1) * L)
        e, g, m_chunk = p_ref[P_E, sl], p_ref[P_G, sl], p_ref[P_MC, sl]
        mp_ref[:, sl] = m_prev
        m_new = jnp.maximum(g + m_prev, m_chunk)
        a_old = jnp.exp(g + m_prev - m_new)
        a_new = jnp.exp(m_chunk - m_new)
        new_cts = []
        for h in range(HEADS):
            r = HEADS * d + h
            vt_aug = jnp.concatenate([vt_ref[h * DV:(h + 1) * DV, sl], ones_rows], axis=0)
            vte = (vt_aug.astype(F32) * e[r:r + 1, :]).astype(BF16)
            pair = slice((h // 2) * 2 * DK, (h // 2 + 1) * 2 * DK)
            k_half = jnp.where(lane_half == _state_half(d, h), k_ref[sl, pair], jnp.zeros((L, 128), BF16))
            ct_chunk = jnp.dot(vte, k_half, preferred_element_type=F32)
            new_cts.append(a_old[r:r + 1, 0:1] * cts[h] + a_new[r:r + 1, 0:1] * ct_chunk)
        for pr in range(HEADS // 2):
            both = jnp.where(lane_half_s == _state_half(d, 2 * pr), cts[2 * pr], cts[2 * pr + 1])
            ct_ref[chunk, pr * DVA:(pr + 1) * DVA, :] = both.astype(BF16)
        cts = new_cts
        m_prev = m_new
    for h in range(HEADS):
        ct_state[d, h] = cts[h]
    m_state[d] = m_prev


def _mlstm_scan_kernel(vt_f, k_f, p_f, vt_b, ksw_b, p_b, ct_f, mp_f, ct_b, mp_b, ct_state, m_state):
    @pl.when(pl.program_id(1) == 0)
    def _():
        ct_state[...] = jnp.zeros(ct_state.shape, F32)
        m_state[...] = jnp.full(m_state.shape, NEG_INF, F32)

    _scan_direction(0, vt_f, k_f, p_f, ct_f, mp_f, ct_state, m_state)
    _scan_direction(1, vt_b, ksw_b, p_b, ct_b, mp_b, ct_state, m_state)


def _mlstm_out_kernel(qt_ref, k_ref, vt_ref, p_ref, mp_f, mp_b, ccol_ref, ct_f, ct_b,
                      o_ref, gain_ref, y_ref):
    L = CHUNK
    n_chunks = k_ref.shape[0] // L
    visible = (_visible(False), _visible(True))
    fwd_row = lax.broadcasted_iota(jnp.int32, (NGR, L), 0) < HEADS
    lane_half = lax.broadcasted_iota(jnp.int32, (DVA, 128), 1) // DK
    ones_rows = jnp.ones((DVA - DV, L), BF16)
    zq = jnp.zeros((DK, L), BF16)
    for chunk in range(n_chunks):
        sl = slice(chunk * L, (chunk + 1) * L)
        m_prev = jnp.where(fwd_row, mp_f[:, sl], mp_b[:, sl])
        n_t = jnp.maximum(m_prev, p_ref[P_M, sl])
        f_inter = jnp.exp(m_prev - n_t)
        e_min = jnp.exp(-(p_ref[P_B, sl] + n_t))
        ccol = ccol_ref[sl, :]
        for h in range(HEADS):
            hs = slice(h * DV, (h + 1) * DV)
            qt = qt_ref[h * DK:(h + 1) * DK, sl]
            k_pair = k_ref[sl, (h // 2) * 2 * DK:(h // 2 + 1) * 2 * DK]
            qt_pair = jnp.concatenate([qt, zq] if h % 2 == 0 else [zq, qt], axis=0)
            st = jnp.dot(k_pair, qt_pair, preferred_element_type=F32)
            pts, qfs = [], []
            for d in range(2):
                r = HEADS * d + h
                arg = jnp.where(visible[d], ccol[:, r:r + 1] - n_t[r:r + 1, :], NEG_INF)
                pts.append((jnp.exp(arg) * st).astype(BF16))
                qfs.append((qt.astype(F32) * f_inter[r:r + 1, :]).astype(BF16))
            inter = [jnp.concatenate([qfs[0], zq], axis=1), jnp.concatenate([zq, qfs[1]], axis=1)]
            first = 0 if _state_half(0, h) == 0 else 1
            rhs = jnp.concatenate([jnp.concatenate(pts, axis=1), inter[first], inter[1 - first]], axis=0)
            vt_aug = jnp.concatenate([vt_ref[hs, sl], ones_rows], axis=0)
            ps = slice((h // 2) * DVA, (h // 2 + 1) * DVA)
            blocks = (ct_f[chunk, ps, :], ct_b[chunk, ps, :])
            ct = jnp.where(lane_half == 0, blocks[first], blocks[1 - first])
            both = jnp.dot(jnp.concatenate([vt_aug, ct], axis=1), rhs, preferred_element_type=F32)
            ht = None
            for d in range(2):
                r = HEADS * d + h
                numer = both[0:DV, d * L:(d + 1) * L]
                denom = both[DV:DV + 1, d * L:(d + 1) * L]
                part = numer / jnp.maximum(jnp.abs(denom), e_min[r:r + 1, :])
                ht = part if ht is None else ht + part
            ms = jnp.mean(ht * ht, axis=0, keepdims=True)
            hn = ht * lax.rsqrt(ms + EPS) * gain_ref[hs, :]
            y_ref[sl, hs] = (jax.nn.sigmoid(o_ref[sl, hs]) * hn.T).astype(BF16)


def _mlstm(qt, k, ksw, vt, p, ccol, o, gain_b, batch, seq, scan_rows, rows):
    T = batch * seq
    ng = seq // scan_rows
    HK, HV = HEADS * DK, HEADS * DV

    def scan_specs(group_of):
        blk = lambda b, j: b * ng + group_of(j)
        ins = [pl.BlockSpec((HV, scan_rows), lambda b, j: (0, blk(b, j))),
               pl.BlockSpec((scan_rows, HK), lambda b, j: (blk(b, j), 0)),
               pl.BlockSpec((5 * NGR, scan_rows), lambda b, j: (0, blk(b, j)))]
        outs = [pl.BlockSpec((scan_rows // CHUNK, CT_ROWS, 128), lambda b, j: (blk(b, j), 0, 0)),
                pl.BlockSpec((NGR, scan_rows), lambda b, j: (0, blk(b, j)))]
        return ins, outs

    ins_f, outs_f = scan_specs(lambda j: j)
    ins_b, outs_b = scan_specs(lambda j: ng - 1 - j)
    state_shapes = [jax.ShapeDtypeStruct((T // CHUNK, CT_ROWS, 128), BF16),
                    jax.ShapeDtypeStruct((NGR, T), F32)]
    if ng == 1:
        scan_kernel = lambda vt_r, k_r, ksw_r, p_r, *rest: _mlstm_scan_kernel(vt_r, k_r, p_r, vt_r, ksw_r, p_r, *rest)
        scan_in_specs, scan_args = [ins_f[0], ins_f[1], ins_f[1], ins_f[2]], (vt, k, ksw, p)
    else:
        scan_kernel, scan_in_specs, scan_args = _mlstm_scan_kernel, ins_f + ins_b, (vt, k, p, vt, ksw, p)
    ct_f, mp_f, ct_b, mp_b = pl.pallas_call(
        scan_kernel,
        grid=(batch, ng),
        in_specs=scan_in_specs,
        out_specs=outs_f + outs_b,
        out_shape=state_shapes + state_shapes,
        scratch_shapes=[pltpu.VMEM((2, HEADS, DVA, 128), F32), pltpu.VMEM((2, NGR, 128), F32)],
        compiler_params=pltpu.CompilerParams(dimension_semantics=("arbitrary", "arbitrary"),
                                             vmem_limit_bytes=VMEM_LIMIT),
        name="mlstm_scan",
    )(*scan_args)

    row = lambda w: pl.BlockSpec((rows, w), lambda i: (i, 0))
    col = lambda h: pl.BlockSpec((h, rows), lambda i: (0, i))
    ctb = pl.BlockSpec((rows // CHUNK, CT_ROWS, 128), lambda i: (i, 0, 0))
    return pl.pallas_call(
        _mlstm_out_kernel,
        grid=(T // rows,),
        in_specs=[col(HK), row(HK), col(HV), col(5 * NGR), col(NGR), col(NGR), row(128), ctb, ctb,
                  row(HV), _resident((HV, 128))],
        out_specs=row(HV),
        out_shape=jax.ShapeDtypeStruct((T, HV), BF16),
        compiler_params=pltpu.CompilerParams(dimension_semantics=("arbitrary",),
                                             vmem_limit_bytes=VMEM_LIMIT),
        name="mlstm_out",
    )(qt, k, vt, p, mp_f, mp_b, ccol, ct_f, ct_b, o, gain_b)


def kernel(x, norm_ffn1_pre, norm_ffn1_post, w_ffn1_in, w_ffn1_out, norm_mix_pre, norm_mix_post,
           w_mix_in, conv_w, conv_b, gate_i_bias, gate_f_bias, mlstm_norm, w_mix_out,
           norm_ffn2_pre, norm_ffn2_post, w_ffn2_in, w_ffn2_out):
    batch, seq, _ = x.shape
    T = batch * seq
    depth = norm_ffn1_pre.shape[0]
    tm = 512
    xt = x.reshape(T, D_MODEL)
    HV = HEADS * DV
    for l in range(depth):
        xt, w2_in, w2_out, w_mo, w_mi = _ffn(
            xt, norm_ffn1_pre[l][None], norm_ffn1_post[l][None], w_ffn1_in, w_ffn1_out,
            w_ffn2_in, w_ffn2_out, w_mix_out, jnp.swapaxes(w_mix_in, 1, 2), l, tm)

        gbias = jnp.pad(jnp.concatenate([gate_i_bias[l], gate_f_bias[l]]), (0, 128 - 2 * NGR))[None]
        bg, u, qt, k, ksw, vt, o, p, ccol = _mix_in(xt, norm_mix_pre[l][None], w_mi, gbias, 2 * tm)
        gain_b = jnp.broadcast_to(mlstm_norm[l][:, None], (HV, 128))
        y_mlstm = _mlstm(qt, k, ksw, vt, p, ccol, o, gain_b, batch, seq, SCAN_ROWS, MLSTM_ROWS)
        xt = _mix_ffn(xt, bg, u, conv_w[l], conv_b[l][None], y_mlstm, norm_mix_post[l][None], w_mo,
                      norm_ffn2_pre[l][None], norm_ffn2_post[l][None], w2_in, w2_out, seq, tm)
    return xt.reshape(batch, seq, D_MODEL)
```

```python
import functools

import jax
import jax.numpy as jnp
from jax import lax
from jax.experimental import pallas as pl
from jax.experimental.pallas import tpu as pltpu

D_MODEL = 1024
D_FF = 2816
CONV_WIDTH = 512
HEADS = 4
DK = 64
DV = 128
CHUNK = 128
EPS = 1e-6
NEG_INF = -1e30

FF_TILE = 256
FF_STAGE_CHUNKS = 8
FF_SIDE_CHUNKS = 8
MIX_OUT_PIECES = 4
FF_LEAD_TILES = 1
MIX_W_BLOCKS = (3 * CONV_WIDTH + 2 * HEADS * DK + 2 * HEADS * DV) // 128
MLSTM_ROWS = 2048
SCAN_ROWS = 4096
DVA = DV + 16
HALO_ROWS = 8
NGR = 2 * HEADS
CT_ROWS = (HEADS // 2) * DVA
VMEM_LIMIT = 56 * 1024 * 1024

P_E, P_M, P_B, P_G, P_MC = (slice(i * NGR, (i + 1) * NGR) for i in range(5))

F32 = jnp.float32
BF16 = jnp.bfloat16


def _rms(x, g):
    return x * lax.rsqrt(jnp.mean(x * x, axis=-1, keepdims=True) + EPS) * g


def _log_sigmoid(z):
    return jnp.minimum(z, 0.0) - jnp.log1p(jnp.exp(-jnp.abs(z)))


def _resident(shape):
    zeros = (0,) * len(shape)
    return pl.BlockSpec(shape, lambda *_: zeros, pipeline_mode=pl.Buffered(1))


def _split3(x):
    hi = x.astype(BF16)
    r1 = x - hi.astype(F32)
    mid = r1.astype(BF16)
    lo = (r1 - mid.astype(F32)).astype(BF16)
    return hi, mid, lo


def _visible(rev):
    s = lax.broadcasted_iota(jnp.int32, (CHUNK, CHUNK), 0)
    t = lax.broadcasted_iota(jnp.int32, (CHUNK, CHUNK), 1)
    return (s >= t) if rev else (s <= t)


def _running_max(x, rev):
    n = x.shape[1]
    pos = lax.broadcasted_iota(jnp.int32, x.shape, 1) & (CHUNK - 1)
    k = 1
    while k < CHUNK:
        if rev:
            shifted, ok = pltpu.roll(x, n - k, 1), pos < CHUNK - k
        else:
            shifted, ok = pltpu.roll(x, k, 1), pos >= k
        x = jnp.maximum(x, jnp.where(ok, shifted, NEG_INF))
        k *= 2
    return x


def _stage_bf16(src_hbm, dst_ref, stage_ref, sem, rows=None):
    rows = stage_ref.shape[1] if rows is None else rows
    n_chunks = src_hbm.shape[0] // rows

    def copy(c):
        return pltpu.make_async_copy(src_hbm.at[pl.ds(c * rows, rows), :],
                                     stage_ref.at[c % 2, pl.ds(0, rows), :], sem.at[c % 2])

    copy(0).start()
    for c in range(n_chunks):
        if c + 1 < n_chunks:
            copy(c + 1).start()
        copy(c).wait()
        dst_ref[c * rows:(c + 1) * rows, :] = stage_ref[c % 2, 0:rows, :].astype(BF16)


def _mix_out_tile(tiles_per_seq, x_ref, bg_ref, u_ref, uprev_ref, unext_ref, cw_ref, cb_ref,
                  y_ref, wmo_ref, gmix_ref, gpre_ref):
    i = pl.program_id(0)
    tm = u_ref.shape[0]
    u = u_ref[...]
    has_prev = (i % tiles_per_seq != 0).astype(F32)
    has_next = (i % tiles_per_seq != tiles_per_seq - 1).astype(F32)
    prev_row = uprev_ref[HALO_ROWS - 1:HALO_ROWS, :] * has_prev
    next_row = unext_ref[0:1, :] * has_next
    ri = lax.broadcasted_iota(jnp.int32, u.shape, 0)
    u_m1 = jnp.where(ri == 0, prev_row, pltpu.roll(u, 1, 0))
    u_p1 = jnp.where(ri == tm - 1, next_row, pltpu.roll(u, tm - 1, 0))
    conv = cw_ref[0:1, :] * u_m1 + cw_ref[1:2, :] * u + cw_ref[2:3, :] * u_p1
    y_conv = (bg_ref[...] * (conv + cb_ref[...])).astype(BF16)
    rp = tm // MIX_OUT_PIECES
    xs, xns = [], []
    for r in range(MIX_OUT_PIECES):
        rs = slice(r * rp, (r + 1) * rp)
        h = jnp.dot(y_ref[rs, :], wmo_ref[CONV_WIDTH:, :], preferred_element_type=F32) \
            + jnp.dot(y_conv[rs, :], wmo_ref[0:CONV_WIDTH, :], preferred_element_type=F32)
        xs.append(x_ref[rs, :] + _rms(h, gmix_ref[...]))
        xns.append(_rms(xs[-1], gpre_ref[...]).astype(BF16))
    return jnp.concatenate(xs, axis=0), xns


def _ffn_tile(x, xn_pieces, gpost_ref, win_ref, wout_ref, h_ref):
    rp = xn_pieces[0].shape[0]
    xn = jnp.concatenate(xn_pieces, axis=0)
    for j in range(D_FF // FF_TILE):
        lo = j * FF_TILE
        lhs = list(enumerate(xn_pieces)) if j < FF_LEAD_TILES and len(xn_pieces) > 1 else [(None, xn)]
        for r, xr in lhs:
            rs = slice(None) if r is None else slice(r * rp, (r + 1) * rp)
            gate = jnp.dot(xr, win_ref[:, lo:lo + FF_TILE], preferred_element_type=F32)
            up = jnp.dot(xr, win_ref[:, D_FF + lo:D_FF + lo + FF_TILE], preferred_element_type=F32)
            h_ref[rs, lo:lo + FF_TILE] = (gate * jax.nn.sigmoid(gate) * up).astype(BF16)
    y = jnp.dot(h_ref[...], wout_ref[...], preferred_element_type=F32)
    return x + _rms(y, 0.5 * gpost_ref[...])


def _zero_after(v):
    u = pltpu.bitcast(v, jnp.uint32)
    acc = None
    for r in range(u.shape[0] // 8):
        for c in range(u.shape[1] // 128):
            t = u[r * 8:(r + 1) * 8, c * 128:(c + 1) * 128]
            acc = t if acc is None else acc | t
    return pltpu.bitcast((acc >> 16) >> 16, F32)


def _convert_later_weights(i, w2in_blk, w2out_blk, wmo_blk, wt_blk, wtg_blk, w2in_bf, w2out_bf, wmo_bf, wb_blk):
    zeros = []
    for src, dst in ((w2in_blk, w2in_bf), (w2out_blk, w2out_bf), (wmo_blk, wmo_bf)):
        v = src[...].astype(BF16)
        dst[...] = v
        zeros.append(_zero_after(v[0:16, 0:128]))
    gates = jnp.concatenate([wtg_blk[...], jnp.zeros((128 - 2 * NGR, D_MODEL), F32)], axis=0)
    last = jnp.clip(i - 1, 0, MIX_W_BLOCKS) == MIX_W_BLOCKS
    blk = jnp.where(last, gates, wt_blk[...]).astype(BF16)
    wb_blk[...] = blk
    zeros.append(_zero_after(blk[0:16, 0:128]))
    return zeros


def _ffn_kernel(layer, n_tiles, xa_ref, xc_ref, gpre_ref, gpost_ref, win_hbm, wout_hbm,
                w2in_blk, w2out_blk, wmo_blk, wt_blk, wtg_blk,
                o_ref, w2in_bf, w2out_bf, wmo_bf, wb_blk,
                h_ref, win_ref, wout_ref, stage_in, stage_out, sem, xn_ref, y_ref):
    i = pl.program_id(0)
    tm = xa_ref.shape[0]
    g_half = 0.5 * gpost_ref[...]

    @pl.when(i == 0)
    def _():
        _stage_bf16(win_hbm.at[layer], win_ref, stage_in, sem)
        _stage_bf16(wout_hbm.at[layer], wout_ref, stage_out, sem)
        xn_ref[0] = _rms(xa_ref[...], gpre_ref[...]).astype(BF16)
        y_ref[...] = jnp.zeros(y_ref.shape, F32)

    @pl.when((i >= 1) & (i <= n_tiles))
    def _():
        xn = xn_ref[(i - 1) % 2]
        rp = tm // FF_SIDE_CHUNKS
        for j in range(D_FF // FF_TILE):
            lo = j * FF_TILE
            gate = jnp.dot(xn, win_ref[:, lo:lo + FF_TILE], preferred_element_type=F32)
            up = jnp.dot(xn, win_ref[:, D_FF + lo:D_FF + lo + FF_TILE], preferred_element_type=F32)
            hm = gate * jax.nn.sigmoid(gate) * up
            h_ref[:, lo:lo + FF_TILE] = hm.astype(BF16)
            if j < FF_SIDE_CHUNKS:
                rs = slice(j * rp, (j + 1) * rp)
                out_rows = xc_ref[rs, :] + _rms(y_ref[rs, :], g_half)
                o_ref[rs, :] = out_rows
                xn_rows = _rms(xa_ref[rs, :], gpre_ref[...])
                xn_ref[i % 2, rs, :] = xn_rows.astype(BF16)
                zero = _zero_after(out_rows) + _zero_after(xn_rows)
                h_ref[0:8, lo:lo + 128] = (hm[0:8, 0:128] + zero).astype(BF16)
            else:
                if j == FF_SIDE_CHUNKS:
                    side = _convert_later_weights(i, w2in_blk, w2out_blk, wmo_blk, wt_blk, wtg_blk,
                                                  w2in_bf, w2out_bf, wmo_bf, wb_blk)
                    side = [side[0], side[1] + side[2], side[3]]
                h_ref[0:8, lo:lo + 128] = (hm[0:8, 0:128] + side[j - FF_SIDE_CHUNKS]).astype(BF16)
        y_ref[...] = jnp.dot(h_ref[...], wout_ref[...], preferred_element_type=F32)

    @pl.when(i == n_tiles + 1)
    def _():
        o_ref[...] = xc_ref[...] + _rms(y_ref[...], g_half)


def _mix_ffn_kernel(tiles_per_seq, x_ref, bg_ref, u_ref, uprev_ref, unext_ref, cw_ref, cb_ref,
                    y_ref, gmix_ref, wmo_ref, gpre_ref, gpost_ref, win_ref, wout_ref, o_ref, h_ref):
    x, xn = _mix_out_tile(tiles_per_seq, x_ref, bg_ref, u_ref, uprev_ref, unext_ref, cw_ref, cb_ref,
                          y_ref, wmo_ref, gmix_ref, gpre_ref)
    o_ref[...] = _ffn_tile(x, xn, gpost_ref, win_ref, wout_ref, h_ref)


def _ffn_scratch(tm):
    return [
        pltpu.VMEM((tm, D_FF), BF16),
        pltpu.VMEM((D_MODEL, 2 * D_FF), BF16),
        pltpu.VMEM((D_FF, D_MODEL), BF16),
    ], [
        pltpu.VMEM((2, D_MODEL // FF_STAGE_CHUNKS, 2 * D_FF), F32),
        pltpu.VMEM((2, D_FF // FF_STAGE_CHUNKS, D_MODEL), F32),
        pltpu.SemaphoreType.DMA((2,)),
    ]


def _ffn(x, gpre, gpost, w_in, w_out, w2_in, w2_out, w_mix_out, w_mix_in_t, layer, tm):
    T = x.shape[0]
    nt = T // tm
    assert nt > MIX_W_BLOCKS and nt % 2 == 0, f"first FFN needs more than {MIX_W_BLOCKS} token tiles, got {nt}"
    g0 = MIX_W_BLOCKS * 128
    resident, staging = _ffn_scratch(tm)
    tile = lambda lag: pl.BlockSpec((tm, D_MODEL), lambda i: (jnp.clip(i - lag, 0, nt - 1), 0))
    rows_in = lambda n, w, last: pl.BlockSpec((None, n, w), lambda i: (layer, jnp.clip(i - 1, 0, last), 0))
    rows_out = lambda n, w, last: pl.BlockSpec((n, w), lambda i: (jnp.clip(i - 1, 0, last), 0))
    r_in, r_out, r_mo = D_MODEL // nt, D_FF // (nt // 2), D_MODEL // (nt // 2)
    return pl.pallas_call(
        functools.partial(_ffn_kernel, layer, nt),
        grid=(nt + 2,),
        in_specs=[
            tile(0), tile(2),
            _resident((1, D_MODEL)),
            _resident((1, D_MODEL)),
            pl.BlockSpec(memory_space=pl.ANY),
            pl.BlockSpec(memory_space=pl.ANY),
            rows_in(r_in, 2 * D_FF, nt - 1),
            rows_in(r_out, D_MODEL, nt // 2 - 1),
            rows_in(r_mo, D_MODEL, nt // 2 - 1),
            rows_in(128, D_MODEL, MIX_W_BLOCKS - 1),
            pl.BlockSpec((None, 2 * NGR, D_MODEL), lambda i: (layer, g0 // (2 * NGR), 0)),
        ],
        out_specs=[
            tile(2),
            rows_out(r_in, 2 * D_FF, nt - 1),
            rows_out(r_out, D_MODEL, nt // 2 - 1),
            rows_out(r_mo, D_MODEL, nt // 2 - 1),
            rows_out(128, D_MODEL, MIX_W_BLOCKS),
        ],
        out_shape=[
            jax.ShapeDtypeStruct((T, D_MODEL), F32),
            jax.ShapeDtypeStruct((D_MODEL, 2 * D_FF), BF16),
            jax.ShapeDtypeStruct((D_FF, D_MODEL), BF16),
            jax.ShapeDtypeStruct((D_MODEL, D_MODEL), BF16),
            jax.ShapeDtypeStruct((g0 + 128, D_MODEL), BF16),
        ],
        scratch_shapes=resident + staging + [pltpu.VMEM((2, tm, D_MODEL), BF16),
                                             pltpu.VMEM((tm, D_MODEL), F32)],
        compiler_params=pltpu.CompilerParams(
            dimension_semantics=("arbitrary",), vmem_limit_bytes=VMEM_LIMIT),
        name="ffn",
    )(x, x, gpre, gpost, w_in, w_out, w2_in, w2_out, w_mix_out, w_mix_in_t, w_mix_in_t)


def _mix_ffn(x, bg, u, conv_w, conv_b, y_mlstm, gmix, w_mix_out, gpre, gpost, w_in, w_out, seq, tm):
    T = x.shape[0]
    tiles_per_seq = seq // tm
    sub = tm // HALO_ROWS
    last = T // HALO_ROWS - 1
    row = lambda w: pl.BlockSpec((tm, w), lambda i: (i, 0))
    return pl.pallas_call(
        functools.partial(_mix_ffn_kernel, tiles_per_seq),
        grid=(T // tm,),
        in_specs=[
            row(D_MODEL), row(CONV_WIDTH), row(CONV_WIDTH),
            pl.BlockSpec((HALO_ROWS, CONV_WIDTH), lambda i: (jnp.maximum(i * sub - 1, 0), 0)),
            pl.BlockSpec((HALO_ROWS, CONV_WIDTH), lambda i: (jnp.minimum((i + 1) * sub, last), 0)),
            _resident((3, CONV_WIDTH)), _resident((1, CONV_WIDTH)),
            row(HEADS * DV), _resident((1, D_MODEL)), _resident(w_mix_out.shape),
            _resident((1, D_MODEL)), _resident((1, D_MODEL)), _resident(w_in.shape), _resident(w_out.shape),
        ],
        out_specs=row(D_MODEL),
        out_shape=jax.ShapeDtypeStruct((T, D_MODEL), F32),
        scratch_shapes=[pltpu.VMEM((tm, D_FF), BF16)],
        compiler_params=pltpu.CompilerParams(
            dimension_semantics=("arbitrary",), vmem_limit_bytes=VMEM_LIMIT),
        name="mix_ffn",
    )(x, bg, u, u, u, conv_w, conv_b, y_mlstm, gmix, w_mix_out, gpre, gpost, w_in, w_out)


def _gate_rows(zr, p_ref, ccol_ref):
    L = CHUNK
    n_chunks = zr.shape[1] // L
    fwd_row = lax.broadcasted_iota(jnp.int32, (NGR, L), 0) < HEADS
    fwd_col = lax.broadcasted_iota(jnp.int32, (NGR, 1), 0) < HEADS
    li = zr[0:NGR]
    gates = jnp.concatenate([li, _log_sigmoid(zr[NGR:2 * NGR])], axis=0)
    x3 = jnp.concatenate(_split3(gates), axis=0)
    stacked = jnp.concatenate([x3[:, c * L:(c + 1) * L] for c in range(n_chunks)], axis=0)
    cum_ops = jnp.concatenate([_visible(False), _visible(True)], axis=1).astype(BF16)
    cum = jnp.dot(stacked, cum_ops, preferred_element_type=F32)
    b_chunks = []
    for c in range(n_chunks):
        blk = cum[c * 48:(c + 1) * 48]
        s16 = blk[0:16] + blk[16:32] + blk[32:48]
        b_chunks.append(jnp.where(fwd_row, s16[NGR:2 * NGR, 0:L], s16[NGR:2 * NGR, L:2 * L]))
    b = jnp.concatenate(b_chunks, axis=1)
    cc = li - b
    fwd_all = lax.broadcasted_iota(jnp.int32, cc.shape, 0) < HEADS
    p_ref[P_M, :] = jnp.where(fwd_all, _running_max(cc, False), _running_max(cc, True))
    p_ref[P_B, :] = b
    pad = jnp.zeros((L - NGR, L), F32)
    for c in range(n_chunks):
        sl = slice(c * L, (c + 1) * L)
        b_c, cc_c = b_chunks[c], cc[:, sl]
        g = jnp.where(fwd_col, b_c[:, L - 1:L], b_c[:, 0:1])
        m_chunk = g + jnp.max(cc_c, axis=1, keepdims=True)
        p_ref[P_E, sl] = jnp.exp(g + cc_c - m_chunk)
        p_ref[P_G, sl] = jnp.broadcast_to(g, (NGR, L))
        p_ref[P_MC, sl] = jnp.broadcast_to(m_chunk, (NGR, L))
        ccol_ref[sl, :] = jnp.concatenate([cc_c, pad], axis=0).T


def _mix_in_kernel(x_ref, gpre_ref, wb_ref, gbias_ref,
                   bg_ref, u_ref, qt_ref, k_ref, ksw_ref, vt_ref, o_ref, p_ref, ccol_ref):
    W, HK, HV = CONV_WIDTH, HEADS * DK, HEADS * DV
    q0 = 3 * W
    k0, v0 = q0 + HK, q0 + 2 * HK
    o0 = v0 + HV
    g0 = o0 + HV

    xn = _rms(x_ref[...], gpre_ref[...]).astype(BF16)
    proj = lambda a, b: lax.dot_general(xn, wb_ref[a:b, :], (((1,), (1,)), ((), ())),
                                        preferred_element_type=F32)

    zg = proj(g0, g0 + 128) + gbias_ref[...]
    bg_ref[...] = proj(0, W)
    u_ref[...] = proj(W, 2 * W) * proj(2 * W, 3 * W)
    _gate_rows(zg.T[0:2 * NGR], p_ref, ccol_ref)
    qt_ref[...] = (proj(q0, k0) * (DK ** -0.5)).T.astype(BF16)
    kk = proj(k0, v0)
    k_ref[...] = kk.astype(BF16)
    for pair in range(HEADS // 2):
        ps = slice(pair * 2 * DK, (pair + 1) * 2 * DK)
        ksw_ref[:, ps] = pltpu.roll(kk[:, ps], DK, 1).astype(BF16)
    vt_ref[...] = proj(v0, o0).T.astype(BF16)
    o_ref[...] = proj(o0, g0)


def _mix_in(x, gpre, wb, gbias, tm):
    T = x.shape[0]
    row = lambda w: pl.BlockSpec((tm, w), lambda i: (i, 0))
    col = lambda h: pl.BlockSpec((h, tm), lambda i: (0, i))
    HK, HV = HEADS * DK, HEADS * DV
    return pl.pallas_call(
        _mix_in_kernel,
        grid=(T // tm,),
        in_specs=[row(D_MODEL), _resident((1, D_MODEL)), _resident(wb.shape), _resident((1, 128))],
        out_specs=[row(CONV_WIDTH), row(CONV_WIDTH), col(HK), row(HK), row(HK), col(HV), row(HV),
                   col(5 * NGR), row(128)],
        out_shape=[
            jax.ShapeDtypeStruct((T, CONV_WIDTH), F32),
            jax.ShapeDtypeStruct((T, CONV_WIDTH), F32),
            jax.ShapeDtypeStruct((HK, T), BF16),
            jax.ShapeDtypeStruct((T, HK), BF16),
            jax.ShapeDtypeStruct((T, HK), BF16),
            jax.ShapeDtypeStruct((HV, T), BF16),
            jax.ShapeDtypeStruct((T, HV), F32),
            jax.ShapeDtypeStruct((5 * NGR, T), F32),
            jax.ShapeDtypeStruct((T, 128), F32),
        ],
        compiler_params=pltpu.CompilerParams(
            dimension_semantics=("arbitrary",), vmem_limit_bytes=VMEM_LIMIT),
        name="mix_in",
    )(x, gpre, wb, gbias)


def _state_half(d, h):
    return (h % 2) ^ d


def _scan_direction(d, vt_ref, k_ref, p_ref, ct_ref, mp_ref, ct_state, m_state):
    L = CHUNK
    n_chunks = k_ref.shape[0] // L
    lane_half = lax.broadcasted_iota(jnp.int32, (L, 128), 1) // DK
    lane_half_s = lax.broadcasted_iota(jnp.int32, (DVA, 128), 1) // DK
    ones_rows = jnp.ones((DVA - DV, L), BF16)

    cts = [ct_state[d, h] for h in range(HEADS)]
    m_prev = m_state[d]
    for chunk in (range(n_chunks - 1, -1, -1) if d == 1 else range(n_chunks)):
        sl = slice(chunk * L, (chunk + 1) * L)
        e, g, m_chunk = p_ref[P_E, sl], p_ref[P_G, sl], p_ref[P_MC, sl]
        mp_ref[:, sl] = m_prev
        m_new = jnp.maximum(g + m_prev, m_chunk)
        a_old = jnp.exp(g + m_prev - m_new)
        a_new = jnp.exp(m_chunk - m_new)
        new_cts = []
        for h in range(HEADS):
            r = HEADS * d + h
            vt_aug = jnp.concatenate([vt_ref[h * DV:(h + 1) * DV, sl], ones_rows], axis=0)
            vte = (vt_aug.astype(F32) * e[r:r + 1, :]).astype(BF16)
            pair = slice((h // 2) * 2 * DK, (h // 2 + 1) * 2 * DK)
            k_half = jnp.where(lane_half == _state_half(d, h), k_ref[sl, pair], jnp.zeros((L, 128), BF16))
            ct_chunk = jnp.dot(vte, k_half, preferred_element_type=F32)
            new_cts.append(a_old[r:r + 1, 0:1] * cts[h] + a_new[r:r + 1, 0:1] * ct_chunk)
        for pr in range(HEADS // 2):
            both = jnp.where(lane_half_s == _state_half(d, 2 * pr), cts[2 * pr], cts[2 * pr + 1])
            ct_ref[chunk, pr * DVA:(pr + 1) * DVA, :] = both.astype(BF16)
        cts = new_cts
        m_prev = m_new
    for h in range(HEADS):
        ct_state[d, h] = cts[h]
    m_state[d] = m_prev


def _mlstm_scan_kernel(vt_f, k_f, p_f, vt_b, ksw_b, p_b, ct_f, mp_f, ct_b, mp_b, ct_state, m_state):
    @pl.when(pl.program_id(1) == 0)
    def _():
        ct_state[...] = jnp.zeros(ct_state.shape, F32)
        m_state[...] = jnp.full(m_state.shape, NEG_INF, F32)

    _scan_direction(0, vt_f, k_f, p_f, ct_f, mp_f, ct_state, m_state)
    _scan_direction(1, vt_b, ksw_b, p_b, ct_b, mp_b, ct_state, m_state)


def _mlstm_out_kernel(qt_ref, k_ref, vt_ref, p_ref, mp_f, mp_b, ccol_ref, ct_f, ct_b,
                      o_ref, gain_ref, y_ref):
    L = CHUNK
    n_chunks = k_ref.shape[0] // L
    visible = (_visible(False), _visible(True))
    fwd_row = lax.broadcasted_iota(jnp.int32, (NGR, L), 0) < HEADS
    lane_half = lax.broadcasted_iota(jnp.int32, (DVA, 128), 1) // DK
    ones_rows = jnp.ones((DVA - DV, L), BF16)
    zq = jnp.zeros((DK, L), BF16)
    for chunk in range(n_chunks):
        sl = slice(chunk * L, (chunk + 1) * L)
        m_prev = jnp.where(fwd_row, mp_f[:, sl], mp_b[:, sl])
        n_t = jnp.maximum(m_prev, p_ref[P_M, sl])
        f_inter = jnp.exp(m_prev - n_t)
        e_min = jnp.exp(-(p_ref[P_B, sl] + n_t))
        ccol = ccol_ref[sl, :]
        for h in range(HEADS):
            hs = slice(h * DV, (h + 1) * DV)
            qt = qt_ref[h * DK:(h + 1) * DK, sl]
            k_pair = k_ref[sl, (h // 2) * 2 * DK:(h // 2 + 1) * 2 * DK]
            qt_pair = jnp.concatenate([qt, zq] if h % 2 == 0 else [zq, qt], axis=0)
            st = jnp.dot(k_pair, qt_pair, preferred_element_type=F32)
            pts, qfs = [], []
            for d in range(2):
                r = HEADS * d + h
                arg = jnp.where(visible[d], ccol[:, r:r + 1] - n_t[r:r + 1, :], NEG_INF)
                pts.append((jnp.exp(arg) * st).astype(BF16))
                qfs.append((qt.astype(F32) * f_inter[r:r + 1, :]).astype(BF16))
            inter = [jnp.concatenate([qfs[0], zq], axis=1), jnp.concatenate([zq, qfs[1]], axis=1)]
            first = 0 if _state_half(0, h) == 0 else 1
            rhs = jnp.concatenate([jnp.concatenate(pts, axis=1), inter[first], inter[1 - first]], axis=0)
            vt_aug = jnp.concatenate([vt_ref[hs, sl], ones_rows], axis=0)
            ps = slice((h // 2) * DVA, (h // 2 + 1) * DVA)
            blocks = (ct_f[chunk, ps, :], ct_b[chunk, ps, :])
            ct = jnp.where(lane_half == 0, blocks[first], blocks[1 - first])
            both = jnp.dot(jnp.concatenate([vt_aug, ct], axis=1), rhs, preferred_element_type=F32)
            ht = None
            for d in range(2):
                r = HEADS * d + h
                numer = both[0:DV, d * L:(d + 1) * L]
                denom = both[DV:DV + 1, d * L:(d + 1) * L]
                part = numer / jnp.maximum(jnp.abs(denom), e_min[r:r + 1, :])
                ht = part if ht is None else ht + part
            ms = jnp.mean(ht * ht, axis=0, keepdims=True)
            hn = ht * lax.rsqrt(ms + EPS) * gain_ref[hs, :]
            y_ref[sl, hs] = (jax.nn.sigmoid(o_ref[sl, hs]) * hn.T).astype(BF16)


def _mlstm(qt, k, ksw, vt, p, ccol, o, gain_b, batch, seq, scan_rows, rows):
    T = batch * seq
    ng = seq // scan_rows
    HK, HV = HEADS * DK, HEADS * DV

    def scan_specs(group_of):
        blk = lambda b, j: b * ng + group_of(j)
        ins = [pl.BlockSpec((HV, scan_rows), lambda b, j: (0, blk(b, j))),
               pl.BlockSpec((scan_rows, HK), lambda b, j: (blk(b, j), 0)),
               pl.BlockSpec((5 * NGR, scan_rows), lambda b, j: (0, blk(b, j)))]
        outs = [pl.BlockSpec((scan_rows // CHUNK, CT_ROWS, 128), lambda b, j: (blk(b, j), 0, 0)),
                pl.BlockSpec((NGR, scan_rows), lambda b, j: (0, blk(b, j)))]
        return ins, outs

    ins_f, outs_f = scan_specs(lambda j: j)
    ins_b, outs_b = scan_specs(lambda j: ng - 1 - j)
    state_shapes = [jax.ShapeDtypeStruct((T // CHUNK, CT_ROWS, 128), BF16),
                    jax.ShapeDtypeStruct((NGR, T), F32)]
    if ng == 1:
        scan_kernel = lambda vt_r, k_r, ksw_r, p_r, *rest: _mlstm_scan_kernel(vt_r, k_r, p_r, vt_r, ksw_r, p_r, *rest)
        scan_in_specs, scan_args = [ins_f[0], ins_f[1], ins_f[1], ins_f[2]], (vt, k, ksw, p)
    else:
        scan_kernel, scan_in_specs, scan_args = _mlstm_scan_kernel, ins_f + ins_b, (vt, k, p, vt, ksw, p)
    ct_f, mp_f, ct_b, mp_b = pl.pallas_call(
        scan_kernel,
        grid=(batch, ng),
        in_specs=scan_in_specs,
        out_specs=outs_f + outs_b,
        out_shape=state_shapes + state_shapes,
        scratch_shapes=[pltpu.VMEM((2, HEADS, DVA, 128), F32), pltpu.VMEM((2, NGR, 128), F32)],
        compiler_params=pltpu.CompilerParams(dimension_semantics=("arbitrary", "arbitrary"),
                                             vmem_limit_bytes=VMEM_LIMIT),
        name="mlstm_scan",
    )(*scan_args)

    row = lambda w: pl.BlockSpec((rows, w), lambda i: (i, 0))
    col = lambda h: pl.BlockSpec((h, rows), lambda i: (0, i))
    ctb = pl.BlockSpec((rows // CHUNK, CT_ROWS, 128), lambda i: (i, 0, 0))
    return pl.pallas_call(
        _mlstm_out_kernel,
        grid=(T // rows,),
        in_specs=[col(HK), row(HK), col(HV), col(5 * NGR), col(NGR), col(NGR), row(128), ctb, ctb,
                  row(HV), _resident((HV, 128))],
        out_specs=row(HV),
        out_shape=jax.ShapeDtypeStruct((T, HV), BF16),
        compiler_params=pltpu.CompilerParams(dimension_semantics=("arbitrary",),
                                             vmem_limit_bytes=VMEM_LIMIT),
        name="mlstm_out",
    )(qt, k, vt, p, mp_f, mp_b, ccol, ct_f, ct_b, o, gain_b)


def kernel(x, norm_ffn1_pre, norm_ffn1_post, w_ffn1_in, w_ffn1_out, norm_mix_pre, norm_mix_post,
           w_mix_in, conv_w, conv_b, gate_i_bias, gate_f_bias, mlstm_norm, w_mix_out,
           norm_ffn2_pre, norm_ffn2_post, w_ffn2_in, w_ffn2_out):
    batch, seq, _ = x.shape
    T = batch * seq
    depth = norm_ffn1_pre.shape[0]
    tm = 512
    xt = x.reshape(T, D_MODEL)
    HV = HEADS * DV
    for l in range(depth):
        xt, w2_in, w2_out, w_mo, w_mi = _ffn(
            xt, norm_ffn1_pre[l][None], norm_ffn1_post[l][None], w_ffn1_in, w_ffn1_out,
            w_ffn2_in, w_ffn2_out, w_mix_out, jnp.swapaxes(w_mix_in, 1, 2), l, tm)

        gbias = jnp.pad(jnp.concatenate([gate_i_bias[l], gate_f_bias[l]]), (0, 128 - 2 * NGR))[None]
        bg, u, qt, k, ksw, vt, o, p, ccol = _mix_in(xt, norm_mix_pre[l][None], w_mi, gbias, 2 * tm)
        gain_b = jnp.broadcast_to(mlstm_norm[l][:, None], (HV, 128))
        y_mlstm = _mlstm(qt, k, ksw, vt, p, ccol, o, gain_b, batch, seq, SCAN_ROWS, MLSTM_ROWS)
        xt = _mix_ffn(xt, bg, u, conv_w[l], conv_b[l][None], y_mlstm, norm_mix_post[l][None], w_mo,
                      norm_ffn2_pre[l][None], norm_ffn2_post[l][None], w2_in, w2_out, seq, tm)
    return xt.reshape(batch, seq, D_MODEL)
```

```python
import functools

import jax
import jax.numpy as jnp
from jax import lax
from jax.experimental import pallas as pl
from jax.experimental.pallas import tpu as pltpu

D_MODEL = 1024
D_FF = 2816
CONV_WIDTH = 512
HEADS = 4
DK = 64
DV = 128
CHUNK = 128
EPS = 1e-6
NEG_INF = -1e30

FF_TILE = 256
FF_STAGE_CHUNKS = 8
FF_SIDE_CHUNKS = 8
MIX_OUT_PIECES = 4
FF_LEAD_TILES = 1
MIX_W_BLOCKS = (3 * CONV_WIDTH + 2 * HEADS * DK + 2 * HEADS * DV) // 128
MLSTM_ROWS = 2048
SCAN_ROWS = 4096
DVA = DV + 16
HALO_ROWS = 8
NGR = 2 * HEADS
RING = 3
CT_ROWS = (HEADS // 2) * DVA
VMEM_LIMIT = 56 * 1024 * 1024

P_E, P_M, P_B, P_G, P_MC = (slice(i * NGR, (i + 1) * NGR) for i in range(5))

F32 = jnp.float32
BF16 = jnp.bfloat16


def _rms(x, g):
    return x * lax.rsqrt(jnp.mean(x * x, axis=-1, keepdims=True) + EPS) * g


def _log_sigmoid(z):
    return jnp.minimum(z, 0.0) - jnp.log1p(jnp.exp(-jnp.abs(z)))


def _resident(shape):
    zeros = (0,) * len(shape)
    return pl.BlockSpec(shape, lambda *_: zeros, pipeline_mode=pl.Buffered(1))


def _split3(x):
    hi = x.astype(BF16)
    r1 = x - hi.astype(F32)
    mid = r1.astype(BF16)
    lo = (r1 - mid.astype(F32)).astype(BF16)
    return hi, mid, lo


def _visible(rev):
    s = lax.broadcasted_iota(jnp.int32, (CHUNK, CHUNK), 0)
    t = lax.broadcasted_iota(jnp.int32, (CHUNK, CHUNK), 1)
    return (s >= t) if rev else (s <= t)


def _running_max(x, rev):
    n = x.shape[1]
    pos = lax.broadcasted_iota(jnp.int32, x.shape, 1) & (CHUNK - 1)
    k = 1
    while k < CHUNK:
        if rev:
            shifted, ok = pltpu.roll(x, n - k, 1), pos < CHUNK - k
        else:
            shifted, ok = pltpu.roll(x, k, 1), pos >= k
        x = jnp.maximum(x, jnp.where(ok, shifted, NEG_INF))
        k *= 2
    return x


def _stage_bf16(src_hbm, dst_ref, stage_ref, sem, rows=None):
    rows = stage_ref.shape[1] if rows is None else rows
    n_chunks = src_hbm.shape[0] // rows

    def copy(c):
        return pltpu.make_async_copy(src_hbm.at[pl.ds(c * rows, rows), :],
                                     stage_ref.at[c % 2, pl.ds(0, rows), :], sem.at[c % 2])

    copy(0).start()
    for c in range(n_chunks):
        if c + 1 < n_chunks:
            copy(c + 1).start()
        copy(c).wait()
        dst_ref[c * rows:(c + 1) * rows, :] = stage_ref[c % 2, 0:rows, :].astype(BF16)


def _mix_out_tile(tiles_per_seq, x_ref, bg_ref, u_ref, uprev_ref, unext_ref, cw_ref, cb_ref,
                  y_ref, wmo_ref, gmix_ref, gpre_ref):
    i = pl.program_id(0)
    tm = u_ref.shape[0]
    u = u_ref[...]
    has_prev = (i % tiles_per_seq != 0).astype(F32)
    has_next = (i % tiles_per_seq != tiles_per_seq - 1).astype(F32)
    prev_row = uprev_ref[HALO_ROWS - 1:HALO_ROWS, :] * has_prev
    next_row = unext_ref[0:1, :] * has_next
    ri = lax.broadcasted_iota(jnp.int32, u.shape, 0)
    u_m1 = jnp.where(ri == 0, prev_row, pltpu.roll(u, 1, 0))
    u_p1 = jnp.where(ri == tm - 1, next_row, pltpu.roll(u, tm - 1, 0))
    conv = cw_ref[0:1, :] * u_m1 + cw_ref[1:2, :] * u + cw_ref[2:3, :] * u_p1
    y_conv = (bg_ref[...] * (conv + cb_ref[...])).astype(BF16)
    rp = tm // MIX_OUT_PIECES
    xs, xns = [], []
    for r in range(MIX_OUT_PIECES):
        rs = slice(r * rp, (r + 1) * rp)
        h = jnp.dot(y_ref[rs, :], wmo_ref[CONV_WIDTH:, :], preferred_element_type=F32) \
            + jnp.dot(y_conv[rs, :], wmo_ref[0:CONV_WIDTH, :], preferred_element_type=F32)
        xs.append(x_ref[rs, :] + _rms(h, gmix_ref[...]))
        xns.append(_rms(xs[-1], gpre_ref[...]).astype(BF16))
    return jnp.concatenate(xs, axis=0), xns


def _ffn_tile(x, xn_pieces, gpost_ref, win_ref, wout_ref, h_ref):
    rp = xn_pieces[0].shape[0]
    xn = jnp.concatenate(xn_pieces, axis=0)
    for j in range(D_FF // FF_TILE):
        lo = j * FF_TILE
        lhs = list(enumerate(xn_pieces)) if j < FF_LEAD_TILES and len(xn_pieces) > 1 else [(None, xn)]
        for r, xr in lhs:
            rs = slice(None) if r is None else slice(r * rp, (r + 1) * rp)
            gate = jnp.dot(xr, win_ref[:, lo:lo + FF_TILE], preferred_element_type=F32)
            up = jnp.dot(xr, win_ref[:, D_FF + lo:D_FF + lo + FF_TILE], preferred_element_type=F32)
            h_ref[rs, lo:lo + FF_TILE] = (gate * jax.nn.sigmoid(gate) * up).astype(BF16)
    y = jnp.dot(h_ref[...], wout_ref[...], preferred_element_type=F32)
    return x + _rms(y, 0.5 * gpost_ref[...])


def _zero_after(v):
    u = pltpu.bitcast(v, jnp.uint32)
    acc = None
    for r in range(u.shape[0] // 8):
        for c in range(u.shape[1] // 128):
            t = u[r * 8:(r + 1) * 8, c * 128:(c + 1) * 128]
            acc = t if acc is None else acc | t
    return pltpu.bitcast((acc >> 16) >> 16, F32)


def _convert_later_weights(i, w2in_blk, w2out_blk, wmo_blk, wt_blk, wtg_blk, w2in_bf, w2out_bf, wmo_bf, wb_blk):
    zeros = []
    for src, dst in ((w2in_blk, w2in_bf), (w2out_blk, w2out_bf), (wmo_blk, wmo_bf)):
        v = src[...].astype(BF16)
        dst[...] = v
        zeros.append(_zero_after(v[0:16, 0:128]))
    gates = jnp.concatenate([wtg_blk[...], jnp.zeros((128 - 2 * NGR, D_MODEL), F32)], axis=0)
    last = jnp.clip(i - 1, 0, MIX_W_BLOCKS) == MIX_W_BLOCKS
    blk = jnp.where(last, gates, wt_blk[...]).T
    wb_blk[...] = blk.astype(BF16)
    zeros.append(_zero_after(blk))
    return zeros


def _ffn_kernel(layer, n_tiles, xa_ref, xc_ref, gpre_ref, gpost_ref, win_hbm, wout_hbm,
                w2in_blk, w2out_blk, wmo_blk, wt_blk, wtg_blk,
                o_ref, w2in_bf, w2out_bf, wmo_bf, wb_blk,
                h_ref, win_ref, wout_ref, stage_in, stage_out, sem, xn_ref, y_ref):
    i = pl.program_id(0)
    tm = xa_ref.shape[0]
    g_half = 0.5 * gpost_ref[...]

    @pl.when(i == 0)
    def _():
        _stage_bf16(win_hbm.at[layer], win_ref, stage_in, sem)
        _stage_bf16(wout_hbm.at[layer], wout_ref, stage_out, sem)
        xn_ref[0] = _rms(xa_ref[...], gpre_ref[...]).astype(BF16)
        y_ref[...] = jnp.zeros(y_ref.shape, F32)

    @pl.when((i >= 1) & (i <= n_tiles))
    def _():
        xn = xn_ref[(i - 1) % 2]
        rp = tm // FF_SIDE_CHUNKS
        for j in range(D_FF // FF_TILE):
            lo = j * FF_TILE
            gate = jnp.dot(xn, win_ref[:, lo:lo + FF_TILE], preferred_element_type=F32)
            up = jnp.dot(xn, win_ref[:, D_FF + lo:D_FF + lo + FF_TILE], preferred_element_type=F32)
            hm = gate * jax.nn.sigmoid(gate) * up
            h_ref[:, lo:lo + FF_TILE] = hm.astype(BF16)
            if j < FF_SIDE_CHUNKS:
                rs = slice(j * rp, (j + 1) * rp)
                out_rows = xc_ref[rs, :] + _rms(y_ref[rs, :], g_half)
                o_ref[rs, :] = out_rows
                xn_rows = _rms(xa_ref[rs, :], gpre_ref[...])
                xn_ref[i % 2, rs, :] = xn_rows.astype(BF16)
                zero = _zero_after(out_rows) + _zero_after(xn_rows)
                h_ref[0:8, lo:lo + 128] = (hm[0:8, 0:128] + zero).astype(BF16)
            else:
                if j == FF_SIDE_CHUNKS:
                    side = _convert_later_weights(i, w2in_blk, w2out_blk, wmo_blk, wt_blk, wtg_blk,
                                                  w2in_bf, w2out_bf, wmo_bf, wb_blk)
                    side = [side[0], side[1] + side[2], side[3]]
                h_ref[0:8, lo:lo + 128] = (hm[0:8, 0:128] + side[j - FF_SIDE_CHUNKS]).astype(BF16)
        y_ref[...] = jnp.dot(h_ref[...], wout_ref[...], preferred_element_type=F32)

    @pl.when(i == n_tiles + 1)
    def _():
        o_ref[...] = xc_ref[...] + _rms(y_ref[...], g_half)


def _mix_ffn_kernel(tiles_per_seq, x_ref, bg_ref, u_ref, uprev_ref, unext_ref, cw_ref, cb_ref,
                    y_ref, gmix_ref, wmo_ref, gpre_ref, gpost_ref, win_ref, wout_ref, o_ref, h_ref):
    x, xn = _mix_out_tile(tiles_per_seq, x_ref, bg_ref, u_ref, uprev_ref, unext_ref, cw_ref, cb_ref,
                          y_ref, wmo_ref, gmix_ref, gpre_ref)
    o_ref[...] = _ffn_tile(x, xn, gpost_ref, win_ref, wout_ref, h_ref)


def _ffn_scratch(tm):
    return [
        pltpu.VMEM((tm, D_FF), BF16),
        pltpu.VMEM((D_MODEL, 2 * D_FF), BF16),
        pltpu.VMEM((D_FF, D_MODEL), BF16),
    ], [
        pltpu.VMEM((2, D_MODEL // FF_STAGE_CHUNKS, 2 * D_FF), F32),
        pltpu.VMEM((2, D_FF // FF_STAGE_CHUNKS, D_MODEL), F32),
        pltpu.SemaphoreType.DMA((2,)),
    ]


def _ffn(x, gpre, gpost, w_in, w_out, w2_in, w2_out, w_mix_out, w_mix_in_t, layer, tm):
    T = x.shape[0]
    nt = T // tm
    assert nt > MIX_W_BLOCKS and nt % 2 == 0, f"first FFN needs more than {MIX_W_BLOCKS} token tiles, got {nt}"
    g0 = MIX_W_BLOCKS * 128
    resident, staging = _ffn_scratch(tm)
    tile = lambda lag: pl.BlockSpec((tm, D_MODEL), lambda i: (jnp.clip(i - lag, 0, nt - 1), 0))
    rows_in = lambda n, w, last: pl.BlockSpec((None, n, w), lambda i: (layer, jnp.clip(i - 1, 0, last), 0))
    rows_out = lambda n, w, last: pl.BlockSpec((n, w), lambda i: (jnp.clip(i - 1, 0, last), 0))
    r_in, r_out, r_mo = D_MODEL // nt, D_FF // (nt // 2), D_MODEL // (nt // 2)
    return pl.pallas_call(
        functools.partial(_ffn_kernel, layer, nt),
        grid=(nt + 2,),
        in_specs=[
            tile(0), tile(2),
            _resident((1, D_MODEL)),
            _resident((1, D_MODEL)),
            pl.BlockSpec(memory_space=pl.ANY),
            pl.BlockSpec(memory_space=pl.ANY),
            rows_in(r_in, 2 * D_FF, nt - 1),
            rows_in(r_out, D_MODEL, nt // 2 - 1),
            rows_in(r_mo, D_MODEL, nt // 2 - 1),
            rows_in(128, D_MODEL, MIX_W_BLOCKS - 1),
            pl.BlockSpec((None, 2 * NGR, D_MODEL), lambda i: (layer, g0 // (2 * NGR), 0)),
        ],
        out_specs=[
            tile(2),
            rows_out(r_in, 2 * D_FF, nt - 1),
            rows_out(r_out, D_MODEL, nt // 2 - 1),
            rows_out(r_mo, D_MODEL, nt // 2 - 1),
            pl.BlockSpec((D_MODEL, 128), lambda i: (0, jnp.clip(i - 1, 0, MIX_W_BLOCKS))),
        ],
        out_shape=[
            jax.ShapeDtypeStruct((T, D_MODEL), F32),
            jax.ShapeDtypeStruct((D_MODEL, 2 * D_FF), BF16),
            jax.ShapeDtypeStruct((D_FF, D_MODEL), BF16),
            jax.ShapeDtypeStruct((D_MODEL, D_MODEL), BF16),
            jax.ShapeDtypeStruct((D_MODEL, g0 + 128), BF16),
        ],
        scratch_shapes=resident + staging + [pltpu.VMEM((2, tm, D_MODEL), BF16),
                                             pltpu.VMEM((tm, D_MODEL), F32)],
        compiler_params=pltpu.CompilerParams(
            dimension_semantics=("arbitrary",), vmem_limit_bytes=VMEM_LIMIT),
        name="ffn",
    )(x, x, gpre, gpost, w_in, w_out, w2_in, w2_out, w_mix_out, w_mix_in_t, w_mix_in_t)


def _mix_ffn(x, bg, u, conv_w, conv_b, y_mlstm, gmix, w_mix_out, gpre, gpost, w_in, w_out, seq, tm):
    T = x.shape[0]
    tiles_per_seq = seq // tm
    sub = tm // HALO_ROWS
    last = T // HALO_ROWS - 1
    row = lambda w: pl.BlockSpec((tm, w), lambda i: (i, 0))
    return pl.pallas_call(
        functools.partial(_mix_ffn_kernel, tiles_per_seq),
        grid=(T // tm,),
        in_specs=[
            row(D_MODEL), row(CONV_WIDTH), row(CONV_WIDTH),
            pl.BlockSpec((HALO_ROWS, CONV_WIDTH), lambda i: (jnp.maximum(i * sub - 1, 0), 0)),
            pl.BlockSpec((HALO_ROWS, CONV_WIDTH), lambda i: (jnp.minimum((i + 1) * sub, last), 0)),
            _resident((3, CONV_WIDTH)), _resident((1, CONV_WIDTH)),
            row(HEADS * DV), _resident((1, D_MODEL)), _resident(w_mix_out.shape),
            _resident((1, D_MODEL)), _resident((1, D_MODEL)), _resident(w_in.shape), _resident(w_out.shape),
        ],
        out_specs=row(D_MODEL),
        out_shape=jax.ShapeDtypeStruct((T, D_MODEL), F32),
        scratch_shapes=[pltpu.VMEM((tm, D_FF), BF16)],
        compiler_params=pltpu.CompilerParams(
            dimension_semantics=("arbitrary",), vmem_limit_bytes=VMEM_LIMIT),
        name="mix_ffn",
    )(x, bg, u, u, u, conv_w, conv_b, y_mlstm, gmix, w_mix_out, gpre, gpost, w_in, w_out)


def _gate_rows(zr, p_ref, ccol_ref):
    L = CHUNK
    n_chunks = zr.shape[1] // L
    fwd_row = lax.broadcasted_iota(jnp.int32, (NGR, L), 0) < HEADS
    fwd_col = lax.broadcasted_iota(jnp.int32, (NGR, 1), 0) < HEADS
    li = zr[0:NGR]
    gates = jnp.concatenate([li, _log_sigmoid(zr[NGR:2 * NGR])], axis=0)
    x3 = jnp.concatenate(_split3(gates), axis=0)
    stacked = jnp.concatenate([x3[:, c * L:(c + 1) * L] for c in range(n_chunks)], axis=0)
    cum_ops = jnp.concatenate([_visible(False), _visible(True)], axis=1).astype(BF16)
    cum = jnp.dot(stacked, cum_ops, preferred_element_type=F32)
    b_chunks = []
    for c in range(n_chunks):
        blk = cum[c * 48:(c + 1) * 48]
        s16 = blk[0:16] + blk[16:32] + blk[32:48]
        b_chunks.append(jnp.where(fwd_row, s16[NGR:2 * NGR, 0:L], s16[NGR:2 * NGR, L:2 * L]))
    b = jnp.concatenate(b_chunks, axis=1)
    cc = li - b
    fwd_all = lax.broadcasted_iota(jnp.int32, cc.shape, 0) < HEADS
    p_ref[P_M, :] = jnp.where(fwd_all, _running_max(cc, False), _running_max(cc, True))
    p_ref[P_B, :] = b
    pad = jnp.zeros((L - NGR, L), F32)
    for c in range(n_chunks):
        sl = slice(c * L, (c + 1) * L)
        b_c, cc_c = b_chunks[c], cc[:, sl]
        g = jnp.where(fwd_col, b_c[:, L - 1:L], b_c[:, 0:1])
        m_chunk = g + jnp.max(cc_c, axis=1, keepdims=True)
        p_ref[P_E, sl] = jnp.exp(g + cc_c - m_chunk)
        p_ref[P_G, sl] = jnp.broadcast_to(g, (NGR, L))
        p_ref[P_MC, sl] = jnp.broadcast_to(m_chunk, (NGR, L))
        ccol_ref[sl, :] = jnp.concatenate([cc_c, pad], axis=0).T


def _mix_in_kernel(x_ref, gpre_ref, wb_ref, gbias_ref,
                   bg_ref, u_ref, qt_ref, k_ref, ksw_ref, vt_ref, o_ref, p_ref, ccol_ref):
    W, HK, HV = CONV_WIDTH, HEADS * DK, HEADS * DV
    q0 = 3 * W
    k0, v0 = q0 + HK, q0 + 2 * HK
    o0 = v0 + HV
    g0 = o0 + HV

    xn = _rms(x_ref[...], gpre_ref[...]).astype(BF16)
    proj = lambda a, b: jnp.dot(xn, wb_ref[:, a:b], preferred_element_type=F32)

    zg = proj(g0, g0 + 128) + gbias_ref[...]
    bg_ref[...] = proj(0, W)
    u_ref[...] = proj(W, 2 * W) * proj(2 * W, 3 * W)
    _gate_rows(zg.T[0:2 * NGR], p_ref, ccol_ref)
    qt_ref[...] = (proj(q0, k0) * (DK ** -0.5)).T.astype(BF16)
    kk = proj(k0, v0)
    k_ref[...] = kk.astype(BF16)
    for pair in range(HEADS // 2):
        ps = slice(pair * 2 * DK, (pair + 1) * 2 * DK)
        ksw_ref[:, ps] = pltpu.roll(kk[:, ps], DK, 1).astype(BF16)
    vt_ref[...] = proj(v0, o0).T.astype(BF16)
    o_ref[...] = proj(o0, g0)


def _mix_in(x, gpre, wb, gbias, tm):
    T = x.shape[0]
    row = lambda w: pl.BlockSpec((tm, w), lambda i: (i, 0))
    col = lambda h: pl.BlockSpec((h, tm), lambda i: (0, i))
    HK, HV = HEADS * DK, HEADS * DV
    return pl.pallas_call(
        _mix_in_kernel,
        grid=(T // tm,),
        in_specs=[row(D_MODEL), _resident((1, D_MODEL)), _resident(wb.shape), _resident((1, 128))],
        out_specs=[row(CONV_WIDTH), row(CONV_WIDTH), col(HK), row(HK), row(HK), col(HV), row(HV),
                   col(5 * NGR), row(128)],
        out_shape=[
            jax.ShapeDtypeStruct((T, CONV_WIDTH), F32),
            jax.ShapeDtypeStruct((T, CONV_WIDTH), F32),
            jax.ShapeDtypeStruct((HK, T), BF16),
            jax.ShapeDtypeStruct((T, HK), BF16),
            jax.ShapeDtypeStruct((T, HK), BF16),
            jax.ShapeDtypeStruct((HV, T), BF16),
            jax.ShapeDtypeStruct((T, HV), F32),
            jax.ShapeDtypeStruct((5 * NGR, T), F32),
            jax.ShapeDtypeStruct((T, 128), F32),
        ],
        compiler_params=pltpu.CompilerParams(
            dimension_semantics=("arbitrary",), vmem_limit_bytes=VMEM_LIMIT),
        name="mix_in",
    )(x, gpre, wb, gbias)


def _state_half(d, h):
    return (h % 2) ^ d


def _scan_direction(d, vt_ref, k_ref, p_ref, ct_ref, mp_ref, ct_state, m_state):
    L = CHUNK
    n_chunks = k_ref.shape[0] // L
    lane_half = lax.broadcasted_iota(jnp.int32, (L, 128), 1) // DK
    lane_half_s = lax.broadcasted_iota(jnp.int32, (DVA, 128), 1) // DK
    ones_rows = jnp.ones((DVA - DV, L), BF16)

    cts = [ct_state[d, h] for h in range(HEADS)]
    m_prev = m_state[d]
    for chunk in (range(n_chunks - 1, -1, -1) if d == 1 else range(n_chunks)):
        sl = slice(chunk * L, (chunk + 1) * L)
        e, g, m_chunk = p_ref[P_E, sl], p_ref[P_G, sl], p_ref[P_MC, sl]
        mp_ref[:, sl] = m_prev
        m_new = jnp.maximum(g + m_prev, m_chunk)
        a_old = jnp.exp(g + m_prev - m_new)
        a_new = jnp.exp(m_chunk - m_new)
        new_cts = []
        for h in range(HEADS):
            r = HEADS * d + h
            vt_aug = jnp.concatenate([vt_ref[h * DV:(h + 1) * DV, sl], ones_rows], axis=0)
            vte = (vt_aug.astype(F32) * e[r:r + 1, :]).astype(BF16)
            pair = slice((h // 2) * 2 * DK, (h // 2 + 1) * 2 * DK)
            k_half = jnp.where(lane_half == _state_half(d, h), k_ref[sl, pair], jnp.zeros((L, 128), BF16))
            ct_chunk = jnp.dot(vte, k_half, preferred_element_type=F32)
            new_cts.append(a_old[r:r + 1, 0:1] * cts[h] + a_new[r:r + 1, 0:1] * ct_chunk)
        for pr in range(HEADS // 2):
            both = jnp.where(lane_half_s == _state_half(d, 2 * pr), cts[2 * pr], cts[2 * pr + 1])
            ct_ref[chunk, pr * DVA:(pr + 1) * DVA, :] = both.astype(BF16)
        cts = new_cts
        m_prev = m_new
    for h in range(HEADS):
        ct_state[d, h] = cts[h]
    m_state[d] = m_prev


def _mlstm_scan_kernel(vt_f, k_f, p_f, vt_b, ksw_b, p_b, ct_f, mp_f, ct_b, mp_b, ct_state, m_state):
    @pl.when(pl.program_id(1) == 0)
    def _():
        ct_state[...] = jnp.zeros(ct_state.shape, F32)
        m_state[...] = jnp.full(m_state.shape, NEG_INF, F32)

    _scan_direction(0, vt_f, k_f, p_f, ct_f, mp_f, ct_state, m_state)
    _scan_direction(1, vt_b, ksw_b, p_b, ct_b, mp_b, ct_state, m_state)


def _mlstm_out_kernel(qt_ref, k_ref, vt_hbm, p_ref, mp_f, mp_b, ccol_ref, ctf_hbm, ctb_hbm,
                      o_hbm, gain_ref, y_ref, vt_buf, ctf_buf, ctb_buf, o_buf, sem):
    L = CHUNK
    rows = k_ref.shape[0]
    n_chunks = rows // L
    i = pl.program_id(0)
    n_steps = pl.num_programs(0)

    def tile_copies(t):
        slot = t % RING
        r0 = pl.multiple_of(t * rows, rows)
        c0 = pl.multiple_of(t * n_chunks, n_chunks)
        return (pltpu.make_async_copy(vt_hbm.at[:, pl.ds(r0, rows)], vt_buf.at[slot], sem.at[0, slot]),
                pltpu.make_async_copy(ctf_hbm.at[pl.ds(c0, n_chunks)], ctf_buf.at[slot], sem.at[1, slot]),
                pltpu.make_async_copy(ctb_hbm.at[pl.ds(c0, n_chunks)], ctb_buf.at[slot], sem.at[2, slot]),
                pltpu.make_async_copy(o_hbm.at[pl.ds(r0, rows)], o_buf.at[slot], sem.at[3, slot]))

    @pl.when(i == 0)
    def _():
        for t in range(RING - 1):
            @pl.when(t < n_steps)
            def _():
                for c in tile_copies(t):
                    c.start()

    @pl.when(i + (RING - 1) < n_steps)
    def _():
        for c in tile_copies(i + (RING - 1)):
            c.start()

    for c in tile_copies(i):
        c.wait()
    slot = i % RING
    vt_ref, ct_f, ct_b, o_ref = vt_buf.at[slot], ctf_buf.at[slot], ctb_buf.at[slot], o_buf.at[slot]
    visible = (_visible(False), _visible(True))
    fwd_row = lax.broadcasted_iota(jnp.int32, (NGR, L), 0) < HEADS
    lane_half = lax.broadcasted_iota(jnp.int32, (DVA, 128), 1) // DK
    ones_rows = jnp.ones((DVA - DV, L), BF16)
    zq = jnp.zeros((DK, L), BF16)
    for chunk in range(n_chunks):
        sl = slice(chunk * L, (chunk + 1) * L)
        m_prev = jnp.where(fwd_row, mp_f[:, sl], mp_b[:, sl])
        n_t = jnp.maximum(m_prev, p_ref[P_M, sl])
        f_inter = jnp.exp(m_prev - n_t)
        e_min = jnp.exp(-(p_ref[P_B, sl] + n_t))
        ccol = ccol_ref[sl, :]
        for h in range(HEADS):
            hs = slice(h * DV, (h + 1) * DV)
            qt = qt_ref[h * DK:(h + 1) * DK, sl]
            k_pair = k_ref[sl, (h // 2) * 2 * DK:(h // 2 + 1) * 2 * DK]
            qt_pair = jnp.concatenate([qt, zq] if h % 2 == 0 else [zq, qt], axis=0)
            st = jnp.dot(k_pair, qt_pair, preferred_element_type=F32)
            pts, qfs = [], []
            for d in range(2):
                r = HEADS * d + h
                arg = jnp.where(visible[d], ccol[:, r:r + 1] - n_t[r:r + 1, :], NEG_INF)
                pts.append((jnp.exp(arg) * st).astype(BF16))
                qfs.append((qt.astype(F32) * f_inter[r:r + 1, :]).astype(BF16))
            inter = [jnp.concatenate([qfs[0], zq], axis=1), jnp.concatenate([zq, qfs[1]], axis=1)]
            first = 0 if _state_half(0, h) == 0 else 1
            rhs = jnp.concatenate([jnp.concatenate(pts, axis=1), inter[first], inter[1 - first]], axis=0)
            vt_aug = jnp.concatenate([vt_ref[hs, sl], ones_rows], axis=0)
            ps = slice((h // 2) * DVA, (h // 2 + 1) * DVA)
            blocks = (ct_f[chunk, ps, :], ct_b[chunk, ps, :])
            ct = jnp.where(lane_half == 0, blocks[first], blocks[1 - first])
            both = jnp.dot(jnp.concatenate([vt_aug, ct], axis=1), rhs, preferred_element_type=F32)
            ht = None
            for d in range(2):
                r = HEADS * d + h
                numer = both[0:DV, d * L:(d + 1) * L]
                denom = both[DV:DV + 1, d * L:(d + 1) * L]
                part = numer / jnp.maximum(jnp.abs(denom), e_min[r:r + 1, :])
                ht = part if ht is None else ht + part
            ms = jnp.mean(ht * ht, axis=0, keepdims=True)
            hn = ht * lax.rsqrt(ms + EPS) * gain_ref[hs, :]
            y_ref[sl, hs] = (jax.nn.sigmoid(o_ref[sl, hs]) * hn.T).astype(BF16)


def _mlstm(qt, k, ksw, vt, p, ccol, o, gain_b, batch, seq, scan_rows, rows):
    T = batch * seq
    ng = seq // scan_rows
    HK, HV = HEADS * DK, HEADS * DV

    def scan_specs(group_of):
        blk = lambda b, j: b * ng + group_of(j)
        ins = [pl.BlockSpec((HV, scan_rows), lambda b, j: (0, blk(b, j))),
               pl.BlockSpec((scan_rows, HK), lambda b, j: (blk(b, j), 0)),
               pl.BlockSpec((5 * NGR, scan_rows), lambda b, j: (0, blk(b, j)))]
        outs = [pl.BlockSpec((scan_rows // CHUNK, CT_ROWS, 128), lambda b, j: (blk(b, j), 0, 0)),
                pl.BlockSpec((NGR, scan_rows), lambda b, j: (0, blk(b, j)))]
        return ins, outs

    ins_f, outs_f = scan_specs(lambda j: j)
    ins_b, outs_b = scan_specs(lambda j: ng - 1 - j)
    state_shapes = [jax.ShapeDtypeStruct((T // CHUNK, CT_ROWS, 128), BF16),
                    jax.ShapeDtypeStruct((NGR, T), F32)]
    if ng == 1:
        scan_kernel = lambda vt_r, k_r, ksw_r, p_r, *rest: _mlstm_scan_kernel(vt_r, k_r, p_r, vt_r, ksw_r, p_r, *rest)
        scan_in_specs, scan_args = [ins_f[0], ins_f[1], ins_f[1], ins_f[2]], (vt, k, ksw, p)
    else:
        scan_kernel, scan_in_specs, scan_args = _mlstm_scan_kernel, ins_f + ins_b, (vt, k, p, vt, ksw, p)
    ct_f, mp_f, ct_b, mp_b = pl.pallas_call(
        scan_kernel,
        grid=(batch, ng),
        in_specs=scan_in_specs,
        out_specs=outs_f + outs_b,
        out_shape=state_shapes + state_shapes,
        scratch_shapes=[pltpu.VMEM((2, HEADS, DVA, 128), F32), pltpu.VMEM((2, NGR, 128), F32)],
        compiler_params=pltpu.CompilerParams(dimension_semantics=("arbitrary", "arbitrary"),
                                             vmem_limit_bytes=VMEM_LIMIT),
        name="mlstm_scan",
    )(*scan_args)

    row = lambda w: pl.BlockSpec((rows, w), lambda i: (i, 0))
    col = lambda h: pl.BlockSpec((h, rows), lambda i: (0, i))
    hbm = pl.BlockSpec(memory_space=pl.ANY)
    return pl.pallas_call(
        _mlstm_out_kernel,
        grid=(T // rows,),
        in_specs=[col(HK), row(HK), hbm, col(5 * NGR), col(NGR), col(NGR), row(128), hbm, hbm,
                  hbm, _resident((HV, 128))],
        out_specs=row(HV),
        out_shape=jax.ShapeDtypeStruct((T, HV), BF16),
        scratch_shapes=[pltpu.VMEM((RING, HV, rows), BF16),
                        pltpu.VMEM((RING, rows // CHUNK, CT_ROWS, 128), BF16),
                        pltpu.VMEM((RING, rows // CHUNK, CT_ROWS, 128), BF16),
                        pltpu.VMEM((RING, rows, HV), F32),
                        pltpu.SemaphoreType.DMA((4, RING))],
        compiler_params=pltpu.CompilerParams(dimension_semantics=("arbitrary",),
                                             vmem_limit_bytes=VMEM_LIMIT),
        name="mlstm_out",
    )(qt, k, vt, p, mp_f, mp_b, ccol, ct_f, ct_b, o, gain_b)


def kernel(x, norm_ffn1_pre, norm_ffn1_post, w_ffn1_in, w_ffn1_out, norm_mix_pre, norm_mix_post,
           w_mix_in, conv_w, conv_b, gate_i_bias, gate_f_bias, mlstm_norm, w_mix_out,
           norm_ffn2_pre, norm_ffn2_post, w_ffn2_in, w_ffn2_out):
    batch, seq, _ = x.shape
    T = batch * seq
    depth = norm_ffn1_pre.shape[0]
    tm = 512
    xt = x.reshape(T, D_MODEL)
    HV = HEADS * DV
    for l in range(depth):
        xt, w2_in, w2_out, w_mo, w_mi = _ffn(
            xt, norm_ffn1_pre[l][None], norm_ffn1_post[l][None], w_ffn1_in, w_ffn1_out,
            w_ffn2_in, w_ffn2_out, w_mix_out, jnp.swapaxes(w_mix_in, 1, 2), l, tm)

        gbias = jnp.pad(jnp.concatenate([gate_i_bias[l], gate_f_bias[l]]), (0, 128 - 2 * NGR))[None]
        bg, u, qt, k, ksw, vt, o, p, ccol = _mix_in(xt, norm_mix_pre[l][None], w_mi, gbias, 2 * tm)
        gain_b = jnp.broadcast_to(mlstm_norm[l][:, None], (HV, 128))
        y_mlstm = _mlstm(qt, k, ksw, vt, p, ccol, o, gain_b, batch, seq, SCAN_ROWS, MLSTM_ROWS)
        xt = _mix_ffn(xt, bg, u, conv_w[l], conv_b[l][None], y_mlstm, norm_mix_post[l][None], w_mo,
                      norm_ffn2_pre[l][None], norm_ffn2_post[l][None], w2_in, w2_out, seq, tm)
    return xt.reshape(batch, seq, D_MODEL)
```

```python
import functools

import jax
import jax.numpy as jnp
from jax import lax
from jax.experimental import pallas as pl
from jax.experimental.pallas import tpu as pltpu

D_MODEL = 1024
D_FF = 2816
CONV_WIDTH = 512
HEADS = 4
DK = 64
DV = 128
CHUNK = 128
EPS = 1e-6
NEG_INF = -1e30

FF_TILE = 256
FF_STAGE_CHUNKS = 8
FF_SIDE_CHUNKS = 8
MIX_OUT_PIECES = 4
FF_LEAD_TILES = 1
MIX_W_BLOCKS = (3 * CONV_WIDTH + 2 * HEADS * DK + 2 * HEADS * DV) // 128
MLSTM_ROWS = 1024
SCAN_ROWS = 4096
DVA = DV + 16
HALO_ROWS = 8
NGR = 2 * HEADS
CT_ROWS = (HEADS // 2) * DVA
VMEM_LIMIT = 56 * 1024 * 1024

P_E, P_M, P_B, P_G, P_MC = (slice(i * NGR, (i + 1) * NGR) for i in range(5))

F32 = jnp.float32
BF16 = jnp.bfloat16


def _rms(x, g):
    return x * lax.rsqrt(jnp.mean(x * x, axis=-1, keepdims=True) + EPS) * g


def _log_sigmoid(z):
    return jnp.minimum(z, 0.0) - jnp.log1p(jnp.exp(-jnp.abs(z)))


def _resident(shape):
    zeros = (0,) * len(shape)
    return pl.BlockSpec(shape, lambda *_: zeros, pipeline_mode=pl.Buffered(1))


def _split3(x):
    hi = x.astype(BF16)
    r1 = x - hi.astype(F32)
    mid = r1.astype(BF16)
    lo = (r1 - mid.astype(F32)).astype(BF16)
    return hi, mid, lo


def _visible(rev):
    s = lax.broadcasted_iota(jnp.int32, (CHUNK, CHUNK), 0)
    t = lax.broadcasted_iota(jnp.int32, (CHUNK, CHUNK), 1)
    return (s >= t) if rev else (s <= t)


def _running_max(x, rev):
    n = x.shape[1]
    pos = lax.broadcasted_iota(jnp.int32, x.shape, 1) & (CHUNK - 1)
    k = 1
    while k < CHUNK:
        if rev:
            shifted, ok = pltpu.roll(x, n - k, 1), pos < CHUNK - k
        else:
            shifted, ok = pltpu.roll(x, k, 1), pos >= k
        x = jnp.maximum(x, jnp.where(ok, shifted, NEG_INF))
        k *= 2
    return x


def _stage_bf16(src_hbm, dst_ref, stage_ref, sem, rows=None):
    rows = stage_ref.shape[1] if rows is None else rows
    n_chunks = src_hbm.shape[0] // rows

    def copy(c):
        return pltpu.make_async_copy(src_hbm.at[pl.ds(c * rows, rows), :],
                                     stage_ref.at[c % 2, pl.ds(0, rows), :], sem.at[c % 2])

    copy(0).start()
    for c in range(n_chunks):
        if c + 1 < n_chunks:
            copy(c + 1).start()
        copy(c).wait()
        dst_ref[c * rows:(c + 1) * rows, :] = stage_ref[c % 2, 0:rows, :].astype(BF16)


def _mix_out_tile(tiles_per_seq, x_ref, bg_ref, u_ref, uprev_ref, unext_ref, cw_ref, cb_ref,
                  y_ref, wmo_ref, gmix_ref, gpre_ref):
    i = pl.program_id(0)
    tm = u_ref.shape[0]
    u = u_ref[...]
    has_prev = (i % tiles_per_seq != 0).astype(F32)
    has_next = (i % tiles_per_seq != tiles_per_seq - 1).astype(F32)
    prev_row = uprev_ref[HALO_ROWS - 1:HALO_ROWS, :] * has_prev
    next_row = unext_ref[0:1, :] * has_next
    ri = lax.broadcasted_iota(jnp.int32, u.shape, 0)
    u_m1 = jnp.where(ri == 0, prev_row, pltpu.roll(u, 1, 0))
    u_p1 = jnp.where(ri == tm - 1, next_row, pltpu.roll(u, tm - 1, 0))
    conv = cw_ref[0:1, :] * u_m1 + cw_ref[1:2, :] * u + cw_ref[2:3, :] * u_p1
    y_conv = (bg_ref[...] * (conv + cb_ref[...])).astype(BF16)
    rp = tm // MIX_OUT_PIECES
    xs, xns = [], []
    for r in range(MIX_OUT_PIECES):
        rs = slice(r * rp, (r + 1) * rp)
        h = jnp.dot(y_ref[rs, :], wmo_ref[CONV_WIDTH:, :], preferred_element_type=F32) \
            + jnp.dot(y_conv[rs, :], wmo_ref[0:CONV_WIDTH, :], preferred_element_type=F32)
        xs.append(x_ref[rs, :] + _rms(h, gmix_ref[...]))
        xns.append(_rms(xs[-1], gpre_ref[...]).astype(BF16))
    return jnp.concatenate(xs, axis=0), xns


def _ffn_tile(x, xn_pieces, gpost_ref, win_ref, wout_ref, h_ref):
    rp = xn_pieces[0].shape[0]
    xn = jnp.concatenate(xn_pieces, axis=0)
    for j in range(D_FF // FF_TILE):
        lo = j * FF_TILE
        lhs = list(enumerate(xn_pieces)) if j < FF_LEAD_TILES and len(xn_pieces) > 1 else [(None, xn)]
        for r, xr in lhs:
            rs = slice(None) if r is None else slice(r * rp, (r + 1) * rp)
            gate = jnp.dot(xr, win_ref[:, lo:lo + FF_TILE], preferred_element_type=F32)
            up = jnp.dot(xr, win_ref[:, D_FF + lo:D_FF + lo + FF_TILE], preferred_element_type=F32)
            h_ref[rs, lo:lo + FF_TILE] = (gate * jax.nn.sigmoid(gate) * up).astype(BF16)
    y = jnp.dot(h_ref[...], wout_ref[...], preferred_element_type=F32)
    return x + _rms(y, 0.5 * gpost_ref[...])


def _zero_after(v):
    u = pltpu.bitcast(v, jnp.uint32)
    acc = None
    for r in range(u.shape[0] // 8):
        for c in range(u.shape[1] // 128):
            t = u[r * 8:(r + 1) * 8, c * 128:(c + 1) * 128]
            acc = t if acc is None else acc | t
    return pltpu.bitcast((acc >> 16) >> 16, F32)


def _convert_later_weights(i, w2in_blk, w2out_blk, wmo_blk, wt_blk, wtg_blk, w2in_bf, w2out_bf, wmo_bf, wb_blk):
    zeros = []
    for src, dst in ((w2in_blk, w2in_bf), (w2out_blk, w2out_bf), (wmo_blk, wmo_bf)):
        v = src[...].astype(BF16)
        dst[...] = v
        zeros.append(_zero_after(v[0:16, 0:128]))
    gates = jnp.concatenate([wtg_blk[...], jnp.zeros((128 - 2 * NGR, D_MODEL), F32)], axis=0)
    last = jnp.clip(i - 1, 0, MIX_W_BLOCKS) == MIX_W_BLOCKS
    blk = jnp.where(last, gates, wt_blk[...]).T
    wb_blk[...] = blk.astype(BF16)
    zeros.append(_zero_after(blk))
    return zeros


def _ffn_kernel(layer, n_tiles, xa_ref, xc_ref, gpre_ref, gpost_ref, win_hbm, wout_hbm,
                w2in_blk, w2out_blk, wmo_blk, wt_blk, wtg_blk,
                o_ref, w2in_bf, w2out_bf, wmo_bf, wb_blk,
                h_ref, win_ref, wout_ref, stage_in, stage_out, sem, xn_ref, y_ref):
    i = pl.program_id(0)
    tm = xa_ref.shape[0]
    g_half = 0.5 * gpost_ref[...]

    @pl.when(i == 0)
    def _():
        _stage_bf16(win_hbm.at[layer], win_ref, stage_in, sem)
        _stage_bf16(wout_hbm.at[layer], wout_ref, stage_out, sem)
        xn_ref[0] = _rms(xa_ref[...], gpre_ref[...]).astype(BF16)
        y_ref[...] = jnp.zeros(y_ref.shape, F32)

    @pl.when((i >= 1) & (i <= n_tiles))
    def _():
        xn = xn_ref[(i - 1) % 2]
        rp = tm // FF_SIDE_CHUNKS
        for j in range(D_FF // FF_TILE):
            lo = j * FF_TILE
            gate = jnp.dot(xn, win_ref[:, lo:lo + FF_TILE], preferred_element_type=F32)
            up = jnp.dot(xn, win_ref[:, D_FF + lo:D_FF + lo + FF_TILE], preferred_element_type=F32)
            hm = gate * jax.nn.sigmoid(gate) * up
            h_ref[:, lo:lo + FF_TILE] = hm.astype(BF16)
            if j < FF_SIDE_CHUNKS:
                rs = slice(j * rp, (j + 1) * rp)
                out_rows = xc_ref[rs, :] + _rms(y_ref[rs, :], g_half)
                o_ref[rs, :] = out_rows
                xn_rows = _rms(xa_ref[rs, :], gpre_ref[...])
                xn_ref[i % 2, rs, :] = xn_rows.astype(BF16)
                zero = _zero_after(out_rows) + _zero_after(xn_rows)
                h_ref[0:8, lo:lo + 128] = (hm[0:8, 0:128] + zero).astype(BF16)
            else:
                if j == FF_SIDE_CHUNKS:
                    side = _convert_later_weights(i, w2in_blk, w2out_blk, wmo_blk, wt_blk, wtg_blk,
                                                  w2in_bf, w2out_bf, wmo_bf, wb_blk)
                    side = [side[0], side[1] + side[2], side[3]]
                h_ref[0:8, lo:lo + 128] = (hm[0:8, 0:128] + side[j - FF_SIDE_CHUNKS]).astype(BF16)
        y_ref[...] = jnp.dot(h_ref[...], wout_ref[...], preferred_element_type=F32)

    @pl.when(i == n_tiles + 1)
    def _():
        o_ref[...] = xc_ref[...] + _rms(y_ref[...], g_half)


def _mix_ffn_kernel(tiles_per_seq, x_ref, bg_ref, u_ref, uprev_ref, unext_ref, cw_ref, cb_ref,
                    y_ref, gmix_ref, wmo_ref, gpre_ref, gpost_ref, win_ref, wout_ref, o_ref, h_ref):
    x, xn = _mix_out_tile(tiles_per_seq, x_ref, bg_ref, u_ref, uprev_ref, unext_ref, cw_ref, cb_ref,
                          y_ref, wmo_ref, gmix_ref, gpre_ref)
    o_ref[...] = _ffn_tile(x, xn, gpost_ref, win_ref, wout_ref, h_ref)


def _ffn_scratch(tm):
    return [
        pltpu.VMEM((tm, D_FF), BF16),
        pltpu.VMEM((D_MODEL, 2 * D_FF), BF16),
        pltpu.VMEM((D_FF, D_MODEL), BF16),
    ], [
        pltpu.VMEM((2, D_MODEL // FF_STAGE_CHUNKS, 2 * D_FF), F32),
        pltpu.VMEM((2, D_FF // FF_STAGE_CHUNKS, D_MODEL), F32),
        pltpu.SemaphoreType.DMA((2,)),
    ]


def _ffn(x, gpre, gpost, w_in, w_out, w2_in, w2_out, w_mix_out, w_mix_in_t, layer, tm):
    T = x.shape[0]
    nt = T // tm
    assert nt > MIX_W_BLOCKS and nt % 2 == 0, f"first FFN needs more than {MIX_W_BLOCKS} token tiles, got {nt}"
    g0 = MIX_W_BLOCKS * 128
    resident, staging = _ffn_scratch(tm)
    tile = lambda lag: pl.BlockSpec((tm, D_MODEL), lambda i: (jnp.clip(i - lag, 0, nt - 1), 0))
    rows_in = lambda n, w, last: pl.BlockSpec((None, n, w), lambda i: (layer, jnp.clip(i - 1, 0, last), 0))
    rows_out = lambda n, w, last: pl.BlockSpec((n, w), lambda i: (jnp.clip(i - 1, 0, last), 0))
    r_in, r_out, r_mo = D_MODEL // nt, D_FF // (nt // 2), D_MODEL // (nt // 2)
    return pl.pallas_call(
        functools.partial(_ffn_kernel, layer, nt),
        grid=(nt + 2,),
        in_specs=[
            tile(0), tile(2),
            _resident((1, D_MODEL)),
            _resident((1, D_MODEL)),
            pl.BlockSpec(memory_space=pl.ANY),
            pl.BlockSpec(memory_space=pl.ANY),
            rows_in(r_in, 2 * D_FF, nt - 1),
            rows_in(r_out, D_MODEL, nt // 2 - 1),
            rows_in(r_mo, D_MODEL, nt // 2 - 1),
            rows_in(128, D_MODEL, MIX_W_BLOCKS - 1),
            pl.BlockSpec((None, 2 * NGR, D_MODEL), lambda i: (layer, g0 // (2 * NGR), 0)),
        ],
        out_specs=[
            tile(2),
            rows_out(r_in, 2 * D_FF, nt - 1),
            rows_out(r_out, D_MODEL, nt // 2 - 1),
            rows_out(r_mo, D_MODEL, nt // 2 - 1),
            pl.BlockSpec((D_MODEL, 128), lambda i: (0, jnp.clip(i - 1, 0, MIX_W_BLOCKS))),
        ],
        out_shape=[
            jax.ShapeDtypeStruct((T, D_MODEL), F32),
            jax.ShapeDtypeStruct((D_MODEL, 2 * D_FF), BF16),
            jax.ShapeDtypeStruct((D_FF, D_MODEL), BF16),
            jax.ShapeDtypeStruct((D_MODEL, D_MODEL), BF16),
            jax.ShapeDtypeStruct((D_MODEL, g0 + 128), BF16),
        ],
        scratch_shapes=resident + staging + [pltpu.VMEM((2, tm, D_MODEL), BF16),
                                             pltpu.VMEM((tm, D_MODEL), F32)],
        compiler_params=pltpu.CompilerParams(
            dimension_semantics=("arbitrary",), vmem_limit_bytes=VMEM_LIMIT),
        name="ffn",
    )(x, x, gpre, gpost, w_in, w_out, w2_in, w2_out, w_mix_out, w_mix_in_t, w_mix_in_t)


def _mix_ffn(x, bg, u, conv_w, conv_b, y_mlstm, gmix, w_mix_out, gpre, gpost, w_in, w_out, seq, tm):
    T = x.shape[0]
    tiles_per_seq = seq // tm
    sub = tm // HALO_ROWS
    last = T // HALO_ROWS - 1
    row = lambda w: pl.BlockSpec((tm, w), lambda i: (i, 0))
    return pl.pallas_call(
        functools.partial(_mix_ffn_kernel, tiles_per_seq),
        grid=(T // tm,),
        in_specs=[
            row(D_MODEL), row(CONV_WIDTH), row(CONV_WIDTH),
            pl.BlockSpec((HALO_ROWS, CONV_WIDTH), lambda i: (jnp.maximum(i * sub - 1, 0), 0)),
            pl.BlockSpec((HALO_ROWS, CONV_WIDTH), lambda i: (jnp.minimum((i + 1) * sub, last), 0)),
            _resident((3, CONV_WIDTH)), _resident((1, CONV_WIDTH)),
            row(HEADS * DV), _resident((1, D_MODEL)), _resident(w_mix_out.shape),
            _resident((1, D_MODEL)), _resident((1, D_MODEL)), _resident(w_in.shape), _resident(w_out.shape),
        ],
        out_specs=row(D_MODEL),
        out_shape=jax.ShapeDtypeStruct((T, D_MODEL), F32),
        scratch_shapes=[pltpu.VMEM((tm, D_FF), BF16)],
        compiler_params=pltpu.CompilerParams(
            dimension_semantics=("arbitrary",), vmem_limit_bytes=VMEM_LIMIT),
        name="mix_ffn",
    )(x, bg, u, u, u, conv_w, conv_b, y_mlstm, gmix, w_mix_out, gpre, gpost, w_in, w_out)


def _gate_rows(zr, p_ref, ccol_ref):
    L = CHUNK
    n_chunks = zr.shape[1] // L
    fwd_row = lax.broadcasted_iota(jnp.int32, (NGR, L), 0) < HEADS
    fwd_col = lax.broadcasted_iota(jnp.int32, (NGR, 1), 0) < HEADS
    li = zr[0:NGR]
    gates = jnp.concatenate([li, _log_sigmoid(zr[NGR:2 * NGR])], axis=0)
    x3 = jnp.concatenate(_split3(gates), axis=0)
    stacked = jnp.concatenate([x3[:, c * L:(c + 1) * L] for c in range(n_chunks)], axis=0)
    cum_ops = jnp.concatenate([_visible(False), _visible(True)], axis=1).astype(BF16)
    cum = jnp.dot(stacked, cum_ops, preferred_element_type=F32)
    b_chunks = []
    for c in range(n_chunks):
        blk = cum[c * 48:(c + 1) * 48]
        s16 = blk[0:16] + blk[16:32] + blk[32:48]
        b_chunks.append(jnp.where(fwd_row, s16[NGR:2 * NGR, 0:L], s16[NGR:2 * NGR, L:2 * L]))
    b = jnp.concatenate(b_chunks, axis=1)
    cc = li - b
    fwd_all = lax.broadcasted_iota(jnp.int32, cc.shape, 0) < HEADS
    p_ref[P_M, :] = jnp.where(fwd_all, _running_max(cc, False), _running_max(cc, True))
    p_ref[P_B, :] = b
    pad = jnp.zeros((L - NGR, L), F32)
    for c in range(n_chunks):
        sl = slice(c * L, (c + 1) * L)
        b_c, cc_c = b_chunks[c], cc[:, sl]
        g = jnp.where(fwd_col, b_c[:, L - 1:L], b_c[:, 0:1])
        m_chunk = g + jnp.max(cc_c, axis=1, keepdims=True)
        p_ref[P_E, sl] = jnp.exp(g + cc_c - m_chunk)
        p_ref[P_G, sl] = jnp.broadcast_to(g, (NGR, L))
        p_ref[P_MC, sl] = jnp.broadcast_to(m_chunk, (NGR, L))
        ccol_ref[sl, :] = jnp.concatenate([cc_c, pad], axis=0).T


def _mix_in_kernel(x_ref, gpre_ref, wb_ref, gbias_ref,
                   bg_ref, u_ref, qt_ref, k_ref, ksw_ref, vt_ref, o_ref, p_ref, ccol_ref):
    W, HK, HV = CONV_WIDTH, HEADS * DK, HEADS * DV
    q0 = 3 * W
    k0, v0 = q0 + HK, q0 + 2 * HK
    o0 = v0 + HV
    g0 = o0 + HV

    xn = _rms(x_ref[...], gpre_ref[...]).astype(BF16)
    proj = lambda a, b: jnp.dot(xn, wb_ref[:, a:b], preferred_element_type=F32)

    zg = proj(g0, g0 + 128) + gbias_ref[...]
    bg_ref[...] = proj(0, W)
    u_ref[...] = proj(W, 2 * W) * proj(2 * W, 3 * W)
    _gate_rows(zg.T[0:2 * NGR], p_ref, ccol_ref)
    qt_ref[...] = (proj(q0, k0) * (DK ** -0.5)).T.astype(BF16)
    kk = proj(k0, v0)
    k_ref[...] = kk.astype(BF16)
    for pair in range(HEADS // 2):
        ps = slice(pair * 2 * DK, (pair + 1) * 2 * DK)
        ksw_ref[:, ps] = pltpu.roll(kk[:, ps], DK, 1).astype(BF16)
    vt_ref[...] = proj(v0, o0).T.astype(BF16)
    o_ref[...] = proj(o0, g0)


def _mix_in(x, gpre, wb, gbias, tm):
    T = x.shape[0]
    row = lambda w: pl.BlockSpec((tm, w), lambda i: (i, 0))
    col = lambda h: pl.BlockSpec((h, tm), lambda i: (0, i))
    HK, HV = HEADS * DK, HEADS * DV
    return pl.pallas_call(
        _mix_in_kernel,
        grid=(T // tm,),
        in_specs=[row(D_MODEL), _resident((1, D_MODEL)), _resident(wb.shape), _resident((1, 128))],
        out_specs=[row(CONV_WIDTH), row(CONV_WIDTH), col(HK), row(HK), row(HK), col(HV), row(HV),
                   col(5 * NGR), row(128)],
        out_shape=[
            jax.ShapeDtypeStruct((T, CONV_WIDTH), F32),
            jax.ShapeDtypeStruct((T, CONV_WIDTH), F32),
            jax.ShapeDtypeStruct((HK, T), BF16),
            jax.ShapeDtypeStruct((T, HK), BF16),
            jax.ShapeDtypeStruct((T, HK), BF16),
            jax.ShapeDtypeStruct((HV, T), BF16),
            jax.ShapeDtypeStruct((T, HV), F32),
            jax.ShapeDtypeStruct((5 * NGR, T), F32),
            jax.ShapeDtypeStruct((T, 128), F32),
        ],
        compiler_params=pltpu.CompilerParams(
            dimension_semantics=("arbitrary",), vmem_limit_bytes=VMEM_LIMIT),
        name="mix_in",
    )(x, gpre, wb, gbias)


def _state_half(d, h):
    return (h % 2) ^ d


def _scan_direction(d, vt_ref, k_ref, p_ref, ct_ref, mp_ref, ct_state, m_state):
    L = CHUNK
    n_chunks = k_ref.shape[0] // L
    lane_half = lax.broadcasted_iota(jnp.int32, (L, 128), 1) // DK
    lane_half_s = lax.broadcasted_iota(jnp.int32, (DVA, 128), 1) // DK
    ones_rows = jnp.ones((DVA - DV, L), BF16)

    cts = [ct_state[d, h] for h in range(HEADS)]
    m_prev = m_state[d]
    for chunk in (range(n_chunks - 1, -1, -1) if d == 1 else range(n_chunks)):
        sl = slice(chunk * L, (chunk + 1) * L)
        e, g, m_chunk = p_ref[P_E, sl], p_ref[P_G, sl], p_ref[P_MC, sl]
        mp_ref[:, sl] = m_prev
        m_new = jnp.maximum(g + m_prev, m_chunk)
        a_old = jnp.exp(g + m_prev - m_new)
        a_new = jnp.exp(m_chunk - m_new)
        new_cts = []
        for h in range(HEADS):
            r = HEADS * d + h
            vt_aug = jnp.concatenate([vt_ref[h * DV:(h + 1) * DV, sl], ones_rows], axis=0)
            vte = (vt_aug.astype(F32) * e[r:r + 1, :]).astype(BF16)
            pair = slice((h // 2) * 2 * DK, (h // 2 + 1) * 2 * DK)
            k_half = jnp.where(lane_half == _state_half(d, h), k_ref[sl, pair], jnp.zeros((L, 128), BF16))
            ct_chunk = jnp.dot(vte, k_half, preferred_element_type=F32)
            new_cts.append(a_old[r:r + 1, 0:1] * cts[h] + a_new[r:r + 1, 0:1] * ct_chunk)
        for pr in range(HEADS // 2):
            both = jnp.where(lane_half_s == _state_half(d, 2 * pr), cts[2 * pr], cts[2 * pr + 1])
            ct_ref[chunk, pr * DVA:(pr + 1) * DVA, :] = both.astype(BF16)
        cts = new_cts
        m_prev = m_new
    for h in range(HEADS):
        ct_state[d, h] = cts[h]
    m_state[d] = m_prev


def _mlstm_scan_kernel(vt_f, k_f, p_f, vt_b, ksw_b, p_b, ct_f, mp_f, ct_b, mp_b, ct_state, m_state):
    @pl.when(pl.program_id(1) == 0)
    def _():
        ct_state[...] = jnp.zeros(ct_state.shape, F32)
        m_state[...] = jnp.full(m_state.shape, NEG_INF, F32)

    _scan_direction(0, vt_f, k_f, p_f, ct_f, mp_f, ct_state, m_state)
    _scan_direction(1, vt_b, ksw_b, p_b, ct_b, mp_b, ct_state, m_state)


def _mlstm_out_kernel(qt_ref, k_ref, vt_ref, p_ref, mp_f, mp_b, ccol_ref, ct_f, ct_b,
                      o_ref, gain_ref, y_ref):
    L = CHUNK
    n_chunks = k_ref.shape[0] // L
    visible = (_visible(False), _visible(True))
    fwd_row = lax.broadcasted_iota(jnp.int32, (NGR, L), 0) < HEADS
    lane_half = lax.broadcasted_iota(jnp.int32, (DVA, 128), 1) // DK
    ones_rows = jnp.ones((DVA - DV, L), BF16)
    zq = jnp.zeros((DK, L), BF16)
    for chunk in range(n_chunks):
        sl = slice(chunk * L, (chunk + 1) * L)
        m_prev = jnp.where(fwd_row, mp_f[:, sl], mp_b[:, sl])
        n_t = jnp.maximum(m_prev, p_ref[P_M, sl])
        f_inter = jnp.exp(m_prev - n_t)
        e_min = jnp.exp(-(p_ref[P_B, sl] + n_t))
        ccol = ccol_ref[sl, :]
        for h in range(HEADS):
            hs = slice(h * DV, (h + 1) * DV)
            qt = qt_ref[h * DK:(h + 1) * DK, sl]
            k_pair = k_ref[sl, (h // 2) * 2 * DK:(h // 2 + 1) * 2 * DK]
            qt_pair = jnp.concatenate([qt, zq] if h % 2 == 0 else [zq, qt], axis=0)
            st = jnp.dot(k_pair, qt_pair, preferred_element_type=F32)
            pts, qfs = [], []
            for d in range(2):
                r = HEADS * d + h
                arg = jnp.where(visible[d], ccol[:, r:r + 1] - n_t[r:r + 1, :], NEG_INF)
                pts.append((jnp.exp(arg) * st).astype(BF16))
                qfs.append((qt.astype(F32) * f_inter[r:r + 1, :]).astype(BF16))
            inter = [jnp.concatenate([qfs[0], zq], axis=1), jnp.concatenate([zq, qfs[1]], axis=1)]
            first = 0 if _state_half(0, h) == 0 else 1
            rhs = jnp.concatenate([jnp.concatenate(pts, axis=1), inter[first], inter[1 - first]], axis=0)
            vt_aug = jnp.concatenate([vt_ref[hs, sl], ones_rows], axis=0)
            ps = slice((h // 2) * DVA, (h // 2 + 1) * DVA)
            blocks = (ct_f[chunk, ps, :], ct_b[chunk, ps, :])
            ct = jnp.where(lane_half == 0, blocks[first], blocks[1 - first])
            both = jnp.dot(jnp.concatenate([vt_aug, ct], axis=1), rhs, preferred_element_type=F32)
            ht = None
            for d in range(2):
                r = HEADS * d + h
                numer = both[0:DV, d * L:(d + 1) * L]
                denom = both[DV:DV + 1, d * L:(d + 1) * L]
                part = numer / jnp.maximum(jnp.abs(denom), e_min[r:r + 1, :])
                ht = part if ht is None else ht + part
            ms = jnp.mean(ht * ht, axis=0, keepdims=True)
            hn = ht * lax.rsqrt(ms + EPS) * gain_ref[hs, :]
            y_ref[sl, hs] = (jax.nn.sigmoid(o_ref[sl, hs]) * hn.T).astype(BF16)


def _mlstm(qt, k, ksw, vt, p, ccol, o, gain_b, batch, seq, scan_rows, rows):
    T = batch * seq
    ng = seq // scan_rows
    HK, HV = HEADS * DK, HEADS * DV

    def scan_specs(group_of):
        blk = lambda b, j: b * ng + group_of(j)
        ins = [pl.BlockSpec((HV, scan_rows), lambda b, j: (0, blk(b, j))),
               pl.BlockSpec((scan_rows, HK), lambda b, j: (blk(b, j), 0)),
               pl.BlockSpec((5 * NGR, scan_rows), lambda b, j: (0, blk(b, j)))]
        outs = [pl.BlockSpec((scan_rows // CHUNK, CT_ROWS, 128), lambda b, j: (blk(b, j), 0, 0)),
                pl.BlockSpec((NGR, scan_rows), lambda b, j: (0, blk(b, j)))]
        return ins, outs

    ins_f, outs_f = scan_specs(lambda j: j)
    ins_b, outs_b = scan_specs(lambda j: ng - 1 - j)
    state_shapes = [jax.ShapeDtypeStruct((T // CHUNK, CT_ROWS, 128), BF16),
                    jax.ShapeDtypeStruct((NGR, T), F32)]
    if ng == 1:
        scan_kernel = lambda vt_r, k_r, ksw_r, p_r, *rest: _mlstm_scan_kernel(vt_r, k_r, p_r, vt_r, ksw_r, p_r, *rest)
        scan_in_specs, scan_args = [ins_f[0], ins_f[1], ins_f[1], ins_f[2]], (vt, k, ksw, p)
    else:
        scan_kernel, scan_in_specs, scan_args = _mlstm_scan_kernel, ins_f + ins_b, (vt, k, p, vt, ksw, p)
    ct_f, mp_f, ct_b, mp_b = pl.pallas_call(
        scan_kernel,
        grid=(batch, ng),
        in_specs=scan_in_specs,
        out_specs=outs_f + outs_b,
        out_shape=state_shapes + state_shapes,
        scratch_shapes=[pltpu.VMEM((2, HEADS, DVA, 128), F32), pltpu.VMEM((2, NGR, 128), F32)],
        compiler_params=pltpu.CompilerParams(dimension_semantics=("arbitrary", "arbitrary"),
                                             vmem_limit_bytes=VMEM_LIMIT),
        name="mlstm_scan",
    )(*scan_args)

    row = lambda w: pl.BlockSpec((rows, w), lambda i: (i, 0))
    col = lambda h: pl.BlockSpec((h, rows), lambda i: (0, i))
    ctb = pl.BlockSpec((rows // CHUNK, CT_ROWS, 128), lambda i: (i, 0, 0))
    return pl.pallas_call(
        _mlstm_out_kernel,
        grid=(T // rows,),
        in_specs=[col(HK), row(HK), col(HV), col(5 * NGR), col(NGR), col(NGR), row(128), ctb, ctb,
                  row(HV), _resident((HV, 128))],
        out_specs=row(HV),
        out_shape=jax.ShapeDtypeStruct((T, HV), BF16),
        compiler_params=pltpu.CompilerParams(dimension_semantics=("arbitrary",),
                                             vmem_limit_bytes=VMEM_LIMIT),
        name="mlstm_out",
    )(qt, k, vt, p, mp_f, mp_b, ccol, ct_f, ct_b, o, gain_b)


def kernel(x, norm_ffn1_pre, norm_ffn1_post, w_ffn1_in, w_ffn1_out, norm_mix_pre, norm_mix_post,
           w_mix_in, conv_w, conv_b, gate_i_bias, gate_f_bias, mlstm_norm, w_mix_out,
           norm_ffn2_pre, norm_ffn2_post, w_ffn2_in, w_ffn2_out):
    batch, seq, _ = x.shape
    T = batch * seq
    depth = norm_ffn1_pre.shape[0]
    tm = 512
    xt = x.reshape(T, D_MODEL)
    HV = HEADS * DV
    for l in range(depth):
        xt, w2_in, w2_out, w_mo, w_mi = _ffn(
            xt, norm_ffn1_pre[l][None], norm_ffn1_post[l][None], w_ffn1_in, w_ffn1_out,
            w_ffn2_in, w_ffn2_out, w_mix_out, jnp.swapaxes(w_mix_in, 1, 2), l, tm)

        gbias = jnp.pad(jnp.concatenate([gate_i_bias[l], gate_f_bias[l]]), (0, 128 - 2 * NGR))[None]
        bg, u, qt, k, ksw, vt, o, p, ccol = _mix_in(xt, norm_mix_pre[l][None], w_mi, gbias, 2 * tm)
        gain_b = jnp.broadcast_to(mlstm_norm[l][:, None], (HV, 128))
        y_mlstm = _mlstm(qt, k, ksw, vt, p, ccol, o, gain_b, batch, seq, SCAN_ROWS, MLSTM_ROWS)
        xt = _mix_ffn(xt, bg, u, conv_w[l], conv_b[l][None], y_mlstm, norm_mix_post[l][None], w_mo,
                      norm_ffn2_pre[l][None], norm_ffn2_post[l][None], w2_in, w2_out, seq, tm)
    return xt.reshape(batch, seq, D_MODEL)
```

```python
import functools

import jax
import jax.numpy as jnp
from jax import lax
from jax.experimental import pallas as pl
from jax.experimental.pallas import tpu as pltpu

D_MODEL = 1024
D_FF = 2816
CONV_WIDTH = 512
HEADS = 4
DK = 64
DV = 128
CHUNK = 128
EPS = 1e-6
NEG_INF = -1e30

FF_TILE = 256
FF_STAGE_CHUNKS = 8
FF_SIDE_CHUNKS = 8
MIX_OUT_PIECES = 4
FF_LEAD_TILES = 1
MIX_W_BLOCKS = (3 * CONV_WIDTH + 2 * HEADS * DK + 2 * HEADS * DV) // 128
MLSTM_ROWS = 2048
SCAN_ROWS = 4096
DVA = DV + 16
HALO_ROWS = 8
NGR = 2 * HEADS
CT_ROWS = (HEADS // 2) * DVA
VMEM_LIMIT = 56 * 1024 * 1024

P_E, P_M, P_B, P_G, P_MC = (slice(i * NGR, (i + 1) * NGR) for i in range(5))

F32 = jnp.float32
BF16 = jnp.bfloat16


def _rms(x, g):
    return x * lax.rsqrt(jnp.mean(x * x, axis=-1, keepdims=True) + EPS) * g


def _log_sigmoid(z):
    return jnp.minimum(z, 0.0) - jnp.log1p(jnp.exp(-jnp.abs(z)))


def _resident(shape):
    zeros = (0,) * len(shape)
    return pl.BlockSpec(shape, lambda *_: zeros, pipeline_mode=pl.Buffered(1))


def _split3(x):
    hi = x.astype(BF16)
    r1 = x - hi.astype(F32)
    mid = r1.astype(BF16)
    lo = (r1 - mid.astype(F32)).astype(BF16)
    return hi, mid, lo


def _visible(rev):
    s = lax.broadcasted_iota(jnp.int32, (CHUNK, CHUNK), 0)
    t = lax.broadcasted_iota(jnp.int32, (CHUNK, CHUNK), 1)
    return (s >= t) if rev else (s <= t)


def _running_max(x, rev):
    n = x.shape[1]
    pos = lax.broadcasted_iota(jnp.int32, x.shape, 1) & (CHUNK - 1)
    k = 1
    while k < CHUNK:
        if rev:
            shifted, ok = pltpu.roll(x, n - k, 1), pos < CHUNK - k
        else:
            shifted, ok = pltpu.roll(x, k, 1), pos >= k
        x = jnp.maximum(x, jnp.where(ok, shifted, NEG_INF))
        k *= 2
    return x


def _stage_bf16(src_hbm, dst_ref, stage_ref, sem, rows=None):
    rows = stage_ref.shape[1] if rows is None else rows
    n_chunks = src_hbm.shape[0] // rows

    def copy(c):
        return pltpu.make_async_copy(src_hbm.at[pl.ds(c * rows, rows), :],
                                     stage_ref.at[c % 2, pl.ds(0, rows), :], sem.at[c % 2])

    copy(0).start()
    for c in range(n_chunks):
        if c + 1 < n_chunks:
            copy(c + 1).start()
        copy(c).wait()
        dst_ref[c * rows:(c + 1) * rows, :] = stage_ref[c % 2, 0:rows, :].astype(BF16)


def _mix_out_tile(tiles_per_seq, x_ref, bg_ref, u_ref, uprev_ref, unext_ref, cw_ref, cb_ref,
                  y_ref, wmo_ref, gmix_ref, gpre_ref):
    i = pl.program_id(0)
    tm = u_ref.shape[0]
    u = u_ref[...]
    has_prev = (i % tiles_per_seq != 0).astype(F32)
    has_next = (i % tiles_per_seq != tiles_per_seq - 1).astype(F32)
    prev_row = uprev_ref[HALO_ROWS - 1:HALO_ROWS, :] * has_prev
    next_row = unext_ref[0:1, :] * has_next
    ri = lax.broadcasted_iota(jnp.int32, u.shape, 0)
    u_m1 = jnp.where(ri == 0, prev_row, pltpu.roll(u, 1, 0))
    u_p1 = jnp.where(ri == tm - 1, next_row, pltpu.roll(u, tm - 1, 0))
    conv = cw_ref[0:1, :] * u_m1 + cw_ref[1:2, :] * u + cw_ref[2:3, :] * u_p1
    y_conv = (bg_ref[...] * (conv + cb_ref[...])).astype(BF16)
    rp = tm // MIX_OUT_PIECES
    xs, xns = [], []
    for r in range(MIX_OUT_PIECES):
        rs = slice(r * rp, (r + 1) * rp)
        h = jnp.dot(y_ref[rs, :], wmo_ref[CONV_WIDTH:, :], preferred_element_type=F32) \
            + jnp.dot(y_conv[rs, :], wmo_ref[0:CONV_WIDTH, :], preferred_element_type=F32)
        xs.append(x_ref[rs, :] + _rms(h, gmix_ref[...]))
        xns.append(_rms(xs[-1], gpre_ref[...]).astype(BF16))
    return jnp.concatenate(xs, axis=0), xns


def _ffn_tile(x, xn_pieces, gpost_ref, win_ref, wout_ref, h_ref):
    rp = xn_pieces[0].shape[0]
    xn = jnp.concatenate(xn_pieces, axis=0)
    for j in range(D_FF // FF_TILE):
        lo = j * FF_TILE
        lhs = list(enumerate(xn_pieces)) if j < FF_LEAD_TILES and len(xn_pieces) > 1 else [(None, xn)]
        for r, xr in lhs:
            rs = slice(None) if r is None else slice(r * rp, (r + 1) * rp)
            gate = jnp.dot(xr, win_ref[:, lo:lo + FF_TILE], preferred_element_type=F32)
            up = jnp.dot(xr, win_ref[:, D_FF + lo:D_FF + lo + FF_TILE], preferred_element_type=F32)
            h_ref[rs, lo:lo + FF_TILE] = (gate * jax.nn.sigmoid(gate) * up).astype(BF16)
    y = jnp.dot(h_ref[...], wout_ref[...], preferred_element_type=F32)
    return x + _rms(y, 0.5 * gpost_ref[...])


def _zero_after(v):
    u = pltpu.bitcast(v, jnp.uint32)
    acc = None
    for r in range(u.shape[0] // 8):
        for c in range(u.shape[1] // 128):
            t = u[r * 8:(r + 1) * 8, c * 128:(c + 1) * 128]
            acc = t if acc is None else acc | t
    return pltpu.bitcast((acc >> 16) >> 16, F32)


def _convert_later_weights(i, w2in_blk, w2out_blk, wmo_blk, wt_blk, wtg_blk, w2in_bf, w2out_bf, wmo_bf, wb_blk):
    zeros = []
    for src, dst in ((w2in_blk, w2in_bf), (w2out_blk, w2out_bf), (wmo_blk, wmo_bf)):
        v = src[...].astype(BF16)
        dst[...] = v
        zeros.append(_zero_after(v[0:16, 0:128]))
    gates = jnp.concatenate([wtg_blk[...], jnp.zeros((128 - 2 * NGR, D_MODEL), F32)], axis=0)
    last = jnp.clip(i - 1, 0, MIX_W_BLOCKS) == MIX_W_BLOCKS
    blk = jnp.where(last, gates, wt_blk[...]).T
    wb_blk[...] = blk.astype(BF16)
    zeros.append(_zero_after(blk))
    return zeros


def _ffn_kernel(layer, n_tiles, xa_ref, xc_ref, gpre_ref, gpost_ref, win_hbm, wout_hbm,
                w2in_blk, w2out_blk, wmo_blk, wt_blk, wtg_blk,
                o_ref, w2in_bf, w2out_bf, wmo_bf, wb_blk,
                h_ref, win_ref, wout_ref, stage_in, stage_out, sem, xn_ref, y_ref):
    i = pl.program_id(0)
    tm = xa_ref.shape[0]
    g_half = 0.5 * gpost_ref[...]

    @pl.when(i == 0)
    def _():
        _stage_bf16(win_hbm.at[layer], win_ref, stage_in, sem)
        _stage_bf16(wout_hbm.at[layer], wout_ref, stage_out, sem)
        xn_ref[0] = _rms(xa_ref[...], gpre_ref[...]).astype(BF16)
        y_ref[...] = jnp.zeros(y_ref.shape, F32)

    @pl.when((i >= 1) & (i <= n_tiles))
    def _():
        xn = xn_ref[(i - 1) % 2]
        rp = tm // FF_SIDE_CHUNKS
        for j in range(D_FF // FF_TILE):
            lo = j * FF_TILE
            gate = jnp.dot(xn, win_ref[:, lo:lo + FF_TILE], preferred_element_type=F32)
            up = jnp.dot(xn, win_ref[:, D_FF + lo:D_FF + lo + FF_TILE], preferred_element_type=F32)
            hm = gate * jax.nn.sigmoid(gate) * up
            h_ref[:, lo:lo + FF_TILE] = hm.astype(BF16)
            if j < FF_SIDE_CHUNKS:
                rs = slice(j * rp, (j + 1) * rp)
                out_rows = xc_ref[rs, :] + _rms(y_ref[rs, :], g_half)
                o_ref[rs, :] = out_rows
                xn_rows = _rms(xa_ref[rs, :], gpre_ref[...])
                xn_ref[i % 2, rs, :] = xn_rows.astype(BF16)
                zero = _zero_after(out_rows) + _zero_after(xn_rows)
                h_ref[0:8, lo:lo + 128] = (hm[0:8, 0:128] + zero).astype(BF16)
            else:
                if j == FF_SIDE_CHUNKS:
                    side = _convert_later_weights(i, w2in_blk, w2out_blk, wmo_blk, wt_blk, wtg_blk,
                                                  w2in_bf, w2out_bf, wmo_bf, wb_blk)
                    side = [side[0], side[1] + side[2], side[3]]
                h_ref[0:8, lo:lo + 128] = (hm[0:8, 0:128] + side[j - FF_SIDE_CHUNKS]).astype(BF16)
        y_ref[...] = jnp.dot(h_ref[...], wout_ref[...], preferred_element_type=F32)

    @pl.when(i == n_tiles + 1)
    def _():
        o_ref[...] = xc_ref[...] + _rms(y_ref[...], g_half)


def _mix_ffn_kernel(tiles_per_seq, x_ref, bg_ref, u_ref, uprev_ref, unext_ref, cw_ref, cb_ref,
                    y_ref, gmix_ref, wmo_ref, gpre_ref, gpost_ref, win_ref, wout_ref, o_ref, h_ref):
    x, xn = _mix_out_tile(tiles_per_seq, x_ref, bg_ref, u_ref, uprev_ref, unext_ref, cw_ref, cb_ref,
                          y_ref, wmo_ref, gmix_ref, gpre_ref)
    o_ref[...] = _ffn_tile(x, xn, gpost_ref, win_ref, wout_ref, h_ref)


def _ffn_scratch(tm):
    return [
        pltpu.VMEM((tm, D_FF), BF16),
        pltpu.VMEM((D_MODEL, 2 * D_FF), BF16),
        pltpu.VMEM((D_FF, D_MODEL), BF16),
    ], [
        pltpu.VMEM((2, D_MODEL // FF_STAGE_CHUNKS, 2 * D_FF), F32),
        pltpu.VMEM((2, D_FF // FF_STAGE_CHUNKS, D_MODEL), F32),
        pltpu.SemaphoreType.DMA((2,)),
    ]


def _ffn(x, gpre, gpost, w_in, w_out, w2_in, w2_out, w_mix_out, w_mix_in_t, layer, tm):
    T = x.shape[0]
    nt = T // tm
    assert nt > MIX_W_BLOCKS and nt % 2 == 0, f"first FFN needs more than {MIX_W_BLOCKS} token tiles, got {nt}"
    g0 = MIX_W_BLOCKS * 128
    resident, staging = _ffn_scratch(tm)
    tile = lambda lag: pl.BlockSpec((tm, D_MODEL), lambda i: (jnp.clip(i - lag, 0, nt - 1), 0))
    rows_in = lambda n, w, last: pl.BlockSpec((None, n, w), lambda i: (layer, jnp.clip(i - 1, 0, last), 0))
    rows_out = lambda n, w, last: pl.BlockSpec((n, w), lambda i: (jnp.clip(i - 1, 0, last), 0))
    r_in, r_out, r_mo = D_MODEL // nt, D_FF // (nt // 2), D_MODEL // (nt // 2)
    return pl.pallas_call(
        functools.partial(_ffn_kernel, layer, nt),
        grid=(nt + 2,),
        in_specs=[
            tile(0), tile(2),
            _resident((1, D_MODEL)),
            _resident((1, D_MODEL)),
            pl.BlockSpec(memory_space=pl.ANY),
            pl.BlockSpec(memory_space=pl.ANY),
            rows_in(r_in, 2 * D_FF, nt - 1),
            rows_in(r_out, D_MODEL, nt // 2 - 1),
            rows_in(r_mo, D_MODEL, nt // 2 - 1),
            rows_in(128, D_MODEL, MIX_W_BLOCKS - 1),
            pl.BlockSpec((None, 2 * NGR, D_MODEL), lambda i: (layer, g0 // (2 * NGR), 0)),
        ],
        out_specs=[
            tile(2),
            rows_out(r_in, 2 * D_FF, nt - 1),
            rows_out(r_out, D_MODEL, nt // 2 - 1),
            rows_out(r_mo, D_MODEL, nt // 2 - 1),
            pl.BlockSpec((D_MODEL, 128), lambda i: (0, jnp.clip(i - 1, 0, MIX_W_BLOCKS))),
        ],
        out_shape=[
            jax.ShapeDtypeStruct((T, D_MODEL), F32),
            jax.ShapeDtypeStruct((D_MODEL, 2 * D_FF), BF16),
            jax.ShapeDtypeStruct((D_FF, D_MODEL), BF16),
            jax.ShapeDtypeStruct((D_MODEL, D_MODEL), BF16),
            jax.ShapeDtypeStruct((D_MODEL, g0 + 128), BF16),
        ],
        scratch_shapes=resident + staging + [pltpu.VMEM((2, tm, D_MODEL), BF16),
                                             pltpu.VMEM((tm, D_MODEL), F32)],
        compiler_params=pltpu.CompilerParams(
            dimension_semantics=("arbitrary",), vmem_limit_bytes=VMEM_LIMIT),
        name="ffn",
    )(x, x, gpre, gpost, w_in, w_out, w2_in, w2_out, w_mix_out, w_mix_in_t, w_mix_in_t)


def _mix_ffn(x, bg, u, conv_w, conv_b, y_mlstm, gmix, w_mix_out, gpre, gpost, w_in, w_out, seq, tm):
    T = x.shape[0]
    tiles_per_seq = seq // tm
    sub = tm // HALO_ROWS
    last = T // HALO_ROWS - 1
    row = lambda w: pl.BlockSpec((tm, w), lambda i: (i, 0))
    return pl.pallas_call(
        functools.partial(_mix_ffn_kernel, tiles_per_seq),
        grid=(T // tm,),
        in_specs=[
            row(D_MODEL), row(CONV_WIDTH), row(CONV_WIDTH),
            pl.BlockSpec((HALO_ROWS, CONV_WIDTH), lambda i: (jnp.maximum(i * sub - 1, 0), 0)),
            pl.BlockSpec((HALO_ROWS, CONV_WIDTH), lambda i: (jnp.minimum((i + 1) * sub, last), 0)),
            _resident((3, CONV_WIDTH)), _resident((1, CONV_WIDTH)),
            row(HEADS * DV), _resident((1, D_MODEL)), _resident(w_mix_out.shape),
            _resident((1, D_MODEL)), _resident((1, D_MODEL)), _resident(w_in.shape), _resident(w_out.shape),
        ],
        out_specs=row(D_MODEL),
        out_shape=jax.ShapeDtypeStruct((T, D_MODEL), F32),
        scratch_shapes=[pltpu.VMEM((tm, D_FF), BF16)],
        compiler_params=pltpu.CompilerParams(
            dimension_semantics=("arbitrary",), vmem_limit_bytes=VMEM_LIMIT),
        name="mix_ffn",
    )(x, bg, u, u, u, conv_w, conv_b, y_mlstm, gmix, w_mix_out, gpre, gpost, w_in, w_out)


def _gate_rows(zr, p_ref, ccol_ref):
    L = CHUNK
    n_chunks = zr.shape[1] // L
    fwd_row = lax.broadcasted_iota(jnp.int32, (NGR, L), 0) < HEADS
    fwd_col = lax.broadcasted_iota(jnp.int32, (NGR, 1), 0) < HEADS
    li = zr[0:NGR]
    gates = jnp.concatenate([li, _log_sigmoid(zr[NGR:2 * NGR])], axis=0)
    x3 = jnp.concatenate(_split3(gates), axis=0)
    stacked = jnp.concatenate([x3[:, c * L:(c + 1) * L] for c in range(n_chunks)], axis=0)
    cum_ops = jnp.concatenate([_visible(False), _visible(True)], axis=1).astype(BF16)
    cum = jnp.dot(stacked, cum_ops, preferred_element_type=F32)
    b_chunks = []
    for c in range(n_chunks):
        blk = cum[c * 48:(c + 1) * 48]
        s16 = blk[0:16] + blk[16:32] + blk[32:48]
        b_chunks.append(jnp.where(fwd_row, s16[NGR:2 * NGR, 0:L], s16[NGR:2 * NGR, L:2 * L]))
    b = jnp.concatenate(b_chunks, axis=1)
    cc = li - b
    fwd_all = lax.broadcasted_iota(jnp.int32, cc.shape, 0) < HEADS
    p_ref[P_M, :] = jnp.where(fwd_all, _running_max(cc, False), _running_max(cc, True))
    p_ref[P_B, :] = b
    pad = jnp.zeros((L - NGR, L), F32)
    for c in range(n_chunks):
        sl = slice(c * L, (c + 1) * L)
        b_c, cc_c = b_chunks[c], cc[:, sl]
        g = jnp.where(fwd_col, b_c[:, L - 1:L], b_c[:, 0:1])
        m_chunk = g + jnp.max(cc_c, axis=1, keepdims=True)
        p_ref[P_E, sl] = jnp.exp(g + cc_c - m_chunk)
        p_ref[P_G, sl] = jnp.broadcast_to(g, (NGR, L))
        p_ref[P_MC, sl] = jnp.broadcast_to(m_chunk, (NGR, L))
        ccol_ref[sl, :] = jnp.concatenate([cc_c, pad], axis=0).T


def _mix_in_kernel(x_ref, gpre_ref, wb_ref, gbias_ref,
                   bg_ref, u_ref, qt_ref, k_ref, ksw_ref, vt_ref, o_ref, p_ref, ccol_ref):
    W, HK, HV = CONV_WIDTH, HEADS * DK, HEADS * DV
    q0 = 3 * W
    k0, v0 = q0 + HK, q0 + 2 * HK
    o0 = v0 + HV
    g0 = o0 + HV

    xn = _rms(x_ref[...], gpre_ref[...]).astype(BF16)
    proj = lambda a, b: jnp.dot(xn, wb_ref[:, a:b], preferred_element_type=F32)

    zg = proj(g0, g0 + 128) + gbias_ref[...]
    bg_ref[...] = proj(0, W)
    u_ref[...] = proj(W, 2 * W) * proj(2 * W, 3 * W)
    _gate_rows(zg.T[0:2 * NGR], p_ref, ccol_ref)
    qt_ref[...] = (proj(q0, k0) * (DK ** -0.5)).T.astype(BF16)
    kk = proj(k0, v0)
    k_ref[...] = kk.astype(BF16)
    for pair in range(HEADS // 2):
        ps = slice(pair * 2 * DK, (pair + 1) * 2 * DK)
        ksw_ref[:, ps] = pltpu.roll(kk[:, ps], DK, 1).astype(BF16)
    vt_ref[...] = proj(v0, o0).T.astype(BF16)
    o_ref[...] = proj(o0, g0)


def _mix_in(x, gpre, wb, gbias, tm):
    T = x.shape[0]
    row = lambda w: pl.BlockSpec((tm, w), lambda i: (i, 0))
    col = lambda h: pl.BlockSpec((h, tm), lambda i: (0, i))
    HK, HV = HEADS * DK, HEADS * DV
    return pl.pallas_call(
        _mix_in_kernel,
        grid=(T // tm,),
        in_specs=[row(D_MODEL), _resident((1, D_MODEL)), _resident(wb.shape), _resident((1, 128))],
        out_specs=[row(CONV_WIDTH), row(CONV_WIDTH), col(HK), row(HK), row(HK), col(HV), row(HV),
                   col(5 * NGR), row(128)],
        out_shape=[
            jax.ShapeDtypeStruct((T, CONV_WIDTH), F32),
            jax.ShapeDtypeStruct((T, CONV_WIDTH), F32),
            jax.ShapeDtypeStruct((HK, T), BF16),
            jax.ShapeDtypeStruct((T, HK), BF16),
            jax.ShapeDtypeStruct((T, HK), BF16),
            jax.ShapeDtypeStruct((HV, T), BF16),
            jax.ShapeDtypeStruct((T, HV), F32),
            jax.ShapeDtypeStruct((5 * NGR, T), F32),
            jax.ShapeDtypeStruct((T, 128), F32),
        ],
        compiler_params=pltpu.CompilerParams(
            dimension_semantics=("arbitrary",), vmem_limit_bytes=VMEM_LIMIT),
        name="mix_in",
    )(x, gpre, wb, gbias)


def _state_half(d, h):
    return (h % 2) ^ d


def _scan_direction(d, vt_ref, k_ref, p_ref, ct_ref, mp_ref, ct_state, m_state):
    L = CHUNK
    n_chunks = k_ref.shape[0] // L
    lane_half = lax.broadcasted_iota(jnp.int32, (L, 128), 1) // DK
    lane_half_s = lax.broadcasted_iota(jnp.int32, (DVA, 128), 1) // DK
    ones_rows = jnp.ones((DVA - DV, L), BF16)

    cts = [ct_state[d, h] for h in range(HEADS)]
    m_prev = m_state[d]
    for chunk in (range(n_chunks - 1, -1, -1) if d == 1 else range(n_chunks)):
        sl = slice(chunk * L, (chunk + 1) * L)
        e, g, m_chunk = p_ref[P_E, sl], p_ref[P_G, sl], p_ref[P_MC, sl]
        mp_ref[:, sl] = m_prev
        m_new = jnp.maximum(g + m_prev, m_chunk)
        a_old = jnp.exp(g + m_prev - m_new)
        a_new = jnp.exp(m_chunk - m_new)
        new_cts = []
        for h in range(HEADS):
            r = HEADS * d + h
            vt_aug = jnp.concatenate([vt_ref[h * DV:(h + 1) * DV, sl], ones_rows], axis=0)
            vte = (vt_aug.astype(F32) * e[r:r + 1, :]).astype(BF16)
            pair = slice((h // 2) * 2 * DK, (h // 2 + 1) * 2 * DK)
            k_half = jnp.where(lane_half == _state_half(d, h), k_ref[sl, pair], jnp.zeros((L, 128), BF16))
            ct_chunk = jnp.dot(vte, k_half, preferred_element_type=F32)
            new_cts.append(a_old[r:r + 1, 0:1] * cts[h] + a_new[r:r + 1, 0:1] * ct_chunk)
        for pr in range(HEADS // 2):
            both = jnp.where(lane_half_s == _state_half(d, 2 * pr), cts[2 * pr], cts[2 * pr + 1])
            ct_ref[chunk, pr * DVA:(pr + 1) * DVA, :] = both.astype(BF16)
        cts = new_cts
        m_prev = m_new
    for h in range(HEADS):
        ct_state[d, h] = cts[h]
    m_state[d] = m_prev


def _mlstm_scan_kernel(vt_f, k_f, p_f, vt_b, ksw_b, p_b, ct_f, mp_f, ct_b, mp_b, ct_state, m_state):
    @pl.when(pl.program_id(1) == 0)
    def _():
        ct_state[...] = jnp.zeros(ct_state.shape, F32)
        m_state[...] = jnp.full(m_state.shape, NEG_INF, F32)

    _scan_direction(0, vt_f, k_f, p_f, ct_f, mp_f, ct_state, m_state)
    _scan_direction(1, vt_b, ksw_b, p_b, ct_b, mp_b, ct_state, m_state)


def _mlstm_out_kernel(qt_ref, k_ref, vt_ref, p_ref, mp_f, mp_b, ccol_ref, ct_f, ct_b,
                      o_ref, gain_ref, y_ref):
    L = CHUNK
    n_chunks = k_ref.shape[0] // L
    visible = (_visible(False), _visible(True))
    fwd_row = lax.broadcasted_iota(jnp.int32, (NGR, L), 0) < HEADS
    lane_half = lax.broadcasted_iota(jnp.int32, (DVA, 128), 1) // DK
    ones_rows = jnp.ones((DVA - DV, L), BF16)
    zq = jnp.zeros((DK, L), BF16)
    for chunk in range(n_chunks):
        sl = slice(chunk * L, (chunk + 1) * L)
        m_prev = jnp.where(fwd_row, mp_f[:, sl], mp_b[:, sl])
        n_t = jnp.maximum(m_prev, p_ref[P_M, sl])
        f_inter = jnp.exp(m_prev - n_t)
        e_min = jnp.exp(-(p_ref[P_B, sl] + n_t))
        ccol = ccol_ref[sl, :]
        for h in range(HEADS):
            hs = slice(h * DV, (h + 1) * DV)
            qt = qt_ref[h * DK:(h + 1) * DK, sl]
            k_pair = k_ref[sl, (h // 2) * 2 * DK:(h // 2 + 1) * 2 * DK]
            qt_pair = jnp.concatenate([qt, zq] if h % 2 == 0 else [zq, qt], axis=0)
            st = jnp.dot(k_pair, qt_pair, preferred_element_type=F32)
            pts, qfs = [], []
            for d in range(2):
                r = HEADS * d + h
                arg = jnp.where(visible[d], ccol[:, r:r + 1] - n_t[r:r + 1, :], NEG_INF)
                pts.append((jnp.exp(arg) * st).astype(BF16))
                qfs.append((qt.astype(F32) * f_inter[r:r + 1, :]).astype(BF16))
            inter = [jnp.concatenate([qfs[0], zq], axis=1), jnp.concatenate([zq, qfs[1]], axis=1)]
            first = 0 if _state_half(0, h) == 0 else 1
            rhs = jnp.concatenate([jnp.concatenate(pts, axis=1), inter[first], inter[1 - first]], axis=0)
            vt_aug = jnp.concatenate([vt_ref[hs, sl], ones_rows], axis=0)
            ps = slice((h // 2) * DVA, (h // 2 + 1) * DVA)
            blocks = (ct_f[chunk, ps, :], ct_b[chunk, ps, :])
            ct = jnp.where(lane_half == 0, blocks[first], blocks[1 - first])
            both = jnp.dot(jnp.concatenate([vt_aug, ct], axis=1), rhs, preferred_element_type=F32)
            ht = None
            for d in range(2):
                r = HEADS * d + h
                numer = both[0:DV, d * L:(d + 1) * L]
                denom = both[DV:DV + 1, d * L:(d + 1) * L]
                part = numer / jnp.maximum(jnp.abs(denom), e_min[r:r + 1, :])
                ht = part if ht is None else ht + part
            ms = jnp.mean(ht * ht, axis=0, keepdims=True)
            hn = ht * lax.rsqrt(ms + EPS)
            y_ref[sl, hs] = (jax.nn.sigmoid(o_ref[sl, hs]) * gain_ref[:, hs] * hn.T).astype(BF16)


def _mlstm(qt, k, ksw, vt, p, ccol, o, gain_b, batch, seq, scan_rows, rows):
    T = batch * seq
    ng = seq // scan_rows
    HK, HV = HEADS * DK, HEADS * DV

    def scan_specs(group_of):
        blk = lambda b, j: b * ng + group_of(j)
        ins = [pl.BlockSpec((HV, scan_rows), lambda b, j: (0, blk(b, j))),
               pl.BlockSpec((scan_rows, HK), lambda b, j: (blk(b, j), 0)),
               pl.BlockSpec((5 * NGR, scan_rows), lambda b, j: (0, blk(b, j)))]
        outs = [pl.BlockSpec((scan_rows // CHUNK, CT_ROWS, 128), lambda b, j: (blk(b, j), 0, 0)),
                pl.BlockSpec((NGR, scan_rows), lambda b, j: (0, blk(b, j)))]
        return ins, outs

    ins_f, outs_f = scan_specs(lambda j: j)
    ins_b, outs_b = scan_specs(lambda j: ng - 1 - j)
    state_shapes = [jax.ShapeDtypeStruct((T // CHUNK, CT_ROWS, 128), BF16),
                    jax.ShapeDtypeStruct((NGR, T), F32)]
    if ng == 1:
        scan_kernel = lambda vt_r, k_r, ksw_r, p_r, *rest: _mlstm_scan_kernel(vt_r, k_r, p_r, vt_r, ksw_r, p_r, *rest)
        scan_in_specs, scan_args = [ins_f[0], ins_f[1], ins_f[1], ins_f[2]], (vt, k, ksw, p)
    else:
        scan_kernel, scan_in_specs, scan_args = _mlstm_scan_kernel, ins_f + ins_b, (vt, k, p, vt, ksw, p)
    ct_f, mp_f, ct_b, mp_b = pl.pallas_call(
        scan_kernel,
        grid=(batch, ng),
        in_specs=scan_in_specs,
        out_specs=outs_f + outs_b,
        out_shape=state_shapes + state_shapes,
        scratch_shapes=[pltpu.VMEM((2, HEADS, DVA, 128), F32), pltpu.VMEM((2, NGR, 128), F32)],
        compiler_params=pltpu.CompilerParams(dimension_semantics=("arbitrary", "arbitrary"),
                                             vmem_limit_bytes=VMEM_LIMIT),
        name="mlstm_scan",
    )(*scan_args)

    row = lambda w: pl.BlockSpec((rows, w), lambda i: (i, 0))
    col = lambda h: pl.BlockSpec((h, rows), lambda i: (0, i))
    ctb = pl.BlockSpec((rows // CHUNK, CT_ROWS, 128), lambda i: (i, 0, 0))
    return pl.pallas_call(
        _mlstm_out_kernel,
        grid=(T // rows,),
        in_specs=[col(HK), row(HK), col(HV), col(5 * NGR), col(NGR), col(NGR), row(128), ctb, ctb,
                  row(HV), _resident((1, HV))],
        out_specs=row(HV),
        out_shape=jax.ShapeDtypeStruct((T, HV), BF16),
        compiler_params=pltpu.CompilerParams(dimension_semantics=("arbitrary",),
                                             vmem_limit_bytes=VMEM_LIMIT),
        name="mlstm_out",
    )(qt, k, vt, p, mp_f, mp_b, ccol, ct_f, ct_b, o, gain_b)


def kernel(x, norm_ffn1_pre, norm_ffn1_post, w_ffn1_in, w_ffn1_out, norm_mix_pre, norm_mix_post,
           w_mix_in, conv_w, conv_b, gate_i_bias, gate_f_bias, mlstm_norm, w_mix_out,
           norm_ffn2_pre, norm_ffn2_post, w_ffn2_in, w_ffn2_out):
    batch, seq, _ = x.shape
    T = batch * seq
    depth = norm_ffn1_pre.shape[0]
    tm = 512
    xt = x.reshape(T, D_MODEL)
    HV = HEADS * DV
    for l in range(depth):
        xt, w2_in, w2_out, w_mo, w_mi = _ffn(
            xt, norm_ffn1_pre[l][None], norm_ffn1_post[l][None], w_ffn1_in, w_ffn1_out,
            w_ffn2_in, w_ffn2_out, w_mix_out, jnp.swapaxes(w_mix_in, 1, 2), l, tm)

        gbias = jnp.pad(jnp.concatenate([gate_i_bias[l], gate_f_bias[l]]), (0, 128 - 2 * NGR))[None]
        bg, u, qt, k, ksw, vt, o, p, ccol = _mix_in(xt, norm_mix_pre[l][None], w_mi, gbias, 2 * tm)
        gain_b = mlstm_norm[l][None]
        y_mlstm = _mlstm(qt, k, ksw, vt, p, ccol, o, gain_b, batch, seq, SCAN_ROWS, MLSTM_ROWS)
        xt = _mix_ffn(xt, bg, u, conv_w[l], conv_b[l][None], y_mlstm, norm_mix_post[l][None], w_mo,
                      norm_ffn2_pre[l][None], norm_ffn2_post[l][None], w2_in, w2_out, seq, tm)
    return xt.reshape(batch, seq, D_MODEL)
```

```python
import functools

import jax
import jax.numpy as jnp
from jax import lax
from jax.experimental import pallas as pl
from jax.experimental.pallas import tpu as pltpu

D_MODEL = 1024
D_FF = 2816
CONV_WIDTH = 512
HEADS = 4
DK = 64
DV = 128
CHUNK = 128
EPS = 1e-6
NEG_INF = -1e30

FF_TILE = 256
FF_STAGE_CHUNKS = 8
FF_SIDE_CHUNKS = 8
MIX_OUT_PIECES = 4
FF_LEAD_TILES = 2
MIX_W_BLOCKS = (3 * CONV_WIDTH + 2 * HEADS * DK + 2 * HEADS * DV) // 128
MLSTM_ROWS = 2048
SCAN_ROWS = 4096
DVA = DV + 16
HALO_ROWS = 8
NGR = 2 * HEADS
CT_ROWS = (HEADS // 2) * DVA
VMEM_LIMIT = 56 * 1024 * 1024

P_E, P_M, P_B, P_G, P_MC = (slice(i * NGR, (i + 1) * NGR) for i in range(5))

F32 = jnp.float32
BF16 = jnp.bfloat16


def _rms(x, g):
    return x * lax.rsqrt(jnp.mean(x * x, axis=-1, keepdims=True) + EPS) * g


def _log_sigmoid(z):
    return jnp.minimum(z, 0.0) - jnp.log1p(jnp.exp(-jnp.abs(z)))


def _resident(shape):
    zeros = (0,) * len(shape)
    return pl.BlockSpec(shape, lambda *_: zeros, pipeline_mode=pl.Buffered(1))


def _split3(x):
    hi = x.astype(BF16)
    r1 = x - hi.astype(F32)
    mid = r1.astype(BF16)
    lo = (r1 - mid.astype(F32)).astype(BF16)
    return hi, mid, lo


def _visible(rev):
    s = lax.broadcasted_iota(jnp.int32, (CHUNK, CHUNK), 0)
    t = lax.broadcasted_iota(jnp.int32, (CHUNK, CHUNK), 1)
    return (s >= t) if rev else (s <= t)


def _running_max(x, rev):
    n = x.shape[1]
    pos = lax.broadcasted_iota(jnp.int32, x.shape, 1) & (CHUNK - 1)
    k = 1
    while k < CHUNK:
        if rev:
            shifted, ok = pltpu.roll(x, n - k, 1), pos < CHUNK - k
        else:
            shifted, ok = pltpu.roll(x, k, 1), pos >= k
        x = jnp.maximum(x, jnp.where(ok, shifted, NEG_INF))
        k *= 2
    return x


def _stage_bf16(src_hbm, dst_ref, stage_ref, sem, rows=None):
    rows = stage_ref.shape[1] if rows is None else rows
    n_chunks = src_hbm.shape[0] // rows

    def copy(c):
        return pltpu.make_async_copy(src_hbm.at[pl.ds(c * rows, rows), :],
                                     stage_ref.at[c % 2, pl.ds(0, rows), :], sem.at[c % 2])

    copy(0).start()
    for c in range(n_chunks):
        if c + 1 < n_chunks:
            copy(c + 1).start()
        copy(c).wait()
        dst_ref[c * rows:(c + 1) * rows, :] = stage_ref[c % 2, 0:rows, :].astype(BF16)


def _mix_out_tile(tiles_per_seq, x_ref, bg_ref, u_ref, uprev_ref, unext_ref, cw_ref, cb_ref,
                  y_ref, wmo_ref, gmix_ref, gpre_ref):
    i = pl.program_id(0)
    tm = u_ref.shape[0]
    u = u_ref[...]
    has_prev = (i % tiles_per_seq != 0).astype(F32)
    has_next = (i % tiles_per_seq != tiles_per_seq - 1).astype(F32)
    prev_row = uprev_ref[HALO_ROWS - 1:HALO_ROWS, :] * has_prev
    next_row = unext_ref[0:1, :] * has_next
    ri = lax.broadcasted_iota(jnp.int32, u.shape, 0)
    u_m1 = jnp.where(ri == 0, prev_row, pltpu.roll(u, 1, 0))
    u_p1 = jnp.where(ri == tm - 1, next_row, pltpu.roll(u, tm - 1, 0))
    conv = cw_ref[0:1, :] * u_m1 + cw_ref[1:2, :] * u + cw_ref[2:3, :] * u_p1
    y_conv = (bg_ref[...] * (conv + cb_ref[...])).astype(BF16)
    rp = tm // MIX_OUT_PIECES
    xs, xns = [], []
    for r in range(MIX_OUT_PIECES):
        rs = slice(r * rp, (r + 1) * rp)
        h = jnp.dot(y_ref[rs, :], wmo_ref[CONV_WIDTH:, :], preferred_element_type=F32) \
            + jnp.dot(y_conv[rs, :], wmo_ref[0:CONV_WIDTH, :], preferred_element_type=F32)
        xs.append(x_ref[rs, :] + _rms(h, gmix_ref[...]))
        xns.append(_rms(xs[-1], gpre_ref[...]).astype(BF16))
    return jnp.concatenate(xs, axis=0), xns


def _ffn_tile(x, xn_pieces, gpost_ref, win_ref, wout_ref, h_ref):
    rp = xn_pieces[0].shape[0]
    xn = jnp.concatenate(xn_pieces, axis=0)
    for j in range(D_FF // FF_TILE):
        lo = j * FF_TILE
        lhs = list(enumerate(xn_pieces)) if j < FF_LEAD_TILES and len(xn_pieces) > 1 else [(None, xn)]
        for r, xr in lhs:
            rs = slice(None) if r is None else slice(r * rp, (r + 1) * rp)
            gate = jnp.dot(xr, win_ref[:, lo:lo + FF_TILE], preferred_element_type=F32)
            up = jnp.dot(xr, win_ref[:, D_FF + lo:D_FF + lo + FF_TILE], preferred_element_type=F32)
            h_ref[rs, lo:lo + FF_TILE] = (gate * jax.nn.sigmoid(gate) * up).astype(BF16)
    y = jnp.dot(h_ref[...], wout_ref[...], preferred_element_type=F32)
    return x + _rms(y, 0.5 * gpost_ref[...])


def _zero_after(v):
    u = pltpu.bitcast(v, jnp.uint32)
    acc = None
    for r in range(u.shape[0] // 8):
        for c in range(u.shape[1] // 128):
            t = u[r * 8:(r + 1) * 8, c * 128:(c + 1) * 128]
            acc = t if acc is None else acc | t
    return pltpu.bitcast((acc >> 16) >> 16, F32)


def _convert_later_weights(i, w2in_blk, w2out_blk, wmo_blk, wt_blk, wtg_blk, w2in_bf, w2out_bf, wmo_bf, wb_blk):
    zeros = []
    for src, dst in ((w2in_blk, w2in_bf), (w2out_blk, w2out_bf), (wmo_blk, wmo_bf)):
        v = src[...].astype(BF16)
        dst[...] = v
        zeros.append(_zero_after(v[0:16, 0:128]))
    gates = jnp.concatenate([wtg_blk[...], jnp.zeros((128 - 2 * NGR, D_MODEL), F32)], axis=0)
    last = jnp.clip(i - 1, 0, MIX_W_BLOCKS) == MIX_W_BLOCKS
    blk = jnp.where(last, gates, wt_blk[...]).T
    wb_blk[...] = blk.astype(BF16)
    zeros.append(_zero_after(blk))
    return zeros


def _ffn_kernel(layer, n_tiles, xa_ref, xc_ref, gpre_ref, gpost_ref, win_hbm, wout_hbm,
                w2in_blk, w2out_blk, wmo_blk, wt_blk, wtg_blk,
                o_ref, w2in_bf, w2out_bf, wmo_bf, wb_blk,
                h_ref, win_ref, wout_ref, stage_in, stage_out, sem, xn_ref, y_ref):
    i = pl.program_id(0)
    tm = xa_ref.shape[0]
    g_half = 0.5 * gpost_ref[...]

    @pl.when(i == 0)
    def _():
        _stage_bf16(win_hbm.at[layer], win_ref, stage_in, sem)
        _stage_bf16(wout_hbm.at[layer], wout_ref, stage_out, sem)
        xn_ref[0] = _rms(xa_ref[...], gpre_ref[...]).astype(BF16)
        y_ref[...] = jnp.zeros(y_ref.shape, F32)

    @pl.when((i >= 1) & (i <= n_tiles))
    def _():
        xn = xn_ref[(i - 1) % 2]
        rp = tm // FF_SIDE_CHUNKS
        for j in range(D_FF // FF_TILE):
            lo = j * FF_TILE
            gate = jnp.dot(xn, win_ref[:, lo:lo + FF_TILE], preferred_element_type=F32)
            up = jnp.dot(xn, win_ref[:, D_FF + lo:D_FF + lo + FF_TILE], preferred_element_type=F32)
            hm = gate * jax.nn.sigmoid(gate) * up
            h_ref[:, lo:lo + FF_TILE] = hm.astype(BF16)
            if j < FF_SIDE_CHUNKS:
                rs = slice(j * rp, (j + 1) * rp)
                out_rows = xc_ref[rs, :] + _rms(y_ref[rs, :], g_half)
                o_ref[rs, :] = out_rows
                xn_rows = _rms(xa_ref[rs, :], gpre_ref[...])
                xn_ref[i % 2, rs, :] = xn_rows.astype(BF16)
                zero = _zero_after(out_rows) + _zero_after(xn_rows)
                h_ref[0:8, lo:lo + 128] = (hm[0:8, 0:128] + zero).astype(BF16)
            else:
                if j == FF_SIDE_CHUNKS:
                    side = _convert_later_weights(i, w2in_blk, w2out_blk, wmo_blk, wt_blk, wtg_blk,
                                                  w2in_bf, w2out_bf, wmo_bf, wb_blk)
                    side = [side[0], side[1] + side[2], side[3]]
                h_ref[0:8, lo:lo + 128] = (hm[0:8, 0:128] + side[j - FF_SIDE_CHUNKS]).astype(BF16)
        y_ref[...] = jnp.dot(h_ref[...], wout_ref[...], preferred_element_type=F32)

    @pl.when(i == n_tiles + 1)
    def _():
        o_ref[...] = xc_ref[...] + _rms(y_ref[...], g_half)


def _mix_ffn_kernel(tiles_per_seq, x_ref, bg_ref, u_ref, uprev_ref, unext_ref, cw_ref, cb_ref,
                    y_ref, gmix_ref, wmo_ref, gpre_ref, gpost_ref, win_ref, wout_ref, o_ref, h_ref):
    x, xn = _mix_out_tile(tiles_per_seq, x_ref, bg_ref, u_ref, uprev_ref, unext_ref, cw_ref, cb_ref,
                          y_ref, wmo_ref, gmix_ref, gpre_ref)
    o_ref[...] = _ffn_tile(x, xn, gpost_ref, win_ref, wout_ref, h_ref)


def _ffn_scratch(tm):
    return [
        pltpu.VMEM((tm, D_FF), BF16),
        pltpu.VMEM((D_MODEL, 2 * D_FF), BF16),
        pltpu.VMEM((D_FF, D_MODEL), BF16),
    ], [
        pltpu.VMEM((2, D_MODEL // FF_STAGE_CHUNKS, 2 * D_FF), F32),
        pltpu.VMEM((2, D_FF // FF_STAGE_CHUNKS, D_MODEL), F32),
        pltpu.SemaphoreType.DMA((2,)),
    ]


def _ffn(x, gpre, gpost, w_in, w_out, w2_in, w2_out, w_mix_out, w_mix_in_t, layer, tm):
    T = x.shape[0]
    nt = T // tm
    assert nt > MIX_W_BLOCKS and nt % 2 == 0, f"first FFN needs more than {MIX_W_BLOCKS} token tiles, got {nt}"
    g0 = MIX_W_BLOCKS * 128
    resident, staging = _ffn_scratch(tm)
    tile = lambda lag: pl.BlockSpec((tm, D_MODEL), lambda i: (jnp.clip(i - lag, 0, nt - 1), 0))
    rows_in = lambda n, w, last: pl.BlockSpec((None, n, w), lambda i: (layer, jnp.clip(i - 1, 0, last), 0))
    rows_out = lambda n, w, last: pl.BlockSpec((n, w), lambda i: (jnp.clip(i - 1, 0, last), 0))
    r_in, r_out, r_mo = D_MODEL // nt, D_FF // (nt // 2), D_MODEL // (nt // 2)
    return pl.pallas_call(
        functools.partial(_ffn_kernel, layer, nt),
        grid=(nt + 2,),
        in_specs=[
            tile(0), tile(2),
            _resident((1, D_MODEL)),
            _resident((1, D_MODEL)),
            pl.BlockSpec(memory_space=pl.ANY),
            pl.BlockSpec(memory_space=pl.ANY),
            rows_in(r_in, 2 * D_FF, nt - 1),
            rows_in(r_out, D_MODEL, nt // 2 - 1),
            rows_in(r_mo, D_MODEL, nt // 2 - 1),
            rows_in(128, D_MODEL, MIX_W_BLOCKS - 1),
            pl.BlockSpec((None, 2 * NGR, D_MODEL), lambda i: (layer, g0 // (2 * NGR), 0)),
        ],
        out_specs=[
            tile(2),
            rows_out(r_in, 2 * D_FF, nt - 1),
            rows_out(r_out, D_MODEL, nt // 2 - 1),
            rows_out(r_mo, D_MODEL, nt // 2 - 1),
            pl.BlockSpec((D_MODEL, 128), lambda i: (0, jnp.clip(i - 1, 0, MIX_W_BLOCKS))),
        ],
        out_shape=[
            jax.ShapeDtypeStruct((T, D_MODEL), F32),
            jax.ShapeDtypeStruct((D_MODEL, 2 * D_FF), BF16),
            jax.ShapeDtypeStruct((D_FF, D_MODEL), BF16),
            jax.ShapeDtypeStruct((D_MODEL, D_MODEL), BF16),
            jax.ShapeDtypeStruct((D_MODEL, g0 + 128), BF16),
        ],
        scratch_shapes=resident + staging + [pltpu.VMEM((2, tm, D_MODEL), BF16),
                                             pltpu.VMEM((tm, D_MODEL), F32)],
        compiler_params=pltpu.CompilerParams(
            dimension_semantics=("arbitrary",), vmem_limit_bytes=VMEM_LIMIT),
        name="ffn",
    )(x, x, gpre, gpost, w_in, w_out, w2_in, w2_out, w_mix_out, w_mix_in_t, w_mix_in_t)


def _mix_ffn(x, bg, u, conv_w, conv_b, y_mlstm, gmix, w_mix_out, gpre, gpost, w_in, w_out, seq, tm):
    T = x.shape[0]
    tiles_per_seq = seq // tm
    sub = tm // HALO_ROWS
    last = T // HALO_ROWS - 1
    row = lambda w: pl.BlockSpec((tm, w), lambda i: (i, 0))
    return pl.pallas_call(
        functools.partial(_mix_ffn_kernel, tiles_per_seq),
        grid=(T // tm,),
        in_specs=[
            row(D_MODEL), row(CONV_WIDTH), row(CONV_WIDTH),
            pl.BlockSpec((HALO_ROWS, CONV_WIDTH), lambda i: (jnp.maximum(i * sub - 1, 0), 0)),
            pl.BlockSpec((HALO_ROWS, CONV_WIDTH), lambda i: (jnp.minimum((i + 1) * sub, last), 0)),
            _resident((3, CONV_WIDTH)), _resident((1, CONV_WIDTH)),
            row(HEADS * DV), _resident((1, D_MODEL)), _resident(w_mix_out.shape),
            _resident((1, D_MODEL)), _resident((1, D_MODEL)), _resident(w_in.shape), _resident(w_out.shape),
        ],
        out_specs=row(D_MODEL),
        out_shape=jax.ShapeDtypeStruct((T, D_MODEL), F32),
        scratch_shapes=[pltpu.VMEM((tm, D_FF), BF16)],
        compiler_params=pltpu.CompilerParams(
            dimension_semantics=("arbitrary",), vmem_limit_bytes=VMEM_LIMIT),
        name="mix_ffn",
    )(x, bg, u, u, u, conv_w, conv_b, y_mlstm, gmix, w_mix_out, gpre, gpost, w_in, w_out)


def _gate_rows(zr, p_ref, ccol_ref):
    L = CHUNK
    n_chunks = zr.shape[1] // L
    fwd_row = lax.broadcasted_iota(jnp.int32, (NGR, L), 0) < HEADS
    fwd_col = lax.broadcasted_iota(jnp.int32, (NGR, 1), 0) < HEADS
    li = zr[0:NGR]
    gates = jnp.concatenate([li, _log_sigmoid(zr[NGR:2 * NGR])], axis=0)
    x3 = jnp.concatenate(_split3(gates), axis=0)
    stacked = jnp.concatenate([x3[:, c * L:(c + 1) * L] for c in range(n_chunks)], axis=0)
    cum_ops = jnp.concatenate([_visible(False), _visible(True)], axis=1).astype(BF16)
    cum = jnp.dot(stacked, cum_ops, preferred_element_type=F32)
    b_chunks = []
    for c in range(n_chunks):
        blk = cum[c * 48:(c + 1) * 48]
        s16 = blk[0:16] + blk[16:32] + blk[32:48]
        b_chunks.append(jnp.where(fwd_row, s16[NGR:2 * NGR, 0:L], s16[NGR:2 * NGR, L:2 * L]))
    b = jnp.concatenate(b_chunks, axis=1)
    cc = li - b
    fwd_all = lax.broadcasted_iota(jnp.int32, cc.shape, 0) < HEADS
    p_ref[P_M, :] = jnp.where(fwd_all, _running_max(cc, False), _running_max(cc, True))
    p_ref[P_B, :] = b
    pad = jnp.zeros((L - NGR, L), F32)
    for c in range(n_chunks):
        sl = slice(c * L, (c + 1) * L)
        b_c, cc_c = b_chunks[c], cc[:, sl]
        g = jnp.where(fwd_col, b_c[:, L - 1:L], b_c[:, 0:1])
        m_chunk = g + jnp.max(cc_c, axis=1, keepdims=True)
        p_ref[P_E, sl] = jnp.exp(g + cc_c - m_chunk)
        p_ref[P_G, sl] = jnp.broadcast_to(g, (NGR, L))
        p_ref[P_MC, sl] = jnp.broadcast_to(m_chunk, (NGR, L))
        ccol_ref[sl, :] = jnp.concatenate([cc_c, pad], axis=0).T


def _mix_in_kernel(x_ref, gpre_ref, wb_ref, gbias_ref,
                   bg_ref, u_ref, qt_ref, k_ref, ksw_ref, vt_ref, o_ref, p_ref, ccol_ref):
    W, HK, HV = CONV_WIDTH, HEADS * DK, HEADS * DV
    q0 = 3 * W
    k0, v0 = q0 + HK, q0 + 2 * HK
    o0 = v0 + HV
    g0 = o0 + HV

    xn = _rms(x_ref[...], gpre_ref[...]).astype(BF16)
    proj = lambda a, b: jnp.dot(xn, wb_ref[:, a:b], preferred_element_type=F32)

    zg = proj(g0, g0 + 128) + gbias_ref[...]
    bg_ref[...] = proj(0, W)
    u_ref[...] = proj(W, 2 * W) * proj(2 * W, 3 * W)
    _gate_rows(zg.T[0:2 * NGR], p_ref, ccol_ref)
    qt_ref[...] = (proj(q0, k0) * (DK ** -0.5)).T.astype(BF16)
    kk = proj(k0, v0)
    k_ref[...] = kk.astype(BF16)
    for pair in range(HEADS // 2):
        ps = slice(pair * 2 * DK, (pair + 1) * 2 * DK)
        ksw_ref[:, ps] = pltpu.roll(kk[:, ps], DK, 1).astype(BF16)
    vt_ref[...] = proj(v0, o0).T.astype(BF16)
    o_ref[...] = proj(o0, g0)


def _mix_in(x, gpre, wb, gbias, tm):
    T = x.shape[0]
    row = lambda w: pl.BlockSpec((tm, w), lambda i: (i, 0))
    col = lambda h: pl.BlockSpec((h, tm), lambda i: (0, i))
    HK, HV = HEADS * DK, HEADS * DV
    return pl.pallas_call(
        _mix_in_kernel,
        grid=(T // tm,),
        in_specs=[row(D_MODEL), _resident((1, D_MODEL)), _resident(wb.shape), _resident((1, 128))],
        out_specs=[row(CONV_WIDTH), row(CONV_WIDTH), col(HK), row(HK), row(HK), col(HV), row(HV),
                   col(5 * NGR), row(128)],
        out_shape=[
            jax.ShapeDtypeStruct((T, CONV_WIDTH), F32),
            jax.ShapeDtypeStruct((T, CONV_WIDTH), F32),
            jax.ShapeDtypeStruct((HK, T), BF16),
            jax.ShapeDtypeStruct((T, HK), BF16),
            jax.ShapeDtypeStruct((T, HK), BF16),
            jax.ShapeDtypeStruct((HV, T), BF16),
            jax.ShapeDtypeStruct((T, HV), F32),
            jax.ShapeDtypeStruct((5 * NGR, T), F32),
            jax.ShapeDtypeStruct((T, 128), F32),
        ],
        compiler_params=pltpu.CompilerParams(
            dimension_semantics=("arbitrary",), vmem_limit_bytes=VMEM_LIMIT),
        name="mix_in",
    )(x, gpre, wb, gbias)


def _state_half(d, h):
    return (h % 2) ^ d


def _scan_direction(d, vt_ref, k_ref, p_ref, ct_ref, mp_ref, ct_state, m_state):
    L = CHUNK
    n_chunks = k_ref.shape[0] // L
    lane_half = lax.broadcasted_iota(jnp.int32, (L, 128), 1) // DK
    lane_half_s = lax.broadcasted_iota(jnp.int32, (DVA, 128), 1) // DK
    ones_rows = jnp.ones((DVA - DV, L), BF16)

    cts = [ct_state[d, h] for h in range(HEADS)]
    m_prev = m_state[d]
    for chunk in (range(n_chunks - 1, -1, -1) if d == 1 else range(n_chunks)):
        sl = slice(chunk * L, (chunk + 1) * L)
        e, g, m_chunk = p_ref[P_E, sl], p_ref[P_G, sl], p_ref[P_MC, sl]
        mp_ref[:, sl] = m_prev
        m_new = jnp.maximum(g + m_prev, m_chunk)
        a_old = jnp.exp(g + m_prev - m_new)
        a_new = jnp.exp(m_chunk - m_new)
        new_cts = []
        for h in range(HEADS):
            r = HEADS * d + h
            vt_aug = jnp.concatenate([vt_ref[h * DV:(h + 1) * DV, sl], ones_rows], axis=0)
            vte = (vt_aug.astype(F32) * e[r:r + 1, :]).astype(BF16)
            pair = slice((h // 2) * 2 * DK, (h // 2 + 1) * 2 * DK)
            k_half = jnp.where(lane_half == _state_half(d, h), k_ref[sl, pair], jnp.zeros((L, 128), BF16))
            ct_chunk = jnp.dot(vte, k_half, preferred_element_type=F32)
            new_cts.append(a_old[r:r + 1, 0:1] * cts[h] + a_new[r:r + 1, 0:1] * ct_chunk)
        for pr in range(HEADS // 2):
            both = jnp.where(lane_half_s == _state_half(d, 2 * pr), cts[2 * pr], cts[2 * pr + 1])
            ct_ref[chunk, pr * DVA:(pr + 1) * DVA, :] = both.astype(BF16)
        cts = new_cts
        m_prev = m_new
    for h in range(HEADS):
        ct_state[d, h] = cts[h]
    m_state[d] = m_prev


def _mlstm_scan_kernel(vt_f, k_f, p_f, vt_b, ksw_b, p_b, ct_f, mp_f, ct_b, mp_b, ct_state, m_state):
    @pl.when(pl.program_id(1) == 0)
    def _():
        ct_state[...] = jnp.zeros(ct_state.shape, F32)
        m_state[...] = jnp.full(m_state.shape, NEG_INF, F32)

    _scan_direction(0, vt_f, k_f, p_f, ct_f, mp_f, ct_state, m_state)
    _scan_direction(1, vt_b, ksw_b, p_b, ct_b, mp_b, ct_state, m_state)


def _mlstm_out_kernel(qt_ref, k_ref, vt_ref, p_ref, mp_f, mp_b, ccol_ref, ct_f, ct_b,
                      o_ref, gain_ref, y_ref):
    L = CHUNK
    n_chunks = k_ref.shape[0] // L
    visible = (_visible(False), _visible(True))
    fwd_row = lax.broadcasted_iota(jnp.int32, (NGR, L), 0) < HEADS
    lane_half = lax.broadcasted_iota(jnp.int32, (DVA, 128), 1) // DK
    ones_rows = jnp.ones((DVA - DV, L), BF16)
    zq = jnp.zeros((DK, L), BF16)
    for chunk in range(n_chunks):
        sl = slice(chunk * L, (chunk + 1) * L)
        m_prev = jnp.where(fwd_row, mp_f[:, sl], mp_b[:, sl])
        n_t = jnp.maximum(m_prev, p_ref[P_M, sl])
        f_inter = jnp.exp(m_prev - n_t)
        e_min = jnp.exp(-(p_ref[P_B, sl] + n_t))
        ccol = ccol_ref[sl, :]
        for h in range(HEADS):
            hs = slice(h * DV, (h + 1) * DV)
            qt = qt_ref[h * DK:(h + 1) * DK, sl]
            k_pair = k_ref[sl, (h // 2) * 2 * DK:(h // 2 + 1) * 2 * DK]
            qt_pair = jnp.concatenate([qt, zq] if h % 2 == 0 else [zq, qt], axis=0)
            st = jnp.dot(k_pair, qt_pair, preferred_element_type=F32)
            pts, qfs = [], []
            for d in range(2):
                r = HEADS * d + h
                arg = jnp.where(visible[d], ccol[:, r:r + 1] - n_t[r:r + 1, :], NEG_INF)
                pts.append((jnp.exp(arg) * st).astype(BF16))
                qfs.append((qt.astype(F32) * f_inter[r:r + 1, :]).astype(BF16))
            inter = [jnp.concatenate([qfs[0], zq], axis=1), jnp.concatenate([zq, qfs[1]], axis=1)]
            first = 0 if _state_half(0, h) == 0 else 1
            rhs = jnp.concatenate([jnp.concatenate(pts, axis=1), inter[first], inter[1 - first]], axis=0)
            vt_aug = jnp.concatenate([vt_ref[hs, sl], ones_rows], axis=0)
            ps = slice((h // 2) * DVA, (h // 2 + 1) * DVA)
            blocks = (ct_f[chunk, ps, :], ct_b[chunk, ps, :])
            ct = jnp.where(lane_half == 0, blocks[first], blocks[1 - first])
            both = jnp.dot(jnp.concatenate([vt_aug, ct], axis=1), rhs, preferred_element_type=F32)
            ht = None
            for d in range(2):
                r = HEADS * d + h
                numer = both[0:DV, d * L:(d + 1) * L]
                denom = both[DV:DV + 1, d * L:(d + 1) * L]
                part = numer / jnp.maximum(jnp.abs(denom), e_min[r:r + 1, :])
                ht = part if ht is None else ht + part
            ms = jnp.mean(ht * ht, axis=0, keepdims=True)
            hn = ht * lax.rsqrt(ms + EPS) * gain_ref[hs, :]
            y_ref[sl, hs] = (jax.nn.sigmoid(o_ref[sl, hs]) * hn.T).astype(BF16)


def _mlstm(qt, k, ksw, vt, p, ccol, o, gain_b, batch, seq, scan_rows, rows):
    T = batch * seq
    ng = seq // scan_rows
    HK, HV = HEADS * DK, HEADS * DV

    def scan_specs(group_of):
        blk = lambda b, j: b * ng + group_of(j)
        ins = [pl.BlockSpec((HV, scan_rows), lambda b, j: (0, blk(b, j))),
               pl.BlockSpec((scan_rows, HK), lambda b, j: (blk(b, j), 0)),
               pl.BlockSpec((5 * NGR, scan_rows), lambda b, j: (0, blk(b, j)))]
        outs = [pl.BlockSpec((scan_rows // CHUNK, CT_ROWS, 128), lambda b, j: (blk(b, j), 0, 0)),
                pl.BlockSpec((NGR, scan_rows), lambda b, j: (0, blk(b, j)))]
        return ins, outs

    ins_f, outs_f = scan_specs(lambda j: j)
    ins_b, outs_b = scan_specs(lambda j: ng - 1 - j)
    state_shapes = [jax.ShapeDtypeStruct((T // CHUNK, CT_ROWS, 128), BF16),
                    jax.ShapeDtypeStruct((NGR, T), F32)]
    if ng == 1:
        scan_kernel = lambda vt_r, k_r, ksw_r, p_r, *rest: _mlstm_scan_kernel(vt_r, k_r, p_r, vt_r, ksw_r, p_r, *rest)
        scan_in_specs, scan_args = [ins_f[0], ins_f[1], ins_f[1], ins_f[2]], (vt, k, ksw, p)
    else:
        scan_kernel, scan_in_specs, scan_args = _mlstm_scan_kernel, ins_f + ins_b, (vt, k, p, vt, ksw, p)
    ct_f, mp_f, ct_b, mp_b = pl.pallas_call(
        scan_kernel,
        grid=(batch, ng),
        in_specs=scan_in_specs,
        out_specs=outs_f + outs_b,
        out_shape=state_shapes + state_shapes,
        scratch_shapes=[pltpu.VMEM((2, HEADS, DVA, 128), F32), pltpu.VMEM((2, NGR, 128), F32)],
        compiler_params=pltpu.CompilerParams(dimension_semantics=("arbitrary", "arbitrary"),
                                             vmem_limit_bytes=VMEM_LIMIT),
        name="mlstm_scan",
    )(*scan_args)

    row = lambda w: pl.BlockSpec((rows, w), lambda i: (i, 0))
    col = lambda h: pl.BlockSpec((h, rows), lambda i: (0, i))
    ctb = pl.BlockSpec((rows // CHUNK, CT_ROWS, 128), lambda i: (i, 0, 0))
    return pl.pallas_call(
        _mlstm_out_kernel,
        grid=(T // rows,),
        in_specs=[col(HK), row(HK), col(HV), col(5 * NGR), col(NGR), col(NGR), row(128), ctb, ctb,
                  row(HV), _resident((HV, 128))],
        out_specs=row(HV),
        out_shape=jax.ShapeDtypeStruct((T, HV), BF16),
        compiler_params=pltpu.CompilerParams(dimension_semantics=("arbitrary",),
                                             vmem_limit_bytes=VMEM_LIMIT),
        name="mlstm_out",
    )(qt, k, vt, p, mp_f, mp_b, ccol, ct_f, ct_b, o, gain_b)


def kernel(x, norm_ffn1_pre, norm_ffn1_post, w_ffn1_in, w_ffn1_out, norm_mix_pre, norm_mix_post,
           w_mix_in, conv_w, conv_b, gate_i_bias, gate_f_bias, mlstm_norm, w_mix_out,
           norm_ffn2_pre, norm_ffn2_post, w_ffn2_in, w_ffn2_out):
    batch, seq, _ = x.shape
    T = batch * seq
    depth = norm_ffn1_pre.shape[0]
    tm = 512
    xt = x.reshape(T, D_MODEL)
    HV = HEADS * DV
    for l in range(depth):
        xt, w2_in, w2_out, w_mo, w_mi = _ffn(
            xt, norm_ffn1_pre[l][None], norm_ffn1_post[l][None], w_ffn1_in, w_ffn1_out,
            w_ffn2_in, w_ffn2_out, w_mix_out, jnp.swapaxes(w_mix_in, 1, 2), l, tm)

        gbias = jnp.pad(jnp.concatenate([gate_i_bias[l], gate_f_bias[l]]), (0, 128 - 2 * NGR))[None]
        bg, u, qt, k, ksw, vt, o, p, ccol = _mix_in(xt, norm_mix_pre[l][None], w_mi, gbias, 2 * tm)
        gain_b = jnp.broadcast_to(mlstm_norm[l][:, None], (HV, 128))
        y_mlstm = _mlstm(qt, k, ksw, vt, p, ccol, o, gain_b, batch, seq, SCAN_ROWS, MLSTM_ROWS)
        xt = _mix_ffn(xt, bg, u, conv_w[l], conv_b[l][None], y_mlstm, norm_mix_post[l][None], w_mo,
                      norm_ffn2_pre[l][None], norm_ffn2_post[l][None], w2_in, w2_out, seq, tm)
    return xt.reshape(batch, seq, D_MODEL)
```

```python
import functools

import jax
import jax.numpy as jnp
from jax import lax
from jax.experimental import pallas as pl
from jax.experimental.pallas import tpu as pltpu

D_MODEL = 1024
D_FF = 2816
CONV_WIDTH = 512
HEADS = 4
DK = 64
DV = 128
CHUNK = 128
EPS = 1e-6
NEG_INF = -1e30

FF_TILE = 256
FF_STAGE_CHUNKS = 8
FF_SIDE_CHUNKS = 8
MIX_OUT_PIECES = 4
FF_LEAD_TILES = 1
MIX_W_BLOCKS = (3 * CONV_WIDTH + 2 * HEADS * DK + 2 * HEADS * DV) // 128
MLSTM_ROWS = 2048
SCAN_ROWS = 4096
DVA = DV + 16
HALO_ROWS = 8
NGR = 2 * HEADS
CT_ROWS = (HEADS // 2) * DVA
VMEM_LIMIT = 56 * 1024 * 1024

P_E, P_M, P_B, P_G, P_MC = (slice(i * NGR, (i + 1) * NGR) for i in range(5))

F32 = jnp.float32
BF16 = jnp.bfloat16


def _rms(x, g):
    return x * lax.rsqrt(jnp.mean(x * x, axis=-1, keepdims=True) + EPS) * g


def _log_sigmoid(z):
    return jnp.minimum(z, 0.0) - jnp.log1p(jnp.exp(-jnp.abs(z)))


def _resident(shape):
    zeros = (0,) * len(shape)
    return pl.BlockSpec(shape, lambda *_: zeros, pipeline_mode=pl.Buffered(1))


def _split3(x):
    hi = x.astype(BF16)
    r1 = x - hi.astype(F32)
    mid = r1.astype(BF16)
    lo = (r1 - mid.astype(F32)).astype(BF16)
    return hi, mid, lo


def _visible(rev):
    s = lax.broadcasted_iota(jnp.int32, (CHUNK, CHUNK), 0)
    t = lax.broadcasted_iota(jnp.int32, (CHUNK, CHUNK), 1)
    return (s >= t) if rev else (s <= t)


def _running_max(x, rev):
    n = x.shape[1]
    pos = lax.broadcasted_iota(jnp.int32, x.shape, 1) & (CHUNK - 1)
    k = 1
    while k < CHUNK:
        if rev:
            shifted, ok = pltpu.roll(x, n - k, 1), pos < CHUNK - k
        else:
            shifted, ok = pltpu.roll(x, k, 1), pos >= k
        x = jnp.maximum(x, jnp.where(ok, shifted, NEG_INF))
        k *= 2
    return x


def _stage_bf16(src_hbm, dst_ref, stage_ref, sem, rows=None):
    rows = stage_ref.shape[1] if rows is None else rows
    n_chunks = src_hbm.shape[0] // rows

    def copy(c):
        return pltpu.make_async_copy(src_hbm.at[pl.ds(c * rows, rows), :],
                                     stage_ref.at[c % 2, pl.ds(0, rows), :], sem.at[c % 2])

    copy(0).start()
    for c in range(n_chunks):
        if c + 1 < n_chunks:
            copy(c + 1).start()
        copy(c).wait()
        dst_ref[c * rows:(c + 1) * rows, :] = stage_ref[c % 2, 0:rows, :].astype(BF16)


def _mix_out_tile(tiles_per_seq, x_ref, bg_ref, u_ref, uprev_ref, unext_ref, cw_ref, cb_ref,
                  y_ref, wmo_ref, gmix_ref, gpre_ref, on_piece=None):
    i = pl.program_id(0)
    tm = u_ref.shape[0]
    u = u_ref[...]
    has_prev = (i % tiles_per_seq != 0).astype(F32)
    has_next = (i % tiles_per_seq != tiles_per_seq - 1).astype(F32)
    prev_row = uprev_ref[HALO_ROWS - 1:HALO_ROWS, :] * has_prev
    next_row = unext_ref[0:1, :] * has_next
    ri = lax.broadcasted_iota(jnp.int32, u.shape, 0)
    u_m1 = jnp.where(ri == 0, prev_row, pltpu.roll(u, 1, 0))
    u_p1 = jnp.where(ri == tm - 1, next_row, pltpu.roll(u, tm - 1, 0))
    conv = cw_ref[0:1, :] * u_m1 + cw_ref[1:2, :] * u + cw_ref[2:3, :] * u_p1
    y_conv = (bg_ref[...] * (conv + cb_ref[...])).astype(BF16)
    rp = tm // MIX_OUT_PIECES
    xs, xns = [], []
    for r in range(MIX_OUT_PIECES):
        rs = slice(r * rp, (r + 1) * rp)
        h = jnp.dot(y_ref[rs, :], wmo_ref[CONV_WIDTH:, :], preferred_element_type=F32) \
            + jnp.dot(y_conv[rs, :], wmo_ref[0:CONV_WIDTH, :], preferred_element_type=F32)
        xs.append(x_ref[rs, :] + _rms(h, gmix_ref[...]))
        xns.append(_rms(xs[-1], gpre_ref[...]).astype(BF16))
        if on_piece is not None:
            on_piece(rs, xns[-1])
    return jnp.concatenate(xs, axis=0), xns


def _hidden_tile(xr, rs, j, win_ref, h_ref):
    lo = j * FF_TILE
    gate = jnp.dot(xr, win_ref[:, lo:lo + FF_TILE], preferred_element_type=F32)
    up = jnp.dot(xr, win_ref[:, D_FF + lo:D_FF + lo + FF_TILE], preferred_element_type=F32)
    h_ref[rs, lo:lo + FF_TILE] = (gate * jax.nn.sigmoid(gate) * up).astype(BF16)


def _ffn_tile(x, xn_pieces, gpost_ref, win_ref, wout_ref, h_ref):
    xn = jnp.concatenate(xn_pieces, axis=0)
    for j in range(FF_LEAD_TILES, D_FF // FF_TILE):
        _hidden_tile(xn, slice(None), j, win_ref, h_ref)
    y = jnp.dot(h_ref[...], wout_ref[...], preferred_element_type=F32)
    return x + _rms(y, 0.5 * gpost_ref[...])


def _zero_after(v):
    u = pltpu.bitcast(v, jnp.uint32)
    acc = None
    for r in range(u.shape[0] // 8):
        for c in range(u.shape[1] // 128):
            t = u[r * 8:(r + 1) * 8, c * 128:(c + 1) * 128]
            acc = t if acc is None else acc | t
    return pltpu.bitcast((acc >> 16) >> 16, F32)


def _convert_later_weights(i, w2in_blk, w2out_blk, wmo_blk, wt_blk, wtg_blk, w2in_bf, w2out_bf, wmo_bf, wb_blk):
    zeros = []
    for src, dst in ((w2in_blk, w2in_bf), (w2out_blk, w2out_bf), (wmo_blk, wmo_bf)):
        v = src[...].astype(BF16)
        dst[...] = v
        zeros.append(_zero_after(v[0:16, 0:128]))
    gates = jnp.concatenate([wtg_blk[...], jnp.zeros((128 - 2 * NGR, D_MODEL), F32)], axis=0)
    last = jnp.clip(i - 1, 0, MIX_W_BLOCKS) == MIX_W_BLOCKS
    blk = jnp.where(last, gates, wt_blk[...]).T
    wb_blk[...] = blk.astype(BF16)
    zeros.append(_zero_after(blk))
    return zeros


def _ffn_kernel(layer, n_tiles, xa_ref, xc_ref, gpre_ref, gpost_ref, win_hbm, wout_hbm,
                w2in_blk, w2out_blk, wmo_blk, wt_blk, wtg_blk,
                o_ref, w2in_bf, w2out_bf, wmo_bf, wb_blk,
                h_ref, win_ref, wout_ref, stage_in, stage_out, sem, xn_ref, y_ref):
    i = pl.program_id(0)
    tm = xa_ref.shape[0]
    g_half = 0.5 * gpost_ref[...]

    @pl.when(i == 0)
    def _():
        _stage_bf16(win_hbm.at[layer], win_ref, stage_in, sem)
        _stage_bf16(wout_hbm.at[layer], wout_ref, stage_out, sem)
        xn_ref[0] = _rms(xa_ref[...], gpre_ref[...]).astype(BF16)
        y_ref[...] = jnp.zeros(y_ref.shape, F32)

    @pl.when((i >= 1) & (i <= n_tiles))
    def _():
        xn = xn_ref[(i - 1) % 2]
        rp = tm // FF_SIDE_CHUNKS
        for j in range(D_FF // FF_TILE):
            lo = j * FF_TILE
            gate = jnp.dot(xn, win_ref[:, lo:lo + FF_TILE], preferred_element_type=F32)
            up = jnp.dot(xn, win_ref[:, D_FF + lo:D_FF + lo + FF_TILE], preferred_element_type=F32)
            hm = gate * jax.nn.sigmoid(gate) * up
            h_ref[:, lo:lo + FF_TILE] = hm.astype(BF16)
            if j < FF_SIDE_CHUNKS:
                rs = slice(j * rp, (j + 1) * rp)
                out_rows = xc_ref[rs, :] + _rms(y_ref[rs, :], g_half)
                o_ref[rs, :] = out_rows
                xn_rows = _rms(xa_ref[rs, :], gpre_ref[...])
                xn_ref[i % 2, rs, :] = xn_rows.astype(BF16)
                zero = _zero_after(out_rows) + _zero_after(xn_rows)
                h_ref[0:8, lo:lo + 128] = (hm[0:8, 0:128] + zero).astype(BF16)
            else:
                if j == FF_SIDE_CHUNKS:
                    side = _convert_later_weights(i, w2in_blk, w2out_blk, wmo_blk, wt_blk, wtg_blk,
                                                  w2in_bf, w2out_bf, wmo_bf, wb_blk)
                    side = [side[0], side[1] + side[2], side[3]]
                h_ref[0:8, lo:lo + 128] = (hm[0:8, 0:128] + side[j - FF_SIDE_CHUNKS]).astype(BF16)
        y_ref[...] = jnp.dot(h_ref[...], wout_ref[...], preferred_element_type=F32)

    @pl.when(i == n_tiles + 1)
    def _():
        o_ref[...] = xc_ref[...] + _rms(y_ref[...], g_half)


def _mix_ffn_kernel(tiles_per_seq, x_ref, bg_ref, u_ref, uprev_ref, unext_ref, cw_ref, cb_ref,
                    y_ref, gmix_ref, wmo_ref, gpre_ref, gpost_ref, win_ref, wout_ref, o_ref, h_ref):
    def lead_tiles(rs, xr):
        for j in range(FF_LEAD_TILES):
            _hidden_tile(xr, rs, j, win_ref, h_ref)

    x, xn = _mix_out_tile(tiles_per_seq, x_ref, bg_ref, u_ref, uprev_ref, unext_ref, cw_ref, cb_ref,
                          y_ref, wmo_ref, gmix_ref, gpre_ref, on_piece=lead_tiles)
    o_ref[...] = _ffn_tile(x, xn, gpost_ref, win_ref, wout_ref, h_ref)


def _ffn_scratch(tm):
    return [
        pltpu.VMEM((tm, D_FF), BF16),
        pltpu.VMEM((D_MODEL, 2 * D_FF), BF16),
        pltpu.VMEM((D_FF, D_MODEL), BF16),
    ], [
        pltpu.VMEM((2, D_MODEL // FF_STAGE_CHUNKS, 2 * D_FF), F32),
        pltpu.VMEM((2, D_FF // FF_STAGE_CHUNKS, D_MODEL), F32),
        pltpu.SemaphoreType.DMA((2,)),
    ]


def _ffn(x, gpre, gpost, w_in, w_out, w2_in, w2_out, w_mix_out, w_mix_in_t, layer, tm):
    T = x.shape[0]
    nt = T // tm
    assert nt > MIX_W_BLOCKS and nt % 2 == 0, f"first FFN needs more than {MIX_W_BLOCKS} token tiles, got {nt}"
    g0 = MIX_W_BLOCKS * 128
    resident, staging = _ffn_scratch(tm)
    tile = lambda lag: pl.BlockSpec((tm, D_MODEL), lambda i: (jnp.clip(i - lag, 0, nt - 1), 0))
    rows_in = lambda n, w, last: pl.BlockSpec((None, n, w), lambda i: (layer, jnp.clip(i - 1, 0, last), 0))
    rows_out = lambda n, w, last: pl.BlockSpec((n, w), lambda i: (jnp.clip(i - 1, 0, last), 0))
    r_in, r_out, r_mo = D_MODEL // nt, D_FF // (nt // 2), D_MODEL // (nt // 2)
    return pl.pallas_call(
        functools.partial(_ffn_kernel, layer, nt),
        grid=(nt + 2,),
        in_specs=[
            tile(0), tile(2),
            _resident((1, D_MODEL)),
            _resident((1, D_MODEL)),
            pl.BlockSpec(memory_space=pl.ANY),
            pl.BlockSpec(memory_space=pl.ANY),
            rows_in(r_in, 2 * D_FF, nt - 1),
            rows_in(r_out, D_MODEL, nt // 2 - 1),
            rows_in(r_mo, D_MODEL, nt // 2 - 1),
            rows_in(128, D_MODEL, MIX_W_BLOCKS - 1),
            pl.BlockSpec((None, 2 * NGR, D_MODEL), lambda i: (layer, g0 // (2 * NGR), 0)),
        ],
        out_specs=[
            tile(2),
            rows_out(r_in, 2 * D_FF, nt - 1),
            rows_out(r_out, D_MODEL, nt // 2 - 1),
            rows_out(r_mo, D_MODEL, nt // 2 - 1),
            pl.BlockSpec((D_MODEL, 128), lambda i: (0, jnp.clip(i - 1, 0, MIX_W_BLOCKS))),
        ],
        out_shape=[
            jax.ShapeDtypeStruct((T, D_MODEL), F32),
            jax.ShapeDtypeStruct((D_MODEL, 2 * D_FF), BF16),
            jax.ShapeDtypeStruct((D_FF, D_MODEL), BF16),
            jax.ShapeDtypeStruct((D_MODEL, D_MODEL), BF16),
            jax.ShapeDtypeStruct((D_MODEL, g0 + 128), BF16),
        ],
        scratch_shapes=resident + staging + [pltpu.VMEM((2, tm, D_MODEL), BF16),
                                             pltpu.VMEM((tm, D_MODEL), F32)],
        compiler_params=pltpu.CompilerParams(
            dimension_semantics=("arbitrary",), vmem_limit_bytes=VMEM_LIMIT),
        name="ffn",
    )(x, x, gpre, gpost, w_in, w_out, w2_in, w2_out, w_mix_out, w_mix_in_t, w_mix_in_t)


def _mix_ffn(x, bg, u, conv_w, conv_b, y_mlstm, gmix, w_mix_out, gpre, gpost, w_in, w_out, seq, tm):
    T = x.shape[0]
    tiles_per_seq = seq // tm
    sub = tm // HALO_ROWS
    last = T // HALO_ROWS - 1
    row = lambda w: pl.BlockSpec((tm, w), lambda i: (i, 0))
    return pl.pallas_call(
        functools.partial(_mix_ffn_kernel, tiles_per_seq),
        grid=(T // tm,),
        in_specs=[
            row(D_MODEL), row(CONV_WIDTH), row(CONV_WIDTH),
            pl.BlockSpec((HALO_ROWS, CONV_WIDTH), lambda i: (jnp.maximum(i * sub - 1, 0), 0)),
            pl.BlockSpec((HALO_ROWS, CONV_WIDTH), lambda i: (jnp.minimum((i + 1) * sub, last), 0)),
            _resident((3, CONV_WIDTH)), _resident((1, CONV_WIDTH)),
            row(HEADS * DV), _resident((1, D_MODEL)), _resident(w_mix_out.shape),
            _resident((1, D_MODEL)), _resident((1, D_MODEL)), _resident(w_in.shape), _resident(w_out.shape),
        ],
        out_specs=row(D_MODEL),
        out_shape=jax.ShapeDtypeStruct((T, D_MODEL), F32),
        scratch_shapes=[pltpu.VMEM((tm, D_FF), BF16)],
        compiler_params=pltpu.CompilerParams(
            dimension_semantics=("arbitrary",), vmem_limit_bytes=VMEM_LIMIT),
        name="mix_ffn",
    )(x, bg, u, u, u, conv_w, conv_b, y_mlstm, gmix, w_mix_out, gpre, gpost, w_in, w_out)


def _gate_rows(zr, p_ref, ccol_ref):
    L = CHUNK
    n_chunks = zr.shape[1] // L
    fwd_row = lax.broadcasted_iota(jnp.int32, (NGR, L), 0) < HEADS
    fwd_col = lax.broadcasted_iota(jnp.int32, (NGR, 1), 0) < HEADS
    li = zr[0:NGR]
    gates = jnp.concatenate([li, _log_sigmoid(zr[NGR:2 * NGR])], axis=0)
    x3 = jnp.concatenate(_split3(gates), axis=0)
    stacked = jnp.concatenate([x3[:, c * L:(c + 1) * L] for c in range(n_chunks)], axis=0)
    cum_ops = jnp.concatenate([_visible(False), _visible(True)], axis=1).astype(BF16)
    cum = jnp.dot(stacked, cum_ops, preferred_element_type=F32)
    b_chunks = []
    for c in range(n_chunks):
        blk = cum[c * 48:(c + 1) * 48]
        s16 = blk[0:16] + blk[16:32] + blk[32:48]
        b_chunks.append(jnp.where(fwd_row, s16[NGR:2 * NGR, 0:L], s16[NGR:2 * NGR, L:2 * L]))
    b = jnp.concatenate(b_chunks, axis=1)
    cc = li - b
    fwd_all = lax.broadcasted_iota(jnp.int32, cc.shape, 0) < HEADS
    p_ref[P_M, :] = jnp.where(fwd_all, _running_max(cc, False), _running_max(cc, True))
    p_ref[P_B, :] = b
    pad = jnp.zeros((L - NGR, L), F32)
    for c in range(n_chunks):
        sl = slice(c * L, (c + 1) * L)
        b_c, cc_c = b_chunks[c], cc[:, sl]
        g = jnp.where(fwd_col, b_c[:, L - 1:L], b_c[:, 0:1])
        m_chunk = g + jnp.max(cc_c, axis=1, keepdims=True)
        p_ref[P_E, sl] = jnp.exp(g + cc_c - m_chunk)
        p_ref[P_G, sl] = jnp.broadcast_to(g, (NGR, L))
        p_ref[P_MC, sl] = jnp.broadcast_to(m_chunk, (NGR, L))
        ccol_ref[sl, :] = jnp.concatenate([cc_c, pad], axis=0).T


def _mix_in_kernel(x_ref, gpre_ref, wb_ref, gbias_ref,
                   bg_ref, u_ref, qt_ref, k_ref, ksw_ref, vt_ref, o_ref, p_ref, ccol_ref):
    W, HK, HV = CONV_WIDTH, HEADS * DK, HEADS * DV
    q0 = 3 * W
    k0, v0 = q0 + HK, q0 + 2 * HK
    o0 = v0 + HV
    g0 = o0 + HV

    xn = _rms(x_ref[...], gpre_ref[...]).astype(BF16)
    proj = lambda a, b: jnp.dot(xn, wb_ref[:, a:b], preferred_element_type=F32)

    zg = proj(g0, g0 + 128) + gbias_ref[...]
    bg_ref[...] = proj(0, W)
    u_ref[...] = proj(W, 2 * W) * proj(2 * W, 3 * W)
    _gate_rows(zg.T[0:2 * NGR], p_ref, ccol_ref)
    qt_ref[...] = (proj(q0, k0) * (DK ** -0.5)).T.astype(BF16)
    kk = proj(k0, v0)
    k_ref[...] = kk.astype(BF16)
    for pair in range(HEADS // 2):
        ps = slice(pair * 2 * DK, (pair + 1) * 2 * DK)
        ksw_ref[:, ps] = pltpu.roll(kk[:, ps], DK, 1).astype(BF16)
    vt_ref[...] = proj(v0, o0).T.astype(BF16)
    o_ref[...] = proj(o0, g0)


def _mix_in(x, gpre, wb, gbias, tm):
    T = x.shape[0]
    row = lambda w: pl.BlockSpec((tm, w), lambda i: (i, 0))
    col = lambda h: pl.BlockSpec((h, tm), lambda i: (0, i))
    HK, HV = HEADS * DK, HEADS * DV
    return pl.pallas_call(
        _mix_in_kernel,
        grid=(T // tm,),
        in_specs=[row(D_MODEL), _resident((1, D_MODEL)), _resident(wb.shape), _resident((1, 128))],
        out_specs=[row(CONV_WIDTH), row(CONV_WIDTH), col(HK), row(HK), row(HK), col(HV), row(HV),
                   col(5 * NGR), row(128)],
        out_shape=[
            jax.ShapeDtypeStruct((T, CONV_WIDTH), F32),
            jax.ShapeDtypeStruct((T, CONV_WIDTH), F32),
            jax.ShapeDtypeStruct((HK, T), BF16),
            jax.ShapeDtypeStruct((T, HK), BF16),
            jax.ShapeDtypeStruct((T, HK), BF16),
            jax.ShapeDtypeStruct((HV, T), BF16),
            jax.ShapeDtypeStruct((T, HV), F32),
            jax.ShapeDtypeStruct((5 * NGR, T), F32),
            jax.ShapeDtypeStruct((T, 128), F32),
        ],
        compiler_params=pltpu.CompilerParams(
            dimension_semantics=("arbitrary",), vmem_limit_bytes=VMEM_LIMIT),
        name="mix_in",
    )(x, gpre, wb, gbias)


def _state_half(d, h):
    return (h % 2) ^ d


def _scan_direction(d, vt_ref, k_ref, p_ref, ct_ref, mp_ref, ct_state, m_state):
    L = CHUNK
    n_chunks = k_ref.shape[0] // L
    lane_half = lax.broadcasted_iota(jnp.int32, (L, 128), 1) // DK
    lane_half_s = lax.broadcasted_iota(jnp.int32, (DVA, 128), 1) // DK
    ones_rows = jnp.ones((DVA - DV, L), BF16)

    cts = [ct_state[d, h] for h in range(HEADS)]
    m_prev = m_state[d]
    for chunk in (range(n_chunks - 1, -1, -1) if d == 1 else range(n_chunks)):
        sl = slice(chunk * L, (chunk + 1) * L)
        e, g, m_chunk = p_ref[P_E, sl], p_ref[P_G, sl], p_ref[P_MC, sl]
        mp_ref[:, sl] = m_prev
        m_new = jnp.maximum(g + m_prev, m_chunk)
        a_old = jnp.exp(g + m_prev - m_new)
        a_new = jnp.exp(m_chunk - m_new)
        new_cts = []
        for h in range(HEADS):
            r = HEADS * d + h
            vt_aug = jnp.concatenate([vt_ref[h * DV:(h + 1) * DV, sl], ones_rows], axis=0)
            vte = (vt_aug.astype(F32) * e[r:r + 1, :]).astype(BF16)
            pair = slice((h // 2) * 2 * DK, (h // 2 + 1) * 2 * DK)
            k_half = jnp.where(lane_half == _state_half(d, h), k_ref[sl, pair], jnp.zeros((L, 128), BF16))
            ct_chunk = jnp.dot(vte, k_half, preferred_element_type=F32)
            new_cts.append(a_old[r:r + 1, 0:1] * cts[h] + a_new[r:r + 1, 0:1] * ct_chunk)
        for pr in range(HEADS // 2):
            both = jnp.where(lane_half_s == _state_half(d, 2 * pr), cts[2 * pr], cts[2 * pr + 1])
            ct_ref[chunk, pr * DVA:(pr + 1) * DVA, :] = both.astype(BF16)
        cts = new_cts
        m_prev = m_new
    for h in range(HEADS):
        ct_state[d, h] = cts[h]
    m_state[d] = m_prev


def _mlstm_scan_kernel(vt_f, k_f, p_f, vt_b, ksw_b, p_b, ct_f, mp_f, ct_b, mp_b, ct_state, m_state):
    @pl.when(pl.program_id(1) == 0)
    def _():
        ct_state[...] = jnp.zeros(ct_state.shape, F32)
        m_state[...] = jnp.full(m_state.shape, NEG_INF, F32)

    _scan_direction(0, vt_f, k_f, p_f, ct_f, mp_f, ct_state, m_state)
    _scan_direction(1, vt_b, ksw_b, p_b, ct_b, mp_b, ct_state, m_state)


def _mlstm_out_kernel(qt_ref, k_ref, vt_ref, p_ref, mp_f, mp_b, ccol_ref, ct_f, ct_b,
                      o_ref, gain_ref, y_ref):
    L = CHUNK
    n_chunks = k_ref.shape[0] // L
    visible = (_visible(False), _visible(True))
    fwd_row = lax.broadcasted_iota(jnp.int32, (NGR, L), 0) < HEADS
    lane_half = lax.broadcasted_iota(jnp.int32, (DVA, 128), 1) // DK
    ones_rows = jnp.ones((DVA - DV, L), BF16)
    zq = jnp.zeros((DK, L), BF16)
    for chunk in range(n_chunks):
        sl = slice(chunk * L, (chunk + 1) * L)
        m_prev = jnp.where(fwd_row, mp_f[:, sl], mp_b[:, sl])
        n_t = jnp.maximum(m_prev, p_ref[P_M, sl])
        f_inter = jnp.exp(m_prev - n_t)
        e_min = jnp.exp(-(p_ref[P_B, sl] + n_t))
        ccol = ccol_ref[sl, :]
        for h in range(HEADS):
            hs = slice(h * DV, (h + 1) * DV)
            qt = qt_ref[h * DK:(h + 1) * DK, sl]
            k_pair = k_ref[sl, (h // 2) * 2 * DK:(h // 2 + 1) * 2 * DK]
            qt_pair = jnp.concatenate([qt, zq] if h % 2 == 0 else [zq, qt], axis=0)
            st = jnp.dot(k_pair, qt_pair, preferred_element_type=F32)
            pts, qfs = [], []
            for d in range(2):
                r = HEADS * d + h
                arg = jnp.where(visible[d], ccol[:, r:r + 1] - n_t[r:r + 1, :], NEG_INF)
                pts.append((jnp.exp(arg) * st).astype(BF16))
                qfs.append((qt.astype(F32) * f_inter[r:r + 1, :]).astype(BF16))
            inter = [jnp.concatenate([qfs[0], zq], axis=1), jnp.concatenate([zq, qfs[1]], axis=1)]
            first = 0 if _state_half(0, h) == 0 else 1
            rhs = jnp.concatenate([jnp.concatenate(pts, axis=1), inter[first], inter[1 - first]], axis=0)
            vt_aug = jnp.concatenate([vt_ref[hs, sl], ones_rows], axis=0)
            ps = slice((h // 2) * DVA, (h // 2 + 1) * DVA)
            blocks = (ct_f[chunk, ps, :], ct_b[chunk, ps, :])
            ct = jnp.where(lane_half == 0, blocks[first], blocks[1 - first])
            both = jnp.dot(jnp.concatenate([vt_aug, ct], axis=1), rhs, preferred_element_type=F32)
            ht = None
            for d in range(2):
                r = HEADS * d + h
                numer = both[0:DV, d * L:(d + 1) * L]
                denom = both[DV:DV + 1, d * L:(d + 1) * L]
                part = numer / jnp.maximum(jnp.abs(denom), e_min[r:r + 1, :])
                ht = part if ht is None else ht + part
            ms = jnp.mean(ht * ht, axis=0, keepdims=True)
            hn = ht * lax.rsqrt(ms + EPS) * gain_ref[hs, :]
            y_ref[sl, hs] = (jax.nn.sigmoid(o_ref[sl, hs]) * hn.T).astype(BF16)


def _mlstm(qt, k, ksw, vt, p, ccol, o, gain_b, batch, seq, scan_rows, rows):
    T = batch * seq
    ng = seq // scan_rows
    HK, HV = HEADS * DK, HEADS * DV

    def scan_specs(group_of):
        blk = lambda b, j: b * ng + group_of(j)
        ins = [pl.BlockSpec((HV, scan_rows), lambda b, j: (0, blk(b, j))),
               pl.BlockSpec((scan_rows, HK), lambda b, j: (blk(b, j), 0)),
               pl.BlockSpec((5 * NGR, scan_rows), lambda b, j: (0, blk(b, j)))]
        outs = [pl.BlockSpec((scan_rows // CHUNK, CT_ROWS, 128), lambda b, j: (blk(b, j), 0, 0)),
                pl.BlockSpec((NGR, scan_rows), lambda b, j: (0, blk(b, j)))]
        return ins, outs

    ins_f, outs_f = scan_specs(lambda j: j)
    ins_b, outs_b = scan_specs(lambda j: ng - 1 - j)
    state_shapes = [jax.ShapeDtypeStruct((T // CHUNK, CT_ROWS, 128), BF16),
                    jax.ShapeDtypeStruct((NGR, T), F32)]
    if ng == 1:
        scan_kernel = lambda vt_r, k_r, ksw_r, p_r, *rest: _mlstm_scan_kernel(vt_r, k_r, p_r, vt_r, ksw_r, p_r, *rest)
        scan_in_specs, scan_args = [ins_f[0], ins_f[1], ins_f[1], ins_f[2]], (vt, k, ksw, p)
    else:
        scan_kernel, scan_in_specs, scan_args = _mlstm_scan_kernel, ins_f + ins_b, (vt, k, p, vt, ksw, p)
    ct_f, mp_f, ct_b, mp_b = pl.pallas_call(
        scan_kernel,
        grid=(batch, ng),
        in_specs=scan_in_specs,
        out_specs=outs_f + outs_b,
        out_shape=state_shapes + state_shapes,
        scratch_shapes=[pltpu.VMEM((2, HEADS, DVA, 128), F32), pltpu.VMEM((2, NGR, 128), F32)],
        compiler_params=pltpu.CompilerParams(dimension_semantics=("arbitrary", "arbitrary"),
                                             vmem_limit_bytes=VMEM_LIMIT),
        name="mlstm_scan",
    )(*scan_args)

    row = lambda w: pl.BlockSpec((rows, w), lambda i: (i, 0))
    col = lambda h: pl.BlockSpec((h, rows), lambda i: (0, i))
    ctb = pl.BlockSpec((rows // CHUNK, CT_ROWS, 128), lambda i: (i, 0, 0))
    return pl.pallas_call(
        _mlstm_out_kernel,
        grid=(T // rows,),
        in_specs=[col(HK), row(HK), col(HV), col(5 * NGR), col(NGR), col(NGR), row(128), ctb, ctb,
                  row(HV), _resident((HV, 128))],
        out_specs=row(HV),
        out_shape=jax.ShapeDtypeStruct((T, HV), BF16),
        compiler_params=pltpu.CompilerParams(dimension_semantics=("arbitrary",),
                                             vmem_limit_bytes=VMEM_LIMIT),
        name="mlstm_out",
    )(qt, k, vt, p, mp_f, mp_b, ccol, ct_f, ct_b, o, gain_b)


def kernel(x, norm_ffn1_pre, norm_ffn1_post, w_ffn1_in, w_ffn1_out, norm_mix_pre, norm_mix_post,
           w_mix_in, conv_w, conv_b, gate_i_bias, gate_f_bias, mlstm_norm, w_mix_out,
           norm_ffn2_pre, norm_ffn2_post, w_ffn2_in, w_ffn2_out):
    batch, seq, _ = x.shape
    T = batch * seq
    depth = norm_ffn1_pre.shape[0]
    tm = 512
    xt = x.reshape(T, D_MODEL)
    HV = HEADS * DV
    for l in range(depth):
        xt, w2_in, w2_out, w_mo, w_mi = _ffn(
            xt, norm_ffn1_pre[l][None], norm_ffn1_post[l][None], w_ffn1_in, w_ffn1_out,
            w_ffn2_in, w_ffn2_out, w_mix_out, jnp.swapaxes(w_mix_in, 1, 2), l, tm)

        gbias = jnp.pad(jnp.concatenate([gate_i_bias[l], gate_f_bias[l]]), (0, 128 - 2 * NGR))[None]
        bg, u, qt, k, ksw, vt, o, p, ccol = _mix_in(xt, norm_mix_pre[l][None], w_mi, gbias, 2 * tm)
        gain_b = jnp.broadcast_to(mlstm_norm[l][:, None], (HV, 128))
        y_mlstm = _mlstm(qt, k, ksw, vt, p, ccol, o, gain_b, batch, seq, SCAN_ROWS, MLSTM_ROWS)
        xt = _mix_ffn(xt, bg, u, conv_w[l], conv_b[l][None], y_mlstm, norm_mix_post[l][None], w_mo,
                      norm_ffn2_pre[l][None], norm_ffn2_post[l][None], w2_in, w2_out, seq, tm)
    return xt.reshape(batch, seq, D_MODEL)
```

```python
import functools

import jax
import jax.numpy as jnp
from jax import lax
from jax.experimental import pallas as pl
from jax.experimental.pallas import tpu as pltpu

D_MODEL = 1024
D_FF = 2816
CONV_WIDTH = 512
HEADS = 4
DK = 64
DV = 128
CHUNK = 128
EPS = 1e-6
NEG_INF = -1e30

FF_TILE = 256
FF_STAGE_CHUNKS = 8
FF_SIDE_CHUNKS = 8
MIX_OUT_PIECES = 4
FF_LEAD_TILES = 1
MIX_W_BLOCKS = (3 * CONV_WIDTH + 2 * HEADS * DK + 2 * HEADS * DV) // 128
MLSTM_ROWS = 2048
SCAN_ROWS = 4096
DVA = DV + 16
HALO_ROWS = 8
NGR = 2 * HEADS
CT_ROWS = (HEADS // 2) * DVA
VMEM_LIMIT = 56 * 1024 * 1024

P_E, P_M, P_B, P_G, P_MC = (slice(i * NGR, (i + 1) * NGR) for i in range(5))

F32 = jnp.float32
BF16 = jnp.bfloat16


def _rms(x, g):
    return x * lax.rsqrt(jnp.mean(x * x, axis=-1, keepdims=True) + EPS) * g


def _log_sigmoid(z):
    return jnp.minimum(z, 0.0) - jnp.log1p(jnp.exp(-jnp.abs(z)))


def _resident(shape):
    zeros = (0,) * len(shape)
    return pl.BlockSpec(shape, lambda *_: zeros, pipeline_mode=pl.Buffered(1))


def _split3(x):
    hi = x.astype(BF16)
    r1 = x - hi.astype(F32)
    mid = r1.astype(BF16)
    lo = (r1 - mid.astype(F32)).astype(BF16)
    return hi, mid, lo


def _visible(rev):
    s = lax.broadcasted_iota(jnp.int32, (CHUNK, CHUNK), 0)
    t = lax.broadcasted_iota(jnp.int32, (CHUNK, CHUNK), 1)
    return (s >= t) if rev else (s <= t)


def _running_max(x, rev):
    n = x.shape[1]
    pos = lax.broadcasted_iota(jnp.int32, x.shape, 1) & (CHUNK - 1)
    k = 1
    while k < CHUNK:
        if rev:
            shifted, ok = pltpu.roll(x, n - k, 1), pos < CHUNK - k
        else:
            shifted, ok = pltpu.roll(x, k, 1), pos >= k
        x = jnp.maximum(x, jnp.where(ok, shifted, NEG_INF))
        k *= 2
    return x


def _stage_bf16(src_hbm, dst_ref, stage_ref, sem, rows=None):
    rows = stage_ref.shape[1] if rows is None else rows
    n_chunks = src_hbm.shape[0] // rows

    def copy(c):
        return pltpu.make_async_copy(src_hbm.at[pl.ds(c * rows, rows), :],
                                     stage_ref.at[c % 2, pl.ds(0, rows), :], sem.at[c % 2])

    copy(0).start()
    for c in range(n_chunks):
        if c + 1 < n_chunks:
            copy(c + 1).start()
        copy(c).wait()
        dst_ref[c * rows:(c + 1) * rows, :] = stage_ref[c % 2, 0:rows, :].astype(BF16)


def _mix_out_tile(tiles_per_seq, x_ref, bg_ref, u_ref, uprev_ref, unext_ref, cw_ref, cb_ref,
                  y_ref, wmo_ref, gmix_ref, gpre_ref):
    i = pl.program_id(0)
    tm = u_ref.shape[0]
    u = u_ref[...]
    has_prev = (i % tiles_per_seq != 0).astype(F32)
    has_next = (i % tiles_per_seq != tiles_per_seq - 1).astype(F32)
    prev_row = uprev_ref[HALO_ROWS - 1:HALO_ROWS, :] * has_prev
    next_row = unext_ref[0:1, :] * has_next
    ri = lax.broadcasted_iota(jnp.int32, u.shape, 0)
    u_m1 = jnp.where(ri == 0, prev_row, pltpu.roll(u, 1, 0))
    u_p1 = jnp.where(ri == tm - 1, next_row, pltpu.roll(u, tm - 1, 0))
    conv = cw_ref[0:1, :] * u_m1 + cw_ref[1:2, :] * u + cw_ref[2:3, :] * u_p1
    y_conv = (bg_ref[...] * (conv + cb_ref[...])).astype(BF16)
    rp = tm // MIX_OUT_PIECES
    xs, xns, hs = [], [], []

    def norms(r):
        xs.append(x_ref[r * rp:(r + 1) * rp, :] + _rms(hs[r], gmix_ref[...]))
        xns.append(_rms(xs[-1], gpre_ref[...]).astype(BF16))

    for r in range(MIX_OUT_PIECES):
        rs = slice(r * rp, (r + 1) * rp)
        hs.append(jnp.dot(y_ref[rs, :], wmo_ref[CONV_WIDTH:, :], preferred_element_type=F32)
                  + jnp.dot(y_conv[rs, :], wmo_ref[0:CONV_WIDTH, :], preferred_element_type=F32))
        if r >= 1:
            norms(r - 1)
    norms(MIX_OUT_PIECES - 1)
    return jnp.concatenate(xs, axis=0), xns


def _ffn_tile(x, xn_pieces, gpost_ref, win_ref, wout_ref, h_ref):
    rp = xn_pieces[0].shape[0]
    xn = jnp.concatenate(xn_pieces, axis=0)
    for j in range(D_FF // FF_TILE):
        lo = j * FF_TILE
        lhs = list(enumerate(xn_pieces)) if j < FF_LEAD_TILES and len(xn_pieces) > 1 else [(None, xn)]
        for r, xr in lhs:
            rs = slice(None) if r is None else slice(r * rp, (r + 1) * rp)
            gate = jnp.dot(xr, win_ref[:, lo:lo + FF_TILE], preferred_element_type=F32)
            up = jnp.dot(xr, win_ref[:, D_FF + lo:D_FF + lo + FF_TILE], preferred_element_type=F32)
            h_ref[rs, lo:lo + FF_TILE] = (gate * jax.nn.sigmoid(gate) * up).astype(BF16)
    y = jnp.dot(h_ref[...], wout_ref[...], preferred_element_type=F32)
    return x + _rms(y, 0.5 * gpost_ref[...])


def _zero_after(v):
    u = pltpu.bitcast(v, jnp.uint32)
    acc = None
    for r in range(u.shape[0] // 8):
        for c in range(u.shape[1] // 128):
            t = u[r * 8:(r + 1) * 8, c * 128:(c + 1) * 128]
            acc = t if acc is None else acc | t
    return pltpu.bitcast((acc >> 16) >> 16, F32)


def _convert_later_weights(i, w2in_blk, w2out_blk, wmo_blk, wt_blk, wtg_blk, w2in_bf, w2out_bf, wmo_bf, wb_blk):
    zeros = []
    for src, dst in ((w2in_blk, w2in_bf), (w2out_blk, w2out_bf), (wmo_blk, wmo_bf)):
        v = src[...].astype(BF16)
        dst[...] = v
        zeros.append(_zero_after(v[0:16, 0:128]))
    gates = jnp.concatenate([wtg_blk[...], jnp.zeros((128 - 2 * NGR, D_MODEL), F32)], axis=0)
    last = jnp.clip(i - 1, 0, MIX_W_BLOCKS) == MIX_W_BLOCKS
    blk = jnp.where(last, gates, wt_blk[...]).T
    wb_blk[...] = blk.astype(BF16)
    zeros.append(_zero_after(blk))
    return zeros


def _ffn_kernel(layer, n_tiles, xa_ref, xc_ref, gpre_ref, gpost_ref, win_hbm, wout_hbm,
                w2in_blk, w2out_blk, wmo_blk, wt_blk, wtg_blk,
                o_ref, w2in_bf, w2out_bf, wmo_bf, wb_blk,
                h_ref, win_ref, wout_ref, stage_in, stage_out, sem, xn_ref, y_ref):
    i = pl.program_id(0)
    tm = xa_ref.shape[0]
    g_half = 0.5 * gpost_ref[...]

    @pl.when(i == 0)
    def _():
        _stage_bf16(win_hbm.at[layer], win_ref, stage_in, sem)
        _stage_bf16(wout_hbm.at[layer], wout_ref, stage_out, sem)
        xn_ref[0] = _rms(xa_ref[...], gpre_ref[...]).astype(BF16)
        y_ref[...] = jnp.zeros(y_ref.shape, F32)

    @pl.when((i >= 1) & (i <= n_tiles))
    def _():
        xn = xn_ref[(i - 1) % 2]
        rp = tm // FF_SIDE_CHUNKS
        for j in range(D_FF // FF_TILE):
            lo = j * FF_TILE
            gate = jnp.dot(xn, win_ref[:, lo:lo + FF_TILE], preferred_element_type=F32)
            up = jnp.dot(xn, win_ref[:, D_FF + lo:D_FF + lo + FF_TILE], preferred_element_type=F32)
            hm = gate * jax.nn.sigmoid(gate) * up
            h_ref[:, lo:lo + FF_TILE] = hm.astype(BF16)
            if j < FF_SIDE_CHUNKS:
                rs = slice(j * rp, (j + 1) * rp)
                out_rows = xc_ref[rs, :] + _rms(y_ref[rs, :], g_half)
                o_ref[rs, :] = out_rows
                xn_rows = _rms(xa_ref[rs, :], gpre_ref[...])
                xn_ref[i % 2, rs, :] = xn_rows.astype(BF16)
                zero = _zero_after(out_rows) + _zero_after(xn_rows)
                h_ref[0:8, lo:lo + 128] = (hm[0:8, 0:128] + zero).astype(BF16)
            else:
                if j == FF_SIDE_CHUNKS:
                    side = _convert_later_weights(i, w2in_blk, w2out_blk, wmo_blk, wt_blk, wtg_blk,
                                                  w2in_bf, w2out_bf, wmo_bf, wb_blk)
                    side = [side[0], side[1] + side[2], side[3]]
                h_ref[0:8, lo:lo + 128] = (hm[0:8, 0:128] + side[j - FF_SIDE_CHUNKS]).astype(BF16)
        y_ref[...] = jnp.dot(h_ref[...], wout_ref[...], preferred_element_type=F32)

    @pl.when(i == n_tiles + 1)
    def _():
        o_ref[...] = xc_ref[...] + _rms(y_ref[...], g_half)


def _mix_ffn_kernel(tiles_per_seq, x_ref, bg_ref, u_ref, uprev_ref, unext_ref, cw_ref, cb_ref,
                    y_ref, gmix_ref, wmo_ref, gpre_ref, gpost_ref, win_ref, wout_ref, o_ref, h_ref):
    x, xn = _mix_out_tile(tiles_per_seq, x_ref, bg_ref, u_ref, uprev_ref, unext_ref, cw_ref, cb_ref,
                          y_ref, wmo_ref, gmix_ref, gpre_ref)
    o_ref[...] = _ffn_tile(x, xn, gpost_ref, win_ref, wout_ref, h_ref)


def _ffn_scratch(tm):
    return [
        pltpu.VMEM((tm, D_FF), BF16),
        pltpu.VMEM((D_MODEL, 2 * D_FF), BF16),
        pltpu.VMEM((D_FF, D_MODEL), BF16),
    ], [
        pltpu.VMEM((2, D_MODEL // FF_STAGE_CHUNKS, 2 * D_FF), F32),
        pltpu.VMEM((2, D_FF // FF_STAGE_CHUNKS, D_MODEL), F32),
        pltpu.SemaphoreType.DMA((2,)),
    ]


def _ffn(x, gpre, gpost, w_in, w_out, w2_in, w2_out, w_mix_out, w_mix_in_t, layer, tm):
    T = x.shape[0]
    nt = T // tm
    assert nt > MIX_W_BLOCKS and nt % 2 == 0, f"first FFN needs more than {MIX_W_BLOCKS} token tiles, got {nt}"
    g0 = MIX_W_BLOCKS * 128
    resident, staging = _ffn_scratch(tm)
    tile = lambda lag: pl.BlockSpec((tm, D_MODEL), lambda i: (jnp.clip(i - lag, 0, nt - 1), 0))
    rows_in = lambda n, w, last: pl.BlockSpec((None, n, w), lambda i: (layer, jnp.clip(i - 1, 0, last), 0))
    rows_out = lambda n, w, last: pl.BlockSpec((n, w), lambda i: (jnp.clip(i - 1, 0, last), 0))
    r_in, r_out, r_mo = D_MODEL // nt, D_FF // (nt // 2), D_MODEL // (nt // 2)
    return pl.pallas_call(
        functools.partial(_ffn_kernel, layer, nt),
        grid=(nt + 2,),
        in_specs=[
            tile(0), tile(2),
            _resident((1, D_MODEL)),
            _resident((1, D_MODEL)),
            pl.BlockSpec(memory_space=pl.ANY),
            pl.BlockSpec(memory_space=pl.ANY),
            rows_in(r_in, 2 * D_FF, nt - 1),
            rows_in(r_out, D_MODEL, nt // 2 - 1),
            rows_in(r_mo, D_MODEL, nt // 2 - 1),
            rows_in(128, D_MODEL, MIX_W_BLOCKS - 1),
            pl.BlockSpec((None, 2 * NGR, D_MODEL), lambda i: (layer, g0 // (2 * NGR), 0)),
        ],
        out_specs=[
            tile(2),
            rows_out(r_in, 2 * D_FF, nt - 1),
            rows_out(r_out, D_MODEL, nt // 2 - 1),
            rows_out(r_mo, D_MODEL, nt // 2 - 1),
            pl.BlockSpec((D_MODEL, 128), lambda i: (0, jnp.clip(i - 1, 0, MIX_W_BLOCKS))),
        ],
        out_shape=[
            jax.ShapeDtypeStruct((T, D_MODEL), F32),
            jax.ShapeDtypeStruct((D_MODEL, 2 * D_FF), BF16),
            jax.ShapeDtypeStruct((D_FF, D_MODEL), BF16),
            jax.ShapeDtypeStruct((D_MODEL, D_MODEL), BF16),
            jax.ShapeDtypeStruct((D_MODEL, g0 + 128), BF16),
        ],
        scratch_shapes=resident + staging + [pltpu.VMEM((2, tm, D_MODEL), BF16),
                                             pltpu.VMEM((tm, D_MODEL), F32)],
        compiler_params=pltpu.CompilerParams(
            dimension_semantics=("arbitrary",), vmem_limit_bytes=VMEM_LIMIT),
        name="ffn",
    )(x, x, gpre, gpost, w_in, w_out, w2_in, w2_out, w_mix_out, w_mix_in_t, w_mix_in_t)


def _mix_ffn(x, bg, u, conv_w, conv_b, y_mlstm, gmix, w_mix_out, gpre, gpost, w_in, w_out, seq, tm):
    T = x.shape[0]
    tiles_per_seq = seq // tm
    sub = tm // HALO_ROWS
    last = T // HALO_ROWS - 1
    row = lambda w: pl.BlockSpec((tm, w), lambda i: (i, 0))
    return pl.pallas_call(
        functools.partial(_mix_ffn_kernel, tiles_per_seq),
        grid=(T // tm,),
        in_specs=[
            row(D_MODEL), row(CONV_WIDTH), row(CONV_WIDTH),
            pl.BlockSpec((HALO_ROWS, CONV_WIDTH), lambda i: (jnp.maximum(i * sub - 1, 0), 0)),
            pl.BlockSpec((HALO_ROWS, CONV_WIDTH), lambda i: (jnp.minimum((i + 1) * sub, last), 0)),
            _resident((3, CONV_WIDTH)), _resident((1, CONV_WIDTH)),
            row(HEADS * DV), _resident((1, D_MODEL)), _resident(w_mix_out.shape),
            _resident((1, D_MODEL)), _resident((1, D_MODEL)), _resident(w_in.shape), _resident(w_out.shape),
        ],
        out_specs=row(D_MODEL),
        out_shape=jax.ShapeDtypeStruct((T, D_MODEL), F32),
        scratch_shapes=[pltpu.VMEM((tm, D_FF), BF16)],
        compiler_params=pltpu.CompilerParams(
            dimension_semantics=("arbitrary",), vmem_limit_bytes=VMEM_LIMIT),
        name="mix_ffn",
    )(x, bg, u, u, u, conv_w, conv_b, y_mlstm, gmix, w_mix_out, gpre, gpost, w_in, w_out)


def _gate_rows(zr, p_ref, ccol_ref):
    L = CHUNK
    n_chunks = zr.shape[1] // L
    fwd_row = lax.broadcasted_iota(jnp.int32, (NGR, L), 0) < HEADS
    fwd_col = lax.broadcasted_iota(jnp.int32, (NGR, 1), 0) < HEADS
    li = zr[0:NGR]
    gates = jnp.concatenate([li, _log_sigmoid(zr[NGR:2 * NGR])], axis=0)
    x3 = jnp.concatenate(_split3(gates), axis=0)
    stacked = jnp.concatenate([x3[:, c * L:(c + 1) * L] for c in range(n_chunks)], axis=0)
    cum_ops = jnp.concatenate([_visible(False), _visible(True)], axis=1).astype(BF16)
    cum = jnp.dot(stacked, cum_ops, preferred_element_type=F32)
    b_chunks = []
    for c in range(n_chunks):
        blk = cum[c * 48:(c + 1) * 48]
        s16 = blk[0:16] + blk[16:32] + blk[32:48]
        b_chunks.append(jnp.where(fwd_row, s16[NGR:2 * NGR, 0:L], s16[NGR:2 * NGR, L:2 * L]))
    b = jnp.concatenate(b_chunks, axis=1)
    cc = li - b
    fwd_all = lax.broadcasted_iota(jnp.int32, cc.shape, 0) < HEADS
    p_ref[P_M, :] = jnp.where(fwd_all, _running_max(cc, False), _running_max(cc, True))
    p_ref[P_B, :] = b
    pad = jnp.zeros((L - NGR, L), F32)
    for c in range(n_chunks):
        sl = slice(c * L, (c + 1) * L)
        b_c, cc_c = b_chunks[c], cc[:, sl]
        g = jnp.where(fwd_col, b_c[:, L - 1:L], b_c[:, 0:1])
        m_chunk = g + jnp.max(cc_c, axis=1, keepdims=True)
        p_ref[P_E, sl] = jnp.exp(g + cc_c - m_chunk)
        p_ref[P_G, sl] = jnp.broadcast_to(g, (NGR, L))
        p_ref[P_MC, sl] = jnp.broadcast_to(m_chunk, (NGR, L))
        ccol_ref[sl, :] = jnp.concatenate([cc_c, pad], axis=0).T


def _mix_in_kernel(x_ref, gpre_ref, wb_ref, gbias_ref,
                   bg_ref, u_ref, qt_ref, k_ref, ksw_ref, vt_ref, o_ref, p_ref, ccol_ref):
    W, HK, HV = CONV_WIDTH, HEADS * DK, HEADS * DV
    q0 = 3 * W
    k0, v0 = q0 + HK, q0 + 2 * HK
    o0 = v0 + HV
    g0 = o0 + HV

    xn = _rms(x_ref[...], gpre_ref[...]).astype(BF16)
    proj = lambda a, b: jnp.dot(xn, wb_ref[:, a:b], preferred_element_type=F32)

    zg = proj(g0, g0 + 128) + gbias_ref[...]
    bg_ref[...] = proj(0, W)
    u_ref[...] = proj(W, 2 * W) * proj(2 * W, 3 * W)
    _gate_rows(zg.T[0:2 * NGR], p_ref, ccol_ref)
    qt_ref[...] = (proj(q0, k0) * (DK ** -0.5)).T.astype(BF16)
    kk = proj(k0, v0)
    k_ref[...] = kk.astype(BF16)
    for pair in range(HEADS // 2):
        ps = slice(pair * 2 * DK, (pair + 1) * 2 * DK)
        ksw_ref[:, ps] = pltpu.roll(kk[:, ps], DK, 1).astype(BF16)
    vt_ref[...] = proj(v0, o0).T.astype(BF16)
    o_ref[...] = proj(o0, g0)


def _mix_in(x, gpre, wb, gbias, tm):
    T = x.shape[0]
    row = lambda w: pl.BlockSpec((tm, w), lambda i: (i, 0))
    col = lambda h: pl.BlockSpec((h, tm), lambda i: (0, i))
    HK, HV = HEADS * DK, HEADS * DV
    return pl.pallas_call(
        _mix_in_kernel,
        grid=(T // tm,),
        in_specs=[row(D_MODEL), _resident((1, D_MODEL)), _resident(wb.shape), _resident((1, 128))],
        out_specs=[row(CONV_WIDTH), row(CONV_WIDTH), col(HK), row(HK), row(HK), col(HV), row(HV),
                   col(5 * NGR), row(128)],
        out_shape=[
            jax.ShapeDtypeStruct((T, CONV_WIDTH), F32),
            jax.ShapeDtypeStruct((T, CONV_WIDTH), F32),
            jax.ShapeDtypeStruct((HK, T), BF16),
            jax.ShapeDtypeStruct((T, HK), BF16),
            jax.ShapeDtypeStruct((T, HK), BF16),
            jax.ShapeDtypeStruct((HV, T), BF16),
            jax.ShapeDtypeStruct((T, HV), F32),
            jax.ShapeDtypeStruct((5 * NGR, T), F32),
            jax.ShapeDtypeStruct((T, 128), F32),
        ],
        compiler_params=pltpu.CompilerParams(
            dimension_semantics=("arbitrary",), vmem_limit_bytes=VMEM_LIMIT),
        name="mix_in",
    )(x, gpre, wb, gbias)


def _state_half(d, h):
    return (h % 2) ^ d


def _scan_direction(d, vt_ref, k_ref, p_ref, ct_ref, mp_ref, ct_state, m_state):
    L = CHUNK
    n_chunks = k_ref.shape[0] // L
    lane_half = lax.broadcasted_iota(jnp.int32, (L, 128), 1) // DK
    lane_half_s = lax.broadcasted_iota(jnp.int32, (DVA, 128), 1) // DK
    ones_rows = jnp.ones((DVA - DV, L), BF16)

    cts = [ct_state[d, h] for h in range(HEADS)]
    m_prev = m_state[d]
    for chunk in (range(n_chunks - 1, -1, -1) if d == 1 else range(n_chunks)):
        sl = slice(chunk * L, (chunk + 1) * L)
        e, g, m_chunk = p_ref[P_E, sl], p_ref[P_G, sl], p_ref[P_MC, sl]
        mp_ref[:, sl] = m_prev
        m_new = jnp.maximum(g + m_prev, m_chunk)
        a_old = jnp.exp(g + m_prev - m_new)
        a_new = jnp.exp(m_chunk - m_new)
        new_cts = []
        for h in range(HEADS):
            r = HEADS * d + h
            vt_aug = jnp.concatenate([vt_ref[h * DV:(h + 1) * DV, sl], ones_rows], axis=0)
            vte = (vt_aug.astype(F32) * e[r:r + 1, :]).astype(BF16)
            pair = slice((h // 2) * 2 * DK, (h // 2 + 1) * 2 * DK)
            k_half = jnp.where(lane_half == _state_half(d, h), k_ref[sl, pair], jnp.zeros((L, 128), BF16))
            ct_chunk = jnp.dot(vte, k_half, preferred_element_type=F32)
            new_cts.append(a_old[r:r + 1, 0:1] * cts[h] + a_new[r:r + 1, 0:1] * ct_chunk)
        for pr in range(HEADS // 2):
            both = jnp.where(lane_half_s == _state_half(d, 2 * pr), cts[2 * pr], cts[2 * pr + 1])
            ct_ref[chunk, pr * DVA:(pr + 1) * DVA, :] = both.astype(BF16)
        cts = new_cts
        m_prev = m_new
    for h in range(HEADS):
        ct_state[d, h] = cts[h]
    m_state[d] = m_prev


def _mlstm_scan_kernel(vt_f, k_f, p_f, vt_b, ksw_b, p_b, ct_f, mp_f, ct_b, mp_b, ct_state, m_state):
    @pl.when(pl.program_id(1) == 0)
    def _():
        ct_state[...] = jnp.zeros(ct_state.shape, F32)
        m_state[...] = jnp.full(m_state.shape, NEG_INF, F32)

    _scan_direction(0, vt_f, k_f, p_f, ct_f, mp_f, ct_state, m_state)
    _scan_direction(1, vt_b, ksw_b, p_b, ct_b, mp_b, ct_state, m_state)


def _mlstm_out_kernel(qt_ref, k_ref, vt_ref, p_ref, mp_f, mp_b, ccol_ref, ct_f, ct_b,
                      o_ref, gain_ref, y_ref):
    L = CHUNK
    n_chunks = k_ref.shape[0] // L
    visible = (_visible(False), _visible(True))
    fwd_row = lax.broadcasted_iota(jnp.int32, (NGR, L), 0) < HEADS
    lane_half = lax.broadcasted_iota(jnp.int32, (DVA, 128), 1) // DK
    ones_rows = jnp.ones((DVA - DV, L), BF16)
    zq = jnp.zeros((DK, L), BF16)
    for chunk in range(n_chunks):
        sl = slice(chunk * L, (chunk + 1) * L)
        m_prev = jnp.where(fwd_row, mp_f[:, sl], mp_b[:, sl])
        n_t = jnp.maximum(m_prev, p_ref[P_M, sl])
        f_inter = jnp.exp(m_prev - n_t)
        e_min = jnp.exp(-(p_ref[P_B, sl] + n_t))
        ccol = ccol_ref[sl, :]
        for h in range(HEADS):
            hs = slice(h * DV, (h + 1) * DV)
            qt = qt_ref[h * DK:(h + 1) * DK, sl]
            k_pair = k_ref[sl, (h // 2) * 2 * DK:(h // 2 + 1) * 2 * DK]
            qt_pair = jnp.concatenate([qt, zq] if h % 2 == 0 else [zq, qt], axis=0)
            st = jnp.dot(k_pair, qt_pair, preferred_element_type=F32)
            pts, qfs = [], []
            for d in range(2):
                r = HEADS * d + h
                arg = jnp.where(visible[d], ccol[:, r:r + 1] - n_t[r:r + 1, :], NEG_INF)
                pts.append((jnp.exp(arg) * st).astype(BF16))
                qfs.append((qt.astype(F32) * f_inter[r:r + 1, :]).astype(BF16))
            inter = [jnp.concatenate([qfs[0], zq], axis=1), jnp.concatenate([zq, qfs[1]], axis=1)]
            first = 0 if _state_half(0, h) == 0 else 1
            rhs = jnp.concatenate([jnp.concatenate(pts, axis=1), inter[first], inter[1 - first]], axis=0)
            vt_aug = jnp.concatenate([vt_ref[hs, sl], ones_rows], axis=0)
            ps = slice((h // 2) * DVA, (h // 2 + 1) * DVA)
            blocks = (ct_f[chunk, ps, :], ct_b[chunk, ps, :])
            ct = jnp.where(lane_half == 0, blocks[first], blocks[1 - first])
            both = jnp.dot(jnp.concatenate([vt_aug, ct], axis=1), rhs, preferred_element_type=F32)
            ht = None
            for d in range(2):
                r = HEADS * d + h
                numer = both[0:DV, d * L:(d + 1) * L]
                denom = both[DV:DV + 1, d * L:(d + 1) * L]
                part = numer / jnp.maximum(jnp.abs(denom), e_min[r:r + 1, :])
                ht = part if ht is None else ht + part
            ms = jnp.mean(ht * ht, axis=0, keepdims=True)
            hn = ht * lax.rsqrt(ms + EPS) * gain_ref[hs, :]
            y_ref[sl, hs] = (jax.nn.sigmoid(o_ref[sl, hs]) * hn.T).astype(BF16)


def _mlstm(qt, k, ksw, vt, p, ccol, o, gain_b, batch, seq, scan_rows, rows):
    T = batch * seq
    ng = seq // scan_rows
    HK, HV = HEADS * DK, HEADS * DV

    def scan_specs(group_of):
        blk = lambda b, j: b * ng + group_of(j)
        ins = [pl.BlockSpec((HV, scan_rows), lambda b, j: (0, blk(b, j))),
               pl.BlockSpec((scan_rows, HK), lambda b, j: (blk(b, j), 0)),
               pl.BlockSpec((5 * NGR, scan_rows), lambda b, j: (0, blk(b, j)))]
        outs = [pl.BlockSpec((scan_rows // CHUNK, CT_ROWS, 128), lambda b, j: (blk(b, j), 0, 0)),
                pl.BlockSpec((NGR, scan_rows), lambda b, j: (0, blk(b, j)))]
        return ins, outs

    ins_f, outs_f = scan_specs(lambda j: j)
    ins_b, outs_b = scan_specs(lambda j: ng - 1 - j)
    state_shapes = [jax.ShapeDtypeStruct((T // CHUNK, CT_ROWS, 128), BF16),
                    jax.ShapeDtypeStruct((NGR, T), F32)]
    if ng == 1:
        scan_kernel = lambda vt_r, k_r, ksw_r, p_r, *rest: _mlstm_scan_kernel(vt_r, k_r, p_r, vt_r, ksw_r, p_r, *rest)
        scan_in_specs, scan_args = [ins_f[0], ins_f[1], ins_f[1], ins_f[2]], (vt, k, ksw, p)
    else:
        scan_kernel, scan_in_specs, scan_args = _mlstm_scan_kernel, ins_f + ins_b, (vt, k, p, vt, ksw, p)
    ct_f, mp_f, ct_b, mp_b = pl.pallas_call(
        scan_kernel,
        grid=(batch, ng),
        in_specs=scan_in_specs,
        out_specs=outs_f + outs_b,
        out_shape=state_shapes + state_shapes,
        scratch_shapes=[pltpu.VMEM((2, HEADS, DVA, 128), F32), pltpu.VMEM((2, NGR, 128), F32)],
        compiler_params=pltpu.CompilerParams(dimension_semantics=("arbitrary", "arbitrary"),
                                             vmem_limit_bytes=VMEM_LIMIT),
        name="mlstm_scan",
    )(*scan_args)

    row = lambda w: pl.BlockSpec((rows, w), lambda i: (i, 0))
    col = lambda h: pl.BlockSpec((h, rows), lambda i: (0, i))
    ctb = pl.BlockSpec((rows // CHUNK, CT_ROWS, 128), lambda i: (i, 0, 0))
    return pl.pallas_call(
        _mlstm_out_kernel,
        grid=(T // rows,),
        in_specs=[col(HK), row(HK), col(HV), col(5 * NGR), col(NGR), col(NGR), row(128), ctb, ctb,
                  row(HV), _resident((HV, 128))],
        out_specs=row(HV),
        out_shape=jax.ShapeDtypeStruct((T, HV), BF16),
        compiler_params=pltpu.CompilerParams(dimension_semantics=("arbitrary",),
                                             vmem_limit_bytes=VMEM_LIMIT),
        name="mlstm_out",
    )(qt, k, vt, p, mp_f, mp_b, ccol, ct_f, ct_b, o, gain_b)


def kernel(x, norm_ffn1_pre, norm_ffn1_post, w_ffn1_in, w_ffn1_out, norm_mix_pre, norm_mix_post,
           w_mix_in, conv_w, conv_b, gate_i_bias, gate_f_bias, mlstm_norm, w_mix_out,
           norm_ffn2_pre, norm_ffn2_post, w_ffn2_in, w_ffn2_out):
    batch, seq, _ = x.shape
    T = batch * seq
    depth = norm_ffn1_pre.shape[0]
    tm = 512
    xt = x.reshape(T, D_MODEL)
    HV = HEADS * DV
    for l in range(depth):
        xt, w2_in, w2_out, w_mo, w_mi = _ffn(
            xt, norm_ffn1_pre[l][None], norm_ffn1_post[l][None], w_ffn1_in, w_ffn1_out,
            w_ffn2_in, w_ffn2_out, w_mix_out, jnp.swapaxes(w_mix_in, 1, 2), l, tm)

        gbias = jnp.pad(jnp.concatenate([gate_i_bias[l], gate_f_bias[l]]), (0, 128 - 2 * NGR))[None]
        bg, u, qt, k, ksw, vt, o, p, ccol = _mix_in(xt, norm_mix_pre[l][None], w_mi, gbias, 2 * tm)
        gain_b = jnp.broadcast_to(mlstm_norm[l][:, None], (HV, 128))
        y_mlstm = _mlstm(qt, k, ksw, vt, p, ccol, o, gain_b, batch, seq, SCAN_ROWS, MLSTM_ROWS)
        xt = _mix_ffn(xt, bg, u, conv_w[l], conv_b[l][None], y_mlstm, norm_mix_post[l][None], w_mo,
                      norm_ffn2_pre[l][None], norm_ffn2_post[l][None], w2_in, w2_out, seq, tm)
    return xt.reshape(batch, seq, D_MODEL)
```
